```python
import jax, jax.numpy as jnp
from jax import lax
import numpy as np

D_MODEL = 2048
BATCH = 8
SEQ = 2048
DEPTH = 4

N_MIXERS = 3
GRID_W = 64
HEAD_DIM = 128
EPS = 1e-6
NEG_INF = -1e30
NA_HEADS = D_MODEL // HEAD_DIM
NA_WIN_R = 8
NA_WIN_C = 16
SC_WIDTH = 3
GQA_Q_HEADS = D_MODEL // HEAD_DIM
GQA_KV_HEADS = GQA_Q_HEADS // 4
GQA_GROUP = GQA_Q_HEADS // GQA_KV_HEADS
Q_BLOCK = 128
ROPE_THETA = 10000.0
D_FF = ((8 * D_MODEL // 3 + 255) // 256) * 256
FFN_CONV_WIDTH = 3

kernel_name = "hybrid_natten_shortconv_gqa_convffn_encoder"


def rms_norm(x, g):
    xf = x.astype(jnp.float32)
    y = xf * lax.rsqrt(jnp.mean(xf * xf, axis=-1, keepdims=True) + EPS)
    return (y * g.astype(jnp.float32)).astype(x.dtype)


def dwconv_centered(x, w):
    k = w.shape[0]
    pad = k // 2
    s = x.shape[1]
    xp = jnp.pad(x, ((0, 0), (pad, k - 1 - pad), (0, 0)))
    out = xp[:, 0:s] * w[0]
    for i in range(1, k):
        out = out + xp[:, i:i + s] * w[i]
    return out


def neighborhood_attention(x, w_qkv, rpb, w_o):
    b, s, _ = x.shape
    rows = s // GRID_W
    wr = min(NA_WIN_R, rows)
    q, k, v = jnp.split(x @ w_qkv, 3, axis=-1)
    grid = lambda t: t.reshape(b, rows, GRID_W, NA_HEADS, HEAD_DIM)
    q = grid(q) * (HEAD_DIM ** -0.5)
    k, v = grid(k), grid(v)
    cols = jnp.arange(GRID_W)
    c_start = jnp.clip(cols - NA_WIN_C // 2, 0, GRID_W - NA_WIN_C)
    col_valid = (cols[None, :] >= c_start[:, None]) & (cols[None, :] < c_start[:, None] + NA_WIN_C)
    dc = jnp.clip(cols[None, :] - cols[:, None] + NA_WIN_C - 1, 0, 2 * NA_WIN_C - 2)
    rpb_cols = rpb[:, :, dc]

    def row_block(r):
        r_start = jnp.clip(r - NA_WIN_R // 2, 0, rows - wr)
        k_r = lax.dynamic_slice_in_dim(k, r_start, wr, axis=1)
        v_r = lax.dynamic_slice_in_dim(v, r_start, wr, axis=1)
        q_r = lax.dynamic_index_in_dim(q, r, axis=1, keepdims=False)
        dr = r_start + jnp.arange(wr) - r + NA_WIN_R - 1
        bias = jnp.transpose(rpb_cols[:, dr], (0, 2, 1, 3))
        sc = jnp.einsum('bqhd,bikhd->bhqik', q_r, k_r, preferred_element_type=jnp.float32)
        sc = sc + bias[None].astype(jnp.float32)
        sc = jnp.where(col_valid[:, None, :], sc, NEG_INF)
        p = jax.nn.softmax(sc.reshape(b, NA_HEADS, GRID_W, wr * GRID_W), axis=-1)
        p = p.reshape(b, NA_HEADS, GRID_W, wr, GRID_W).astype(v.dtype)
        return jnp.einsum('bhqik,bikhd->bqhd', p, v_r)

    out = lax.map(row_block, jnp.arange(rows))
    out = jnp.transpose(out, (1, 0, 2, 3, 4)).reshape(b, s, NA_HEADS * HEAD_DIM)
    return out @ w_o


def short_conv_mixer(x, w_in, conv_w, w_out):
    gb, gc, h = jnp.split(x @ w_in, 3, axis=-1)
    return (gb * dwconv_centered(gc * h, conv_w)) @ w_out


def axial_rope_tables(s):
    t = jnp.arange(s)
    row = (t // GRID_W).astype(jnp.float32)[:, None]
    col = (t % GRID_W).astype(jnp.float32)[:, None]
    half = HEAD_DIM // 2
    inv = ROPE_THETA ** (-jnp.arange(0, half, 2, dtype=jnp.float32) / half)
    ang = jnp.concatenate([row * inv, row * inv, col * inv, col * inv], axis=-1)
    return jnp.cos(ang), jnp.sin(ang)


def rotate_half_axial(x):
    lead = x.shape[:-1]
    xs = x.reshape(*lead, 2, 2, HEAD_DIM // 4)
    x1, x2 = xs[..., 0, :], xs[..., 1, :]
    return jnp.stack([-x2, x1], axis=-2).reshape(*lead, HEAD_DIM)


def apply_axial_rope(x, cos, sin):
    xf = x.astype(jnp.float32)
    return (xf * cos[:, None, :] + rotate_half_axial(xf) * sin[:, None, :]).astype(x.dtype)


def gqa_axial_attention(x, w_qkv, q_norm, k_norm, w_o):
    b, s, _ = x.shape
    qkv = x @ w_qkv
    nq = GQA_Q_HEADS * HEAD_DIM
    nk = GQA_KV_HEADS * HEAD_DIM
    q = qkv[..., :nq].reshape(b, s, GQA_Q_HEADS, HEAD_DIM)
    k = qkv[..., nq:nq + nk].reshape(b, s, GQA_KV_HEADS, HEAD_DIM)
    v = qkv[..., nq + nk:].reshape(b, s, GQA_KV_HEADS, HEAD_DIM)
    cos, sin = axial_rope_tables(s)
    q = apply_axial_rope(rms_norm(q, q_norm), cos, sin) * (HEAD_DIM ** -0.5)
    k = apply_axial_rope(rms_norm(k, k_norm), cos, sin)
    qb = q.reshape(b, s // Q_BLOCK, Q_BLOCK, GQA_KV_HEADS, GQA_GROUP, HEAD_DIM)

    def block(q_blk):
        sc = jnp.einsum('bqkgd,bskd->bkgqs', q_blk, k, preferred_element_type=jnp.float32)
        p = jax.nn.softmax(sc, axis=-1).astype(v.dtype)
        return jnp.einsum('bkgqs,bskd->bqkgd', p, v)

    out = lax.map(block, jnp.moveaxis(qb, 1, 0))
    out = jnp.moveaxis(out, 0, 1).reshape(b, s, nq)
    return out @ w_o


def conv_glu_ffn(x, w_up, conv_w, conv_b, w_down):
    h = dwconv_centered(x @ w_up, conv_w) + conv_b
    g, u = jnp.split(h, 2, axis=-1)
    return (jax.nn.silu(g) * u) @ w_down


def _fwd_setup_inputs(seed: int = 0) -> dict:
    key = jax.random.key(seed)
    ks = iter(jax.random.split(key, 32))
    n_a, n_b, n_c = (len(range(m, DEPTH, N_MIXERS)) for m in range(N_MIXERS))

    def w(shape, fan_in):
        return jax.random.normal(next(ks), shape, jnp.float32) * (fan_in ** -0.5)

    def gain(shape):
        return 1.0 + 0.02 * jax.random.normal(next(ks), shape, jnp.float32)

    gqa_cols = (GQA_Q_HEADS + 2 * GQA_KV_HEADS) * HEAD_DIM
    return {
        "x": jax.random.normal(next(ks), (BATCH, SEQ, D_MODEL), jnp.float32),
        "mix_norm": gain((DEPTH, D_MODEL)),
        "ffn_norm": gain((DEPTH, D_MODEL)),
        "final_norm": gain((D_MODEL,)),
        "na_w_qkv": w((n_a, D_MODEL, 3 * NA_HEADS * HEAD_DIM), D_MODEL),
        "na_rpb": 0.1 * jax.random.normal(next(ks), (n_a, NA_HEADS, 2 * NA_WIN_R - 1, 2 * NA_WIN_C - 1), jnp.float32),
        "na_w_o": w((n_a, NA_HEADS * HEAD_DIM, D_MODEL), NA_HEADS * HEAD_DIM),
        "sc_w_in": w((n_b, D_MODEL, 3 * D_MODEL), D_MODEL),
        "sc_conv_w": w((n_b, SC_WIDTH, D_MODEL), SC_WIDTH),
        "sc_w_out": w((n_b, D_MODEL, D_MODEL), D_MODEL),
        "gqa_w_qkv": w((n_c, D_MODEL, gqa_cols), D_MODEL),
        "gqa_q_norm": gain((n_c, HEAD_DIM)),
        "gqa_k_norm": gain((n_c, HEAD_DIM)),
        "gqa_w_o": w((n_c, GQA_Q_HEADS * HEAD_DIM, D_MODEL), GQA_Q_HEADS * HEAD_DIM),
        "ffn_w_up": w((DEPTH, D_MODEL, 2 * D_FF), D_MODEL),
        "ffn_conv_w": w((DEPTH, FFN_CONV_WIDTH, 2 * D_FF), FFN_CONV_WIDTH),
        "ffn_conv_b": 0.01 * jax.random.normal(next(ks), (DEPTH, 2 * D_FF), jnp.float32),
        "ffn_w_down": w((DEPTH, D_FF, D_MODEL), D_FF),
    }


def _fwd_reference(x, mix_norm, ffn_norm, final_norm, na_w_qkv, na_rpb, na_w_o,
              sc_w_in, sc_conv_w, sc_w_out, gqa_w_qkv, gqa_q_norm, gqa_k_norm,
              gqa_w_o, ffn_w_up, ffn_conv_w, ffn_conv_b, ffn_w_down):
    h = x
    for i in range(DEPTH):
        m, j = i % N_MIXERS, i // N_MIXERS
        a = rms_norm(h, mix_norm[i])
        if m == 0:
            mixed = neighborhood_attention(a, na_w_qkv[j], na_rpb[j], na_w_o[j])
        elif m == 1:
            mixed = short_conv_mixer(a, sc_w_in[j], sc_conv_w[j], sc_w_out[j])
        else:
            mixed = gqa_axial_attention(a, gqa_w_qkv[j], gqa_q_norm[j], gqa_k_norm[j], gqa_w_o[j])
        h = h + mixed
        h = h + conv_glu_ffn(rms_norm(h, ffn_norm[i]), ffn_w_up[i], ffn_conv_w[i],
                             ffn_conv_b[i], ffn_w_down[i])
    return rms_norm(h, final_norm)


import jax as _jax
import jax.numpy as _jnp

TWIN_FORMAT = 'train_step'
FWD_PARAMS = ['x', 'mix_norm', 'ffn_norm', 'final_norm', 'na_w_qkv', 'na_rpb', 'na_w_o', 'sc_w_in', 'sc_conv_w', 'sc_w_out', 'gqa_w_qkv', 'gqa_q_norm', 'gqa_k_norm', 'gqa_w_o', 'ffn_w_up', 'ffn_conv_w', 'ffn_conv_b', 'ffn_w_down']
TWIN_WEIGHTS = ['mix_norm', 'ffn_norm', 'final_norm', 'na_w_qkv', 'na_rpb', 'na_w_o', 'sc_w_in', 'sc_conv_w', 'sc_w_out', 'gqa_w_qkv', 'gqa_q_norm', 'gqa_k_norm', 'gqa_w_o', 'ffn_w_up', 'ffn_conv_w', 'ffn_conv_b', 'ffn_w_down']
TWIN_DIFF_INPUT = 'x'
TWIN_INPUTS = ['x', 'mix_norm', 'ffn_norm', 'final_norm', 'na_w_qkv', 'na_rpb', 'na_w_o', 'sc_w_in', 'sc_conv_w', 'sc_w_out', 'gqa_w_qkv', 'gqa_q_norm', 'gqa_k_norm', 'gqa_w_o', 'ffn_w_up', 'ffn_conv_w', 'ffn_conv_b', 'ffn_w_down', 'loss_target', 'm_mix_norm', 'm_ffn_norm', 'm_final_norm', 'm_na_w_qkv', 'm_na_rpb', 'm_na_w_o', 'm_sc_w_in', 'm_sc_conv_w', 'm_sc_w_out', 'm_gqa_w_qkv', 'm_gqa_q_norm', 'm_gqa_k_norm', 'm_gqa_w_o', 'm_ffn_w_up', 'm_ffn_conv_w', 'm_ffn_conv_b', 'm_ffn_w_down', 'v_mix_norm', 'v_ffn_norm', 'v_final_norm', 'v_na_w_qkv', 'v_na_rpb', 'v_na_w_o', 'v_sc_w_in', 'v_sc_conv_w', 'v_sc_w_out', 'v_gqa_w_qkv', 'v_gqa_q_norm', 'v_gqa_k_norm', 'v_gqa_w_o', 'v_ffn_w_up', 'v_ffn_conv_w', 'v_ffn_conv_b', 'v_ffn_w_down']
TWIN_OUTPUTS = ['loss', 'grad_x', 'grad_mix_norm', 'grad_ffn_norm', 'grad_final_norm', 'grad_na_w_qkv', 'grad_na_rpb', 'grad_na_w_o', 'grad_sc_w_in', 'grad_sc_conv_w', 'grad_sc_w_out', 'grad_gqa_w_qkv', 'grad_gqa_q_norm', 'grad_gqa_k_norm', 'grad_gqa_w_o', 'grad_ffn_w_up', 'grad_ffn_conv_w', 'grad_ffn_conv_b', 'grad_ffn_w_down', 'delta_mix_norm', 'delta_ffn_norm', 'delta_final_norm', 'delta_na_w_qkv', 'delta_na_rpb', 'delta_na_w_o', 'delta_sc_w_in', 'delta_sc_conv_w', 'delta_sc_w_out', 'delta_gqa_w_qkv', 'delta_gqa_q_norm', 'delta_gqa_k_norm', 'delta_gqa_w_o', 'delta_ffn_w_up', 'delta_ffn_conv_w', 'delta_ffn_conv_b', 'delta_ffn_w_down', 'new_m_mix_norm', 'new_m_ffn_norm', 'new_m_final_norm', 'new_m_na_w_qkv', 'new_m_na_rpb', 'new_m_na_w_o', 'new_m_sc_w_in', 'new_m_sc_conv_w', 'new_m_sc_w_out', 'new_m_gqa_w_qkv', 'new_m_gqa_q_norm', 'new_m_gqa_k_norm', 'new_m_gqa_w_o', 'new_m_ffn_w_up', 'new_m_ffn_conv_w', 'new_m_ffn_conv_b', 'new_m_ffn_w_down', 'new_v_mix_norm', 'new_v_ffn_norm', 'new_v_final_norm', 'new_v_na_w_qkv', 'new_v_na_rpb', 'new_v_na_w_o', 'new_v_sc_w_in', 'new_v_sc_conv_w', 'new_v_sc_w_out', 'new_v_gqa_w_qkv', 'new_v_gqa_q_norm', 'new_v_gqa_k_norm', 'new_v_gqa_w_o', 'new_v_ffn_w_up', 'new_v_ffn_conv_w', 'new_v_ffn_conv_b', 'new_v_ffn_w_down']
TWIN_LEAF_KINDS = {'loss': 'loss', 'grad_x': 'grad_x', 'grad_mix_norm': 'grad_w', 'grad_ffn_norm': 'grad_w', 'grad_final_norm': 'grad_w', 'grad_na_w_qkv': 'grad_w', 'grad_na_rpb': 'grad_w', 'grad_na_w_o': 'grad_w', 'grad_sc_w_in': 'grad_w', 'grad_sc_conv_w': 'grad_w', 'grad_sc_w_out': 'grad_w', 'grad_gqa_w_qkv': 'grad_w', 'grad_gqa_q_norm': 'grad_w', 'grad_gqa_k_norm': 'grad_w', 'grad_gqa_w_o': 'grad_w', 'grad_ffn_w_up': 'grad_w', 'grad_ffn_conv_w': 'grad_w', 'grad_ffn_conv_b': 'grad_w', 'grad_ffn_w_down': 'grad_w', 'delta_mix_norm': 'delta_w', 'delta_ffn_norm': 'delta_w', 'delta_final_norm': 'delta_w', 'delta_na_w_qkv': 'delta_w', 'delta_na_rpb': 'delta_w', 'delta_na_w_o': 'delta_w', 'delta_sc_w_in': 'delta_w', 'delta_sc_conv_w': 'delta_w', 'delta_sc_w_out': 'delta_w', 'delta_gqa_w_qkv': 'delta_w', 'delta_gqa_q_norm': 'delta_w', 'delta_gqa_k_norm': 'delta_w', 'delta_gqa_w_o': 'delta_w', 'delta_ffn_w_up': 'delta_w', 'delta_ffn_conv_w': 'delta_w', 'delta_ffn_conv_b': 'delta_w', 'delta_ffn_w_down': 'delta_w', 'new_m_mix_norm': 'new_m', 'new_m_ffn_norm': 'new_m', 'new_m_final_norm': 'new_m', 'new_m_na_w_qkv': 'new_m', 'new_m_na_rpb': 'new_m', 'new_m_na_w_o': 'new_m', 'new_m_sc_w_in': 'new_m', 'new_m_sc_conv_w': 'new_m', 'new_m_sc_w_out': 'new_m', 'new_m_gqa_w_qkv': 'new_m', 'new_m_gqa_q_norm': 'new_m', 'new_m_gqa_k_norm': 'new_m', 'new_m_gqa_w_o': 'new_m', 'new_m_ffn_w_up': 'new_m', 'new_m_ffn_conv_w': 'new_m', 'new_m_ffn_conv_b': 'new_m', 'new_m_ffn_w_down': 'new_m', 'new_v_mix_norm': 'new_v', 'new_v_ffn_norm': 'new_v', 'new_v_final_norm': 'new_v', 'new_v_na_w_qkv': 'new_v', 'new_v_na_rpb': 'new_v', 'new_v_na_w_o': 'new_v', 'new_v_sc_w_in': 'new_v', 'new_v_sc_conv_w': 'new_v', 'new_v_sc_w_out': 'new_v', 'new_v_gqa_w_qkv': 'new_v', 'new_v_gqa_q_norm': 'new_v', 'new_v_gqa_k_norm': 'new_v', 'new_v_gqa_w_o': 'new_v', 'new_v_ffn_w_up': 'new_v', 'new_v_ffn_conv_w': 'new_v', 'new_v_ffn_conv_b': 'new_v', 'new_v_ffn_w_down': 'new_v'}


def _forward(args):
    return _fwd_reference(*[args[k] for k in FWD_PARAMS])


def _output_shape():
    out = _jax.eval_shape(lambda: _forward(_fwd_setup_inputs(0)))
    return out.shape, out.dtype

N_MICROBATCH = 1
ADAM_LR = 0.001
ADAM_B1 = 0.9
ADAM_B2 = 0.999
ADAM_EPS = 1e-08
ADAM_WD = 0.01
ADAM_STEP = 10
PER_EXAMPLE_BATCH_AXIS = {'x': 0, 'loss_target': 0}
SHARED_INPUTS = []
_WEIGHT_DTYPES = {'mix_norm': _jnp.float32, 'ffn_norm': _jnp.float32, 'final_norm': _jnp.float32, 'na_w_qkv': _jnp.float32, 'na_rpb': _jnp.float32, 'na_w_o': _jnp.float32, 'sc_w_in': _jnp.float32, 'sc_conv_w': _jnp.float32, 'sc_w_out': _jnp.float32, 'gqa_w_qkv': _jnp.float32, 'gqa_q_norm': _jnp.float32, 'gqa_k_norm': _jnp.float32, 'gqa_w_o': _jnp.float32, 'ffn_w_up': _jnp.float32, 'ffn_conv_w': _jnp.float32, 'ffn_conv_b': _jnp.float32, 'ffn_w_down': _jnp.float32}
MOMENT_SCALE = {'mix_norm': 4.590373e-02, 'ffn_norm': 4.836893e-02, 'final_norm': 8.000907e+00, 'na_w_qkv': 1.625451e-02, 'na_rpb': 8.267031e-03, 'na_w_o': 1.703384e-02, 'sc_w_in': 4.830884e-02, 'sc_conv_w': 4.931685e-02, 'sc_w_out': 4.824515e-02, 'gqa_w_qkv': 6.361183e-03, 'gqa_q_norm': 1.714319e-02, 'gqa_k_norm': 1.717535e-02, 'gqa_w_o': 4.935214e-03, 'ffn_w_up': 2.044836e-02, 'ffn_conv_w': 2.030049e-02, 'ffn_conv_b': 2.008311e-02, 'ffn_w_down': 3.335842e-02}


def _to_microbatches(a, axis):
    t = _jnp.moveaxis(a, axis, 0)
    t = t.reshape((N_MICROBATCH, t.shape[0] // N_MICROBATCH) + t.shape[1:])
    return _jnp.moveaxis(t, 1, axis + 1)


def setup_inputs(seed: int = 0) -> dict:
    inp = _fwd_setup_inputs(seed)
    key = _jax.random.fold_in(_jax.random.key(seed), 7919)
    shape, _ = _output_shape()
    out = dict(inp)
    out["loss_target"] = _jax.random.normal(_jax.random.fold_in(key, 0), shape, _jnp.float32)
    for i, name in enumerate(TWIN_WEIGHTS):
        w = inp[name].astype(_jnp.float32)
        if MOMENT_SCALE is None:
            s = _jnp.sqrt(_jnp.mean(_jnp.square(w)) + 1e-30)
        else:
            s = MOMENT_SCALE[name]
        km, kv = _jax.random.split(_jax.random.fold_in(key, i + 1))
        out[name] = w
        out["m_" + name] = s * _jax.random.normal(km, w.shape, _jnp.float32)
        out["v_" + name] = (s * s) * _jax.random.uniform(kv, w.shape, _jnp.float32, 0.5, 1.5)
    if N_MICROBATCH > 1:
        for name, axis in PER_EXAMPLE_BATCH_AXIS.items():
            out[name] = _to_microbatches(out[name], axis)
    return {'x': out['x'], 'mix_norm': out['mix_norm'], 'ffn_norm': out['ffn_norm'], 'final_norm': out['final_norm'], 'na_w_qkv': out['na_w_qkv'], 'na_rpb': out['na_rpb'], 'na_w_o': out['na_w_o'], 'sc_w_in': out['sc_w_in'], 'sc_conv_w': out['sc_conv_w'], 'sc_w_out': out['sc_w_out'], 'gqa_w_qkv': out['gqa_w_qkv'], 'gqa_q_norm': out['gqa_q_norm'], 'gqa_k_norm': out['gqa_k_norm'], 'gqa_w_o': out['gqa_w_o'], 'ffn_w_up': out['ffn_w_up'], 'ffn_conv_w': out['ffn_conv_w'], 'ffn_conv_b': out['ffn_conv_b'], 'ffn_w_down': out['ffn_w_down'], 'loss_target': out['loss_target'], 'm_mix_norm': out['m_mix_norm'], 'm_ffn_norm': out['m_ffn_norm'], 'm_final_norm': out['m_final_norm'], 'm_na_w_qkv': out['m_na_w_qkv'], 'm_na_rpb': out['m_na_rpb'], 'm_na_w_o': out['m_na_w_o'], 'm_sc_w_in': out['m_sc_w_in'], 'm_sc_conv_w': out['m_sc_conv_w'], 'm_sc_w_out': out['m_sc_w_out'], 'm_gqa_w_qkv': out['m_gqa_w_qkv'], 'm_gqa_q_norm': out['m_gqa_q_norm'], 'm_gqa_k_norm': out['m_gqa_k_norm'], 'm_gqa_w_o': out['m_gqa_w_o'], 'm_ffn_w_up': out['m_ffn_w_up'], 'm_ffn_conv_w': out['m_ffn_conv_w'], 'm_ffn_conv_b': out['m_ffn_conv_b'], 'm_ffn_w_down': out['m_ffn_w_down'], 'v_mix_norm': out['v_mix_norm'], 'v_ffn_norm': out['v_ffn_norm'], 'v_final_norm': out['v_final_norm'], 'v_na_w_qkv': out['v_na_w_qkv'], 'v_na_rpb': out['v_na_rpb'], 'v_na_w_o': out['v_na_w_o'], 'v_sc_w_in': out['v_sc_w_in'], 'v_sc_conv_w': out['v_sc_conv_w'], 'v_sc_w_out': out['v_sc_w_out'], 'v_gqa_w_qkv': out['v_gqa_w_qkv'], 'v_gqa_q_norm': out['v_gqa_q_norm'], 'v_gqa_k_norm': out['v_gqa_k_norm'], 'v_gqa_w_o': out['v_gqa_w_o'], 'v_ffn_w_up': out['v_ffn_w_up'], 'v_ffn_conv_w': out['v_ffn_conv_w'], 'v_ffn_conv_b': out['v_ffn_conv_b'], 'v_ffn_w_down': out['v_ffn_w_down']}


def _loss(weights, diff, rest, loss_target):
    with _jax.named_scope("forward"):
        args = {**rest, TWIN_DIFF_INPUT: diff, **{k: w.astype(_WEIGHT_DTYPES[k]) for k, w in weights.items()}}
        y = _forward(args)
    with _jax.named_scope("loss_head"):
        err = _jnp.square(y.astype(_jnp.float32) - loss_target)
        return 0.5 * _jnp.sum(_jnp.mean(err, axis=-1)) if err.ndim else 0.5 * err


def _adamw(w, g, m, v):
    m = ADAM_B1 * m + (1.0 - ADAM_B1) * g
    v = ADAM_B2 * v + (1.0 - ADAM_B2) * _jnp.square(g)
    m_hat = m / (1.0 - ADAM_B1 ** ADAM_STEP)
    v_hat = v / (1.0 - ADAM_B2 ** ADAM_STEP)
    delta = -ADAM_LR * (m_hat / (_jnp.sqrt(v_hat) + ADAM_EPS) + ADAM_WD * w)
    return delta, m, v


def reference(x, mix_norm, ffn_norm, final_norm, na_w_qkv, na_rpb, na_w_o, sc_w_in, sc_conv_w, sc_w_out, gqa_w_qkv, gqa_q_norm, gqa_k_norm, gqa_w_o, ffn_w_up, ffn_conv_w, ffn_conv_b, ffn_w_down, loss_target, m_mix_norm, m_ffn_norm, m_final_norm, m_na_w_qkv, m_na_rpb, m_na_w_o, m_sc_w_in, m_sc_conv_w, m_sc_w_out, m_gqa_w_qkv, m_gqa_q_norm, m_gqa_k_norm, m_gqa_w_o, m_ffn_w_up, m_ffn_conv_w, m_ffn_conv_b, m_ffn_w_down, v_mix_norm, v_ffn_norm, v_final_norm, v_na_w_qkv, v_na_rpb, v_na_w_o, v_sc_w_in, v_sc_conv_w, v_sc_w_out, v_gqa_w_qkv, v_gqa_q_norm, v_gqa_k_norm, v_gqa_w_o, v_ffn_w_up, v_ffn_conv_w, v_ffn_conv_b, v_ffn_w_down):
    given = dict(x=x, mix_norm=mix_norm, ffn_norm=ffn_norm, final_norm=final_norm, na_w_qkv=na_w_qkv, na_rpb=na_rpb, na_w_o=na_w_o, sc_w_in=sc_w_in, sc_conv_w=sc_conv_w, sc_w_out=sc_w_out, gqa_w_qkv=gqa_w_qkv, gqa_q_norm=gqa_q_norm, gqa_k_norm=gqa_k_norm, gqa_w_o=gqa_w_o, ffn_w_up=ffn_w_up, ffn_conv_w=ffn_conv_w, ffn_conv_b=ffn_conv_b, ffn_w_down=ffn_w_down, loss_target=loss_target, m_mix_norm=m_mix_norm, m_ffn_norm=m_ffn_norm, m_final_norm=m_final_norm, m_na_w_qkv=m_na_w_qkv, m_na_rpb=m_na_rpb, m_na_w_o=m_na_w_o, m_sc_w_in=m_sc_w_in, m_sc_conv_w=m_sc_conv_w, m_sc_w_out=m_sc_w_out, m_gqa_w_qkv=m_gqa_w_qkv, m_gqa_q_norm=m_gqa_q_norm, m_gqa_k_norm=m_gqa_k_norm, m_gqa_w_o=m_gqa_w_o, m_ffn_w_up=m_ffn_w_up, m_ffn_conv_w=m_ffn_conv_w, m_ffn_conv_b=m_ffn_conv_b, m_ffn_w_down=m_ffn_w_down, v_mix_norm=v_mix_norm, v_ffn_norm=v_ffn_norm, v_final_norm=v_final_norm, v_na_w_qkv=v_na_w_qkv, v_na_rpb=v_na_rpb, v_na_w_o=v_na_w_o, v_sc_w_in=v_sc_w_in, v_sc_conv_w=v_sc_conv_w, v_sc_w_out=v_sc_w_out, v_gqa_w_qkv=v_gqa_w_qkv, v_gqa_q_norm=v_gqa_q_norm, v_gqa_k_norm=v_gqa_k_norm, v_gqa_w_o=v_gqa_w_o, v_ffn_w_up=v_ffn_w_up, v_ffn_conv_w=v_ffn_conv_w, v_ffn_conv_b=v_ffn_conv_b, v_ffn_w_down=v_ffn_w_down)
    weights = {n: given[n] for n in TWIN_WEIGHTS}
    shared = {n: given[n] for n in SHARED_INPUTS}
    per_example = {n: given[n] for n in ['x']}
    grad_fn = _jax.value_and_grad(_loss, argnums=(0, 1))

    def one_microbatch(ex, loss_target):
        ex = dict(ex)
        diff = ex.pop(TWIN_DIFF_INPUT)
        return grad_fn(weights, diff, {**shared, **ex}, loss_target)

    if N_MICROBATCH == 1:
        loss, (grad_w, grad_x) = one_microbatch(per_example, given["loss_target"])
    else:
        def body(carry, xs):
            loss_sum, grad_sum = carry
            l_k, (gw_k, gx_k) = one_microbatch(xs[0], xs[1])
            with _jax.named_scope("update"):
                return (loss_sum + l_k, _jax.tree.map(_jnp.add, grad_sum, gw_k)), gx_k

        init = (_jnp.zeros((), _jnp.float32), _jax.tree.map(_jnp.zeros_like, weights))
        (loss, grad_w), grad_x = _jax.lax.scan(body, init, (per_example, given["loss_target"]))
    with _jax.named_scope("update"):
        delta_w, new_m, new_v = {}, {}, {}
        for n in TWIN_WEIGHTS:
            delta_w[n], new_m[n], new_v[n] = _adamw(weights[n], grad_w[n], given["m_" + n], given["v_" + n])
    return (loss, grad_x, *[grad_w[n] for n in TWIN_WEIGHTS], *[delta_w[n] for n in TWIN_WEIGHTS],
            *[new_m[n] for n in TWIN_WEIGHTS], *[new_v[n] for n in TWIN_WEIGHTS])
```

```python
import functools

import jax
import jax.numpy as jnp
from jax import lax
from jax.experimental import pallas as pl
from jax.experimental.pallas import tpu as pltpu

F32 = jnp.float32
BF16 = jnp.bfloat16
MESH = pl.DeviceIdType.MESH

N_CHIPS = 4
N_DEV = 8
N_MIXERS = 3
GRID_W = 64
HEAD_DIM = 128
EPS = 1e-6
NEG_INF = -1e30
NA_WIN_R = 8
NA_WIN_C = 16
GQA_GROUP = 4
ROPE_THETA = 10000.0
ADAM_LR = 0.001
ADAM_B1 = 0.9
ADAM_B2 = 0.999
ADAM_EPS = 1e-08
ADAM_WD = 0.01
ADAM_STEP = 10

LANES = 128
VMEM_LIMIT = 48 * 1024 * 1024
NT_DIMS = (((1,), (1,)), ((), ()))
TN_DIMS = (((0,), (0,)), ((), ()))


def _pick(n, cap, mult=LANES):
    best = None
    for t in range(mult, min(n, cap) + 1, mult):
        if n % t == 0:
            best = t
    return best if best is not None else n


def _params(*sem):
    return pltpu.CompilerParams(dimension_semantics=sem, vmem_limit_bytes=VMEM_LIMIT)


def _mm_nn(a, b, *, out_dtype, name, residual=None):
    m, k = a.shape
    nc, _, ncol = b.shape
    tm, tn, tk = _pick(m, 1024, 16), _pick(ncol, 1536), _pick(k, 512)
    per, nk = ncol // tn, k // tk

    def body(*refs):
        if residual is None:
            a_ref, b_ref, o_ref, acc = refs
        else:
            a_ref, b_ref, r_ref, o_ref, acc = refs
        kk = pl.program_id(2)

        @pl.when(kk == 0)
        def _():
            acc[...] = jnp.zeros_like(acc)

        acc[...] += jnp.dot(a_ref[...].astype(BF16), b_ref[...], preferred_element_type=F32)

        @pl.when(kk == nk - 1)
        def _():
            r = acc[...]
            if residual is not None:
                r = r + r_ref[...]
            o_ref[...] = r.astype(o_ref.dtype)

    in_specs = [
        pl.BlockSpec((tm, tk), lambda i, j, kk: (i, kk)),
        pl.BlockSpec((None, tk, tn), lambda i, j, kk: (j // per, kk, j % per)),
    ]
    ops = [a, b]
    if residual is not None:
        in_specs.append(pl.BlockSpec((tm, tn), lambda i, j, kk: (i, j)))
        ops.append(residual)
    return pl.pallas_call(
        body,
        name=name,
        grid=(m // tm, nc * per, nk),
        in_specs=in_specs,
        out_specs=pl.BlockSpec((tm, tn), lambda i, j, kk: (i, j)),
        out_shape=jax.ShapeDtypeStruct((m, nc * ncol), out_dtype),
        scratch_shapes=[pltpu.VMEM((tm, tn), F32)],
        compiler_params=_params("parallel", "parallel", "arbitrary"),
    )(*ops)


def _mm_nt(a, b, *, out_dtype, name):
    m, n = a.shape
    nc, k, ncol = b.shape
    tm, tko, tn = _pick(m, 1024, 16), _pick(k, 1024), _pick(ncol, 1536)
    per, nn = ncol // tn, n // tn

    def body(a_ref, b_ref, o_ref, acc):
        s = pl.program_id(2)

        @pl.when(s == 0)
        def _():
            acc[...] = jnp.zeros_like(acc)

        acc[...] += lax.dot_general(a_ref[...].astype(BF16), b_ref[...], NT_DIMS, preferred_element_type=F32)

        @pl.when(s == nn - 1)
        def _():
            o_ref[...] = acc[...].astype(o_ref.dtype)

    return pl.pallas_call(
        body,
        name=name,
        grid=(m // tm, k // tko, nn),
        in_specs=[
            pl.BlockSpec((tm, tn), lambda i, j, s: (i, s)),
            pl.BlockSpec((None, tko, tn), lambda i, j, s: (s // per, j, s % per)),
        ],
        out_specs=pl.BlockSpec((tm, tko), lambda i, j, s: (i, j)),
        out_shape=jax.ShapeDtypeStruct((m, k), out_dtype),
        scratch_shapes=[pltpu.VMEM((tm, tko), F32)],
        compiler_params=_params("parallel", "parallel", "arbitrary"),
    )(a, b)


def _mm_tn(a, g, nc, *, name):
    s, k = a.shape
    n = g.shape[1]
    ncol = n // nc
    ts, tko, tn = _pick(s, 512, 16), _pick(k, 1024), _pick(ncol, 1536)
    per, ns = ncol // tn, s // ts

    def body(a_ref, g_ref, o_ref, acc):
        t = pl.program_id(2)

        @pl.when(t == 0)
        def _():
            acc[...] = jnp.zeros_like(acc)

        acc[...] += lax.dot_general(
            a_ref[...].astype(BF16), g_ref[...].astype(BF16), TN_DIMS, preferred_element_type=F32
        )

        @pl.when(t == ns - 1)
        def _():
            o_ref[...] = acc[...].astype(o_ref.dtype)

    return pl.pallas_call(
        body,
        name=name,
        grid=(k // tko, nc * per, ns),
        in_specs=[
            pl.BlockSpec((ts, tko), lambda i, j, t: (t, i)),
            pl.BlockSpec((ts, tn), lambda i, j, t: (t, j)),
        ],
        out_specs=pl.BlockSpec((None, tko, tn), lambda i, j, t: (j // per, i, j % per)),
        out_shape=jax.ShapeDtypeStruct((nc, k, ncol), BF16),
        scratch_shapes=[pltpu.VMEM((tko, tn), F32)],
        compiler_params=_params("parallel", "parallel", "arbitrary"),
    )(a, g)


ROW_TILE = 256


def _rms_fwd(h, g, *, name):
    s, d = h.shape
    tr = _pick(s, ROW_TILE, 16)

    def body(h_ref, g_ref, o_ref):
        x = h_ref[...]
        r = lax.rsqrt(jnp.mean(x * x, axis=-1, keepdims=True) + EPS)
        o_ref[...] = (x * r * g_ref[...]).astype(o_ref.dtype)

    return pl.pallas_call(
        body,
        name=name,
        grid=(s // tr,),
        in_specs=[pl.BlockSpec((tr, d), lambda i: (i, 0)), pl.BlockSpec((1, d), lambda i: (0, 0))],
        out_specs=pl.BlockSpec((tr, d), lambda i: (i, 0)),
        out_shape=jax.ShapeDtypeStruct((s, d), BF16),
        compiler_params=_params("parallel"),
    )(h, g)


def _rms_bwd(h, g, dy, dres, *, name):
    s, d = h.shape
    tr = _pick(s, ROW_TILE, 16)

    def body(h_ref, g_ref, dy_ref, dres_ref, dh_ref, dg_ref):
        x = h_ref[...]
        r = lax.rsqrt(jnp.mean(x * x, axis=-1, keepdims=True) + EPS)
        xhat = x * r
        dyv = dy_ref[...].astype(F32)
        dyg = dyv * g_ref[...]
        dx = r * (dyg - xhat * jnp.mean(dyg * xhat, axis=-1, keepdims=True))
        dh_ref[...] = dres_ref[...] + dx
        part = jnp.sum(dyv * xhat, axis=0, keepdims=True)

        @pl.when(pl.program_id(0) == 0)
        def _():
            dg_ref[...] = part

        @pl.when(pl.program_id(0) != 0)
        def _():
            dg_ref[...] += part

    row = pl.BlockSpec((tr, d), lambda i: (i, 0))
    vec = pl.BlockSpec((1, d), lambda i: (0, 0))
    return pl.pallas_call(
        body,
        name=name,
        grid=(s // tr,),
        in_specs=[row, vec, row, row],
        out_specs=[row, vec],
        out_shape=[jax.ShapeDtypeStruct((s, d), F32), jax.ShapeDtypeStruct((1, d), F32)],
        compiler_params=_params("arbitrary"),
    )(h, g, dy, dres)


def _loss_head(h, g, target, *, name):
    s, d = h.shape
    tr = _pick(s, ROW_TILE, 16)

    def body(h_ref, g_ref, t_ref, dh_ref, dg_ref, loss_ref):
        x = h_ref[...]
        r = lax.rsqrt(jnp.mean(x * x, axis=-1, keepdims=True) + EPS)
        xhat = x * r
        gv = g_ref[...]
        err = xhat * gv - t_ref[...]
        dyv = err * (1.0 / d)
        dyg = dyv * gv
        dh_ref[...] = r * (dyg - xhat * jnp.mean(dyg * xhat, axis=-1, keepdims=True))
        part = jnp.sum(dyv * xhat, axis=0, keepdims=True)
        lpart = jnp.sum(jnp.sum(err * err, axis=-1, keepdims=True), axis=0, keepdims=True) * (0.5 / d)

        @pl.when(pl.program_id(0) == 0)
        def _():
            dg_ref[...] = part
            loss_ref[...] = jnp.broadcast_to(lpart, loss_ref.shape)

        @pl.when(pl.program_id(0) != 0)
        def _():
            dg_ref[...] += part
            loss_ref[...] += jnp.broadcast_to(lpart, loss_ref.shape)

    row = pl.BlockSpec((tr, d), lambda i: (i, 0))
    vec = pl.BlockSpec((1, d), lambda i: (0, 0))
    return pl.pallas_call(
        body,
        name=name,
        grid=(s // tr,),
        in_specs=[row, vec, row],
        out_specs=[row, vec, pl.BlockSpec((1, LANES), lambda i: (0, 0))],
        out_shape=[
            jax.ShapeDtypeStruct((s, d), F32),
            jax.ShapeDtypeStruct((1, d), F32),
            jax.ShapeDtypeStruct((1, LANES), F32),
        ],
        compiler_params=_params("arbitrary"),
    )(h, g, target)


def _shift_prev(x):
    row = lax.broadcasted_iota(jnp.int32, x.shape, 0)
    return jnp.where(row == 0, 0.0, pltpu.roll(x, 1, 0))


def _shift_next(x):
    n = x.shape[0]
    row = lax.broadcasted_iota(jnp.int32, x.shape, 0)
    return jnp.where(row == n - 1, 0.0, pltpu.roll(x, n - 1, 0))


def _conv3(x, w):
    xm, xp = _shift_prev(x), _shift_next(x)
    return xm * w[0:1] + x * w[1:2] + xp * w[2:3], xm, xp


def _conv3_t(d, w):
    return _shift_next(d) * w[0:1] + d * w[1:2] + _shift_prev(d) * w[2:3]


def _colsum(x):
    return jnp.sum(x, axis=0, keepdims=True)


def _ffn_mid_fwd(up, cw, cb, *, name):
    s, f2 = up.shape
    f = f2 // 2
    ncol = cw.shape[2]
    tc = _pick(ncol, 256)
    nt, per = f // tc, ncol // tc

    def body(ug_ref, uu_ref, wg_ref, wu_ref, bg_ref, bu_ref, o_ref):
        cg = _conv3(ug_ref[...].astype(F32), wg_ref[...])[0] + bg_ref[...]
        cu = _conv3(uu_ref[...].astype(F32), wu_ref[...])[0] + bu_ref[...]
        o_ref[...] = (cg * (1.0 / (1.0 + jnp.exp(-cg))) * cu).astype(o_ref.dtype)

    return pl.pallas_call(
        body,
        name=name,
        grid=(nt,),
        in_specs=[
            pl.BlockSpec((s, tc), lambda j: (0, j)),
            pl.BlockSpec((s, tc), lambda j: (0, nt + j)),
            pl.BlockSpec((None, 3, tc), lambda j: (j // per, 0, j % per)),
            pl.BlockSpec((None, 3, tc), lambda j: ((nt + j) // per, 0, (nt + j) % per)),
            pl.BlockSpec((1, tc), lambda j: (0, j)),
            pl.BlockSpec((1, tc), lambda j: (0, nt + j)),
        ],
        out_specs=pl.BlockSpec((s, tc), lambda j: (0, j)),
        out_shape=jax.ShapeDtypeStruct((s, f), BF16),
        compiler_params=_params("parallel"),
    )(up, up, cw, cw, cb, cb)


def _ffn_mid_bwd(up, cw, cb, dact, *, name):
    s, f2 = up.shape
    f = f2 // 2
    ncol = cw.shape[2]
    tc = _pick(ncol, 256)
    nt, per = f // tc, ncol // tc

    def body(ug_ref, uu_ref, wg_ref, wu_ref, bg_ref, bu_ref, da_ref, dug_ref, duu_ref, dwg_ref, dwu_ref, dbg_ref, dbu_ref):
        ug, uu = ug_ref[...].astype(F32), uu_ref[...].astype(F32)
        wg, wu = wg_ref[...], wu_ref[...]
        cg, ugm, ugp = _conv3(ug, wg)
        cu, uum, uup = _conv3(uu, wu)
        cg = cg + bg_ref[...]
        cu = cu + bu_ref[...]
        da = da_ref[...].astype(F32)
        sig = 1.0 / (1.0 + jnp.exp(-cg))
        dcu = da * (cg * sig)
        dcg = da * cu * (sig * (1.0 + cg * (1.0 - sig)))
        dug_ref[...] = _conv3_t(dcg, wg).astype(dug_ref.dtype)
        duu_ref[...] = _conv3_t(dcu, wu).astype(duu_ref.dtype)
        dwg_ref[0:1, :] = _colsum(dcg * ugm)
        dwg_ref[1:2, :] = _colsum(dcg * ug)
        dwg_ref[2:3, :] = _colsum(dcg * ugp)
        dwu_ref[0:1, :] = _colsum(dcu * uum)
        dwu_ref[1:2, :] = _colsum(dcu * uu)
        dwu_ref[2:3, :] = _colsum(dcu * uup)
        dbg_ref[...] = _colsum(dcg)
        dbu_ref[...] = _colsum(dcu)

    col = pl.BlockSpec((s, tc), lambda j: (0, j))
    w3 = pl.BlockSpec((3, tc), lambda j: (0, j))
    b1 = pl.BlockSpec((1, tc), lambda j: (0, j))
    return pl.pallas_call(
        body,
        name=name,
        grid=(nt,),
        in_specs=[
            col,
            pl.BlockSpec((s, tc), lambda j: (0, nt + j)),
            pl.BlockSpec((None, 3, tc), lambda j: (j // per, 0, j % per)),
            pl.BlockSpec((None, 3, tc), lambda j: ((nt + j) // per, 0, (nt + j) % per)),
            b1,
            pl.BlockSpec((1, tc), lambda j: (0, nt + j)),
            col,
        ],
        out_specs=[col, col, w3, w3, b1, b1],
        out_shape=[
            jax.ShapeDtypeStruct((s, f), BF16),
            jax.ShapeDtypeStruct((s, f), BF16),
            jax.ShapeDtypeStruct((3, f), F32),
            jax.ShapeDtypeStruct((3, f), F32),
            jax.ShapeDtypeStruct((1, f), F32),
            jax.ShapeDtypeStruct((1, f), F32),
        ],
        compiler_params=_params("parallel"),
    )(up, up, cw, cw, cb, cb, dact)


def _sc_mid_fwd(z, cw, *, name):
    s, d3 = z.shape
    d = d3 // 3
    ncol = cw.shape[2]
    tc = _pick(ncol, 256)
    nt, per = d // tc, ncol // tc

    def body(gb_ref, gc_ref, hh_ref, w_ref, o_ref):
        p = gc_ref[...].astype(F32) * hh_ref[...].astype(F32)
        o_ref[...] = (gb_ref[...].astype(F32) * _conv3(p, w_ref[...])[0]).astype(o_ref.dtype)

    return pl.pallas_call(
        body,
        name=name,
        grid=(nt,),
        in_specs=[
            pl.BlockSpec((s, tc), lambda j: (0, j)),
            pl.BlockSpec((s, tc), lambda j: (0, nt + j)),
            pl.BlockSpec((s, tc), lambda j: (0, 2 * nt + j)),
            pl.BlockSpec((None, 3, tc), lambda j: (j // per, 0, j % per)),
        ],
        out_specs=pl.BlockSpec((s, tc), lambda j: (0, j)),
        out_shape=jax.ShapeDtypeStruct((s, d), BF16),
        compiler_params=_params("parallel"),
    )(z, z, z, cw)


def _sc_mid_bwd(z, cw, dmid, *, name):
    s, d3 = z.shape
    d = d3 // 3
    ncol = cw.shape[2]
    tc = _pick(ncol, 256)
    nt, per = d // tc, ncol // tc

    def body(gb_ref, gc_ref, hh_ref, w_ref, dm_ref, dgb_ref, dgc_ref, dhh_ref, dw_ref):
        gb, gc, hh = gb_ref[...].astype(F32), gc_ref[...].astype(F32), hh_ref[...].astype(F32)
        w = w_ref[...]
        p = gc * hh
        cv, pm, pp = _conv3(p, w)
        dm = dm_ref[...].astype(F32)
        dgb_ref[...] = (dm * cv).astype(dgb_ref.dtype)
        dcv = dm * gb
        dp = _conv3_t(dcv, w)
        dgc_ref[...] = (dp * hh).astype(dgc_ref.dtype)
        dhh_ref[...] = (dp * gc).astype(dhh_ref.dtype)
        dw_ref[0:1, :] = _colsum(dcv * pm)
        dw_ref[1:2, :] = _colsum(dcv * p)
        dw_ref[2:3, :] = _colsum(dcv * pp)

    col = pl.BlockSpec((s, tc), lambda j: (0, j))
    return pl.pallas_call(
        body,
        name=name,
        grid=(nt,),
        in_specs=[
            col,
            pl.BlockSpec((s, tc), lambda j: (0, nt + j)),
            pl.BlockSpec((s, tc), lambda j: (0, 2 * nt + j)),
            pl.BlockSpec((None, 3, tc), lambda j: (j // per, 0, j % per)),
            col,
        ],
        out_specs=[col, col, col, pl.BlockSpec((3, tc), lambda j: (0, j))],
        out_shape=[jax.ShapeDtypeStruct((s, d), BF16)] * 3 + [jax.ShapeDtypeStruct((3, d), F32)],
        compiler_params=_params("parallel"),
    )(z, z, z, cw, dmid)


NA_KEYS = NA_WIN_R * GRID_W
NA_SKEW_BITS = 6


def _na_row_start(r, rows):
    return jnp.clip(r - NA_WIN_R // 2, 0, rows - NA_WIN_R)


def _na_bias_slot(r, rows):
    return _na_row_start(r, rows) - r + NA_WIN_R - 1


def _na_base(rpb):
    h = rpb.shape[0]
    pos, neg = rpb[:, :, NA_WIN_C - 1:], rpb[:, :, : NA_WIN_C - 1]
    zeros = jnp.zeros((h, NA_WIN_R, GRID_W - 2 * NA_WIN_C + 1), F32)
    out = []
    for first in range(NA_WIN_R):
        p = pos[:, first : first + NA_WIN_R]
        n = jnp.roll(neg[:, first : first + NA_WIN_R], -1, axis=1)
        out.append(jnp.concatenate([p, zeros, n], axis=-1).reshape(h, 1, NA_KEYS))
    return jnp.stack(out, axis=1)


def _skew(x, left):
    n = x.shape[1]
    row = lax.broadcasted_iota(jnp.int32, x.shape, 0)
    for b in range(NA_SKEW_BITS):
        shift = n - (1 << b) if left else (1 << b)
        x = jnp.where(((row >> b) & 1) == 1, pltpu.roll(x, shift, 1), x)
    return x


def _na_bias(base, *, name):
    h = base.shape[0]

    def body(b_ref, o_ref):
        x = _skew(jnp.broadcast_to(b_ref[...], (GRID_W, NA_KEYS)), left=False)
        q = lax.broadcasted_iota(jnp.int32, x.shape, 0)
        kc = lax.broadcasted_iota(jnp.int32, x.shape, 1) % GRID_W
        start = jnp.clip(q - NA_WIN_C // 2, 0, GRID_W - NA_WIN_C)
        o_ref[...] = jnp.where((kc >= start) & (kc < start + NA_WIN_C), x, NEG_INF)

    return pl.pallas_call(
        body,
        name=name,
        grid=(h, NA_WIN_R),
        in_specs=[pl.BlockSpec((None, None, 1, NA_KEYS), lambda i, j: (i, j, 0, 0))],
        out_specs=pl.BlockSpec((None, None, GRID_W, NA_KEYS), lambda i, j: (i, j, 0, 0)),
        out_shape=jax.ShapeDtypeStruct((h, NA_WIN_R, GRID_W, NA_KEYS), F32),
        compiler_params=_params("parallel", "parallel"),
    )(base)


def _na_specs(s, heads, rows):
    q = pl.BlockSpec((GRID_W, HEAD_DIM), lambda h, r: (r, h))
    k = pl.BlockSpec((s, HEAD_DIM), lambda h, r: (0, heads + h))
    v = pl.BlockSpec((s, HEAD_DIM), lambda h, r: (0, 2 * heads + h))
    bias = pl.BlockSpec((None, None, GRID_W, NA_KEYS), lambda h, r: (h, _na_bias_slot(r, rows), 0, 0))
    return q, k, v, bias


def _na_probs(q, k, bias):
    sc = lax.dot_general(q, k, NT_DIMS, preferred_element_type=F32) * (HEAD_DIM ** -0.5) + bias
    p = jnp.exp(sc - jnp.max(sc, axis=-1, keepdims=True))
    return p, jnp.sum(p, axis=-1, keepdims=True)


def _na_fwd(qkv, bias, *, name):
    s = qkv.shape[0]
    heads = qkv.shape[1] // (3 * HEAD_DIM)
    rows = s // GRID_W

    def body(q_ref, k_ref, v_ref, b_ref, o_ref):
        st = pl.multiple_of(_na_row_start(pl.program_id(1), rows) * GRID_W, GRID_W)
        p, l = _na_probs(q_ref[...], k_ref[pl.ds(st, NA_KEYS), :], b_ref[...])
        o = jnp.dot(p.astype(BF16), v_ref[pl.ds(st, NA_KEYS), :], preferred_element_type=F32)
        o_ref[...] = (o / l).astype(o_ref.dtype)

    q, k, v, b = _na_specs(s, heads, rows)
    return pl.pallas_call(
        body,
        name=name,
        grid=(heads, rows),
        in_specs=[q, k, v, b],
        out_specs=q,
        out_shape=jax.ShapeDtypeStruct((s, heads * HEAD_DIM), BF16),
        compiler_params=_params("parallel", "arbitrary"),
    )(qkv, qkv, qkv, bias)


def _na_bwd(qkv, bias, dout, *, name):
    s = qkv.shape[0]
    heads = qkv.shape[1] // (3 * HEAD_DIM)
    rows = s // GRID_W
    scale = HEAD_DIM ** -0.5

    def body(q_ref, k_ref, v_ref, b_ref, do_ref, dq_ref, dk_ref, dv_ref, db_ref, dk_acc, dv_acc):
        r = pl.program_id(1)

        @pl.when(r == 0)
        def _():
            dk_acc[...] = jnp.zeros_like(dk_acc)
            dv_acc[...] = jnp.zeros_like(dv_acc)

        st = pl.multiple_of(_na_row_start(r, rows) * GRID_W, GRID_W)
        win = pl.ds(st, NA_KEYS)
        q, k, v, do = q_ref[...], k_ref[win, :], v_ref[win, :], do_ref[...]
        p, l = _na_probs(q, k, b_ref[...])
        pn = p / l
        dp = lax.dot_general(do, v, NT_DIMS, preferred_element_type=F32)
        ds = pn * (dp - jnp.sum(pn * dp, axis=-1, keepdims=True))
        dsb = ds.astype(BF16)
        dq_ref[...] = (jnp.dot(dsb, k, preferred_element_type=F32) * scale).astype(dq_ref.dtype)
        dk_acc[win, :] += lax.dot_general(dsb, q, TN_DIMS, preferred_element_type=F32) * scale
        dv_acc[win, :] += lax.dot_general(pn.astype(BF16), do, TN_DIMS, preferred_element_type=F32)
        col = _colsum(_skew(ds, left=True))
        first = (r <= NA_WIN_R // 2) | (r >= rows - NA_WIN_R // 2 + 1)

        @pl.when(first)
        def _():
            db_ref[...] = col

        @pl.when(jnp.logical_not(first))
        def _():
            db_ref[...] += col

        @pl.when(r == rows - 1)
        def _():
            dk_ref[...] = dk_acc[...].astype(dk_ref.dtype)
            dv_ref[...] = dv_acc[...].astype(dv_ref.dtype)

    q, k, v, b = _na_specs(s, heads, rows)
    kv_out = pl.BlockSpec((s, HEAD_DIM), lambda h, r: (0, h))
    shape = jax.ShapeDtypeStruct((s, heads * HEAD_DIM), BF16)
    return pl.pallas_call(
        body,
        name=name,
        grid=(heads, rows),
        in_specs=[q, k, v, b, q],
        out_specs=[
            q,
            kv_out,
            kv_out,
            pl.BlockSpec((None, None, 1, NA_KEYS), lambda h, r: (h, _na_bias_slot(r, rows), 0, 0)),
        ],
        out_shape=[shape, shape, shape, jax.ShapeDtypeStruct((heads, NA_WIN_R, 1, NA_KEYS), F32)],
        scratch_shapes=[pltpu.VMEM((s, HEAD_DIM), F32), pltpu.VMEM((s, HEAD_DIM), F32)],
        compiler_params=_params("parallel", "arbitrary"),
    )(qkv, qkv, qkv, bias, dout)


RPB_ROWS = 2 * NA_WIN_R - 1
RPB_COLS = 2 * NA_WIN_C - 1
RPB_PAD = 512


def _rpb_fold_matrix():
    idx = jnp.arange(NA_WIN_R * NA_KEYS, dtype=jnp.int32)
    first, i, kc = idx // NA_KEYS, (idx // GRID_W) % NA_WIN_R, idx % GRID_W
    pos, neg = kc < NA_WIN_C, kc >= GRID_W - NA_WIN_C + 1
    dr = jnp.where(pos, first + i, first + (i + 1) % NA_WIN_R)
    dc = jnp.where(pos, kc + NA_WIN_C - 1, kc - (GRID_W - NA_WIN_C + 1))
    target = jnp.where(pos | neg, dr * RPB_COLS + dc, -1)
    return (target[:, None] == jnp.arange(RPB_PAD, dtype=jnp.int32)[None, :]).astype(F32)


def _rpb_fold(dbias, *, name):
    h = dbias.shape[0]
    flat = dbias.reshape(h, NA_WIN_R * NA_KEYS)

    def body(g_ref, m_ref, o_ref):
        o_ref[...] = jnp.dot(g_ref[...], m_ref[...], preferred_element_type=F32, precision=lax.Precision.HIGHEST)

    return pl.pallas_call(
        body,
        name=name,
        out_shape=jax.ShapeDtypeStruct((h, RPB_PAD), F32),
        compiler_params=pltpu.CompilerParams(vmem_limit_bytes=VMEM_LIMIT),
    )(flat, _rpb_fold_matrix())


GQA_Q_TILE = 256


def _rope_tables(s):
    t = jnp.arange(s)
    row = (t // GRID_W).astype(F32)[:, None]
    col = (t % GRID_W).astype(F32)[:, None]
    half = HEAD_DIM // 2
    inv = ROPE_THETA ** (-jnp.arange(0, half, 2, dtype=F32) / half)
    ang = jnp.concatenate([row * inv, row * inv, col * inv, col * inv], axis=-1)
    return jnp.cos(ang), jnp.sin(ang)


def _rot_half(y):
    quarter = HEAD_DIM // 4
    lane = lax.broadcasted_iota(jnp.int32, y.shape, 1)
    low = (lane % (2 * quarter)) < quarter
    return jnp.where(low, -pltpu.roll(y, HEAD_DIM - quarter, 1), pltpu.roll(y, quarter, 1))


def _gqa_prep_fwd(qkv, gq, gk, cos, sin, hq, hkv, *, name):
    s = qkv.shape[0]

    def body(x_ref, gq_ref, gk_ref, cos_ref, sin_ref, o_ref):
        isq = pl.program_id(0) < hq
        x = x_ref[...].astype(F32)
        g = jnp.where(isq, gq_ref[...], gk_ref[...])
        y = x * lax.rsqrt(jnp.mean(x * x, axis=-1, keepdims=True) + EPS) * g
        z = y * cos_ref[...] + _rot_half(y) * sin_ref[...]
        o_ref[...] = (z * jnp.where(isq, HEAD_DIM ** -0.5, 1.0)).astype(o_ref.dtype)

    head = pl.BlockSpec((s, HEAD_DIM), lambda h: (0, h))
    vec = pl.BlockSpec((1, HEAD_DIM), lambda h: (0, 0))
    tab = pl.BlockSpec((s, HEAD_DIM), lambda h: (0, 0))
    return pl.pallas_call(
        body,
        name=name,
        grid=(hq + hkv,),
        in_specs=[head, vec, vec, tab, tab],
        out_specs=head,
        out_shape=jax.ShapeDtypeStruct((s, (hq + hkv) * HEAD_DIM), BF16),
        compiler_params=_params("parallel"),
    )(qkv, gq, gk, cos, sin)


def _gqa_prep_bwd(qkv, gq, gk, cos, sin, dqn, dkn, hq, hkv, *, name):
    s = qkv.shape[0]

    def body(x_ref, gq_ref, gk_ref, cos_ref, sin_ref, dq_ref, dk_ref, dx_ref, dgq_ref, dgk_ref):
        hh = pl.program_id(0)
        isq = hh < hq
        x = x_ref[...].astype(F32)
        g = jnp.where(isq, gq_ref[...], gk_ref[...])
        r = lax.rsqrt(jnp.mean(x * x, axis=-1, keepdims=True) + EPS)
        xhat = x * r
        dz = jnp.where(isq, dq_ref[...].astype(F32) * (HEAD_DIM ** -0.5), dk_ref[...].astype(F32))
        dy = dz * cos_ref[...] - _rot_half(dz * sin_ref[...])
        dyg = dy * g
        dx_ref[...] = (r * (dyg - xhat * jnp.mean(dyg * xhat, axis=-1, keepdims=True))).astype(dx_ref.dtype)
        part = _colsum(dy * xhat)

        @pl.when(hh == 0)
        def _():
            dgq_ref[...] = jnp.zeros_like(dgq_ref)
            dgk_ref[...] = jnp.zeros_like(dgk_ref)

        @pl.when(isq)
        def _():
            dgq_ref[...] += part

        @pl.when(jnp.logical_not(isq))
        def _():
            dgk_ref[...] += part

    head = pl.BlockSpec((s, HEAD_DIM), lambda h: (0, h))
    vec = pl.BlockSpec((1, HEAD_DIM), lambda h: (0, 0))
    tab = pl.BlockSpec((s, HEAD_DIM), lambda h: (0, 0))
    return pl.pallas_call(
        body,
        name=name,
        grid=(hq + hkv,),
        in_specs=[
            head,
            vec,
            vec,
            tab,
            tab,
            pl.BlockSpec((s, HEAD_DIM), lambda h: (0, jnp.minimum(h, hq - 1))),
            pl.BlockSpec((s, HEAD_DIM), lambda h: (0, jnp.maximum(h - hq, 0))),
        ],
        out_specs=[head, vec, vec],
        out_shape=[
            jax.ShapeDtypeStruct((s, (hq + hkv) * HEAD_DIM), BF16),
            jax.ShapeDtypeStruct((1, HEAD_DIM), F32),
            jax.ShapeDtypeStruct((1, HEAD_DIM), F32),
        ],
        compiler_params=_params("arbitrary"),
    )(qkv, gq, gk, cos, sin, dqn, dkn)


def _gqa_fwd(qkn, qkv, hq, hkv, *, name):
    s = qkv.shape[0]
    tq = _pick(s, GQA_Q_TILE, 16)

    def body(q_ref, k_ref, v_ref, o_ref):
        sc = lax.dot_general(q_ref[...], k_ref[...], NT_DIMS, preferred_element_type=F32)
        p = jnp.exp(sc - jnp.max(sc, axis=-1, keepdims=True))
        l = jnp.sum(p, axis=-1, keepdims=True)
        o_ref[...] = (jnp.dot(p.astype(BF16), v_ref[...], preferred_element_type=F32) / l).astype(o_ref.dtype)

    q = pl.BlockSpec((tq, HEAD_DIM), lambda h, i: (i, h))
    return pl.pallas_call(
        body,
        name=name,
        grid=(hq, s // tq),
        in_specs=[
            q,
            pl.BlockSpec((s, HEAD_DIM), lambda h, i: (0, hq + h // GQA_GROUP)),
            pl.BlockSpec((s, HEAD_DIM), lambda h, i: (0, hq + hkv + h // GQA_GROUP)),
        ],
        out_specs=q,
        out_shape=jax.ShapeDtypeStruct((s, hq * HEAD_DIM), BF16),
        compiler_params=_params("parallel", "parallel"),
    )(qkn, qkn, qkv)


def _gqa_bwd(qkn, qkv, dout, hq, hkv, *, name):
    s = qkv.shape[0]
    tq = _pick(s, GQA_Q_TILE, 16)
    nq = s // tq

    def body(q_ref, k_ref, v_ref, do_ref, dq_ref, dk_ref, dv_ref, dk_acc, dv_acc):
        g, i = pl.program_id(1), pl.program_id(2)

        @pl.when((g == 0) & (i == 0))
        def _():
            dk_acc[...] = jnp.zeros_like(dk_acc)
            dv_acc[...] = jnp.zeros_like(dv_acc)

        q, k, v, do = q_ref[...], k_ref[...], v_ref[...], do_ref[...]
        sc = lax.dot_general(q, k, NT_DIMS, preferred_element_type=F32)
        p = jnp.exp(sc - jnp.max(sc, axis=-1, keepdims=True))
        pn = p / jnp.sum(p, axis=-1, keepdims=True)
        dp = lax.dot_general(do, v, NT_DIMS, preferred_element_type=F32)
        dsb = (pn * (dp - jnp.sum(pn * dp, axis=-1, keepdims=True))).astype(BF16)
        dq_ref[...] = jnp.dot(dsb, k, preferred_element_type=F32).astype(dq_ref.dtype)
        dk_acc[...] += lax.dot_general(dsb, q, TN_DIMS, preferred_element_type=F32)
        dv_acc[...] += lax.dot_general(pn.astype(BF16), do, TN_DIMS, preferred_element_type=F32)

        @pl.when((g == GQA_GROUP - 1) & (i == nq - 1))
        def _():
            dk_ref[...] = dk_acc[...].astype(dk_ref.dtype)
            dv_ref[...] = dv_acc[...].astype(dv_ref.dtype)

    q = pl.BlockSpec((tq, HEAD_DIM), lambda kv, g, i: (i, kv * GQA_GROUP + g))
    kv_out = pl.BlockSpec((s, HEAD_DIM), lambda kv, g, i: (0, kv))
    return pl.pallas_call(
        body,
        name=name,
        grid=(hkv, GQA_GROUP, nq),
        in_specs=[
            q,
            pl.BlockSpec((s, HEAD_DIM), lambda kv, g, i: (0, hq + kv)),
            pl.BlockSpec((s, HEAD_DIM), lambda kv, g, i: (0, hq + hkv + kv)),
            q,
        ],
        out_specs=[q, kv_out, kv_out],
        out_shape=[
            jax.ShapeDtypeStruct((s, hq * HEAD_DIM), BF16),
            jax.ShapeDtypeStruct((s, hkv * HEAD_DIM), BF16),
            jax.ShapeDtypeStruct((s, hkv * HEAD_DIM), BF16),
        ],
        scratch_shapes=[pltpu.VMEM((s, HEAD_DIM), F32), pltpu.VMEM((s, HEAD_DIM), F32)],
        compiler_params=_params("parallel", "arbitrary", "arbitrary"),
    )(qkn, qkn, qkv, dout)


ADAM_ROWS = 64


def _adam_update(w, g, m, v):
    m = ADAM_B1 * m + (1.0 - ADAM_B1) * g
    v = ADAM_B2 * v + (1.0 - ADAM_B2) * (g * g)
    m_hat = m / (1.0 - ADAM_B1 ** ADAM_STEP)
    v_hat = v / (1.0 - ADAM_B2 ** ADAM_STEP)
    return -ADAM_LR * (m_hat / (jnp.sqrt(v_hat) + ADAM_EPS) + ADAM_WD * w), m, v


def _adamw_sharded(w, m, v, parts, *, name):
    nl, rows, cols = w.shape
    tr = _pick(rows, ADAM_ROWS, 16)
    nr = rows // tr

    def body(*refs):
        w_ref, m_ref, v_ref = refs[:3]
        piece_refs = refs[3 : 3 + 2 * nl]
        g_ref, d_ref, nm_ref, nv_ref = refs[3 + 2 * nl :]
        layer = pl.program_id(0)
        for l in range(nl):

            @pl.when(layer == l)
            def _(l=l):
                own, sib = piece_refs[2 * l], piece_refs[2 * l + 1]
                g = own[0].astype(F32) + sib[0].astype(F32)
                for q in range(1, N_CHIPS):
                    g = g + (own[q].astype(F32) + sib[q].astype(F32))
                g_ref[...] = g
                d_ref[...], nm_ref[...], nv_ref[...] = _adam_update(w_ref[...], g, m_ref[...], v_ref[...])

    def piece_spec(l):
        return pl.BlockSpec(
            (N_CHIPS, tr, cols), lambda ll, i: (0, jnp.where(ll < l, 0, jnp.where(ll == l, i, nr - 1)), 0)
        )

    full = pl.BlockSpec((None, tr, cols), lambda ll, i: (ll, i, 0))
    flat = [p for pair in parts for p in pair]
    shape = jax.ShapeDtypeStruct(w.shape, F32)
    return pl.pallas_call(
        body,
        name=name,
        grid=(nl, nr),
        in_specs=[full] * 3 + [piece_spec(l) for l in range(nl) for _ in range(2)],
        out_specs=[full] * 4,
        out_shape=[shape] * 4,
        compiler_params=_params("arbitrary", "arbitrary"),
    )(w, m, v, *flat)


def _adamw_small(w, g, m, v, *, name):
    def body(w_ref, g_ref, m_ref, v_ref, d_ref, nm_ref, nv_ref):
        d_ref[...], nm_ref[...], nv_ref[...] = _adam_update(w_ref[...], g_ref[...], m_ref[...], v_ref[...])

    shape = jax.ShapeDtypeStruct(w.shape, F32)
    return pl.pallas_call(
        body,
        name=name,
        out_shape=[shape] * 3,
        compiler_params=pltpu.CompilerParams(vmem_limit_bytes=VMEM_LIMIT),
    )(w, g, m, v)


def _position():
    x, y, c = lax.axis_index("x"), lax.axis_index("y"), lax.axis_index("c")
    return x, y, c, 2 * x + y


def _chip_device(chip, c):
    return (chip >> 1, chip & 1, c)


ANY = pl.BlockSpec(memory_space=pl.ANY)


def _gather_shards(shards, split, *, name):
    n = len(shards)

    def body(*refs):
        ins, outs = refs[:n], refs[n : 2 * n]
        local_sem, send_sem, recv_sem, pass_send, pass_recv = refs[2 * n :]
        x, y, c, k = _position()
        sibling = (x, y, 1 - c)

        def rows(a, core):
            r = shards[a].shape[0]
            return pl.ds(core * (r // 2), r // 2) if split[a] else pl.ds(0, r)

        def over_ici(a, j, src_chip, to):
            return pltpu.make_async_remote_copy(
                src_ref=ins[a].at[rows(a, c)],
                dst_ref=outs[a].at[src_chip, rows(a, c)],
                send_sem=send_sem.at[a, j],
                recv_sem=recv_sem.at[a, j],
                device_id=to,
                device_id_type=MESH,
            )

        def to_sibling(a, j, chip, core):
            part = outs[a].at[chip, rows(a, core)]
            return pltpu.make_async_remote_copy(
                src_ref=part,
                dst_ref=part,
                send_sem=pass_send.at[a, j],
                recv_sem=pass_recv.at[a, j],
                device_id=sibling,
                device_id_type=MESH,
            )

        mine = [pltpu.make_async_copy(ins[a], outs[a].at[k], local_sem.at[a]) for a in range(n)]
        for cp in mine:
            cp.start()
        sent = []
        for j in range(N_CHIPS - 1):
            other = k ^ (j + 1)
            for a in range(n):
                cp = over_ici(a, j, k, _chip_device(other, c))
                cp.start()
                sent.append(cp)
        for j in range(N_CHIPS - 1):
            other = k ^ (j + 1)
            for a in range(n):
                over_ici(a, j, other, sibling).wait_recv()
                if split[a]:
                    cp = to_sibling(a, j, other, c)
                    cp.start()
                    sent.append(cp)
        for j in range(N_CHIPS - 1):
            other = k ^ (j + 1)
            for a in range(n):
                if split[a]:
                    to_sibling(a, j, other, 1 - c).wait_recv()
        for cp in sent:
            cp.wait_send()
        for cp in mine:
            cp.wait()

    return pl.pallas_call(
        body,
        name=name,
        in_specs=[ANY] * n,
        out_specs=[ANY] * n,
        out_shape=[jax.ShapeDtypeStruct((N_CHIPS,) + a.shape, a.dtype) for a in shards],
        scratch_shapes=[
            pltpu.SemaphoreType.DMA((n,)),
            pltpu.SemaphoreType.DMA((n, N_CHIPS - 1)),
            pltpu.SemaphoreType.DMA((n, N_CHIPS - 1)),
            pltpu.SemaphoreType.DMA((n, N_CHIPS - 1)),
            pltpu.SemaphoreType.DMA((n, N_CHIPS - 1)),
        ],
    )(*shards)


def _scatter_pieces(pieces, *, name):
    n = len(pieces)

    def body(*refs):
        ins, own, sib = refs[:n], refs[n : 2 * n], refs[2 * n : 3 * n]
        local_sem, send_sem, recv_sem, pass_send, pass_recv = refs[3 * n :]
        x, y, c, k = _position()
        sibling = (x, y, 1 - c)

        def over_ici(a, j, piece, slot, to):
            return pltpu.make_async_remote_copy(
                src_ref=ins[a].at[piece],
                dst_ref=own[a].at[slot],
                send_sem=send_sem.at[a, j],
                recv_sem=recv_sem.at[a, j],
                device_id=to,
                device_id_type=MESH,
            )

        def to_sibling(a, j, slot):
            return pltpu.make_async_remote_copy(
                src_ref=own[a].at[slot],
                dst_ref=sib[a].at[slot],
                send_sem=pass_send.at[a, j],
                recv_sem=pass_recv.at[a, j],
                device_id=sibling,
                device_id_type=MESH,
            )

        mine = [pltpu.make_async_copy(ins[a].at[k], own[a].at[k], local_sem.at[a]) for a in range(n)]
        for cp in mine:
            cp.start()
        sent = []
        for j in range(N_CHIPS - 1):
            other = k ^ (j + 1)
            for a in range(n):
                cp = over_ici(a, j, other, k, _chip_device(other, c))
                cp.start()
                sent.append(cp)
        for a in range(n):
            mine[a].wait()
            cp = to_sibling(a, N_CHIPS - 1, k)
            cp.start()
            sent.append(cp)
        for j in range(N_CHIPS - 1):
            other = k ^ (j + 1)
            for a in range(n):
                over_ici(a, j, other, other, sibling).wait_recv()
                cp = to_sibling(a, j, other)
                cp.start()
                sent.append(cp)
        for j in range(N_CHIPS):
            for a in range(n):
                to_sibling(a, j, k).wait_recv()
        for cp in sent:
            cp.wait_send()

    shapes = [jax.ShapeDtypeStruct(a.shape, a.dtype) for a in pieces]
    outs = pl.pallas_call(
        body,
        name=name,
        in_specs=[ANY] * n,
        out_specs=[ANY] * (2 * n),
        out_shape=shapes + shapes,
        scratch_shapes=[
            pltpu.SemaphoreType.DMA((n,)),
            pltpu.SemaphoreType.DMA((n, N_CHIPS - 1)),
            pltpu.SemaphoreType.DMA((n, N_CHIPS - 1)),
            pltpu.SemaphoreType.DMA((n, N_CHIPS)),
            pltpu.SemaphoreType.DMA((n, N_CHIPS)),
        ],
    )(*pieces)
    return outs[:n], outs[n:]


def _allreduce_small(buf, *, name):
    def body(x_ref, o_ref, slots, send_sem, recv_sem):
        x, y, c, _ = _position()
        me = 4 * x + 2 * y + c
        slots[me] = x_ref[...]

        def copy(d, slot):
            peer = me ^ d
            return pltpu.make_async_remote_copy(
                src_ref=x_ref,
                dst_ref=slots.at[slot],
                send_sem=send_sem.at[d - 1],
                recv_sem=recv_sem.at[d - 1],
                device_id=(peer >> 2, (peer >> 1) & 1, peer & 1),
                device_id_type=MESH,
            )

        sent = [copy(d, me) for d in range(1, N_DEV)]
        for cp in sent:
            cp.start()
        for d in range(1, N_DEV):
            copy(d, me ^ d).wait_recv()
        for cp in sent:
            cp.wait_send()
        acc = slots[0]
        for s in range(1, N_DEV):
            acc = acc + slots[s]
        o_ref[...] = acc

    return pl.pallas_call(
        body,
        name=name,
        in_specs=[pl.BlockSpec(memory_space=pltpu.VMEM)],
        out_specs=pl.BlockSpec(memory_space=pltpu.VMEM),
        out_shape=jax.ShapeDtypeStruct(buf.shape, F32),
        scratch_shapes=[
            pltpu.VMEM((N_DEV,) + buf.shape, F32),
            pltpu.SemaphoreType.DMA((N_DEV - 1,)),
            pltpu.SemaphoreType.DMA((N_DEV - 1,)),
        ],
        compiler_params=pltpu.CompilerParams(vmem_limit_bytes=VMEM_LIMIT),
    )(buf)


def _mixer_of(i):
    return i % N_MIXERS, i // N_MIXERS


def _forward_backward(x, target, norms, layers):
    s, d = x.shape
    depth = len(layers)
    heads = d // HEAD_DIM
    hkv = heads // GQA_GROUP
    cos, sin = _rope_tables(s)
    saved = []
    h = x
    for i, lw in enumerate(layers):
        kind, j = _mixer_of(i)
        tag = f"l{i}"
        sv = {"h_in": h}
        a = _rms_fwd(h, norms["mix_norm"][i : i + 1], name=f"{tag}_mix_norm")
        qkv = _mm_nn(a, lw["w_in"], out_dtype=BF16, name=f"{tag}_w_in")
        if kind == 0:
            bias = _na_bias(_na_base(norms["na_rpb"][j]), name=f"{tag}_na_bias")
            o = _na_fwd(qkv, bias, name=f"{tag}_na_fwd")
            sv["bias"] = bias
        elif kind == 1:
            o = _sc_mid_fwd(qkv, lw["sc_conv_w"], name=f"{tag}_sc_fwd")
        else:
            gq, gk = norms["gqa_q_norm"][j : j + 1], norms["gqa_k_norm"][j : j + 1]
            qkn = _gqa_prep_fwd(qkv, gq, gk, cos, sin, heads, hkv, name=f"{tag}_gqa_prep")
            o = _gqa_fwd(qkn, qkv, heads, hkv, name=f"{tag}_gqa_fwd")
            sv["qkn"] = qkn
        h_mid = _mm_nn(o, lw["w_out"], out_dtype=F32, residual=h, name=f"{tag}_w_out")
        b = _rms_fwd(h_mid, norms["ffn_norm"][i : i + 1], name=f"{tag}_ffn_norm")
        up = _mm_nn(b, lw["w_up"], out_dtype=BF16, name=f"{tag}_w_up")
        act = _ffn_mid_fwd(up, lw["ffn_conv_w"], lw["ffn_conv_b"], name=f"{tag}_ffn_fwd")
        h = _mm_nn(act, lw["w_down"], out_dtype=F32, residual=h_mid, name=f"{tag}_w_down")
        sv.update(a=a, qkv=qkv, o=o, h_mid=h_mid, b=b, up=up, act=act)
        saved.append(sv)

    dh, d_final, loss = _loss_head(h, norms["final_norm"][None], target, name="loss_head")

    big = [None] * depth
    small = {"final_norm": d_final, "mix_norm": [None] * depth, "ffn_norm": [None] * depth,
             "ffn_conv_w": [None] * depth, "ffn_conv_b": [None] * depth, "na_rpb": {}}
    for i in reversed(range(depth)):
        kind, j = _mixer_of(i)
        tag = f"l{i}b"
        lw, sv = layers[i], saved[i]
        dact = _mm_nt(dh, lw["w_down"], out_dtype=BF16, name=f"{tag}_d_act")
        dw_down = _mm_tn(sv["act"], dh, 1, name=f"{tag}_dw_down")
        dug, duu, dwg, dwu, dbg, dbu = _ffn_mid_bwd(
            sv["up"], lw["ffn_conv_w"], lw["ffn_conv_b"], dact, name=f"{tag}_ffn_bwd"
        )
        dup = jnp.concatenate([dug, duu], axis=1)
        small["ffn_conv_w"][i] = jnp.concatenate([dwg, dwu], axis=1)
        small["ffn_conv_b"][i] = jnp.concatenate([dbg, dbu], axis=1)
        dw_up = _mm_tn(sv["b"], dup, N_CHIPS, name=f"{tag}_dw_up")
        db = _mm_nt(dup, lw["w_up"], out_dtype=F32, name=f"{tag}_d_b")
        dh_mid, small["ffn_norm"][i] = _rms_bwd(
            sv["h_mid"], norms["ffn_norm"][i : i + 1], db, dh, name=f"{tag}_ffn_norm"
        )
        do = _mm_nt(dh_mid, lw["w_out"], out_dtype=BF16, name=f"{tag}_d_o")
        dw_out = _mm_tn(sv["o"], dh_mid, 1, name=f"{tag}_dw_out")
        if kind == 0:
            dq, dk, dv, dbias = _na_bwd(sv["qkv"], sv["bias"], do, name=f"{tag}_na_bwd")
            dqkv = jnp.concatenate([dq, dk, dv], axis=1)
            small["na_rpb"][j] = _rpb_fold(dbias, name=f"{tag}_rpb_fold")
        elif kind == 1:
            dgb, dgc, dhh, small["sc_conv_w"] = _sc_mid_bwd(sv["qkv"], lw["sc_conv_w"], do, name=f"{tag}_sc_bwd")
            dqkv = jnp.concatenate([dgb, dgc, dhh], axis=1)
        else:
            gq, gk = norms["gqa_q_norm"][j : j + 1], norms["gqa_k_norm"][j : j + 1]
            dqn, dkn, dv = _gqa_bwd(sv["qkn"], sv["qkv"], do, heads, hkv, name=f"{tag}_gqa_bwd")
            dqk, small["gqa_q_norm"], small["gqa_k_norm"] = _gqa_prep_bwd(
                sv["qkv"], gq, gk, cos, sin, dqn, dkn, heads, hkv, name=f"{tag}_gqa_prep_bwd"
            )
            dqkv = jnp.concatenate([dqk, dv], axis=1)
        dw_in = _mm_tn(sv["a"], dqkv, N_CHIPS, name=f"{tag}_dw_in")
        da = _mm_nt(dqkv, lw["w_in"], out_dtype=F32, name=f"{tag}_d_a")
        dh, small["mix_norm"][i] = _rms_bwd(
            sv["h_in"], norms["mix_norm"][i : i + 1], da, dh_mid, name=f"{tag}_mix_norm"
        )
        rows_out, rows_down = dw_out.shape[1] // N_CHIPS, dw_down.shape[1] // N_CHIPS
        big[i] = [
            dw_in,
            dw_out.reshape(N_CHIPS, rows_out, d),
            dw_up,
            dw_down.reshape(N_CHIPS, rows_down, d),
        ]
    return loss, dh, big, small


def _pack(parts):
    flat = jnp.concatenate([p.reshape(-1).astype(F32) for p in parts])
    pad = (-flat.shape[0]) % (8 * LANES)
    return jnp.pad(flat, (0, pad)).reshape(-1, LANES)


def _unpack(buf, shapes):
    flat = buf.reshape(-1)
    out, at = [], 0
    for shp in shapes:
        size = 1
        for n in shp:
            size *= n
        out.append(flat[at : at + size].reshape(shp))
        at += size
    return out


def kernel(x, mix_norm, ffn_norm, final_norm, na_w_qkv, na_rpb, na_w_o, sc_w_in, sc_conv_w, sc_w_out, gqa_w_qkv, gqa_q_norm, gqa_k_norm, gqa_w_o, ffn_w_up, ffn_conv_w, ffn_conv_b, ffn_w_down, loss_target, m_mix_norm, m_ffn_norm, m_final_norm, m_na_w_qkv, m_na_rpb, m_na_w_o, m_sc_w_in, m_sc_conv_w, m_sc_w_out, m_gqa_w_qkv, m_gqa_q_norm, m_gqa_k_norm, m_gqa_w_o, m_ffn_w_up, m_ffn_conv_w, m_ffn_conv_b, m_ffn_w_down, v_mix_norm, v_ffn_norm, v_final_norm, v_na_w_qkv, v_na_rpb, v_na_w_o, v_sc_w_in, v_sc_conv_w, v_sc_w_out, v_gqa_w_qkv, v_gqa_q_norm, v_gqa_k_norm, v_gqa_w_o, v_ffn_w_up, v_ffn_conv_w, v_ffn_conv_b, v_ffn_w_down):
    depth, d = mix_norm.shape
    chip = 2 * lax.axis_index("x") + lax.axis_index("y")
    w_in_of = {0: na_w_qkv, 1: sc_w_in, 2: gqa_w_qkv}
    w_out_of = {0: na_w_o, 1: sc_w_out, 2: gqa_w_o}

    layers = []
    for i in range(depth):
        kind, j = _mixer_of(i)
        shards = [
            w_in_of[kind][j].astype(BF16),
            w_out_of[kind][j].astype(BF16),
            ffn_w_up[i].astype(BF16),
            ffn_w_down[i].astype(BF16),
            ffn_conv_w[i],
        ]
        split = [True, True, True, True, False]
        if kind == 1:
            shards.append(sc_conv_w[j])
            split.append(False)
        got = _gather_shards(shards, split, name=f"gather_l{i}")
        lw = {
            "w_in": got[0],
            "w_out": got[1].reshape(1, -1, d),
            "w_up": got[2],
            "w_down": got[3].reshape(1, -1, d),
            "ffn_conv_w": got[4],
            "ffn_conv_b": ffn_conv_b[i : i + 1],
        }
        if kind == 1:
            lw["sc_conv_w"] = got[5]
        layers.append(lw)

    norms = dict(mix_norm=mix_norm, ffn_norm=ffn_norm, final_norm=final_norm, na_rpb=na_rpb,
                 gqa_q_norm=gqa_q_norm, gqa_k_norm=gqa_k_norm)
    loss, grad_x, big, small = _forward_backward(x[0], loss_target[0], norms, layers)

    own, sib = [None] * depth, [None] * depth
    for i in range(depth):
        own[i], sib[i] = _scatter_pieces(big[i], name=f"scatter_l{i}")

    def update(w, m, v, slot, layer_ids, name):
        parts = [(own[i][slot], sib[i][slot]) for i in layer_ids]
        return _adamw_sharded(w, m, v, parts, name=name)

    by_kind = {k: [i for i in range(depth) if i % N_MIXERS == k] for k in range(N_MIXERS)}
    res = {
        "na_w_qkv": update(na_w_qkv, m_na_w_qkv, v_na_w_qkv, 0, by_kind[0], "adamw_na_w_qkv"),
        "na_w_o": update(na_w_o, m_na_w_o, v_na_w_o, 1, by_kind[0], "adamw_na_w_o"),
        "sc_w_in": update(sc_w_in, m_sc_w_in, v_sc_w_in, 0, by_kind[1], "adamw_sc_w_in"),
        "sc_w_out": update(sc_w_out, m_sc_w_out, v_sc_w_out, 1, by_kind[1], "adamw_sc_w_out"),
        "gqa_w_qkv": update(gqa_w_qkv, m_gqa_w_qkv, v_gqa_w_qkv, 0, by_kind[2], "adamw_gqa_w_qkv"),
        "gqa_w_o": update(gqa_w_o, m_gqa_w_o, v_gqa_w_o, 1, by_kind[2], "adamw_gqa_w_o"),
        "ffn_w_up": update(ffn_w_up, m_ffn_w_up, v_ffn_w_up, 2, list(range(depth)), "adamw_ffn_w_up"),
        "ffn_w_down": update(ffn_w_down, m_ffn_w_down, v_ffn_w_down, 3, list(range(depth)), "adamw_ffn_w_down"),
    }

    n_na = na_rpb.shape[0]
    rpb_flat = jnp.stack([small["na_rpb"][j] for j in range(n_na)])
    full_parts = [
        loss[:, :1],
        jnp.concatenate(small["mix_norm"], axis=0),
        jnp.concatenate(small["ffn_norm"], axis=0),
        small["final_norm"],
        rpb_flat,
        small["sc_conv_w"],
        small["gqa_q_norm"],
        small["gqa_k_norm"],
        jnp.stack(small["ffn_conv_w"]),
        jnp.concatenate(small["ffn_conv_b"], axis=0),
    ]
    summed = _unpack(_allreduce_small(_pack(full_parts), name="allreduce_small"), [p.shape for p in full_parts])
    loss_all, g_mix, g_ffn, g_final, g_rpb, g_sc_cw, g_gq, g_gk, g_ffn_cw, g_ffn_cb = summed
    g_rpb = g_rpb[:, :, : RPB_ROWS * RPB_COLS].reshape(na_rpb.shape)
    g_sc_cw = lax.dynamic_slice_in_dim(g_sc_cw, chip * sc_conv_w.shape[2], sc_conv_w.shape[2], axis=1)[None]
    g_ffn_cw = lax.dynamic_slice_in_dim(g_ffn_cw, chip * ffn_conv_w.shape[2], ffn_conv_w.shape[2], axis=2)
    small_names = ["mix_norm", "ffn_norm", "final_norm", "na_rpb", "sc_conv_w", "gqa_q_norm", "gqa_k_norm",
                   "ffn_conv_w", "ffn_conv_b"]
    small_g = [g_mix, g_ffn, g_final.reshape(final_norm.shape), g_rpb, g_sc_cw, g_gq, g_gk, g_ffn_cw, g_ffn_cb]
    small_w = [mix_norm, ffn_norm, final_norm, na_rpb, sc_conv_w, gqa_q_norm, gqa_k_norm, ffn_conv_w, ffn_conv_b]
    small_m = [m_mix_norm, m_ffn_norm, m_final_norm, m_na_rpb, m_sc_conv_w, m_gqa_q_norm, m_gqa_k_norm,
               m_ffn_conv_w, m_ffn_conv_b]
    small_v = [v_mix_norm, v_ffn_norm, v_final_norm, v_na_rpb, v_sc_conv_w, v_gqa_q_norm, v_gqa_k_norm,
               v_ffn_conv_w, v_ffn_conv_b]
    shapes = [w.shape for w in small_w]
    packed = _adamw_small(_pack(small_w), _pack(small_g), _pack(small_m), _pack(small_v), name="adamw_small")
    small_d, small_nm, small_nv = (_unpack(p, shapes) for p in packed)
    for n, g, dl, nm, nv in zip(small_names, small_g, small_d, small_nm, small_nv):
        res[n] = (g.reshape(dl.shape), dl, nm, nv)

    order = ["mix_norm", "ffn_norm", "final_norm", "na_w_qkv", "na_rpb", "na_w_o", "sc_w_in", "sc_conv_w",
             "sc_w_out", "gqa_w_qkv", "gqa_q_norm", "gqa_k_norm", "gqa_w_o", "ffn_w_up", "ffn_conv_w",
             "ffn_conv_b", "ffn_w_down"]
    outs = [loss_all.reshape(()), grad_x[None]]
    for part in range(4):
        outs.extend(res[n][part] for n in order)
    return tuple(outs)
```

```python
import functools

import jax
import jax.numpy as jnp
from jax import lax
from jax.experimental import pallas as pl
from jax.experimental.pallas import tpu as pltpu
from jax.experimental.pallas import tpu_sc as plsc

F32 = jnp.float32
BF16 = jnp.bfloat16
MESH = pl.DeviceIdType.MESH

N_CHIPS = 4
N_DEV = 8
N_MIXERS = 3
GRID_W = 64
HEAD_DIM = 128
EPS = 1e-6
NEG_INF = -1e30
NA_WIN_R = 8
NA_WIN_C = 16
GQA_GROUP = 4
ROPE_THETA = 10000.0
ADAM_LR = 0.001
ADAM_B1 = 0.9
ADAM_B2 = 0.999
ADAM_EPS = 1e-08
ADAM_WD = 0.01
ADAM_STEP = 10

LANES = 128
VMEM_LIMIT = 48 * 1024 * 1024
NT_DIMS = (((1,), (1,)), ((), ()))
TN_DIMS = (((0,), (0,)), ((), ()))


def _pick(n, cap, mult=LANES):
    best = None
    for t in range(mult, min(n, cap) + 1, mult):
        if n % t == 0:
            best = t
    return best if best is not None else n


def _params(*sem):
    return pltpu.CompilerParams(dimension_semantics=sem, vmem_limit_bytes=VMEM_LIMIT)


def _mm_nn(a, b, *, out_dtype, name, residual=None):
    m, k = a.shape
    nc, _, ncol = b.shape
    tm, tn, tk = _pick(m, 1024, 16), _pick(ncol, 1536), _pick(k, 512)
    per, nk = ncol // tn, k // tk

    def body(*refs):
        if residual is None:
            a_ref, b_ref, o_ref, acc = refs
        else:
            a_ref, b_ref, r_ref, o_ref, acc = refs
        kk = pl.program_id(2)

        @pl.when(kk == 0)
        def _():
            acc[...] = jnp.zeros_like(acc)

        acc[...] += jnp.dot(a_ref[...].astype(BF16), b_ref[...], preferred_element_type=F32)

        @pl.when(kk == nk - 1)
        def _():
            r = acc[...]
            if residual is not None:
                r = r + r_ref[...]
            o_ref[...] = r.astype(o_ref.dtype)

    in_specs = [
        pl.BlockSpec((tm, tk), lambda i, j, kk: (i, kk)),
        pl.BlockSpec((None, tk, tn), lambda i, j, kk: (j // per, kk, j % per)),
    ]
    ops = [a, b]
    if residual is not None:
        in_specs.append(pl.BlockSpec((tm, tn), lambda i, j, kk: (i, j)))
        ops.append(residual)
    return pl.pallas_call(
        body,
        name=name,
        grid=(m // tm, nc * per, nk),
        in_specs=in_specs,
        out_specs=pl.BlockSpec((tm, tn), lambda i, j, kk: (i, j)),
        out_shape=jax.ShapeDtypeStruct((m, nc * ncol), out_dtype),
        scratch_shapes=[pltpu.VMEM((tm, tn), F32)],
        compiler_params=_params("parallel", "parallel", "arbitrary"),
    )(*ops)


def _mm_nt(a, b, *, out_dtype, name):
    m, n = a.shape
    nc, k, ncol = b.shape
    tm, tko, tn = _pick(m, 1024, 16), _pick(k, 1024), _pick(ncol, 1536)
    per, nn = ncol // tn, n // tn

    def body(a_ref, b_ref, o_ref, acc):
        s = pl.program_id(2)

        @pl.when(s == 0)
        def _():
            acc[...] = jnp.zeros_like(acc)

        acc[...] += lax.dot_general(a_ref[...].astype(BF16), b_ref[...], NT_DIMS, preferred_element_type=F32)

        @pl.when(s == nn - 1)
        def _():
            o_ref[...] = acc[...].astype(o_ref.dtype)

    return pl.pallas_call(
        body,
        name=name,
        grid=(m // tm, k // tko, nn),
        in_specs=[
            pl.BlockSpec((tm, tn), lambda i, j, s: (i, s)),
            pl.BlockSpec((None, tko, tn), lambda i, j, s: (s // per, j, s % per)),
        ],
        out_specs=pl.BlockSpec((tm, tko), lambda i, j, s: (i, j)),
        out_shape=jax.ShapeDtypeStruct((m, k), out_dtype),
        scratch_shapes=[pltpu.VMEM((tm, tko), F32)],
        compiler_params=_params("parallel", "parallel", "arbitrary"),
    )(a, b)


def _mm_tn(a, g, nc, *, name):
    s, k = a.shape
    n = g.shape[1]
    ncol = n // nc
    ts, tko, tn = _pick(s, 512, 16), _pick(k, 1024), _pick(ncol, 1536)
    per, ns = ncol // tn, s // ts

    def body(a_ref, g_ref, o_ref, acc):
        t = pl.program_id(2)

        @pl.when(t == 0)
        def _():
            acc[...] = jnp.zeros_like(acc)

        acc[...] += lax.dot_general(
            a_ref[...].astype(BF16), g_ref[...].astype(BF16), TN_DIMS, preferred_element_type=F32
        )

        @pl.when(t == ns - 1)
        def _():
            o_ref[...] = acc[...].astype(o_ref.dtype)

    return pl.pallas_call(
        body,
        name=name,
        grid=(k // tko, nc * per, ns),
        in_specs=[
            pl.BlockSpec((ts, tko), lambda i, j, t: (t, i)),
            pl.BlockSpec((ts, tn), lambda i, j, t: (t, j)),
        ],
        out_specs=pl.BlockSpec((None, tko, tn), lambda i, j, t: (j // per, i, j % per)),
        out_shape=jax.ShapeDtypeStruct((nc, k, ncol), BF16),
        scratch_shapes=[pltpu.VMEM((tko, tn), F32)],
        compiler_params=_params("parallel", "parallel", "arbitrary"),
    )(a, g)


ROW_TILE = 256


def _rms_fwd(h, g, *, name):
    s, d = h.shape
    tr = _pick(s, ROW_TILE, 16)

    def body(h_ref, g_ref, o_ref):
        x = h_ref[...]
        r = lax.rsqrt(jnp.mean(x * x, axis=-1, keepdims=True) + EPS)
        o_ref[...] = (x * r * g_ref[...]).astype(o_ref.dtype)

    return pl.pallas_call(
        body,
        name=name,
        grid=(s // tr,),
        in_specs=[pl.BlockSpec((tr, d), lambda i: (i, 0)), pl.BlockSpec((1, d), lambda i: (0, 0))],
        out_specs=pl.BlockSpec((tr, d), lambda i: (i, 0)),
        out_shape=jax.ShapeDtypeStruct((s, d), BF16),
        compiler_params=_params("parallel"),
    )(h, g)


def _rms_bwd(h, g, dy, dres, *, name):
    s, d = h.shape
    tr = _pick(s, ROW_TILE, 16)

    def body(h_ref, g_ref, dy_ref, dres_ref, dh_ref, dg_ref):
        x = h_ref[...]
        r = lax.rsqrt(jnp.mean(x * x, axis=-1, keepdims=True) + EPS)
        xhat = x * r
        dyv = dy_ref[...].astype(F32)
        dyg = dyv * g_ref[...]
        dx = r * (dyg - xhat * jnp.mean(dyg * xhat, axis=-1, keepdims=True))
        dh_ref[...] = dres_ref[...] + dx
        part = jnp.sum(dyv * xhat, axis=0, keepdims=True)

        @pl.when(pl.program_id(0) == 0)
        def _():
            dg_ref[...] = part

        @pl.when(pl.program_id(0) != 0)
        def _():
            dg_ref[...] += part

    row = pl.BlockSpec((tr, d), lambda i: (i, 0))
    vec = pl.BlockSpec((1, d), lambda i: (0, 0))
    return pl.pallas_call(
        body,
        name=name,
        grid=(s // tr,),
        in_specs=[row, vec, row, row],
        out_specs=[row, vec],
        out_shape=[jax.ShapeDtypeStruct((s, d), F32), jax.ShapeDtypeStruct((1, d), F32)],
        compiler_params=_params("arbitrary"),
    )(h, g, dy, dres)


def _loss_head(h, g, target, *, name):
    s, d = h.shape
    tr = _pick(s, ROW_TILE, 16)

    def body(h_ref, g_ref, t_ref, dh_ref, dg_ref, loss_ref):
        x = h_ref[...]
        r = lax.rsqrt(jnp.mean(x * x, axis=-1, keepdims=True) + EPS)
        xhat = x * r
        gv = g_ref[...]
        err = xhat * gv - t_ref[...]
        dyv = err * (1.0 / d)
        dyg = dyv * gv
        dh_ref[...] = r * (dyg - xhat * jnp.mean(dyg * xhat, axis=-1, keepdims=True))
        part = jnp.sum(dyv * xhat, axis=0, keepdims=True)
        lpart = jnp.sum(jnp.sum(err * err, axis=-1, keepdims=True), axis=0, keepdims=True) * (0.5 / d)

        @pl.when(pl.program_id(0) == 0)
        def _():
            dg_ref[...] = part
            loss_ref[...] = jnp.broadcast_to(lpart, loss_ref.shape)

        @pl.when(pl.program_id(0) != 0)
        def _():
            dg_ref[...] += part
            loss_ref[...] += jnp.broadcast_to(lpart, loss_ref.shape)

    row = pl.BlockSpec((tr, d), lambda i: (i, 0))
    vec = pl.BlockSpec((1, d), lambda i: (0, 0))
    return pl.pallas_call(
        body,
        name=name,
        grid=(s // tr,),
        in_specs=[row, vec, row],
        out_specs=[row, vec, pl.BlockSpec((1, LANES), lambda i: (0, 0))],
        out_shape=[
            jax.ShapeDtypeStruct((s, d), F32),
            jax.ShapeDtypeStruct((1, d), F32),
            jax.ShapeDtypeStruct((1, LANES), F32),
        ],
        compiler_params=_params("arbitrary"),
    )(h, g, target)


def _shift_prev(x):
    row = lax.broadcasted_iota(jnp.int32, x.shape, 0)
    return jnp.where(row == 0, 0.0, pltpu.roll(x, 1, 0))


def _shift_next(x):
    n = x.shape[0]
    row = lax.broadcasted_iota(jnp.int32, x.shape, 0)
    return jnp.where(row == n - 1, 0.0, pltpu.roll(x, n - 1, 0))


def _conv3(x, w):
    xm, xp = _shift_prev(x), _shift_next(x)
    return xm * w[0:1] + x * w[1:2] + xp * w[2:3], xm, xp


def _conv3_t(d, w):
    return _shift_next(d) * w[0:1] + d * w[1:2] + _shift_prev(d) * w[2:3]


def _colsum(x):
    return jnp.sum(x, axis=0, keepdims=True)


def _ffn_mid_fwd(up, cw, cb, *, name):
    s, f2 = up.shape
    f = f2 // 2
    ncol = cw.shape[2]
    tc = _pick(ncol, 256)
    nt, per = f // tc, ncol // tc

    def body(ug_ref, uu_ref, wg_ref, wu_ref, bg_ref, bu_ref, o_ref):
        cg = _conv3(ug_ref[...].astype(F32), wg_ref[...])[0] + bg_ref[...]
        cu = _conv3(uu_ref[...].astype(F32), wu_ref[...])[0] + bu_ref[...]
        o_ref[...] = (cg * (1.0 / (1.0 + jnp.exp(-cg))) * cu).astype(o_ref.dtype)

    return pl.pallas_call(
        body,
        name=name,
        grid=(nt,),
        in_specs=[
            pl.BlockSpec((s, tc), lambda j: (0, j)),
            pl.BlockSpec((s, tc), lambda j: (0, nt + j)),
            pl.BlockSpec((None, 3, tc), lambda j: (j // per, 0, j % per)),
            pl.BlockSpec((None, 3, tc), lambda j: ((nt + j) // per, 0, (nt + j) % per)),
            pl.BlockSpec((1, tc), lambda j: (0, j)),
            pl.BlockSpec((1, tc), lambda j: (0, nt + j)),
        ],
        out_specs=pl.BlockSpec((s, tc), lambda j: (0, j)),
        out_shape=jax.ShapeDtypeStruct((s, f), BF16),
        compiler_params=_params("parallel"),
    )(up, up, cw, cw, cb, cb)


def _ffn_mid_bwd(up, cw, cb, dact, *, name):
    s, f2 = up.shape
    f = f2 // 2
    ncol = cw.shape[2]
    tc = _pick(ncol, 256)
    nt, per = f // tc, ncol // tc

    def body(ug_ref, uu_ref, wg_ref, wu_ref, bg_ref, bu_ref, da_ref, dug_ref, duu_ref, dwg_ref, dwu_ref, dbg_ref, dbu_ref):
        ug, uu = ug_ref[...].astype(F32), uu_ref[...].astype(F32)
        wg, wu = wg_ref[...], wu_ref[...]
        cg, ugm, ugp = _conv3(ug, wg)
        cu, uum, uup = _conv3(uu, wu)
        cg = cg + bg_ref[...]
        cu = cu + bu_ref[...]
        da = da_ref[...].astype(F32)
        sig = 1.0 / (1.0 + jnp.exp(-cg))
        dcu = da * (cg * sig)
        dcg = da * cu * (sig * (1.0 + cg * (1.0 - sig)))
        dug_ref[...] = _conv3_t(dcg, wg).astype(dug_ref.dtype)
        duu_ref[...] = _conv3_t(dcu, wu).astype(duu_ref.dtype)
        dwg_ref[0:1, :] = _colsum(dcg * ugm)
        dwg_ref[1:2, :] = _colsum(dcg * ug)
        dwg_ref[2:3, :] = _colsum(dcg * ugp)
        dwu_ref[0:1, :] = _colsum(dcu * uum)
        dwu_ref[1:2, :] = _colsum(dcu * uu)
        dwu_ref[2:3, :] = _colsum(dcu * uup)
        dbg_ref[...] = _colsum(dcg)
        dbu_ref[...] = _colsum(dcu)

    col = pl.BlockSpec((s, tc), lambda j: (0, j))
    w3 = pl.BlockSpec((3, tc), lambda j: (0, j))
    b1 = pl.BlockSpec((1, tc), lambda j: (0, j))
    return pl.pallas_call(
        body,
        name=name,
        grid=(nt,),
        in_specs=[
            col,
            pl.BlockSpec((s, tc), lambda j: (0, nt + j)),
            pl.BlockSpec((None, 3, tc), lambda j: (j // per, 0, j % per)),
            pl.BlockSpec((None, 3, tc), lambda j: ((nt + j) // per, 0, (nt + j) % per)),
            b1,
            pl.BlockSpec((1, tc), lambda j: (0, nt + j)),
            col,
        ],
        out_specs=[col, col, w3, w3, b1, b1],
        out_shape=[
            jax.ShapeDtypeStruct((s, f), BF16),
            jax.ShapeDtypeStruct((s, f), BF16),
            jax.ShapeDtypeStruct((3, f), F32),
            jax.ShapeDtypeStruct((3, f), F32),
            jax.ShapeDtypeStruct((1, f), F32),
            jax.ShapeDtypeStruct((1, f), F32),
        ],
        compiler_params=_params("parallel"),
    )(up, up, cw, cw, cb, cb, dact)


def _sc_mid_fwd(z, cw, *, name):
    s, d3 = z.shape
    d = d3 // 3
    ncol = cw.shape[2]
    tc = _pick(ncol, 256)
    nt, per = d // tc, ncol // tc

    def body(gb_ref, gc_ref, hh_ref, w_ref, o_ref):
        p = gc_ref[...].astype(F32) * hh_ref[...].astype(F32)
        o_ref[...] = (gb_ref[...].astype(F32) * _conv3(p, w_ref[...])[0]).astype(o_ref.dtype)

    return pl.pallas_call(
        body,
        name=name,
        grid=(nt,),
        in_specs=[
            pl.BlockSpec((s, tc), lambda j: (0, j)),
            pl.BlockSpec((s, tc), lambda j: (0, nt + j)),
            pl.BlockSpec((s, tc), lambda j: (0, 2 * nt + j)),
            pl.BlockSpec((None, 3, tc), lambda j: (j // per, 0, j % per)),
        ],
        out_specs=pl.BlockSpec((s, tc), lambda j: (0, j)),
        out_shape=jax.ShapeDtypeStruct((s, d), BF16),
        compiler_params=_params("parallel"),
    )(z, z, z, cw)


def _sc_mid_bwd(z, cw, dmid, *, name):
    s, d3 = z.shape
    d = d3 // 3
    ncol = cw.shape[2]
    tc = _pick(ncol, 256)
    nt, per = d // tc, ncol // tc

    def body(gb_ref, gc_ref, hh_ref, w_ref, dm_ref, dgb_ref, dgc_ref, dhh_ref, dw_ref):
        gb, gc, hh = gb_ref[...].astype(F32), gc_ref[...].astype(F32), hh_ref[...].astype(F32)
        w = w_ref[...]
        p = gc * hh
        cv, pm, pp = _conv3(p, w)
        dm = dm_ref[...].astype(F32)
        dgb_ref[...] = (dm * cv).astype(dgb_ref.dtype)
        dcv = dm * gb
        dp = _conv3_t(dcv, w)
        dgc_ref[...] = (dp * hh).astype(dgc_ref.dtype)
        dhh_ref[...] = (dp * gc).astype(dhh_ref.dtype)
        dw_ref[0:1, :] = _colsum(dcv * pm)
        dw_ref[1:2, :] = _colsum(dcv * p)
        dw_ref[2:3, :] = _colsum(dcv * pp)

    col = pl.BlockSpec((s, tc), lambda j: (0, j))
    return pl.pallas_call(
        body,
        name=name,
        grid=(nt,),
        in_specs=[
            col,
            pl.BlockSpec((s, tc), lambda j: (0, nt + j)),
            pl.BlockSpec((s, tc), lambda j: (0, 2 * nt + j)),
            pl.BlockSpec((None, 3, tc), lambda j: (j // per, 0, j % per)),
            col,
        ],
        out_specs=[col, col, col, pl.BlockSpec((3, tc), lambda j: (0, j))],
        out_shape=[jax.ShapeDtypeStruct((s, d), BF16)] * 3 + [jax.ShapeDtypeStruct((3, d), F32)],
        compiler_params=_params("parallel"),
    )(z, z, z, cw, dmid)


NA_KEYS = NA_WIN_R * GRID_W
NA_SKEW_BITS = 6


def _na_row_start(r, rows):
    return jnp.clip(r - NA_WIN_R // 2, 0, rows - NA_WIN_R)


def _na_bias_slot(r, rows):
    return _na_row_start(r, rows) - r + NA_WIN_R - 1


def _na_base(rpb):
    h = rpb.shape[0]
    pos, neg = rpb[:, :, NA_WIN_C - 1:], rpb[:, :, : NA_WIN_C - 1]
    zeros = jnp.zeros((h, NA_WIN_R, GRID_W - 2 * NA_WIN_C + 1), F32)
    out = []
    for first in range(NA_WIN_R):
        p = pos[:, first : first + NA_WIN_R]
        n = jnp.roll(neg[:, first : first + NA_WIN_R], -1, axis=1)
        out.append(jnp.concatenate([p, zeros, n], axis=-1).reshape(h, 1, NA_KEYS))
    return jnp.stack(out, axis=1)


def _skew(x, left):
    n = x.shape[1]
    row = lax.broadcasted_iota(jnp.int32, x.shape, 0)
    for b in range(NA_SKEW_BITS):
        shift = n - (1 << b) if left else (1 << b)
        x = jnp.where(((row >> b) & 1) == 1, pltpu.roll(x, shift, 1), x)
    return x


def _na_bias(base, *, name):
    h = base.shape[0]

    def body(b_ref, o_ref):
        x = _skew(jnp.broadcast_to(b_ref[...], (GRID_W, NA_KEYS)), left=False)
        q = lax.broadcasted_iota(jnp.int32, x.shape, 0)
        kc = lax.broadcasted_iota(jnp.int32, x.shape, 1) % GRID_W
        start = jnp.clip(q - NA_WIN_C // 2, 0, GRID_W - NA_WIN_C)
        o_ref[...] = jnp.where((kc >= start) & (kc < start + NA_WIN_C), x, NEG_INF)

    return pl.pallas_call(
        body,
        name=name,
        grid=(h, NA_WIN_R),
        in_specs=[pl.BlockSpec((None, None, 1, NA_KEYS), lambda i, j: (i, j, 0, 0))],
        out_specs=pl.BlockSpec((None, None, GRID_W, NA_KEYS), lambda i, j: (i, j, 0, 0)),
        out_shape=jax.ShapeDtypeStruct((h, NA_WIN_R, GRID_W, NA_KEYS), F32),
        compiler_params=_params("parallel", "parallel"),
    )(base)


def _na_specs(s, heads, rows):
    q = pl.BlockSpec((GRID_W, HEAD_DIM), lambda h, r: (r, h))
    k = pl.BlockSpec((s, HEAD_DIM), lambda h, r: (0, heads + h))
    v = pl.BlockSpec((s, HEAD_DIM), lambda h, r: (0, 2 * heads + h))
    bias = pl.BlockSpec((None, None, GRID_W, NA_KEYS), lambda h, r: (h, _na_bias_slot(r, rows), 0, 0))
    return q, k, v, bias


def _na_probs(q, k, bias):
    sc = lax.dot_general(q, k, NT_DIMS, preferred_element_type=F32) * (HEAD_DIM ** -0.5) + bias
    p = jnp.exp(sc - jnp.max(sc, axis=-1, keepdims=True))
    return p, jnp.sum(p, axis=-1, keepdims=True)


def _na_fwd(qkv, bias, *, name):
    s = qkv.shape[0]
    heads = qkv.shape[1] // (3 * HEAD_DIM)
    rows = s // GRID_W

    def body(q_ref, k_ref, v_ref, b_ref, o_ref):
        st = pl.multiple_of(_na_row_start(pl.program_id(1), rows) * GRID_W, GRID_W)
        p, l = _na_probs(q_ref[...], k_ref[pl.ds(st, NA_KEYS), :], b_ref[...])
        o = jnp.dot(p.astype(BF16), v_ref[pl.ds(st, NA_KEYS), :], preferred_element_type=F32)
        o_ref[...] = (o / l).astype(o_ref.dtype)

    q, k, v, b = _na_specs(s, heads, rows)
    return pl.pallas_call(
        body,
        name=name,
        grid=(heads, rows),
        in_specs=[q, k, v, b],
        out_specs=q,
        out_shape=jax.ShapeDtypeStruct((s, heads * HEAD_DIM), BF16),
        compiler_params=_params("parallel", "arbitrary"),
    )(qkv, qkv, qkv, bias)


def _na_bwd(qkv, bias, dout, *, name):
    s = qkv.shape[0]
    heads = qkv.shape[1] // (3 * HEAD_DIM)
    rows = s // GRID_W
    scale = HEAD_DIM ** -0.5

    def body(q_ref, k_ref, v_ref, b_ref, do_ref, dq_ref, dk_ref, dv_ref, db_ref, dk_acc, dv_acc):
        r = pl.program_id(1)

        @pl.when(r == 0)
        def _():
            dk_acc[...] = jnp.zeros_like(dk_acc)
            dv_acc[...] = jnp.zeros_like(dv_acc)

        st = pl.multiple_of(_na_row_start(r, rows) * GRID_W, GRID_W)
        win = pl.ds(st, NA_KEYS)
        q, k, v, do = q_ref[...], k_ref[win, :], v_ref[win, :], do_ref[...]
        p, l = _na_probs(q, k, b_ref[...])
        pn = p / l
        dp = lax.dot_general(do, v, NT_DIMS, preferred_element_type=F32)
        ds = pn * (dp - jnp.sum(pn * dp, axis=-1, keepdims=True))
        dsb = ds.astype(BF16)
        dq_ref[...] = (jnp.dot(dsb, k, preferred_element_type=F32) * scale).astype(dq_ref.dtype)
        dk_acc[win, :] += lax.dot_general(dsb, q, TN_DIMS, preferred_element_type=F32) * scale
        dv_acc[win, :] += lax.dot_general(pn.astype(BF16), do, TN_DIMS, preferred_element_type=F32)
        col = _colsum(_skew(ds, left=True))
        first = (r <= NA_WIN_R // 2) | (r >= rows - NA_WIN_R // 2 + 1)

        @pl.when(first)
        def _():
            db_ref[...] = col

        @pl.when(jnp.logical_not(first))
        def _():
            db_ref[...] += col

        @pl.when(r == rows - 1)
        def _():
            dk_ref[...] = dk_acc[...].astype(dk_ref.dtype)
            dv_ref[...] = dv_acc[...].astype(dv_ref.dtype)

    q, k, v, b = _na_specs(s, heads, rows)
    kv_out = pl.BlockSpec((s, HEAD_DIM), lambda h, r: (0, h))
    shape = jax.ShapeDtypeStruct((s, heads * HEAD_DIM), BF16)
    return pl.pallas_call(
        body,
        name=name,
        grid=(heads, rows),
        in_specs=[q, k, v, b, q],
        out_specs=[
            q,
            kv_out,
            kv_out,
            pl.BlockSpec((None, None, 1, NA_KEYS), lambda h, r: (h, _na_bias_slot(r, rows), 0, 0)),
        ],
        out_shape=[shape, shape, shape, jax.ShapeDtypeStruct((heads, NA_WIN_R, 1, NA_KEYS), F32)],
        scratch_shapes=[pltpu.VMEM((s, HEAD_DIM), F32), pltpu.VMEM((s, HEAD_DIM), F32)],
        compiler_params=_params("parallel", "arbitrary"),
    )(qkv, qkv, qkv, bias, dout)


RPB_ROWS = 2 * NA_WIN_R - 1
RPB_COLS = 2 * NA_WIN_C - 1
RPB_PAD = 512


def _rpb_fold_matrix():
    idx = jnp.arange(NA_WIN_R * NA_KEYS, dtype=jnp.int32)
    first, i, kc = idx // NA_KEYS, (idx // GRID_W) % NA_WIN_R, idx % GRID_W
    pos, neg = kc < NA_WIN_C, kc >= GRID_W - NA_WIN_C + 1
    dr = jnp.where(pos, first + i, first + (i + 1) % NA_WIN_R)
    dc = jnp.where(pos, kc + NA_WIN_C - 1, kc - (GRID_W - NA_WIN_C + 1))
    target = jnp.where(pos | neg, dr * RPB_COLS + dc, -1)
    return (target[:, None] == jnp.arange(RPB_PAD, dtype=jnp.int32)[None, :]).astype(F32)


def _rpb_fold(dbias, *, name):
    h = dbias.shape[0]
    flat = dbias.reshape(h, NA_WIN_R * NA_KEYS)

    def body(g_ref, m_ref, o_ref):
        o_ref[...] = jnp.dot(g_ref[...], m_ref[...], preferred_element_type=F32, precision=lax.Precision.HIGHEST)

    return pl.pallas_call(
        body,
        name=name,
        out_shape=jax.ShapeDtypeStruct((h, RPB_PAD), F32),
        compiler_params=pltpu.CompilerParams(vmem_limit_bytes=VMEM_LIMIT),
    )(flat, _rpb_fold_matrix())


GQA_Q_TILE = 256


def _rope_tables(s):
    t = jnp.arange(s)
    row = (t // GRID_W).astype(F32)[:, None]
    col = (t % GRID_W).astype(F32)[:, None]
    half = HEAD_DIM // 2
    inv = ROPE_THETA ** (-jnp.arange(0, half, 2, dtype=F32) / half)
    ang = jnp.concatenate([row * inv, row * inv, col * inv, col * inv], axis=-1)
    return jnp.cos(ang), jnp.sin(ang)


def _rot_half(y):
    quarter = HEAD_DIM // 4
    lane = lax.broadcasted_iota(jnp.int32, y.shape, 1)
    low = (lane % (2 * quarter)) < quarter
    return jnp.where(low, -pltpu.roll(y, HEAD_DIM - quarter, 1), pltpu.roll(y, quarter, 1))


def _gqa_prep_fwd(qkv, gq, gk, cos, sin, hq, hkv, *, name):
    s = qkv.shape[0]

    def body(x_ref, gq_ref, gk_ref, cos_ref, sin_ref, o_ref):
        isq = pl.program_id(0) < hq
        x = x_ref[...].astype(F32)
        g = jnp.where(isq, gq_ref[...], gk_ref[...])
        y = x * lax.rsqrt(jnp.mean(x * x, axis=-1, keepdims=True) + EPS) * g
        z = y * cos_ref[...] + _rot_half(y) * sin_ref[...]
        o_ref[...] = (z * jnp.where(isq, HEAD_DIM ** -0.5, 1.0)).astype(o_ref.dtype)

    head = pl.BlockSpec((s, HEAD_DIM), lambda h: (0, h))
    vec = pl.BlockSpec((1, HEAD_DIM), lambda h: (0, 0))
    tab = pl.BlockSpec((s, HEAD_DIM), lambda h: (0, 0))
    return pl.pallas_call(
        body,
        name=name,
        grid=(hq + hkv,),
        in_specs=[head, vec, vec, tab, tab],
        out_specs=head,
        out_shape=jax.ShapeDtypeStruct((s, (hq + hkv) * HEAD_DIM), BF16),
        compiler_params=_params("parallel"),
    )(qkv, gq, gk, cos, sin)


def _gqa_prep_bwd(qkv, gq, gk, cos, sin, dqn, dkn, hq, hkv, *, name):
    s = qkv.shape[0]

    def body(x_ref, gq_ref, gk_ref, cos_ref, sin_ref, dq_ref, dk_ref, dx_ref, dgq_ref, dgk_ref):
        hh = pl.program_id(0)
        isq = hh < hq
        x = x_ref[...].astype(F32)
        g = jnp.where(isq, gq_ref[...], gk_ref[...])
        r = lax.rsqrt(jnp.mean(x * x, axis=-1, keepdims=True) + EPS)
        xhat = x * r
        dz = jnp.where(isq, dq_ref[...].astype(F32) * (HEAD_DIM ** -0.5), dk_ref[...].astype(F32))
        dy = dz * cos_ref[...] - _rot_half(dz * sin_ref[...])
        dyg = dy * g
        dx_ref[...] = (r * (dyg - xhat * jnp.mean(dyg * xhat, axis=-1, keepdims=True))).astype(dx_ref.dtype)
        part = _colsum(dy * xhat)

        @pl.when(hh == 0)
        def _():
            dgq_ref[...] = jnp.zeros_like(dgq_ref)
            dgk_ref[...] = jnp.zeros_like(dgk_ref)

        @pl.when(isq)
        def _():
            dgq_ref[...] += part

        @pl.when(jnp.logical_not(isq))
        def _():
            dgk_ref[...] += part

    head = pl.BlockSpec((s, HEAD_DIM), lambda h: (0, h))
    vec = pl.BlockSpec((1, HEAD_DIM), lambda h: (0, 0))
    tab = pl.BlockSpec((s, HEAD_DIM), lambda h: (0, 0))
    return pl.pallas_call(
        body,
        name=name,
        grid=(hq + hkv,),
        in_specs=[
            head,
            vec,
            vec,
            tab,
            tab,
            pl.BlockSpec((s, HEAD_DIM), lambda h: (0, jnp.minimum(h, hq - 1))),
            pl.BlockSpec((s, HEAD_DIM), lambda h: (0, jnp.maximum(h - hq, 0))),
        ],
        out_specs=[head, vec, vec],
        out_shape=[
            jax.ShapeDtypeStruct((s, (hq + hkv) * HEAD_DIM), BF16),
            jax.ShapeDtypeStruct((1, HEAD_DIM), F32),
            jax.ShapeDtypeStruct((1, HEAD_DIM), F32),
        ],
        compiler_params=_params("arbitrary"),
    )(qkv, gq, gk, cos, sin, dqn, dkn)


def _gqa_fwd(qkn, qkv, hq, hkv, *, name):
    s = qkv.shape[0]
    tq = _pick(s, GQA_Q_TILE, 16)

    def body(q_ref, k_ref, v_ref, o_ref):
        sc = lax.dot_general(q_ref[...], k_ref[...], NT_DIMS, preferred_element_type=F32)
        p = jnp.exp(sc - jnp.max(sc, axis=-1, keepdims=True))
        l = jnp.sum(p, axis=-1, keepdims=True)
        o_ref[...] = (jnp.dot(p.astype(BF16), v_ref[...], preferred_element_type=F32) / l).astype(o_ref.dtype)

    q = pl.BlockSpec((tq, HEAD_DIM), lambda h, i: (i, h))
    return pl.pallas_call(
        body,
        name=name,
        grid=(hq, s // tq),
        in_specs=[
            q,
            pl.BlockSpec((s, HEAD_DIM), lambda h, i: (0, hq + h // GQA_GROUP)),
            pl.BlockSpec((s, HEAD_DIM), lambda h, i: (0, hq + hkv + h // GQA_GROUP)),
        ],
        out_specs=q,
        out_shape=jax.ShapeDtypeStruct((s, hq * HEAD_DIM), BF16),
        compiler_params=_params("parallel", "parallel"),
    )(qkn, qkn, qkv)


def _gqa_bwd(qkn, qkv, dout, hq, hkv, *, name):
    s = qkv.shape[0]
    tq = _pick(s, GQA_Q_TILE, 16)
    nq = s // tq

    def body(q_ref, k_ref, v_ref, do_ref, dq_ref, dk_ref, dv_ref, dk_acc, dv_acc):
        g, i = pl.program_id(1), pl.program_id(2)

        @pl.when((g == 0) & (i == 0))
        def _():
            dk_acc[...] = jnp.zeros_like(dk_acc)
            dv_acc[...] = jnp.zeros_like(dv_acc)

        q, k, v, do = q_ref[...], k_ref[...], v_ref[...], do_ref[...]
        sc = lax.dot_general(q, k, NT_DIMS, preferred_element_type=F32)
        p = jnp.exp(sc - jnp.max(sc, axis=-1, keepdims=True))
        pn = p / jnp.sum(p, axis=-1, keepdims=True)
        dp = lax.dot_general(do, v, NT_DIMS, preferred_element_type=F32)
        dsb = (pn * (dp - jnp.sum(pn * dp, axis=-1, keepdims=True))).astype(BF16)
        dq_ref[...] = jnp.dot(dsb, k, preferred_element_type=F32).astype(dq_ref.dtype)
        dk_acc[...] += lax.dot_general(dsb, q, TN_DIMS, preferred_element_type=F32)
        dv_acc[...] += lax.dot_general(pn.astype(BF16), do, TN_DIMS, preferred_element_type=F32)

        @pl.when((g == GQA_GROUP - 1) & (i == nq - 1))
        def _():
            dk_ref[...] = dk_acc[...].astype(dk_ref.dtype)
            dv_ref[...] = dv_acc[...].astype(dv_ref.dtype)

    q = pl.BlockSpec((tq, HEAD_DIM), lambda kv, g, i: (i, kv * GQA_GROUP + g))
    kv_out = pl.BlockSpec((s, HEAD_DIM), lambda kv, g, i: (0, kv))
    return pl.pallas_call(
        body,
        name=name,
        grid=(hkv, GQA_GROUP, nq),
        in_specs=[
            q,
            pl.BlockSpec((s, HEAD_DIM), lambda kv, g, i: (0, hq + kv)),
            pl.BlockSpec((s, HEAD_DIM), lambda kv, g, i: (0, hq + hkv + kv)),
            q,
        ],
        out_specs=[q, kv_out, kv_out],
        out_shape=[
            jax.ShapeDtypeStruct((s, hq * HEAD_DIM), BF16),
            jax.ShapeDtypeStruct((s, hkv * HEAD_DIM), BF16),
            jax.ShapeDtypeStruct((s, hkv * HEAD_DIM), BF16),
        ],
        scratch_shapes=[pltpu.VMEM((s, HEAD_DIM), F32), pltpu.VMEM((s, HEAD_DIM), F32)],
        compiler_params=_params("parallel", "arbitrary", "arbitrary"),
    )(qkn, qkn, qkv, dout)


ADAM_ROWS = 64


def _adam_update(w, g, m, v):
    m = ADAM_B1 * m + (1.0 - ADAM_B1) * g
    v = ADAM_B2 * v + (1.0 - ADAM_B2) * (g * g)
    m_hat = m / (1.0 - ADAM_B1 ** ADAM_STEP)
    v_hat = v / (1.0 - ADAM_B2 ** ADAM_STEP)
    return -ADAM_LR * (m_hat / (jnp.sqrt(v_hat) + ADAM_EPS) + ADAM_WD * w), m, v


def _adamw_sharded(w, m, v, parts, *, name):
    nl, rows, cols = w.shape
    tr = _pick(rows, ADAM_ROWS, 16)
    nr = rows // tr

    def body(*refs):
        w_ref, m_ref, v_ref = refs[:3]
        piece_refs = refs[3 : 3 + 2 * nl]
        g_ref, d_ref, nm_ref, nv_ref = refs[3 + 2 * nl :]
        layer = pl.program_id(0)
        for l in range(nl):

            @pl.when(layer == l)
            def _(l=l):
                own, sib = piece_refs[2 * l], piece_refs[2 * l + 1]
                g = own[0].astype(F32) + sib[0].astype(F32)
                for q in range(1, N_CHIPS):
                    g = g + (own[q].astype(F32) + sib[q].astype(F32))
                g_ref[...] = g
                d_ref[...], nm_ref[...], nv_ref[...] = _adam_update(w_ref[...], g, m_ref[...], v_ref[...])

    def piece_spec(l):
        return pl.BlockSpec(
            (N_CHIPS, tr, cols), lambda ll, i: (0, jnp.where(ll < l, 0, jnp.where(ll == l, i, nr - 1)), 0)
        )

    full = pl.BlockSpec((None, tr, cols), lambda ll, i: (ll, i, 0))
    flat = [p for pair in parts for p in pair]
    shape = jax.ShapeDtypeStruct(w.shape, F32)
    return pl.pallas_call(
        body,
        name=name,
        grid=(nl, nr),
        in_specs=[full] * 3 + [piece_spec(l) for l in range(nl) for _ in range(2)],
        out_specs=[full] * 4,
        out_shape=[shape] * 4,
        compiler_params=_params("arbitrary", "arbitrary"),
    )(w, m, v, *flat)


def _adamw_small(w, g, m, v, *, name):
    def body(w_ref, g_ref, m_ref, v_ref, d_ref, nm_ref, nv_ref):
        d_ref[...], nm_ref[...], nv_ref[...] = _adam_update(w_ref[...], g_ref[...], m_ref[...], v_ref[...])

    shape = jax.ShapeDtypeStruct(w.shape, F32)
    return pl.pallas_call(
        body,
        name=name,
        out_shape=[shape] * 3,
        compiler_params=pltpu.CompilerParams(vmem_limit_bytes=VMEM_LIMIT),
    )(w, g, m, v)


def _position():
    x, y, c = lax.axis_index("x"), lax.axis_index("y"), lax.axis_index("c")
    return x, y, c, 2 * x + y


def _chip_device(chip, c):
    return (chip >> 1, chip & 1, c)


def _handshake(peers):
    barrier = pltpu.get_barrier_semaphore()
    for peer in peers:
        pl.semaphore_signal(barrier, inc=1, device_id=peer, device_id_type=MESH)
    pl.semaphore_wait(barrier, len(peers))


def _gather_shards(shards, split, *, name, collective_id):
    n = len(shards)

    def body(*refs):
        ins, outs = refs[:n], refs[n : 2 * n]
        local_sem, send_sem, recv_sem, pass_send, pass_recv = refs[2 * n :]
        x, y, c, k = _position()
        sibling = (x, y, 1 - c)
        _handshake([sibling] + [_chip_device(k ^ j, c) for j in range(1, N_CHIPS)])

        def rows(a, core):
            r = shards[a].shape[0]
            return pl.ds(core * (r // 2), r // 2) if split[a] else pl.ds(0, r)

        def over_ici(a, j, src_chip, to):
            return pltpu.make_async_remote_copy(
                src_ref=ins[a].at[rows(a, c)],
                dst_ref=outs[a].at[src_chip, rows(a, c)],
                send_sem=send_sem.at[a, j],
                recv_sem=recv_sem.at[a, j],
                device_id=to,
                device_id_type=MESH,
            )

        def to_sibling(a, j, chip, core):
            part = outs[a].at[chip, rows(a, core)]
            return pltpu.make_async_remote_copy(
                src_ref=part,
                dst_ref=part,
                send_sem=pass_send.at[a, j],
                recv_sem=pass_recv.at[a, j],
                device_id=sibling,
                device_id_type=MESH,
            )

        mine = [pltpu.make_async_copy(ins[a], outs[a].at[k], local_sem.at[a]) for a in range(n)]
        for cp in mine:
            cp.start()
        sent = []
        for j in range(N_CHIPS - 1):
            other = k ^ (j + 1)
            for a in range(n):
                cp = over_ici(a, j, k, _chip_device(other, c))
                cp.start()
                sent.append(cp)
        for j in range(N_CHIPS - 1):
            other = k ^ (j + 1)
            for a in range(n):
                over_ici(a, j, other, sibling).wait_recv()
                if split[a]:
                    cp = to_sibling(a, j, other, c)
                    cp.start()
                    sent.append(cp)
        for j in range(N_CHIPS - 1):
            other = k ^ (j + 1)
            for a in range(n):
                if split[a]:
                    to_sibling(a, j, other, 1 - c).wait_recv()
        for cp in sent:
            cp.wait_send()
        for cp in mine:
            cp.wait()

    return pl.kernel(
        body,
        name=name,
        out_type=[jax.ShapeDtypeStruct((N_CHIPS,) + a.shape, a.dtype) for a in shards],
        mesh=plsc.ScalarSubcoreMesh(axis_name="sequencer", num_cores=1),
        scratch_types=[
            pltpu.SemaphoreType.DMA((n,)),
            pltpu.SemaphoreType.DMA((n, N_CHIPS - 1)),
            pltpu.SemaphoreType.DMA((n, N_CHIPS - 1)),
            pltpu.SemaphoreType.DMA((n, N_CHIPS - 1)),
            pltpu.SemaphoreType.DMA((n, N_CHIPS - 1)),
        ],
        compiler_params=pltpu.CompilerParams(collective_id=collective_id),
    )(*shards)


def _scatter_pieces(pieces, *, name, collective_id):
    n = len(pieces)

    def body(*refs):
        ins, own, sib = refs[:n], refs[n : 2 * n], refs[2 * n : 3 * n]
        local_sem, send_sem, recv_sem, pass_send, pass_recv = refs[3 * n :]
        x, y, c, k = _position()
        sibling = (x, y, 1 - c)
        _handshake([sibling] + [_chip_device(k ^ j, c) for j in range(1, N_CHIPS)])

        def over_ici(a, j, piece, slot, to):
            return pltpu.make_async_remote_copy(
                src_ref=ins[a].at[piece],
                dst_ref=own[a].at[slot],
                send_sem=send_sem.at[a, j],
                recv_sem=recv_sem.at[a, j],
                device_id=to,
                device_id_type=MESH,
            )

        def to_sibling(a, j, slot):
            return pltpu.make_async_remote_copy(
                src_ref=own[a].at[slot],
                dst_ref=sib[a].at[slot],
                send_sem=pass_send.at[a, j],
                recv_sem=pass_recv.at[a, j],
                device_id=sibling,
                device_id_type=MESH,
            )

        mine = [pltpu.make_async_copy(ins[a].at[k], own[a].at[k], local_sem.at[a]) for a in range(n)]
        for cp in mine:
            cp.start()
        sent = []
        for j in range(N_CHIPS - 1):
            other = k ^ (j + 1)
            for a in range(n):
                cp = over_ici(a, j, other, k, _chip_device(other, c))
                cp.start()
                sent.append(cp)
        for a in range(n):
            mine[a].wait()
            cp = to_sibling(a, N_CHIPS - 1, k)
            cp.start()
            sent.append(cp)
        for j in range(N_CHIPS - 1):
            other = k ^ (j + 1)
            for a in range(n):
                over_ici(a, j, other, other, sibling).wait_recv()
                cp = to_sibling(a, j, other)
                cp.start()
                sent.append(cp)
        for j in range(N_CHIPS):
            for a in range(n):
                to_sibling(a, j, k).wait_recv()
        for cp in sent:
            cp.wait_send()

    shapes = [jax.ShapeDtypeStruct(a.shape, a.dtype) for a in pieces]
    outs = pl.kernel(
        body,
        name=name,
        out_type=shapes + shapes,
        mesh=plsc.ScalarSubcoreMesh(axis_name="sequencer", num_cores=1),
        scratch_types=[
            pltpu.SemaphoreType.DMA((n,)),
            pltpu.SemaphoreType.DMA((n, N_CHIPS - 1)),
            pltpu.SemaphoreType.DMA((n, N_CHIPS - 1)),
            pltpu.SemaphoreType.DMA((n, N_CHIPS)),
            pltpu.SemaphoreType.DMA((n, N_CHIPS)),
        ],
        compiler_params=pltpu.CompilerParams(collective_id=collective_id),
    )(*pieces)
    return outs[:n], outs[n:]


def _allreduce_small(buf, *, name):
    def body(x_ref, o_ref, slots, send_sem, recv_sem):
        x, y, c, _ = _position()
        me = 4 * x + 2 * y + c
        slots[me] = x_ref[...]

        def copy(d, slot):
            peer = me ^ d
            return pltpu.make_async_remote_copy(
                src_ref=x_ref,
                dst_ref=slots.at[slot],
                send_sem=send_sem.at[d - 1],
                recv_sem=recv_sem.at[d - 1],
                device_id=(peer >> 2, (peer >> 1) & 1, peer & 1),
                device_id_type=MESH,
            )

        sent = [copy(d, me) for d in range(1, N_DEV)]
        for cp in sent:
            cp.start()
        for d in range(1, N_DEV):
            copy(d, me ^ d).wait_recv()
        for cp in sent:
            cp.wait_send()
        acc = slots[0]
        for s in range(1, N_DEV):
            acc = acc + slots[s]
        o_ref[...] = acc

    return pl.pallas_call(
        body,
        name=name,
        in_specs=[pl.BlockSpec(memory_space=pltpu.VMEM)],
        out_specs=pl.BlockSpec(memory_space=pltpu.VMEM),
        out_shape=jax.ShapeDtypeStruct(buf.shape, F32),
        scratch_shapes=[
            pltpu.VMEM((N_DEV,) + buf.shape, F32),
            pltpu.SemaphoreType.DMA((N_DEV - 1,)),
            pltpu.SemaphoreType.DMA((N_DEV - 1,)),
        ],
        compiler_params=pltpu.CompilerParams(vmem_limit_bytes=VMEM_LIMIT),
    )(buf)


def _mixer_of(i):
    return i % N_MIXERS, i // N_MIXERS


def _forward_backward(x, target, norms, layers, send=lambda i, pieces: pieces):
    s, d = x.shape
    depth = len(layers)
    heads = d // HEAD_DIM
    hkv = heads // GQA_GROUP
    cos, sin = _rope_tables(s)
    saved = []
    h = x
    for i, lw in enumerate(layers):
        kind, j = _mixer_of(i)
        tag = f"l{i}"
        sv = {"h_in": h}
        a = _rms_fwd(h, norms["mix_norm"][i : i + 1], name=f"{tag}_mix_norm")
        qkv = _mm_nn(a, lw["w_in"], out_dtype=BF16, name=f"{tag}_w_in")
        if kind == 0:
            bias = _na_bias(_na_base(norms["na_rpb"][j]), name=f"{tag}_na_bias")
            o = _na_fwd(qkv, bias, name=f"{tag}_na_fwd")
            sv["bias"] = bias
        elif kind == 1:
            o = _sc_mid_fwd(qkv, lw["sc_conv_w"], name=f"{tag}_sc_fwd")
        else:
            gq, gk = norms["gqa_q_norm"][j : j + 1], norms["gqa_k_norm"][j : j + 1]
            qkn = _gqa_prep_fwd(qkv, gq, gk, cos, sin, heads, hkv, name=f"{tag}_gqa_prep")
            o = _gqa_fwd(qkn, qkv, heads, hkv, name=f"{tag}_gqa_fwd")
            sv["qkn"] = qkn
        h_mid = _mm_nn(o, lw["w_out"], out_dtype=F32, residual=h, name=f"{tag}_w_out")
        b = _rms_fwd(h_mid, norms["ffn_norm"][i : i + 1], name=f"{tag}_ffn_norm")
        up = _mm_nn(b, lw["w_up"], out_dtype=BF16, name=f"{tag}_w_up")
        act = _ffn_mid_fwd(up, lw["ffn_conv_w"], lw["ffn_conv_b"], name=f"{tag}_ffn_fwd")
        h = _mm_nn(act, lw["w_down"], out_dtype=F32, residual=h_mid, name=f"{tag}_w_down")
        sv.update(a=a, qkv=qkv, o=o, h_mid=h_mid, b=b, up=up, act=act)
        saved.append(sv)

    dh, d_final, loss = _loss_head(h, norms["final_norm"][None], target, name="loss_head")

    big = [None] * depth
    small = {"final_norm": d_final, "mix_norm": [None] * depth, "ffn_norm": [None] * depth,
             "ffn_conv_w": [None] * depth, "ffn_conv_b": [None] * depth, "na_rpb": {}}
    for i in reversed(range(depth)):
        kind, j = _mixer_of(i)
        tag = f"l{i}b"
        lw, sv = layers[i], saved[i]
        dact = _mm_nt(dh, lw["w_down"], out_dtype=BF16, name=f"{tag}_d_act")
        dw_down = _mm_tn(sv["act"], dh, 1, name=f"{tag}_dw_down")
        dug, duu, dwg, dwu, dbg, dbu = _ffn_mid_bwd(
            sv["up"], lw["ffn_conv_w"], lw["ffn_conv_b"], dact, name=f"{tag}_ffn_bwd"
        )
        dup = jnp.concatenate([dug, duu], axis=1)
        small["ffn_conv_w"][i] = jnp.concatenate([dwg, dwu], axis=1)
        small["ffn_conv_b"][i] = jnp.concatenate([dbg, dbu], axis=1)
        dw_up = _mm_tn(sv["b"], dup, N_CHIPS, name=f"{tag}_dw_up")
        db = _mm_nt(dup, lw["w_up"], out_dtype=F32, name=f"{tag}_d_b")
        dh_mid, small["ffn_norm"][i] = _rms_bwd(
            sv["h_mid"], norms["ffn_norm"][i : i + 1], db, dh, name=f"{tag}_ffn_norm"
        )
        do = _mm_nt(dh_mid, lw["w_out"], out_dtype=BF16, name=f"{tag}_d_o")
        dw_out = _mm_tn(sv["o"], dh_mid, 1, name=f"{tag}_dw_out")
        if kind == 0:
            dq, dk, dv, dbias = _na_bwd(sv["qkv"], sv["bias"], do, name=f"{tag}_na_bwd")
            dqkv = jnp.concatenate([dq, dk, dv], axis=1)
            small["na_rpb"][j] = _rpb_fold(dbias, name=f"{tag}_rpb_fold")
        elif kind == 1:
            dgb, dgc, dhh, small["sc_conv_w"] = _sc_mid_bwd(sv["qkv"], lw["sc_conv_w"], do, name=f"{tag}_sc_bwd")
            dqkv = jnp.concatenate([dgb, dgc, dhh], axis=1)
        else:
            gq, gk = norms["gqa_q_norm"][j : j + 1], norms["gqa_k_norm"][j : j + 1]
            dqn, dkn, dv = _gqa_bwd(sv["qkn"], sv["qkv"], do, heads, hkv, name=f"{tag}_gqa_bwd")
            dqk, small["gqa_q_norm"], small["gqa_k_norm"] = _gqa_prep_bwd(
                sv["qkv"], gq, gk, cos, sin, dqn, dkn, heads, hkv, name=f"{tag}_gqa_prep_bwd"
            )
            dqkv = jnp.concatenate([dqk, dv], axis=1)
        dw_in = _mm_tn(sv["a"], dqkv, N_CHIPS, name=f"{tag}_dw_in")
        da = _mm_nt(dqkv, lw["w_in"], out_dtype=F32, name=f"{tag}_d_a")
        dh, small["mix_norm"][i] = _rms_bwd(
            sv["h_in"], norms["mix_norm"][i : i + 1], da, dh_mid, name=f"{tag}_mix_norm"
        )
        rows_out, rows_down = dw_out.shape[1] // N_CHIPS, dw_down.shape[1] // N_CHIPS
        big[i] = send(i, [
            dw_in,
            dw_out.reshape(N_CHIPS, rows_out, d),
            dw_up,
            dw_down.reshape(N_CHIPS, rows_down, d),
        ])
    return loss, dh, big, small


def _pack(parts):
    flat = jnp.concatenate([p.reshape(-1).astype(F32) for p in parts])
    pad = (-flat.shape[0]) % (8 * LANES)
    return jnp.pad(flat, (0, pad)).reshape(-1, LANES)


def _unpack(buf, shapes):
    flat = buf.reshape(-1)
    out, at = [], 0
    for shp in shapes:
        size = 1
        for n in shp:
            size *= n
        out.append(flat[at : at + size].reshape(shp))
        at += size
    return out


def kernel(x, mix_norm, ffn_norm, final_norm, na_w_qkv, na_rpb, na_w_o, sc_w_in, sc_conv_w, sc_w_out, gqa_w_qkv, gqa_q_norm, gqa_k_norm, gqa_w_o, ffn_w_up, ffn_conv_w, ffn_conv_b, ffn_w_down, loss_target, m_mix_norm, m_ffn_norm, m_final_norm, m_na_w_qkv, m_na_rpb, m_na_w_o, m_sc_w_in, m_sc_conv_w, m_sc_w_out, m_gqa_w_qkv, m_gqa_q_norm, m_gqa_k_norm, m_gqa_w_o, m_ffn_w_up, m_ffn_conv_w, m_ffn_conv_b, m_ffn_w_down, v_mix_norm, v_ffn_norm, v_final_norm, v_na_w_qkv, v_na_rpb, v_na_w_o, v_sc_w_in, v_sc_conv_w, v_sc_w_out, v_gqa_w_qkv, v_gqa_q_norm, v_gqa_k_norm, v_gqa_w_o, v_ffn_w_up, v_ffn_conv_w, v_ffn_conv_b, v_ffn_w_down):
    depth, d = mix_norm.shape
    chip = 2 * lax.axis_index("x") + lax.axis_index("y")
    w_in_of = {0: na_w_qkv, 1: sc_w_in, 2: gqa_w_qkv}
    w_out_of = {0: na_w_o, 1: sc_w_out, 2: gqa_w_o}

    layers = []
    for i in range(depth):
        kind, j = _mixer_of(i)
        shards = [
            w_in_of[kind][j].astype(BF16),
            w_out_of[kind][j].astype(BF16),
            ffn_w_up[i].astype(BF16),
            ffn_w_down[i].astype(BF16),
            ffn_conv_w[i],
        ]
        split = [True, True, True, True, False]
        if kind == 1:
            shards.append(sc_conv_w[j])
            split.append(False)
        got = _gather_shards(shards, split, name=f"gather_l{i}", collective_id=1 + i)
        lw = {
            "w_in": got[0],
            "w_out": got[1].reshape(1, -1, d),
            "w_up": got[2],
            "w_down": got[3].reshape(1, -1, d),
            "ffn_conv_w": got[4],
            "ffn_conv_b": ffn_conv_b[i : i + 1],
        }
        if kind == 1:
            lw["sc_conv_w"] = got[5]
        layers.append(lw)

    norms = dict(mix_norm=mix_norm, ffn_norm=ffn_norm, final_norm=final_norm, na_rpb=na_rpb,
                 gqa_q_norm=gqa_q_norm, gqa_k_norm=gqa_k_norm)

    def send(i, pieces):
        return _scatter_pieces(pieces, name=f"scatter_l{i}", collective_id=1 + depth + i)

    loss, grad_x, big, small = _forward_backward(x[0], loss_target[0], norms, layers, send)
    own, sib = [b[0] for b in big], [b[1] for b in big]

    def update(w, m, v, slot, layer_ids, name):
        parts = [(own[i][slot], sib[i][slot]) for i in layer_ids]
        return _adamw_sharded(w, m, v, parts, name=name)

    by_kind = {k: [i for i in range(depth) if i % N_MIXERS == k] for k in range(N_MIXERS)}
    res = {
        "na_w_qkv": update(na_w_qkv, m_na_w_qkv, v_na_w_qkv, 0, by_kind[0], "adamw_na_w_qkv"),
        "na_w_o": update(na_w_o, m_na_w_o, v_na_w_o, 1, by_kind[0], "adamw_na_w_o"),
        "sc_w_in": update(sc_w_in, m_sc_w_in, v_sc_w_in, 0, by_kind[1], "adamw_sc_w_in"),
        "sc_w_out": update(sc_w_out, m_sc_w_out, v_sc_w_out, 1, by_kind[1], "adamw_sc_w_out"),
        "gqa_w_qkv": update(gqa_w_qkv, m_gqa_w_qkv, v_gqa_w_qkv, 0, by_kind[2], "adamw_gqa_w_qkv"),
        "gqa_w_o": update(gqa_w_o, m_gqa_w_o, v_gqa_w_o, 1, by_kind[2], "adamw_gqa_w_o"),
        "ffn_w_up": update(ffn_w_up, m_ffn_w_up, v_ffn_w_up, 2, list(range(depth)), "adamw_ffn_w_up"),
        "ffn_w_down": update(ffn_w_down, m_ffn_w_down, v_ffn_w_down, 3, list(range(depth)), "adamw_ffn_w_down"),
    }

    n_na = na_rpb.shape[0]
    rpb_flat = jnp.stack([small["na_rpb"][j] for j in range(n_na)])
    full_parts = [
        loss[:, :1],
        jnp.concatenate(small["mix_norm"], axis=0),
        jnp.concatenate(small["ffn_norm"], axis=0),
        small["final_norm"],
        rpb_flat,
        small["sc_conv_w"],
        small["gqa_q_norm"],
        small["gqa_k_norm"],
        jnp.stack(small["ffn_conv_w"]),
        jnp.concatenate(small["ffn_conv_b"], axis=0),
    ]
    summed = _unpack(_allreduce_small(_pack(full_parts), name="allreduce_small"), [p.shape for p in full_parts])
    loss_all, g_mix, g_ffn, g_final, g_rpb, g_sc_cw, g_gq, g_gk, g_ffn_cw, g_ffn_cb = summed
    g_rpb = g_rpb[:, :, : RPB_ROWS * RPB_COLS].reshape(na_rpb.shape)
    g_sc_cw = lax.dynamic_slice_in_dim(g_sc_cw, chip * sc_conv_w.shape[2], sc_conv_w.shape[2], axis=1)[None]
    g_ffn_cw = lax.dynamic_slice_in_dim(g_ffn_cw, chip * ffn_conv_w.shape[2], ffn_conv_w.shape[2], axis=2)
    small_names = ["mix_norm", "ffn_norm", "final_norm", "na_rpb", "sc_conv_w", "gqa_q_norm", "gqa_k_norm",
                   "ffn_conv_w", "ffn_conv_b"]
    small_g = [g_mix, g_ffn, g_final.reshape(final_norm.shape), g_rpb, g_sc_cw, g_gq, g_gk, g_ffn_cw, g_ffn_cb]
    small_w = [mix_norm, ffn_norm, final_norm, na_rpb, sc_conv_w, gqa_q_norm, gqa_k_norm, ffn_conv_w, ffn_conv_b]
    small_m = [m_mix_norm, m_ffn_norm, m_final_norm, m_na_rpb, m_sc_conv_w, m_gqa_q_norm, m_gqa_k_norm,
               m_ffn_conv_w, m_ffn_conv_b]
    small_v = [v_mix_norm, v_ffn_norm, v_final_norm, v_na_rpb, v_sc_conv_w, v_gqa_q_norm, v_gqa_k_norm,
               v_ffn_conv_w, v_ffn_conv_b]
    shapes = [w.shape for w in small_w]
    packed = _adamw_small(_pack(small_w), _pack(small_g), _pack(small_m), _pack(small_v), name="adamw_small")
    small_d, small_nm, small_nv = (_unpack(p, shapes) for p in packed)
    for n, g, dl, nm, nv in zip(small_names, small_g, small_d, small_nm, small_nv):
        res[n] = (g.reshape(dl.shape), dl, nm, nv)

    order = ["mix_norm", "ffn_norm", "final_norm", "na_w_qkv", "na_rpb", "na_w_o", "sc_w_in", "sc_conv_w",
             "sc_w_out", "gqa_w_qkv", "gqa_q_norm", "gqa_k_norm", "gqa_w_o", "ffn_w_up", "ffn_conv_w",
             "ffn_conv_b", "ffn_w_down"]
    outs = [loss_all.reshape(()), grad_x[None]]
    for part in range(4):
        outs.extend(res[n][part] for n in order)
    return tuple(outs)
```

```python
import functools

import jax
import jax.numpy as jnp
from jax import lax
from jax.experimental import pallas as pl
from jax.experimental.pallas import tpu as pltpu
from jax.experimental.pallas import tpu_sc as plsc

F32 = jnp.float32
BF16 = jnp.bfloat16
MESH = pl.DeviceIdType.MESH

N_CHIPS = 4
N_DEV = 8
N_MIXERS = 3
GRID_W = 64
HEAD_DIM = 128
EPS = 1e-6
NEG_INF = -1e30
NA_WIN_R = 8
NA_WIN_C = 16
GQA_GROUP = 4
ROPE_THETA = 10000.0
ADAM_LR = 0.001
ADAM_B1 = 0.9
ADAM_B2 = 0.999
ADAM_EPS = 1e-08
ADAM_WD = 0.01
ADAM_STEP = 10

LANES = 128
VMEM_LIMIT = 48 * 1024 * 1024
NT_DIMS = (((1,), (1,)), ((), ()))
TN_DIMS = (((0,), (0,)), ((), ()))


def _pick(n, cap, mult=LANES):
    best = None
    for t in range(mult, min(n, cap) + 1, mult):
        if n % t == 0:
            best = t
    return best if best is not None else n


def _params(*sem):
    return pltpu.CompilerParams(dimension_semantics=sem, vmem_limit_bytes=VMEM_LIMIT)


def _mm_nn(a, b, *, out_dtype, name, residual=None):
    m, k = a.shape
    nc, _, ncol = b.shape
    tm, tn, tk = _pick(m, 1024, 16), _pick(ncol, 1536), _pick(k, 512)
    per, nk = ncol // tn, k // tk

    def body(*refs):
        if residual is None:
            a_ref, b_ref, o_ref, acc = refs
        else:
            a_ref, b_ref, r_ref, o_ref, acc = refs
        kk = pl.program_id(2)

        @pl.when(kk == 0)
        def _():
            acc[...] = jnp.zeros_like(acc)

        acc[...] += jnp.dot(a_ref[...].astype(BF16), b_ref[...], preferred_element_type=F32)

        @pl.when(kk == nk - 1)
        def _():
            r = acc[...]
            if residual is not None:
                r = r + r_ref[...]
            o_ref[...] = r.astype(o_ref.dtype)

    in_specs = [
        pl.BlockSpec((tm, tk), lambda i, j, kk: (i, kk)),
        pl.BlockSpec((None, tk, tn), lambda i, j, kk: (j // per, kk, j % per)),
    ]
    ops = [a, b]
    if residual is not None:
        in_specs.append(pl.BlockSpec((tm, tn), lambda i, j, kk: (i, j)))
        ops.append(residual)
    return pl.pallas_call(
        body,
        name=name,
        grid=(m // tm, nc * per, nk),
        in_specs=in_specs,
        out_specs=pl.BlockSpec((tm, tn), lambda i, j, kk: (i, j)),
        out_shape=jax.ShapeDtypeStruct((m, nc * ncol), out_dtype),
        scratch_shapes=[pltpu.VMEM((tm, tn), F32)],
        compiler_params=_params("parallel", "parallel", "arbitrary"),
    )(*ops)


def _token(x):
    return x[(0,) * (x.ndim - 2)][:16, :LANES]


def _mm_nt(a, b, *, out_dtype, name, after=()):
    m, n = a.shape
    nc, k, ncol = b.shape
    tm, tko, tn = _pick(m, 1024, 16), _pick(k, 1024), _pick(ncol, 1536)
    per, nn = ncol // tn, n // tn

    def body(a_ref, b_ref, *rest):
        o_ref, acc = rest[len(after) :]
        s = pl.program_id(2)

        @pl.when(s == 0)
        def _():
            acc[...] = jnp.zeros_like(acc)

        acc[...] += lax.dot_general(a_ref[...].astype(BF16), b_ref[...], NT_DIMS, preferred_element_type=F32)

        @pl.when(s == nn - 1)
        def _():
            o_ref[...] = acc[...].astype(o_ref.dtype)

    return pl.pallas_call(
        body,
        name=name,
        grid=(m // tm, k // tko, nn),
        in_specs=[
            pl.BlockSpec((tm, tn), lambda i, j, s: (i, s)),
            pl.BlockSpec((None, tko, tn), lambda i, j, s: (s // per, j, s % per)),
        ]
        + [pl.BlockSpec(t.shape, lambda i, j, s: (0, 0)) for t in after],
        out_specs=pl.BlockSpec((tm, tko), lambda i, j, s: (i, j)),
        out_shape=jax.ShapeDtypeStruct((m, k), out_dtype),
        scratch_shapes=[pltpu.VMEM((tm, tko), F32)],
        compiler_params=_params("parallel", "parallel", "arbitrary"),
    )(a, b, *after)


def _mm_tn(a, g, nc, *, name):
    s, k = a.shape
    n = g.shape[1]
    ncol = n // nc
    ts, tko, tn = _pick(s, 512, 16), _pick(k, 1024), _pick(ncol, 1536)
    per, ns = ncol // tn, s // ts

    def body(a_ref, g_ref, o_ref, acc):
        t = pl.program_id(2)

        @pl.when(t == 0)
        def _():
            acc[...] = jnp.zeros_like(acc)

        acc[...] += lax.dot_general(
            a_ref[...].astype(BF16), g_ref[...].astype(BF16), TN_DIMS, preferred_element_type=F32
        )

        @pl.when(t == ns - 1)
        def _():
            o_ref[...] = acc[...].astype(o_ref.dtype)

    return pl.pallas_call(
        body,
        name=name,
        grid=(k // tko, nc * per, ns),
        in_specs=[
            pl.BlockSpec((ts, tko), lambda i, j, t: (t, i)),
            pl.BlockSpec((ts, tn), lambda i, j, t: (t, j)),
        ],
        out_specs=pl.BlockSpec((None, tko, tn), lambda i, j, t: (j // per, i, j % per)),
        out_shape=jax.ShapeDtypeStruct((nc, k, ncol), BF16),
        scratch_shapes=[pltpu.VMEM((tko, tn), F32)],
        compiler_params=_params("parallel", "parallel", "arbitrary"),
    )(a, g)


ROW_TILE = 256


def _rms_fwd(h, g, *, name):
    s, d = h.shape
    tr = _pick(s, ROW_TILE, 16)

    def body(h_ref, g_ref, o_ref):
        x = h_ref[...]
        r = lax.rsqrt(jnp.mean(x * x, axis=-1, keepdims=True) + EPS)
        o_ref[...] = (x * r * g_ref[...]).astype(o_ref.dtype)

    return pl.pallas_call(
        body,
        name=name,
        grid=(s // tr,),
        in_specs=[pl.BlockSpec((tr, d), lambda i: (i, 0)), pl.BlockSpec((1, d), lambda i: (0, 0))],
        out_specs=pl.BlockSpec((tr, d), lambda i: (i, 0)),
        out_shape=jax.ShapeDtypeStruct((s, d), BF16),
        compiler_params=_params("parallel"),
    )(h, g)


def _rms_bwd(h, g, dy, dres, *, name):
    s, d = h.shape
    tr = _pick(s, ROW_TILE, 16)

    def body(h_ref, g_ref, dy_ref, dres_ref, dh_ref, dg_ref):
        x = h_ref[...]
        r = lax.rsqrt(jnp.mean(x * x, axis=-1, keepdims=True) + EPS)
        xhat = x * r
        dyv = dy_ref[...].astype(F32)
        dyg = dyv * g_ref[...]
        dx = r * (dyg - xhat * jnp.mean(dyg * xhat, axis=-1, keepdims=True))
        dh_ref[...] = dres_ref[...] + dx
        part = jnp.sum(dyv * xhat, axis=0, keepdims=True)

        @pl.when(pl.program_id(0) == 0)
        def _():
            dg_ref[...] = part

        @pl.when(pl.program_id(0) != 0)
        def _():
            dg_ref[...] += part

    row = pl.BlockSpec((tr, d), lambda i: (i, 0))
    vec = pl.BlockSpec((1, d), lambda i: (0, 0))
    return pl.pallas_call(
        body,
        name=name,
        grid=(s // tr,),
        in_specs=[row, vec, row, row],
        out_specs=[row, vec],
        out_shape=[jax.ShapeDtypeStruct((s, d), F32), jax.ShapeDtypeStruct((1, d), F32)],
        compiler_params=_params("arbitrary"),
    )(h, g, dy, dres)


def _loss_head(h, g, target, *, name):
    s, d = h.shape
    tr = _pick(s, ROW_TILE, 16)

    def body(h_ref, g_ref, t_ref, dh_ref, dg_ref, loss_ref):
        x = h_ref[...]
        r = lax.rsqrt(jnp.mean(x * x, axis=-1, keepdims=True) + EPS)
        xhat = x * r
        gv = g_ref[...]
        err = xhat * gv - t_ref[...]
        dyv = err * (1.0 / d)
        dyg = dyv * gv
        dh_ref[...] = r * (dyg - xhat * jnp.mean(dyg * xhat, axis=-1, keepdims=True))
        part = jnp.sum(dyv * xhat, axis=0, keepdims=True)
        lpart = jnp.sum(jnp.sum(err * err, axis=-1, keepdims=True), axis=0, keepdims=True) * (0.5 / d)

        @pl.when(pl.program_id(0) == 0)
        def _():
            dg_ref[...] = part
            loss_ref[...] = jnp.broadcast_to(lpart, loss_ref.shape)

        @pl.when(pl.program_id(0) != 0)
        def _():
            dg_ref[...] += part
            loss_ref[...] += jnp.broadcast_to(lpart, loss_ref.shape)

    row = pl.BlockSpec((tr, d), lambda i: (i, 0))
    vec = pl.BlockSpec((1, d), lambda i: (0, 0))
    return pl.pallas_call(
        body,
        name=name,
        grid=(s // tr,),
        in_specs=[row, vec, row],
        out_specs=[row, vec, pl.BlockSpec((1, LANES), lambda i: (0, 0))],
        out_shape=[
            jax.ShapeDtypeStruct((s, d), F32),
            jax.ShapeDtypeStruct((1, d), F32),
            jax.ShapeDtypeStruct((1, LANES), F32),
        ],
        compiler_params=_params("arbitrary"),
    )(h, g, target)


def _shift_prev(x):
    row = lax.broadcasted_iota(jnp.int32, x.shape, 0)
    return jnp.where(row == 0, 0.0, pltpu.roll(x, 1, 0))


def _shift_next(x):
    n = x.shape[0]
    row = lax.broadcasted_iota(jnp.int32, x.shape, 0)
    return jnp.where(row == n - 1, 0.0, pltpu.roll(x, n - 1, 0))


def _conv3(x, w):
    xm, xp = _shift_prev(x), _shift_next(x)
    return xm * w[0:1] + x * w[1:2] + xp * w[2:3], xm, xp


def _conv3_t(d, w):
    return _shift_next(d) * w[0:1] + d * w[1:2] + _shift_prev(d) * w[2:3]


def _colsum(x):
    return jnp.sum(x, axis=0, keepdims=True)


def _ffn_mid_fwd(up, cw, cb, *, name):
    s, f2 = up.shape
    f = f2 // 2
    ncol = cw.shape[2]
    tc = _pick(ncol, 256)
    nt, per = f // tc, ncol // tc

    def body(ug_ref, uu_ref, wg_ref, wu_ref, bg_ref, bu_ref, o_ref):
        cg = _conv3(ug_ref[...].astype(F32), wg_ref[...])[0] + bg_ref[...]
        cu = _conv3(uu_ref[...].astype(F32), wu_ref[...])[0] + bu_ref[...]
        o_ref[...] = (cg * (1.0 / (1.0 + jnp.exp(-cg))) * cu).astype(o_ref.dtype)

    return pl.pallas_call(
        body,
        name=name,
        grid=(nt,),
        in_specs=[
            pl.BlockSpec((s, tc), lambda j: (0, j)),
            pl.BlockSpec((s, tc), lambda j: (0, nt + j)),
            pl.BlockSpec((None, 3, tc), lambda j: (j // per, 0, j % per)),
            pl.BlockSpec((None, 3, tc), lambda j: ((nt + j) // per, 0, (nt + j) % per)),
            pl.BlockSpec((1, tc), lambda j: (0, j)),
            pl.BlockSpec((1, tc), lambda j: (0, nt + j)),
        ],
        out_specs=pl.BlockSpec((s, tc), lambda j: (0, j)),
        out_shape=jax.ShapeDtypeStruct((s, f), BF16),
        compiler_params=_params("parallel"),
    )(up, up, cw, cw, cb, cb)


def _ffn_mid_bwd(up, cw, cb, dact, *, name):
    s, f2 = up.shape
    f = f2 // 2
    ncol = cw.shape[2]
    tc = _pick(ncol, 256)
    nt, per = f // tc, ncol // tc

    def body(ug_ref, uu_ref, wg_ref, wu_ref, bg_ref, bu_ref, da_ref, dug_ref, duu_ref, dwg_ref, dwu_ref, dbg_ref, dbu_ref):
        ug, uu = ug_ref[...].astype(F32), uu_ref[...].astype(F32)
        wg, wu = wg_ref[...], wu_ref[...]
        cg, ugm, ugp = _conv3(ug, wg)
        cu, uum, uup = _conv3(uu, wu)
        cg = cg + bg_ref[...]
        cu = cu + bu_ref[...]
        da = da_ref[...].astype(F32)
        sig = 1.0 / (1.0 + jnp.exp(-cg))
        dcu = da * (cg * sig)
        dcg = da * cu * (sig * (1.0 + cg * (1.0 - sig)))
        dug_ref[...] = _conv3_t(dcg, wg).astype(dug_ref.dtype)
        duu_ref[...] = _conv3_t(dcu, wu).astype(duu_ref.dtype)
        dwg_ref[0:1, :] = _colsum(dcg * ugm)
        dwg_ref[1:2, :] = _colsum(dcg * ug)
        dwg_ref[2:3, :] = _colsum(dcg * ugp)
        dwu_ref[0:1, :] = _colsum(dcu * uum)
        dwu_ref[1:2, :] = _colsum(dcu * uu)
        dwu_ref[2:3, :] = _colsum(dcu * uup)
        dbg_ref[...] = _colsum(dcg)
        dbu_ref[...] = _colsum(dcu)

    col = pl.BlockSpec((s, tc), lambda j: (0, j))
    w3 = pl.BlockSpec((3, tc), lambda j: (0, j))
    b1 = pl.BlockSpec((1, tc), lambda j: (0, j))
    return pl.pallas_call(
        body,
        name=name,
        grid=(nt,),
        in_specs=[
            col,
            pl.BlockSpec((s, tc), lambda j: (0, nt + j)),
            pl.BlockSpec((None, 3, tc), lambda j: (j // per, 0, j % per)),
            pl.BlockSpec((None, 3, tc), lambda j: ((nt + j) // per, 0, (nt + j) % per)),
            b1,
            pl.BlockSpec((1, tc), lambda j: (0, nt + j)),
            col,
        ],
        out_specs=[col, col, w3, w3, b1, b1],
        out_shape=[
            jax.ShapeDtypeStruct((s, f), BF16),
            jax.ShapeDtypeStruct((s, f), BF16),
            jax.ShapeDtypeStruct((3, f), F32),
            jax.ShapeDtypeStruct((3, f), F32),
            jax.ShapeDtypeStruct((1, f), F32),
            jax.ShapeDtypeStruct((1, f), F32),
        ],
        compiler_params=_params("parallel"),
    )(up, up, cw, cw, cb, cb, dact)


def _sc_mid_fwd(z, cw, *, name):
    s, d3 = z.shape
    d = d3 // 3
    ncol = cw.shape[2]
    tc = _pick(ncol, 256)
    nt, per = d // tc, ncol // tc

    def body(gb_ref, gc_ref, hh_ref, w_ref, o_ref):
        p = gc_ref[...].astype(F32) * hh_ref[...].astype(F32)
        o_ref[...] = (gb_ref[...].astype(F32) * _conv3(p, w_ref[...])[0]).astype(o_ref.dtype)

    return pl.pallas_call(
        body,
        name=name,
        grid=(nt,),
        in_specs=[
            pl.BlockSpec((s, tc), lambda j: (0, j)),
            pl.BlockSpec((s, tc), lambda j: (0, nt + j)),
            pl.BlockSpec((s, tc), lambda j: (0, 2 * nt + j)),
            pl.BlockSpec((None, 3, tc), lambda j: (j // per, 0, j % per)),
        ],
        out_specs=pl.BlockSpec((s, tc), lambda j: (0, j)),
        out_shape=jax.ShapeDtypeStruct((s, d), BF16),
        compiler_params=_params("parallel"),
    )(z, z, z, cw)


def _sc_mid_bwd(z, cw, dmid, *, name):
    s, d3 = z.shape
    d = d3 // 3
    ncol = cw.shape[2]
    tc = _pick(ncol, 256)
    nt, per = d // tc, ncol // tc

    def body(gb_ref, gc_ref, hh_ref, w_ref, dm_ref, dgb_ref, dgc_ref, dhh_ref, dw_ref):
        gb, gc, hh = gb_ref[...].astype(F32), gc_ref[...].astype(F32), hh_ref[...].astype(F32)
        w = w_ref[...]
        p = gc * hh
        cv, pm, pp = _conv3(p, w)
        dm = dm_ref[...].astype(F32)
        dgb_ref[...] = (dm * cv).astype(dgb_ref.dtype)
        dcv = dm * gb
        dp = _conv3_t(dcv, w)
        dgc_ref[...] = (dp * hh).astype(dgc_ref.dtype)
        dhh_ref[...] = (dp * gc).astype(dhh_ref.dtype)
        dw_ref[0:1, :] = _colsum(dcv * pm)
        dw_ref[1:2, :] = _colsum(dcv * p)
        dw_ref[2:3, :] = _colsum(dcv * pp)

    col = pl.BlockSpec((s, tc), lambda j: (0, j))
    return pl.pallas_call(
        body,
        name=name,
        grid=(nt,),
        in_specs=[
            col,
            pl.BlockSpec((s, tc), lambda j: (0, nt + j)),
            pl.BlockSpec((s, tc), lambda j: (0, 2 * nt + j)),
            pl.BlockSpec((None, 3, tc), lambda j: (j // per, 0, j % per)),
            col,
        ],
        out_specs=[col, col, col, pl.BlockSpec((3, tc), lambda j: (0, j))],
        out_shape=[jax.ShapeDtypeStruct((s, d), BF16)] * 3 + [jax.ShapeDtypeStruct((3, d), F32)],
        compiler_params=_params("parallel"),
    )(z, z, z, cw, dmid)


NA_KEYS = NA_WIN_R * GRID_W
NA_ROWS_PER_STEP = 8


def _na_row_start(r, rows):
    return jnp.clip(r - NA_WIN_R // 2, 0, rows - NA_WIN_R)


def _na_bias_slot(r, rows):
    return _na_row_start(r, rows) - r + NA_WIN_R - 1


def _na_base(rpb):
    h = rpb.shape[0]
    pos, neg = rpb[:, :, NA_WIN_C - 1:], rpb[:, :, : NA_WIN_C - 1]
    zeros = jnp.zeros((h, NA_WIN_R, GRID_W - 2 * NA_WIN_C + 1), F32)
    out = []
    for first in range(NA_WIN_R):
        p = pos[:, first : first + NA_WIN_R]
        n = jnp.roll(neg[:, first : first + NA_WIN_R], -1, axis=1)
        out.append(jnp.concatenate([p, zeros, n], axis=-1).reshape(h, 1, NA_KEYS))
    return jnp.stack(out, axis=1)


def _skew_right(x):
    return pltpu.roll(x, 0, 1, stride=1, stride_axis=0)


def _skew_left(x):
    n = x.shape[1]
    row = lax.broadcasted_iota(jnp.int32, x.shape, 0)
    for b in range(GRID_W.bit_length() - 1):
        x = jnp.where(((row >> b) & 1) == 1, pltpu.roll(x, n - (1 << b), 1), x)
    return x


def _na_bias(base, *, name):
    h = base.shape[0]

    def body(b_ref, o_ref):
        x = _skew_right(jnp.broadcast_to(b_ref[...], (GRID_W, NA_KEYS)))
        q = lax.broadcasted_iota(jnp.int32, x.shape, 0)
        kc = lax.broadcasted_iota(jnp.int32, x.shape, 1) % GRID_W
        start = jnp.clip(q - NA_WIN_C // 2, 0, GRID_W - NA_WIN_C)
        o_ref[...] = jnp.where((kc >= start) & (kc < start + NA_WIN_C), x, NEG_INF)

    return pl.pallas_call(
        body,
        name=name,
        grid=(h, NA_WIN_R),
        in_specs=[pl.BlockSpec((None, None, 1, NA_KEYS), lambda i, j: (i, j, 0, 0))],
        out_specs=pl.BlockSpec((None, None, GRID_W, NA_KEYS), lambda i, j: (i, j, 0, 0)),
        out_shape=jax.ShapeDtypeStruct((h, NA_WIN_R, GRID_W, NA_KEYS), F32),
        compiler_params=_params("parallel", "parallel"),
    )(base)


def _na_specs(s, heads, per):
    q = pl.BlockSpec((per * GRID_W, HEAD_DIM), lambda h, r: (r, h))
    k = pl.BlockSpec((s, HEAD_DIM), lambda h, r: (0, heads + h))
    v = pl.BlockSpec((s, HEAD_DIM), lambda h, r: (0, 2 * heads + h))
    bias = pl.BlockSpec((None, NA_WIN_R, GRID_W, NA_KEYS), lambda h, r: (h, 0, 0, 0))
    return q, k, v, bias


def _na_probs(q, k, bias):
    sc = lax.dot_general(q, k, NT_DIMS, preferred_element_type=F32) * (HEAD_DIM ** -0.5) + bias
    p = jnp.exp(sc - jnp.max(sc, axis=-1, keepdims=True))
    return p, jnp.sum(p, axis=-1, keepdims=True)


def _na_fwd(qkv, bias, *, name):
    s = qkv.shape[0]
    heads = qkv.shape[1] // (3 * HEAD_DIM)
    rows = s // GRID_W
    per = _pick(rows, NA_ROWS_PER_STEP, 1)

    def body(q_ref, k_ref, v_ref, b_ref, o_ref):
        for i in range(per):
            r = pl.program_id(1) * per + i
            win = pl.ds(pl.multiple_of(_na_row_start(r, rows) * GRID_W, GRID_W), NA_KEYS)
            mine = pl.ds(i * GRID_W, GRID_W)
            p, l = _na_probs(q_ref[mine, :], k_ref[win, :], b_ref[_na_bias_slot(r, rows)])
            o = jnp.dot(p.astype(BF16), v_ref[win, :], preferred_element_type=F32)
            o_ref[mine, :] = (o / l).astype(o_ref.dtype)

    q, k, v, b = _na_specs(s, heads, per)
    return pl.pallas_call(
        body,
        name=name,
        grid=(heads, rows // per),
        in_specs=[q, k, v, b],
        out_specs=q,
        out_shape=jax.ShapeDtypeStruct((s, heads * HEAD_DIM), BF16),
        compiler_params=_params("parallel", "arbitrary"),
    )(qkv, qkv, qkv, bias)


def _na_bwd(qkv, bias, dout, *, name):
    s = qkv.shape[0]
    heads = qkv.shape[1] // (3 * HEAD_DIM)
    rows = s // GRID_W
    per = _pick(rows, NA_ROWS_PER_STEP, 1)
    steps = rows // per
    scale = HEAD_DIM ** -0.5

    def body(q_ref, k_ref, v_ref, b_ref, do_ref, dq_ref, dk_ref, dv_ref, db_ref, dk_acc, dv_acc):
        step = pl.program_id(1)

        @pl.when(step == 0)
        def _():
            dk_acc[...] = jnp.zeros_like(dk_acc)
            dv_acc[...] = jnp.zeros_like(dv_acc)
            db_ref[...] = jnp.zeros_like(db_ref)

        for i in range(per):
            r = step * per + i
            win = pl.ds(pl.multiple_of(_na_row_start(r, rows) * GRID_W, GRID_W), NA_KEYS)
            mine = pl.ds(i * GRID_W, GRID_W)
            slot = _na_bias_slot(r, rows)
            q, k, v, do = q_ref[mine, :], k_ref[win, :], v_ref[win, :], do_ref[mine, :]
            p, l = _na_probs(q, k, b_ref[slot])
            pn = p / l
            dp = lax.dot_general(do, v, NT_DIMS, preferred_element_type=F32)
            ds = pn * (dp - jnp.sum(pn * dp, axis=-1, keepdims=True))
            dsb = ds.astype(BF16)
            dq_ref[mine, :] = (jnp.dot(dsb, k, preferred_element_type=F32) * scale).astype(dq_ref.dtype)
            dk_acc[win, :] += lax.dot_general(dsb, q, TN_DIMS, preferred_element_type=F32) * scale
            dv_acc[win, :] += lax.dot_general(pn.astype(BF16), do, TN_DIMS, preferred_element_type=F32)
            db_ref[slot] += ds

        @pl.when(step == steps - 1)
        def _():
            dk_ref[...] = dk_acc[...].astype(dk_ref.dtype)
            dv_ref[...] = dv_acc[...].astype(dv_ref.dtype)

    q, k, v, b = _na_specs(s, heads, per)
    kv_out = pl.BlockSpec((s, HEAD_DIM), lambda h, r: (0, h))
    shape = jax.ShapeDtypeStruct((s, heads * HEAD_DIM), BF16)
    return pl.pallas_call(
        body,
        name=name,
        grid=(heads, steps),
        in_specs=[q, k, v, b, q],
        out_specs=[q, kv_out, kv_out, b],
        out_shape=[shape, shape, shape, jax.ShapeDtypeStruct((heads, NA_WIN_R, GRID_W, NA_KEYS), F32)],
        scratch_shapes=[pltpu.VMEM((s, HEAD_DIM), F32), pltpu.VMEM((s, HEAD_DIM), F32)],
        compiler_params=_params("parallel", "arbitrary"),
    )(qkv, qkv, qkv, bias, dout)


RPB_ROWS = 2 * NA_WIN_R - 1
RPB_COLS = 2 * NA_WIN_C - 1
RPB_PAD = 512


def _rpb_fold_matrix():
    idx = jnp.arange(NA_WIN_R * NA_KEYS, dtype=jnp.int32)
    first, i, kc = idx // NA_KEYS, (idx // GRID_W) % NA_WIN_R, idx % GRID_W
    pos, neg = kc < NA_WIN_C, kc >= GRID_W - NA_WIN_C + 1
    dr = jnp.where(pos, first + i, first + (i + 1) % NA_WIN_R)
    dc = jnp.where(pos, kc + NA_WIN_C - 1, kc - (GRID_W - NA_WIN_C + 1))
    target = jnp.where(pos | neg, dr * RPB_COLS + dc, -1)
    return (target[:, None] == jnp.arange(RPB_PAD, dtype=jnp.int32)[None, :]).astype(F32)


def _rpb_fold(dbias, *, name):
    h = dbias.shape[0]

    def skew_body(g_ref, o_ref):
        o_ref[...] = _colsum(_skew_left(g_ref[...]))

    skewed = pl.pallas_call(
        skew_body,
        name=name + "_skew",
        grid=(h, NA_WIN_R),
        in_specs=[pl.BlockSpec((None, None, GRID_W, NA_KEYS), lambda i, j: (i, j, 0, 0))],
        out_specs=pl.BlockSpec((None, None, 1, NA_KEYS), lambda i, j: (i, j, 0, 0)),
        out_shape=jax.ShapeDtypeStruct((h, NA_WIN_R, 1, NA_KEYS), F32),
        compiler_params=_params("parallel", "parallel"),
    )(dbias)

    def fold_body(g_ref, m_ref, o_ref):
        o_ref[...] = jnp.dot(g_ref[...], m_ref[...], preferred_element_type=F32, precision=lax.Precision.HIGHEST)

    return pl.pallas_call(
        fold_body,
        name=name,
        out_shape=jax.ShapeDtypeStruct((h, RPB_PAD), F32),
        compiler_params=pltpu.CompilerParams(vmem_limit_bytes=VMEM_LIMIT),
    )(skewed.reshape(h, NA_WIN_R * NA_KEYS), _rpb_fold_matrix())


GQA_Q_TILE = 256


def _rope_tables(s):
    t = jnp.arange(s)
    row = (t // GRID_W).astype(F32)[:, None]
    col = (t % GRID_W).astype(F32)[:, None]
    half = HEAD_DIM // 2
    inv = ROPE_THETA ** (-jnp.arange(0, half, 2, dtype=F32) / half)
    ang = jnp.concatenate([row * inv, row * inv, col * inv, col * inv], axis=-1)
    return jnp.cos(ang), jnp.sin(ang)


def _rot_half(y):
    quarter = HEAD_DIM // 4
    lane = lax.broadcasted_iota(jnp.int32, y.shape, 1)
    low = (lane % (2 * quarter)) < quarter
    return jnp.where(low, -pltpu.roll(y, HEAD_DIM - quarter, 1), pltpu.roll(y, quarter, 1))


def _gqa_prep_fwd(qkv, gq, gk, cos, sin, hq, hkv, *, name):
    s = qkv.shape[0]

    def body(x_ref, gq_ref, gk_ref, cos_ref, sin_ref, o_ref):
        isq = pl.program_id(0) < hq
        x = x_ref[...].astype(F32)
        g = jnp.where(isq, gq_ref[...], gk_ref[...])
        y = x * lax.rsqrt(jnp.mean(x * x, axis=-1, keepdims=True) + EPS) * g
        z = y * cos_ref[...] + _rot_half(y) * sin_ref[...]
        o_ref[...] = (z * jnp.where(isq, HEAD_DIM ** -0.5, 1.0)).astype(o_ref.dtype)

    head = pl.BlockSpec((s, HEAD_DIM), lambda h: (0, h))
    vec = pl.BlockSpec((1, HEAD_DIM), lambda h: (0, 0))
    tab = pl.BlockSpec((s, HEAD_DIM), lambda h: (0, 0))
    return pl.pallas_call(
        body,
        name=name,
        grid=(hq + hkv,),
        in_specs=[head, vec, vec, tab, tab],
        out_specs=head,
        out_shape=jax.ShapeDtypeStruct((s, (hq + hkv) * HEAD_DIM), BF16),
        compiler_params=_params("parallel"),
    )(qkv, gq, gk, cos, sin)


def _gqa_prep_bwd(qkv, gq, gk, cos, sin, dqn, dkn, hq, hkv, *, name):
    s = qkv.shape[0]

    def body(x_ref, gq_ref, gk_ref, cos_ref, sin_ref, dq_ref, dk_ref, dx_ref, dgq_ref, dgk_ref):
        hh = pl.program_id(0)
        isq = hh < hq
        x = x_ref[...].astype(F32)
        g = jnp.where(isq, gq_ref[...], gk_ref[...])
        r = lax.rsqrt(jnp.mean(x * x, axis=-1, keepdims=True) + EPS)
        xhat = x * r
        dz = jnp.where(isq, dq_ref[...].astype(F32) * (HEAD_DIM ** -0.5), dk_ref[...].astype(F32))
        dy = dz * cos_ref[...] - _rot_half(dz * sin_ref[...])
        dyg = dy * g
        dx_ref[...] = (r * (dyg - xhat * jnp.mean(dyg * xhat, axis=-1, keepdims=True))).astype(dx_ref.dtype)
        part = _colsum(dy * xhat)

        @pl.when(hh == 0)
        def _():
            dgq_ref[...] = jnp.zeros_like(dgq_ref)
            dgk_ref[...] = jnp.zeros_like(dgk_ref)

        @pl.when(isq)
        def _():
            dgq_ref[...] += part

        @pl.when(jnp.logical_not(isq))
        def _():
            dgk_ref[...] += part

    head = pl.BlockSpec((s, HEAD_DIM), lambda h: (0, h))
    vec = pl.BlockSpec((1, HEAD_DIM), lambda h: (0, 0))
    tab = pl.BlockSpec((s, HEAD_DIM), lambda h: (0, 0))
    return pl.pallas_call(
        body,
        name=name,
        grid=(hq + hkv,),
        in_specs=[
            head,
            vec,
            vec,
            tab,
            tab,
            pl.BlockSpec((s, HEAD_DIM), lambda h: (0, jnp.minimum(h, hq - 1))),
            pl.BlockSpec((s, HEAD_DIM), lambda h: (0, jnp.maximum(h - hq, 0))),
        ],
        out_specs=[head, vec, vec],
        out_shape=[
            jax.ShapeDtypeStruct((s, (hq + hkv) * HEAD_DIM), BF16),
            jax.ShapeDtypeStruct((1, HEAD_DIM), F32),
            jax.ShapeDtypeStruct((1, HEAD_DIM), F32),
        ],
        compiler_params=_params("arbitrary"),
    )(qkv, gq, gk, cos, sin, dqn, dkn)


def _gqa_fwd(qkn, qkv, hq, hkv, *, name):
    s = qkv.shape[0]
    tq = _pick(s, GQA_Q_TILE, 16)

    def body(q_ref, k_ref, v_ref, o_ref):
        sc = lax.dot_general(q_ref[...], k_ref[...], NT_DIMS, preferred_element_type=F32)
        p = jnp.exp(sc - jnp.max(sc, axis=-1, keepdims=True))
        l = jnp.sum(p, axis=-1, keepdims=True)
        o_ref[...] = (jnp.dot(p.astype(BF16), v_ref[...], preferred_element_type=F32) / l).astype(o_ref.dtype)

    q = pl.BlockSpec((tq, HEAD_DIM), lambda h, i: (i, h))
    return pl.pallas_call(
        body,
        name=name,
        grid=(hq, s // tq),
        in_specs=[
            q,
            pl.BlockSpec((s, HEAD_DIM), lambda h, i: (0, hq + h // GQA_GROUP)),
            pl.BlockSpec((s, HEAD_DIM), lambda h, i: (0, hq + hkv + h // GQA_GROUP)),
        ],
        out_specs=q,
        out_shape=jax.ShapeDtypeStruct((s, hq * HEAD_DIM), BF16),
        compiler_params=_params("parallel", "parallel"),
    )(qkn, qkn, qkv)


def _gqa_bwd(qkn, qkv, dout, hq, hkv, *, name):
    s = qkv.shape[0]
    tq = _pick(s, GQA_Q_TILE, 16)
    nq = s // tq

    def body(q_ref, k_ref, v_ref, do_ref, dq_ref, dk_ref, dv_ref, dk_acc, dv_acc):
        g, i = pl.program_id(1), pl.program_id(2)

        @pl.when((g == 0) & (i == 0))
        def _():
            dk_acc[...] = jnp.zeros_like(dk_acc)
            dv_acc[...] = jnp.zeros_like(dv_acc)

        q, k, v, do = q_ref[...], k_ref[...], v_ref[...], do_ref[...]
        sc = lax.dot_general(q, k, NT_DIMS, preferred_element_type=F32)
        p = jnp.exp(sc - jnp.max(sc, axis=-1, keepdims=True))
        pn = p / jnp.sum(p, axis=-1, keepdims=True)
        dp = lax.dot_general(do, v, NT_DIMS, preferred_element_type=F32)
        dsb = (pn * (dp - jnp.sum(pn * dp, axis=-1, keepdims=True))).astype(BF16)
        dq_ref[...] = jnp.dot(dsb, k, preferred_element_type=F32).astype(dq_ref.dtype)
        dk_acc[...] += lax.dot_general(dsb, q, TN_DIMS, preferred_element_type=F32)
        dv_acc[...] += lax.dot_general(pn.astype(BF16), do, TN_DIMS, preferred_element_type=F32)

        @pl.when((g == GQA_GROUP - 1) & (i == nq - 1))
        def _():
            dk_ref[...] = dk_acc[...].astype(dk_ref.dtype)
            dv_ref[...] = dv_acc[...].astype(dv_ref.dtype)

    q = pl.BlockSpec((tq, HEAD_DIM), lambda kv, g, i: (i, kv * GQA_GROUP + g))
    kv_out = pl.BlockSpec((s, HEAD_DIM), lambda kv, g, i: (0, kv))
    return pl.pallas_call(
        body,
        name=name,
        grid=(hkv, GQA_GROUP, nq),
        in_specs=[
            q,
            pl.BlockSpec((s, HEAD_DIM), lambda kv, g, i: (0, hq + kv)),
            pl.BlockSpec((s, HEAD_DIM), lambda kv, g, i: (0, hq + hkv + kv)),
            q,
        ],
        out_specs=[q, kv_out, kv_out],
        out_shape=[
            jax.ShapeDtypeStruct((s, hq * HEAD_DIM), BF16),
            jax.ShapeDtypeStruct((s, hkv * HEAD_DIM), BF16),
            jax.ShapeDtypeStruct((s, hkv * HEAD_DIM), BF16),
        ],
        scratch_shapes=[pltpu.VMEM((s, HEAD_DIM), F32), pltpu.VMEM((s, HEAD_DIM), F32)],
        compiler_params=_params("parallel", "arbitrary", "arbitrary"),
    )(qkn, qkn, qkv, dout)


ADAM_ROWS = 64


def _adam_update(w, g, m, v):
    m = ADAM_B1 * m + (1.0 - ADAM_B1) * g
    v = ADAM_B2 * v + (1.0 - ADAM_B2) * (g * g)
    m_hat = m / (1.0 - ADAM_B1 ** ADAM_STEP)
    v_hat = v / (1.0 - ADAM_B2 ** ADAM_STEP)
    return -ADAM_LR * (m_hat / (jnp.sqrt(v_hat) + ADAM_EPS) + ADAM_WD * w), m, v


def _adamw_slab(w, m, v, own, sib, slab, prev, *, name):
    _, rows, cols = w.shape
    tr = _pick(rows, ADAM_ROWS, 16)

    def body(w_ref, m_ref, v_ref, own_ref, sib_ref, *rest):
        g_ref, d_ref, nm_ref, nv_ref = rest[-4:]
        g = own_ref[0].astype(F32) + sib_ref[0].astype(F32)
        for q in range(1, N_CHIPS):
            g = g + (own_ref[q].astype(F32) + sib_ref[q].astype(F32))
        g_ref[...] = g
        d_ref[...], nm_ref[...], nv_ref[...] = _adam_update(w_ref[...], g, m_ref[...], v_ref[...])

    one = pl.BlockSpec((None, tr, cols), lambda i: (slab, i, 0))
    piece = pl.BlockSpec((N_CHIPS, tr, cols), lambda i: (0, i, 0))
    carried = [] if prev is None else list(prev)
    shape = jax.ShapeDtypeStruct(w.shape, F32)
    return pl.pallas_call(
        body,
        name=name,
        grid=(rows // tr,),
        in_specs=[one] * 3 + [piece] * 2 + [pl.BlockSpec(memory_space=pl.ANY)] * len(carried),
        out_specs=[one] * 4,
        out_shape=[shape] * 4,
        input_output_aliases={5 + i: i for i in range(len(carried))},
        compiler_params=_params("parallel"),
    )(w, m, v, own, sib, *carried)


def _adamw_small(w, g, m, v, *, name):
    def body(w_ref, g_ref, m_ref, v_ref, d_ref, nm_ref, nv_ref):
        d_ref[...], nm_ref[...], nv_ref[...] = _adam_update(w_ref[...], g_ref[...], m_ref[...], v_ref[...])

    shape = jax.ShapeDtypeStruct(w.shape, F32)
    return pl.pallas_call(
        body,
        name=name,
        out_shape=[shape] * 3,
        compiler_params=pltpu.CompilerParams(vmem_limit_bytes=VMEM_LIMIT),
    )(w, g, m, v)


def _position():
    x, y, c = lax.axis_index("x"), lax.axis_index("y"), lax.axis_index("c")
    return x, y, c, 2 * x + y


def _chip_device(chip, c):
    return (chip >> 1, chip & 1, c)


def _handshake(peers):
    barrier = pltpu.get_barrier_semaphore()
    for peer in peers:
        pl.semaphore_signal(barrier, inc=1, device_id=peer, device_id_type=MESH)
    pl.semaphore_wait(barrier, len(peers))


def _gather_shards(shards, split, *, name, collective_id):
    n = len(shards)

    def body(*refs):
        ins, outs = refs[:n], refs[n : 2 * n]
        local_sem, send_sem, recv_sem, pass_send, pass_recv = refs[2 * n :]
        x, y, c, k = _position()
        sibling = (x, y, 1 - c)
        _handshake([sibling] + [_chip_device(k ^ j, c) for j in range(1, N_CHIPS)])

        def rows(a, core):
            r = shards[a].shape[0]
            return pl.ds(core * (r // 2), r // 2) if split[a] else pl.ds(0, r)

        def over_ici(a, j, src_chip, to):
            return pltpu.make_async_remote_copy(
                src_ref=ins[a].at[rows(a, c)],
                dst_ref=outs[a].at[src_chip, rows(a, c)],
                send_sem=send_sem.at[a, j],
                recv_sem=recv_sem.at[a, j],
                device_id=to,
                device_id_type=MESH,
            )

        def to_sibling(a, j, chip, core):
            part = outs[a].at[chip, rows(a, core)]
            return pltpu.make_async_remote_copy(
                src_ref=part,
                dst_ref=part,
                send_sem=pass_send.at[a, j],
                recv_sem=pass_recv.at[a, j],
                device_id=sibling,
                device_id_type=MESH,
            )

        mine = [pltpu.make_async_copy(ins[a], outs[a].at[k], local_sem.at[a]) for a in range(n)]
        for cp in mine:
            cp.start()
        sent = []
        for j in range(N_CHIPS - 1):
            other = k ^ (j + 1)
            for a in range(n):
                cp = over_ici(a, j, k, _chip_device(other, c))
                cp.start()
                sent.append(cp)
        for j in range(N_CHIPS - 1):
            other = k ^ (j + 1)
            for a in range(n):
                over_ici(a, j, other, sibling).wait_recv()
                if split[a]:
                    cp = to_sibling(a, j, other, c)
                    cp.start()
                    sent.append(cp)
        for j in range(N_CHIPS - 1):
            other = k ^ (j + 1)
            for a in range(n):
                if split[a]:
                    to_sibling(a, j, other, 1 - c).wait_recv()
        for cp in sent:
            cp.wait_send()
        for cp in mine:
            cp.wait()

    return pl.kernel(
        body,
        name=name,
        out_type=[jax.ShapeDtypeStruct((N_CHIPS,) + a.shape, a.dtype) for a in shards],
        mesh=plsc.ScalarSubcoreMesh(axis_name="sequencer", num_cores=1),
        scratch_types=[
            pltpu.SemaphoreType.DMA((n,)),
            pltpu.SemaphoreType.DMA((n, N_CHIPS - 1)),
            pltpu.SemaphoreType.DMA((n, N_CHIPS - 1)),
            pltpu.SemaphoreType.DMA((n, N_CHIPS - 1)),
            pltpu.SemaphoreType.DMA((n, N_CHIPS - 1)),
        ],
        compiler_params=pltpu.CompilerParams(collective_id=collective_id),
    )(*shards)


def _scatter_pieces(pieces, *, name, collective_id):
    n = len(pieces)

    def body(*refs):
        ins, own, sib = refs[:n], refs[n : 2 * n], refs[2 * n : 3 * n]
        local_sem, send_sem, recv_sem, pass_send, pass_recv = refs[3 * n :]
        x, y, c, k = _position()
        sibling = (x, y, 1 - c)
        _handshake([sibling] + [_chip_device(k ^ j, c) for j in range(1, N_CHIPS)])

        def over_ici(a, j, piece, slot, to):
            return pltpu.make_async_remote_copy(
                src_ref=ins[a].at[piece],
                dst_ref=own[a].at[slot],
                send_sem=send_sem.at[a, j],
                recv_sem=recv_sem.at[a, j],
                device_id=to,
                device_id_type=MESH,
            )

        def to_sibling(a, j, slot):
            return pltpu.make_async_remote_copy(
                src_ref=own[a].at[slot],
                dst_ref=sib[a].at[slot],
                send_sem=pass_send.at[a, j],
                recv_sem=pass_recv.at[a, j],
                device_id=sibling,
                device_id_type=MESH,
            )

        mine = [pltpu.make_async_copy(ins[a].at[k], own[a].at[k], local_sem.at[a]) for a in range(n)]
        for cp in mine:
            cp.start()
        sent = []
        for j in range(N_CHIPS - 1):
            other = k ^ (j + 1)
            for a in range(n):
                cp = over_ici(a, j, other, k, _chip_device(other, c))
                cp.start()
                sent.append(cp)
        for a in range(n):
            mine[a].wait()
            cp = to_sibling(a, N_CHIPS - 1, k)
            cp.start()
            sent.append(cp)
        for j in range(N_CHIPS - 1):
            other = k ^ (j + 1)
            for a in range(n):
                over_ici(a, j, other, other, sibling).wait_recv()
                cp = to_sibling(a, j, other)
                cp.start()
                sent.append(cp)
        for j in range(N_CHIPS):
            for a in range(n):
                to_sibling(a, j, k).wait_recv()
        for cp in sent:
            cp.wait_send()

    shapes = [jax.ShapeDtypeStruct(a.shape, a.dtype) for a in pieces]
    outs = pl.kernel(
        body,
        name=name,
        out_type=shapes + shapes,
        mesh=plsc.ScalarSubcoreMesh(axis_name="sequencer", num_cores=1),
        scratch_types=[
            pltpu.SemaphoreType.DMA((n,)),
            pltpu.SemaphoreType.DMA((n, N_CHIPS - 1)),
            pltpu.SemaphoreType.DMA((n, N_CHIPS - 1)),
            pltpu.SemaphoreType.DMA((n, N_CHIPS)),
            pltpu.SemaphoreType.DMA((n, N_CHIPS)),
        ],
        compiler_params=pltpu.CompilerParams(collective_id=collective_id),
    )(*pieces)
    return outs[:n], outs[n:]


def _allreduce_small(buf, *, name):
    def body(x_ref, o_ref, slots, send_sem, recv_sem):
        x, y, c, _ = _position()
        me = 4 * x + 2 * y + c
        slots[me] = x_ref[...]

        def copy(d, slot):
            peer = me ^ d
            return pltpu.make_async_remote_copy(
                src_ref=x_ref,
                dst_ref=slots.at[slot],
                send_sem=send_sem.at[d - 1],
                recv_sem=recv_sem.at[d - 1],
                device_id=(peer >> 2, (peer >> 1) & 1, peer & 1),
                device_id_type=MESH,
            )

        sent = [copy(d, me) for d in range(1, N_DEV)]
        for cp in sent:
            cp.start()
        for d in range(1, N_DEV):
            copy(d, me ^ d).wait_recv()
        for cp in sent:
            cp.wait_send()
        acc = slots[0]
        for s in range(1, N_DEV):
            acc = acc + slots[s]
        o_ref[...] = acc

    return pl.pallas_call(
        body,
        name=name,
        in_specs=[pl.BlockSpec(memory_space=pltpu.VMEM)],
        out_specs=pl.BlockSpec(memory_space=pltpu.VMEM),
        out_shape=jax.ShapeDtypeStruct(buf.shape, F32),
        scratch_shapes=[
            pltpu.VMEM((N_DEV,) + buf.shape, F32),
            pltpu.SemaphoreType.DMA((N_DEV - 1,)),
            pltpu.SemaphoreType.DMA((N_DEV - 1,)),
        ],
        compiler_params=pltpu.CompilerParams(vmem_limit_bytes=VMEM_LIMIT),
    )(buf)


def _mixer_of(i):
    return i % N_MIXERS, i // N_MIXERS


def _forward_backward(x, target, norms, layers, send=lambda i, part, pieces: pieces):
    s, d = x.shape
    depth = len(layers)
    heads = d // HEAD_DIM
    hkv = heads // GQA_GROUP
    cos, sin = _rope_tables(s)
    saved = []
    h = x
    for i, lw in enumerate(layers):
        kind, j = _mixer_of(i)
        tag = f"l{i}"
        sv = {"h_in": h}
        a = _rms_fwd(h, norms["mix_norm"][i : i + 1], name=f"{tag}_mix_norm")
        qkv = _mm_nn(a, lw["w_in"], out_dtype=BF16, name=f"{tag}_w_in")
        if kind == 0:
            bias = _na_bias(_na_base(norms["na_rpb"][j]), name=f"{tag}_na_bias")
            o = _na_fwd(qkv, bias, name=f"{tag}_na_fwd")
            sv["bias"] = bias
        elif kind == 1:
            o = _sc_mid_fwd(qkv, lw["sc_conv_w"], name=f"{tag}_sc_fwd")
        else:
            gq, gk = norms["gqa_q_norm"][j : j + 1], norms["gqa_k_norm"][j : j + 1]
            qkn = _gqa_prep_fwd(qkv, gq, gk, cos, sin, heads, hkv, name=f"{tag}_gqa_prep")
            o = _gqa_fwd(qkn, qkv, heads, hkv, name=f"{tag}_gqa_fwd")
            sv["qkn"] = qkn
        h_mid = _mm_nn(o, lw["w_out"], out_dtype=F32, residual=h, name=f"{tag}_w_out")
        b = _rms_fwd(h_mid, norms["ffn_norm"][i : i + 1], name=f"{tag}_ffn_norm")
        up = _mm_nn(b, lw["w_up"], out_dtype=BF16, name=f"{tag}_w_up")
        act = _ffn_mid_fwd(up, lw["ffn_conv_w"], lw["ffn_conv_b"], name=f"{tag}_ffn_fwd")
        h = _mm_nn(act, lw["w_down"], out_dtype=F32, residual=h_mid, name=f"{tag}_w_down")
        sv.update(a=a, qkv=qkv, o=o, h_mid=h_mid, b=b, up=up, act=act)
        saved.append(sv)

    dh, d_final, loss = _loss_head(h, norms["final_norm"][None], target, name="loss_head")

    big = [None] * depth
    small = {"final_norm": d_final, "mix_norm": [None] * depth, "ffn_norm": [None] * depth,
             "ffn_conv_w": [None] * depth, "ffn_conv_b": [None] * depth, "na_rpb": {}}
    after = ()
    for i in reversed(range(depth)):
        kind, j = _mixer_of(i)
        tag = f"l{i}b"
        lw, sv = layers[i], saved[i]
        dact = _mm_nt(dh, lw["w_down"], out_dtype=BF16, name=f"{tag}_d_act", after=after)
        dw_down = _mm_tn(sv["act"], dh, 1, name=f"{tag}_dw_down")
        dug, duu, dwg, dwu, dbg, dbu = _ffn_mid_bwd(
            sv["up"], lw["ffn_conv_w"], lw["ffn_conv_b"], dact, name=f"{tag}_ffn_bwd"
        )
        dup = jnp.concatenate([dug, duu], axis=1)
        small["ffn_conv_w"][i] = jnp.concatenate([dwg, dwu], axis=1)
        small["ffn_conv_b"][i] = jnp.concatenate([dbg, dbu], axis=1)
        dw_up = _mm_tn(sv["b"], dup, N_CHIPS, name=f"{tag}_dw_up")
        sent_ffn = send(i, "ffn", [dw_up, dw_down.reshape(N_CHIPS, dw_down.shape[1] // N_CHIPS, d)])
        db = _mm_nt(dup, lw["w_up"], out_dtype=F32, name=f"{tag}_d_b")
        dh_mid, small["ffn_norm"][i] = _rms_bwd(
            sv["h_mid"], norms["ffn_norm"][i : i + 1], db, dh, name=f"{tag}_ffn_norm"
        )
        do = _mm_nt(dh_mid, lw["w_out"], out_dtype=BF16, name=f"{tag}_d_o", after=(_token(dw_up), _token(dw_down)))
        dw_out = _mm_tn(sv["o"], dh_mid, 1, name=f"{tag}_dw_out")
        if kind == 0:
            dq, dk, dv, dbias = _na_bwd(sv["qkv"], sv["bias"], do, name=f"{tag}_na_bwd")
            dqkv = jnp.concatenate([dq, dk, dv], axis=1)
            small["na_rpb"][j] = _rpb_fold(dbias, name=f"{tag}_rpb_fold")
        elif kind == 1:
            dgb, dgc, dhh, small["sc_conv_w"] = _sc_mid_bwd(sv["qkv"], lw["sc_conv_w"], do, name=f"{tag}_sc_bwd")
            dqkv = jnp.concatenate([dgb, dgc, dhh], axis=1)
        else:
            gq, gk = norms["gqa_q_norm"][j : j + 1], norms["gqa_k_norm"][j : j + 1]
            dqn, dkn, dv = _gqa_bwd(sv["qkn"], sv["qkv"], do, heads, hkv, name=f"{tag}_gqa_bwd")
            dqk, small["gqa_q_norm"], small["gqa_k_norm"] = _gqa_prep_bwd(
                sv["qkv"], gq, gk, cos, sin, dqn, dkn, heads, hkv, name=f"{tag}_gqa_prep_bwd"
            )
            dqkv = jnp.concatenate([dqk, dv], axis=1)
        dw_in = _mm_tn(sv["a"], dqkv, N_CHIPS, name=f"{tag}_dw_in")
        da = _mm_nt(dqkv, lw["w_in"], out_dtype=F32, name=f"{tag}_d_a")
        dh, small["mix_norm"][i] = _rms_bwd(
            sv["h_in"], norms["mix_norm"][i : i + 1], da, dh_mid, name=f"{tag}_mix_norm"
        )
        sent_mix = send(i, "mix", [dw_in, dw_out.reshape(N_CHIPS, dw_out.shape[1] // N_CHIPS, d)])
        after = (_token(dw_in), _token(dw_out))
        big[i] = {"mix": sent_mix, "ffn": sent_ffn}
    return loss, dh, big, small


def _pack(parts):
    flat = jnp.concatenate([p.reshape(-1).astype(F32) for p in parts])
    pad = (-flat.shape[0]) % (8 * LANES)
    return jnp.pad(flat, (0, pad)).reshape(-1, LANES)


def _unpack(buf, shapes):
    flat = buf.reshape(-1)
    out, at = [], 0
    for shp in shapes:
        size = 1
        for n in shp:
            size *= n
        out.append(flat[at : at + size].reshape(shp))
        at += size
    return out


def kernel(x, mix_norm, ffn_norm, final_norm, na_w_qkv, na_rpb, na_w_o, sc_w_in, sc_conv_w, sc_w_out, gqa_w_qkv, gqa_q_norm, gqa_k_norm, gqa_w_o, ffn_w_up, ffn_conv_w, ffn_conv_b, ffn_w_down, loss_target, m_mix_norm, m_ffn_norm, m_final_norm, m_na_w_qkv, m_na_rpb, m_na_w_o, m_sc_w_in, m_sc_conv_w, m_sc_w_out, m_gqa_w_qkv, m_gqa_q_norm, m_gqa_k_norm, m_gqa_w_o, m_ffn_w_up, m_ffn_conv_w, m_ffn_conv_b, m_ffn_w_down, v_mix_norm, v_ffn_norm, v_final_norm, v_na_w_qkv, v_na_rpb, v_na_w_o, v_sc_w_in, v_sc_conv_w, v_sc_w_out, v_gqa_w_qkv, v_gqa_q_norm, v_gqa_k_norm, v_gqa_w_o, v_ffn_w_up, v_ffn_conv_w, v_ffn_conv_b, v_ffn_w_down):
    depth, d = mix_norm.shape
    chip = 2 * lax.axis_index("x") + lax.axis_index("y")
    w_in_of = {0: na_w_qkv, 1: sc_w_in, 2: gqa_w_qkv}
    w_out_of = {0: na_w_o, 1: sc_w_out, 2: gqa_w_o}

    layers = []
    for i in range(depth):
        kind, j = _mixer_of(i)
        shards, split = [w_in_of[kind][j].astype(BF16), w_out_of[kind][j].astype(BF16)], [True, True]
        if kind == 1:
            shards.append(sc_conv_w[j])
            split.append(False)
        mix = _gather_shards(shards, split, name=f"gather_mix_l{i}", collective_id=1 + 2 * i)
        ffn = _gather_shards(
            [ffn_w_up[i].astype(BF16), ffn_w_down[i].astype(BF16), ffn_conv_w[i]],
            [True, True, False],
            name=f"gather_ffn_l{i}",
            collective_id=2 + 2 * i,
        )
        lw = {
            "w_in": mix[0],
            "w_out": mix[1].reshape(1, -1, d),
            "w_up": ffn[0],
            "w_down": ffn[1].reshape(1, -1, d),
            "ffn_conv_w": ffn[2],
            "ffn_conv_b": ffn_conv_b[i : i + 1],
        }
        if kind == 1:
            lw["sc_conv_w"] = mix[2]
        layers.append(lw)

    norms = dict(mix_norm=mix_norm, ffn_norm=ffn_norm, final_norm=final_norm, na_rpb=na_rpb,
                 gqa_q_norm=gqa_q_norm, gqa_k_norm=gqa_k_norm)

    def send(i, part, pieces):
        cid = 1 + 2 * depth + 2 * i + (part == "mix")
        return _scatter_pieces(pieces, name=f"scatter_{part}_l{i}", collective_id=cid)

    loss, grad_x, big, small = _forward_backward(x[0], loss_target[0], norms, layers, send)

    def update(w, m, v, part, slot, layer_ids, name):
        out = None
        for slab, i in reversed(list(enumerate(layer_ids))):
            own, sib = big[i][part]
            out = _adamw_slab(w, m, v, own[slot], sib[slot], slab, out, name=f"adamw_{name}_l{i}")
        return out

    by_kind = {k: [i for i in range(depth) if i % N_MIXERS == k] for k in range(N_MIXERS)}
    every = list(range(depth))
    res = {
        "na_w_qkv": update(na_w_qkv, m_na_w_qkv, v_na_w_qkv, "mix", 0, by_kind[0], "na_w_qkv"),
        "na_w_o": update(na_w_o, m_na_w_o, v_na_w_o, "mix", 1, by_kind[0], "na_w_o"),
        "sc_w_in": update(sc_w_in, m_sc_w_in, v_sc_w_in, "mix", 0, by_kind[1], "sc_w_in"),
        "sc_w_out": update(sc_w_out, m_sc_w_out, v_sc_w_out, "mix", 1, by_kind[1], "sc_w_out"),
        "gqa_w_qkv": update(gqa_w_qkv, m_gqa_w_qkv, v_gqa_w_qkv, "mix", 0, by_kind[2], "gqa_w_qkv"),
        "gqa_w_o": update(gqa_w_o, m_gqa_w_o, v_gqa_w_o, "mix", 1, by_kind[2], "gqa_w_o"),
        "ffn_w_up": update(ffn_w_up, m_ffn_w_up, v_ffn_w_up, "ffn", 0, every, "ffn_w_up"),
        "ffn_w_down": update(ffn_w_down, m_ffn_w_down, v_ffn_w_down, "ffn", 1, every, "ffn_w_down"),
    }

    n_na = na_rpb.shape[0]
    rpb_flat = jnp.stack([small["na_rpb"][j] for j in range(n_na)])
    full_parts = [
        loss[:, :1],
        jnp.concatenate(small["mix_norm"], axis=0),
        jnp.concatenate(small["ffn_norm"], axis=0),
        small["final_norm"],
        rpb_flat,
        small["sc_conv_w"],
        small["gqa_q_norm"],
        small["gqa_k_norm"],
        jnp.stack(small["ffn_conv_w"]),
        jnp.concatenate(small["ffn_conv_b"], axis=0),
    ]
    summed = _unpack(_allreduce_small(_pack(full_parts), name="allreduce_small"), [p.shape for p in full_parts])
    loss_all, g_mix, g_ffn, g_final, g_rpb, g_sc_cw, g_gq, g_gk, g_ffn_cw, g_ffn_cb = summed
    g_rpb = g_rpb[:, :, : RPB_ROWS * RPB_COLS].reshape(na_rpb.shape)
    g_sc_cw = lax.dynamic_slice_in_dim(g_sc_cw, chip * sc_conv_w.shape[2], sc_conv_w.shape[2], axis=1)[None]
    g_ffn_cw = lax.dynamic_slice_in_dim(g_ffn_cw, chip * ffn_conv_w.shape[2], ffn_conv_w.shape[2], axis=2)
    small_names = ["mix_norm", "ffn_norm", "final_norm", "na_rpb", "sc_conv_w", "gqa_q_norm", "gqa_k_norm",
                   "ffn_conv_w", "ffn_conv_b"]
    small_g = [g_mix, g_ffn, g_final.reshape(final_norm.shape), g_rpb, g_sc_cw, g_gq, g_gk, g_ffn_cw, g_ffn_cb]
    small_w = [mix_norm, ffn_norm, final_norm, na_rpb, sc_conv_w, gqa_q_norm, gqa_k_norm, ffn_conv_w, ffn_conv_b]
    small_m = [m_mix_norm, m_ffn_norm, m_final_norm, m_na_rpb, m_sc_conv_w, m_gqa_q_norm, m_gqa_k_norm,
               m_ffn_conv_w, m_ffn_conv_b]
    small_v = [v_mix_norm, v_ffn_norm, v_final_norm, v_na_rpb, v_sc_conv_w, v_gqa_q_norm, v_gqa_k_norm,
               v_ffn_conv_w, v_ffn_conv_b]
    shapes = [w.shape for w in small_w]
    packed = _adamw_small(_pack(small_w), _pack(small_g), _pack(small_m), _pack(small_v), name="adamw_small")
    small_d, small_nm, small_nv = (_unpack(p, shapes) for p in packed)
    for n, g, dl, nm, nv in zip(small_names, small_g, small_d, small_nm, small_nv):
        res[n] = (g.reshape(dl.shape), dl, nm, nv)

    order = ["mix_norm", "ffn_norm", "final_norm", "na_w_qkv", "na_rpb", "na_w_o", "sc_w_in", "sc_conv_w",
             "sc_w_out", "gqa_w_qkv", "gqa_q_norm", "gqa_k_norm", "gqa_w_o", "ffn_w_up", "ffn_conv_w",
             "ffn_conv_b", "ffn_w_down"]
    outs = [loss_all.reshape(()), grad_x[None]]
    for part in range(4):
        outs.extend(res[n][part] for n in order)
    return tuple(outs)
```

```python
import functools

import jax
import jax.numpy as jnp
from jax import lax
from jax.experimental import pallas as pl
from jax.experimental.pallas import tpu as pltpu
from jax.experimental.pallas import tpu_sc as plsc

F32 = jnp.float32
BF16 = jnp.bfloat16
MESH = pl.DeviceIdType.MESH

N_CHIPS = 4
N_DEV = 8
N_MIXERS = 3
GRID_W = 64
HEAD_DIM = 128
EPS = 1e-6
NEG_INF = -1e30
NA_WIN_R = 8
NA_WIN_C = 16
GQA_GROUP = 4
ROPE_THETA = 10000.0
ADAM_LR = 0.001
ADAM_B1 = 0.9
ADAM_B2 = 0.999
ADAM_EPS = 1e-08
ADAM_WD = 0.01
ADAM_STEP = 10

LANES = 128
VMEM_LIMIT = 48 * 1024 * 1024
NT_DIMS = (((1,), (1,)), ((), ()))
TN_DIMS = (((0,), (0,)), ((), ()))


def _pick(n, cap, mult=LANES):
    best = None
    for t in range(mult, min(n, cap) + 1, mult):
        if n % t == 0:
            best = t
    return best if best is not None else n


def _params(*sem):
    return pltpu.CompilerParams(dimension_semantics=sem, vmem_limit_bytes=VMEM_LIMIT)


MM_VMEM_BUDGET = 40 * 1024 * 1024
MM_CONTRACT = 2816


def _mm_rows(m, blocks_for, mult=16):
    for cap in (1024, 512, 256, 128):
        tm = _pick(m, cap, mult)
        if sum(r * c * b * n for r, c, b, n in blocks_for(tm)) <= MM_VMEM_BUDGET:
            return tm
    return _pick(m, 128, mult)


def _accumulate(acc, step, steps, part, finish):
    if steps == 1:
        finish(part)
        return

    @pl.when(step == 0)
    def _():
        acc[...] = part

    @pl.when(step != 0)
    def _():
        acc[...] += part

    @pl.when(step == steps - 1)
    def _():
        finish(acc[...])


def _mm_nn(a, b, *, out_dtype, name, residual=None):
    m, k = a.shape
    nc, _, ncol = b.shape
    tn, tk = _pick(ncol, 1536), _pick(k, MM_CONTRACT)
    per, nk = ncol // tn, k // tk
    osz = jnp.dtype(out_dtype).itemsize
    tm = _mm_rows(m, lambda t: [(t, tk, a.dtype.itemsize, 2), (tk, tn, 2, 2), (t, tn, osz, 2),
                                (t, tn, 4, 2 * (residual is not None)), (t, tn, 4, nk > 1)])

    def body(a_ref, b_ref, *rest):
        o_ref = rest[-2] if nk > 1 else rest[-1]

        def finish(r):
            if residual is not None:
                r = r + rest[0][...]
            o_ref[...] = r.astype(o_ref.dtype)

        part = jnp.dot(a_ref[...].astype(BF16), b_ref[...], preferred_element_type=F32)
        _accumulate(rest[-1], pl.program_id(2), nk, part, finish)

    in_specs = [
        pl.BlockSpec((tm, tk), lambda i, j, kk: (i, kk)),
        pl.BlockSpec((None, tk, tn), lambda i, j, kk: (j // per, kk, j % per)),
    ]
    ops = [a, b]
    if residual is not None:
        in_specs.append(pl.BlockSpec((tm, tn), lambda i, j, kk: (i, j)))
        ops.append(residual)
    return pl.pallas_call(
        body,
        name=name,
        grid=(m // tm, nc * per, nk),
        in_specs=in_specs,
        out_specs=pl.BlockSpec((tm, tn), lambda i, j, kk: (i, j)),
        out_shape=jax.ShapeDtypeStruct((m, nc * ncol), out_dtype),
        scratch_shapes=[pltpu.VMEM((tm, tn), F32)] * (nk > 1),
        compiler_params=_params("parallel", "parallel", "arbitrary"),
    )(*ops)


def _token(x):
    return x[(0,) * (x.ndim - 2)][:16, :LANES]


def _mm_nt(a, b, *, out_dtype, name, after=()):
    m, n = a.shape
    nc, k, ncol = b.shape
    tko, tn = _pick(k, 1024), _pick(ncol, MM_CONTRACT)
    per, nn = ncol // tn, n // tn
    osz = jnp.dtype(out_dtype).itemsize
    tm = _mm_rows(m, lambda t: [(t, tn, a.dtype.itemsize, 2), (tko, tn, 2, 2), (t, tko, osz, 2), (t, tko, 4, nn > 1)])

    def body(a_ref, b_ref, *rest):
        o_ref = rest[len(after)]

        def finish(r):
            o_ref[...] = r.astype(o_ref.dtype)

        part = lax.dot_general(a_ref[...].astype(BF16), b_ref[...], NT_DIMS, preferred_element_type=F32)
        _accumulate(rest[-1], pl.program_id(2), nn, part, finish)

    return pl.pallas_call(
        body,
        name=name,
        grid=(m // tm, k // tko, nn),
        in_specs=[
            pl.BlockSpec((tm, tn), lambda i, j, s: (i, s)),
            pl.BlockSpec((None, tko, tn), lambda i, j, s: (s // per, j, s % per)),
        ]
        + [pl.BlockSpec(t.shape, lambda i, j, s: (0, 0)) for t in after],
        out_specs=pl.BlockSpec((tm, tko), lambda i, j, s: (i, j)),
        out_shape=jax.ShapeDtypeStruct((m, k), out_dtype),
        scratch_shapes=[pltpu.VMEM((tm, tko), F32)] * (nn > 1),
        compiler_params=_params("parallel", "parallel", "arbitrary"),
    )(a, b, *after)


def _mm_tn(a, g, nc, *, name):
    s, k = a.shape
    n = g.shape[1]
    ncol = n // nc
    ts, tn = _pick(s, MM_CONTRACT, 16), _pick(ncol, 1536)
    per, ns = ncol // tn, s // ts
    tko = _mm_rows(k, lambda t: [(ts, t, a.dtype.itemsize, 2), (ts, tn, g.dtype.itemsize, 2), (t, tn, 2, 2),
                                 (t, tn, 4, ns > 1)], mult=LANES)

    def body(a_ref, g_ref, *rest):
        o_ref = rest[0]

        def finish(r):
            o_ref[...] = r.astype(o_ref.dtype)

        part = lax.dot_general(a_ref[...].astype(BF16), g_ref[...].astype(BF16), TN_DIMS, preferred_element_type=F32)
        _accumulate(rest[-1], pl.program_id(2), ns, part, finish)

    return pl.pallas_call(
        body,
        name=name,
        grid=(k // tko, nc * per, ns),
        in_specs=[
            pl.BlockSpec((ts, tko), lambda i, j, t: (t, i)),
            pl.BlockSpec((ts, tn), lambda i, j, t: (t, j)),
        ],
        out_specs=pl.BlockSpec((None, tko, tn), lambda i, j, t: (j // per, i, j % per)),
        out_shape=jax.ShapeDtypeStruct((nc, k, ncol), BF16),
        scratch_shapes=[pltpu.VMEM((tko, tn), F32)] * (ns > 1),
        compiler_params=_params("parallel", "parallel", "arbitrary"),
    )(a, g)


ROW_TILE = 256


def _cast_slab(w, slab, *, name):
    _, rows, cols = w.shape
    tr = _pick(rows, ROW_TILE, 16)

    def body(w_ref, o_ref):
        o_ref[...] = w_ref[...].astype(o_ref.dtype)

    return pl.pallas_call(
        body,
        name=name,
        grid=(rows // tr,),
        in_specs=[pl.BlockSpec((None, tr, cols), lambda i: (slab, i, 0))],
        out_specs=pl.BlockSpec((tr, cols), lambda i: (i, 0)),
        out_shape=jax.ShapeDtypeStruct((rows, cols), BF16),
        compiler_params=_params("parallel"),
    )(w)


def _rms_fwd(h, g, *, name):
    s, d = h.shape
    tr = _pick(s, ROW_TILE, 16)

    def body(h_ref, g_ref, o_ref):
        x = h_ref[...]
        r = lax.rsqrt(jnp.mean(x * x, axis=-1, keepdims=True) + EPS)
        o_ref[...] = (x * r * g_ref[...]).astype(o_ref.dtype)

    return pl.pallas_call(
        body,
        name=name,
        grid=(s // tr,),
        in_specs=[pl.BlockSpec((tr, d), lambda i: (i, 0)), pl.BlockSpec((1, d), lambda i: (0, 0))],
        out_specs=pl.BlockSpec((tr, d), lambda i: (i, 0)),
        out_shape=jax.ShapeDtypeStruct((s, d), BF16),
        compiler_params=_params("parallel"),
    )(h, g)


def _rms_bwd(h, g, dy, dres, *, name):
    s, d = h.shape
    tr = _pick(s, ROW_TILE, 16)

    def body(h_ref, g_ref, dy_ref, dres_ref, dh_ref, dhb_ref, dg_ref):
        x = h_ref[...]
        r = lax.rsqrt(jnp.mean(x * x, axis=-1, keepdims=True) + EPS)
        xhat = x * r
        dyv = dy_ref[...].astype(F32)
        dyg = dyv * g_ref[...]
        dx = r * (dyg - xhat * jnp.mean(dyg * xhat, axis=-1, keepdims=True))
        dh = dres_ref[...] + dx
        dh_ref[...] = dh
        dhb_ref[...] = dh.astype(dhb_ref.dtype)
        part = jnp.sum(dyv * xhat, axis=0, keepdims=True)

        @pl.when(pl.program_id(0) == 0)
        def _():
            dg_ref[...] = part

        @pl.when(pl.program_id(0) != 0)
        def _():
            dg_ref[...] += part

    row = pl.BlockSpec((tr, d), lambda i: (i, 0))
    vec = pl.BlockSpec((1, d), lambda i: (0, 0))
    return pl.pallas_call(
        body,
        name=name,
        grid=(s // tr,),
        in_specs=[row, vec, row, row],
        out_specs=[row, row, vec],
        out_shape=[jax.ShapeDtypeStruct((s, d), F32), jax.ShapeDtypeStruct((s, d), BF16),
                   jax.ShapeDtypeStruct((1, d), F32)],
        compiler_params=_params("arbitrary"),
    )(h, g, dy, dres)


def _loss_head(h, g, target, *, name):
    s, d = h.shape
    tr = _pick(s, ROW_TILE, 16)

    def body(h_ref, g_ref, t_ref, dh_ref, dhb_ref, dg_ref, loss_ref):
        x = h_ref[...]
        r = lax.rsqrt(jnp.mean(x * x, axis=-1, keepdims=True) + EPS)
        xhat = x * r
        gv = g_ref[...]
        err = xhat * gv - t_ref[...]
        dyv = err * (1.0 / d)
        dyg = dyv * gv
        dh = r * (dyg - xhat * jnp.mean(dyg * xhat, axis=-1, keepdims=True))
        dh_ref[...] = dh
        dhb_ref[...] = dh.astype(dhb_ref.dtype)
        part = jnp.sum(dyv * xhat, axis=0, keepdims=True)
        lpart =jnp.sum(jnp.sum(err * err, axis=-1, keepdims=True), axis=0, keepdims=True) * (0.5 / d)

        @pl.when(pl.program_id(0) == 0)
        def _():
            dg_ref[...] = part
            loss_ref[...] = jnp.broadcast_to(lpart, loss_ref.shape)

        @pl.when(pl.program_id(0) != 0)
        def _():
            dg_ref[...] += part
            loss_ref[...] += jnp.broadcast_to(lpart, loss_ref.shape)

    row = pl.BlockSpec((tr, d), lambda i: (i, 0))
    vec = pl.BlockSpec((1, d), lambda i: (0, 0))
    return pl.pallas_call(
        body,
        name=name,
        grid=(s // tr,),
        in_specs=[row, vec, row],
        out_specs=[row, row, vec, pl.BlockSpec((1, LANES), lambda i: (0, 0))],
        out_shape=[
            jax.ShapeDtypeStruct((s, d), F32),
            jax.ShapeDtypeStruct((s, d), BF16),
            jax.ShapeDtypeStruct((1, d), F32),
            jax.ShapeDtypeStruct((1, LANES), F32),
        ],
        compiler_params=_params("arbitrary"),
    )(h, g, target)


def _shift_prev(x):
    row = lax.broadcasted_iota(jnp.int32, x.shape, 0)
    return jnp.where(row == 0, 0.0, pltpu.roll(x, 1, 0))


def _shift_next(x):
    n = x.shape[0]
    row = lax.broadcasted_iota(jnp.int32, x.shape, 0)
    return jnp.where(row == n - 1, 0.0, pltpu.roll(x, n - 1, 0))


def _conv3(x, w):
    xm, xp = _shift_prev(x), _shift_next(x)
    return xm * w[0:1] + x * w[1:2] + xp * w[2:3], xm, xp


def _conv3_t(d, w):
    return _shift_next(d) * w[0:1] + d * w[1:2] + _shift_prev(d) * w[2:3]


def _colsum(x):
    return jnp.sum(x, axis=0, keepdims=True)


def _ffn_mid_fwd(up, cw, cb, *, name):
    s, f2 = up.shape
    f = f2 // 2
    ncol = cw.shape[2]
    tc = _pick(ncol, 256)
    nt, per = f // tc, ncol // tc

    def body(ug_ref, uu_ref, wg_ref, wu_ref, bg_ref, bu_ref, o_ref):
        cg = _conv3(ug_ref[...].astype(F32), wg_ref[...])[0] + bg_ref[...]
        cu = _conv3(uu_ref[...].astype(F32), wu_ref[...])[0] + bu_ref[...]
        o_ref[...] = (cg * (1.0 / (1.0 + jnp.exp(-cg))) * cu).astype(o_ref.dtype)

    return pl.pallas_call(
        body,
        name=name,
        grid=(nt,),
        in_specs=[
            pl.BlockSpec((s, tc), lambda j: (0, j)),
            pl.BlockSpec((s, tc), lambda j: (0, nt + j)),
            pl.BlockSpec((None, 3, tc), lambda j: (j // per, 0, j % per)),
            pl.BlockSpec((None, 3, tc), lambda j: ((nt + j) // per, 0, (nt + j) % per)),
            pl.BlockSpec((1, tc), lambda j: (0, j)),
            pl.BlockSpec((1, tc), lambda j: (0, nt + j)),
        ],
        out_specs=pl.BlockSpec((s, tc), lambda j: (0, j)),
        out_shape=jax.ShapeDtypeStruct((s, f), BF16),
        compiler_params=_params("parallel"),
    )(up, up, cw, cw, cb, cb)


def _ffn_mid_bwd(up, cw, cb, dact, *, name):
    s, f2 = up.shape
    f = f2 // 2
    ncol = cw.shape[2]
    tc = _pick(ncol, 256)
    nt, per = f // tc, ncol // tc

    def body(ug_ref, uu_ref, wg_ref, wu_ref, bg_ref, bu_ref, da_ref, dug_ref, duu_ref, dwg_ref, dwu_ref, dbg_ref, dbu_ref):
        ug, uu = ug_ref[...].astype(F32), uu_ref[...].astype(F32)
        wg, wu = wg_ref[...], wu_ref[...]
        cg, ugm, ugp = _conv3(ug, wg)
        cu, uum, uup = _conv3(uu, wu)
        cg = cg + bg_ref[...]
        cu = cu + bu_ref[...]
        da = da_ref[...].astype(F32)
        sig = 1.0 / (1.0 + jnp.exp(-cg))
        dcu = da * (cg * sig)
        dcg = da * cu * (sig * (1.0 + cg * (1.0 - sig)))
        dug_ref[...] = _conv3_t(dcg, wg).astype(dug_ref.dtype)
        duu_ref[...] = _conv3_t(dcu, wu).astype(duu_ref.dtype)
        dwg_ref[0:1, :] = _colsum(dcg * ugm)
        dwg_ref[1:2, :] = _colsum(dcg * ug)
        dwg_ref[2:3, :] = _colsum(dcg * ugp)
        dwu_ref[0:1, :] = _colsum(dcu * uum)
        dwu_ref[1:2, :] = _colsum(dcu * uu)
        dwu_ref[2:3, :] = _colsum(dcu * uup)
        dbg_ref[...] = _colsum(dcg)
        dbu_ref[...] = _colsum(dcu)

    col = pl.BlockSpec((s, tc), lambda j: (0, j))
    w3 = pl.BlockSpec((3, tc), lambda j: (0, j))
    b1 = pl.BlockSpec((1, tc), lambda j: (0, j))
    return pl.pallas_call(
        body,
        name=name,
        grid=(nt,),
        in_specs=[
            col,
            pl.BlockSpec((s, tc), lambda j: (0, nt + j)),
            pl.BlockSpec((None, 3, tc), lambda j: (j // per, 0, j % per)),
            pl.BlockSpec((None, 3, tc), lambda j: ((nt + j) // per, 0, (nt + j) % per)),
            b1,
            pl.BlockSpec((1, tc), lambda j: (0, nt + j)),
            col,
        ],
        out_specs=[col, col, w3, w3, b1, b1],
        out_shape=[
            jax.ShapeDtypeStruct((s, f), BF16),
            jax.ShapeDtypeStruct((s, f), BF16),
            jax.ShapeDtypeStruct((3, f), F32),
            jax.ShapeDtypeStruct((3, f), F32),
            jax.ShapeDtypeStruct((1, f), F32),
            jax.ShapeDtypeStruct((1, f), F32),
        ],
        compiler_params=_params("parallel"),
    )(up, up, cw, cw, cb, cb, dact)


def _sc_mid_fwd(z, cw, *, name):
    s, d3 = z.shape
    d = d3 // 3
    ncol = cw.shape[2]
    tc = _pick(ncol, 256)
    nt, per = d // tc, ncol // tc

    def body(gb_ref, gc_ref, hh_ref, w_ref, o_ref):
        p = gc_ref[...].astype(F32) * hh_ref[...].astype(F32)
        o_ref[...] = (gb_ref[...].astype(F32) * _conv3(p, w_ref[...])[0]).astype(o_ref.dtype)

    return pl.pallas_call(
        body,
        name=name,
        grid=(nt,),
        in_specs=[
            pl.BlockSpec((s, tc), lambda j: (0, j)),
            pl.BlockSpec((s, tc), lambda j: (0, nt + j)),
            pl.BlockSpec((s, tc), lambda j: (0, 2 * nt + j)),
            pl.BlockSpec((None, 3, tc), lambda j: (j // per, 0, j % per)),
        ],
        out_specs=pl.BlockSpec((s, tc), lambda j: (0, j)),
        out_shape=jax.ShapeDtypeStruct((s, d), BF16),
        compiler_params=_params("parallel"),
    )(z, z, z, cw)


def _sc_mid_bwd(z, cw, dmid, *, name):
    s, d3 = z.shape
    d = d3 // 3
    ncol = cw.shape[2]
    tc = _pick(ncol, 256)
    nt, per = d // tc, ncol // tc

    def body(gb_ref, gc_ref, hh_ref, w_ref, dm_ref, dgb_ref, dgc_ref, dhh_ref, dw_ref):
        gb, gc, hh = gb_ref[...].astype(F32), gc_ref[...].astype(F32), hh_ref[...].astype(F32)
        w = w_ref[...]
        p = gc * hh
        cv, pm, pp = _conv3(p, w)
        dm = dm_ref[...].astype(F32)
        dgb_ref[...] = (dm * cv).astype(dgb_ref.dtype)
        dcv = dm * gb
        dp = _conv3_t(dcv, w)
        dgc_ref[...] = (dp * hh).astype(dgc_ref.dtype)
        dhh_ref[...] = (dp * gc).astype(dhh_ref.dtype)
        dw_ref[0:1, :] = _colsum(dcv * pm)
        dw_ref[1:2, :] = _colsum(dcv * p)
        dw_ref[2:3, :] = _colsum(dcv * pp)

    col = pl.BlockSpec((s, tc), lambda j: (0, j))
    return pl.pallas_call(
        body,
        name=name,
        grid=(nt,),
        in_specs=[
            col,
            pl.BlockSpec((s, tc), lambda j: (0, nt + j)),
            pl.BlockSpec((s, tc), lambda j: (0, 2 * nt + j)),
            pl.BlockSpec((None, 3, tc), lambda j: (j // per, 0, j % per)),
            col,
        ],
        out_specs=[col, col, col, pl.BlockSpec((3, tc), lambda j: (0, j))],
        out_shape=[jax.ShapeDtypeStruct((s, d), BF16)] * 3 + [jax.ShapeDtypeStruct((3, d), F32)],
        compiler_params=_params("parallel"),
    )(z, z, z, cw, dmid)


NA_KEYS = NA_WIN_R * GRID_W
NA_ROWS_PER_STEP = 8


def _na_row_start(r, rows):
    return jnp.clip(r - NA_WIN_R // 2, 0, rows - NA_WIN_R)


def _na_bias_slot(r, rows):
    return _na_row_start(r, rows) - r + NA_WIN_R - 1


def _na_base(rpb):
    h = rpb.shape[0]
    pos, neg = rpb[:, :, NA_WIN_C - 1:], rpb[:, :, : NA_WIN_C - 1]
    zeros = jnp.zeros((h, NA_WIN_R, GRID_W - 2 * NA_WIN_C + 1), F32)
    out = []
    for first in range(NA_WIN_R):
        p = pos[:, first : first + NA_WIN_R]
        n = jnp.roll(neg[:, first : first + NA_WIN_R], -1, axis=1)
        out.append(jnp.concatenate([p, zeros, n], axis=-1).reshape(h, 1, NA_KEYS))
    return jnp.stack(out, axis=1)


def _skew_right(x):
    return pltpu.roll(x, 0, 1, stride=1, stride_axis=0)


def _skew_left(x):
    n = x.shape[1]
    row = lax.broadcasted_iota(jnp.int32, x.shape, 0)
    for b in range(GRID_W.bit_length() - 1):
        x = jnp.where(((row >> b) & 1) == 1, pltpu.roll(x, n - (1 << b), 1), x)
    return x


def _na_bias(base, *, name):
    h = base.shape[0]

    def body(b_ref, o_ref):
        x = _skew_right(jnp.broadcast_to(b_ref[...], (GRID_W, NA_KEYS)))
        q = lax.broadcasted_iota(jnp.int32, x.shape, 0)
        kc = lax.broadcasted_iota(jnp.int32, x.shape, 1) % GRID_W
        start = jnp.clip(q - NA_WIN_C // 2, 0, GRID_W - NA_WIN_C)
        o_ref[...] = jnp.where((kc >= start) & (kc < start + NA_WIN_C), x, NEG_INF)

    return pl.pallas_call(
        body,
        name=name,
        grid=(h, NA_WIN_R),
        in_specs=[pl.BlockSpec((None, None, 1, NA_KEYS), lambda i, j: (i, j, 0, 0))],
        out_specs=pl.BlockSpec((None, None, GRID_W, NA_KEYS), lambda i, j: (i, j, 0, 0)),
        out_shape=jax.ShapeDtypeStruct((h, NA_WIN_R, GRID_W, NA_KEYS), F32),
        compiler_params=_params("parallel", "parallel"),
    )(base)


def _na_specs(s, heads, per):
    q = pl.BlockSpec((per * GRID_W, HEAD_DIM), lambda h, r: (r, h))
    k = pl.BlockSpec((s, HEAD_DIM), lambda h, r: (0, heads + h))
    v = pl.BlockSpec((s, HEAD_DIM), lambda h, r: (0, 2 * heads + h))
    bias = pl.BlockSpec((None, NA_WIN_R, GRID_W, NA_KEYS), lambda h, r: (h, 0, 0, 0))
    return q, k, v, bias


def _na_probs(q, k, bias):
    sc = lax.dot_general(q, k, NT_DIMS, preferred_element_type=F32) * (HEAD_DIM ** -0.5) + bias
    p = jnp.exp(sc - jnp.max(sc, axis=-1, keepdims=True))
    return p, jnp.sum(p, axis=-1, keepdims=True)


def _na_fwd(qkv, bias, *, name):
    s = qkv.shape[0]
    heads = qkv.shape[1] // (3 * HEAD_DIM)
    rows = s // GRID_W
    per = _pick(rows, NA_ROWS_PER_STEP, 1)

    def body(q_ref, k_ref, v_ref, b_ref, o_ref):
        for i in range(per):
            r = pl.program_id(1) * per + i
            win = pl.ds(pl.multiple_of(_na_row_start(r, rows) * GRID_W, GRID_W), NA_KEYS)
            mine = pl.ds(i * GRID_W, GRID_W)
            p, l = _na_probs(q_ref[mine, :], k_ref[win, :], b_ref[_na_bias_slot(r, rows)])
            o = jnp.dot(p.astype(BF16), v_ref[win, :], preferred_element_type=F32)
            o_ref[mine, :] = (o / l).astype(o_ref.dtype)

    q, k, v, b = _na_specs(s, heads, per)
    return pl.pallas_call(
        body,
        name=name,
        grid=(heads, rows // per),
        in_specs=[q, k, v, b],
        out_specs=q,
        out_shape=jax.ShapeDtypeStruct((s, heads * HEAD_DIM), BF16),
        compiler_params=_params("parallel", "arbitrary"),
    )(qkv, qkv, qkv, bias)


def _na_bwd(qkv, bias, dout, *, name):
    s = qkv.shape[0]
    heads = qkv.shape[1] // (3 * HEAD_DIM)
    rows = s // GRID_W
    per = _pick(rows, NA_ROWS_PER_STEP, 1)
    steps = rows // per
    scale = HEAD_DIM ** -0.5

    def body(q_ref, k_ref, v_ref, b_ref, do_ref, dq_ref, dk_ref, dv_ref, db_ref, dk_acc, dv_acc):
        step = pl.program_id(1)

        @pl.when(step == 0)
        def _():
            dk_acc[...] = jnp.zeros_like(dk_acc)
            dv_acc[...] = jnp.zeros_like(dv_acc)
            db_ref[...] = jnp.zeros_like(db_ref)

        for i in range(per):
            r = step * per + i
            win = pl.ds(pl.multiple_of(_na_row_start(r, rows) * GRID_W, GRID_W), NA_KEYS)
            mine = pl.ds(i * GRID_W, GRID_W)
            slot = _na_bias_slot(r, rows)
            q, k, v, do = q_ref[mine, :], k_ref[win, :], v_ref[win, :], do_ref[mine, :]
            p, l = _na_probs(q, k, b_ref[slot])
            pn = p / l
            dp = lax.dot_general(do, v, NT_DIMS, preferred_element_type=F32)
            ds = pn * (dp - jnp.sum(pn * dp, axis=-1, keepdims=True))
            dsb = ds.astype(BF16)
            dq_ref[mine, :] = (jnp.dot(dsb, k, preferred_element_type=F32) * scale).astype(dq_ref.dtype)
            dk_acc[win, :] += lax.dot_general(dsb, q, TN_DIMS, preferred_element_type=F32) * scale
            dv_acc[win, :] += lax.dot_general(pn.astype(BF16), do, TN_DIMS, preferred_element_type=F32)
            db_ref[slot] += ds

        @pl.when(step == steps - 1)
        def _():
            dk_ref[...] = dk_acc[...].astype(dk_ref.dtype)
            dv_ref[...] = dv_acc[...].astype(dv_ref.dtype)

    q, k, v, b = _na_specs(s, heads, per)
    kv_out = pl.BlockSpec((s, HEAD_DIM), lambda h, r: (0, h))
    shape = jax.ShapeDtypeStruct((s, heads * HEAD_DIM), BF16)
    return pl.pallas_call(
        body,
        name=name,
        grid=(heads, steps),
        in_specs=[q, k, v, b, q],
        out_specs=[q, kv_out, kv_out, b],
        out_shape=[shape, shape, shape, jax.ShapeDtypeStruct((heads, NA_WIN_R, GRID_W, NA_KEYS), F32)],
        scratch_shapes=[pltpu.VMEM((s, HEAD_DIM), F32), pltpu.VMEM((s, HEAD_DIM), F32)],
        compiler_params=_params("parallel", "arbitrary"),
    )(qkv, qkv, qkv, bias, dout)


RPB_ROWS = 2 * NA_WIN_R - 1
RPB_COLS = 2 * NA_WIN_C - 1
RPB_PAD = 512


def _rpb_fold_matrix():
    idx = jnp.arange(NA_WIN_R * NA_KEYS, dtype=jnp.int32)
    first, i, kc = idx // NA_KEYS, (idx // GRID_W) % NA_WIN_R, idx % GRID_W
    pos, neg = kc < NA_WIN_C, kc >= GRID_W - NA_WIN_C + 1
    dr = jnp.where(pos, first + i, first + (i + 1) % NA_WIN_R)
    dc = jnp.where(pos, kc + NA_WIN_C - 1, kc - (GRID_W - NA_WIN_C + 1))
    target = jnp.where(pos | neg, dr * RPB_COLS + dc, -1)
    return (target[:, None] == jnp.arange(RPB_PAD, dtype=jnp.int32)[None, :]).astype(F32)


def _rpb_fold(dbias, *, name):
    h = dbias.shape[0]

    def skew_body(g_ref, o_ref):
        for slot in range(NA_WIN_R):
            o_ref[slot] = _colsum(_skew_left(g_ref[slot]))

    skewed = pl.pallas_call(
        skew_body,
        name=name + "_skew",
        grid=(h,),
        in_specs=[pl.BlockSpec((None, NA_WIN_R, GRID_W, NA_KEYS), lambda i: (i, 0, 0, 0))],
        out_specs=pl.BlockSpec((None, NA_WIN_R, 1, NA_KEYS), lambda i: (i, 0, 0, 0)),
        out_shape=jax.ShapeDtypeStruct((h, NA_WIN_R, 1, NA_KEYS), F32),
        compiler_params=_params("parallel"),
    )(dbias)

    def fold_body(g_ref, m_ref, o_ref):
        o_ref[...] = jnp.dot(g_ref[...], m_ref[...], preferred_element_type=F32, precision=lax.Precision.HIGHEST)

    return pl.pallas_call(
        fold_body,
        name=name,
        out_shape=jax.ShapeDtypeStruct((h, RPB_PAD), F32),
        compiler_params=pltpu.CompilerParams(vmem_limit_bytes=VMEM_LIMIT),
    )(skewed.reshape(h, NA_WIN_R * NA_KEYS), _rpb_fold_matrix())


GQA_Q_TILE = 256


def _rope_tables(s):
    t = jnp.arange(s)
    row = (t // GRID_W).astype(F32)[:, None]
    col = (t % GRID_W).astype(F32)[:, None]
    half = HEAD_DIM // 2
    inv = ROPE_THETA ** (-jnp.arange(0, half, 2, dtype=F32) / half)
    ang = jnp.concatenate([row * inv, row * inv, col * inv, col * inv], axis=-1)
    return jnp.cos(ang), jnp.sin(ang)


def _rot_half(y):
    quarter = HEAD_DIM // 4
    lane = lax.broadcasted_iota(jnp.int32, y.shape, 1)
    low = (lane % (2 * quarter)) < quarter
    return jnp.where(low, -pltpu.roll(y, HEAD_DIM - quarter, 1), pltpu.roll(y, quarter, 1))


def _gqa_prep_fwd(qkv, gq, gk, cos, sin, hq, hkv, *, name):
    s = qkv.shape[0]

    def body(x_ref, gq_ref, gk_ref, cos_ref, sin_ref, o_ref):
        isq = pl.program_id(0) < hq
        x = x_ref[...].astype(F32)
        g = jnp.where(isq, gq_ref[...], gk_ref[...])
        y = x * lax.rsqrt(jnp.mean(x * x, axis=-1, keepdims=True) + EPS) * g
        z = y * cos_ref[...] + _rot_half(y) * sin_ref[...]
        o_ref[...] = (z * jnp.where(isq, HEAD_DIM ** -0.5, 1.0)).astype(o_ref.dtype)

    head = pl.BlockSpec((s, HEAD_DIM), lambda h: (0, h))
    vec = pl.BlockSpec((1, HEAD_DIM), lambda h: (0, 0))
    tab = pl.BlockSpec((s, HEAD_DIM), lambda h: (0, 0))
    return pl.pallas_call(
        body,
        name=name,
        grid=(hq + hkv,),
        in_specs=[head, vec, vec, tab, tab],
        out_specs=head,
        out_shape=jax.ShapeDtypeStruct((s, (hq + hkv) * HEAD_DIM), BF16),
        compiler_params=_params("parallel"),
    )(qkv, gq, gk, cos, sin)


def _gqa_prep_bwd(qkv, gq, gk, cos, sin, dqn, dkn, hq, hkv, *, name):
    s = qkv.shape[0]

    def body(x_ref, gq_ref, gk_ref, cos_ref, sin_ref, dq_ref, dk_ref, dx_ref, dgq_ref, dgk_ref):
        hh = pl.program_id(0)
        isq = hh < hq
        x = x_ref[...].astype(F32)
        g = jnp.where(isq, gq_ref[...], gk_ref[...])
        r = lax.rsqrt(jnp.mean(x * x, axis=-1, keepdims=True) + EPS)
        xhat = x * r
        dz = jnp.where(isq, dq_ref[...].astype(F32) * (HEAD_DIM ** -0.5), dk_ref[...].astype(F32))
        dy = dz * cos_ref[...] - _rot_half(dz * sin_ref[...])
        dyg = dy * g
        dx_ref[...] = (r * (dyg - xhat * jnp.mean(dyg * xhat, axis=-1, keepdims=True))).astype(dx_ref.dtype)
        part = _colsum(dy * xhat)

        @pl.when(hh == 0)
        def _():
            dgq_ref[...] = jnp.zeros_like(dgq_ref)
            dgk_ref[...] = jnp.zeros_like(dgk_ref)

        @pl.when(isq)
        def _():
            dgq_ref[...] += part

        @pl.when(jnp.logical_not(isq))
        def _():
            dgk_ref[...] += part

    head = pl.BlockSpec((s, HEAD_DIM), lambda h: (0, h))
    vec = pl.BlockSpec((1, HEAD_DIM), lambda h: (0, 0))
    tab = pl.BlockSpec((s, HEAD_DIM), lambda h: (0, 0))
    return pl.pallas_call(
        body,
        name=name,
        grid=(hq + hkv,),
        in_specs=[
            head,
            vec,
            vec,
            tab,
            tab,
            pl.BlockSpec((s, HEAD_DIM), lambda h: (0, jnp.minimum(h, hq - 1))),
            pl.BlockSpec((s, HEAD_DIM), lambda h: (0, jnp.maximum(h - hq, 0))),
        ],
        out_specs=[head, vec, vec],
        out_shape=[
            jax.ShapeDtypeStruct((s, (hq + hkv) * HEAD_DIM), BF16),
            jax.ShapeDtypeStruct((1, HEAD_DIM), F32),
            jax.ShapeDtypeStruct((1, HEAD_DIM), F32),
        ],
        compiler_params=_params("arbitrary"),
    )(qkv, gq, gk, cos, sin, dqn, dkn)


def _gqa_fwd(qkn, qkv, hq, hkv, *, name):
    s = qkv.shape[0]
    tq = _pick(s, GQA_Q_TILE, 16)

    def body(q_ref, k_ref, v_ref, o_ref):
        sc = lax.dot_general(q_ref[...], k_ref[...], NT_DIMS, preferred_element_type=F32)
        p = jnp.exp(sc - jnp.max(sc, axis=-1, keepdims=True))
        l = jnp.sum(p, axis=-1, keepdims=True)
        o_ref[...] = (jnp.dot(p.astype(BF16), v_ref[...], preferred_element_type=F32) / l).astype(o_ref.dtype)

    q = pl.BlockSpec((tq, HEAD_DIM), lambda h, i: (i, h))
    return pl.pallas_call(
        body,
        name=name,
        grid=(hq, s // tq),
        in_specs=[
            q,
            pl.BlockSpec((s, HEAD_DIM), lambda h, i: (0, hq + h // GQA_GROUP)),
            pl.BlockSpec((s, HEAD_DIM), lambda h, i: (0, hq + hkv + h // GQA_GROUP)),
        ],
        out_specs=q,
        out_shape=jax.ShapeDtypeStruct((s, hq * HEAD_DIM), BF16),
        compiler_params=_params("parallel", "parallel"),
    )(qkn, qkn, qkv)


def _gqa_bwd(qkn, qkv, dout, hq, hkv, *, name):
    s = qkv.shape[0]
    tq = _pick(s, GQA_Q_TILE, 16)
    nq = s // tq

    def body(q_ref, k_ref, v_ref, do_ref, dq_ref, dk_ref, dv_ref, dk_acc, dv_acc):
        g, i = pl.program_id(1), pl.program_id(2)

        @pl.when((g == 0) & (i == 0))
        def _():
            dk_acc[...] = jnp.zeros_like(dk_acc)
            dv_acc[...] = jnp.zeros_like(dv_acc)

        q, k, v, do = q_ref[...], k_ref[...], v_ref[...], do_ref[...]
        sc = lax.dot_general(q, k, NT_DIMS, preferred_element_type=F32)
        p = jnp.exp(sc - jnp.max(sc, axis=-1, keepdims=True))
        pn = p / jnp.sum(p, axis=-1, keepdims=True)
        dp = lax.dot_general(do, v, NT_DIMS, preferred_element_type=F32)
        dsb = (pn * (dp - jnp.sum(pn * dp, axis=-1, keepdims=True))).astype(BF16)
        dq_ref[...] = jnp.dot(dsb, k, preferred_element_type=F32).astype(dq_ref.dtype)
        dk_acc[...] += lax.dot_general(dsb, q, TN_DIMS, preferred_element_type=F32)
        dv_acc[...] += lax.dot_general(pn.astype(BF16), do, TN_DIMS, preferred_element_type=F32)

        @pl.when((g == GQA_GROUP - 1) & (i == nq - 1))
        def _():
            dk_ref[...] = dk_acc[...].astype(dk_ref.dtype)
            dv_ref[...] = dv_acc[...].astype(dv_ref.dtype)

    q = pl.BlockSpec((tq, HEAD_DIM), lambda kv, g, i: (i, kv * GQA_GROUP + g))
    kv_out = pl.BlockSpec((s, HEAD_DIM), lambda kv, g, i: (0, kv))
    return pl.pallas_call(
        body,
        name=name,
        grid=(hkv, GQA_GROUP, nq),
        in_specs=[
            q,
            pl.BlockSpec((s, HEAD_DIM), lambda kv, g, i: (0, hq + kv)),
            pl.BlockSpec((s, HEAD_DIM), lambda kv, g, i: (0, hq + hkv + kv)),
            q,
        ],
        out_specs=[q, kv_out, kv_out],
        out_shape=[
            jax.ShapeDtypeStruct((s, hq * HEAD_DIM), BF16),
            jax.ShapeDtypeStruct((s, hkv * HEAD_DIM), BF16),
            jax.ShapeDtypeStruct((s, hkv * HEAD_DIM), BF16),
        ],
        scratch_shapes=[pltpu.VMEM((s, HEAD_DIM), F32), pltpu.VMEM((s, HEAD_DIM), F32)],
        compiler_params=_params("parallel", "arbitrary", "arbitrary"),
    )(qkn, qkn, qkv, dout)


ADAM_ROWS = 64


def _adam_update(w, g, m, v):
    m = ADAM_B1 * m + (1.0 - ADAM_B1) * g
    v = ADAM_B2 * v + (1.0 - ADAM_B2) * (g * g)
    m_hat = m / (1.0 - ADAM_B1 ** ADAM_STEP)
    v_hat = v / (1.0 - ADAM_B2 ** ADAM_STEP)
    return -ADAM_LR * (m_hat / (jnp.sqrt(v_hat) + ADAM_EPS) + ADAM_WD * w), m, v


def _adamw_slab(w, m, v, own, sib, slab, prev, after, *, name):
    _, rows, cols = w.shape
    tr = _pick(rows, ADAM_ROWS, 16)
    tokens = [] if after is None else [after]

    def body(w_ref, m_ref, v_ref, own_ref, sib_ref, *rest):
        g_ref, d_ref, nm_ref, nv_ref = rest[-4:]
        g = own_ref[0].astype(F32) + sib_ref[0].astype(F32)
        for q in range(1, N_CHIPS):
            g = g + (own_ref[q].astype(F32) + sib_ref[q].astype(F32))
        g_ref[...] = g
        d_ref[...], nm_ref[...], nv_ref[...] = _adam_update(w_ref[...], g, m_ref[...], v_ref[...])

    one = pl.BlockSpec((None, tr, cols), lambda i: (slab, i, 0))
    piece = pl.BlockSpec((N_CHIPS, tr, cols), lambda i: (0, i, 0))
    carried = [] if prev is None else list(prev)
    shape = jax.ShapeDtypeStruct(w.shape, F32)
    return pl.pallas_call(
        body,
        name=name,
        grid=(rows // tr,),
        in_specs=[one] * 3
        + [piece] * 2
        + [pl.BlockSpec(memory_space=pl.ANY)] * len(carried)
        + [pl.BlockSpec(t.shape, lambda i: (0, 0)) for t in tokens],
        out_specs=[one] * 4,
        out_shape=[shape] * 4,
        input_output_aliases={5 + i: i for i in range(len(carried))},
        compiler_params=_params("parallel"),
    )(w, m, v, own, sib, *carried, *tokens)


def _adamw_small(w, g, m, v, *, name):
    def body(w_ref, g_ref, m_ref, v_ref, d_ref, nm_ref, nv_ref):
        d_ref[...], nm_ref[...], nv_ref[...] = _adam_update(w_ref[...], g_ref[...], m_ref[...], v_ref[...])

    shape = jax.ShapeDtypeStruct(w.shape, F32)
    return pl.pallas_call(
        body,
        name=name,
        out_shape=[shape] * 3,
        compiler_params=pltpu.CompilerParams(vmem_limit_bytes=VMEM_LIMIT),
    )(w, g, m, v)


def _position():
    x, y, c = lax.axis_index("x"), lax.axis_index("y"), lax.axis_index("c")
    return x, y, c, 2 * x + y


def _chip_device(chip, c):
    return (chip >> 1, chip & 1, c)


def _handshake(peers):
    barrier = pltpu.get_barrier_semaphore()
    for peer in peers:
        pl.semaphore_signal(barrier, inc=1, device_id=peer, device_id_type=MESH)
    pl.semaphore_wait(barrier, len(peers))


def _gather_shards(shards, split, *, name, collective_id):
    n = len(shards)

    def body(*refs):
        ins, outs = refs[:n], refs[n : 2 * n]
        local_sem, send_sem, recv_sem, pass_send, pass_recv = refs[2 * n :]
        x, y, c, k = _position()
        sibling = (x, y, 1 - c)
        _handshake([sibling] + [_chip_device(k ^ j, c) for j in range(1, N_CHIPS)])

        def rows(a, core):
            r = shards[a].shape[0]
            return pl.ds(core * (r // 2), r // 2) if split[a] else pl.ds(0, r)

        def over_ici(a, j, src_chip, to):
            return pltpu.make_async_remote_copy(
                src_ref=ins[a].at[rows(a, c)],
                dst_ref=outs[a].at[src_chip, rows(a, c)],
                send_sem=send_sem.at[a, j],
                recv_sem=recv_sem.at[a, j],
                device_id=to,
                device_id_type=MESH,
            )

        def to_sibling(a, j, chip, core):
            part = outs[a].at[chip, rows(a, core)]
            return pltpu.make_async_remote_copy(
                src_ref=part,
                dst_ref=part,
                send_sem=pass_send.at[a, j],
                recv_sem=pass_recv.at[a, j],
                device_id=sibling,
                device_id_type=MESH,
            )

        mine = [pltpu.make_async_copy(ins[a], outs[a].at[k], local_sem.at[a]) for a in range(n)]
        for cp in mine:
            cp.start()
        sent = []
        for j in range(N_CHIPS - 1):
            other = k ^ (j + 1)
            for a in range(n):
                cp = over_ici(a, j, k, _chip_device(other, c))
                cp.start()
                sent.append(cp)
        for j in range(N_CHIPS - 1):
            other = k ^ (j + 1)
            for a in range(n):
                over_ici(a, j, other, sibling).wait_recv()
                if split[a]:
                    cp = to_sibling(a, j, other, c)
                    cp.start()
                    sent.append(cp)
        for j in range(N_CHIPS - 1):
            other = k ^ (j + 1)
            for a in range(n):
                if split[a]:
                    to_sibling(a, j, other, 1 - c).wait_recv()
        for cp in sent:
            cp.wait_send()
        for cp in mine:
            cp.wait()

    return pl.kernel(
        body,
        name=name,
        out_type=[jax.ShapeDtypeStruct((N_CHIPS,) + a.shape, a.dtype) for a in shards],
        mesh=plsc.ScalarSubcoreMesh(axis_name="sequencer", num_cores=1),
        scratch_types=[
            pltpu.SemaphoreType.DMA((n,)),
            pltpu.SemaphoreType.DMA((n, N_CHIPS - 1)),
            pltpu.SemaphoreType.DMA((n, N_CHIPS - 1)),
            pltpu.SemaphoreType.DMA((n, N_CHIPS - 1)),
            pltpu.SemaphoreType.DMA((n, N_CHIPS - 1)),
        ],
        compiler_params=pltpu.CompilerParams(collective_id=collective_id),
    )(*shards)


def _scatter_pieces(pieces, *, name, collective_id):
    n = len(pieces)

    def body(*refs):
        ins, own, sib = refs[:n], refs[n : 2 * n], refs[2 * n : 3 * n]
        local_sem, send_sem, recv_sem, pass_send, pass_recv = refs[3 * n :]
        x, y, c, k = _position()
        sibling = (x, y, 1 - c)
        _handshake([sibling] + [_chip_device(k ^ j, c) for j in range(1, N_CHIPS)])

        def over_ici(a, j, piece, slot, to):
            return pltpu.make_async_remote_copy(
                src_ref=ins[a].at[piece],
                dst_ref=own[a].at[slot],
                send_sem=send_sem.at[a, j],
                recv_sem=recv_sem.at[a, j],
                device_id=to,
                device_id_type=MESH,
            )

        def to_sibling(a, j, slot):
            return pltpu.make_async_remote_copy(
                src_ref=own[a].at[slot],
                dst_ref=sib[a].at[slot],
                send_sem=pass_send.at[a, j],
                recv_sem=pass_recv.at[a, j],
                device_id=sibling,
                device_id_type=MESH,
            )

        mine = [pltpu.make_async_copy(ins[a].at[k], own[a].at[k], local_sem.at[a]) for a in range(n)]
        for cp in mine:
            cp.start()
        sent = []
        for j in range(N_CHIPS - 1):
            other = k ^ (j + 1)
            for a in range(n):
                cp = over_ici(a, j, other, k, _chip_device(other, c))
                cp.start()
                sent.append(cp)
        for a in range(n):
            mine[a].wait()
            cp = to_sibling(a, N_CHIPS - 1, k)
            cp.start()
            sent.append(cp)
        for j in range(N_CHIPS - 1):
            other = k ^ (j + 1)
            for a in range(n):
                over_ici(a, j, other, other, sibling).wait_recv()
                cp = to_sibling(a, j, other)
                cp.start()
                sent.append(cp)
        for j in range(N_CHIPS):
            for a in range(n):
                to_sibling(a, j, k).wait_recv()
        for cp in sent:
            cp.wait_send()

    shapes = [jax.ShapeDtypeStruct(a.shape, a.dtype) for a in pieces]
    outs = pl.kernel(
        body,
        name=name,
        out_type=shapes + shapes,
        mesh=plsc.ScalarSubcoreMesh(axis_name="sequencer", num_cores=1),
        scratch_types=[
            pltpu.SemaphoreType.DMA((n,)),
            pltpu.SemaphoreType.DMA((n, N_CHIPS - 1)),
            pltpu.SemaphoreType.DMA((n, N_CHIPS - 1)),
            pltpu.SemaphoreType.DMA((n, N_CHIPS)),
            pltpu.SemaphoreType.DMA((n, N_CHIPS)),
        ],
        compiler_params=pltpu.CompilerParams(collective_id=collective_id),
    )(*pieces)
    return outs[:n], outs[n:]


def _allreduce_small(buf, *, name):
    def body(x_ref, o_ref, slots, send_sem, recv_sem):
        x, y, c, _ = _position()
        me = 4 * x + 2 * y + c
        slots[me] = x_ref[...]

        def copy(d, slot):
            peer = me ^ d
            return pltpu.make_async_remote_copy(
                src_ref=x_ref,
                dst_ref=slots.at[slot],
                send_sem=send_sem.at[d - 1],
                recv_sem=recv_sem.at[d - 1],
                device_id=(peer >> 2, (peer >> 1) & 1, peer & 1),
                device_id_type=MESH,
            )

        sent = [copy(d, me) for d in range(1, N_DEV)]
        for cp in sent:
            cp.start()
        for d in range(1, N_DEV):
            copy(d, me ^ d).wait_recv()
        for cp in sent:
            cp.wait_send()
        acc = slots[0]
        for s in range(1, N_DEV):
            acc = acc + slots[s]
        o_ref[...] = acc

    return pl.pallas_call(
        body,
        name=name,
        in_specs=[pl.BlockSpec(memory_space=pltpu.VMEM)],
        out_specs=pl.BlockSpec(memory_space=pltpu.VMEM),
        out_shape=jax.ShapeDtypeStruct(buf.shape, F32),
        scratch_shapes=[
            pltpu.VMEM((N_DEV,) + buf.shape, F32),
            pltpu.SemaphoreType.DMA((N_DEV - 1,)),
            pltpu.SemaphoreType.DMA((N_DEV - 1,)),
        ],
        compiler_params=pltpu.CompilerParams(vmem_limit_bytes=VMEM_LIMIT),
    )(buf)


def _mixer_of(i):
    return i % N_MIXERS, i // N_MIXERS


def _forward_backward(x, target, norms, layers, send=lambda i, part, pieces: pieces):
    s, d = x.shape
    depth = len(layers)
    heads = d // HEAD_DIM
    hkv = heads // GQA_GROUP
    cos, sin = _rope_tables(s)
    saved = []
    h = x
    for i, lw in enumerate(layers):
        kind, j = _mixer_of(i)
        tag = f"l{i}"
        sv = {"h_in": h}
        a = _rms_fwd(h, norms["mix_norm"][i : i + 1], name=f"{tag}_mix_norm")
        qkv = _mm_nn(a, lw["w_in"], out_dtype=BF16, name=f"{tag}_w_in")
        if kind == 0:
            bias = _na_bias(_na_base(norms["na_rpb"][j]), name=f"{tag}_na_bias")
            o = _na_fwd(qkv, bias, name=f"{tag}_na_fwd")
            sv["bias"] = bias
        elif kind == 1:
            o = _sc_mid_fwd(qkv, lw["sc_conv_w"], name=f"{tag}_sc_fwd")
        else:
            gq, gk = norms["gqa_q_norm"][j : j + 1], norms["gqa_k_norm"][j : j + 1]
            qkn = _gqa_prep_fwd(qkv, gq, gk, cos, sin, heads, hkv, name=f"{tag}_gqa_prep")
            o = _gqa_fwd(qkn, qkv, heads, hkv, name=f"{tag}_gqa_fwd")
            sv["qkn"] = qkn
        h_mid = _mm_nn(o, lw["w_out"], out_dtype=F32, residual=h, name=f"{tag}_w_out")
        b = _rms_fwd(h_mid, norms["ffn_norm"][i : i + 1], name=f"{tag}_ffn_norm")
        up = _mm_nn(b, lw["w_up"], out_dtype=BF16, name=f"{tag}_w_up")
        act = _ffn_mid_fwd(up, lw["ffn_conv_w"], lw["ffn_conv_b"], name=f"{tag}_ffn_fwd")
        h = _mm_nn(act, lw["w_down"], out_dtype=F32, residual=h_mid, name=f"{tag}_w_down")
        sv.update(a=a, qkv=qkv, o=o, h_mid=h_mid, b=b, up=up, act=act)
        saved.append(sv)

    dh, dh_b, d_final, loss = _loss_head(h, norms["final_norm"][None], target, name="loss_head")

    big = [None] * depth
    small = {"final_norm": d_final, "mix_norm": [None] * depth, "ffn_norm": [None] * depth,
             "ffn_conv_w": [None] * depth, "ffn_conv_b": [None] * depth, "na_rpb": {}}
    after = ()
    for i in reversed(range(depth)):
        kind, j = _mixer_of(i)
        tag = f"l{i}b"
        lw, sv = layers[i], saved[i]
        dact = _mm_nt(dh_b, lw["w_down"], out_dtype=BF16, name=f"{tag}_d_act", after=after)
        dw_down = _mm_tn(sv["act"], dh_b, 1, name=f"{tag}_dw_down")
        dug, duu, dwg, dwu, dbg, dbu = _ffn_mid_bwd(
            sv["up"], lw["ffn_conv_w"], lw["ffn_conv_b"], dact, name=f"{tag}_ffn_bwd"
        )
        dup = jnp.concatenate([dug, duu], axis=1)
        small["ffn_conv_w"][i] = jnp.concatenate([dwg, dwu], axis=1)
        small["ffn_conv_b"][i] = jnp.concatenate([dbg, dbu], axis=1)
        dw_up = _mm_tn(sv["b"], dup, N_CHIPS, name=f"{tag}_dw_up")
        sent_ffn = send(i, "ffn", [dw_up, dw_down.reshape(N_CHIPS, dw_down.shape[1] // N_CHIPS, d)])
        db = _mm_nt(dup, lw["w_up"], out_dtype=F32, name=f"{tag}_d_b")
        dh_mid, dh_mid_b, small["ffn_norm"][i] = _rms_bwd(
            sv["h_mid"], norms["ffn_norm"][i : i + 1], db, dh, name=f"{tag}_ffn_norm"
        )
        do = _mm_nt(dh_mid_b, lw["w_out"], out_dtype=BF16, name=f"{tag}_d_o", after=(_token(dw_up), _token(dw_down)))
        dw_out = _mm_tn(sv["o"], dh_mid_b, 1, name=f"{tag}_dw_out")
        if kind == 0:
            dq, dk, dv, dbias = _na_bwd(sv["qkv"], sv["bias"], do, name=f"{tag}_na_bwd")
            dqkv = jnp.concatenate([dq, dk, dv], axis=1)
            small["na_rpb"][j] = _rpb_fold(dbias, name=f"{tag}_rpb_fold")
        elif kind == 1:
            dgb, dgc, dhh, small["sc_conv_w"] = _sc_mid_bwd(sv["qkv"], lw["sc_conv_w"], do, name=f"{tag}_sc_bwd")
            dqkv = jnp.concatenate([dgb, dgc, dhh], axis=1)
        else:
            gq, gk = norms["gqa_q_norm"][j : j + 1], norms["gqa_k_norm"][j : j + 1]
            dqn, dkn, dv = _gqa_bwd(sv["qkn"], sv["qkv"], do, heads, hkv, name=f"{tag}_gqa_bwd")
            dqk, small["gqa_q_norm"], small["gqa_k_norm"] = _gqa_prep_bwd(
                sv["qkv"], gq, gk, cos, sin, dqn, dkn, heads, hkv, name=f"{tag}_gqa_prep_bwd"
            )
            dqkv = jnp.concatenate([dqk, dv], axis=1)
        dw_in = _mm_tn(sv["a"], dqkv, N_CHIPS, name=f"{tag}_dw_in")
        da = _mm_nt(dqkv, lw["w_in"], out_dtype=F32, name=f"{tag}_d_a")
        dh, dh_b, small["mix_norm"][i] = _rms_bwd(
            sv["h_in"], norms["mix_norm"][i : i + 1], da, dh_mid, name=f"{tag}_mix_norm"
        )
        sent_mix = send(i, "mix", [dw_in, dw_out.reshape(N_CHIPS, dw_out.shape[1] // N_CHIPS, d)])
        after = (_token(dw_in), _token(dw_out))
        big[i] = {"mix": sent_mix, "ffn": sent_ffn}
    return loss, dh, big, small


def _pack(parts):
    flat = jnp.concatenate([p.reshape(-1).astype(F32) for p in parts])
    pad = (-flat.shape[0]) % (8 * LANES)
    return jnp.pad(flat, (0, pad)).reshape(-1, LANES)


def _unpack(buf, shapes):
    flat = buf.reshape(-1)
    out, at = [], 0
    for shp in shapes:
        size = 1
        for n in shp:
            size *= n
        out.append(flat[at : at + size].reshape(shp))
        at += size
    return out


def kernel(x, mix_norm, ffn_norm, final_norm, na_w_qkv, na_rpb, na_w_o, sc_w_in, sc_conv_w, sc_w_out, gqa_w_qkv, gqa_q_norm, gqa_k_norm, gqa_w_o, ffn_w_up, ffn_conv_w, ffn_conv_b, ffn_w_down, loss_target, m_mix_norm, m_ffn_norm, m_final_norm, m_na_w_qkv, m_na_rpb, m_na_w_o, m_sc_w_in, m_sc_conv_w, m_sc_w_out, m_gqa_w_qkv, m_gqa_q_norm, m_gqa_k_norm, m_gqa_w_o, m_ffn_w_up, m_ffn_conv_w, m_ffn_conv_b, m_ffn_w_down, v_mix_norm, v_ffn_norm, v_final_norm, v_na_w_qkv, v_na_rpb, v_na_w_o, v_sc_w_in, v_sc_conv_w, v_sc_w_out, v_gqa_w_qkv, v_gqa_q_norm, v_gqa_k_norm, v_gqa_w_o, v_ffn_w_up, v_ffn_conv_w, v_ffn_conv_b, v_ffn_w_down):
    depth, d = mix_norm.shape
    chip = 2 * lax.axis_index("x") + lax.axis_index("y")
    w_in_of = {0: na_w_qkv, 1: sc_w_in, 2: gqa_w_qkv}
    w_out_of = {0: na_w_o, 1: sc_w_out, 2: gqa_w_o}

    layers = []
    for i in range(depth):
        kind, j = _mixer_of(i)
        shards = [_cast_slab(w_in_of[kind], j, name=f"cast_w_in_l{i}"), _cast_slab(w_out_of[kind], j, name=f"cast_w_out_l{i}")]
        split = [True, True]
        if kind == 1:
            shards.append(sc_conv_w[j])
            split.append(False)
        mix = _gather_shards(shards, split, name=f"gather_mix_l{i}", collective_id=1 + 2 * i)
        ffn = _gather_shards(
            [_cast_slab(ffn_w_up, i, name=f"cast_w_up_l{i}"), _cast_slab(ffn_w_down, i, name=f"cast_w_down_l{i}"),
             ffn_conv_w[i]],
            [True, True, False],
            name=f"gather_ffn_l{i}",
            collective_id=2 + 2 * i,
        )
        lw = {
            "w_in": mix[0],
            "w_out": mix[1].reshape(1, -1, d),
            "w_up": ffn[0],
            "w_down": ffn[1].reshape(1, -1, d),
            "ffn_conv_w": ffn[2],
            "ffn_conv_b": ffn_conv_b[i : i + 1],
        }
        if kind == 1:
            lw["sc_conv_w"] = mix[2]
        layers.append(lw)

    norms = dict(mix_norm=mix_norm, ffn_norm=ffn_norm, final_norm=final_norm, na_rpb=na_rpb,
                 gqa_q_norm=gqa_q_norm, gqa_k_norm=gqa_k_norm)

    def send(i, part, pieces):
        cid = 1 + 2 * depth + 2 * i + (part == "mix")
        return _scatter_pieces(pieces, name=f"scatter_{part}_l{i}", collective_id=cid)

    loss, grad_x, big, small = _forward_backward(x[0], loss_target[0], norms, layers, send)

    mixer_names = {0: ("na_w_qkv", "na_w_o"), 1: ("sc_w_in", "sc_w_out"), 2: ("gqa_w_qkv", "gqa_w_o")}
    state = {
        "na_w_qkv": (na_w_qkv, m_na_w_qkv, v_na_w_qkv), "na_w_o": (na_w_o, m_na_w_o, v_na_w_o),
        "sc_w_in": (sc_w_in, m_sc_w_in, v_sc_w_in), "sc_w_out": (sc_w_out, m_sc_w_out, v_sc_w_out),
        "gqa_w_qkv": (gqa_w_qkv, m_gqa_w_qkv, v_gqa_w_qkv), "gqa_w_o": (gqa_w_o, m_gqa_w_o, v_gqa_w_o),
        "ffn_w_up": (ffn_w_up, m_ffn_w_up, v_ffn_w_up), "ffn_w_down": (ffn_w_down, m_ffn_w_down, v_ffn_w_down),
    }
    res = {n: None for n in state}
    token = None
    for i in reversed(range(depth)):
        kind, j = _mixer_of(i)
        for part, names, slab in (("ffn", ("ffn_w_up", "ffn_w_down"), i), ("mix", mixer_names[kind], j)):
            own, sib = big[i][part]
            for slot, n in enumerate(names):
                res[n] = _adamw_slab(*state[n], own[slot], sib[slot], slab, res[n], token, name=f"adamw_{n}_l{i}")
                token = _token(res[n][0][slab])

    n_na = na_rpb.shape[0]
    rpb_flat = jnp.stack([small["na_rpb"][j] for j in range(n_na)])
    full_parts = [
        loss[:, :1],
        jnp.concatenate(small["mix_norm"], axis=0),
        jnp.concatenate(small["ffn_norm"], axis=0),
        small["final_norm"],
        rpb_flat,
        small["sc_conv_w"],
        small["gqa_q_norm"],
        small["gqa_k_norm"],
        jnp.stack(small["ffn_conv_w"]),
        jnp.concatenate(small["ffn_conv_b"], axis=0),
    ]
    summed = _unpack(_allreduce_small(_pack(full_parts), name="allreduce_small"), [p.shape for p in full_parts])
    loss_all, g_mix, g_ffn, g_final, g_rpb, g_sc_cw, g_gq, g_gk, g_ffn_cw, g_ffn_cb = summed
    g_rpb = g_rpb[:, :, : RPB_ROWS * RPB_COLS].reshape(na_rpb.shape)
    g_sc_cw = lax.dynamic_slice_in_dim(g_sc_cw, chip * sc_conv_w.shape[2], sc_conv_w.shape[2], axis=1)[None]
    g_ffn_cw = lax.dynamic_slice_in_dim(g_ffn_cw, chip * ffn_conv_w.shape[2], ffn_conv_w.shape[2], axis=2)
    small_names = ["mix_norm", "ffn_norm", "final_norm", "na_rpb", "sc_conv_w", "gqa_q_norm", "gqa_k_norm",
                   "ffn_conv_w", "ffn_conv_b"]
    small_g = [g_mix, g_ffn, g_final.reshape(final_norm.shape), g_rpb, g_sc_cw, g_gq, g_gk, g_ffn_cw, g_ffn_cb]
    small_w = [mix_norm, ffn_norm, final_norm, na_rpb, sc_conv_w, gqa_q_norm, gqa_k_norm, ffn_conv_w, ffn_conv_b]
    small_m = [m_mix_norm, m_ffn_norm, m_final_norm, m_na_rpb, m_sc_conv_w, m_gqa_q_norm, m_gqa_k_norm,
               m_ffn_conv_w, m_ffn_conv_b]
    small_v = [v_mix_norm, v_ffn_norm, v_final_norm, v_na_rpb, v_sc_conv_w, v_gqa_q_norm, v_gqa_k_norm,
               v_ffn_conv_w, v_ffn_conv_b]
    shapes = [w.shape for w in small_w]
    packed = _adamw_small(_pack(small_w), _pack(small_g), _pack(small_m), _pack(small_v), name="adamw_small")
    small_d, small_nm, small_nv = (_unpack(p, shapes) for p in packed)
    for n, g, dl, nm, nv in zip(small_names, small_g, small_d, small_nm, small_nv):
        res[n] = (g.reshape(dl.shape), dl, nm, nv)

    order = ["mix_norm", "ffn_norm", "final_norm", "na_w_qkv", "na_rpb", "na_w_o", "sc_w_in", "sc_conv_w",
             "sc_w_out", "gqa_w_qkv", "gqa_q_norm", "gqa_k_norm", "gqa_w_o", "ffn_w_up", "ffn_conv_w",
             "ffn_conv_b", "ffn_w_down"]
    outs = [loss_all.reshape(()), grad_x[None]]
    for part in range(4):
        outs.extend(res[n][part] for n in order)
    return tuple(outs)
```

```python
import functools

import jax
import jax.numpy as jnp
from jax import lax
from jax.experimental import pallas as pl
from jax.experimental.pallas import tpu as pltpu
from jax.experimental.pallas import tpu_sc as plsc

F32 = jnp.float32
BF16 = jnp.bfloat16
MESH = pl.DeviceIdType.MESH

N_CHIPS = 4
N_DEV = 8
N_MIXERS = 3
GRID_W = 64
HEAD_DIM = 128
EPS = 1e-6
NEG_INF = -1e30
NA_WIN_R = 8
NA_WIN_C = 16
GQA_GROUP = 4
ROPE_THETA = 10000.0
ADAM_LR = 0.001
ADAM_B1 = 0.9
ADAM_B2 = 0.999
ADAM_EPS = 1e-08
ADAM_WD = 0.01
ADAM_STEP = 10

LANES = 128
VMEM_LIMIT = 48 * 1024 * 1024
NT_DIMS = (((1,), (1,)), ((), ()))
TN_DIMS = (((0,), (0,)), ((), ()))


def _pick(n, cap, mult=LANES):
    best = None
    for t in range(mult, min(n, cap) + 1, mult):
        if n % t == 0:
            best = t
    return best if best is not None else n


def _params(*sem):
    return pltpu.CompilerParams(dimension_semantics=sem, vmem_limit_bytes=VMEM_LIMIT)


MM_VMEM_BUDGET = 40 * 1024 * 1024
MM_CONTRACT = 2816


def _mm_rows(m, blocks_for, mult=16):
    for cap in (1024, 512, 256, 128):
        tm = _pick(m, cap, mult)
        if sum(r * c * b * n for r, c, b, n in blocks_for(tm)) <= MM_VMEM_BUDGET:
            return tm
    return _pick(m, 128, mult)


def _accumulate(acc, step, steps, part, finish):
    if steps == 1:
        finish(part)
        return

    @pl.when(step == 0)
    def _():
        acc[...] = part

    @pl.when(step != 0)
    def _():
        acc[...] += part

    @pl.when(step == steps - 1)
    def _():
        finish(acc[...])


def _mm_nn(a, b, *, out_dtype, name, residual=None):
    m, k = a.shape
    nc, _, ncol = b.shape
    tn, tk = _pick(ncol, 1536), _pick(k, MM_CONTRACT)
    per, nk = ncol // tn, k // tk
    osz = jnp.dtype(out_dtype).itemsize
    tm = _mm_rows(m, lambda t: [(t, tk, a.dtype.itemsize, 2), (tk, tn, 2, 2), (t, tn, osz, 2),
                                (t, tn, 4, 2 * (residual is not None)), (t, tn, 4, nk > 1)])

    def body(a_ref, b_ref, *rest):
        o_ref = rest[-2] if nk > 1 else rest[-1]

        def finish(r):
            if residual is not None:
                r = r + rest[0][...]
            o_ref[...] = r.astype(o_ref.dtype)

        part = jnp.dot(a_ref[...].astype(BF16), b_ref[...], preferred_element_type=F32)
        _accumulate(rest[-1], pl.program_id(2), nk, part, finish)

    in_specs = [
        pl.BlockSpec((tm, tk), lambda i, j, kk: (i, kk)),
        pl.BlockSpec((None, tk, tn), lambda i, j, kk: (j // per, kk, j % per)),
    ]
    ops = [a, b]
    if residual is not None:
        in_specs.append(pl.BlockSpec((tm, tn), lambda i, j, kk: (i, j)))
        ops.append(residual)
    return pl.pallas_call(
        body,
        name=name,
        grid=(m // tm, nc * per, nk),
        in_specs=in_specs,
        out_specs=pl.BlockSpec((tm, tn), lambda i, j, kk: (i, j)),
        out_shape=jax.ShapeDtypeStruct((m, nc * ncol), out_dtype),
        scratch_shapes=[pltpu.VMEM((tm, tn), F32)] * (nk > 1),
        compiler_params=_params("parallel", "parallel", "arbitrary"),
    )(*ops)


def _token(x):
    return x[(0,) * (x.ndim - 2)][:16, :LANES]


def _mm_nt(a, b, *, out_dtype, name, after=()):
    m, n = a.shape
    nc, k, ncol = b.shape
    tko, tn = _pick(k, 1024), _pick(ncol, MM_CONTRACT)
    per, nn = ncol // tn, n // tn
    osz = jnp.dtype(out_dtype).itemsize
    tm = _mm_rows(m, lambda t: [(t, tn, a.dtype.itemsize, 2), (tko, tn, 2, 2), (t, tko, osz, 2), (t, tko, 4, nn > 1)])

    def body(a_ref, b_ref, *rest):
        o_ref = rest[len(after)]

        def finish(r):
            o_ref[...] = r.astype(o_ref.dtype)

        part = lax.dot_general(a_ref[...].astype(BF16), b_ref[...], NT_DIMS, preferred_element_type=F32)
        _accumulate(rest[-1], pl.program_id(2), nn, part, finish)

    return pl.pallas_call(
        body,
        name=name,
        grid=(m // tm, k // tko, nn),
        in_specs=[
            pl.BlockSpec((tm, tn), lambda i, j, s: (i, s)),
            pl.BlockSpec((None, tko, tn), lambda i, j, s: (s // per, j, s % per)),
        ]
        + [pl.BlockSpec(t.shape, lambda i, j, s: (0, 0)) for t in after],
        out_specs=pl.BlockSpec((tm, tko), lambda i, j, s: (i, j)),
        out_shape=jax.ShapeDtypeStruct((m, k), out_dtype),
        scratch_shapes=[pltpu.VMEM((tm, tko), F32)] * (nn > 1),
        compiler_params=_params("parallel", "parallel", "arbitrary"),
    )(a, b, *after)


def _mm_tn(a, g, nc, *, name):
    s, k = a.shape
    n = g.shape[1]
    ncol = n // nc
    ts, tn = _pick(s, MM_CONTRACT, 16), _pick(ncol, 1536)
    per, ns = ncol // tn, s // ts
    tko = _mm_rows(k, lambda t: [(ts, t, a.dtype.itemsize, 2), (ts, tn, g.dtype.itemsize, 2), (t, tn, 2, 2),
                                 (t, tn, 4, ns > 1)], mult=LANES)

    def body(a_ref, g_ref, *rest):
        o_ref = rest[0]

        def finish(r):
            o_ref[...] = r.astype(o_ref.dtype)

        part = lax.dot_general(a_ref[...].astype(BF16), g_ref[...].astype(BF16), TN_DIMS, preferred_element_type=F32)
        _accumulate(rest[-1], pl.program_id(2), ns, part, finish)

    return pl.pallas_call(
        body,
        name=name,
        grid=(k // tko, nc * per, ns),
        in_specs=[
            pl.BlockSpec((ts, tko), lambda i, j, t: (t, i)),
            pl.BlockSpec((ts, tn), lambda i, j, t: (t, j)),
        ],
        out_specs=pl.BlockSpec((None, tko, tn), lambda i, j, t: (j // per, i, j % per)),
        out_shape=jax.ShapeDtypeStruct((nc, k, ncol), BF16),
        scratch_shapes=[pltpu.VMEM((tko, tn), F32)] * (ns > 1),
        compiler_params=_params("parallel", "parallel", "arbitrary"),
    )(a, g)


ROW_TILE = 256


def _cast_slab(w, slab, *, name):
    _, rows, cols = w.shape
    tr = _pick(rows, ROW_TILE, 16)

    def body(w_ref, o_ref):
        o_ref[...] = w_ref[...].astype(o_ref.dtype)

    return pl.pallas_call(
        body,
        name=name,
        grid=(rows // tr,),
        in_specs=[pl.BlockSpec((None, tr, cols), lambda i: (slab, i, 0))],
        out_specs=pl.BlockSpec((tr, cols), lambda i: (i, 0)),
        out_shape=jax.ShapeDtypeStruct((rows, cols), BF16),
        compiler_params=_params("parallel"),
    )(w)


def _rms_fwd(h, g, *, name):
    s, d = h.shape
    tr = _pick(s, ROW_TILE, 16)

    def body(h_ref, g_ref, o_ref):
        x = h_ref[...]
        r = lax.rsqrt(jnp.mean(x * x, axis=-1, keepdims=True) + EPS)
        o_ref[...] = (x * r * g_ref[...]).astype(o_ref.dtype)

    return pl.pallas_call(
        body,
        name=name,
        grid=(s // tr,),
        in_specs=[pl.BlockSpec((tr, d), lambda i: (i, 0)), pl.BlockSpec((1, d), lambda i: (0, 0))],
        out_specs=pl.BlockSpec((tr, d), lambda i: (i, 0)),
        out_shape=jax.ShapeDtypeStruct((s, d), BF16),
        compiler_params=_params("parallel"),
    )(h, g)


def _rms_bwd(h, g, dy, dres, *, name):
    s, d = h.shape
    tr = _pick(s, ROW_TILE, 16)

    def body(h_ref, g_ref, dy_ref, dres_ref, dh_ref, dhb_ref, dg_ref):
        x = h_ref[...]
        r = lax.rsqrt(jnp.mean(x * x, axis=-1, keepdims=True) + EPS)
        xhat = x * r
        dyv = dy_ref[...].astype(F32)
        dyg = dyv * g_ref[...]
        dx = r * (dyg - xhat * jnp.mean(dyg * xhat, axis=-1, keepdims=True))
        dh = dres_ref[...] + dx
        dh_ref[...] = dh
        dhb_ref[...] = dh.astype(dhb_ref.dtype)
        part = jnp.sum(dyv * xhat, axis=0, keepdims=True)

        @pl.when(pl.program_id(0) == 0)
        def _():
            dg_ref[...] = part

        @pl.when(pl.program_id(0) != 0)
        def _():
            dg_ref[...] += part

    row = pl.BlockSpec((tr, d), lambda i: (i, 0))
    vec = pl.BlockSpec((1, d), lambda i: (0, 0))
    return pl.pallas_call(
        body,
        name=name,
        grid=(s // tr,),
        in_specs=[row, vec, row, row],
        out_specs=[row, row, vec],
        out_shape=[jax.ShapeDtypeStruct((s, d), F32), jax.ShapeDtypeStruct((s, d), BF16),
                   jax.ShapeDtypeStruct((1, d), F32)],
        compiler_params=_params("arbitrary"),
    )(h, g, dy, dres)


def _loss_head(h, g, target, *, name):
    s, d = h.shape
    tr = _pick(s, ROW_TILE, 16)

    def body(h_ref, g_ref, t_ref, dh_ref, dhb_ref, dg_ref, loss_ref):
        x = h_ref[...]
        r = lax.rsqrt(jnp.mean(x * x, axis=-1, keepdims=True) + EPS)
        xhat = x * r
        gv = g_ref[...]
        err = xhat * gv - t_ref[...]
        dyv = err * (1.0 / d)
        dyg = dyv * gv
        dh = r * (dyg - xhat * jnp.mean(dyg * xhat, axis=-1, keepdims=True))
        dh_ref[...] = dh
        dhb_ref[...] = dh.astype(dhb_ref.dtype)
        part = jnp.sum(dyv * xhat, axis=0, keepdims=True)
        lpart =jnp.sum(jnp.sum(err * err, axis=-1, keepdims=True), axis=0, keepdims=True) * (0.5 / d)

        @pl.when(pl.program_id(0) == 0)
        def _():
            dg_ref[...] = part
            loss_ref[...] = jnp.broadcast_to(lpart, loss_ref.shape)

        @pl.when(pl.program_id(0) != 0)
        def _():
            dg_ref[...] += part
            loss_ref[...] += jnp.broadcast_to(lpart, loss_ref.shape)

    row = pl.BlockSpec((tr, d), lambda i: (i, 0))
    vec = pl.BlockSpec((1, d), lambda i: (0, 0))
    return pl.pallas_call(
        body,
        name=name,
        grid=(s // tr,),
        in_specs=[row, vec, row],
        out_specs=[row, row, vec, pl.BlockSpec((1, LANES), lambda i: (0, 0))],
        out_shape=[
            jax.ShapeDtypeStruct((s, d), F32),
            jax.ShapeDtypeStruct((s, d), BF16),
            jax.ShapeDtypeStruct((1, d), F32),
            jax.ShapeDtypeStruct((1, LANES), F32),
        ],
        compiler_params=_params("arbitrary"),
    )(h, g, target)


def _shift_prev(x):
    row = lax.broadcasted_iota(jnp.int32, x.shape, 0)
    return jnp.where(row == 0, 0.0, pltpu.roll(x, 1, 0))


def _shift_next(x):
    n = x.shape[0]
    row = lax.broadcasted_iota(jnp.int32, x.shape, 0)
    return jnp.where(row == n - 1, 0.0, pltpu.roll(x, n - 1, 0))


def _conv3(x, w):
    xm, xp = _shift_prev(x), _shift_next(x)
    return xm * w[0:1] + x * w[1:2] + xp * w[2:3], xm, xp


def _conv3_t(d, w):
    return _shift_next(d) * w[0:1] + d * w[1:2] + _shift_prev(d) * w[2:3]


def _colsum(x):
    return jnp.sum(x, axis=0, keepdims=True)


def _ffn_mid_fwd(up, cw, cb, *, name):
    s, f2 = up.shape
    f = f2 // 2
    ncol = cw.shape[2]
    tc = _pick(ncol, 256)
    nt, per = f // tc, ncol // tc

    def body(ug_ref, uu_ref, wg_ref, wu_ref, bg_ref, bu_ref, o_ref):
        cg = _conv3(ug_ref[...].astype(F32), wg_ref[...])[0] + bg_ref[...]
        cu = _conv3(uu_ref[...].astype(F32), wu_ref[...])[0] + bu_ref[...]
        o_ref[...] = (cg * (1.0 / (1.0 + jnp.exp(-cg))) * cu).astype(o_ref.dtype)

    return pl.pallas_call(
        body,
        name=name,
        grid=(nt,),
        in_specs=[
            pl.BlockSpec((s, tc), lambda j: (0, j)),
            pl.BlockSpec((s, tc), lambda j: (0, nt + j)),
            pl.BlockSpec((None, 3, tc), lambda j: (j // per, 0, j % per)),
            pl.BlockSpec((None, 3, tc), lambda j: ((nt + j) // per, 0, (nt + j) % per)),
            pl.BlockSpec((1, tc), lambda j: (0, j)),
            pl.BlockSpec((1, tc), lambda j: (0, nt + j)),
        ],
        out_specs=pl.BlockSpec((s, tc), lambda j: (0, j)),
        out_shape=jax.ShapeDtypeStruct((s, f), BF16),
        compiler_params=_params("parallel"),
    )(up, up, cw, cw, cb, cb)


def _ffn_mid_bwd(up, cw, cb, dact, *, name):
    s, f2 = up.shape
    f = f2 // 2
    ncol = cw.shape[2]
    tc = _pick(ncol, 256)
    nt, per = f // tc, ncol // tc

    def body(ug_ref, uu_ref, wg_ref, wu_ref, bg_ref, bu_ref, da_ref, dug_ref, duu_ref, dwg_ref, dwu_ref, dbg_ref, dbu_ref):
        ug, uu = ug_ref[...].astype(F32), uu_ref[...].astype(F32)
        wg, wu = wg_ref[...], wu_ref[...]
        cg, ugm, ugp = _conv3(ug, wg)
        cu, uum, uup = _conv3(uu, wu)
        cg = cg + bg_ref[...]
        cu = cu + bu_ref[...]
        da = da_ref[...].astype(F32)
        sig = 1.0 / (1.0 + jnp.exp(-cg))
        dcu = da * (cg * sig)
        dcg = da * cu * (sig * (1.0 + cg * (1.0 - sig)))
        dug_ref[...] = _conv3_t(dcg, wg).astype(dug_ref.dtype)
        duu_ref[...] = _conv3_t(dcu, wu).astype(duu_ref.dtype)
        dwg_ref[0:1, :] = _colsum(dcg * ugm)
        dwg_ref[1:2, :] = _colsum(dcg * ug)
        dwg_ref[2:3, :] = _colsum(dcg * ugp)
        dwu_ref[0:1, :] = _colsum(dcu * uum)
        dwu_ref[1:2, :] = _colsum(dcu * uu)
        dwu_ref[2:3, :] = _colsum(dcu * uup)
        dbg_ref[...] = _colsum(dcg)
        dbu_ref[...] = _colsum(dcu)

    col = pl.BlockSpec((s, tc), lambda j: (0, j))
    w3 = pl.BlockSpec((3, tc), lambda j: (0, j))
    b1 = pl.BlockSpec((1, tc), lambda j: (0, j))
    return pl.pallas_call(
        body,
        name=name,
        grid=(nt,),
        in_specs=[
            col,
            pl.BlockSpec((s, tc), lambda j: (0, nt + j)),
            pl.BlockSpec((None, 3, tc), lambda j: (j // per, 0, j % per)),
            pl.BlockSpec((None, 3, tc), lambda j: ((nt + j) // per, 0, (nt + j) % per)),
            b1,
            pl.BlockSpec((1, tc), lambda j: (0, nt + j)),
            col,
        ],
        out_specs=[col, col, w3, w3, b1, b1],
        out_shape=[
            jax.ShapeDtypeStruct((s, f), BF16),
            jax.ShapeDtypeStruct((s, f), BF16),
            jax.ShapeDtypeStruct((3, f), F32),
            jax.ShapeDtypeStruct((3, f), F32),
            jax.ShapeDtypeStruct((1, f), F32),
            jax.ShapeDtypeStruct((1, f), F32),
        ],
        compiler_params=_params("parallel"),
    )(up, up, cw, cw, cb, cb, dact)


def _sc_mid_fwd(z, cw, *, name):
    s, d3 = z.shape
    d = d3 // 3
    ncol = cw.shape[2]
    tc = _pick(ncol, 256)
    nt, per = d // tc, ncol // tc

    def body(gb_ref, gc_ref, hh_ref, w_ref, o_ref):
        p = gc_ref[...].astype(F32) * hh_ref[...].astype(F32)
        o_ref[...] = (gb_ref[...].astype(F32) * _conv3(p, w_ref[...])[0]).astype(o_ref.dtype)

    return pl.pallas_call(
        body,
        name=name,
        grid=(nt,),
        in_specs=[
            pl.BlockSpec((s, tc), lambda j: (0, j)),
            pl.BlockSpec((s, tc), lambda j: (0, nt + j)),
            pl.BlockSpec((s, tc), lambda j: (0, 2 * nt + j)),
            pl.BlockSpec((None, 3, tc), lambda j: (j // per, 0, j % per)),
        ],
        out_specs=pl.BlockSpec((s, tc), lambda j: (0, j)),
        out_shape=jax.ShapeDtypeStruct((s, d), BF16),
        compiler_params=_params("parallel"),
    )(z, z, z, cw)


def _sc_mid_bwd(z, cw, dmid, *, name):
    s, d3 = z.shape
    d = d3 // 3
    ncol = cw.shape[2]
    tc = _pick(ncol, 256)
    nt, per = d // tc, ncol // tc

    def body(gb_ref, gc_ref, hh_ref, w_ref, dm_ref, dgb_ref, dgc_ref, dhh_ref, dw_ref):
        gb, gc, hh = gb_ref[...].astype(F32), gc_ref[...].astype(F32), hh_ref[...].astype(F32)
        w = w_ref[...]
        p = gc * hh
        cv, pm, pp = _conv3(p, w)
        dm = dm_ref[...].astype(F32)
        dgb_ref[...] = (dm * cv).astype(dgb_ref.dtype)
        dcv = dm * gb
        dp = _conv3_t(dcv, w)
        dgc_ref[...] = (dp * hh).astype(dgc_ref.dtype)
        dhh_ref[...] = (dp * gc).astype(dhh_ref.dtype)
        dw_ref[0:1, :] = _colsum(dcv * pm)
        dw_ref[1:2, :] = _colsum(dcv * p)
        dw_ref[2:3, :] = _colsum(dcv * pp)

    col = pl.BlockSpec((s, tc), lambda j: (0, j))
    return pl.pallas_call(
        body,
        name=name,
        grid=(nt,),
        in_specs=[
            col,
            pl.BlockSpec((s, tc), lambda j: (0, nt + j)),
            pl.BlockSpec((s, tc), lambda j: (0, 2 * nt + j)),
            pl.BlockSpec((None, 3, tc), lambda j: (j // per, 0, j % per)),
            col,
        ],
        out_specs=[col, col, col, pl.BlockSpec((3, tc), lambda j: (0, j))],
        out_shape=[jax.ShapeDtypeStruct((s, d), BF16)] * 3 + [jax.ShapeDtypeStruct((3, d), F32)],
        compiler_params=_params("parallel"),
    )(z, z, z, cw, dmid)


NA_KEYS = NA_WIN_R * GRID_W
NA_ROWS_PER_STEP = 8


def _na_row_start(r, rows):
    return jnp.clip(r - NA_WIN_R // 2, 0, rows - NA_WIN_R)


def _na_bias_slot(r, rows):
    return _na_row_start(r, rows) - r + NA_WIN_R - 1


def _na_base(rpb):
    h = rpb.shape[0]
    pos, neg = rpb[:, :, NA_WIN_C - 1:], rpb[:, :, : NA_WIN_C - 1]
    zeros = jnp.zeros((h, NA_WIN_R, GRID_W - 2 * NA_WIN_C + 1), F32)
    out = []
    for first in range(NA_WIN_R):
        p = pos[:, first : first + NA_WIN_R]
        n = jnp.roll(neg[:, first : first + NA_WIN_R], -1, axis=1)
        out.append(jnp.concatenate([p, zeros, n], axis=-1).reshape(h, 1, NA_KEYS))
    return jnp.stack(out, axis=1)


def _skew_right(x):
    return pltpu.roll(x, 0, 1, stride=1, stride_axis=0)


def _skew_left(x):
    n = x.shape[1]
    row = lax.broadcasted_iota(jnp.int32, x.shape, 0)
    for b in range(GRID_W.bit_length() - 1):
        x = jnp.where(((row >> b) & 1) == 1, pltpu.roll(x, n - (1 << b), 1), x)
    return x


def _na_bias(base, *, name):
    h = base.shape[0]

    def body(b_ref, o_ref):
        x = _skew_right(jnp.broadcast_to(b_ref[...], (GRID_W, NA_KEYS)))
        q = lax.broadcasted_iota(jnp.int32, x.shape, 0)
        kc = lax.broadcasted_iota(jnp.int32, x.shape, 1) % GRID_W
        start = jnp.clip(q - NA_WIN_C // 2, 0, GRID_W - NA_WIN_C)
        o_ref[...] = jnp.where((kc >= start) & (kc < start + NA_WIN_C), x, NEG_INF)

    return pl.pallas_call(
        body,
        name=name,
        grid=(h, NA_WIN_R),
        in_specs=[pl.BlockSpec((None, None, 1, NA_KEYS), lambda i, j: (i, j, 0, 0))],
        out_specs=pl.BlockSpec((None, None, GRID_W, NA_KEYS), lambda i, j: (i, j, 0, 0)),
        out_shape=jax.ShapeDtypeStruct((h, NA_WIN_R, GRID_W, NA_KEYS), F32),
        compiler_params=_params("parallel", "parallel"),
    )(base)


def _na_specs(s, heads, per):
    q = pl.BlockSpec((per * GRID_W, HEAD_DIM), lambda h, r: (r, h))
    k = pl.BlockSpec((s, HEAD_DIM), lambda h, r: (0, heads + h))
    v = pl.BlockSpec((s, HEAD_DIM), lambda h, r: (0, 2 * heads + h))
    bias = pl.BlockSpec((None, NA_WIN_R, GRID_W, NA_KEYS), lambda h, r: (h, 0, 0, 0))
    return q, k, v, bias


def _na_probs(q, k, bias):
    sc = lax.dot_general(q, k, NT_DIMS, preferred_element_type=F32) * (HEAD_DIM ** -0.5) + bias
    p = jnp.exp(sc - jnp.max(sc, axis=-1, keepdims=True))
    return p, jnp.sum(p, axis=-1, keepdims=True)


def _na_fwd(qkv, bias, *, name):
    s = qkv.shape[0]
    heads = qkv.shape[1] // (3 * HEAD_DIM)
    rows = s // GRID_W
    per = _pick(rows, NA_ROWS_PER_STEP, 1)

    def body(q_ref, k_ref, v_ref, b_ref, o_ref):
        for i in range(per):
            r = pl.program_id(1) * per + i
            win = pl.ds(pl.multiple_of(_na_row_start(r, rows) * GRID_W, GRID_W), NA_KEYS)
            mine = pl.ds(i * GRID_W, GRID_W)
            p, l = _na_probs(q_ref[mine, :], k_ref[win, :], b_ref[_na_bias_slot(r, rows)])
            o = jnp.dot(p.astype(BF16), v_ref[win, :], preferred_element_type=F32)
            o_ref[mine, :] = (o / l).astype(o_ref.dtype)

    q, k, v, b = _na_specs(s, heads, per)
    return pl.pallas_call(
        body,
        name=name,
        grid=(heads, rows // per),
        in_specs=[q, k, v, b],
        out_specs=q,
        out_shape=jax.ShapeDtypeStruct((s, heads * HEAD_DIM), BF16),
        compiler_params=_params("parallel", "arbitrary"),
    )(qkv, qkv, qkv, bias)


def _na_bwd(qkv, bias, dout, *, name):
    s = qkv.shape[0]
    heads = qkv.shape[1] // (3 * HEAD_DIM)
    rows = s // GRID_W
    per = _pick(rows, NA_ROWS_PER_STEP, 1)
    steps = rows // per
    scale = HEAD_DIM ** -0.5

    def body(q_ref, k_ref, v_ref, b_ref, do_ref, dq_ref, dk_ref, dv_ref, db_ref, dk_acc, dv_acc):
        step = pl.program_id(1)

        @pl.when(step == 0)
        def _():
            dk_acc[...] = jnp.zeros_like(dk_acc)
            dv_acc[...] = jnp.zeros_like(dv_acc)
            db_ref[...] = jnp.zeros_like(db_ref)

        for i in range(per):
            r = step * per + i
            win = pl.ds(pl.multiple_of(_na_row_start(r, rows) * GRID_W, GRID_W), NA_KEYS)
            mine = pl.ds(i * GRID_W, GRID_W)
            slot = _na_bias_slot(r, rows)
            q, k, v, do = q_ref[mine, :], k_ref[win, :], v_ref[win, :], do_ref[mine, :]
            p, l = _na_probs(q, k, b_ref[slot])
            pn = p / l
            dp = lax.dot_general(do, v, NT_DIMS, preferred_element_type=F32)
            ds = pn * (dp - jnp.sum(pn * dp, axis=-1, keepdims=True))
            dsb = ds.astype(BF16)
            dq_ref[mine, :] = (jnp.dot(dsb, k, preferred_element_type=F32) * scale).astype(dq_ref.dtype)
            dk_acc[win, :] += lax.dot_general(dsb, q, TN_DIMS, preferred_element_type=F32) * scale
            dv_acc[win, :] += lax.dot_general(pn.astype(BF16), do, TN_DIMS, preferred_element_type=F32)
            db_ref[slot] += ds

        @pl.when(step == steps - 1)
        def _():
            dk_ref[...] = dk_acc[...].astype(dk_ref.dtype)
            dv_ref[...] = dv_acc[...].astype(dv_ref.dtype)

    q, k, v, b = _na_specs(s, heads, per)
    kv_out = pl.BlockSpec((s, HEAD_DIM), lambda h, r: (0, h))
    shape = jax.ShapeDtypeStruct((s, heads * HEAD_DIM), BF16)
    return pl.pallas_call(
        body,
        name=name,
        grid=(heads, steps),
        in_specs=[q, k, v, b, q],
        out_specs=[q, kv_out, kv_out, b],
        out_shape=[shape, shape, shape, jax.ShapeDtypeStruct((heads, NA_WIN_R, GRID_W, NA_KEYS), F32)],
        scratch_shapes=[pltpu.VMEM((s, HEAD_DIM), F32), pltpu.VMEM((s, HEAD_DIM), F32)],
        compiler_params=_params("parallel", "arbitrary"),
    )(qkv, qkv, qkv, bias, dout)


RPB_ROWS = 2 * NA_WIN_R - 1
RPB_COLS = 2 * NA_WIN_C - 1
RPB_PAD = 512


def _rpb_fold_matrix():
    idx = jnp.arange(NA_WIN_R * NA_KEYS, dtype=jnp.int32)
    first, i, kc = idx // NA_KEYS, (idx // GRID_W) % NA_WIN_R, idx % GRID_W
    pos, neg = kc < NA_WIN_C, kc >= GRID_W - NA_WIN_C + 1
    dr = jnp.where(pos, first + i, first + (i + 1) % NA_WIN_R)
    dc = jnp.where(pos, kc + NA_WIN_C - 1, kc - (GRID_W - NA_WIN_C + 1))
    target = jnp.where(pos | neg, dr * RPB_COLS + dc, -1)
    return (target[:, None] == jnp.arange(RPB_PAD, dtype=jnp.int32)[None, :]).astype(F32)


def _rpb_fold(dbias, *, name):
    h = dbias.shape[0]

    def skew_body(g_ref, o_ref):
        for slot in range(NA_WIN_R):
            o_ref[slot] = _colsum(_skew_left(g_ref[slot]))

    skewed = pl.pallas_call(
        skew_body,
        name=name + "_skew",
        grid=(h,),
        in_specs=[pl.BlockSpec((None, NA_WIN_R, GRID_W, NA_KEYS), lambda i: (i, 0, 0, 0))],
        out_specs=pl.BlockSpec((None, NA_WIN_R, 1, NA_KEYS), lambda i: (i, 0, 0, 0)),
        out_shape=jax.ShapeDtypeStruct((h, NA_WIN_R, 1, NA_KEYS), F32),
        compiler_params=_params("parallel"),
    )(dbias)

    def fold_body(g_ref, m_ref, o_ref):
        o_ref[...] = jnp.dot(g_ref[...], m_ref[...], preferred_element_type=F32, precision=lax.Precision.HIGHEST)

    return pl.pallas_call(
        fold_body,
        name=name,
        out_shape=jax.ShapeDtypeStruct((h, RPB_PAD), F32),
        compiler_params=pltpu.CompilerParams(vmem_limit_bytes=VMEM_LIMIT),
    )(skewed.reshape(h, NA_WIN_R * NA_KEYS), _rpb_fold_matrix())


GQA_Q_TILE = 256


def _rope_tables(s):
    t = jnp.arange(s)
    row = (t // GRID_W).astype(F32)[:, None]
    col = (t % GRID_W).astype(F32)[:, None]
    half = HEAD_DIM // 2
    inv = ROPE_THETA ** (-jnp.arange(0, half, 2, dtype=F32) / half)
    ang = jnp.concatenate([row * inv, row * inv, col * inv, col * inv], axis=-1)
    return jnp.cos(ang), jnp.sin(ang)


def _rot_half(y):
    quarter = HEAD_DIM // 4
    lane = lax.broadcasted_iota(jnp.int32, y.shape, 1)
    low = (lane % (2 * quarter)) < quarter
    return jnp.where(low, -pltpu.roll(y, HEAD_DIM - quarter, 1), pltpu.roll(y, quarter, 1))


def _gqa_prep_fwd(qkv, gq, gk, cos, sin, hq, hkv, *, name):
    s = qkv.shape[0]

    def body(x_ref, gq_ref, gk_ref, cos_ref, sin_ref, o_ref):
        isq = pl.program_id(0) < hq
        x = x_ref[...].astype(F32)
        g = jnp.where(isq, gq_ref[...], gk_ref[...])
        y = x * lax.rsqrt(jnp.mean(x * x, axis=-1, keepdims=True) + EPS) * g
        z = y * cos_ref[...] + _rot_half(y) * sin_ref[...]
        o_ref[...] = (z * jnp.where(isq, HEAD_DIM ** -0.5, 1.0)).astype(o_ref.dtype)

    head = pl.BlockSpec((s, HEAD_DIM), lambda h: (0, h))
    vec = pl.BlockSpec((1, HEAD_DIM), lambda h: (0, 0))
    tab = pl.BlockSpec((s, HEAD_DIM), lambda h: (0, 0))
    return pl.pallas_call(
        body,
        name=name,
        grid=(hq + hkv,),
        in_specs=[head, vec, vec, tab, tab],
        out_specs=head,
        out_shape=jax.ShapeDtypeStruct((s, (hq + hkv) * HEAD_DIM), BF16),
        compiler_params=_params("parallel"),
    )(qkv, gq, gk, cos, sin)


def _gqa_prep_bwd(qkv, gq, gk, cos, sin, dqn, dkn, hq, hkv, *, name):
    s = qkv.shape[0]

    def body(x_ref, gq_ref, gk_ref, cos_ref, sin_ref, dq_ref, dk_ref, dx_ref, dgq_ref, dgk_ref):
        hh = pl.program_id(0)
        isq = hh < hq
        x = x_ref[...].astype(F32)
        g = jnp.where(isq, gq_ref[...], gk_ref[...])
        r = lax.rsqrt(jnp.mean(x * x, axis=-1, keepdims=True) + EPS)
        xhat = x * r
        dz = jnp.where(isq, dq_ref[...].astype(F32) * (HEAD_DIM ** -0.5), dk_ref[...].astype(F32))
        dy = dz * cos_ref[...] - _rot_half(dz * sin_ref[...])
        dyg = dy * g
        dx_ref[...] = (r * (dyg - xhat * jnp.mean(dyg * xhat, axis=-1, keepdims=True))).astype(dx_ref.dtype)
        part = _colsum(dy * xhat)

        @pl.when(hh == 0)
        def _():
            dgq_ref[...] = jnp.zeros_like(dgq_ref)
            dgk_ref[...] = jnp.zeros_like(dgk_ref)

        @pl.when(isq)
        def _():
            dgq_ref[...] += part

        @pl.when(jnp.logical_not(isq))
        def _():
            dgk_ref[...] += part

    head = pl.BlockSpec((s, HEAD_DIM), lambda h: (0, h))
    vec = pl.BlockSpec((1, HEAD_DIM), lambda h: (0, 0))
    tab = pl.BlockSpec((s, HEAD_DIM), lambda h: (0, 0))
    return pl.pallas_call(
        body,
        name=name,
        grid=(hq + hkv,),
        in_specs=[
            head,
            vec,
            vec,
            tab,
            tab,
            pl.BlockSpec((s, HEAD_DIM), lambda h: (0, jnp.minimum(h, hq - 1))),
            pl.BlockSpec((s, HEAD_DIM), lambda h: (0, jnp.maximum(h - hq, 0))),
        ],
        out_specs=[head, vec, vec],
        out_shape=[
            jax.ShapeDtypeStruct((s, (hq + hkv) * HEAD_DIM), BF16),
            jax.ShapeDtypeStruct((1, HEAD_DIM), F32),
            jax.ShapeDtypeStruct((1, HEAD_DIM), F32),
        ],
        compiler_params=_params("arbitrary"),
    )(qkv, gq, gk, cos, sin, dqn, dkn)


def _gqa_fwd(qkn, qkv, hq, hkv, *, name):
    s = qkv.shape[0]
    tq = _pick(s, GQA_Q_TILE, 16)

    def body(q_ref, k_ref, v_ref, o_ref):
        sc = lax.dot_general(q_ref[...], k_ref[...], NT_DIMS, preferred_element_type=F32)
        p = jnp.exp(sc - jnp.max(sc, axis=-1, keepdims=True))
        l = jnp.sum(p, axis=-1, keepdims=True)
        o_ref[...] = (jnp.dot(p.astype(BF16), v_ref[...], preferred_element_type=F32) / l).astype(o_ref.dtype)

    q = pl.BlockSpec((tq, HEAD_DIM), lambda h, i: (i, h))
    return pl.pallas_call(
        body,
        name=name,
        grid=(hq, s // tq),
        in_specs=[
            q,
            pl.BlockSpec((s, HEAD_DIM), lambda h, i: (0, hq + h // GQA_GROUP)),
            pl.BlockSpec((s, HEAD_DIM), lambda h, i: (0, hq + hkv + h // GQA_GROUP)),
        ],
        out_specs=q,
        out_shape=jax.ShapeDtypeStruct((s, hq * HEAD_DIM), BF16),
        compiler_params=_params("parallel", "parallel"),
    )(qkn, qkn, qkv)


def _gqa_bwd(qkn, qkv, dout, hq, hkv, *, name):
    s = qkv.shape[0]
    tq = _pick(s, GQA_Q_TILE, 16)
    nq = s // tq

    def body(q_ref, k_ref, v_ref, do_ref, dq_ref, dk_ref, dv_ref, dk_acc, dv_acc):
        g, i = pl.program_id(1), pl.program_id(2)

        @pl.when((g == 0) & (i == 0))
        def _():
            dk_acc[...] = jnp.zeros_like(dk_acc)
            dv_acc[...] = jnp.zeros_like(dv_acc)

        q, k, v, do = q_ref[...], k_ref[...], v_ref[...], do_ref[...]
        sc = lax.dot_general(q, k, NT_DIMS, preferred_element_type=F32)
        p = jnp.exp(sc - jnp.max(sc, axis=-1, keepdims=True))
        pn = p / jnp.sum(p, axis=-1, keepdims=True)
        dp = lax.dot_general(do, v, NT_DIMS, preferred_element_type=F32)
        dsb = (pn * (dp - jnp.sum(pn * dp, axis=-1, keepdims=True))).astype(BF16)
        dq_ref[...] = jnp.dot(dsb, k, preferred_element_type=F32).astype(dq_ref.dtype)
        dk_acc[...] += lax.dot_general(dsb, q, TN_DIMS, preferred_element_type=F32)
        dv_acc[...] += lax.dot_general(pn.astype(BF16), do, TN_DIMS, preferred_element_type=F32)

        @pl.when((g == GQA_GROUP - 1) & (i == nq - 1))
        def _():
            dk_ref[...] = dk_acc[...].astype(dk_ref.dtype)
            dv_ref[...] = dv_acc[...].astype(dv_ref.dtype)

    q = pl.BlockSpec((tq, HEAD_DIM), lambda kv, g, i: (i, kv * GQA_GROUP + g))
    kv_out = pl.BlockSpec((s, HEAD_DIM), lambda kv, g, i: (0, kv))
    return pl.pallas_call(
        body,
        name=name,
        grid=(hkv, GQA_GROUP, nq),
        in_specs=[
            q,
            pl.BlockSpec((s, HEAD_DIM), lambda kv, g, i: (0, hq + kv)),
            pl.BlockSpec((s, HEAD_DIM), lambda kv, g, i: (0, hq + hkv + kv)),
            q,
        ],
        out_specs=[q, kv_out, kv_out],
        out_shape=[
            jax.ShapeDtypeStruct((s, hq * HEAD_DIM), BF16),
            jax.ShapeDtypeStruct((s, hkv * HEAD_DIM), BF16),
            jax.ShapeDtypeStruct((s, hkv * HEAD_DIM), BF16),
        ],
        scratch_shapes=[pltpu.VMEM((s, HEAD_DIM), F32), pltpu.VMEM((s, HEAD_DIM), F32)],
        compiler_params=_params("parallel", "arbitrary", "arbitrary"),
    )(qkn, qkn, qkv, dout)


ADAM_ROWS = 64


def _adam_update(w, g, m, v):
    m = ADAM_B1 * m + (1.0 - ADAM_B1) * g
    v = ADAM_B2 * v + (1.0 - ADAM_B2) * (g * g)
    m_hat = m / (1.0 - ADAM_B1 ** ADAM_STEP)
    v_hat = v / (1.0 - ADAM_B2 ** ADAM_STEP)
    return -ADAM_LR * (m_hat / (jnp.sqrt(v_hat) + ADAM_EPS) + ADAM_WD * w), m, v


def _adamw_slab(w, m, v, own, sib, slab, prev, after, *, name):
    _, rows, cols = w.shape
    tr = _pick(rows, ADAM_ROWS, 16)
    tokens = [] if after is None else [after]

    def body(w_ref, m_ref, v_ref, own_ref, sib_ref, *rest):
        g_ref, d_ref, nm_ref, nv_ref = rest[-4:]
        g = own_ref[0].astype(F32) + sib_ref[0].astype(F32)
        for q in range(1, N_CHIPS):
            g = g + (own_ref[q].astype(F32) + sib_ref[q].astype(F32))
        g_ref[...] = g
        d_ref[...], nm_ref[...], nv_ref[...] = _adam_update(w_ref[...], g, m_ref[...], v_ref[...])

    one = pl.BlockSpec((None, tr, cols), lambda i: (slab, i, 0))
    piece = pl.BlockSpec((N_CHIPS, tr, cols), lambda i: (0, i, 0))
    carried = [] if prev is None else list(prev)
    shape = jax.ShapeDtypeStruct(w.shape, F32)
    return pl.pallas_call(
        body,
        name=name,
        grid=(rows // tr,),
        in_specs=[one] * 3
        + [piece] * 2
        + [pl.BlockSpec(memory_space=pl.ANY)] * len(carried)
        + [pl.BlockSpec(t.shape, lambda i: (0, 0)) for t in tokens],
        out_specs=[one] * 4,
        out_shape=[shape] * 4,
        input_output_aliases={5 + i: i for i in range(len(carried))},
        compiler_params=_params("parallel"),
    )(w, m, v, own, sib, *carried, *tokens)


def _adamw_small(w, g, m, v, *, name):
    def body(w_ref, g_ref, m_ref, v_ref, d_ref, nm_ref, nv_ref):
        d_ref[...], nm_ref[...], nv_ref[...] = _adam_update(w_ref[...], g_ref[...], m_ref[...], v_ref[...])

    shape = jax.ShapeDtypeStruct(w.shape, F32)
    return pl.pallas_call(
        body,
        name=name,
        out_shape=[shape] * 3,
        compiler_params=pltpu.CompilerParams(vmem_limit_bytes=VMEM_LIMIT),
    )(w, g, m, v)


def _position():
    x, y, c = lax.axis_index("x"), lax.axis_index("y"), lax.axis_index("c")
    return x, y, c, 2 * x + y


def _chip_device(chip, c):
    return (chip >> 1, chip & 1, c)


def _handshake(peers):
    barrier = pltpu.get_barrier_semaphore()
    for peer in peers:
        pl.semaphore_signal(barrier, inc=1, device_id=peer, device_id_type=MESH)
    pl.semaphore_wait(barrier, len(peers))


GATHER_CHUNKS = 2


def _gather_shards(shards, split, *, name, collective_id):
    n = len(shards)
    big = [a for a in range(n) if split[a]]
    small = [a for a in range(n) if not split[a]]
    y_nbr, x_nbr, far = 1, 2, 3

    def body(*refs):
        ins, outs = refs[:n], refs[n : 2 * n]
        near_send, near_recv, far_send, far_recv, pass_send, pass_recv, own_send, own_recv = refs[2 * n :]
        x, y, c, k = _position()
        _handshake([(x, y, 1 - c)] + [_chip_device(k ^ j, c) for j in range(1, N_CHIPS)])

        def own(a, chunk):
            r = shards[a].shape[0]
            part = pl.ds(chunk * (r // GATHER_CHUNKS), r // GATHER_CHUNKS) if split[a] else pl.ds(0, r)
            return pltpu.make_async_remote_copy(
                src_ref=ins[a].at[part],
                dst_ref=outs[a].at[k, part],
                send_sem=own_send.at[a, chunk],
                recv_sem=own_recv.at[a, chunk],
                device_id=(x, y, 1 - c),
                device_id_type=MESH,
            )

        own_copies = [own(a, ch) for a in range(n) for ch in range(GATHER_CHUNKS if split[a] else 1)]
        for cp in own_copies:
            cp.start()

        def run(core):
            sibling = (x, y, 1 - core)
            relay_from, relay_to = (x_nbr, y_nbr) if core == 0 else (y_nbr, x_nbr)

            def rows(a, which, chunk=None):
                r = shards[a].shape[0]
                if not split[a]:
                    return pl.ds(0, r)
                half = r // 2
                if chunk is None:
                    return pl.ds(which * half, half)
                return pl.ds(which * half + chunk * (half // GATHER_CHUNKS), half // GATHER_CHUNKS)

            def direct(a, mask, chunk, src_chip):
                part = rows(a, core, chunk)
                return pltpu.make_async_remote_copy(
                    src_ref=ins[a].at[part],
                    dst_ref=outs[a].at[src_chip, part],
                    send_sem=near_send.at[a, mask - 1, chunk or 0],
                    recv_sem=near_recv.at[a, mask - 1, chunk or 0],
                    device_id=_chip_device(k ^ mask, core),
                    device_id_type=MESH,
                )

            def relay(a, chunk, src_chip, mask):
                part = outs[a].at[src_chip, rows(a, core, chunk)]
                return pltpu.make_async_remote_copy(
                    src_ref=part,
                    dst_ref=part,
                    send_sem=far_send.at[a, chunk or 0],
                    recv_sem=far_recv.at[a, chunk or 0],
                    device_id=_chip_device(k ^ mask, core),
                    device_id_type=MESH,
                )

            def to_sibling(a, mask, which):
                part = outs[a].at[k ^ mask, rows(a, which)]
                return pltpu.make_async_remote_copy(
                    src_ref=part,
                    dst_ref=part,
                    send_sem=pass_send.at[a, mask - 1],
                    recv_sem=pass_recv.at[a, mask - 1],
                    device_id=sibling,
                    device_id_type=MESH,
                )

            sent = []

            def start(cp):
                cp.start()
                sent.append(cp)

            chunks = range(GATHER_CHUNKS)
            for chunk in chunks:
                for a in big:
                    start(direct(a, relay_from, chunk, k))
                    start(direct(a, relay_to, chunk, k))
            for a in small:
                start(direct(a, x_nbr, None, k))
                start(direct(a, y_nbr, None, k))
                start(pltpu.make_async_remote_copy(
                    src_ref=ins[a], dst_ref=outs[a].at[k], send_sem=far_send.at[a, 0], recv_sem=far_recv.at[a, 0],
                    device_id=_chip_device(k ^ far, core), device_id_type=MESH))
            for chunk in chunks:
                for a in big:
                    direct(a, relay_from, chunk, k ^ relay_from).wait_recv()
                    start(relay(a, chunk, k ^ relay_from, relay_to))
            for a in big:
                start(to_sibling(a, relay_from, core))
            for a in big:
                for chunk in chunks:
                    direct(a, relay_to, chunk, k ^ relay_to).wait_recv()
                start(to_sibling(a, relay_to, core))
            for a in big:
                for chunk in chunks:
                    relay(a, chunk, k ^ far, far).wait_recv()
                start(to_sibling(a, far, core))
            for a in small:
                direct(a, x_nbr, None, k ^ x_nbr).wait_recv()
                direct(a, y_nbr, None, k ^ y_nbr).wait_recv()
                relay(a, None, k ^ far, far).wait_recv()
            for a in big:
                for mask in (y_nbr, x_nbr, far):
                    to_sibling(a, mask, 1 - core).wait_recv()
            for cp in sent:
                cp.wait_send()

        for core in (0, 1):
            pl.when(c == core)(functools.partial(run, core))
        for cp in own_copies:
            cp.wait()

    return pl.kernel(
        body,
        name=name,
        out_type=[jax.ShapeDtypeStruct((N_CHIPS,) + a.shape, a.dtype) for a in shards],
        mesh=plsc.ScalarSubcoreMesh(axis_name="sequencer", num_cores=1),
        scratch_types=[
            pltpu.SemaphoreType.DMA((n, 2, GATHER_CHUNKS)),
            pltpu.SemaphoreType.DMA((n, 2, GATHER_CHUNKS)),
            pltpu.SemaphoreType.DMA((n, GATHER_CHUNKS)),
            pltpu.SemaphoreType.DMA((n, GATHER_CHUNKS)),
            pltpu.SemaphoreType.DMA((n, N_CHIPS - 1)),
            pltpu.SemaphoreType.DMA((n, N_CHIPS - 1)),
            pltpu.SemaphoreType.DMA((n, GATHER_CHUNKS)),
            pltpu.SemaphoreType.DMA((n, GATHER_CHUNKS)),
        ],
        compiler_params=pltpu.CompilerParams(collective_id=collective_id),
    )(*shards)


def _scatter_pieces(pieces, *, name, collective_id):
    n = len(pieces)

    def body(*refs):
        ins, own, sib = refs[:n], refs[n : 2 * n], refs[2 * n : 3 * n]
        local_sem, send_sem, recv_sem, pass_send, pass_recv = refs[3 * n :]
        x, y, c, k = _position()
        sibling = (x, y, 1 - c)
        _handshake([sibling] + [_chip_device(k ^ j, c) for j in range(1, N_CHIPS)])

        def over_ici(a, j, piece, slot, to):
            return pltpu.make_async_remote_copy(
                src_ref=ins[a].at[piece],
                dst_ref=own[a].at[slot],
                send_sem=send_sem.at[a, j],
                recv_sem=recv_sem.at[a, j],
                device_id=to,
                device_id_type=MESH,
            )

        def to_sibling(a, j, slot):
            return pltpu.make_async_remote_copy(
                src_ref=own[a].at[slot],
                dst_ref=sib[a].at[slot],
                send_sem=pass_send.at[a, j],
                recv_sem=pass_recv.at[a, j],
                device_id=sibling,
                device_id_type=MESH,
            )

        mine = [pltpu.make_async_copy(ins[a].at[k], own[a].at[k], local_sem.at[a]) for a in range(n)]
        for cp in mine:
            cp.start()
        sent = []
        for j in range(N_CHIPS - 1):
            other = k ^ (j + 1)
            for a in range(n):
                cp = over_ici(a, j, other, k, _chip_device(other, c))
                cp.start()
                sent.append(cp)
        for a in range(n):
            mine[a].wait()
            cp = to_sibling(a, N_CHIPS - 1, k)
            cp.start()
            sent.append(cp)
        for j in range(N_CHIPS - 1):
            other = k ^ (j + 1)
            for a in range(n):
                over_ici(a, j, other, other, sibling).wait_recv()
                cp = to_sibling(a, j, other)
                cp.start()
                sent.append(cp)
        for j in range(N_CHIPS):
            for a in range(n):
                to_sibling(a, j, k).wait_recv()
        for cp in sent:
            cp.wait_send()

    shapes = [jax.ShapeDtypeStruct(a.shape, a.dtype) for a in pieces]
    outs = pl.kernel(
        body,
        name=name,
        out_type=shapes + shapes,
        mesh=plsc.ScalarSubcoreMesh(axis_name="sequencer", num_cores=1),
        scratch_types=[
            pltpu.SemaphoreType.DMA((n,)),
            pltpu.SemaphoreType.DMA((n, N_CHIPS - 1)),
            pltpu.SemaphoreType.DMA((n, N_CHIPS - 1)),
            pltpu.SemaphoreType.DMA((n, N_CHIPS)),
            pltpu.SemaphoreType.DMA((n, N_CHIPS)),
        ],
        compiler_params=pltpu.CompilerParams(collective_id=collective_id),
    )(*pieces)
    return outs[:n], outs[n:]


def _allreduce_small(buf, *, name):
    def body(x_ref, o_ref, slots, send_sem, recv_sem):
        x, y, c, _ = _position()
        me = 4 * x + 2 * y + c
        slots[me] = x_ref[...]

        def copy(d, slot):
            peer = me ^ d
            return pltpu.make_async_remote_copy(
                src_ref=x_ref,
                dst_ref=slots.at[slot],
                send_sem=send_sem.at[d - 1],
                recv_sem=recv_sem.at[d - 1],
                device_id=(peer >> 2, (peer >> 1) & 1, peer & 1),
                device_id_type=MESH,
            )

        sent = [copy(d, me) for d in range(1, N_DEV)]
        for cp in sent:
            cp.start()
        for d in range(1, N_DEV):
            copy(d, me ^ d).wait_recv()
        for cp in sent:
            cp.wait_send()
        acc = slots[0]
        for s in range(1, N_DEV):
            acc = acc + slots[s]
        o_ref[...] = acc

    return pl.pallas_call(
        body,
        name=name,
        in_specs=[pl.BlockSpec(memory_space=pltpu.VMEM)],
        out_specs=pl.BlockSpec(memory_space=pltpu.VMEM),
        out_shape=jax.ShapeDtypeStruct(buf.shape, F32),
        scratch_shapes=[
            pltpu.VMEM((N_DEV,) + buf.shape, F32),
            pltpu.SemaphoreType.DMA((N_DEV - 1,)),
            pltpu.SemaphoreType.DMA((N_DEV - 1,)),
        ],
        compiler_params=pltpu.CompilerParams(vmem_limit_bytes=VMEM_LIMIT),
    )(buf)


def _mixer_of(i):
    return i % N_MIXERS, i // N_MIXERS


def _forward_backward(x, target, norms, layers, send=lambda i, part, pieces: pieces):
    s, d = x.shape
    depth = len(layers)
    heads = d // HEAD_DIM
    hkv = heads // GQA_GROUP
    cos, sin = _rope_tables(s)
    saved = []
    h = x
    for i, lw in enumerate(layers):
        kind, j = _mixer_of(i)
        tag = f"l{i}"
        sv = {"h_in": h}
        a = _rms_fwd(h, norms["mix_norm"][i : i + 1], name=f"{tag}_mix_norm")
        qkv = _mm_nn(a, lw["w_in"], out_dtype=BF16, name=f"{tag}_w_in")
        if kind == 0:
            bias = _na_bias(_na_base(norms["na_rpb"][j]), name=f"{tag}_na_bias")
            o = _na_fwd(qkv, bias, name=f"{tag}_na_fwd")
            sv["bias"] = bias
        elif kind == 1:
            o = _sc_mid_fwd(qkv, lw["sc_conv_w"], name=f"{tag}_sc_fwd")
        else:
            gq, gk = norms["gqa_q_norm"][j : j + 1], norms["gqa_k_norm"][j : j + 1]
            qkn = _gqa_prep_fwd(qkv, gq, gk, cos, sin, heads, hkv, name=f"{tag}_gqa_prep")
            o = _gqa_fwd(qkn, qkv, heads, hkv, name=f"{tag}_gqa_fwd")
            sv["qkn"] = qkn
        h_mid = _mm_nn(o, lw["w_out"], out_dtype=F32, residual=h, name=f"{tag}_w_out")
        b = _rms_fwd(h_mid, norms["ffn_norm"][i : i + 1], name=f"{tag}_ffn_norm")
        up = _mm_nn(b, lw["w_up"], out_dtype=BF16, name=f"{tag}_w_up")
        act = _ffn_mid_fwd(up, lw["ffn_conv_w"], lw["ffn_conv_b"], name=f"{tag}_ffn_fwd")
        h = _mm_nn(act, lw["w_down"], out_dtype=F32, residual=h_mid, name=f"{tag}_w_down")
        sv.update(a=a, qkv=qkv, o=o, h_mid=h_mid, b=b, up=up, act=act)
        saved.append(sv)

    dh, dh_b, d_final, loss = _loss_head(h, norms["final_norm"][None], target, name="loss_head")

    big = [None] * depth
    small = {"final_norm": d_final, "mix_norm": [None] * depth, "ffn_norm": [None] * depth,
             "ffn_conv_w": [None] * depth, "ffn_conv_b": [None] * depth, "na_rpb": {}}
    after = ()
    for i in reversed(range(depth)):
        kind, j = _mixer_of(i)
        tag = f"l{i}b"
        lw, sv = layers[i], saved[i]
        dact = _mm_nt(dh_b, lw["w_down"], out_dtype=BF16, name=f"{tag}_d_act", after=after)
        dw_down = _mm_tn(sv["act"], dh_b, 1, name=f"{tag}_dw_down")
        dug, duu, dwg, dwu, dbg, dbu = _ffn_mid_bwd(
            sv["up"], lw["ffn_conv_w"], lw["ffn_conv_b"], dact, name=f"{tag}_ffn_bwd"
        )
        dup = jnp.concatenate([dug, duu], axis=1)
        small["ffn_conv_w"][i] = jnp.concatenate([dwg, dwu], axis=1)
        small["ffn_conv_b"][i] = jnp.concatenate([dbg, dbu], axis=1)
        dw_up = _mm_tn(sv["b"], dup, N_CHIPS, name=f"{tag}_dw_up")
        sent_ffn = send(i, "ffn", [dw_up, dw_down.reshape(N_CHIPS, dw_down.shape[1] // N_CHIPS, d)])
        db = _mm_nt(dup, lw["w_up"], out_dtype=F32, name=f"{tag}_d_b")
        dh_mid, dh_mid_b, small["ffn_norm"][i] = _rms_bwd(
            sv["h_mid"], norms["ffn_norm"][i : i + 1], db, dh, name=f"{tag}_ffn_norm"
        )
        do = _mm_nt(dh_mid_b, lw["w_out"], out_dtype=BF16, name=f"{tag}_d_o", after=(_token(dw_up), _token(dw_down)))
        dw_out = _mm_tn(sv["o"], dh_mid_b, 1, name=f"{tag}_dw_out")
        if kind == 0:
            dq, dk, dv, dbias = _na_bwd(sv["qkv"], sv["bias"], do, name=f"{tag}_na_bwd")
            dqkv = jnp.concatenate([dq, dk, dv], axis=1)
            small["na_rpb"][j] = _rpb_fold(dbias, name=f"{tag}_rpb_fold")
        elif kind == 1:
            dgb, dgc, dhh, small["sc_conv_w"] = _sc_mid_bwd(sv["qkv"], lw["sc_conv_w"], do, name=f"{tag}_sc_bwd")
            dqkv = jnp.concatenate([dgb, dgc, dhh], axis=1)
        else:
            gq, gk = norms["gqa_q_norm"][j : j + 1], norms["gqa_k_norm"][j : j + 1]
            dqn, dkn, dv = _gqa_bwd(sv["qkn"], sv["qkv"], do, heads, hkv, name=f"{tag}_gqa_bwd")
            dqk, small["gqa_q_norm"], small["gqa_k_norm"] = _gqa_prep_bwd(
                sv["qkv"], gq, gk, cos, sin, dqn, dkn, heads, hkv, name=f"{tag}_gqa_prep_bwd"
            )
            dqkv = jnp.concatenate([dqk, dv], axis=1)
        dw_in = _mm_tn(sv["a"], dqkv, N_CHIPS, name=f"{tag}_dw_in")
        da = _mm_nt(dqkv, lw["w_in"], out_dtype=F32, name=f"{tag}_d_a")
        dh, dh_b, small["mix_norm"][i] = _rms_bwd(
            sv["h_in"], norms["mix_norm"][i : i + 1], da, dh_mid, name=f"{tag}_mix_norm"
        )
        sent_mix = send(i, "mix", [dw_in, dw_out.reshape(N_CHIPS, dw_out.shape[1] // N_CHIPS, d)])
        after = (_token(dw_in), _token(dw_out))
        big[i] = {"mix": sent_mix, "ffn": sent_ffn}
    return loss, dh, big, small


def _pack(parts):
    flat = jnp.concatenate([p.reshape(-1).astype(F32) for p in parts])
    pad = (-flat.shape[0]) % (8 * LANES)
    return jnp.pad(flat, (0, pad)).reshape(-1, LANES)


def _unpack(buf, shapes):
    flat = buf.reshape(-1)
    out, at = [], 0
    for shp in shapes:
        size = 1
        for n in shp:
            size *= n
        out.append(flat[at : at + size].reshape(shp))
        at += size
    return out


def kernel(x, mix_norm, ffn_norm, final_norm, na_w_qkv, na_rpb, na_w_o, sc_w_in, sc_conv_w, sc_w_out, gqa_w_qkv, gqa_q_norm, gqa_k_norm, gqa_w_o, ffn_w_up, ffn_conv_w, ffn_conv_b, ffn_w_down, loss_target, m_mix_norm, m_ffn_norm, m_final_norm, m_na_w_qkv, m_na_rpb, m_na_w_o, m_sc_w_in, m_sc_conv_w, m_sc_w_out, m_gqa_w_qkv, m_gqa_q_norm, m_gqa_k_norm, m_gqa_w_o, m_ffn_w_up, m_ffn_conv_w, m_ffn_conv_b, m_ffn_w_down, v_mix_norm, v_ffn_norm, v_final_norm, v_na_w_qkv, v_na_rpb, v_na_w_o, v_sc_w_in, v_sc_conv_w, v_sc_w_out, v_gqa_w_qkv, v_gqa_q_norm, v_gqa_k_norm, v_gqa_w_o, v_ffn_w_up, v_ffn_conv_w, v_ffn_conv_b, v_ffn_w_down):
    depth, d = mix_norm.shape
    chip = 2 * lax.axis_index("x") + lax.axis_index("y")
    w_in_of = {0: na_w_qkv, 1: sc_w_in, 2: gqa_w_qkv}
    w_out_of = {0: na_w_o, 1: sc_w_out, 2: gqa_w_o}

    layers = []
    for i in range(depth):
        kind, j = _mixer_of(i)
        shards = [_cast_slab(w_in_of[kind], j, name=f"cast_w_in_l{i}"), _cast_slab(w_out_of[kind], j, name=f"cast_w_out_l{i}")]
        split = [True, True]
        if kind == 1:
            shards.append(sc_conv_w[j])
            split.append(False)
        mix = _gather_shards(shards, split, name=f"gather_mix_l{i}", collective_id=1 + 2 * i)
        ffn = _gather_shards(
            [_cast_slab(ffn_w_up, i, name=f"cast_w_up_l{i}"), _cast_slab(ffn_w_down, i, name=f"cast_w_down_l{i}"),
             ffn_conv_w[i]],
            [True, True, False],
            name=f"gather_ffn_l{i}",
            collective_id=2 + 2 * i,
        )
        lw = {
            "w_in": mix[0],
            "w_out": mix[1].reshape(1, -1, d),
            "w_up": ffn[0],
            "w_down": ffn[1].reshape(1, -1, d),
            "ffn_conv_w": ffn[2],
            "ffn_conv_b": ffn_conv_b[i : i + 1],
        }
        if kind == 1:
            lw["sc_conv_w"] = mix[2]
        layers.append(lw)

    norms = dict(mix_norm=mix_norm, ffn_norm=ffn_norm, final_norm=final_norm, na_rpb=na_rpb,
                 gqa_q_norm=gqa_q_norm, gqa_k_norm=gqa_k_norm)

    def send(i, part, pieces):
        cid = 1 + 2 * depth + 2 * i + (part == "mix")
        return _scatter_pieces(pieces, name=f"scatter_{part}_l{i}", collective_id=cid)

    loss, grad_x, big, small = _forward_backward(x[0], loss_target[0], norms, layers, send)

    mixer_names = {0: ("na_w_qkv", "na_w_o"), 1: ("sc_w_in", "sc_w_out"), 2: ("gqa_w_qkv", "gqa_w_o")}
    state = {
        "na_w_qkv": (na_w_qkv, m_na_w_qkv, v_na_w_qkv), "na_w_o": (na_w_o, m_na_w_o, v_na_w_o),
        "sc_w_in": (sc_w_in, m_sc_w_in, v_sc_w_in), "sc_w_out": (sc_w_out, m_sc_w_out, v_sc_w_out),
        "gqa_w_qkv": (gqa_w_qkv, m_gqa_w_qkv, v_gqa_w_qkv), "gqa_w_o": (gqa_w_o, m_gqa_w_o, v_gqa_w_o),
        "ffn_w_up": (ffn_w_up, m_ffn_w_up, v_ffn_w_up), "ffn_w_down": (ffn_w_down, m_ffn_w_down, v_ffn_w_down),
    }
    res = {n: None for n in state}
    token = None
    for i in reversed(range(depth)):
        kind, j = _mixer_of(i)
        for part, names, slab in (("ffn", ("ffn_w_up", "ffn_w_down"), i), ("mix", mixer_names[kind], j)):
            own, sib = big[i][part]
            for slot, n in enumerate(names):
                res[n] = _adamw_slab(*state[n], own[slot], sib[slot], slab, res[n], token, name=f"adamw_{n}_l{i}")
                token = _token(res[n][0][slab])

    n_na = na_rpb.shape[0]
    rpb_flat = jnp.stack([small["na_rpb"][j] for j in range(n_na)])
    full_parts = [
        loss[:, :1],
        jnp.concatenate(small["mix_norm"], axis=0),
        jnp.concatenate(small["ffn_norm"], axis=0),
        small["final_norm"],
        rpb_flat,
        small["sc_conv_w"],
        small["gqa_q_norm"],
        small["gqa_k_norm"],
        jnp.stack(small["ffn_conv_w"]),
        jnp.concatenate(small["ffn_conv_b"], axis=0),
    ]
    summed = _unpack(_allreduce_small(_pack(full_parts), name="allreduce_small"), [p.shape for p in full_parts])
    loss_all, g_mix, g_ffn, g_final, g_rpb, g_sc_cw, g_gq, g_gk, g_ffn_cw, g_ffn_cb = summed
    g_rpb = g_rpb[:, :, : RPB_ROWS * RPB_COLS].reshape(na_rpb.shape)
    g_sc_cw = lax.dynamic_slice_in_dim(g_sc_cw, chip * sc_conv_w.shape[2], sc_conv_w.shape[2], axis=1)[None]
    g_ffn_cw = lax.dynamic_slice_in_dim(g_ffn_cw, chip * ffn_conv_w.shape[2], ffn_conv_w.shape[2], axis=2)
    small_names = ["mix_norm", "ffn_norm", "final_norm", "na_rpb", "sc_conv_w", "gqa_q_norm", "gqa_k_norm",
                   "ffn_conv_w", "ffn_conv_b"]
    small_g = [g_mix, g_ffn, g_final.reshape(final_norm.shape), g_rpb, g_sc_cw, g_gq, g_gk, g_ffn_cw, g_ffn_cb]
    small_w = [mix_norm, ffn_norm, final_norm, na_rpb, sc_conv_w, gqa_q_norm, gqa_k_norm, ffn_conv_w, ffn_conv_b]
    small_m = [m_mix_norm, m_ffn_norm, m_final_norm, m_na_rpb, m_sc_conv_w, m_gqa_q_norm, m_gqa_k_norm,
               m_ffn_conv_w, m_ffn_conv_b]
    small_v = [v_mix_norm, v_ffn_norm, v_final_norm, v_na_rpb, v_sc_conv_w, v_gqa_q_norm, v_gqa_k_norm,
               v_ffn_conv_w, v_ffn_conv_b]
    shapes = [w.shape for w in small_w]
    packed = _adamw_small(_pack(small_w), _pack(small_g), _pack(small_m), _pack(small_v), name="adamw_small")
    small_d, small_nm, small_nv = (_unpack(p, shapes) for p in packed)
    for n, g, dl, nm, nv in zip(small_names, small_g, small_d, small_nm, small_nv):
        res[n] = (g.reshape(dl.shape), dl, nm, nv)

    order = ["mix_norm", "ffn_norm", "final_norm", "na_w_qkv", "na_rpb", "na_w_o", "sc_w_in", "sc_conv_w",
             "sc_w_out", "gqa_w_qkv", "gqa_q_norm", "gqa_k_norm", "gqa_w_o", "ffn_w_up", "ffn_conv_w",
             "ffn_conv_b", "ffn_w_down"]
    outs = [loss_all.reshape(()), grad_x[None]]
    for part in range(4):
        outs.extend(res[n][part] for n in order)
    return tuple(outs)
```

```python
import functools
import math

import jax
import jax.numpy as jnp
from jax import lax
from jax.experimental import pallas as pl
from jax.experimental.pallas import tpu as pltpu
from jax.experimental.pallas import tpu_sc as plsc

F32 = jnp.float32
BF16 = jnp.bfloat16
MESH = pl.DeviceIdType.MESH

N_CHIPS = 4
N_DEV = 8
N_MIXERS = 3
GRID_W = 64
HEAD_DIM = 128
EPS = 1e-6
NEG_INF = -1e30
NA_WIN_R = 8
NA_WIN_C = 16
GQA_GROUP = 4
ROPE_THETA = 10000.0
ADAM_LR = 0.001
ADAM_B1 = 0.9
ADAM_B2 = 0.999
ADAM_EPS = 1e-08
ADAM_WD = 0.01
ADAM_STEP = 10

LANES = 128
VMEM_LIMIT = 48 * 1024 * 1024
NT_DIMS = (((1,), (1,)), ((), ()))
TN_DIMS = (((0,), (0,)), ((), ()))


def _pick(n, cap, mult=LANES):
    best = None
    for t in range(mult, min(n, cap) + 1, mult):
        if n % t == 0:
            best = t
    return best if best is not None else n


def _params(*sem):
    return pltpu.CompilerParams(dimension_semantics=sem, vmem_limit_bytes=VMEM_LIMIT)


MM_VMEM_BUDGET = 40 * 1024 * 1024
MM_CONTRACT = 2816


def _mm_rows(m, blocks_for, mult=16):
    for cap in (1024, 512, 256, 128):
        tm = _pick(m, cap, mult)
        if sum(r * c * b * n for r, c, b, n in blocks_for(tm)) <= MM_VMEM_BUDGET:
            return tm
    return _pick(m, 128, mult)


def _accumulate(acc, step, steps, part, finish):
    if steps == 1:
        finish(part)
        return

    @pl.when(step == 0)
    def _():
        acc[...] = part

    @pl.when(step != 0)
    def _():
        acc[...] += part

    @pl.when(step == steps - 1)
    def _():
        finish(acc[...])


def _mm_nn(a, b, *, out_dtype, name, residual=None):
    m, k = a.shape
    nc, _, ncol = b.shape
    tn, tk = _pick(ncol, 1536), _pick(k, MM_CONTRACT)
    per, nk = ncol // tn, k // tk
    osz = jnp.dtype(out_dtype).itemsize
    tm = _mm_rows(m, lambda t: [(t, tk, a.dtype.itemsize, 2), (tk, tn, 2, 2), (t, tn, osz, 2),
                                (t, tn, 4, 2 * (residual is not None)), (t, tn, 4, nk > 1)])

    def body(a_ref, b_ref, *rest):
        o_ref = rest[-2] if nk > 1 else rest[-1]

        def finish(r):
            if residual is not None:
                r = r + rest[0][...]
            o_ref[...] = r.astype(o_ref.dtype)

        part = jnp.dot(a_ref[...].astype(BF16), b_ref[...], preferred_element_type=F32)
        _accumulate(rest[-1], pl.program_id(2), nk, part, finish)

    in_specs = [
        pl.BlockSpec((tm, tk), lambda i, j, kk: (i, kk)),
        pl.BlockSpec((None, tk, tn), lambda i, j, kk: (j // per, kk, j % per)),
    ]
    ops = [a, b]
    if residual is not None:
        in_specs.append(pl.BlockSpec((tm, tn), lambda i, j, kk: (i, j)))
        ops.append(residual)
    return pl.pallas_call(
        body,
        name=name,
        grid=(m // tm, nc * per, nk),
        in_specs=in_specs,
        out_specs=pl.BlockSpec((tm, tn), lambda i, j, kk: (i, j)),
        out_shape=jax.ShapeDtypeStruct((m, nc * ncol), out_dtype),
        scratch_shapes=[pltpu.VMEM((tm, tn), F32)] * (nk > 1),
        compiler_params=_params("parallel", "parallel", "arbitrary"),
    )(*ops)


def _token(x):
    return x[(0,) * (x.ndim - 2)][:16, :LANES]


def _mm_nt(a, b, *, out_dtype, name, after=()):
    parts = tuple(a) if isinstance(a, (tuple, list)) else (a,)
    m, width = parts[0].shape
    nc, k, ncol = b.shape
    tko, tn = _pick(k, 1024), _pick(math.gcd(ncol, width), MM_CONTRACT)
    per, each, nn = ncol // tn, width // tn, len(parts) * width // tn
    osz = jnp.dtype(out_dtype).itemsize
    tm = _mm_rows(m, lambda t: [(t, tn, parts[0].dtype.itemsize, 2 * len(parts)), (tko, tn, 2, 2), (t, tko, osz, 2),
                                (t, tko, 4, nn > 1)])

    def body(*refs):
        a_refs, b_ref, rest = refs[: len(parts)], refs[len(parts)], refs[len(parts) + 1 :]
        o_ref = rest[len(after)]
        step = pl.program_id(2)

        def finish(r):
            o_ref[...] = r.astype(o_ref.dtype)

        def use(a_ref):
            part = lax.dot_general(a_ref[...].astype(BF16), b_ref[...], NT_DIMS, preferred_element_type=F32)
            _accumulate(rest[-1], step, nn, part, finish)

        _for_part(step // each, a_refs, use)

    return pl.pallas_call(
        body,
        name=name,
        grid=(m // tm, k // tko, nn),
        in_specs=[_part_spec((tm, tn), p, each, lambda i, j, s: (i, s)) for p in range(len(parts))]
        + [pl.BlockSpec((None, tko, tn), lambda i, j, s: (s // per, j, s % per))]
        + [pl.BlockSpec(t.shape, lambda i, j, s: (0, 0)) for t in after],
        out_specs=pl.BlockSpec((tm, tko), lambda i, j, s: (i, j)),
        out_shape=jax.ShapeDtypeStruct((m, k), out_dtype),
        scratch_shapes=[pltpu.VMEM((tm, tko), F32)] * (nn > 1),
        compiler_params=_params("parallel", "parallel", "arbitrary"),
    )(*parts, b, *after)


def _for_part(which, refs, use):
    if len(refs) == 1:
        use(refs[0])
        return
    for p, ref in enumerate(refs):
        pl.when(which == p)(functools.partial(use, ref))


def _part_spec(block, p, each, tile_of):
    def index(*ids):
        r, c = tile_of(*ids)
        return r, jnp.clip(c - p * each, 0, each - 1)

    return pl.BlockSpec(block, index)


def _mm_tn(a, g, nc, *, name):
    parts = tuple(g) if isinstance(g, (tuple, list)) else (g,)
    s, k = a.shape
    width = parts[0].shape[1]
    ncol = len(parts) * width // nc
    ts, tn = _pick(s, MM_CONTRACT, 16), _pick(math.gcd(ncol, width), 1536)
    per, each, ns = ncol // tn, width // tn, s // ts
    tko = _mm_rows(k, lambda t: [(ts, t, a.dtype.itemsize, 2), (ts, tn, parts[0].dtype.itemsize, 2 * len(parts)),
                                 (t, tn, 2, 2), (t, tn, 4, ns > 1)], mult=LANES)

    def body(a_ref, *refs):
        g_refs, rest = refs[: len(parts)], refs[len(parts) :]
        o_ref = rest[0]

        def finish(r):
            o_ref[...] = r.astype(o_ref.dtype)

        def use(g_ref):
            part = lax.dot_general(a_ref[...].astype(BF16), g_ref[...].astype(BF16), TN_DIMS, preferred_element_type=F32)
            _accumulate(rest[-1], pl.program_id(2), ns, part, finish)

        _for_part(pl.program_id(1) // each, g_refs, use)

    return pl.pallas_call(
        body,
        name=name,
        grid=(k // tko, nc * per, ns),
        in_specs=[pl.BlockSpec((ts, tko), lambda i, j, t: (t, i))]
        + [_part_spec((ts, tn), p, each, lambda i, j, t: (t, j)) for p in range(len(parts))],
        out_specs=pl.BlockSpec((None, tko, tn), lambda i, j, t: (j // per, i, j % per)),
        out_shape=jax.ShapeDtypeStruct((nc, k, ncol), BF16),
        scratch_shapes=[pltpu.VMEM((tko, tn), F32)] * (ns > 1),
        compiler_params=_params("parallel", "parallel", "arbitrary"),
    )(a, *parts)


ROW_TILE = 256


def _cast_slab(w, slab, *, name):
    _, rows, cols = w.shape
    tr = _pick(rows, ROW_TILE, 16)

    def body(w_ref, o_ref):
        o_ref[...] = w_ref[...].astype(o_ref.dtype)

    return pl.pallas_call(
        body,
        name=name,
        grid=(rows // tr,),
        in_specs=[pl.BlockSpec((None, tr, cols), lambda i: (slab, i, 0))],
        out_specs=pl.BlockSpec((tr, cols), lambda i: (i, 0)),
        out_shape=jax.ShapeDtypeStruct((rows, cols), BF16),
        compiler_params=_params("parallel"),
    )(w)


def _rms_fwd(h, g, *, name):
    s, d = h.shape
    tr = _pick(s, ROW_TILE, 16)

    def body(h_ref, g_ref, o_ref):
        x = h_ref[...]
        r = lax.rsqrt(jnp.mean(x * x, axis=-1, keepdims=True) + EPS)
        o_ref[...] = (x * r * g_ref[...]).astype(o_ref.dtype)

    return pl.pallas_call(
        body,
        name=name,
        grid=(s // tr,),
        in_specs=[pl.BlockSpec((tr, d), lambda i: (i, 0)), pl.BlockSpec((1, d), lambda i: (0, 0))],
        out_specs=pl.BlockSpec((tr, d), lambda i: (i, 0)),
        out_shape=jax.ShapeDtypeStruct((s, d), BF16),
        compiler_params=_params("parallel"),
    )(h, g)


def _rms_bwd(h, g, dy, dres, *, name):
    s, d = h.shape
    tr = _pick(s, ROW_TILE, 16)

    def body(h_ref, g_ref, dy_ref, dres_ref, dh_ref, dhb_ref, dg_ref):
        x = h_ref[...]
        r = lax.rsqrt(jnp.mean(x * x, axis=-1, keepdims=True) + EPS)
        xhat = x * r
        dyv = dy_ref[...].astype(F32)
        dyg = dyv * g_ref[...]
        dx = r * (dyg - xhat * jnp.mean(dyg * xhat, axis=-1, keepdims=True))
        dh = dres_ref[...] + dx
        dh_ref[...] = dh
        dhb_ref[...] = dh.astype(dhb_ref.dtype)
        part = jnp.sum(dyv * xhat, axis=0, keepdims=True)

        @pl.when(pl.program_id(0) == 0)
        def _():
            dg_ref[...] = part

        @pl.when(pl.program_id(0) != 0)
        def _():
            dg_ref[...] += part

    row = pl.BlockSpec((tr, d), lambda i: (i, 0))
    vec = pl.BlockSpec((1, d), lambda i: (0, 0))
    return pl.pallas_call(
        body,
        name=name,
        grid=(s // tr,),
        in_specs=[row, vec, row, row],
        out_specs=[row, row, vec],
        out_shape=[jax.ShapeDtypeStruct((s, d), F32), jax.ShapeDtypeStruct((s, d), BF16),
                   jax.ShapeDtypeStruct((1, d), F32)],
        compiler_params=_params("arbitrary"),
    )(h, g, dy, dres)


def _loss_head(h, g, target, *, name):
    s, d = h.shape
    tr = _pick(s, ROW_TILE, 16)

    def body(h_ref, g_ref, t_ref, dh_ref, dhb_ref, dg_ref, loss_ref):
        x = h_ref[...]
        r = lax.rsqrt(jnp.mean(x * x, axis=-1, keepdims=True) + EPS)
        xhat = x * r
        gv = g_ref[...]
        err = xhat * gv - t_ref[...]
        dyv = err * (1.0 / d)
        dyg = dyv * gv
        dh = r * (dyg - xhat * jnp.mean(dyg * xhat, axis=-1, keepdims=True))
        dh_ref[...] = dh
        dhb_ref[...] = dh.astype(dhb_ref.dtype)
        part = jnp.sum(dyv * xhat, axis=0, keepdims=True)
        lpart =jnp.sum(jnp.sum(err * err, axis=-1, keepdims=True), axis=0, keepdims=True) * (0.5 / d)

        @pl.when(pl.program_id(0) == 0)
        def _():
            dg_ref[...] = part
            loss_ref[...] = jnp.broadcast_to(lpart, loss_ref.shape)

        @pl.when(pl.program_id(0) != 0)
        def _():
            dg_ref[...] += part
            loss_ref[...] += jnp.broadcast_to(lpart, loss_ref.shape)

    row = pl.BlockSpec((tr, d), lambda i: (i, 0))
    vec = pl.BlockSpec((1, d), lambda i: (0, 0))
    return pl.pallas_call(
        body,
        name=name,
        grid=(s // tr,),
        in_specs=[row, vec, row],
        out_specs=[row, row, vec, pl.BlockSpec((1, LANES), lambda i: (0, 0))],
        out_shape=[
            jax.ShapeDtypeStruct((s, d), F32),
            jax.ShapeDtypeStruct((s, d), BF16),
            jax.ShapeDtypeStruct((1, d), F32),
            jax.ShapeDtypeStruct((1, LANES), F32),
        ],
        compiler_params=_params("arbitrary"),
    )(h, g, target)


def _shift_prev(x):
    row = lax.broadcasted_iota(jnp.int32, x.shape, 0)
    return jnp.where(row == 0, 0.0, pltpu.roll(x, 1, 0))


def _shift_next(x):
    n = x.shape[0]
    row = lax.broadcasted_iota(jnp.int32, x.shape, 0)
    return jnp.where(row == n - 1, 0.0, pltpu.roll(x, n - 1, 0))


def _conv3(x, w):
    xm, xp = _shift_prev(x), _shift_next(x)
    return xm * w[0:1] + x * w[1:2] + xp * w[2:3], xm, xp


def _conv3_t(d, w):
    return _shift_next(d) * w[0:1] + d * w[1:2] + _shift_prev(d) * w[2:3]


def _colsum(x):
    return jnp.sum(x, axis=0, keepdims=True)


def _ffn_mid_fwd(up, cw, cb, *, name):
    s, f2 = up.shape
    f = f2 // 2
    ncol = cw.shape[2]
    tc = _pick(ncol, 256)
    nt, per = f // tc, ncol // tc

    def body(ug_ref, uu_ref, wg_ref, wu_ref, bg_ref, bu_ref, o_ref, cg_ref, cu_ref):
        cg = _conv3(ug_ref[...].astype(F32), wg_ref[...])[0] + bg_ref[...]
        cu = _conv3(uu_ref[...].astype(F32), wu_ref[...])[0] + bu_ref[...]
        o_ref[...] = (cg * (1.0 / (1.0 + jnp.exp(-cg))) * cu).astype(o_ref.dtype)
        cg_ref[...] = cg.astype(cg_ref.dtype)
        cu_ref[...] = cu.astype(cu_ref.dtype)

    out = pl.BlockSpec((s, tc), lambda j: (0, j))
    return pl.pallas_call(
        body,
        name=name,
        grid=(nt,),
        in_specs=[
            pl.BlockSpec((s, tc), lambda j: (0, j)),
            pl.BlockSpec((s, tc), lambda j: (0, nt + j)),
            pl.BlockSpec((None, 3, tc), lambda j: (j // per, 0, j % per)),
            pl.BlockSpec((None, 3, tc), lambda j: ((nt + j) // per, 0, (nt + j) % per)),
            pl.BlockSpec((1, tc), lambda j: (0, j)),
            pl.BlockSpec((1, tc), lambda j: (0, nt + j)),
        ],
        out_specs=[out, out, out],
        out_shape=[jax.ShapeDtypeStruct((s, f), BF16)] * 3,
        compiler_params=_params("parallel"),
    )(up, up, cw, cw, cb, cb)


def _ffn_mid_bwd(up, cg, cu, cw, dact, *, name):
    s, f2 = up.shape
    f = f2 // 2
    ncol = cw.shape[2]
    tc = _pick(ncol, 256)
    nt, per = f // tc, ncol // tc

    def side(dc, u_ref, w_ref, du_ref, dw_ref, db_ref):
        w, u = w_ref[...], u_ref[...].astype(F32)
        nxt, prv = _shift_next(dc), _shift_prev(dc)
        du_ref[...] = (nxt * w[0:1] + dc * w[1:2] + prv * w[2:3]).astype(du_ref.dtype)
        dw_ref[0:1, :] = _colsum(nxt * u)
        dw_ref[1:2, :] = _colsum(dc * u)
        dw_ref[2:3, :] = _colsum(prv * u)
        db_ref[...] = _colsum(dc)

    def body(ug_ref, uu_ref, cg_ref, cu_ref, wg_ref, wu_ref, da_ref, dug_ref, duu_ref, dwg_ref, dwu_ref, dbg_ref, dbu_ref):
        cgv, cuv, da = cg_ref[...].astype(F32), cu_ref[...].astype(F32), da_ref[...].astype(F32)
        sig = 1.0 / (1.0 + jnp.exp(-cgv))
        side(da * cuv * (sig * (1.0 + cgv * (1.0 - sig))), ug_ref, wg_ref, dug_ref, dwg_ref, dbg_ref)
        side(da * (cgv * sig), uu_ref, wu_ref, duu_ref, dwu_ref, dbu_ref)

    col = pl.BlockSpec((s, tc), lambda j: (0, j))
    w3 = pl.BlockSpec((3, tc), lambda j: (0, j))
    b1 = pl.BlockSpec((1, tc), lambda j: (0, j))
    return pl.pallas_call(
        body,
        name=name,
        grid=(nt,),
        in_specs=[
            col,
            pl.BlockSpec((s, tc), lambda j: (0, nt + j)),
            col,
            col,
            pl.BlockSpec((None, 3, tc), lambda j: (j // per, 0, j % per)),
            pl.BlockSpec((None, 3, tc), lambda j: ((nt + j) // per, 0, (nt + j) % per)),
            col,
        ],
        out_specs=[col, col, w3, w3, b1, b1],
        out_shape=[
            jax.ShapeDtypeStruct((s, f), BF16),
            jax.ShapeDtypeStruct((s, f), BF16),
            jax.ShapeDtypeStruct((3, f), F32),
            jax.ShapeDtypeStruct((3, f), F32),
            jax.ShapeDtypeStruct((1, f), F32),
            jax.ShapeDtypeStruct((1, f), F32),
        ],
        compiler_params=_params("parallel"),
    )(up, up, cg, cu, cw, cw, dact)


def _sc_mid_fwd(z, cw, *, name):
    s, d3 = z.shape
    d = d3 // 3
    ncol = cw.shape[2]
    tc = _pick(ncol, 256)
    nt, per = d // tc, ncol // tc

    def body(gb_ref, gc_ref, hh_ref, w_ref, o_ref):
        p = gc_ref[...].astype(F32) * hh_ref[...].astype(F32)
        o_ref[...] = (gb_ref[...].astype(F32) * _conv3(p, w_ref[...])[0]).astype(o_ref.dtype)

    return pl.pallas_call(
        body,
        name=name,
        grid=(nt,),
        in_specs=[
            pl.BlockSpec((s, tc), lambda j: (0, j)),
            pl.BlockSpec((s, tc), lambda j: (0, nt + j)),
            pl.BlockSpec((s, tc), lambda j: (0, 2 * nt + j)),
            pl.BlockSpec((None, 3, tc), lambda j: (j // per, 0, j % per)),
        ],
        out_specs=pl.BlockSpec((s, tc), lambda j: (0, j)),
        out_shape=jax.ShapeDtypeStruct((s, d), BF16),
        compiler_params=_params("parallel"),
    )(z, z, z, cw)


def _sc_mid_bwd(z, cw, dmid, *, name):
    s, d3 = z.shape
    d = d3 // 3
    ncol = cw.shape[2]
    tc = _pick(ncol, 256)
    nt, per = d // tc, ncol // tc

    def body(gb_ref, gc_ref, hh_ref, w_ref, dm_ref, dgb_ref, dgc_ref, dhh_ref, dw_ref):
        gb, gc, hh = gb_ref[...].astype(F32), gc_ref[...].astype(F32), hh_ref[...].astype(F32)
        w = w_ref[...]
        p = gc * hh
        cv, pm, pp = _conv3(p, w)
        dm = dm_ref[...].astype(F32)
        dgb_ref[...] = (dm * cv).astype(dgb_ref.dtype)
        dcv = dm * gb
        dp = _conv3_t(dcv, w)
        dgc_ref[...] = (dp * hh).astype(dgc_ref.dtype)
        dhh_ref[...] = (dp * gc).astype(dhh_ref.dtype)
        dw_ref[0:1, :] = _colsum(dcv * pm)
        dw_ref[1:2, :] = _colsum(dcv * p)
        dw_ref[2:3, :] = _colsum(dcv * pp)

    col = pl.BlockSpec((s, tc), lambda j: (0, j))
    return pl.pallas_call(
        body,
        name=name,
        grid=(nt,),
        in_specs=[
            col,
            pl.BlockSpec((s, tc), lambda j: (0, nt + j)),
            pl.BlockSpec((s, tc), lambda j: (0, 2 * nt + j)),
            pl.BlockSpec((None, 3, tc), lambda j: (j // per, 0, j % per)),
            col,
        ],
        out_specs=[col, col, col, pl.BlockSpec((3, tc), lambda j: (0, j))],
        out_shape=[jax.ShapeDtypeStruct((s, d), BF16)] * 3 + [jax.ShapeDtypeStruct((3, d), F32)],
        compiler_params=_params("parallel"),
    )(z, z, z, cw, dmid)


NA_KEYS = NA_WIN_R * GRID_W
NA_ROWS_PER_STEP = 8


def _na_row_start(r, rows):
    return jnp.clip(r - NA_WIN_R // 2, 0, rows - NA_WIN_R)


def _na_bias_slot(r, rows):
    return _na_row_start(r, rows) - r + NA_WIN_R - 1


def _na_base(rpb):
    h = rpb.shape[0]
    pos, neg = rpb[:, :, NA_WIN_C - 1:], rpb[:, :, : NA_WIN_C - 1]
    zeros = jnp.zeros((h, NA_WIN_R, GRID_W - 2 * NA_WIN_C + 1), F32)
    out = []
    for first in range(NA_WIN_R):
        p = pos[:, first : first + NA_WIN_R]
        n = jnp.roll(neg[:, first : first + NA_WIN_R], -1, axis=1)
        out.append(jnp.concatenate([p, zeros, n], axis=-1).reshape(h, 1, NA_KEYS))
    return jnp.stack(out, axis=1)


def _skew_right(x):
    return pltpu.roll(x, 0, 1, stride=1, stride_axis=0)


def _skew_left(x):
    n = x.shape[1]
    row = lax.broadcasted_iota(jnp.int32, x.shape, 0)
    for b in range(GRID_W.bit_length() - 1):
        x = jnp.where(((row >> b) & 1) == 1, pltpu.roll(x, n - (1 << b), 1), x)
    return x


def _na_bias(base, *, name):
    h = base.shape[0]

    def body(b_ref, o_ref):
        q = lax.broadcasted_iota(jnp.int32, (GRID_W, NA_KEYS), 0)
        kc = lax.broadcasted_iota(jnp.int32, (GRID_W, NA_KEYS), 1) % GRID_W
        start = jnp.clip(q - NA_WIN_C // 2, 0, GRID_W - NA_WIN_C)
        inside = (kc >= start) & (kc < start + NA_WIN_C)
        for slot in range(NA_WIN_R):
            x = _skew_right(jnp.broadcast_to(b_ref[slot], (GRID_W, NA_KEYS)))
            o_ref[slot] = jnp.where(inside, x, NEG_INF)

    return pl.pallas_call(
        body,
        name=name,
        grid=(h,),
        in_specs=[pl.BlockSpec((None, NA_WIN_R, 1, NA_KEYS), lambda i: (i, 0, 0, 0))],
        out_specs=pl.BlockSpec((None, NA_WIN_R, GRID_W, NA_KEYS), lambda i: (i, 0, 0, 0)),
        out_shape=jax.ShapeDtypeStruct((h, NA_WIN_R, GRID_W, NA_KEYS), F32),
        compiler_params=_params("parallel"),
    )(base)


def _na_specs(s, heads, per):
    q = pl.BlockSpec((per * GRID_W, HEAD_DIM), lambda h, r: (r, h))
    k = pl.BlockSpec((s, HEAD_DIM), lambda h, r: (0, heads + h))
    v = pl.BlockSpec((s, HEAD_DIM), lambda h, r: (0, 2 * heads + h))
    bias = pl.BlockSpec((None, NA_WIN_R, GRID_W, NA_KEYS), lambda h, r: (h, 0, 0, 0))
    return q, k, v, bias


def _na_probs(q, k, bias):
    sc = lax.dot_general(q, k, NT_DIMS, preferred_element_type=F32) * (HEAD_DIM ** -0.5) + bias
    p = jnp.exp(sc - jnp.max(sc, axis=-1, keepdims=True))
    return p, jnp.sum(p, axis=-1, keepdims=True)


def _na_fwd(qkv, bias, *, name):
    s = qkv.shape[0]
    heads = qkv.shape[1] // (3 * HEAD_DIM)
    rows = s // GRID_W
    per = _pick(rows, NA_ROWS_PER_STEP, 1)

    def body(q_ref, k_ref, v_ref, b_ref, o_ref):
        for i in range(per):
            r = pl.program_id(1) * per + i
            win = pl.ds(pl.multiple_of(_na_row_start(r, rows) * GRID_W, GRID_W), NA_KEYS)
            mine = pl.ds(i * GRID_W, GRID_W)
            p, l = _na_probs(q_ref[mine, :], k_ref[win, :], b_ref[_na_bias_slot(r, rows)])
            o = jnp.dot(p.astype(BF16), v_ref[win, :], preferred_element_type=F32)
            o_ref[mine, :] = (o / l).astype(o_ref.dtype)

    q, k, v, b = _na_specs(s, heads, per)
    return pl.pallas_call(
        body,
        name=name,
        grid=(heads, rows // per),
        in_specs=[q, k, v, b],
        out_specs=q,
        out_shape=jax.ShapeDtypeStruct((s, heads * HEAD_DIM), BF16),
        compiler_params=_params("parallel", "arbitrary"),
    )(qkv, qkv, qkv, bias)


def _na_bwd(qkv, bias, dout, *, name):
    s = qkv.shape[0]
    heads = qkv.shape[1] // (3 * HEAD_DIM)
    rows = s // GRID_W
    per = _pick(rows, NA_ROWS_PER_STEP, 1)
    steps = rows // per
    scale = HEAD_DIM ** -0.5

    def body(q_ref, k_ref, v_ref, b_ref, do_ref, dq_ref, dk_ref, dv_ref, db_ref, dk_acc, dv_acc):
        step = pl.program_id(1)

        @pl.when(step == 0)
        def _():
            dk_acc[...] = jnp.zeros_like(dk_acc)
            dv_acc[...] = jnp.zeros_like(dv_acc)
            db_ref[...] = jnp.zeros_like(db_ref)

        for i in range(per):
            r = step * per + i
            win = pl.ds(pl.multiple_of(_na_row_start(r, rows) * GRID_W, GRID_W), NA_KEYS)
            mine = pl.ds(i * GRID_W, GRID_W)
            slot = _na_bias_slot(r, rows)
            q, k, v, do = q_ref[mine, :], k_ref[win, :], v_ref[win, :], do_ref[mine, :]
            p, l = _na_probs(q, k, b_ref[slot])
            pn = p / l
            dp = lax.dot_general(do, v, NT_DIMS, preferred_element_type=F32)
            ds = pn * (dp - jnp.sum(pn * dp, axis=-1, keepdims=True))
            dsb = ds.astype(BF16)
            dq_ref[mine, :] = (jnp.dot(dsb, k, preferred_element_type=F32) * scale).astype(dq_ref.dtype)
            dk_acc[win, :] += lax.dot_general(dsb, q, TN_DIMS, preferred_element_type=F32) * scale
            dv_acc[win, :] += lax.dot_general(pn.astype(BF16), do, TN_DIMS, preferred_element_type=F32)
            db_ref[slot] += ds

        @pl.when(step == steps - 1)
        def _():
            dk_ref[...] = dk_acc[...].astype(dk_ref.dtype)
            dv_ref[...] = dv_acc[...].astype(dv_ref.dtype)

    q, k, v, b = _na_specs(s, heads, per)
    kv_out = pl.BlockSpec((s, HEAD_DIM), lambda h, r: (0, h))
    shape = jax.ShapeDtypeStruct((s, heads * HEAD_DIM), BF16)
    return pl.pallas_call(
        body,
        name=name,
        grid=(heads, steps),
        in_specs=[q, k, v, b, q],
        out_specs=[q, kv_out, kv_out, b],
        out_shape=[shape, shape, shape, jax.ShapeDtypeStruct((heads, NA_WIN_R, GRID_W, NA_KEYS), F32)],
        scratch_shapes=[pltpu.VMEM((s, HEAD_DIM), F32), pltpu.VMEM((s, HEAD_DIM), F32)],
        compiler_params=_params("parallel", "arbitrary"),
    )(qkv, qkv, qkv, bias, dout)


RPB_ROWS = 2 * NA_WIN_R - 1
RPB_COLS = 2 * NA_WIN_C - 1
RPB_PAD = 512


def _rpb_fold_matrix():
    idx = jnp.arange(NA_WIN_R * NA_KEYS, dtype=jnp.int32)
    first, i, kc = idx // NA_KEYS, (idx // GRID_W) % NA_WIN_R, idx % GRID_W
    pos, neg = kc < NA_WIN_C, kc >= GRID_W - NA_WIN_C + 1
    dr = jnp.where(pos, first + i, first + (i + 1) % NA_WIN_R)
    dc = jnp.where(pos, kc + NA_WIN_C - 1, kc - (GRID_W - NA_WIN_C + 1))
    target = jnp.where(pos | neg, dr * RPB_COLS + dc, -1)
    return (target[:, None] == jnp.arange(RPB_PAD, dtype=jnp.int32)[None, :]).astype(F32)


def _rpb_fold(dbias, *, name):
    h = dbias.shape[0]

    def skew_body(g_ref, o_ref):
        for slot in range(NA_WIN_R):
            o_ref[slot] = _colsum(_skew_left(g_ref[slot]))

    skewed = pl.pallas_call(
        skew_body,
        name=name + "_skew",
        grid=(h,),
        in_specs=[pl.BlockSpec((None, NA_WIN_R, GRID_W, NA_KEYS), lambda i: (i, 0, 0, 0))],
        out_specs=pl.BlockSpec((None, NA_WIN_R, 1, NA_KEYS), lambda i: (i, 0, 0, 0)),
        out_shape=jax.ShapeDtypeStruct((h, NA_WIN_R, 1, NA_KEYS), F32),
        compiler_params=_params("parallel"),
    )(dbias)

    def fold_body(g_ref, m_ref, o_ref):
        o_ref[...] = jnp.dot(g_ref[...], m_ref[...], preferred_element_type=F32, precision=lax.Precision.HIGHEST)

    return pl.pallas_call(
        fold_body,
        name=name,
        out_shape=jax.ShapeDtypeStruct((h, RPB_PAD), F32),
        compiler_params=pltpu.CompilerParams(vmem_limit_bytes=VMEM_LIMIT),
    )(skewed.reshape(h, NA_WIN_R * NA_KEYS), _rpb_fold_matrix())


GQA_Q_TILE = 256


def _rope_tables(s):
    t = jnp.arange(s)
    row = (t // GRID_W).astype(F32)[:, None]
    col = (t % GRID_W).astype(F32)[:, None]
    half = HEAD_DIM // 2
    inv = ROPE_THETA ** (-jnp.arange(0, half, 2, dtype=F32) / half)
    ang = jnp.concatenate([row * inv, row * inv, col * inv, col * inv], axis=-1)
    return jnp.cos(ang), jnp.sin(ang)


def _rot_half(y):
    quarter = HEAD_DIM // 4
    lane = lax.broadcasted_iota(jnp.int32, y.shape, 1)
    low = (lane % (2 * quarter)) < quarter
    return jnp.where(low, -pltpu.roll(y, HEAD_DIM - quarter, 1), pltpu.roll(y, quarter, 1))


def _gqa_prep_fwd(qkv, gq, gk, cos, sin, hq, hkv, *, name):
    s = qkv.shape[0]

    def body(x_ref, gq_ref, gk_ref, cos_ref, sin_ref, o_ref):
        isq = pl.program_id(0) < hq
        x = x_ref[...].astype(F32)
        g = jnp.where(isq, gq_ref[...], gk_ref[...])
        y = x * lax.rsqrt(jnp.mean(x * x, axis=-1, keepdims=True) + EPS) * g
        z = y * cos_ref[...] + _rot_half(y) * sin_ref[...]
        o_ref[...] = (z * jnp.where(isq, HEAD_DIM ** -0.5, 1.0)).astype(o_ref.dtype)

    head = pl.BlockSpec((s, HEAD_DIM), lambda h: (0, h))
    vec = pl.BlockSpec((1, HEAD_DIM), lambda h: (0, 0))
    tab = pl.BlockSpec((s, HEAD_DIM), lambda h: (0, 0))
    return pl.pallas_call(
        body,
        name=name,
        grid=(hq + hkv,),
        in_specs=[head, vec, vec, tab, tab],
        out_specs=head,
        out_shape=jax.ShapeDtypeStruct((s, (hq + hkv) * HEAD_DIM), BF16),
        compiler_params=_params("parallel"),
    )(qkv, gq, gk, cos, sin)


def _gqa_prep_bwd(qkv, gq, gk, cos, sin, dqn, dkn, hq, hkv, *, name):
    s = qkv.shape[0]

    def body(x_ref, gq_ref, gk_ref, cos_ref, sin_ref, dq_ref, dk_ref, dx_ref, dgq_ref, dgk_ref):
        hh = pl.program_id(0)
        isq = hh < hq
        x = x_ref[...].astype(F32)
        g = jnp.where(isq, gq_ref[...], gk_ref[...])
        r = lax.rsqrt(jnp.mean(x * x, axis=-1, keepdims=True) + EPS)
        xhat = x * r
        dz = jnp.where(isq, dq_ref[...].astype(F32) * (HEAD_DIM ** -0.5), dk_ref[...].astype(F32))
        dy = dz * cos_ref[...] - _rot_half(dz * sin_ref[...])
        dyg = dy * g
        dx_ref[...] = (r * (dyg - xhat * jnp.mean(dyg * xhat, axis=-1, keepdims=True))).astype(dx_ref.dtype)
        part = _colsum(dy * xhat)

        @pl.when(hh == 0)
        def _():
            dgq_ref[...] = jnp.zeros_like(dgq_ref)
            dgk_ref[...] = jnp.zeros_like(dgk_ref)

        @pl.when(isq)
        def _():
            dgq_ref[...] += part

        @pl.when(jnp.logical_not(isq))
        def _():
            dgk_ref[...] += part

    head = pl.BlockSpec((s, HEAD_DIM), lambda h: (0, h))
    vec = pl.BlockSpec((1, HEAD_DIM), lambda h: (0, 0))
    tab = pl.BlockSpec((s, HEAD_DIM), lambda h: (0, 0))
    return pl.pallas_call(
        body,
        name=name,
        grid=(hq + hkv,),
        in_specs=[
            head,
            vec,
            vec,
            tab,
            tab,
            pl.BlockSpec((s, HEAD_DIM), lambda h: (0, jnp.minimum(h, hq - 1))),
            pl.BlockSpec((s, HEAD_DIM), lambda h: (0, jnp.maximum(h - hq, 0))),
        ],
        out_specs=[head, vec, vec],
        out_shape=[
            jax.ShapeDtypeStruct((s, (hq + hkv) * HEAD_DIM), BF16),
            jax.ShapeDtypeStruct((1, HEAD_DIM), F32),
            jax.ShapeDtypeStruct((1, HEAD_DIM), F32),
        ],
        compiler_params=_params("arbitrary"),
    )(qkv, gq, gk, cos, sin, dqn, dkn)


def _gqa_fwd(qkn, qkv, hq, hkv, *, name):
    s = qkv.shape[0]
    tq = _pick(s, GQA_Q_TILE, 16)

    def body(q_ref, k_ref, v_ref, o_ref):
        sc = lax.dot_general(q_ref[...], k_ref[...], NT_DIMS, preferred_element_type=F32)
        p = jnp.exp(sc - jnp.max(sc, axis=-1, keepdims=True))
        l = jnp.sum(p, axis=-1, keepdims=True)
        o_ref[...] = (jnp.dot(p.astype(BF16), v_ref[...], preferred_element_type=F32) / l).astype(o_ref.dtype)

    q = pl.BlockSpec((tq, HEAD_DIM), lambda h, i: (i, h))
    return pl.pallas_call(
        body,
        name=name,
        grid=(hq, s // tq),
        in_specs=[
            q,
            pl.BlockSpec((s, HEAD_DIM), lambda h, i: (0, hq + h // GQA_GROUP)),
            pl.BlockSpec((s, HEAD_DIM), lambda h, i: (0, hq + hkv + h // GQA_GROUP)),
        ],
        out_specs=q,
        out_shape=jax.ShapeDtypeStruct((s, hq * HEAD_DIM), BF16),
        compiler_params=_params("parallel", "parallel"),
    )(qkn, qkn, qkv)


def _gqa_bwd(qkn, qkv, dout, hq, hkv, *, name):
    s = qkv.shape[0]
    tq = _pick(s, GQA_Q_TILE, 16)
    nq = s // tq

    def body(q_ref, k_ref, v_ref, do_ref, dq_ref, dk_ref, dv_ref, dk_acc, dv_acc):
        g, i = pl.program_id(1), pl.program_id(2)

        @pl.when((g == 0) & (i == 0))
        def _():
            dk_acc[...] = jnp.zeros_like(dk_acc)
            dv_acc[...] = jnp.zeros_like(dv_acc)

        q, k, v, do = q_ref[...], k_ref[...], v_ref[...], do_ref[...]
        sc = lax.dot_general(q, k, NT_DIMS, preferred_element_type=F32)
        p = jnp.exp(sc - jnp.max(sc, axis=-1, keepdims=True))
        pn = p / jnp.sum(p, axis=-1, keepdims=True)
        dp = lax.dot_general(do, v, NT_DIMS, preferred_element_type=F32)
        dsb = (pn * (dp - jnp.sum(pn * dp, axis=-1, keepdims=True))).astype(BF16)
        dq_ref[...] = jnp.dot(dsb, k, preferred_element_type=F32).astype(dq_ref.dtype)
        dk_acc[...] += lax.dot_general(dsb, q, TN_DIMS, preferred_element_type=F32)
        dv_acc[...] += lax.dot_general(pn.astype(BF16), do, TN_DIMS, preferred_element_type=F32)

        @pl.when((g == GQA_GROUP - 1) & (i == nq - 1))
        def _():
            dk_ref[...] = dk_acc[...].astype(dk_ref.dtype)
            dv_ref[...] = dv_acc[...].astype(dv_ref.dtype)

    q = pl.BlockSpec((tq, HEAD_DIM), lambda kv, g, i: (i, kv * GQA_GROUP + g))
    kv_out = pl.BlockSpec((s, HEAD_DIM), lambda kv, g, i: (0, kv))
    return pl.pallas_call(
        body,
        name=name,
        grid=(hkv, GQA_GROUP, nq),
        in_specs=[
            q,
            pl.BlockSpec((s, HEAD_DIM), lambda kv, g, i: (0, hq + kv)),
            pl.BlockSpec((s, HEAD_DIM), lambda kv, g, i: (0, hq + hkv + kv)),
            q,
        ],
        out_specs=[q, kv_out, kv_out],
        out_shape=[
            jax.ShapeDtypeStruct((s, hq * HEAD_DIM), BF16),
            jax.ShapeDtypeStruct((s, hkv * HEAD_DIM), BF16),
            jax.ShapeDtypeStruct((s, hkv * HEAD_DIM), BF16),
        ],
        scratch_shapes=[pltpu.VMEM((s, HEAD_DIM), F32), pltpu.VMEM((s, HEAD_DIM), F32)],
        compiler_params=_params("parallel", "arbitrary", "arbitrary"),
    )(qkn, qkn, qkv, dout)


ADAM_ROWS = 64


def _adam_update(w, g, m, v):
    m = ADAM_B1 * m + (1.0 - ADAM_B1) * g
    v = ADAM_B2 * v + (1.0 - ADAM_B2) * (g * g)
    m_hat = m / (1.0 - ADAM_B1 ** ADAM_STEP)
    v_hat = v / (1.0 - ADAM_B2 ** ADAM_STEP)
    return -ADAM_LR * (m_hat / (jnp.sqrt(v_hat) + ADAM_EPS) + ADAM_WD * w), m, v


def _adamw_slab(w, m, v, own, sib, slab, prev, after, *, name):
    _, rows, cols = w.shape
    tr = _pick(rows, ADAM_ROWS, 16)
    tokens = [] if after is None else [after]

    def body(w_ref, m_ref, v_ref, own_ref, sib_ref, *rest):
        g_ref, d_ref, nm_ref, nv_ref = rest[-4:]
        g = own_ref[0].astype(F32) + sib_ref[0].astype(F32)
        for q in range(1, N_CHIPS):
            g = g + (own_ref[q].astype(F32) + sib_ref[q].astype(F32))
        g_ref[...] = g
        d_ref[...], nm_ref[...], nv_ref[...] = _adam_update(w_ref[...], g, m_ref[...], v_ref[...])

    one = pl.BlockSpec((None, tr, cols), lambda i: (slab, i, 0))
    piece = pl.BlockSpec((N_CHIPS, tr, cols), lambda i: (0, i, 0))
    carried = [] if prev is None else list(prev)
    shape = jax.ShapeDtypeStruct(w.shape, F32)
    return pl.pallas_call(
        body,
        name=name,
        grid=(rows // tr,),
        in_specs=[one] * 3
        + [piece] * 2
        + [pl.BlockSpec(memory_space=pl.ANY)] * len(carried)
        + [pl.BlockSpec(t.shape, lambda i: (0, 0)) for t in tokens],
        out_specs=[one] * 4,
        out_shape=[shape] * 4,
        input_output_aliases={5 + i: i for i in range(len(carried))},
        compiler_params=_params("parallel"),
    )(w, m, v, own, sib, *carried, *tokens)


def _adamw_small(w, g, m, v, *, name):
    def body(w_ref, g_ref, m_ref, v_ref, d_ref, nm_ref, nv_ref):
        d_ref[...], nm_ref[...], nv_ref[...] = _adam_update(w_ref[...], g_ref[...], m_ref[...], v_ref[...])

    shape = jax.ShapeDtypeStruct(w.shape, F32)
    return pl.pallas_call(
        body,
        name=name,
        out_shape=[shape] * 3,
        compiler_params=pltpu.CompilerParams(vmem_limit_bytes=VMEM_LIMIT),
    )(w, g, m, v)


def _position():
    x, y, c = lax.axis_index("x"), lax.axis_index("y"), lax.axis_index("c")
    return x, y, c, 2 * x + y


def _chip_device(chip, c):
    return (chip >> 1, chip & 1, c)


def _handshake(peers):
    barrier = pltpu.get_barrier_semaphore()
    for peer in peers:
        pl.semaphore_signal(barrier, inc=1, device_id=peer, device_id_type=MESH)
    pl.semaphore_wait(barrier, len(peers))


GATHER_CHUNKS = 2


def _gather_shards(shards, split, *, name, collective_id):
    n = len(shards)
    big = [a for a in range(n) if split[a]]
    small = [a for a in range(n) if not split[a]]
    y_nbr, x_nbr, far = 1, 2, 3

    def body(*refs):
        ins, outs = refs[:n], refs[n : 2 * n]
        near_send, near_recv, far_send, far_recv, pass_send, pass_recv, own_send, own_recv = refs[2 * n :]
        x, y, c, k = _position()
        _handshake([(x, y, 1 - c)] + [_chip_device(k ^ j, c) for j in range(1, N_CHIPS)])

        def own(a, chunk):
            r = shards[a].shape[0]
            part = pl.ds(chunk * (r // GATHER_CHUNKS), r // GATHER_CHUNKS) if split[a] else pl.ds(0, r)
            return pltpu.make_async_remote_copy(
                src_ref=ins[a].at[part],
                dst_ref=outs[a].at[k, part],
                send_sem=own_send.at[a, chunk],
                recv_sem=own_recv.at[a, chunk],
                device_id=(x, y, 1 - c),
                device_id_type=MESH,
            )

        own_copies = [own(a, ch) for a in range(n) for ch in range(GATHER_CHUNKS if split[a] else 1)]
        for cp in own_copies:
            cp.start()

        def run(core):
            sibling = (x, y, 1 - core)
            relay_from, relay_to = (x_nbr, y_nbr) if core == 0 else (y_nbr, x_nbr)

            def rows(a, which, chunk=None):
                r = shards[a].shape[0]
                if not split[a]:
                    return pl.ds(0, r)
                half = r // 2
                if chunk is None:
                    return pl.ds(which * half, half)
                return pl.ds(which * half + chunk * (half // GATHER_CHUNKS), half // GATHER_CHUNKS)

            def direct(a, mask, chunk, src_chip):
                part = rows(a, core, chunk)
                return pltpu.make_async_remote_copy(
                    src_ref=ins[a].at[part],
                    dst_ref=outs[a].at[src_chip, part],
                    send_sem=near_send.at[a, mask - 1, chunk or 0],
                    recv_sem=near_recv.at[a, mask - 1, chunk or 0],
                    device_id=_chip_device(k ^ mask, core),
                    device_id_type=MESH,
                )

            def relay(a, chunk, src_chip, mask):
                part = outs[a].at[src_chip, rows(a, core, chunk)]
                return pltpu.make_async_remote_copy(
                    src_ref=part,
                    dst_ref=part,
                    send_sem=far_send.at[a, chunk or 0],
                    recv_sem=far_recv.at[a, chunk or 0],
                    device_id=_chip_device(k ^ mask, core),
                    device_id_type=MESH,
                )

            def to_sibling(a, mask, which):
                part = outs[a].at[k ^ mask, rows(a, which)]
                return pltpu.make_async_remote_copy(
                    src_ref=part,
                    dst_ref=part,
                    send_sem=pass_send.at[a, mask - 1],
                    recv_sem=pass_recv.at[a, mask - 1],
                    device_id=sibling,
                    device_id_type=MESH,
                )

            sent = []

            def start(cp):
                cp.start()
                sent.append(cp)

            chunks = range(GATHER_CHUNKS)
            for chunk in chunks:
                for a in big:
                    start(direct(a, relay_from, chunk, k))
                    start(direct(a, relay_to, chunk, k))
            for a in small:
                start(direct(a, x_nbr, None, k))
                start(direct(a, y_nbr, None, k))
                start(pltpu.make_async_remote_copy(
                    src_ref=ins[a], dst_ref=outs[a].at[k], send_sem=far_send.at[a, 0], recv_sem=far_recv.at[a, 0],
                    device_id=_chip_device(k ^ far, core), device_id_type=MESH))
            for chunk in chunks:
                for a in big:
                    direct(a, relay_from, chunk, k ^ relay_from).wait_recv()
                    start(relay(a, chunk, k ^ relay_from, relay_to))
            for a in big:
                start(to_sibling(a, relay_from, core))
            for a in big:
                for chunk in chunks:
                    direct(a, relay_to, chunk, k ^ relay_to).wait_recv()
                start(to_sibling(a, relay_to, core))
            for a in big:
                for chunk in chunks:
                    relay(a, chunk, k ^ far, far).wait_recv()
                start(to_sibling(a, far, core))
            for a in small:
                direct(a, x_nbr, None, k ^ x_nbr).wait_recv()
                direct(a, y_nbr, None, k ^ y_nbr).wait_recv()
                relay(a, None, k ^ far, far).wait_recv()
            for a in big:
                for mask in (y_nbr, x_nbr, far):
                    to_sibling(a, mask, 1 - core).wait_recv()
            for cp in sent:
                cp.wait_send()

        for core in (0, 1):
            pl.when(c == core)(functools.partial(run, core))
        for cp in own_copies:
            cp.wait()

    return pl.kernel(
        body,
        name=name,
        out_type=[jax.ShapeDtypeStruct((N_CHIPS,) + a.shape, a.dtype) for a in shards],
        mesh=plsc.ScalarSubcoreMesh(axis_name="sequencer", num_cores=1),
        scratch_types=[
            pltpu.SemaphoreType.DMA((n, 2, GATHER_CHUNKS)),
            pltpu.SemaphoreType.DMA((n, 2, GATHER_CHUNKS)),
            pltpu.SemaphoreType.DMA((n, GATHER_CHUNKS)),
            pltpu.SemaphoreType.DMA((n, GATHER_CHUNKS)),
            pltpu.SemaphoreType.DMA((n, N_CHIPS - 1)),
            pltpu.SemaphoreType.DMA((n, N_CHIPS - 1)),
            pltpu.SemaphoreType.DMA((n, GATHER_CHUNKS)),
            pltpu.SemaphoreType.DMA((n, GATHER_CHUNKS)),
        ],
        compiler_params=pltpu.CompilerParams(collective_id=collective_id),
    )(*shards)


def _scatter_pieces(pieces, *, name, collective_id):
    n = len(pieces)

    def body(*refs):
        ins, own, sib = refs[:n], refs[n : 2 * n], refs[2 * n : 3 * n]
        local_sem, send_sem, recv_sem, pass_send, pass_recv = refs[3 * n :]
        x, y, c, k = _position()
        sibling = (x, y, 1 - c)
        _handshake([sibling] + [_chip_device(k ^ j, c) for j in range(1, N_CHIPS)])

        def over_ici(a, j, piece, slot, to):
            return pltpu.make_async_remote_copy(
                src_ref=ins[a].at[piece],
                dst_ref=own[a].at[slot],
                send_sem=send_sem.at[a, j],
                recv_sem=recv_sem.at[a, j],
                device_id=to,
                device_id_type=MESH,
            )

        def to_sibling(a, j, slot):
            return pltpu.make_async_remote_copy(
                src_ref=own[a].at[slot],
                dst_ref=sib[a].at[slot],
                send_sem=pass_send.at[a, j],
                recv_sem=pass_recv.at[a, j],
                device_id=sibling,
                device_id_type=MESH,
            )

        mine = [pltpu.make_async_copy(ins[a].at[k], own[a].at[k], local_sem.at[a]) for a in range(n)]
        for cp in mine:
            cp.start()
        sent = []
        for j in range(N_CHIPS - 1):
            other = k ^ (j + 1)
            for a in range(n):
                cp = over_ici(a, j, other, k, _chip_device(other, c))
                cp.start()
                sent.append(cp)
        for a in range(n):
            mine[a].wait()
            cp = to_sibling(a, N_CHIPS - 1, k)
            cp.start()
            sent.append(cp)
        for j in range(N_CHIPS - 1):
            other = k ^ (j + 1)
            for a in range(n):
                over_ici(a, j, other, other, sibling).wait_recv()
                cp = to_sibling(a, j, other)
                cp.start()
                sent.append(cp)
        for j in range(N_CHIPS):
            for a in range(n):
                to_sibling(a, j, k).wait_recv()
        for cp in sent:
            cp.wait_send()

    shapes = [jax.ShapeDtypeStruct(a.shape, a.dtype) for a in pieces]
    outs = pl.kernel(
        body,
        name=name,
        out_type=shapes + shapes,
        mesh=plsc.ScalarSubcoreMesh(axis_name="sequencer", num_cores=1),
        scratch_types=[
            pltpu.SemaphoreType.DMA((n,)),
            pltpu.SemaphoreType.DMA((n, N_CHIPS - 1)),
            pltpu.SemaphoreType.DMA((n, N_CHIPS - 1)),
            pltpu.SemaphoreType.DMA((n, N_CHIPS)),
            pltpu.SemaphoreType.DMA((n, N_CHIPS)),
        ],
        compiler_params=pltpu.CompilerParams(collective_id=collective_id),
    )(*pieces)
    return outs[:n], outs[n:]


def _allreduce_small(buf, *, name):
    def body(x_ref, o_ref, slots, send_sem, recv_sem):
        x, y, c, _ = _position()
        me = 4 * x + 2 * y + c
        slots[me] = x_ref[...]

        def copy(d, slot):
            peer = me ^ d
            return pltpu.make_async_remote_copy(
                src_ref=x_ref,
                dst_ref=slots.at[slot],
                send_sem=send_sem.at[d - 1],
                recv_sem=recv_sem.at[d - 1],
                device_id=(peer >> 2, (peer >> 1) & 1, peer & 1),
                device_id_type=MESH,
            )

        sent = [copy(d, me) for d in range(1, N_DEV)]
        for cp in sent:
            cp.start()
        for d in range(1, N_DEV):
            copy(d, me ^ d).wait_recv()
        for cp in sent:
            cp.wait_send()
        acc = slots[0]
        for s in range(1, N_DEV):
            acc = acc + slots[s]
        o_ref[...] = acc

    return pl.pallas_call(
        body,
        name=name,
        in_specs=[pl.BlockSpec(memory_space=pltpu.VMEM)],
        out_specs=pl.BlockSpec(memory_space=pltpu.VMEM),
        out_shape=jax.ShapeDtypeStruct(buf.shape, F32),
        scratch_shapes=[
            pltpu.VMEM((N_DEV,) + buf.shape, F32),
            pltpu.SemaphoreType.DMA((N_DEV - 1,)),
            pltpu.SemaphoreType.DMA((N_DEV - 1,)),
        ],
        compiler_params=pltpu.CompilerParams(vmem_limit_bytes=VMEM_LIMIT),
    )(buf)


def _mixer_of(i):
    return i % N_MIXERS, i // N_MIXERS


def _forward_backward(x, target, norms, layers, send=lambda i, part, pieces: pieces):
    s, d = x.shape
    depth = len(layers)
    heads = d // HEAD_DIM
    hkv = heads // GQA_GROUP
    cos, sin = _rope_tables(s)
    saved = []
    h = x
    for i, lw in enumerate(layers):
        kind, j = _mixer_of(i)
        tag = f"l{i}"
        sv = {"h_in": h}
        a = _rms_fwd(h, norms["mix_norm"][i : i + 1], name=f"{tag}_mix_norm")
        qkv = _mm_nn(a, lw["w_in"], out_dtype=BF16, name=f"{tag}_w_in")
        if kind == 0:
            bias = _na_bias(_na_base(norms["na_rpb"][j]), name=f"{tag}_na_bias")
            o = _na_fwd(qkv, bias, name=f"{tag}_na_fwd")
            sv["bias"] = bias
        elif kind == 1:
            o = _sc_mid_fwd(qkv, lw["sc_conv_w"], name=f"{tag}_sc_fwd")
        else:
            gq, gk = norms["gqa_q_norm"][j : j + 1], norms["gqa_k_norm"][j : j + 1]
            qkn = _gqa_prep_fwd(qkv, gq, gk, cos, sin, heads, hkv, name=f"{tag}_gqa_prep")
            o = _gqa_fwd(qkn, qkv, heads, hkv, name=f"{tag}_gqa_fwd")
            sv["qkn"] = qkn
        h_mid = _mm_nn(o, lw["w_out"], out_dtype=F32, residual=h, name=f"{tag}_w_out")
        b = _rms_fwd(h_mid, norms["ffn_norm"][i : i + 1], name=f"{tag}_ffn_norm")
        up = _mm_nn(b, lw["w_up"], out_dtype=BF16, name=f"{tag}_w_up")
        act, cg, cu = _ffn_mid_fwd(up, lw["ffn_conv_w"], lw["ffn_conv_b"], name=f"{tag}_ffn_fwd")
        h = _mm_nn(act, lw["w_down"], out_dtype=F32, residual=h_mid, name=f"{tag}_w_down")
        sv.update(a=a, qkv=qkv, o=o, h_mid=h_mid, b=b, up=up, act=act, cg=cg, cu=cu)
        saved.append(sv)

    dh, dh_b, d_final, loss = _loss_head(h, norms["final_norm"][None], target, name="loss_head")

    big = [None] * depth
    small = {"final_norm": d_final, "mix_norm": [None] * depth, "ffn_norm": [None] * depth,
             "ffn_conv_w": [None] * depth, "ffn_conv_b": [None] * depth, "na_rpb": {}}
    after = ()
    for i in reversed(range(depth)):
        kind, j = _mixer_of(i)
        tag = f"l{i}b"
        lw, sv = layers[i], saved[i]
        dact = _mm_nt(dh_b, lw["w_down"], out_dtype=BF16, name=f"{tag}_d_act", after=after)
        dw_down = _mm_tn(sv["act"], dh_b, 1, name=f"{tag}_dw_down")
        dug, duu, dwg, dwu, dbg, dbu = _ffn_mid_bwd(
            sv["up"], sv["cg"], sv["cu"], lw["ffn_conv_w"], dact, name=f"{tag}_ffn_bwd"
        )
        small["ffn_conv_w"][i] = jnp.concatenate([dwg, dwu], axis=1)
        small["ffn_conv_b"][i] = jnp.concatenate([dbg, dbu], axis=1)
        dw_up = _mm_tn(sv["b"], (dug, duu), N_CHIPS, name=f"{tag}_dw_up")
        sent_ffn = send(i, "ffn", [dw_up, dw_down.reshape(N_CHIPS, dw_down.shape[1] // N_CHIPS, d)])
        db = _mm_nt((dug, duu), lw["w_up"], out_dtype=F32, name=f"{tag}_d_b")
        dh_mid, dh_mid_b, small["ffn_norm"][i] = _rms_bwd(
            sv["h_mid"], norms["ffn_norm"][i : i + 1], db, dh, name=f"{tag}_ffn_norm"
        )
        do = _mm_nt(dh_mid_b, lw["w_out"], out_dtype=BF16, name=f"{tag}_d_o", after=(_token(dw_up), _token(dw_down)))
        dw_out = _mm_tn(sv["o"], dh_mid_b, 1, name=f"{tag}_dw_out")
        if kind == 0:
            dq, dk, dv, dbias = _na_bwd(sv["qkv"], sv["bias"], do, name=f"{tag}_na_bwd")
            dqkv = jnp.concatenate([dq, dk, dv], axis=1)
            small["na_rpb"][j] = _rpb_fold(dbias, name=f"{tag}_rpb_fold")
        elif kind == 1:
            dgb, dgc, dhh, small["sc_conv_w"] = _sc_mid_bwd(sv["qkv"], lw["sc_conv_w"], do, name=f"{tag}_sc_bwd")
            dqkv = jnp.concatenate([dgb, dgc, dhh], axis=1)
        else:
            gq, gk = norms["gqa_q_norm"][j : j + 1], norms["gqa_k_norm"][j : j + 1]
            dqn, dkn, dv = _gqa_bwd(sv["qkn"], sv["qkv"], do, heads, hkv, name=f"{tag}_gqa_bwd")
            dqk, small["gqa_q_norm"], small["gqa_k_norm"] = _gqa_prep_bwd(
                sv["qkv"], gq, gk, cos, sin, dqn, dkn, heads, hkv, name=f"{tag}_gqa_prep_bwd"
            )
            dqkv = jnp.concatenate([dqk, dv], axis=1)
        dw_in = _mm_tn(sv["a"], dqkv, N_CHIPS, name=f"{tag}_dw_in")
        da = _mm_nt(dqkv, lw["w_in"], out_dtype=F32, name=f"{tag}_d_a")
        dh, dh_b, small["mix_norm"][i] = _rms_bwd(
            sv["h_in"], norms["mix_norm"][i : i + 1], da, dh_mid, name=f"{tag}_mix_norm"
        )
        sent_mix = send(i, "mix", [dw_in, dw_out.reshape(N_CHIPS, dw_out.shape[1] // N_CHIPS, d)])
        after = (_token(dw_in), _token(dw_out))
        big[i] = {"mix": sent_mix, "ffn": sent_ffn}
    return loss, dh, big, small


def _pack(parts):
    flat = jnp.concatenate([p.reshape(-1).astype(F32) for p in parts])
    pad = (-flat.shape[0]) % (8 * LANES)
    return jnp.pad(flat, (0, pad)).reshape(-1, LANES)


def _unpack(buf, shapes):
    flat = buf.reshape(-1)
    out, at = [], 0
    for shp in shapes:
        size = 1
        for n in shp:
            size *= n
        out.append(flat[at : at + size].reshape(shp))
        at += size
    return out


def kernel(x, mix_norm, ffn_norm, final_norm, na_w_qkv, na_rpb, na_w_o, sc_w_in, sc_conv_w, sc_w_out, gqa_w_qkv, gqa_q_norm, gqa_k_norm, gqa_w_o, ffn_w_up, ffn_conv_w, ffn_conv_b, ffn_w_down, loss_target, m_mix_norm, m_ffn_norm, m_final_norm, m_na_w_qkv, m_na_rpb, m_na_w_o, m_sc_w_in, m_sc_conv_w, m_sc_w_out, m_gqa_w_qkv, m_gqa_q_norm, m_gqa_k_norm, m_gqa_w_o, m_ffn_w_up, m_ffn_conv_w, m_ffn_conv_b, m_ffn_w_down, v_mix_norm, v_ffn_norm, v_final_norm, v_na_w_qkv, v_na_rpb, v_na_w_o, v_sc_w_in, v_sc_conv_w, v_sc_w_out, v_gqa_w_qkv, v_gqa_q_norm, v_gqa_k_norm, v_gqa_w_o, v_ffn_w_up, v_ffn_conv_w, v_ffn_conv_b, v_ffn_w_down):
    depth, d = mix_norm.shape
    chip = 2 * lax.axis_index("x") + lax.axis_index("y")
    w_in_of = {0: na_w_qkv, 1: sc_w_in, 2: gqa_w_qkv}
    w_out_of = {0: na_w_o, 1: sc_w_out, 2: gqa_w_o}

    layers = []
    for i in range(depth):
        kind, j = _mixer_of(i)
        shards = [_cast_slab(w_in_of[kind], j, name=f"cast_w_in_l{i}"), _cast_slab(w_out_of[kind], j, name=f"cast_w_out_l{i}")]
        split = [True, True]
        if kind == 1:
            shards.append(sc_conv_w[j])
            split.append(False)
        mix = _gather_shards(shards, split, name=f"gather_mix_l{i}", collective_id=1 + 2 * i)
        ffn = _gather_shards(
            [_cast_slab(ffn_w_up, i, name=f"cast_w_up_l{i}"), _cast_slab(ffn_w_down, i, name=f"cast_w_down_l{i}"),
             ffn_conv_w[i]],
            [True, True, False],
            name=f"gather_ffn_l{i}",
            collective_id=2 + 2 * i,
        )
        lw = {
            "w_in": mix[0],
            "w_out": mix[1].reshape(1, -1, d),
            "w_up": ffn[0],
            "w_down": ffn[1].reshape(1, -1, d),
            "ffn_conv_w": ffn[2],
            "ffn_conv_b": ffn_conv_b[i : i + 1],
        }
        if kind == 1:
            lw["sc_conv_w"] = mix[2]
        layers.append(lw)

    norms = dict(mix_norm=mix_norm, ffn_norm=ffn_norm, final_norm=final_norm, na_rpb=na_rpb,
                 gqa_q_norm=gqa_q_norm, gqa_k_norm=gqa_k_norm)

    def send(i, part, pieces):
        cid = 1 + 2 * depth + 2 * i + (part == "mix")
        return _scatter_pieces(pieces, name=f"scatter_{part}_l{i}", collective_id=cid)

    loss, grad_x, big, small = _forward_backward(x[0], loss_target[0], norms, layers, send)

    mixer_names = {0: ("na_w_qkv", "na_w_o"), 1: ("sc_w_in", "sc_w_out"), 2: ("gqa_w_qkv", "gqa_w_o")}
    state = {
        "na_w_qkv": (na_w_qkv, m_na_w_qkv, v_na_w_qkv), "na_w_o": (na_w_o, m_na_w_o, v_na_w_o),
        "sc_w_in": (sc_w_in, m_sc_w_in, v_sc_w_in), "sc_w_out": (sc_w_out, m_sc_w_out, v_sc_w_out),
        "gqa_w_qkv": (gqa_w_qkv, m_gqa_w_qkv, v_gqa_w_qkv), "gqa_w_o": (gqa_w_o, m_gqa_w_o, v_gqa_w_o),
        "ffn_w_up": (ffn_w_up, m_ffn_w_up, v_ffn_w_up), "ffn_w_down": (ffn_w_down, m_ffn_w_down, v_ffn_w_down),
    }
    res = {n: None for n in state}
    token = None
    for i in reversed(range(depth)):
        kind, j = _mixer_of(i)
        for part, names, slab in (("ffn", ("ffn_w_up", "ffn_w_down"), i), ("mix", mixer_names[kind], j)):
            own, sib = big[i][part]
            for slot, n in enumerate(names):
                res[n] = _adamw_slab(*state[n], own[slot], sib[slot], slab, res[n], token, name=f"adamw_{n}_l{i}")
                token = _token(res[n][0][slab])

    n_na = na_rpb.shape[0]
    rpb_flat = jnp.stack([small["na_rpb"][j] for j in range(n_na)])
    full_parts = [
        loss[:, :1],
        jnp.concatenate(small["mix_norm"], axis=0),
        jnp.concatenate(small["ffn_norm"], axis=0),
        small["final_norm"],
        rpb_flat,
        small["sc_conv_w"],
        small["gqa_q_norm"],
        small["gqa_k_norm"],
        jnp.stack(small["ffn_conv_w"]),
        jnp.concatenate(small["ffn_conv_b"], axis=0),
    ]
    summed = _unpack(_allreduce_small(_pack(full_parts), name="allreduce_small"), [p.shape for p in full_parts])
    loss_all, g_mix, g_ffn, g_final, g_rpb, g_sc_cw, g_gq, g_gk, g_ffn_cw, g_ffn_cb = summed
    g_rpb = g_rpb[:, :, : RPB_ROWS * RPB_COLS].reshape(na_rpb.shape)
    g_sc_cw = lax.dynamic_slice_in_dim(g_sc_cw, chip * sc_conv_w.shape[2], sc_conv_w.shape[2], axis=1)[None]
    g_ffn_cw = lax.dynamic_slice_in_dim(g_ffn_cw, chip * ffn_conv_w.shape[2], ffn_conv_w.shape[2], axis=2)
    small_names = ["mix_norm", "ffn_norm", "final_norm", "na_rpb", "sc_conv_w", "gqa_q_norm", "gqa_k_norm",
                   "ffn_conv_w", "ffn_conv_b"]
    small_g = [g_mix, g_ffn, g_final.reshape(final_norm.shape), g_rpb, g_sc_cw, g_gq, g_gk, g_ffn_cw, g_ffn_cb]
    small_w = [mix_norm, ffn_norm, final_norm, na_rpb, sc_conv_w, gqa_q_norm, gqa_k_norm, ffn_conv_w, ffn_conv_b]
    small_m = [m_mix_norm, m_ffn_norm, m_final_norm, m_na_rpb, m_sc_conv_w, m_gqa_q_norm, m_gqa_k_norm,
               m_ffn_conv_w, m_ffn_conv_b]
    small_v = [v_mix_norm, v_ffn_norm, v_final_norm, v_na_rpb, v_sc_conv_w, v_gqa_q_norm, v_gqa_k_norm,
               v_ffn_conv_w, v_ffn_conv_b]
    shapes = [w.shape for w in small_w]
    packed = _adamw_small(_pack(small_w), _pack(small_g), _pack(small_m), _pack(small_v), name="adamw_small")
    small_d, small_nm, small_nv = (_unpack(p, shapes) for p in packed)
    for n, g, dl, nm, nv in zip(small_names, small_g, small_d, small_nm, small_nv):
        res[n] = (g.reshape(dl.shape), dl, nm, nv)

    order = ["mix_norm", "ffn_norm", "final_norm", "na_w_qkv", "na_rpb", "na_w_o", "sc_w_in", "sc_conv_w",
             "sc_w_out", "gqa_w_qkv", "gqa_q_norm", "gqa_k_norm", "gqa_w_o", "ffn_w_up", "ffn_conv_w",
             "ffn_conv_b", "ffn_w_down"]
    outs = [loss_all.reshape(()), grad_x[None]]
    for part in range(4):
        outs.extend(res[n][part] for n in order)
    return tuple(outs)
```

```python
import functools
import math

import jax
import jax.numpy as jnp
from jax import lax
from jax.experimental import pallas as pl
from jax.experimental.pallas import tpu as pltpu
from jax.experimental.pallas import tpu_sc as plsc

F32 = jnp.float32
BF16 = jnp.bfloat16
MESH = pl.DeviceIdType.MESH

N_CHIPS = 4
N_DEV = 8
N_MIXERS = 3
GRID_W = 64
HEAD_DIM = 128
EPS = 1e-6
NEG_INF = -1e30
NA_WIN_R = 8
NA_WIN_C = 16
GQA_GROUP = 4
ROPE_THETA = 10000.0
ADAM_LR = 0.001
ADAM_B1 = 0.9
ADAM_B2 = 0.999
ADAM_EPS = 1e-08
ADAM_WD = 0.01
ADAM_STEP = 10

LANES = 128
VMEM_LIMIT = 56 * 1024 * 1024
NT_DIMS = (((1,), (1,)), ((), ()))
TN_DIMS = (((0,), (0,)), ((), ()))


def _pick(n, cap, mult=LANES):
    best = None
    for t in range(mult, min(n, cap) + 1, mult):
        if n % t == 0:
            best = t
    return best if best is not None else n


def _params(*sem):
    return pltpu.CompilerParams(dimension_semantics=sem, vmem_limit_bytes=VMEM_LIMIT)


MM_VMEM_BUDGET = 47 * 1024 * 1024
MM_CONTRACT = 2816


def _mm_rows(m, blocks_for, mult=16):
    for cap in (1024, 512, 256, 128):
        tm = _pick(m, cap, mult)
        if sum(r * c * b * n for r, c, b, n in blocks_for(tm)) <= MM_VMEM_BUDGET:
            return tm
    return _pick(m, 128, mult)


def _accumulate(acc, step, steps, part, finish):
    if steps == 1:
        finish(part)
        return

    @pl.when(step == 0)
    def _():
        acc[...] = part

    @pl.when(step != 0)
    def _():
        acc[...] += part

    @pl.when(step == steps - 1)
    def _():
        finish(acc[...])


def _mm_nn(a, b, *, out_dtype, name, residual=None):
    m, k = a.shape
    nc, _, ncol = b.shape
    tn, tk = _pick(ncol, 1536), _pick(k, MM_CONTRACT)
    per, nk = ncol // tn, k // tk
    osz = jnp.dtype(out_dtype).itemsize
    tm = _mm_rows(m, lambda t: [(t, tk, a.dtype.itemsize, 2), (tk, tn, 2, 2), (t, tn, osz, 2),
                                (t, tn, 4, 2 * (residual is not None)), (t, tn, 4, nk > 1)])

    def body(a_ref, b_ref, *rest):
        o_ref = rest[-2] if nk > 1 else rest[-1]

        def finish(r):
            if residual is not None:
                r = r + rest[0][...]
            o_ref[...] = r.astype(o_ref.dtype)

        part = jnp.dot(a_ref[...].astype(BF16), b_ref[...], preferred_element_type=F32)
        _accumulate(rest[-1], pl.program_id(2), nk, part, finish)

    in_specs = [
        pl.BlockSpec((tm, tk), lambda i, j, kk: (i, kk)),
        pl.BlockSpec((None, tk, tn), lambda i, j, kk: (j // per, kk, j % per)),
    ]
    ops = [a, b]
    if residual is not None:
        in_specs.append(pl.BlockSpec((tm, tn), lambda i, j, kk: (i, j)))
        ops.append(residual)
    return pl.pallas_call(
        body,
        name=name,
        grid=(m // tm, nc * per, nk),
        in_specs=in_specs,
        out_specs=pl.BlockSpec((tm, tn), lambda i, j, kk: (i, j)),
        out_shape=jax.ShapeDtypeStruct((m, nc * ncol), out_dtype),
        scratch_shapes=[pltpu.VMEM((tm, tn), F32)] * (nk > 1),
        compiler_params=_params("parallel", "parallel", "arbitrary"),
    )(*ops)


def _token(x):
    return x[(0,) * (x.ndim - 2) + (slice(0, 16), slice(0, LANES))]


def _mm_nt(a, b, *, out_dtype, name, after=()):
    parts = tuple(a) if isinstance(a, (tuple, list)) else (a,)
    m, width = parts[0].shape
    nc, k, ncol = b.shape
    tko, tn = _pick(k, 1024), _pick(math.gcd(ncol, width), MM_CONTRACT)
    per, each, nn = ncol // tn, width // tn, len(parts) * width // tn
    osz = jnp.dtype(out_dtype).itemsize
    tm = _mm_rows(m, lambda t: [(t, tn, parts[0].dtype.itemsize, 2 * len(parts)), (tko, tn, 2, 2), (t, tko, osz, 2),
                                (t, tko, 4, nn > 1)])

    def body(*refs):
        a_refs, b_ref, rest = refs[: len(parts)], refs[len(parts)], refs[len(parts) + 1 :]
        o_ref = rest[len(after)]
        step = pl.program_id(2)

        def finish(r):
            o_ref[...] = r.astype(o_ref.dtype)

        def use(a_ref):
            part = lax.dot_general(a_ref[...].astype(BF16), b_ref[...], NT_DIMS, preferred_element_type=F32)
            _accumulate(rest[-1], step, nn, part, finish)

        _for_part(step // each, a_refs, use)

    return pl.pallas_call(
        body,
        name=name,
        grid=(m // tm, k // tko, nn),
        in_specs=[_part_spec((tm, tn), p, each, lambda i, j, s: (i, s)) for p in range(len(parts))]
        + [pl.BlockSpec((None, tko, tn), lambda i, j, s: (s // per, j, s % per))]
        + [pl.BlockSpec(t.shape, lambda i, j, s: (0, 0)) for t in after],
        out_specs=pl.BlockSpec((tm, tko), lambda i, j, s: (i, j)),
        out_shape=jax.ShapeDtypeStruct((m, k), out_dtype),
        scratch_shapes=[pltpu.VMEM((tm, tko), F32)] * (nn > 1),
        compiler_params=_params("parallel", "parallel", "arbitrary"),
    )(*parts, b, *after)


def _for_part(which, refs, use):
    if len(refs) == 1:
        use(refs[0])
        return
    for p, ref in enumerate(refs):
        pl.when(which == p)(functools.partial(use, ref))


def _part_spec(block, p, each, tile_of):
    def index(*ids):
        r, c = tile_of(*ids)
        return r, jnp.clip(c - p * each, 0, each - 1)

    return pl.BlockSpec(block, index)


def _mm_tn(a, g, nc, *, name):
    parts = tuple(g) if isinstance(g, (tuple, list)) else (g,)
    s, k = a.shape
    width = parts[0].shape[1]
    ncol = len(parts) * width // nc
    ts, tn = _pick(s, MM_CONTRACT, 16), _pick(math.gcd(ncol, width), 1536)
    per, each, ns = ncol // tn, width // tn, s // ts
    tko = _mm_rows(k, lambda t: [(ts, t, a.dtype.itemsize, 2), (ts, tn, parts[0].dtype.itemsize, 2 * len(parts)),
                                 (t, tn, 2, 2), (t, tn, 4, ns > 1)], mult=LANES)

    def body(a_ref, *refs):
        g_refs, rest = refs[: len(parts)], refs[len(parts) :]
        o_ref = rest[0]

        def finish(r):
            o_ref[...] = r.astype(o_ref.dtype)

        def use(g_ref):
            part = lax.dot_general(a_ref[...].astype(BF16), g_ref[...].astype(BF16), TN_DIMS, preferred_element_type=F32)
            _accumulate(rest[-1], pl.program_id(2), ns, part, finish)

        _for_part(pl.program_id(1) // each, g_refs, use)

    return pl.pallas_call(
        body,
        name=name,
        grid=(k // tko, nc * per, ns),
        in_specs=[pl.BlockSpec((ts, tko), lambda i, j, t: (t, i))]
        + [_part_spec((ts, tn), p, each, lambda i, j, t: (t, j)) for p in range(len(parts))],
        out_specs=pl.BlockSpec((None, tko, tn), lambda i, j, t: (j // per, i, j % per)),
        out_shape=jax.ShapeDtypeStruct((nc, k, ncol), BF16),
        scratch_shapes=[pltpu.VMEM((tko, tn), F32)] * (ns > 1),
        compiler_params=_params("parallel", "parallel", "arbitrary"),
    )(a, *parts)


ROW_TILE = 256


def _cast_slab(w, slab, *, name):
    _, rows, cols = w.shape
    tr = _pick(rows, ROW_TILE, 16)

    def body(w_ref, o_ref):
        o_ref[...] = w_ref[...].astype(o_ref.dtype)

    return pl.pallas_call(
        body,
        name=name,
        grid=(rows // tr,),
        in_specs=[pl.BlockSpec((None, tr, cols), lambda i: (slab, i, 0))],
        out_specs=pl.BlockSpec((tr, cols), lambda i: (i, 0)),
        out_shape=jax.ShapeDtypeStruct((rows, cols), BF16),
        compiler_params=_params("parallel"),
    )(w)


def _rms_fwd(h, g, *, name):
    s, d = h.shape
    tr = _pick(s, ROW_TILE, 16)

    def body(h_ref, g_ref, o_ref):
        x = h_ref[...]
        r = lax.rsqrt(jnp.mean(x * x, axis=-1, keepdims=True) + EPS)
        o_ref[...] = (x * r * g_ref[...]).astype(o_ref.dtype)

    return pl.pallas_call(
        body,
        name=name,
        grid=(s // tr,),
        in_specs=[pl.BlockSpec((tr, d), lambda i: (i, 0)), pl.BlockSpec((1, d), lambda i: (0, 0))],
        out_specs=pl.BlockSpec((tr, d), lambda i: (i, 0)),
        out_shape=jax.ShapeDtypeStruct((s, d), BF16),
        compiler_params=_params("parallel"),
    )(h, g)


def _rms_bwd(h, g, dy, dres, *, name):
    s, d = h.shape
    tr = _pick(s, ROW_TILE, 16)

    def body(h_ref, g_ref, dy_ref, dres_ref, dh_ref, dhb_ref, dg_ref):
        x = h_ref[...]
        r = lax.rsqrt(jnp.mean(x * x, axis=-1, keepdims=True) + EPS)
        xhat = x * r
        dyv = dy_ref[...].astype(F32)
        dyg = dyv * g_ref[...]
        dx = r * (dyg - xhat * jnp.mean(dyg * xhat, axis=-1, keepdims=True))
        dh = dres_ref[...] + dx
        dh_ref[...] = dh
        dhb_ref[...] = dh.astype(dhb_ref.dtype)
        part = jnp.sum(dyv * xhat, axis=0, keepdims=True)

        @pl.when(pl.program_id(0) == 0)
        def _():
            dg_ref[...] = part

        @pl.when(pl.program_id(0) != 0)
        def _():
            dg_ref[...] += part

    row = pl.BlockSpec((tr, d), lambda i: (i, 0))
    vec = pl.BlockSpec((1, d), lambda i: (0, 0))
    return pl.pallas_call(
        body,
        name=name,
        grid=(s // tr,),
        in_specs=[row, vec, row, row],
        out_specs=[row, row, vec],
        out_shape=[jax.ShapeDtypeStruct((s, d), F32), jax.ShapeDtypeStruct((s, d), BF16),
                   jax.ShapeDtypeStruct((1, d), F32)],
        compiler_params=_params("arbitrary"),
    )(h, g, dy, dres)


def _loss_head(h, g, target, *, name):
    s, d = h.shape
    tr = _pick(s, ROW_TILE, 16)

    def body(h_ref, g_ref, t_ref, dh_ref, dhb_ref, dg_ref, loss_ref):
        x = h_ref[...]
        r = lax.rsqrt(jnp.mean(x * x, axis=-1, keepdims=True) + EPS)
        xhat = x * r
        gv = g_ref[...]
        err = xhat * gv - t_ref[...]
        dyv = err * (1.0 / d)
        dyg = dyv * gv
        dh = r * (dyg - xhat * jnp.mean(dyg * xhat, axis=-1, keepdims=True))
        dh_ref[...] = dh
        dhb_ref[...] = dh.astype(dhb_ref.dtype)
        part = jnp.sum(dyv * xhat, axis=0, keepdims=True)
        lpart =jnp.sum(jnp.sum(err * err, axis=-1, keepdims=True), axis=0, keepdims=True) * (0.5 / d)

        @pl.when(pl.program_id(0) == 0)
        def _():
            dg_ref[...] = part
            loss_ref[...] = jnp.broadcast_to(lpart, loss_ref.shape)

        @pl.when(pl.program_id(0) != 0)
        def _():
            dg_ref[...] += part
            loss_ref[...] += jnp.broadcast_to(lpart, loss_ref.shape)

    row = pl.BlockSpec((tr, d), lambda i: (i, 0))
    vec = pl.BlockSpec((1, d), lambda i: (0, 0))
    return pl.pallas_call(
        body,
        name=name,
        grid=(s // tr,),
        in_specs=[row, vec, row],
        out_specs=[row, row, vec, pl.BlockSpec((1, LANES), lambda i: (0, 0))],
        out_shape=[
            jax.ShapeDtypeStruct((s, d), F32),
            jax.ShapeDtypeStruct((s, d), BF16),
            jax.ShapeDtypeStruct((1, d), F32),
            jax.ShapeDtypeStruct((1, LANES), F32),
        ],
        compiler_params=_params("arbitrary"),
    )(h, g, target)


def _shift_prev(x):
    row = lax.broadcasted_iota(jnp.int32, x.shape, 0)
    return jnp.where(row == 0, 0.0, pltpu.roll(x, 1, 0))


def _shift_next(x):
    n = x.shape[0]
    row = lax.broadcasted_iota(jnp.int32, x.shape, 0)
    return jnp.where(row == n - 1, 0.0, pltpu.roll(x, n - 1, 0))


def _conv3(x, w):
    xm, xp = _shift_prev(x), _shift_next(x)
    return xm * w[0:1] + x * w[1:2] + xp * w[2:3], xm, xp


def _conv3_t(d, w):
    return _shift_next(d) * w[0:1] + d * w[1:2] + _shift_prev(d) * w[2:3]


def _colsum(x):
    return jnp.sum(x, axis=0, keepdims=True)


def _ffn_mid_fwd(up, cw, cb, *, name):
    s, f2 = up.shape
    f = f2 // 2
    ncol = cw.shape[2]
    tc = _pick(ncol, 256)
    nt, per = f // tc, ncol // tc

    def body(ug_ref, uu_ref, wg_ref, wu_ref, bg_ref, bu_ref, o_ref, cg_ref, cu_ref):
        cg = _conv3(ug_ref[...].astype(F32), wg_ref[...])[0] + bg_ref[...]
        cu = _conv3(uu_ref[...].astype(F32), wu_ref[...])[0] + bu_ref[...]
        o_ref[...] = (cg * (1.0 / (1.0 + jnp.exp(-cg))) * cu).astype(o_ref.dtype)
        cg_ref[...] = cg.astype(cg_ref.dtype)
        cu_ref[...] = cu.astype(cu_ref.dtype)

    out = pl.BlockSpec((s, tc), lambda j: (0, j))
    return pl.pallas_call(
        body,
        name=name,
        grid=(nt,),
        in_specs=[
            pl.BlockSpec((s, tc), lambda j: (0, j)),
            pl.BlockSpec((s, tc), lambda j: (0, nt + j)),
            pl.BlockSpec((None, 3, tc), lambda j: (j // per, 0, j % per)),
            pl.BlockSpec((None, 3, tc), lambda j: ((nt + j) // per, 0, (nt + j) % per)),
            pl.BlockSpec((1, tc), lambda j: (0, j)),
            pl.BlockSpec((1, tc), lambda j: (0, nt + j)),
        ],
        out_specs=[out, out, out],
        out_shape=[jax.ShapeDtypeStruct((s, f), BF16)] * 3,
        compiler_params=_params("parallel"),
    )(up, up, cw, cw, cb, cb)


def _ffn_mid_bwd(up, cg, cu, cw, dact, *, name):
    s, f2 = up.shape
    f = f2 // 2
    ncol = cw.shape[2]
    tc = _pick(ncol, 256)
    nt, per = f // tc, ncol // tc

    def side(dc, u_ref, w_ref, du_ref, dw_ref, db_ref):
        w, u = w_ref[...], u_ref[...].astype(F32)
        nxt, prv = _shift_next(dc), _shift_prev(dc)
        du_ref[...] = (nxt * w[0:1] + dc * w[1:2] + prv * w[2:3]).astype(du_ref.dtype)
        dw_ref[0:1, :] = _colsum(nxt * u)
        dw_ref[1:2, :] = _colsum(dc * u)
        dw_ref[2:3, :] = _colsum(prv * u)
        db_ref[...] = _colsum(dc)

    def body(ug_ref, uu_ref, cg_ref, cu_ref, wg_ref, wu_ref, da_ref, dug_ref, duu_ref, dwg_ref, dwu_ref, dbg_ref, dbu_ref):
        cgv, cuv, da = cg_ref[...].astype(F32), cu_ref[...].astype(F32), da_ref[...].astype(F32)
        sig = 1.0 / (1.0 + jnp.exp(-cgv))
        side(da * cuv * (sig * (1.0 + cgv * (1.0 - sig))), ug_ref, wg_ref, dug_ref, dwg_ref, dbg_ref)
        side(da * (cgv * sig), uu_ref, wu_ref, duu_ref, dwu_ref, dbu_ref)

    col = pl.BlockSpec((s, tc), lambda j: (0, j))
    w3 = pl.BlockSpec((3, tc), lambda j: (0, j))
    b1 = pl.BlockSpec((1, tc), lambda j: (0, j))
    return pl.pallas_call(
        body,
        name=name,
        grid=(nt,),
        in_specs=[
            col,
            pl.BlockSpec((s, tc), lambda j: (0, nt + j)),
            col,
            col,
            pl.BlockSpec((None, 3, tc), lambda j: (j // per, 0, j % per)),
            pl.BlockSpec((None, 3, tc), lambda j: ((nt + j) // per, 0, (nt + j) % per)),
            col,
        ],
        out_specs=[col, col, w3, w3, b1, b1],
        out_shape=[
            jax.ShapeDtypeStruct((s, f), BF16),
            jax.ShapeDtypeStruct((s, f), BF16),
            jax.ShapeDtypeStruct((3, f), F32),
            jax.ShapeDtypeStruct((3, f), F32),
            jax.ShapeDtypeStruct((1, f), F32),
            jax.ShapeDtypeStruct((1, f), F32),
        ],
        compiler_params=_params("parallel"),
    )(up, up, cg, cu, cw, cw, dact)


def _sc_mid_fwd(z, cw, *, name):
    s, d3 = z.shape
    d = d3 // 3
    ncol = cw.shape[2]
    tc = _pick(ncol, 256)
    nt, per = d // tc, ncol // tc

    def body(gb_ref, gc_ref, hh_ref, w_ref, o_ref):
        p = gc_ref[...].astype(F32) * hh_ref[...].astype(F32)
        o_ref[...] = (gb_ref[...].astype(F32) * _conv3(p, w_ref[...])[0]).astype(o_ref.dtype)

    return pl.pallas_call(
        body,
        name=name,
        grid=(nt,),
        in_specs=[
            pl.BlockSpec((s, tc), lambda j: (0, j)),
            pl.BlockSpec((s, tc), lambda j: (0, nt + j)),
            pl.BlockSpec((s, tc), lambda j: (0, 2 * nt + j)),
            pl.BlockSpec((None, 3, tc), lambda j: (j // per, 0, j % per)),
        ],
        out_specs=pl.BlockSpec((s, tc), lambda j: (0, j)),
        out_shape=jax.ShapeDtypeStruct((s, d), BF16),
        compiler_params=_params("parallel"),
    )(z, z, z, cw)


def _sc_mid_bwd(z, cw, dmid, *, name):
    s, d3 = z.shape
    d = d3 // 3
    ncol = cw.shape[2]
    tc = _pick(ncol, 256)
    nt, per = d // tc, ncol // tc

    def body(gb_ref, gc_ref, hh_ref, w_ref, dm_ref, dgb_ref, dgc_ref, dhh_ref, dw_ref):
        gb, gc, hh = gb_ref[...].astype(F32), gc_ref[...].astype(F32), hh_ref[...].astype(F32)
        w = w_ref[...]
        p = gc * hh
        cv, pm, pp = _conv3(p, w)
        dm = dm_ref[...].astype(F32)
        dgb_ref[...] = (dm * cv).astype(dgb_ref.dtype)
        dcv = dm * gb
        dp = _conv3_t(dcv, w)
        dgc_ref[...] = (dp * hh).astype(dgc_ref.dtype)
        dhh_ref[...] = (dp * gc).astype(dhh_ref.dtype)
        dw_ref[0:1, :] = _colsum(dcv * pm)
        dw_ref[1:2, :] = _colsum(dcv * p)
        dw_ref[2:3, :] = _colsum(dcv * pp)

    col = pl.BlockSpec((s, tc), lambda j: (0, j))
    return pl.pallas_call(
        body,
        name=name,
        grid=(nt,),
        in_specs=[
            col,
            pl.BlockSpec((s, tc), lambda j: (0, nt + j)),
            pl.BlockSpec((s, tc), lambda j: (0, 2 * nt + j)),
            pl.BlockSpec((None, 3, tc), lambda j: (j // per, 0, j % per)),
            col,
        ],
        out_specs=[col, col, col, pl.BlockSpec((3, tc), lambda j: (0, j))],
        out_shape=[jax.ShapeDtypeStruct((s, d), BF16)] * 3 + [jax.ShapeDtypeStruct((3, d), F32)],
        compiler_params=_params("parallel"),
    )(z, z, z, cw, dmid)


NA_KEYS = NA_WIN_R * GRID_W
NA_ROWS_PER_STEP = 8


def _na_row_start(r, rows):
    return jnp.clip(r - NA_WIN_R // 2, 0, rows - NA_WIN_R)


def _na_bias_slot(r, rows):
    return _na_row_start(r, rows) - r + NA_WIN_R - 1


def _na_base(rpb):
    h = rpb.shape[0]
    pos, neg = rpb[:, :, NA_WIN_C - 1:], rpb[:, :, : NA_WIN_C - 1]
    zeros = jnp.zeros((h, NA_WIN_R, GRID_W - 2 * NA_WIN_C + 1), F32)
    out = []
    for first in range(NA_WIN_R):
        p = pos[:, first : first + NA_WIN_R]
        n = jnp.roll(neg[:, first : first + NA_WIN_R], -1, axis=1)
        out.append(jnp.concatenate([p, zeros, n], axis=-1).reshape(h, 1, NA_KEYS))
    return jnp.stack(out, axis=1)


def _skew_right(x):
    return pltpu.roll(x, 0, 1, stride=1, stride_axis=0)


def _skew_left(x):
    n = x.shape[1]
    row = lax.broadcasted_iota(jnp.int32, x.shape, 0)
    for b in range(GRID_W.bit_length() - 1):
        x = jnp.where(((row >> b) & 1) == 1, pltpu.roll(x, n - (1 << b), 1), x)
    return x


def _na_bias(base, *, name):
    h = base.shape[0]

    def body(b_ref, o_ref):
        q = lax.broadcasted_iota(jnp.int32, (GRID_W, NA_KEYS), 0)
        kc = lax.broadcasted_iota(jnp.int32, (GRID_W, NA_KEYS), 1) % GRID_W
        start = jnp.clip(q - NA_WIN_C // 2, 0, GRID_W - NA_WIN_C)
        inside = (kc >= start) & (kc < start + NA_WIN_C)
        for slot in range(NA_WIN_R):
            x = _skew_right(jnp.broadcast_to(b_ref[slot], (GRID_W, NA_KEYS)))
            o_ref[slot] = jnp.where(inside, x, NEG_INF)

    return pl.pallas_call(
        body,
        name=name,
        grid=(h,),
        in_specs=[pl.BlockSpec((None, NA_WIN_R, 1, NA_KEYS), lambda i: (i, 0, 0, 0))],
        out_specs=pl.BlockSpec((None, NA_WIN_R, GRID_W, NA_KEYS), lambda i: (i, 0, 0, 0)),
        out_shape=jax.ShapeDtypeStruct((h, NA_WIN_R, GRID_W, NA_KEYS), F32),
        compiler_params=_params("parallel"),
    )(base)


def _na_specs(s, heads, per):
    q = pl.BlockSpec((per * GRID_W, HEAD_DIM), lambda h, r: (r, h))
    k = pl.BlockSpec((s, HEAD_DIM), lambda h, r: (0, heads + h))
    v = pl.BlockSpec((s, HEAD_DIM), lambda h, r: (0, 2 * heads + h))
    bias = pl.BlockSpec((None, NA_WIN_R, GRID_W, NA_KEYS), lambda h, r: (h, 0, 0, 0))
    return q, k, v, bias


def _na_probs(q, k, bias):
    sc = lax.dot_general(q, k, NT_DIMS, preferred_element_type=F32) * (HEAD_DIM ** -0.5) + bias
    p = jnp.exp(sc - jnp.max(sc, axis=-1, keepdims=True))
    return p, jnp.sum(p, axis=-1, keepdims=True)


def _na_fwd(qkv, bias, *, name):
    s = qkv.shape[0]
    heads = qkv.shape[1] // (3 * HEAD_DIM)
    rows = s // GRID_W
    per = _pick(rows, NA_ROWS_PER_STEP, 1)

    def body(q_ref, k_ref, v_ref, b_ref, o_ref):
        for i in range(per):
            r = pl.program_id(1) * per + i
            win = pl.ds(pl.multiple_of(_na_row_start(r, rows) * GRID_W, GRID_W), NA_KEYS)
            mine = pl.ds(i * GRID_W, GRID_W)
            p, l = _na_probs(q_ref[mine, :], k_ref[win, :], b_ref[_na_bias_slot(r, rows)])
            o = jnp.dot(p.astype(BF16), v_ref[win, :], preferred_element_type=F32)
            o_ref[mine, :] = (o / l).astype(o_ref.dtype)

    q, k, v, b = _na_specs(s, heads, per)
    return pl.pallas_call(
        body,
        name=name,
        grid=(heads, rows // per),
        in_specs=[q, k, v, b],
        out_specs=q,
        out_shape=jax.ShapeDtypeStruct((s, heads * HEAD_DIM), BF16),
        compiler_params=_params("parallel", "arbitrary"),
    )(qkv, qkv, qkv, bias)


def _na_bwd(qkv, bias, dout, *, name):
    s = qkv.shape[0]
    heads = qkv.shape[1] // (3 * HEAD_DIM)
    rows = s // GRID_W
    per = _pick(rows, NA_ROWS_PER_STEP, 1)
    steps = rows // per
    scale = HEAD_DIM ** -0.5

    def body(q_ref, k_ref, v_ref, b_ref, do_ref, dq_ref, dk_ref, dv_ref, db_ref, dk_acc, dv_acc):
        step = pl.program_id(1)

        @pl.when(step == 0)
        def _():
            dk_acc[...] = jnp.zeros_like(dk_acc)
            dv_acc[...] = jnp.zeros_like(dv_acc)
            db_ref[...] = jnp.zeros_like(db_ref)

        for i in range(per):
            r = step * per + i
            win = pl.ds(pl.multiple_of(_na_row_start(r, rows) * GRID_W, GRID_W), NA_KEYS)
            mine = pl.ds(i * GRID_W, GRID_W)
            slot = _na_bias_slot(r, rows)
            q, k, v, do = q_ref[mine, :], k_ref[win, :], v_ref[win, :], do_ref[mine, :]
            p, l = _na_probs(q, k, b_ref[slot])
            pn = p / l
            dp = lax.dot_general(do, v, NT_DIMS, preferred_element_type=F32)
            ds = pn * (dp - jnp.sum(pn * dp, axis=-1, keepdims=True))
            dsb = ds.astype(BF16)
            dq_ref[mine, :] = (jnp.dot(dsb, k, preferred_element_type=F32) * scale).astype(dq_ref.dtype)
            dk_acc[win, :] += lax.dot_general(dsb, q, TN_DIMS, preferred_element_type=F32) * scale
            dv_acc[win, :] += lax.dot_general(pn.astype(BF16), do, TN_DIMS, preferred_element_type=F32)
            db_ref[slot] += ds

        @pl.when(step == steps - 1)
        def _():
            dk_ref[...] = dk_acc[...].astype(dk_ref.dtype)
            dv_ref[...] = dv_acc[...].astype(dv_ref.dtype)

    q, k, v, b = _na_specs(s, heads, per)
    kv_out = pl.BlockSpec((s, HEAD_DIM), lambda h, r: (0, h))
    shape = jax.ShapeDtypeStruct((s, heads * HEAD_DIM), BF16)
    return pl.pallas_call(
        body,
        name=name,
        grid=(heads, steps),
        in_specs=[q, k, v, b, q],
        out_specs=[q, kv_out, kv_out, b],
        out_shape=[shape, shape, shape, jax.ShapeDtypeStruct((heads, NA_WIN_R, GRID_W, NA_KEYS), F32)],
        scratch_shapes=[pltpu.VMEM((s, HEAD_DIM), F32), pltpu.VMEM((s, HEAD_DIM), F32)],
        compiler_params=_params("parallel", "arbitrary"),
    )(qkv, qkv, qkv, bias, dout)


RPB_ROWS = 2 * NA_WIN_R - 1
RPB_COLS = 2 * NA_WIN_C - 1
RPB_PAD = 512


def _rpb_fold_matrix():
    idx = jnp.arange(NA_WIN_R * NA_KEYS, dtype=jnp.int32)
    first, i, kc = idx // NA_KEYS, (idx // GRID_W) % NA_WIN_R, idx % GRID_W
    pos, neg = kc < NA_WIN_C, kc >= GRID_W - NA_WIN_C + 1
    dr = jnp.where(pos, first + i, first + (i + 1) % NA_WIN_R)
    dc = jnp.where(pos, kc + NA_WIN_C - 1, kc - (GRID_W - NA_WIN_C + 1))
    target = jnp.where(pos | neg, dr * RPB_COLS + dc, -1)
    return (target[:, None] == jnp.arange(RPB_PAD, dtype=jnp.int32)[None, :]).astype(F32)


def _rpb_fold(dbias, *, name):
    h = dbias.shape[0]

    def skew_body(g_ref, o_ref):
        for slot in range(NA_WIN_R):
            o_ref[slot] = _colsum(_skew_left(g_ref[slot]))

    skewed = pl.pallas_call(
        skew_body,
        name=name + "_skew",
        grid=(h,),
        in_specs=[pl.BlockSpec((None, NA_WIN_R, GRID_W, NA_KEYS), lambda i: (i, 0, 0, 0))],
        out_specs=pl.BlockSpec((None, NA_WIN_R, 1, NA_KEYS), lambda i: (i, 0, 0, 0)),
        out_shape=jax.ShapeDtypeStruct((h, NA_WIN_R, 1, NA_KEYS), F32),
        compiler_params=_params("parallel"),
    )(dbias)

    def fold_body(g_ref, m_ref, o_ref):
        o_ref[...] = jnp.dot(g_ref[...], m_ref[...], preferred_element_type=F32, precision=lax.Precision.HIGHEST)

    return pl.pallas_call(
        fold_body,
        name=name,
        out_shape=jax.ShapeDtypeStruct((h, RPB_PAD), F32),
        compiler_params=pltpu.CompilerParams(vmem_limit_bytes=VMEM_LIMIT),
    )(skewed.reshape(h, NA_WIN_R * NA_KEYS), _rpb_fold_matrix())


GQA_Q_TILE = 256
GQA_Q_TILE_BWD = 512


def _rope_tables(s):
    t = jnp.arange(s)
    row = (t // GRID_W).astype(F32)[:, None]
    col = (t % GRID_W).astype(F32)[:, None]
    half = HEAD_DIM // 2
    inv = ROPE_THETA ** (-jnp.arange(0, half, 2, dtype=F32) / half)
    ang = jnp.concatenate([row * inv, row * inv, col * inv, col * inv], axis=-1)
    return jnp.cos(ang), jnp.sin(ang)


def _rot_half(y):
    quarter = HEAD_DIM // 4
    lane = lax.broadcasted_iota(jnp.int32, y.shape, 1)
    low = (lane % (2 * quarter)) < quarter
    return jnp.where(low, -pltpu.roll(y, HEAD_DIM - quarter, 1), pltpu.roll(y, quarter, 1))


def _gqa_prep_fwd(qkv, gq, gk, cos, sin, hq, hkv, *, name):
    s = qkv.shape[0]

    def body(x_ref, gq_ref, gk_ref, cos_ref, sin_ref, o_ref):
        isq = pl.program_id(0) < hq
        x = x_ref[...].astype(F32)
        g = jnp.where(isq, gq_ref[...], gk_ref[...])
        y = x * lax.rsqrt(jnp.mean(x * x, axis=-1, keepdims=True) + EPS) * g
        z = y * cos_ref[...] + _rot_half(y) * sin_ref[...]
        o_ref[...] = (z * jnp.where(isq, HEAD_DIM ** -0.5, 1.0)).astype(o_ref.dtype)

    head = pl.BlockSpec((s, HEAD_DIM), lambda h: (0, h))
    vec = pl.BlockSpec((1, HEAD_DIM), lambda h: (0, 0))
    tab = pl.BlockSpec((s, HEAD_DIM), lambda h: (0, 0))
    return pl.pallas_call(
        body,
        name=name,
        grid=(hq + hkv,),
        in_specs=[head, vec, vec, tab, tab],
        out_specs=head,
        out_shape=jax.ShapeDtypeStruct((s, (hq + hkv) * HEAD_DIM), BF16),
        compiler_params=_params("parallel"),
    )(qkv, gq, gk, cos, sin)


def _gqa_prep_bwd(qkv, gq, gk, cos, sin, dqn, dkn, hq, hkv, *, name):
    s = qkv.shape[0]

    def body(x_ref, gq_ref, gk_ref, cos_ref, sin_ref, dq_ref, dk_ref, dx_ref, dgq_ref, dgk_ref):
        hh = pl.program_id(0)
        isq = hh < hq
        x = x_ref[...].astype(F32)
        g = jnp.where(isq, gq_ref[...], gk_ref[...])
        r = lax.rsqrt(jnp.mean(x * x, axis=-1, keepdims=True) + EPS)
        xhat = x * r
        dz = jnp.where(isq, dq_ref[...].astype(F32) * (HEAD_DIM ** -0.5), dk_ref[...].astype(F32))
        dy = dz * cos_ref[...] - _rot_half(dz * sin_ref[...])
        dyg = dy * g
        dx_ref[...] = (r * (dyg - xhat * jnp.mean(dyg * xhat, axis=-1, keepdims=True))).astype(dx_ref.dtype)
        part = _colsum(dy * xhat)

        @pl.when(hh == 0)
        def _():
            dgq_ref[...] = jnp.zeros_like(dgq_ref)
            dgk_ref[...] = jnp.zeros_like(dgk_ref)

        @pl.when(isq)
        def _():
            dgq_ref[...] += part

        @pl.when(jnp.logical_not(isq))
        def _():
            dgk_ref[...] += part

    head = pl.BlockSpec((s, HEAD_DIM), lambda h: (0, h))
    vec = pl.BlockSpec((1, HEAD_DIM), lambda h: (0, 0))
    tab = pl.BlockSpec((s, HEAD_DIM), lambda h: (0, 0))
    return pl.pallas_call(
        body,
        name=name,
        grid=(hq + hkv,),
        in_specs=[
            head,
            vec,
            vec,
            tab,
            tab,
            pl.BlockSpec((s, HEAD_DIM), lambda h: (0, jnp.minimum(h, hq - 1))),
            pl.BlockSpec((s, HEAD_DIM), lambda h: (0, jnp.maximum(h - hq, 0))),
        ],
        out_specs=[head, vec, vec],
        out_shape=[
            jax.ShapeDtypeStruct((s, (hq + hkv) * HEAD_DIM), BF16),
            jax.ShapeDtypeStruct((1, HEAD_DIM), F32),
            jax.ShapeDtypeStruct((1, HEAD_DIM), F32),
        ],
        compiler_params=_params("arbitrary"),
    )(qkv, gq, gk, cos, sin, dqn, dkn)


def _gqa_fwd(qkn, qkv, hq, hkv, *, name):
    s = qkv.shape[0]
    tq = _pick(s, GQA_Q_TILE, 16)

    def body(q_ref, k_ref, v_ref, o_ref):
        sc = lax.dot_general(q_ref[...], k_ref[...], NT_DIMS, preferred_element_type=F32)
        p = jnp.exp(sc - jnp.max(sc, axis=-1, keepdims=True))
        l = jnp.sum(p, axis=-1, keepdims=True)
        o_ref[...] = (jnp.dot(p.astype(BF16), v_ref[...], preferred_element_type=F32) / l).astype(o_ref.dtype)

    q = pl.BlockSpec((tq, HEAD_DIM), lambda h, i: (i, h))
    return pl.pallas_call(
        body,
        name=name,
        grid=(hq, s // tq),
        in_specs=[
            q,
            pl.BlockSpec((s, HEAD_DIM), lambda h, i: (0, hq + h // GQA_GROUP)),
            pl.BlockSpec((s, HEAD_DIM), lambda h, i: (0, hq + hkv + h // GQA_GROUP)),
        ],
        out_specs=q,
        out_shape=jax.ShapeDtypeStruct((s, hq * HEAD_DIM), BF16),
        compiler_params=_params("parallel", "parallel"),
    )(qkn, qkn, qkv)


def _gqa_bwd(qkn, qkv, dout, hq, hkv, *, name):
    s = qkv.shape[0]
    tq = _pick(s, GQA_Q_TILE_BWD, 16)
    nq = s // tq

    def body(q_ref, k_ref, v_ref, do_ref, dq_ref, dk_ref, dv_ref, dk_acc, dv_acc):
        g, i = pl.program_id(1), pl.program_id(2)

        @pl.when((g == 0) & (i == 0))
        def _():
            dk_acc[...] = jnp.zeros_like(dk_acc)
            dv_acc[...] = jnp.zeros_like(dv_acc)

        q, k, v, do = q_ref[...], k_ref[...], v_ref[...], do_ref[...]
        sc = lax.dot_general(q, k, NT_DIMS, preferred_element_type=F32)
        p = jnp.exp(sc - jnp.max(sc, axis=-1, keepdims=True))
        pn = p / jnp.sum(p, axis=-1, keepdims=True)
        dp = lax.dot_general(do, v, NT_DIMS, preferred_element_type=F32)
        dsb = (pn * (dp - jnp.sum(pn * dp, axis=-1, keepdims=True))).astype(BF16)
        dq_ref[...] = jnp.dot(dsb, k, preferred_element_type=F32).astype(dq_ref.dtype)
        dk_acc[...] += lax.dot_general(dsb, q, TN_DIMS, preferred_element_type=F32)
        dv_acc[...] += lax.dot_general(pn.astype(BF16), do, TN_DIMS, preferred_element_type=F32)

        @pl.when((g == GQA_GROUP - 1) & (i == nq - 1))
        def _():
            dk_ref[...] = dk_acc[...].astype(dk_ref.dtype)
            dv_ref[...] = dv_acc[...].astype(dv_ref.dtype)

    q = pl.BlockSpec((tq, HEAD_DIM), lambda kv, g, i: (i, kv * GQA_GROUP + g))
    kv_out = pl.BlockSpec((s, HEAD_DIM), lambda kv, g, i: (0, kv))
    return pl.pallas_call(
        body,
        name=name,
        grid=(hkv, GQA_GROUP, nq),
        in_specs=[
            q,
            pl.BlockSpec((s, HEAD_DIM), lambda kv, g, i: (0, hq + kv)),
            pl.BlockSpec((s, HEAD_DIM), lambda kv, g, i: (0, hq + hkv + kv)),
            q,
        ],
        out_specs=[q, kv_out, kv_out],
        out_shape=[
            jax.ShapeDtypeStruct((s, hq * HEAD_DIM), BF16),
            jax.ShapeDtypeStruct((s, hkv * HEAD_DIM), BF16),
            jax.ShapeDtypeStruct((s, hkv * HEAD_DIM), BF16),
        ],
        scratch_shapes=[pltpu.VMEM((s, HEAD_DIM), F32), pltpu.VMEM((s, HEAD_DIM), F32)],
        compiler_params=_params("parallel", "arbitrary", "arbitrary"),
    )(qkn, qkn, qkv, dout)


ADAM_ROWS = 128


def _adam_update(w, g, m, v):
    m = ADAM_B1 * m + (1.0 - ADAM_B1) * g
    v = ADAM_B2 * v + (1.0 - ADAM_B2) * (g * g)
    m_hat = m / (1.0 - ADAM_B1 ** ADAM_STEP)
    v_hat = v / (1.0 - ADAM_B2 ** ADAM_STEP)
    return -ADAM_LR * (m_hat / (jnp.sqrt(v_hat) + ADAM_EPS) + ADAM_WD * w), m, v


def _adamw_slab(w, m, v, own, sib, slab, prev, after, *, name):
    _, rows, cols = w.shape
    tr = _pick(rows, ADAM_ROWS, 16)
    tokens = [] if after is None else [after]

    def body(w_ref, m_ref, v_ref, own_ref, sib_ref, *rest):
        g_ref, d_ref, nm_ref, nv_ref = rest[-4:]
        g = own_ref[0].astype(F32) + sib_ref[0].astype(F32)
        for q in range(1, N_CHIPS):
            g = g + (own_ref[q].astype(F32) + sib_ref[q].astype(F32))
        g_ref[...] = g
        d_ref[...], nm_ref[...], nv_ref[...] = _adam_update(w_ref[...], g, m_ref[...], v_ref[...])

    one = pl.BlockSpec((None, tr, cols), lambda i: (slab, i, 0))
    piece = pl.BlockSpec((N_CHIPS, tr, cols), lambda i: (0, i, 0))
    carried = [] if prev is None else list(prev)
    shape = jax.ShapeDtypeStruct(w.shape, F32)
    return pl.pallas_call(
        body,
        name=name,
        grid=(rows // tr,),
        in_specs=[one] * 3
        + [piece] * 2
        + [pl.BlockSpec(memory_space=pl.ANY)] * len(carried)
        + [pl.BlockSpec(t.shape, lambda i: (0, 0)) for t in tokens],
        out_specs=[one] * 4,
        out_shape=[shape] * 4,
        input_output_aliases={5 + i: i for i in range(len(carried))},
        compiler_params=_params("parallel"),
    )(w, m, v, own, sib, *carried, *tokens)


def _adamw_small(w, g, m, v, *, name):
    def body(w_ref, g_ref, m_ref, v_ref, d_ref, nm_ref, nv_ref):
        d_ref[...], nm_ref[...], nv_ref[...] = _adam_update(w_ref[...], g_ref[...], m_ref[...], v_ref[...])

    shape = jax.ShapeDtypeStruct(w.shape, F32)
    return pl.pallas_call(
        body,
        name=name,
        out_shape=[shape] * 3,
        compiler_params=pltpu.CompilerParams(vmem_limit_bytes=VMEM_LIMIT),
    )(w, g, m, v)


def _position():
    x, y, c = lax.axis_index("x"), lax.axis_index("y"), lax.axis_index("c")
    return x, y, c, 2 * x + y


def _chip_device(chip, c):
    return (chip >> 1, chip & 1, c)


def _handshake(peers):
    barrier = pltpu.get_barrier_semaphore()
    for peer in peers:
        pl.semaphore_signal(barrier, inc=1, device_id=peer, device_id_type=MESH)
    pl.semaphore_wait(barrier, len(peers))


GATHER_CHUNKS = 2


def _gather_shards(shards, split, *, name, collective_id):
    n = len(shards)
    big = [a for a in range(n) if split[a]]
    small = [a for a in range(n) if not split[a]]
    y_nbr, x_nbr, far = 1, 2, 3

    def body(*refs):
        ins, outs = refs[:n], refs[n : 2 * n]
        near_send, near_recv, far_send, far_recv, pass_send, pass_recv, own_send, own_recv = refs[2 * n :]
        x, y, c, k = _position()
        _handshake([(x, y, 1 - c)] + [_chip_device(k ^ j, c) for j in range(1, N_CHIPS)])

        def own(a, chunk):
            r = shards[a].shape[0]
            part = pl.ds(chunk * (r // GATHER_CHUNKS), r // GATHER_CHUNKS) if split[a] else pl.ds(0, r)
            return pltpu.make_async_remote_copy(
                src_ref=ins[a].at[part],
                dst_ref=outs[a].at[k, part],
                send_sem=own_send.at[a, chunk],
                recv_sem=own_recv.at[a, chunk],
                device_id=(x, y, 1 - c),
                device_id_type=MESH,
            )

        own_copies = [own(a, ch) for a in range(n) for ch in range(GATHER_CHUNKS if split[a] else 1)]
        for cp in own_copies:
            cp.start()

        def run(core):
            sibling = (x, y, 1 - core)
            relay_from, relay_to = (x_nbr, y_nbr) if core == 0 else (y_nbr, x_nbr)

            def rows(a, which, chunk=None):
                r = shards[a].shape[0]
                if not split[a]:
                    return pl.ds(0, r)
                half = r // 2
                if chunk is None:
                    return pl.ds(which * half, half)
                return pl.ds(which * half + chunk * (half // GATHER_CHUNKS), half // GATHER_CHUNKS)

            def direct(a, mask, chunk, src_chip):
                part = rows(a, core, chunk)
                return pltpu.make_async_remote_copy(
                    src_ref=ins[a].at[part],
                    dst_ref=outs[a].at[src_chip, part],
                    send_sem=near_send.at[a, mask - 1, chunk or 0],
                    recv_sem=near_recv.at[a, mask - 1, chunk or 0],
                    device_id=_chip_device(k ^ mask, core),
                    device_id_type=MESH,
                )

            def relay(a, chunk, src_chip, mask):
                part = outs[a].at[src_chip, rows(a, core, chunk)]
                return pltpu.make_async_remote_copy(
                    src_ref=part,
                    dst_ref=part,
                    send_sem=far_send.at[a, chunk or 0],
                    recv_sem=far_recv.at[a, chunk or 0],
                    device_id=_chip_device(k ^ mask, core),
                    device_id_type=MESH,
                )

            def to_sibling(a, mask, which):
                part = outs[a].at[k ^ mask, rows(a, which)]
                return pltpu.make_async_remote_copy(
                    src_ref=part,
                    dst_ref=part,
                    send_sem=pass_send.at[a, mask - 1],
                    recv_sem=pass_recv.at[a, mask - 1],
                    device_id=sibling,
                    device_id_type=MESH,
                )

            sent = []

            def start(cp):
                cp.start()
                sent.append(cp)

            chunks = range(GATHER_CHUNKS)
            for chunk in chunks:
                for a in big:
                    start(direct(a, relay_from, chunk, k))
                    start(direct(a, relay_to, chunk, k))
            for a in small:
                start(direct(a, x_nbr, None, k))
                start(direct(a, y_nbr, None, k))
                start(pltpu.make_async_remote_copy(
                    src_ref=ins[a], dst_ref=outs[a].at[k], send_sem=far_send.at[a, 0], recv_sem=far_recv.at[a, 0],
                    device_id=_chip_device(k ^ far, core), device_id_type=MESH))
            for chunk in chunks:
                for a in big:
                    direct(a, relay_from, chunk, k ^ relay_from).wait_recv()
                    start(relay(a, chunk, k ^ relay_from, relay_to))
            for a in big:
                start(to_sibling(a, relay_from, core))
            for a in big:
                for chunk in chunks:
                    direct(a, relay_to, chunk, k ^ relay_to).wait_recv()
                start(to_sibling(a, relay_to, core))
            for a in big:
                for chunk in chunks:
                    relay(a, chunk, k ^ far, far).wait_recv()
                start(to_sibling(a, far, core))
            for a in small:
                direct(a, x_nbr, None, k ^ x_nbr).wait_recv()
                direct(a, y_nbr, None, k ^ y_nbr).wait_recv()
                relay(a, None, k ^ far, far).wait_recv()
            for a in big:
                for mask in (y_nbr, x_nbr, far):
                    to_sibling(a, mask, 1 - core).wait_recv()
            for cp in sent:
                cp.wait_send()

        for core in (0, 1):
            pl.when(c == core)(functools.partial(run, core))
        for cp in own_copies:
            cp.wait()

    return pl.kernel(
        body,
        name=name,
        out_type=[jax.ShapeDtypeStruct((N_CHIPS,) + a.shape, a.dtype) for a in shards],
        mesh=plsc.ScalarSubcoreMesh(axis_name="sequencer", num_cores=1),
        scratch_types=[
            pltpu.SemaphoreType.DMA((n, 2, GATHER_CHUNKS)),
            pltpu.SemaphoreType.DMA((n, 2, GATHER_CHUNKS)),
            pltpu.SemaphoreType.DMA((n, GATHER_CHUNKS)),
            pltpu.SemaphoreType.DMA((n, GATHER_CHUNKS)),
            pltpu.SemaphoreType.DMA((n, N_CHIPS - 1)),
            pltpu.SemaphoreType.DMA((n, N_CHIPS - 1)),
            pltpu.SemaphoreType.DMA((n, GATHER_CHUNKS)),
            pltpu.SemaphoreType.DMA((n, GATHER_CHUNKS)),
        ],
        compiler_params=pltpu.CompilerParams(collective_id=collective_id),
    )(*shards)


def _scatter_pieces(pieces, *, name, collective_id):
    n = len(pieces)

    def body(*refs):
        ins, own, sib = refs[:n], refs[n : 2 * n], refs[2 * n : 3 * n]
        local_sem, send_sem, recv_sem, pass_send, pass_recv = refs[3 * n :]
        x, y, c, k = _position()
        sibling = (x, y, 1 - c)
        _handshake([sibling] + [_chip_device(k ^ j, c) for j in range(1, N_CHIPS)])

        def over_ici(a, j, piece, slot, to):
            return pltpu.make_async_remote_copy(
                src_ref=ins[a].at[piece],
                dst_ref=own[a].at[slot],
                send_sem=send_sem.at[a, j],
                recv_sem=recv_sem.at[a, j],
                device_id=to,
                device_id_type=MESH,
            )

        def to_sibling(a, j, slot):
            return pltpu.make_async_remote_copy(
                src_ref=own[a].at[slot],
                dst_ref=sib[a].at[slot],
                send_sem=pass_send.at[a, j],
                recv_sem=pass_recv.at[a, j],
                device_id=sibling,
                device_id_type=MESH,
            )

        mine = [pltpu.make_async_copy(ins[a].at[k], own[a].at[k], local_sem.at[a]) for a in range(n)]
        for cp in mine:
            cp.start()
        sent = []
        for j in range(N_CHIPS - 1):
            other = k ^ (j + 1)
            for a in range(n):
                cp = over_ici(a, j, other, k, _chip_device(other, c))
                cp.start()
                sent.append(cp)
        for a in range(n):
            mine[a].wait()
            cp = to_sibling(a, N_CHIPS - 1, k)
            cp.start()
            sent.append(cp)
        for j in range(N_CHIPS - 1):
            other = k ^ (j + 1)
            for a in range(n):
                over_ici(a, j, other, other, sibling).wait_recv()
                cp = to_sibling(a, j, other)
                cp.start()
                sent.append(cp)
        for j in range(N_CHIPS):
            for a in range(n):
                to_sibling(a, j, k).wait_recv()
        for cp in sent:
            cp.wait_send()

    shapes = [jax.ShapeDtypeStruct(a.shape, a.dtype) for a in pieces]
    outs = pl.kernel(
        body,
        name=name,
        out_type=shapes + shapes,
        mesh=plsc.ScalarSubcoreMesh(axis_name="sequencer", num_cores=1),
        scratch_types=[
            pltpu.SemaphoreType.DMA((n,)),
            pltpu.SemaphoreType.DMA((n, N_CHIPS - 1)),
            pltpu.SemaphoreType.DMA((n, N_CHIPS - 1)),
            pltpu.SemaphoreType.DMA((n, N_CHIPS)),
            pltpu.SemaphoreType.DMA((n, N_CHIPS)),
        ],
        compiler_params=pltpu.CompilerParams(collective_id=collective_id),
    )(*pieces)
    return outs[:n], outs[n:]


def _allreduce_small(buf, *, name):
    def body(x_ref, o_ref, slots, send_sem, recv_sem):
        x, y, c, _ = _position()
        me = 4 * x + 2 * y + c
        slots[me] = x_ref[...]

        def copy(d, slot):
            peer = me ^ d
            return pltpu.make_async_remote_copy(
                src_ref=x_ref,
                dst_ref=slots.at[slot],
                send_sem=send_sem.at[d - 1],
                recv_sem=recv_sem.at[d - 1],
                device_id=(peer >> 2, (peer >> 1) & 1, peer & 1),
                device_id_type=MESH,
            )

        sent = [copy(d, me) for d in range(1, N_DEV)]
        for cp in sent:
            cp.start()
        for d in range(1, N_DEV):
            copy(d, me ^ d).wait_recv()
        for cp in sent:
            cp.wait_send()
        acc = slots[0]
        for s in range(1, N_DEV):
            acc = acc + slots[s]
        o_ref[...] = acc

    return pl.pallas_call(
        body,
        name=name,
        in_specs=[pl.BlockSpec(memory_space=pltpu.VMEM)],
        out_specs=pl.BlockSpec(memory_space=pltpu.VMEM),
        out_shape=jax.ShapeDtypeStruct(buf.shape, F32),
        scratch_shapes=[
            pltpu.VMEM((N_DEV,) + buf.shape, F32),
            pltpu.SemaphoreType.DMA((N_DEV - 1,)),
            pltpu.SemaphoreType.DMA((N_DEV - 1,)),
        ],
        compiler_params=pltpu.CompilerParams(vmem_limit_bytes=VMEM_LIMIT),
    )(buf)


def _mixer_of(i):
    return i % N_MIXERS, i // N_MIXERS


def _forward_backward(x, target, norms, layers, send=lambda i, part, pieces: pieces):
    s, d = x.shape
    depth = len(layers)
    heads = d // HEAD_DIM
    hkv = heads // GQA_GROUP
    cos, sin = _rope_tables(s)
    saved = []
    h = x
    for i, lw in enumerate(layers):
        kind, j = _mixer_of(i)
        tag = f"l{i}"
        sv = {"h_in": h}
        a = _rms_fwd(h, norms["mix_norm"][i : i + 1], name=f"{tag}_mix_norm")
        qkv = _mm_nn(a, lw["w_in"], out_dtype=BF16, name=f"{tag}_w_in")
        if kind == 0:
            bias = _na_bias(_na_base(norms["na_rpb"][j]), name=f"{tag}_na_bias")
            o = _na_fwd(qkv, bias, name=f"{tag}_na_fwd")
            sv["bias"] = bias
        elif kind == 1:
            o = _sc_mid_fwd(qkv, lw["sc_conv_w"], name=f"{tag}_sc_fwd")
        else:
            gq, gk = norms["gqa_q_norm"][j : j + 1], norms["gqa_k_norm"][j : j + 1]
            qkn = _gqa_prep_fwd(qkv, gq, gk, cos, sin, heads, hkv, name=f"{tag}_gqa_prep")
            o = _gqa_fwd(qkn, qkv, heads, hkv, name=f"{tag}_gqa_fwd")
            sv["qkn"] = qkn
        h_mid = _mm_nn(o, lw["w_out"], out_dtype=F32, residual=h, name=f"{tag}_w_out")
        b = _rms_fwd(h_mid, norms["ffn_norm"][i : i + 1], name=f"{tag}_ffn_norm")
        up = _mm_nn(b, lw["w_up"], out_dtype=BF16, name=f"{tag}_w_up")
        act, cg, cu = _ffn_mid_fwd(up, lw["ffn_conv_w"], lw["ffn_conv_b"], name=f"{tag}_ffn_fwd")
        h = _mm_nn(act, lw["w_down"], out_dtype=F32, residual=h_mid, name=f"{tag}_w_down")
        sv.update(a=a, qkv=qkv, o=o, h_mid=h_mid, b=b, up=up, act=act, cg=cg, cu=cu)
        saved.append(sv)

    dh, dh_b, d_final, loss = _loss_head(h, norms["final_norm"][None], target, name="loss_head")

    big = [None] * depth
    small = {"final_norm": d_final, "mix_norm": [None] * depth, "ffn_norm": [None] * depth,
             "ffn_conv_w": [None] * depth, "ffn_conv_b": [None] * depth, "na_rpb": {}}
    after = ()
    for i in reversed(range(depth)):
        kind, j = _mixer_of(i)
        tag = f"l{i}b"
        lw, sv = layers[i], saved[i]
        dact = _mm_nt(dh_b, lw["w_down"], out_dtype=BF16, name=f"{tag}_d_act", after=after)
        dw_down = _mm_tn(sv["act"], dh_b, 1, name=f"{tag}_dw_down")
        dug, duu, dwg, dwu, dbg, dbu = _ffn_mid_bwd(
            sv["up"], sv["cg"], sv["cu"], lw["ffn_conv_w"], dact, name=f"{tag}_ffn_bwd"
        )
        small["ffn_conv_w"][i] = jnp.concatenate([dwg, dwu], axis=1)
        small["ffn_conv_b"][i] = jnp.concatenate([dbg, dbu], axis=1)
        dw_up = _mm_tn(sv["b"], (dug, duu), N_CHIPS, name=f"{tag}_dw_up")
        sent_ffn = send(i, "ffn", [dw_up, dw_down.reshape(N_CHIPS, dw_down.shape[1] // N_CHIPS, d)])
        db = _mm_nt((dug, duu), lw["w_up"], out_dtype=F32, name=f"{tag}_d_b")
        dh_mid, dh_mid_b, small["ffn_norm"][i] = _rms_bwd(
            sv["h_mid"], norms["ffn_norm"][i : i + 1], db, dh, name=f"{tag}_ffn_norm"
        )
        do = _mm_nt(dh_mid_b, lw["w_out"], out_dtype=BF16, name=f"{tag}_d_o", after=(_token(dw_up), _token(dw_down)))
        dw_out = _mm_tn(sv["o"], dh_mid_b, 1, name=f"{tag}_dw_out")
        if kind == 0:
            dq, dk, dv, dbias = _na_bwd(sv["qkv"], sv["bias"], do, name=f"{tag}_na_bwd")
            dqkv = jnp.concatenate([dq, dk, dv], axis=1)
            small["na_rpb"][j] = _rpb_fold(dbias, name=f"{tag}_rpb_fold")
        elif kind == 1:
            dgb, dgc, dhh, small["sc_conv_w"] = _sc_mid_bwd(sv["qkv"], lw["sc_conv_w"], do, name=f"{tag}_sc_bwd")
            dqkv = jnp.concatenate([dgb, dgc, dhh], axis=1)
        else:
            gq, gk = norms["gqa_q_norm"][j : j + 1], norms["gqa_k_norm"][j : j + 1]
            dqn, dkn, dv = _gqa_bwd(sv["qkn"], sv["qkv"], do, heads, hkv, name=f"{tag}_gqa_bwd")
            dqk, small["gqa_q_norm"], small["gqa_k_norm"] = _gqa_prep_bwd(
                sv["qkv"], gq, gk, cos, sin, dqn, dkn, heads, hkv, name=f"{tag}_gqa_prep_bwd"
            )
            dqkv = jnp.concatenate([dqk, dv], axis=1)
        dw_in = _mm_tn(sv["a"], dqkv, N_CHIPS, name=f"{tag}_dw_in")
        da = _mm_nt(dqkv, lw["w_in"], out_dtype=F32, name=f"{tag}_d_a")
        dh, dh_b, small["mix_norm"][i] = _rms_bwd(
            sv["h_in"], norms["mix_norm"][i : i + 1], da, dh_mid, name=f"{tag}_mix_norm"
        )
        sent_mix = send(i, "mix", [dw_in, dw_out.reshape(N_CHIPS, dw_out.shape[1] // N_CHIPS, d)])
        after = (_token(dw_in), _token(dw_out))
        big[i] = {"mix": sent_mix, "ffn": sent_ffn}
    return loss, dh, big, small


def _pack(parts):
    flat = jnp.concatenate([p.reshape(-1).astype(F32) for p in parts])
    pad = (-flat.shape[0]) % (8 * LANES)
    return jnp.pad(flat, (0, pad)).reshape(-1, LANES)


def _unpack(buf, shapes):
    flat = buf.reshape(-1)
    out, at = [], 0
    for shp in shapes:
        size = 1
        for n in shp:
            size *= n
        out.append(flat[at : at + size].reshape(shp))
        at += size
    return out


def kernel(x, mix_norm, ffn_norm, final_norm, na_w_qkv, na_rpb, na_w_o, sc_w_in, sc_conv_w, sc_w_out, gqa_w_qkv, gqa_q_norm, gqa_k_norm, gqa_w_o, ffn_w_up, ffn_conv_w, ffn_conv_b, ffn_w_down, loss_target, m_mix_norm, m_ffn_norm, m_final_norm, m_na_w_qkv, m_na_rpb, m_na_w_o, m_sc_w_in, m_sc_conv_w, m_sc_w_out, m_gqa_w_qkv, m_gqa_q_norm, m_gqa_k_norm, m_gqa_w_o, m_ffn_w_up, m_ffn_conv_w, m_ffn_conv_b, m_ffn_w_down, v_mix_norm, v_ffn_norm, v_final_norm, v_na_w_qkv, v_na_rpb, v_na_w_o, v_sc_w_in, v_sc_conv_w, v_sc_w_out, v_gqa_w_qkv, v_gqa_q_norm, v_gqa_k_norm, v_gqa_w_o, v_ffn_w_up, v_ffn_conv_w, v_ffn_conv_b, v_ffn_w_down):
    depth, d = mix_norm.shape
    chip = 2 * lax.axis_index("x") + lax.axis_index("y")
    w_in_of = {0: na_w_qkv, 1: sc_w_in, 2: gqa_w_qkv}
    w_out_of = {0: na_w_o, 1: sc_w_out, 2: gqa_w_o}

    layers = []
    for i in range(depth):
        kind, j = _mixer_of(i)
        shards = [_cast_slab(w_in_of[kind], j, name=f"cast_w_in_l{i}"), _cast_slab(w_out_of[kind], j, name=f"cast_w_out_l{i}")]
        split = [True, True]
        if kind == 1:
            shards.append(sc_conv_w[j])
            split.append(False)
        mix = _gather_shards(shards, split, name=f"gather_mix_l{i}", collective_id=1 + 2 * i)
        ffn = _gather_shards(
            [_cast_slab(ffn_w_up, i, name=f"cast_w_up_l{i}"), _cast_slab(ffn_w_down, i, name=f"cast_w_down_l{i}"),
             ffn_conv_w[i]],
            [True, True, False],
            name=f"gather_ffn_l{i}",
            collective_id=2 + 2 * i,
        )
        lw = {
            "w_in": mix[0],
            "w_out": mix[1].reshape(1, -1, d),
            "w_up": ffn[0],
            "w_down": ffn[1].reshape(1, -1, d),
            "ffn_conv_w": ffn[2],
            "ffn_conv_b": ffn_conv_b[i : i + 1],
        }
        if kind == 1:
            lw["sc_conv_w"] = mix[2]
        layers.append(lw)

    norms = dict(mix_norm=mix_norm, ffn_norm=ffn_norm, final_norm=final_norm, na_rpb=na_rpb,
                 gqa_q_norm=gqa_q_norm, gqa_k_norm=gqa_k_norm)

    def send(i, part, pieces):
        cid = 1 + 2 * depth + 2 * i + (part == "mix")
        return _scatter_pieces(pieces, name=f"scatter_{part}_l{i}", collective_id=cid)

    loss, grad_x, big, small = _forward_backward(x[0], loss_target[0], norms, layers, send)

    mixer_names = {0: ("na_w_qkv", "na_w_o"), 1: ("sc_w_in", "sc_w_out"), 2: ("gqa_w_qkv", "gqa_w_o")}
    state = {
        "na_w_qkv": (na_w_qkv, m_na_w_qkv, v_na_w_qkv), "na_w_o": (na_w_o, m_na_w_o, v_na_w_o),
        "sc_w_in": (sc_w_in, m_sc_w_in, v_sc_w_in), "sc_w_out": (sc_w_out, m_sc_w_out, v_sc_w_out),
        "gqa_w_qkv": (gqa_w_qkv, m_gqa_w_qkv, v_gqa_w_qkv), "gqa_w_o": (gqa_w_o, m_gqa_w_o, v_gqa_w_o),
        "ffn_w_up": (ffn_w_up, m_ffn_w_up, v_ffn_w_up), "ffn_w_down": (ffn_w_down, m_ffn_w_down, v_ffn_w_down),
    }
    res = {n: None for n in state}
    token = None
    for i in reversed(range(depth)):
        kind, j = _mixer_of(i)
        for part, names, slab in (("ffn", ("ffn_w_up", "ffn_w_down"), i), ("mix", mixer_names[kind], j)):
            own, sib = big[i][part]
            for slot, n in enumerate(names):
                res[n] = _adamw_slab(*state[n], own[slot], sib[slot], slab, res[n], token, name=f"adamw_{n}_l{i}")
                token = res[n][0][slab, :16, :LANES]

    n_na = na_rpb.shape[0]
    rpb_flat = jnp.stack([small["na_rpb"][j] for j in range(n_na)])
    full_parts = [
        loss[:, :1],
        jnp.concatenate(small["mix_norm"], axis=0),
        jnp.concatenate(small["ffn_norm"], axis=0),
        small["final_norm"],
        rpb_flat,
        small["sc_conv_w"],
        small["gqa_q_norm"],
        small["gqa_k_norm"],
        jnp.stack(small["ffn_conv_w"]),
        jnp.concatenate(small["ffn_conv_b"], axis=0),
    ]
    summed = _unpack(_allreduce_small(_pack(full_parts), name="allreduce_small"), [p.shape for p in full_parts])
    loss_all, g_mix, g_ffn, g_final, g_rpb, g_sc_cw, g_gq, g_gk, g_ffn_cw, g_ffn_cb = summed
    g_rpb = g_rpb[:, :, : RPB_ROWS * RPB_COLS].reshape(na_rpb.shape)
    g_sc_cw = lax.dynamic_slice_in_dim(g_sc_cw, chip * sc_conv_w.shape[2], sc_conv_w.shape[2], axis=1)[None]
    g_ffn_cw = lax.dynamic_slice_in_dim(g_ffn_cw, chip * ffn_conv_w.shape[2], ffn_conv_w.shape[2], axis=2)
    small_names = ["mix_norm", "ffn_norm", "final_norm", "na_rpb", "sc_conv_w", "gqa_q_norm", "gqa_k_norm",
                   "ffn_conv_w", "ffn_conv_b"]
    small_g = [g_mix, g_ffn, g_final.reshape(final_norm.shape), g_rpb, g_sc_cw, g_gq, g_gk, g_ffn_cw, g_ffn_cb]
    small_w = [mix_norm, ffn_norm, final_norm, na_rpb, sc_conv_w, gqa_q_norm, gqa_k_norm, ffn_conv_w, ffn_conv_b]
    small_m = [m_mix_norm, m_ffn_norm, m_final_norm, m_na_rpb, m_sc_conv_w, m_gqa_q_norm, m_gqa_k_norm,
               m_ffn_conv_w, m_ffn_conv_b]
    small_v = [v_mix_norm, v_ffn_norm, v_final_norm, v_na_rpb, v_sc_conv_w, v_gqa_q_norm, v_gqa_k_norm,
               v_ffn_conv_w, v_ffn_conv_b]
    shapes = [w.shape for w in small_w]
    packed = _adamw_small(_pack(small_w), _pack(small_g), _pack(small_m), _pack(small_v), name="adamw_small")
    small_d, small_nm, small_nv = (_unpack(p, shapes) for p in packed)
    for n, g, dl, nm, nv in zip(small_names, small_g, small_d, small_nm, small_nv):
        res[n] = (g.reshape(dl.shape), dl, nm, nv)

    order = ["mix_norm", "ffn_norm", "final_norm", "na_w_qkv", "na_rpb", "na_w_o", "sc_w_in", "sc_conv_w",
             "sc_w_out", "gqa_w_qkv", "gqa_q_norm", "gqa_k_norm", "gqa_w_o", "ffn_w_up", "ffn_conv_w",
             "ffn_conv_b", "ffn_w_down"]
    outs = [loss_all.reshape(()), grad_x[None]]
    for part in range(4):
        outs.extend(res[n][part] for n in order)
    return tuple(outs)
```

```python
import functools
import math

import jax
import jax.numpy as jnp
from jax import lax
from jax.experimental import pallas as pl
from jax.experimental.pallas import tpu as pltpu
from jax.experimental.pallas import tpu_sc as plsc

F32 = jnp.float32
BF16 = jnp.bfloat16
MESH = pl.DeviceIdType.MESH

N_CHIPS = 4
N_DEV = 8
N_MIXERS = 3
GRID_W = 64
HEAD_DIM = 128
EPS = 1e-6
NEG_INF = -1e30
NA_WIN_R = 8
NA_WIN_C = 16
GQA_GROUP = 4
ROPE_THETA = 10000.0
ADAM_LR = 0.001
ADAM_B1 = 0.9
ADAM_B2 = 0.999
ADAM_EPS = 1e-08
ADAM_WD = 0.01
ADAM_STEP = 10

LANES = 128
VMEM_LIMIT = 56 * 1024 * 1024
NT_DIMS = (((1,), (1,)), ((), ()))
TN_DIMS = (((0,), (0,)), ((), ()))


def _pick(n, cap, mult=LANES):
    best = None
    for t in range(mult, min(n, cap) + 1, mult):
        if n % t == 0:
            best = t
    return best if best is not None else n


def _params(*sem):
    return pltpu.CompilerParams(dimension_semantics=sem, vmem_limit_bytes=VMEM_LIMIT)


MM_VMEM_BUDGET = 47 * 1024 * 1024
MM_CONTRACT = 2816


def _mm_rows(m, blocks_for, mult=16):
    for cap in (1024, 512, 256, 128):
        tm = _pick(m, cap, mult)
        if sum(r * c * b * n for r, c, b, n in blocks_for(tm)) <= MM_VMEM_BUDGET:
            return tm
    return _pick(m, 128, mult)


def _accumulate(acc, step, steps, part, finish):
    if steps == 1:
        finish(part)
        return

    @pl.when(step == 0)
    def _():
        acc[...] = part

    @pl.when(step != 0)
    def _():
        acc[...] += part

    @pl.when(step == steps - 1)
    def _():
        finish(acc[...])


def _mm_nn(a, b, *, out_dtype, name, residual=None):
    m, k = a.shape
    nc, _, ncol = b.shape
    tn, tk = _pick(ncol, 1536), _pick(k, MM_CONTRACT)
    per, nk = ncol // tn, k // tk
    osz = jnp.dtype(out_dtype).itemsize
    tm = _mm_rows(m, lambda t: [(t, tk, a.dtype.itemsize, 2), (tk, tn, 2, 2), (t, tn, osz, 2),
                                (t, tn, 4, 2 * (residual is not None)), (t, tn, 4, nk > 1)])

    def body(a_ref, b_ref, *rest):
        o_ref = rest[-2] if nk > 1 else rest[-1]

        def finish(r):
            if residual is not None:
                r = r + rest[0][...]
            o_ref[...] = r.astype(o_ref.dtype)

        part = jnp.dot(a_ref[...].astype(BF16), b_ref[...], preferred_element_type=F32)
        _accumulate(rest[-1], pl.program_id(2), nk, part, finish)

    in_specs = [
        pl.BlockSpec((tm, tk), lambda i, j, kk: (i, kk)),
        pl.BlockSpec((None, tk, tn), lambda i, j, kk: (j // per, kk, j % per)),
    ]
    ops = [a, b]
    if residual is not None:
        in_specs.append(pl.BlockSpec((tm, tn), lambda i, j, kk: (i, j)))
        ops.append(residual)
    return pl.pallas_call(
        body,
        name=name,
        grid=(m // tm, nc * per, nk),
        in_specs=in_specs,
        out_specs=pl.BlockSpec((tm, tn), lambda i, j, kk: (i, j)),
        out_shape=jax.ShapeDtypeStruct((m, nc * ncol), out_dtype),
        scratch_shapes=[pltpu.VMEM((tm, tn), F32)] * (nk > 1),
        compiler_params=_params("parallel", "parallel", "arbitrary"),
    )(*ops)


def _token(x):
    return x[(0,) * (x.ndim - 2) + (slice(0, 16), slice(0, LANES))]


def _mm_nt(a, b, *, out_dtype, name, after=()):
    parts = tuple(a) if isinstance(a, (tuple, list)) else (a,)
    m, width = parts[0].shape
    nc, k, ncol = b.shape
    tko, tn = _pick(k, 1024), _pick(math.gcd(ncol, width), MM_CONTRACT)
    per, each, nn = ncol // tn, width // tn, len(parts) * width // tn
    osz = jnp.dtype(out_dtype).itemsize
    tm = _mm_rows(m, lambda t: [(t, tn, parts[0].dtype.itemsize, 2 * len(parts)), (tko, tn, 2, 2), (t, tko, osz, 2),
                                (t, tko, 4, nn > 1)])

    def body(*refs):
        a_refs, b_ref, rest = refs[: len(parts)], refs[len(parts)], refs[len(parts) + 1 :]
        o_ref = rest[len(after)]
        step = pl.program_id(2)

        def finish(r):
            o_ref[...] = r.astype(o_ref.dtype)

        def use(a_ref):
            part = lax.dot_general(a_ref[...].astype(BF16), b_ref[...], NT_DIMS, preferred_element_type=F32)
            _accumulate(rest[-1], step, nn, part, finish)

        _for_part(step // each, a_refs, use)

    return pl.pallas_call(
        body,
        name=name,
        grid=(m // tm, k // tko, nn),
        in_specs=[_part_spec((tm, tn), p, each, lambda i, j, s: (i, s)) for p in range(len(parts))]
        + [pl.BlockSpec((None, tko, tn), lambda i, j, s: (s // per, j, s % per))]
        + [pl.BlockSpec(t.shape, lambda i, j, s: (0, 0)) for t in after],
        out_specs=pl.BlockSpec((tm, tko), lambda i, j, s: (i, j)),
        out_shape=jax.ShapeDtypeStruct((m, k), out_dtype),
        scratch_shapes=[pltpu.VMEM((tm, tko), F32)] * (nn > 1),
        compiler_params=_params("parallel", "parallel", "arbitrary"),
    )(*parts, b, *after)


def _for_part(which, refs, use):
    if len(refs) == 1:
        use(refs[0])
        return
    for p, ref in enumerate(refs):
        pl.when(which == p)(functools.partial(use, ref))


def _part_spec(block, p, each, tile_of):
    def index(*ids):
        r, c = tile_of(*ids)
        return r, jnp.clip(c - p * each, 0, each - 1)

    return pl.BlockSpec(block, index)


def _mm_tn(a, g, nc, *, name):
    parts = tuple(g) if isinstance(g, (tuple, list)) else (g,)
    s, k = a.shape
    width = parts[0].shape[1]
    ncol = len(parts) * width // nc
    ts, tn = _pick(s, MM_CONTRACT, 16), _pick(math.gcd(ncol, width), 1536)
    per, each, ns = ncol // tn, width // tn, s // ts
    tko = _mm_rows(k, lambda t: [(ts, t, a.dtype.itemsize, 2), (ts, tn, parts[0].dtype.itemsize, 2 * len(parts)),
                                 (t, tn, 2, 2), (t, tn, 4, ns > 1)], mult=LANES)

    def body(a_ref, *refs):
        g_refs, rest = refs[: len(parts)], refs[len(parts) :]
        o_ref = rest[0]

        def finish(r):
            o_ref[...] = r.astype(o_ref.dtype)

        def use(g_ref):
            part = lax.dot_general(a_ref[...].astype(BF16), g_ref[...].astype(BF16), TN_DIMS, preferred_element_type=F32)
            _accumulate(rest[-1], pl.program_id(2), ns, part, finish)

        _for_part(pl.program_id(1) // each, g_refs, use)

    return pl.pallas_call(
        body,
        name=name,
        grid=(k // tko, nc * per, ns),
        in_specs=[pl.BlockSpec((ts, tko), lambda i, j, t: (t, i))]
        + [_part_spec((ts, tn), p, each, lambda i, j, t: (t, j)) for p in range(len(parts))],
        out_specs=pl.BlockSpec((None, tko, tn), lambda i, j, t: (j // per, i, j % per)),
        out_shape=jax.ShapeDtypeStruct((nc, k, ncol), BF16),
        scratch_shapes=[pltpu.VMEM((tko, tn), F32)] * (ns > 1),
        compiler_params=_params("parallel", "parallel", "arbitrary"),
    )(a, *parts)


ROW_TILE = 256


def _cast_slab(w, slab, *, name):
    _, rows, cols = w.shape
    tr = _pick(rows, ROW_TILE, 16)

    def body(w_ref, o_ref):
        o_ref[...] = w_ref[...].astype(o_ref.dtype)

    return pl.pallas_call(
        body,
        name=name,
        grid=(rows // tr,),
        in_specs=[pl.BlockSpec((None, tr, cols), lambda i: (slab, i, 0))],
        out_specs=pl.BlockSpec((tr, cols), lambda i: (i, 0)),
        out_shape=jax.ShapeDtypeStruct((rows, cols), BF16),
        compiler_params=_params("parallel"),
    )(w)


def _rms_fwd(h, g, *, name):
    s, d = h.shape
    tr = _pick(s, ROW_TILE, 16)

    def body(h_ref, g_ref, o_ref):
        x = h_ref[...]
        r = lax.rsqrt(jnp.mean(x * x, axis=-1, keepdims=True) + EPS)
        o_ref[...] = (x * r * g_ref[...]).astype(o_ref.dtype)

    return pl.pallas_call(
        body,
        name=name,
        grid=(s // tr,),
        in_specs=[pl.BlockSpec((tr, d), lambda i: (i, 0)), pl.BlockSpec((1, d), lambda i: (0, 0))],
        out_specs=pl.BlockSpec((tr, d), lambda i: (i, 0)),
        out_shape=jax.ShapeDtypeStruct((s, d), BF16),
        compiler_params=_params("parallel"),
    )(h, g)


def _rms_bwd(h, g, dy, dres, *, name):
    s, d = h.shape
    tr = _pick(s, ROW_TILE, 16)

    def body(h_ref, g_ref, dy_ref, dres_ref, dh_ref, dhb_ref, dg_ref):
        x = h_ref[...]
        r = lax.rsqrt(jnp.mean(x * x, axis=-1, keepdims=True) + EPS)
        xhat = x * r
        dyv = dy_ref[...].astype(F32)
        dyg = dyv * g_ref[...]
        dx = r * (dyg - xhat * jnp.mean(dyg * xhat, axis=-1, keepdims=True))
        dh = dres_ref[...] + dx
        dh_ref[...] = dh
        dhb_ref[...] = dh.astype(dhb_ref.dtype)
        part = jnp.sum(dyv * xhat, axis=0, keepdims=True)

        @pl.when(pl.program_id(0) == 0)
        def _():
            dg_ref[...] = part

        @pl.when(pl.program_id(0) != 0)
        def _():
            dg_ref[...] += part

    row = pl.BlockSpec((tr, d), lambda i: (i, 0))
    vec = pl.BlockSpec((1, d), lambda i: (0, 0))
    return pl.pallas_call(
        body,
        name=name,
        grid=(s // tr,),
        in_specs=[row, vec, row, row],
        out_specs=[row, row, vec],
        out_shape=[jax.ShapeDtypeStruct((s, d), F32), jax.ShapeDtypeStruct((s, d), BF16),
                   jax.ShapeDtypeStruct((1, d), F32)],
        compiler_params=_params("arbitrary"),
    )(h, g, dy, dres)


def _loss_head(h, g, target, *, name):
    s, d = h.shape
    tr = _pick(s, ROW_TILE, 16)

    def body(h_ref, g_ref, t_ref, dh_ref, dhb_ref, dg_ref, loss_ref):
        x = h_ref[...]
        r = lax.rsqrt(jnp.mean(x * x, axis=-1, keepdims=True) + EPS)
        xhat = x * r
        gv = g_ref[...]
        err = xhat * gv - t_ref[...]
        dyv = err * (1.0 / d)
        dyg = dyv * gv
        dh = r * (dyg - xhat * jnp.mean(dyg * xhat, axis=-1, keepdims=True))
        dh_ref[...] = dh
        dhb_ref[...] = dh.astype(dhb_ref.dtype)
        part = jnp.sum(dyv * xhat, axis=0, keepdims=True)
        lpart =jnp.sum(jnp.sum(err * err, axis=-1, keepdims=True), axis=0, keepdims=True) * (0.5 / d)

        @pl.when(pl.program_id(0) == 0)
        def _():
            dg_ref[...] = part
            loss_ref[...] = jnp.broadcast_to(lpart, loss_ref.shape)

        @pl.when(pl.program_id(0) != 0)
        def _():
            dg_ref[...] += part
            loss_ref[...] += jnp.broadcast_to(lpart, loss_ref.shape)

    row = pl.BlockSpec((tr, d), lambda i: (i, 0))
    vec = pl.BlockSpec((1, d), lambda i: (0, 0))
    return pl.pallas_call(
        body,
        name=name,
        grid=(s // tr,),
        in_specs=[row, vec, row],
        out_specs=[row, row, vec, pl.BlockSpec((1, LANES), lambda i: (0, 0))],
        out_shape=[
            jax.ShapeDtypeStruct((s, d), F32),
            jax.ShapeDtypeStruct((s, d), BF16),
            jax.ShapeDtypeStruct((1, d), F32),
            jax.ShapeDtypeStruct((1, LANES), F32),
        ],
        compiler_params=_params("arbitrary"),
    )(h, g, target)


def _shift_prev(x):
    row = lax.broadcasted_iota(jnp.int32, x.shape, 0)
    return jnp.where(row == 0, 0.0, pltpu.roll(x, 1, 0))


def _shift_next(x):
    n = x.shape[0]
    row = lax.broadcasted_iota(jnp.int32, x.shape, 0)
    return jnp.where(row == n - 1, 0.0, pltpu.roll(x, n - 1, 0))


def _conv3(x, w):
    xm, xp = _shift_prev(x), _shift_next(x)
    return xm * w[0:1] + x * w[1:2] + xp * w[2:3], xm, xp


def _conv3_t(d, w):
    return _shift_next(d) * w[0:1] + d * w[1:2] + _shift_prev(d) * w[2:3]


def _colsum(x):
    return jnp.sum(x, axis=0, keepdims=True)


def _ffn_mid_fwd(up, cw, cb, *, name):
    s, f2 = up.shape
    f = f2 // 2
    ncol = cw.shape[2]
    tc = _pick(ncol, 256)
    nt, per = f // tc, ncol // tc

    def body(ug_ref, uu_ref, wg_ref, wu_ref, bg_ref, bu_ref, o_ref, cg_ref, cu_ref):
        cg = _conv3(ug_ref[...].astype(F32), wg_ref[...])[0] + bg_ref[...]
        cu = _conv3(uu_ref[...].astype(F32), wu_ref[...])[0] + bu_ref[...]
        o_ref[...] = (cg * (1.0 / (1.0 + jnp.exp(-cg))) * cu).astype(o_ref.dtype)
        cg_ref[...] = cg.astype(cg_ref.dtype)
        cu_ref[...] = cu.astype(cu_ref.dtype)

    out = pl.BlockSpec((s, tc), lambda j: (0, j))
    return pl.pallas_call(
        body,
        name=name,
        grid=(nt,),
        in_specs=[
            pl.BlockSpec((s, tc), lambda j: (0, j)),
            pl.BlockSpec((s, tc), lambda j: (0, nt + j)),
            pl.BlockSpec((None, 3, tc), lambda j: (j // per, 0, j % per)),
            pl.BlockSpec((None, 3, tc), lambda j: ((nt + j) // per, 0, (nt + j) % per)),
            pl.BlockSpec((1, tc), lambda j: (0, j)),
            pl.BlockSpec((1, tc), lambda j: (0, nt + j)),
        ],
        out_specs=[out, out, out],
        out_shape=[jax.ShapeDtypeStruct((s, f), BF16)] * 3,
        compiler_params=_params("parallel"),
    )(up, up, cw, cw, cb, cb)


def _ffn_mid_bwd(up, cg, cu, cw, dact, *, name):
    s, f2 = up.shape
    f = f2 // 2
    ncol = cw.shape[2]
    tc = _pick(ncol, 256)
    nt, per = f // tc, ncol // tc

    def side(dc, u_ref, w_ref, du_ref, dw_ref, db_ref):
        w, u = w_ref[...], u_ref[...].astype(F32)
        nxt, prv = _shift_next(dc), _shift_prev(dc)
        du_ref[...] = (nxt * w[0:1] + dc * w[1:2] + prv * w[2:3]).astype(du_ref.dtype)
        dw_ref[0:1, :] = _colsum(nxt * u)
        dw_ref[1:2, :] = _colsum(dc * u)
        dw_ref[2:3, :] = _colsum(prv * u)
        db_ref[...] = _colsum(dc)

    def body(ug_ref, uu_ref, cg_ref, cu_ref, wg_ref, wu_ref, da_ref, dug_ref, duu_ref, dwg_ref, dwu_ref, dbg_ref, dbu_ref):
        cgv, cuv, da = cg_ref[...].astype(F32), cu_ref[...].astype(F32), da_ref[...].astype(F32)
        sig = 1.0 / (1.0 + jnp.exp(-cgv))
        side(da * cuv * (sig * (1.0 + cgv * (1.0 - sig))), ug_ref, wg_ref, dug_ref, dwg_ref, dbg_ref)
        side(da * (cgv * sig), uu_ref, wu_ref, duu_ref, dwu_ref, dbu_ref)

    col = pl.BlockSpec((s, tc), lambda j: (0, j))
    w3 = pl.BlockSpec((3, tc), lambda j: (0, j))
    b1 = pl.BlockSpec((1, tc), lambda j: (0, j))
    return pl.pallas_call(
        body,
        name=name,
        grid=(nt,),
        in_specs=[
            col,
            pl.BlockSpec((s, tc), lambda j: (0, nt + j)),
            col,
            col,
            pl.BlockSpec((None, 3, tc), lambda j: (j // per, 0, j % per)),
            pl.BlockSpec((None, 3, tc), lambda j: ((nt + j) // per, 0, (nt + j) % per)),
            col,
        ],
        out_specs=[col, col, w3, w3, b1, b1],
        out_shape=[
            jax.ShapeDtypeStruct((s, f), BF16),
            jax.ShapeDtypeStruct((s, f), BF16),
            jax.ShapeDtypeStruct((3, f), F32),
            jax.ShapeDtypeStruct((3, f), F32),
            jax.ShapeDtypeStruct((1, f), F32),
            jax.ShapeDtypeStruct((1, f), F32),
        ],
        compiler_params=_params("parallel"),
    )(up, up, cg, cu, cw, cw, dact)


def _sc_mid_fwd(z, cw, *, name):
    s, d3 = z.shape
    d = d3 // 3
    ncol = cw.shape[2]
    tc = _pick(ncol, 256)
    nt, per = d // tc, ncol // tc

    def body(gb_ref, gc_ref, hh_ref, w_ref, o_ref):
        p = gc_ref[...].astype(F32) * hh_ref[...].astype(F32)
        o_ref[...] = (gb_ref[...].astype(F32) * _conv3(p, w_ref[...])[0]).astype(o_ref.dtype)

    return pl.pallas_call(
        body,
        name=name,
        grid=(nt,),
        in_specs=[
            pl.BlockSpec((s, tc), lambda j: (0, j)),
            pl.BlockSpec((s, tc), lambda j: (0, nt + j)),
            pl.BlockSpec((s, tc), lambda j: (0, 2 * nt + j)),
            pl.BlockSpec((None, 3, tc), lambda j: (j // per, 0, j % per)),
        ],
        out_specs=pl.BlockSpec((s, tc), lambda j: (0, j)),
        out_shape=jax.ShapeDtypeStruct((s, d), BF16),
        compiler_params=_params("parallel"),
    )(z, z, z, cw)


def _sc_mid_bwd(z, cw, dmid, *, name):
    s, d3 = z.shape
    d = d3 // 3
    ncol = cw.shape[2]
    tc = _pick(ncol, 256)
    nt, per = d // tc, ncol // tc

    def body(gb_ref, gc_ref, hh_ref, w_ref, dm_ref, dgb_ref, dgc_ref, dhh_ref, dw_ref):
        gb, gc, hh = gb_ref[...].astype(F32), gc_ref[...].astype(F32), hh_ref[...].astype(F32)
        w = w_ref[...]
        p = gc * hh
        cv, pm, pp = _conv3(p, w)
        dm = dm_ref[...].astype(F32)
        dgb_ref[...] = (dm * cv).astype(dgb_ref.dtype)
        dcv = dm * gb
        dp = _conv3_t(dcv, w)
        dgc_ref[...] = (dp * hh).astype(dgc_ref.dtype)
        dhh_ref[...] = (dp * gc).astype(dhh_ref.dtype)
        dw_ref[0:1, :] = _colsum(dcv * pm)
        dw_ref[1:2, :] = _colsum(dcv * p)
        dw_ref[2:3, :] = _colsum(dcv * pp)

    col = pl.BlockSpec((s, tc), lambda j: (0, j))
    return pl.pallas_call(
        body,
        name=name,
        grid=(nt,),
        in_specs=[
            col,
            pl.BlockSpec((s, tc), lambda j: (0, nt + j)),
            pl.BlockSpec((s, tc), lambda j: (0, 2 * nt + j)),
            pl.BlockSpec((None, 3, tc), lambda j: (j // per, 0, j % per)),
            col,
        ],
        out_specs=[col, col, col, pl.BlockSpec((3, tc), lambda j: (0, j))],
        out_shape=[jax.ShapeDtypeStruct((s, d), BF16)] * 3 + [jax.ShapeDtypeStruct((3, d), F32)],
        compiler_params=_params("parallel"),
    )(z, z, z, cw, dmid)


NA_KEYS = NA_WIN_R * GRID_W


def _na_row_start(r, rows):
    return jnp.clip(r - NA_WIN_R // 2, 0, rows - NA_WIN_R)


def _na_base(rpb):
    h = rpb.shape[0]
    pos, neg = rpb[:, :, NA_WIN_C - 1:], rpb[:, :, : NA_WIN_C - 1]
    zeros = jnp.zeros((h, NA_WIN_R, GRID_W - 2 * NA_WIN_C + 1), F32)
    out = []
    for first in range(NA_WIN_R):
        p = pos[:, first : first + NA_WIN_R]
        n = jnp.roll(neg[:, first : first + NA_WIN_R], -1, axis=1)
        out.append(jnp.concatenate([p, zeros, n], axis=-1).reshape(h, 1, NA_KEYS))
    return jnp.stack(out, axis=1)


def _skew_right(x):
    return pltpu.roll(x, 0, 1, stride=1, stride_axis=0)


def _skew_left(x):
    n = x.shape[1]
    row = lax.broadcasted_iota(jnp.int32, x.shape, 0)
    for b in range(GRID_W.bit_length() - 1):
        x = jnp.where(((row >> b) & 1) == 1, pltpu.roll(x, n - (1 << b), 1), x)
    return x


def _na_bias(base, *, name):
    h = base.shape[0]

    def body(b_ref, o_ref):
        q = lax.broadcasted_iota(jnp.int32, (GRID_W, NA_KEYS), 0)
        kc = lax.broadcasted_iota(jnp.int32, (GRID_W, NA_KEYS), 1) % GRID_W
        start = jnp.clip(q - NA_WIN_C // 2, 0, GRID_W - NA_WIN_C)
        inside = (kc >= start) & (kc < start + NA_WIN_C)
        for slot in range(NA_WIN_R):
            x = _skew_right(jnp.broadcast_to(b_ref[slot], (GRID_W, NA_KEYS)))
            o_ref[slot] = jnp.where(inside, x, NEG_INF)

    return pl.pallas_call(
        body,
        name=name,
        grid=(h,),
        in_specs=[pl.BlockSpec((None, NA_WIN_R, 1, NA_KEYS), lambda i: (i, 0, 0, 0))],
        out_specs=pl.BlockSpec((None, NA_WIN_R, GRID_W, NA_KEYS), lambda i: (i, 0, 0, 0)),
        out_shape=jax.ShapeDtypeStruct((h, NA_WIN_R, GRID_W, NA_KEYS), F32),
        compiler_params=_params("parallel"),
    )(base)


NA_STEP_ROWS = 4
NA_BLOCK_ROWS = NA_STEP_ROWS + NA_WIN_R
NA_STEP_Q = NA_STEP_ROWS * GRID_W
NA_BLOCK_K = NA_BLOCK_ROWS * GRID_W
NA_PATTERNS = 3


def _na_plan(rows):
    steps = rows // NA_STEP_ROWS

    def start(r):
        return min(max(r - NA_WIN_R // 2, 0), rows - NA_WIN_R)

    plan = []
    for t in range(steps):
        first = [start(NA_STEP_ROWS * t + i) for i in range(NA_STEP_ROWS)]
        block = min(first[0], rows - NA_BLOCK_ROWS)
        offsets = tuple(f - block for f in first)
        slots = tuple(f - (NA_STEP_ROWS * t + i) + NA_WIN_R - 1 for i, f in enumerate(first))
        plan.append((offsets, slots))
    middle = plan[1] if steps > 2 else (tuple(range(NA_STEP_ROWS)), (NA_WIN_R // 2 - 1,) * NA_STEP_ROWS)
    assert steps >= 2 and all(p == middle for p in plan[1:-1])
    return steps, (plan[0], middle, plan[-1])


def _na_block_start(t, rows):
    return jnp.minimum(_na_row_start(NA_STEP_ROWS * t, rows), rows - NA_BLOCK_ROWS)


def _na_pattern(t, steps):
    return jnp.where(t == 0, 0, jnp.where(t == steps - 1, 2, 1))


def _na_block_bias(bias, rows, *, name):
    h = bias.shape[0]
    _, patterns = _na_plan(rows)

    def body(b_ref, o_ref):
        outside = jnp.full((GRID_W, NA_BLOCK_K - NA_KEYS), NEG_INF, F32)
        for p, (offsets, slots) in enumerate(patterns):

            @pl.when(pl.program_id(1) == p)
            def _(offsets=offsets, slots=slots):
                for i in range(NA_STEP_ROWS):
                    wide = jnp.concatenate([b_ref[slots[i]], outside], axis=1)
                    shift = offsets[i] * GRID_W
                    o_ref[i * GRID_W : (i + 1) * GRID_W, :] = pltpu.roll(wide, shift, 1) if shift else wide

    return pl.pallas_call(
        body,
        name=name,
        grid=(h, NA_PATTERNS),
        in_specs=[pl.BlockSpec((None, NA_WIN_R, GRID_W, NA_KEYS), lambda i, p: (i, 0, 0, 0))],
        out_specs=pl.BlockSpec((None, None, NA_STEP_Q, NA_BLOCK_K), lambda i, p: (i, p, 0, 0)),
        out_shape=jax.ShapeDtypeStruct((h, NA_PATTERNS, NA_STEP_Q, NA_BLOCK_K), F32),
        compiler_params=_params("parallel", "arbitrary"),
    )(bias)


def _na_unblock(dblock, rows, *, name):
    h = dblock.shape[0]
    steps, patterns = _na_plan(rows)
    used = [p for p in range(NA_PATTERNS) if p != 1 or steps > 2]

    def body(d_ref, o_ref):
        total = {}
        for p in used:
            offsets, slots = patterns[p]
            for i in range(NA_STEP_ROWS):
                wide = d_ref[p, i * GRID_W : (i + 1) * GRID_W, :]
                back = (NA_BLOCK_K - offsets[i] * GRID_W) % NA_BLOCK_K
                piece = (pltpu.roll(wide, back, 1) if back else wide)[:, :NA_KEYS]
                total[slots[i]] = piece if slots[i] not in total else total[slots[i]] + piece
        for slot in range(NA_WIN_R):
            o_ref[slot] = total.get(slot, jnp.zeros((GRID_W, NA_KEYS), F32))

    return pl.pallas_call(
        body,
        name=name,
        grid=(h,),
        in_specs=[pl.BlockSpec((None, NA_PATTERNS, NA_STEP_Q, NA_BLOCK_K), lambda i: (i, 0, 0, 0))],
        out_specs=pl.BlockSpec((None, NA_WIN_R, GRID_W, NA_KEYS), lambda i: (i, 0, 0, 0)),
        out_shape=jax.ShapeDtypeStruct((h, NA_WIN_R, GRID_W, NA_KEYS), F32),
        compiler_params=_params("parallel"),
    )(dblock)


def _na_specs(s, heads, rows, steps):
    q = pl.BlockSpec((NA_STEP_Q, HEAD_DIM), lambda h, t: (t, h))
    k = pl.BlockSpec((s, HEAD_DIM), lambda h, t: (0, heads + h))
    v = pl.BlockSpec((s, HEAD_DIM), lambda h, t: (0, 2 * heads + h))
    bias = pl.BlockSpec((None, None, NA_STEP_Q, NA_BLOCK_K), lambda h, t: (h, _na_pattern(t, steps), 0, 0))
    return q, k, v, bias


def _na_probs(q, k, bias):
    sc = lax.dot_general(q, k, NT_DIMS, preferred_element_type=F32) * (HEAD_DIM ** -0.5) + bias
    p = jnp.exp(sc - jnp.max(sc, axis=-1, keepdims=True))
    return p, jnp.sum(p, axis=-1, keepdims=True)


def _na_fwd(qkv, block_bias, *, name):
    s = qkv.shape[0]
    heads = qkv.shape[1] // (3 * HEAD_DIM)
    rows = s // GRID_W
    steps, _ = _na_plan(rows)

    def body(q_ref, k_ref, v_ref, b_ref, o_ref):
        start = _na_block_start(pl.program_id(1), rows)
        block = pl.ds(pl.multiple_of(start * GRID_W, GRID_W), NA_BLOCK_K)
        p, l = _na_probs(q_ref[...], k_ref[block, :], b_ref[...])
        o = jnp.dot(p.astype(BF16), v_ref[block, :], preferred_element_type=F32)
        o_ref[...] = (o / l).astype(o_ref.dtype)

    q, k, v, b = _na_specs(s, heads, rows, steps)
    return pl.pallas_call(
        body,
        name=name,
        grid=(heads, steps),
        in_specs=[q, k, v, b],
        out_specs=q,
        out_shape=jax.ShapeDtypeStruct((s, heads * HEAD_DIM), BF16),
        compiler_params=_params("parallel", "arbitrary"),
    )(qkv, qkv, qkv, block_bias)


def _na_bwd(qkv, block_bias, dout, *, name):
    s = qkv.shape[0]
    heads = qkv.shape[1] // (3 * HEAD_DIM)
    rows = s // GRID_W
    steps, _ = _na_plan(rows)
    scale = HEAD_DIM ** -0.5

    def body(q_ref, k_ref, v_ref, b_ref, do_ref, dq_ref, dk_ref, dv_ref, db_ref, dk_acc, dv_acc):
        step = pl.program_id(1)

        @pl.when(step == 0)
        def _():
            dk_acc[...] = jnp.zeros_like(dk_acc)
            dv_acc[...] = jnp.zeros_like(dv_acc)

        block = pl.ds(pl.multiple_of(_na_block_start(step, rows) * GRID_W, GRID_W), NA_BLOCK_K)
        q, k, v, do = q_ref[...], k_ref[block, :], v_ref[block, :], do_ref[...]
        p, l = _na_probs(q, k, b_ref[...])
        pn = p / l
        dp = lax.dot_general(do, v, NT_DIMS, preferred_element_type=F32)
        ds = pn * (dp - jnp.sum(pn * dp, axis=-1, keepdims=True))
        dsb = ds.astype(BF16)
        dq_ref[...] = (jnp.dot(dsb, k, preferred_element_type=F32) * scale).astype(dq_ref.dtype)
        dk_acc[block, :] += lax.dot_general(dsb, q, TN_DIMS, preferred_element_type=F32) * scale
        dv_acc[block, :] += lax.dot_general(pn.astype(BF16), do, TN_DIMS, preferred_element_type=F32)
        opens = (step <= 1) | (step == steps - 1)

        @pl.when(opens)
        def _():
            db_ref[...] = ds

        @pl.when(jnp.logical_not(opens))
        def _():
            db_ref[...] += ds

        @pl.when(step == steps - 1)
        def _():
            dk_ref[...] = dk_acc[...].astype(dk_ref.dtype)
            dv_ref[...] = dv_acc[...].astype(dv_ref.dtype)

    q, k, v, b = _na_specs(s, heads, rows, steps)
    kv_out = pl.BlockSpec((s, HEAD_DIM), lambda h, t: (0, h))
    shape = jax.ShapeDtypeStruct((s, heads * HEAD_DIM), BF16)
    return pl.pallas_call(
        body,
        name=name,
        grid=(heads, steps),
        in_specs=[q, k, v, b, q],
        out_specs=[q, kv_out, kv_out, b],
        out_shape=[shape, shape, shape, jax.ShapeDtypeStruct((heads, NA_PATTERNS, NA_STEP_Q, NA_BLOCK_K), F32)],
        scratch_shapes=[pltpu.VMEM((s, HEAD_DIM), F32), pltpu.VMEM((s, HEAD_DIM), F32)],
        compiler_params=_params("parallel", "arbitrary"),
    )(qkv, qkv, qkv, block_bias, dout)


RPB_ROWS = 2 * NA_WIN_R - 1
RPB_COLS = 2 * NA_WIN_C - 1
RPB_PAD = 512


def _rpb_fold_matrix():
    idx = jnp.arange(NA_WIN_R * NA_KEYS, dtype=jnp.int32)
    first, i, kc = idx // NA_KEYS, (idx // GRID_W) % NA_WIN_R, idx % GRID_W
    pos, neg = kc < NA_WIN_C, kc >= GRID_W - NA_WIN_C + 1
    dr = jnp.where(pos, first + i, first + (i + 1) % NA_WIN_R)
    dc = jnp.where(pos, kc + NA_WIN_C - 1, kc - (GRID_W - NA_WIN_C + 1))
    target = jnp.where(pos | neg, dr * RPB_COLS + dc, -1)
    return (target[:, None] == jnp.arange(RPB_PAD, dtype=jnp.int32)[None, :]).astype(F32)


def _rpb_fold(dbias, *, name):
    h = dbias.shape[0]

    def skew_body(g_ref, o_ref):
        for slot in range(NA_WIN_R):
            o_ref[slot] = _colsum(_skew_left(g_ref[slot]))

    skewed = pl.pallas_call(
        skew_body,
        name=name + "_skew",
        grid=(h,),
        in_specs=[pl.BlockSpec((None, NA_WIN_R, GRID_W, NA_KEYS), lambda i: (i, 0, 0, 0))],
        out_specs=pl.BlockSpec((None, NA_WIN_R, 1, NA_KEYS), lambda i: (i, 0, 0, 0)),
        out_shape=jax.ShapeDtypeStruct((h, NA_WIN_R, 1, NA_KEYS), F32),
        compiler_params=_params("parallel"),
    )(dbias)

    def fold_body(g_ref, m_ref, o_ref):
        o_ref[...] = jnp.dot(g_ref[...], m_ref[...], preferred_element_type=F32, precision=lax.Precision.HIGHEST)

    return pl.pallas_call(
        fold_body,
        name=name,
        out_shape=jax.ShapeDtypeStruct((h, RPB_PAD), F32),
        compiler_params=pltpu.CompilerParams(vmem_limit_bytes=VMEM_LIMIT),
    )(skewed.reshape(h, NA_WIN_R * NA_KEYS), _rpb_fold_matrix())


GQA_Q_TILE = 256
GQA_Q_TILE_BWD = 512


def _rope_tables(s):
    t = jnp.arange(s)
    row = (t // GRID_W).astype(F32)[:, None]
    col = (t % GRID_W).astype(F32)[:, None]
    half = HEAD_DIM // 2
    inv = ROPE_THETA ** (-jnp.arange(0, half, 2, dtype=F32) / half)
    ang = jnp.concatenate([row * inv, row * inv, col * inv, col * inv], axis=-1)
    return jnp.cos(ang), jnp.sin(ang)


def _rot_half(y):
    quarter = HEAD_DIM // 4
    lane = lax.broadcasted_iota(jnp.int32, y.shape, 1)
    low = (lane % (2 * quarter)) < quarter
    return jnp.where(low, -pltpu.roll(y, HEAD_DIM - quarter, 1), pltpu.roll(y, quarter, 1))


def _gqa_prep_fwd(qkv, gq, gk, cos, sin, hq, hkv, *, name):
    s = qkv.shape[0]

    def body(x_ref, gq_ref, gk_ref, cos_ref, sin_ref, o_ref):
        isq = pl.program_id(0) < hq
        x = x_ref[...].astype(F32)
        g = jnp.where(isq, gq_ref[...], gk_ref[...])
        y = x * lax.rsqrt(jnp.mean(x * x, axis=-1, keepdims=True) + EPS) * g
        z = y * cos_ref[...] + _rot_half(y) * sin_ref[...]
        o_ref[...] = (z * jnp.where(isq, HEAD_DIM ** -0.5, 1.0)).astype(o_ref.dtype)

    head = pl.BlockSpec((s, HEAD_DIM), lambda h: (0, h))
    vec = pl.BlockSpec((1, HEAD_DIM), lambda h: (0, 0))
    tab = pl.BlockSpec((s, HEAD_DIM), lambda h: (0, 0))
    return pl.pallas_call(
        body,
        name=name,
        grid=(hq + hkv,),
        in_specs=[head, vec, vec, tab, tab],
        out_specs=head,
        out_shape=jax.ShapeDtypeStruct((s, (hq + hkv) * HEAD_DIM), BF16),
        compiler_params=_params("parallel"),
    )(qkv, gq, gk, cos, sin)


def _gqa_prep_bwd(qkv, gq, gk, cos, sin, dqn, dkn, hq, hkv, *, name):
    s = qkv.shape[0]

    def body(x_ref, gq_ref, gk_ref, cos_ref, sin_ref, dq_ref, dk_ref, dx_ref, dgq_ref, dgk_ref):
        hh = pl.program_id(0)
        isq = hh < hq
        x = x_ref[...].astype(F32)
        g = jnp.where(isq, gq_ref[...], gk_ref[...])
        r = lax.rsqrt(jnp.mean(x * x, axis=-1, keepdims=True) + EPS)
        xhat = x * r
        dz = jnp.where(isq, dq_ref[...].astype(F32) * (HEAD_DIM ** -0.5), dk_ref[...].astype(F32))
        dy = dz * cos_ref[...] - _rot_half(dz * sin_ref[...])
        dyg = dy * g
        dx_ref[...] = (r * (dyg - xhat * jnp.mean(dyg * xhat, axis=-1, keepdims=True))).astype(dx_ref.dtype)
        part = _colsum(dy * xhat)

        @pl.when(hh == 0)
        def _():
            dgq_ref[...] = jnp.zeros_like(dgq_ref)
            dgk_ref[...] = jnp.zeros_like(dgk_ref)

        @pl.when(isq)
        def _():
            dgq_ref[...] += part

        @pl.when(jnp.logical_not(isq))
        def _():
            dgk_ref[...] += part

    head = pl.BlockSpec((s, HEAD_DIM), lambda h: (0, h))
    vec = pl.BlockSpec((1, HEAD_DIM), lambda h: (0, 0))
    tab = pl.BlockSpec((s, HEAD_DIM), lambda h: (0, 0))
    return pl.pallas_call(
        body,
        name=name,
        grid=(hq + hkv,),
        in_specs=[
            head,
            vec,
            vec,
            tab,
            tab,
            pl.BlockSpec((s, HEAD_DIM), lambda h: (0, jnp.minimum(h, hq - 1))),
            pl.BlockSpec((s, HEAD_DIM), lambda h: (0, jnp.maximum(h - hq, 0))),
        ],
        out_specs=[head, vec, vec],
        out_shape=[
            jax.ShapeDtypeStruct((s, (hq + hkv) * HEAD_DIM), BF16),
            jax.ShapeDtypeStruct((1, HEAD_DIM), F32),
            jax.ShapeDtypeStruct((1, HEAD_DIM), F32),
        ],
        compiler_params=_params("arbitrary"),
    )(qkv, gq, gk, cos, sin, dqn, dkn)


def _gqa_fwd(qkn, qkv, hq, hkv, *, name):
    s = qkv.shape[0]
    tq = _pick(s, GQA_Q_TILE, 16)

    def body(q_ref, k_ref, v_ref, o_ref):
        sc = lax.dot_general(q_ref[...], k_ref[...], NT_DIMS, preferred_element_type=F32)
        p = jnp.exp(sc - jnp.max(sc, axis=-1, keepdims=True))
        l = jnp.sum(p, axis=-1, keepdims=True)
        o_ref[...] = (jnp.dot(p.astype(BF16), v_ref[...], preferred_element_type=F32) / l).astype(o_ref.dtype)

    q = pl.BlockSpec((tq, HEAD_DIM), lambda h, i: (i, h))
    return pl.pallas_call(
        body,
        name=name,
        grid=(hq, s // tq),
        in_specs=[
            q,
            pl.BlockSpec((s, HEAD_DIM), lambda h, i: (0, hq + h // GQA_GROUP)),
            pl.BlockSpec((s, HEAD_DIM), lambda h, i: (0, hq + hkv + h // GQA_GROUP)),
        ],
        out_specs=q,
        out_shape=jax.ShapeDtypeStruct((s, hq * HEAD_DIM), BF16),
        compiler_params=_params("parallel", "parallel"),
    )(qkn, qkn, qkv)


def _gqa_bwd(qkn, qkv, dout, hq, hkv, *, name):
    s = qkv.shape[0]
    tq = _pick(s, GQA_Q_TILE_BWD, 16)
    nq = s // tq

    def body(q_ref, k_ref, v_ref, do_ref, dq_ref, dk_ref, dv_ref, dk_acc, dv_acc):
        g, i = pl.program_id(1), pl.program_id(2)

        @pl.when((g == 0) & (i == 0))
        def _():
            dk_acc[...] = jnp.zeros_like(dk_acc)
            dv_acc[...] = jnp.zeros_like(dv_acc)

        q, k, v, do = q_ref[...], k_ref[...], v_ref[...], do_ref[...]
        sc = lax.dot_general(q, k, NT_DIMS, preferred_element_type=F32)
        p = jnp.exp(sc - jnp.max(sc, axis=-1, keepdims=True))
        pn = p / jnp.sum(p, axis=-1, keepdims=True)
        dp = lax.dot_general(do, v, NT_DIMS, preferred_element_type=F32)
        dsb = (pn * (dp - jnp.sum(pn * dp, axis=-1, keepdims=True))).astype(BF16)
        dq_ref[...] = jnp.dot(dsb, k, preferred_element_type=F32).astype(dq_ref.dtype)
        dk_acc[...] += lax.dot_general(dsb, q, TN_DIMS, preferred_element_type=F32)
        dv_acc[...] += lax.dot_general(pn.astype(BF16), do, TN_DIMS, preferred_element_type=F32)

        @pl.when((g == GQA_GROUP - 1) & (i == nq - 1))
        def _():
            dk_ref[...] = dk_acc[...].astype(dk_ref.dtype)
            dv_ref[...] = dv_acc[...].astype(dv_ref.dtype)

    q = pl.BlockSpec((tq, HEAD_DIM), lambda kv, g, i: (i, kv * GQA_GROUP + g))
    kv_out = pl.BlockSpec((s, HEAD_DIM), lambda kv, g, i: (0, kv))
    return pl.pallas_call(
        body,
        name=name,
        grid=(hkv, GQA_GROUP, nq),
        in_specs=[
            q,
            pl.BlockSpec((s, HEAD_DIM), lambda kv, g, i: (0, hq + kv)),
            pl.BlockSpec((s, HEAD_DIM), lambda kv, g, i: (0, hq + hkv + kv)),
            q,
        ],
        out_specs=[q, kv_out, kv_out],
        out_shape=[
            jax.ShapeDtypeStruct((s, hq * HEAD_DIM), BF16),
            jax.ShapeDtypeStruct((s, hkv * HEAD_DIM), BF16),
            jax.ShapeDtypeStruct((s, hkv * HEAD_DIM), BF16),
        ],
        scratch_shapes=[pltpu.VMEM((s, HEAD_DIM), F32), pltpu.VMEM((s, HEAD_DIM), F32)],
        compiler_params=_params("parallel", "arbitrary", "arbitrary"),
    )(qkn, qkn, qkv, dout)


ADAM_ROWS = 128


def _adam_update(w, g, m, v):
    m = ADAM_B1 * m + (1.0 - ADAM_B1) * g
    v = ADAM_B2 * v + (1.0 - ADAM_B2) * (g * g)
    m_hat = m / (1.0 - ADAM_B1 ** ADAM_STEP)
    v_hat = v / (1.0 - ADAM_B2 ** ADAM_STEP)
    return -ADAM_LR * (m_hat / (jnp.sqrt(v_hat) + ADAM_EPS) + ADAM_WD * w), m, v


def _adamw_slab(w, m, v, own, sib, slab, prev, after, *, name):
    _, rows, cols = w.shape
    tr = _pick(rows, ADAM_ROWS, 16)
    tokens = [] if after is None else [after]

    def body(w_ref, m_ref, v_ref, own_ref, sib_ref, *rest):
        g_ref, d_ref, nm_ref, nv_ref = rest[-4:]
        g = own_ref[0].astype(F32) + sib_ref[0].astype(F32)
        for q in range(1, N_CHIPS):
            g = g + (own_ref[q].astype(F32) + sib_ref[q].astype(F32))
        g_ref[...] = g
        d_ref[...], nm_ref[...], nv_ref[...] = _adam_update(w_ref[...], g, m_ref[...], v_ref[...])

    one = pl.BlockSpec((None, tr, cols), lambda i: (slab, i, 0))
    piece = pl.BlockSpec((N_CHIPS, tr, cols), lambda i: (0, i, 0))
    carried = [] if prev is None else list(prev)
    shape = jax.ShapeDtypeStruct(w.shape, F32)
    return pl.pallas_call(
        body,
        name=name,
        grid=(rows // tr,),
        in_specs=[one] * 3
        + [piece] * 2
        + [pl.BlockSpec(memory_space=pl.ANY)] * len(carried)
        + [pl.BlockSpec(t.shape, lambda i: (0, 0)) for t in tokens],
        out_specs=[one] * 4,
        out_shape=[shape] * 4,
        input_output_aliases={5 + i: i for i in range(len(carried))},
        compiler_params=_params("parallel"),
    )(w, m, v, own, sib, *carried, *tokens)


def _adamw_small(w, g, m, v, *, name):
    def body(w_ref, g_ref, m_ref, v_ref, d_ref, nm_ref, nv_ref):
        d_ref[...], nm_ref[...], nv_ref[...] = _adam_update(w_ref[...], g_ref[...], m_ref[...], v_ref[...])

    shape = jax.ShapeDtypeStruct(w.shape, F32)
    return pl.pallas_call(
        body,
        name=name,
        out_shape=[shape] * 3,
        compiler_params=pltpu.CompilerParams(vmem_limit_bytes=VMEM_LIMIT),
    )(w, g, m, v)


def _position():
    x, y, c = lax.axis_index("x"), lax.axis_index("y"), lax.axis_index("c")
    return x, y, c, 2 * x + y


def _chip_device(chip, c):
    return (chip >> 1, chip & 1, c)


def _handshake(peers):
    barrier = pltpu.get_barrier_semaphore()
    for peer in peers:
        pl.semaphore_signal(barrier, inc=1, device_id=peer, device_id_type=MESH)
    pl.semaphore_wait(barrier, len(peers))


GATHER_CHUNKS = 2


def _gather_shards(shards, split, *, name, collective_id):
    n = len(shards)
    big = [a for a in range(n) if split[a]]
    small = [a for a in range(n) if not split[a]]
    y_nbr, x_nbr, far = 1, 2, 3

    def body(*refs):
        ins, outs = refs[:n], refs[n : 2 * n]
        near_send, near_recv, far_send, far_recv, pass_send, pass_recv, own_send, own_recv = refs[2 * n :]
        x, y, c, k = _position()
        _handshake([(x, y, 1 - c)] + [_chip_device(k ^ j, c) for j in range(1, N_CHIPS)])

        def own(a, chunk):
            r = shards[a].shape[0]
            part = pl.ds(chunk * (r // GATHER_CHUNKS), r // GATHER_CHUNKS) if split[a] else pl.ds(0, r)
            return pltpu.make_async_remote_copy(
                src_ref=ins[a].at[part],
                dst_ref=outs[a].at[k, part],
                send_sem=own_send.at[a, chunk],
                recv_sem=own_recv.at[a, chunk],
                device_id=(x, y, 1 - c),
                device_id_type=MESH,
            )

        own_copies = [own(a, ch) for a in range(n) for ch in range(GATHER_CHUNKS if split[a] else 1)]
        for cp in own_copies:
            cp.start()

        def run(core):
            sibling = (x, y, 1 - core)
            relay_from, relay_to = (x_nbr, y_nbr) if core == 0 else (y_nbr, x_nbr)

            def rows(a, which, chunk=None):
                r = shards[a].shape[0]
                if not split[a]:
                    return pl.ds(0, r)
                half = r // 2
                if chunk is None:
                    return pl.ds(which * half, half)
                return pl.ds(which * half + chunk * (half // GATHER_CHUNKS), half // GATHER_CHUNKS)

            def direct(a, mask, chunk, src_chip):
                part = rows(a, core, chunk)
                return pltpu.make_async_remote_copy(
                    src_ref=ins[a].at[part],
                    dst_ref=outs[a].at[src_chip, part],
                    send_sem=near_send.at[a, mask - 1, chunk or 0],
                    recv_sem=near_recv.at[a, mask - 1, chunk or 0],
                    device_id=_chip_device(k ^ mask, core),
                    device_id_type=MESH,
                )

            def relay(a, chunk, src_chip, mask):
                part = outs[a].at[src_chip, rows(a, core, chunk)]
                return pltpu.make_async_remote_copy(
                    src_ref=part,
                    dst_ref=part,
                    send_sem=far_send.at[a, chunk or 0],
                    recv_sem=far_recv.at[a, chunk or 0],
                    device_id=_chip_device(k ^ mask, core),
                    device_id_type=MESH,
                )

            def to_sibling(a, mask, which):
                part = outs[a].at[k ^ mask, rows(a, which)]
                return pltpu.make_async_remote_copy(
                    src_ref=part,
                    dst_ref=part,
                    send_sem=pass_send.at[a, mask - 1],
                    recv_sem=pass_recv.at[a, mask - 1],
                    device_id=sibling,
                    device_id_type=MESH,
                )

            sent = []

            def start(cp):
                cp.start()
                sent.append(cp)

            chunks = range(GATHER_CHUNKS)
            for chunk in chunks:
                for a in big:
                    start(direct(a, relay_from, chunk, k))
                    start(direct(a, relay_to, chunk, k))
            for a in small:
                start(direct(a, x_nbr, None, k))
                start(direct(a, y_nbr, None, k))
                start(pltpu.make_async_remote_copy(
                    src_ref=ins[a], dst_ref=outs[a].at[k], send_sem=far_send.at[a, 0], recv_sem=far_recv.at[a, 0],
                    device_id=_chip_device(k ^ far, core), device_id_type=MESH))
            for chunk in chunks:
                for a in big:
                    direct(a, relay_from, chunk, k ^ relay_from).wait_recv()
                    start(relay(a, chunk, k ^ relay_from, relay_to))
            for a in big:
                start(to_sibling(a, relay_from, core))
            for a in big:
                for chunk in chunks:
                    direct(a, relay_to, chunk, k ^ relay_to).wait_recv()
                start(to_sibling(a, relay_to, core))
            for a in big:
                for chunk in chunks:
                    relay(a, chunk, k ^ far, far).wait_recv()
                start(to_sibling(a, far, core))
            for a in small:
                direct(a, x_nbr, None, k ^ x_nbr).wait_recv()
                direct(a, y_nbr, None, k ^ y_nbr).wait_recv()
                relay(a, None, k ^ far, far).wait_recv()
            for a in big:
                for mask in (y_nbr, x_nbr, far):
                    to_sibling(a, mask, 1 - core).wait_recv()
            for cp in sent:
                cp.wait_send()

        for core in (0, 1):
            pl.when(c == core)(functools.partial(run, core))
        for cp in own_copies:
            cp.wait()

    return pl.kernel(
        body,
        name=name,
        out_type=[jax.ShapeDtypeStruct((N_CHIPS,) + a.shape, a.dtype) for a in shards],
        mesh=plsc.ScalarSubcoreMesh(axis_name="sequencer", num_cores=1),
        scratch_types=[
            pltpu.SemaphoreType.DMA((n, 2, GATHER_CHUNKS)),
            pltpu.SemaphoreType.DMA((n, 2, GATHER_CHUNKS)),
            pltpu.SemaphoreType.DMA((n, GATHER_CHUNKS)),
            pltpu.SemaphoreType.DMA((n, GATHER_CHUNKS)),
            pltpu.SemaphoreType.DMA((n, N_CHIPS - 1)),
            pltpu.SemaphoreType.DMA((n, N_CHIPS - 1)),
            pltpu.SemaphoreType.DMA((n, GATHER_CHUNKS)),
            pltpu.SemaphoreType.DMA((n, GATHER_CHUNKS)),
        ],
        compiler_params=pltpu.CompilerParams(collective_id=collective_id),
    )(*shards)


def _scatter_pieces(pieces, *, name, collective_id):
    n = len(pieces)

    def body(*refs):
        ins, own, sib = refs[:n], refs[n : 2 * n], refs[2 * n : 3 * n]
        local_sem, send_sem, recv_sem, pass_send, pass_recv = refs[3 * n :]
        x, y, c, k = _position()
        sibling = (x, y, 1 - c)
        _handshake([sibling] + [_chip_device(k ^ j, c) for j in range(1, N_CHIPS)])

        def over_ici(a, j, piece, slot, to):
            return pltpu.make_async_remote_copy(
                src_ref=ins[a].at[piece],
                dst_ref=own[a].at[slot],
                send_sem=send_sem.at[a, j],
                recv_sem=recv_sem.at[a, j],
                device_id=to,
                device_id_type=MESH,
            )

        def to_sibling(a, j, slot):
            return pltpu.make_async_remote_copy(
                src_ref=own[a].at[slot],
                dst_ref=sib[a].at[slot],
                send_sem=pass_send.at[a, j],
                recv_sem=pass_recv.at[a, j],
                device_id=sibling,
                device_id_type=MESH,
            )

        mine = [pltpu.make_async_copy(ins[a].at[k], own[a].at[k], local_sem.at[a]) for a in range(n)]
        for cp in mine:
            cp.start()
        sent = []
        for j in range(N_CHIPS - 1):
            other = k ^ (j + 1)
            for a in range(n):
                cp = over_ici(a, j, other, k, _chip_device(other, c))
                cp.start()
                sent.append(cp)
        for a in range(n):
            mine[a].wait()
            cp = to_sibling(a, N_CHIPS - 1, k)
            cp.start()
            sent.append(cp)
        for j in range(N_CHIPS - 1):
            other = k ^ (j + 1)
            for a in range(n):
                over_ici(a, j, other, other, sibling).wait_recv()
                cp = to_sibling(a, j, other)
                cp.start()
                sent.append(cp)
        for j in range(N_CHIPS):
            for a in range(n):
                to_sibling(a, j, k).wait_recv()
        for cp in sent:
            cp.wait_send()

    shapes = [jax.ShapeDtypeStruct(a.shape, a.dtype) for a in pieces]
    outs = pl.kernel(
        body,
        name=name,
        out_type=shapes + shapes,
        mesh=plsc.ScalarSubcoreMesh(axis_name="sequencer", num_cores=1),
        scratch_types=[
            pltpu.SemaphoreType.DMA((n,)),
            pltpu.SemaphoreType.DMA((n, N_CHIPS - 1)),
            pltpu.SemaphoreType.DMA((n, N_CHIPS - 1)),
            pltpu.SemaphoreType.DMA((n, N_CHIPS)),
            pltpu.SemaphoreType.DMA((n, N_CHIPS)),
        ],
        compiler_params=pltpu.CompilerParams(collective_id=collective_id),
    )(*pieces)
    return outs[:n], outs[n:]


def _allreduce_small(buf, *, name):
    def body(x_ref, o_ref, slots, send_sem, recv_sem):
        x, y, c, _ = _position()
        me = 4 * x + 2 * y + c
        slots[me] = x_ref[...]

        def copy(d, slot):
            peer = me ^ d
            return pltpu.make_async_remote_copy(
                src_ref=x_ref,
                dst_ref=slots.at[slot],
                send_sem=send_sem.at[d - 1],
                recv_sem=recv_sem.at[d - 1],
                device_id=(peer >> 2, (peer >> 1) & 1, peer & 1),
                device_id_type=MESH,
            )

        sent = [copy(d, me) for d in range(1, N_DEV)]
        for cp in sent:
            cp.start()
        for d in range(1, N_DEV):
            copy(d, me ^ d).wait_recv()
        for cp in sent:
            cp.wait_send()
        acc = slots[0]
        for s in range(1, N_DEV):
            acc = acc + slots[s]
        o_ref[...] = acc

    return pl.pallas_call(
        body,
        name=name,
        in_specs=[pl.BlockSpec(memory_space=pltpu.VMEM)],
        out_specs=pl.BlockSpec(memory_space=pltpu.VMEM),
        out_shape=jax.ShapeDtypeStruct(buf.shape, F32),
        scratch_shapes=[
            pltpu.VMEM((N_DEV,) + buf.shape, F32),
            pltpu.SemaphoreType.DMA((N_DEV - 1,)),
            pltpu.SemaphoreType.DMA((N_DEV - 1,)),
        ],
        compiler_params=pltpu.CompilerParams(vmem_limit_bytes=VMEM_LIMIT),
    )(buf)


def _mixer_of(i):
    return i % N_MIXERS, i // N_MIXERS


def _forward_backward(x, target, norms, layers, send=lambda i, part, pieces: pieces):
    s, d = x.shape
    depth = len(layers)
    heads = d // HEAD_DIM
    hkv = heads // GQA_GROUP
    cos, sin = _rope_tables(s)
    saved = []
    h = x
    for i, lw in enumerate(layers):
        kind, j = _mixer_of(i)
        tag = f"l{i}"
        sv = {"h_in": h}
        a = _rms_fwd(h, norms["mix_norm"][i : i + 1], name=f"{tag}_mix_norm")
        qkv = _mm_nn(a, lw["w_in"], out_dtype=BF16, name=f"{tag}_w_in")
        if kind == 0:
            bias = _na_bias(_na_base(norms["na_rpb"][j]), name=f"{tag}_na_bias")
            sv["bias"] = _na_block_bias(bias, s // GRID_W, name=f"{tag}_na_block_bias")
            o = _na_fwd(qkv, sv["bias"], name=f"{tag}_na_fwd")
        elif kind == 1:
            o = _sc_mid_fwd(qkv, lw["sc_conv_w"], name=f"{tag}_sc_fwd")
        else:
            gq, gk = norms["gqa_q_norm"][j : j + 1], norms["gqa_k_norm"][j : j + 1]
            qkn = _gqa_prep_fwd(qkv, gq, gk, cos, sin, heads, hkv, name=f"{tag}_gqa_prep")
            o = _gqa_fwd(qkn, qkv, heads, hkv, name=f"{tag}_gqa_fwd")
            sv["qkn"] = qkn
        h_mid = _mm_nn(o, lw["w_out"], out_dtype=F32, residual=h, name=f"{tag}_w_out")
        b = _rms_fwd(h_mid, norms["ffn_norm"][i : i + 1], name=f"{tag}_ffn_norm")
        up = _mm_nn(b, lw["w_up"], out_dtype=BF16, name=f"{tag}_w_up")
        act, cg, cu = _ffn_mid_fwd(up, lw["ffn_conv_w"], lw["ffn_conv_b"], name=f"{tag}_ffn_fwd")
        h = _mm_nn(act, lw["w_down"], out_dtype=F32, residual=h_mid, name=f"{tag}_w_down")
        sv.update(a=a, qkv=qkv, o=o, h_mid=h_mid, b=b, up=up, act=act, cg=cg, cu=cu)
        saved.append(sv)

    dh, dh_b, d_final, loss = _loss_head(h, norms["final_norm"][None], target, name="loss_head")

    big = [None] * depth
    small = {"final_norm": d_final, "mix_norm": [None] * depth, "ffn_norm": [None] * depth,
             "ffn_conv_w": [None] * depth, "ffn_conv_b": [None] * depth, "na_rpb": {}}
    after = ()
    for i in reversed(range(depth)):
        kind, j = _mixer_of(i)
        tag = f"l{i}b"
        lw, sv = layers[i], saved[i]
        dact = _mm_nt(dh_b, lw["w_down"], out_dtype=BF16, name=f"{tag}_d_act", after=after)
        dw_down = _mm_tn(sv["act"], dh_b, 1, name=f"{tag}_dw_down")
        dug, duu, dwg, dwu, dbg, dbu = _ffn_mid_bwd(
            sv["up"], sv["cg"], sv["cu"], lw["ffn_conv_w"], dact, name=f"{tag}_ffn_bwd"
        )
        small["ffn_conv_w"][i] = jnp.concatenate([dwg, dwu], axis=1)
        small["ffn_conv_b"][i] = jnp.concatenate([dbg, dbu], axis=1)
        dw_up = _mm_tn(sv["b"], (dug, duu), N_CHIPS, name=f"{tag}_dw_up")
        sent_ffn = send(i, "ffn", [dw_up, dw_down.reshape(N_CHIPS, dw_down.shape[1] // N_CHIPS, d)])
        db = _mm_nt((dug, duu), lw["w_up"], out_dtype=F32, name=f"{tag}_d_b")
        dh_mid, dh_mid_b, small["ffn_norm"][i] = _rms_bwd(
            sv["h_mid"], norms["ffn_norm"][i : i + 1], db, dh, name=f"{tag}_ffn_norm"
        )
        do = _mm_nt(dh_mid_b, lw["w_out"], out_dtype=BF16, name=f"{tag}_d_o", after=(_token(dw_up), _token(dw_down)))
        dw_out = _mm_tn(sv["o"], dh_mid_b, 1, name=f"{tag}_dw_out")
        if kind == 0:
            dq, dk, dv, dblock = _na_bwd(sv["qkv"], sv["bias"], do, name=f"{tag}_na_bwd")
            dbias = _na_unblock(dblock, s // GRID_W, name=f"{tag}_na_unblock")
            dqkv = jnp.concatenate([dq, dk, dv], axis=1)
            small["na_rpb"][j] = _rpb_fold(dbias, name=f"{tag}_rpb_fold")
        elif kind == 1:
            dgb, dgc, dhh, small["sc_conv_w"] = _sc_mid_bwd(sv["qkv"], lw["sc_conv_w"], do, name=f"{tag}_sc_bwd")
            dqkv = jnp.concatenate([dgb, dgc, dhh], axis=1)
        else:
            gq, gk = norms["gqa_q_norm"][j : j + 1], norms["gqa_k_norm"][j : j + 1]
            dqn, dkn, dv = _gqa_bwd(sv["qkn"], sv["qkv"], do, heads, hkv, name=f"{tag}_gqa_bwd")
            dqk, small["gqa_q_norm"], small["gqa_k_norm"] = _gqa_prep_bwd(
                sv["qkv"], gq, gk, cos, sin, dqn, dkn, heads, hkv, name=f"{tag}_gqa_prep_bwd"
            )
            dqkv = jnp.concatenate([dqk, dv], axis=1)
        dw_in = _mm_tn(sv["a"], dqkv, N_CHIPS, name=f"{tag}_dw_in")
        da = _mm_nt(dqkv, lw["w_in"], out_dtype=F32, name=f"{tag}_d_a")
        dh, dh_b, small["mix_norm"][i] = _rms_bwd(
            sv["h_in"], norms["mix_norm"][i : i + 1], da, dh_mid, name=f"{tag}_mix_norm"
        )
        sent_mix = send(i, "mix", [dw_in, dw_out.reshape(N_CHIPS, dw_out.shape[1] // N_CHIPS, d)])
        after = (_token(dw_in), _token(dw_out))
        big[i] = {"mix": sent_mix, "ffn": sent_ffn}
    return loss, dh, big, small


def _pack(parts):
    flat = jnp.concatenate([p.reshape(-1).astype(F32) for p in parts])
    pad = (-flat.shape[0]) % (8 * LANES)
    return jnp.pad(flat, (0, pad)).reshape(-1, LANES)


def _unpack(buf, shapes):
    flat = buf.reshape(-1)
    out, at = [], 0
    for shp in shapes:
        size = 1
        for n in shp:
            size *= n
        out.append(flat[at : at + size].reshape(shp))
        at += size
    return out


def kernel(x, mix_norm, ffn_norm, final_norm, na_w_qkv, na_rpb, na_w_o, sc_w_in, sc_conv_w, sc_w_out, gqa_w_qkv, gqa_q_norm, gqa_k_norm, gqa_w_o, ffn_w_up, ffn_conv_w, ffn_conv_b, ffn_w_down, loss_target, m_mix_norm, m_ffn_norm, m_final_norm, m_na_w_qkv, m_na_rpb, m_na_w_o, m_sc_w_in, m_sc_conv_w, m_sc_w_out, m_gqa_w_qkv, m_gqa_q_norm, m_gqa_k_norm, m_gqa_w_o, m_ffn_w_up, m_ffn_conv_w, m_ffn_conv_b, m_ffn_w_down, v_mix_norm, v_ffn_norm, v_final_norm, v_na_w_qkv, v_na_rpb, v_na_w_o, v_sc_w_in, v_sc_conv_w, v_sc_w_out, v_gqa_w_qkv, v_gqa_q_norm, v_gqa_k_norm, v_gqa_w_o, v_ffn_w_up, v_ffn_conv_w, v_ffn_conv_b, v_ffn_w_down):
    depth, d = mix_norm.shape
    chip = 2 * lax.axis_index("x") + lax.axis_index("y")
    w_in_of = {0: na_w_qkv, 1: sc_w_in, 2: gqa_w_qkv}
    w_out_of = {0: na_w_o, 1: sc_w_out, 2: gqa_w_o}

    layers = []
    for i in range(depth):
        kind, j = _mixer_of(i)
        shards = [_cast_slab(w_in_of[kind], j, name=f"cast_w_in_l{i}"), _cast_slab(w_out_of[kind], j, name=f"cast_w_out_l{i}")]
        split = [True, True]
        if kind == 1:
            shards.append(sc_conv_w[j])
            split.append(False)
        mix = _gather_shards(shards, split, name=f"gather_mix_l{i}", collective_id=1 + 2 * i)
        ffn = _gather_shards(
            [_cast_slab(ffn_w_up, i, name=f"cast_w_up_l{i}"), _cast_slab(ffn_w_down, i, name=f"cast_w_down_l{i}"),
             ffn_conv_w[i]],
            [True, True, False],
            name=f"gather_ffn_l{i}",
            collective_id=2 + 2 * i,
        )
        lw = {
            "w_in": mix[0],
            "w_out": mix[1].reshape(1, -1, d),
            "w_up": ffn[0],
            "w_down": ffn[1].reshape(1, -1, d),
            "ffn_conv_w": ffn[2],
            "ffn_conv_b": ffn_conv_b[i : i + 1],
        }
        if kind == 1:
            lw["sc_conv_w"] = mix[2]
        layers.append(lw)

    norms = dict(mix_norm=mix_norm, ffn_norm=ffn_norm, final_norm=final_norm, na_rpb=na_rpb,
                 gqa_q_norm=gqa_q_norm, gqa_k_norm=gqa_k_norm)

    def send(i, part, pieces):
        cid = 1 + 2 * depth + 2 * i + (part == "mix")
        return _scatter_pieces(pieces, name=f"scatter_{part}_l{i}", collective_id=cid)

    loss, grad_x, big, small = _forward_backward(x[0], loss_target[0], norms, layers, send)

    mixer_names = {0: ("na_w_qkv", "na_w_o"), 1: ("sc_w_in", "sc_w_out"), 2: ("gqa_w_qkv", "gqa_w_o")}
    state = {
        "na_w_qkv": (na_w_qkv, m_na_w_qkv, v_na_w_qkv), "na_w_o": (na_w_o, m_na_w_o, v_na_w_o),
        "sc_w_in": (sc_w_in, m_sc_w_in, v_sc_w_in), "sc_w_out": (sc_w_out, m_sc_w_out, v_sc_w_out),
        "gqa_w_qkv": (gqa_w_qkv, m_gqa_w_qkv, v_gqa_w_qkv), "gqa_w_o": (gqa_w_o, m_gqa_w_o, v_gqa_w_o),
        "ffn_w_up": (ffn_w_up, m_ffn_w_up, v_ffn_w_up), "ffn_w_down": (ffn_w_down, m_ffn_w_down, v_ffn_w_down),
    }
    res = {n: None for n in state}
    token = None
    for i in reversed(range(depth)):
        kind, j = _mixer_of(i)
        for part, names, slab in (("ffn", ("ffn_w_up", "ffn_w_down"), i), ("mix", mixer_names[kind], j)):
            own, sib = big[i][part]
            for slot, n in enumerate(names):
                res[n] = _adamw_slab(*state[n], own[slot], sib[slot], slab, res[n], token, name=f"adamw_{n}_l{i}")
                token = res[n][0][slab, :16, :LANES]

    n_na = na_rpb.shape[0]
    rpb_flat = jnp.stack([small["na_rpb"][j] for j in range(n_na)])
    full_parts = [
        loss[:, :1],
        jnp.concatenate(small["mix_norm"], axis=0),
        jnp.concatenate(small["ffn_norm"], axis=0),
        small["final_norm"],
        rpb_flat,
        small["sc_conv_w"],
        small["gqa_q_norm"],
        small["gqa_k_norm"],
        jnp.stack(small["ffn_conv_w"]),
        jnp.concatenate(small["ffn_conv_b"], axis=0),
    ]
    summed = _unpack(_allreduce_small(_pack(full_parts), name="allreduce_small"), [p.shape for p in full_parts])
    loss_all, g_mix, g_ffn, g_final, g_rpb, g_sc_cw, g_gq, g_gk, g_ffn_cw, g_ffn_cb = summed
    g_rpb = g_rpb[:, :, : RPB_ROWS * RPB_COLS].reshape(na_rpb.shape)
    g_sc_cw = lax.dynamic_slice_in_dim(g_sc_cw, chip * sc_conv_w.shape[2], sc_conv_w.shape[2], axis=1)[None]
    g_ffn_cw = lax.dynamic_slice_in_dim(g_ffn_cw, chip * ffn_conv_w.shape[2], ffn_conv_w.shape[2], axis=2)
    small_names = ["mix_norm", "ffn_norm", "final_norm", "na_rpb", "sc_conv_w", "gqa_q_norm", "gqa_k_norm",
                   "ffn_conv_w", "ffn_conv_b"]
    small_g = [g_mix, g_ffn, g_final.reshape(final_norm.shape), g_rpb, g_sc_cw, g_gq, g_gk, g_ffn_cw, g_ffn_cb]
    small_w = [mix_norm, ffn_norm, final_norm, na_rpb, sc_conv_w, gqa_q_norm, gqa_k_norm, ffn_conv_w, ffn_conv_b]
    small_m = [m_mix_norm, m_ffn_norm, m_final_norm, m_na_rpb, m_sc_conv_w, m_gqa_q_norm, m_gqa_k_norm,
               m_ffn_conv_w, m_ffn_conv_b]
    small_v = [v_mix_norm, v_ffn_norm, v_final_norm, v_na_rpb, v_sc_conv_w, v_gqa_q_norm, v_gqa_k_norm,
               v_ffn_conv_w, v_ffn_conv_b]
    shapes = [w.shape for w in small_w]
    packed = _adamw_small(_pack(small_w), _pack(small_g), _pack(small_m), _pack(small_v), name="adamw_small")
    small_d, small_nm, small_nv = (_unpack(p, shapes) for p in packed)
    for n, g, dl, nm, nv in zip(small_names, small_g, small_d, small_nm, small_nv):
        res[n] = (g.reshape(dl.shape), dl, nm, nv)

    order = ["mix_norm", "ffn_norm", "final_norm", "na_w_qkv", "na_rpb", "na_w_o", "sc_w_in", "sc_conv_w",
             "sc_w_out", "gqa_w_qkv", "gqa_q_norm", "gqa_k_norm", "gqa_w_o", "ffn_w_up", "ffn_conv_w",
             "ffn_conv_b", "ffn_w_down"]
    outs = [loss_all.reshape(()), grad_x[None]]
    for part in range(4):
        outs.extend(res[n][part] for n in order)
    return tuple(outs)
```

```python
import functools
import math

import jax
import jax.numpy as jnp
from jax import lax
from jax.experimental import pallas as pl
from jax.experimental.pallas import tpu as pltpu
from jax.experimental.pallas import tpu_sc as plsc

F32 = jnp.float32
BF16 = jnp.bfloat16
MESH = pl.DeviceIdType.MESH

N_CHIPS = 4
N_DEV = 8
N_MIXERS = 3
GRID_W = 64
HEAD_DIM = 128
EPS = 1e-6
NEG_INF = -1e30
NA_WIN_R = 8
NA_WIN_C = 16
GQA_GROUP = 4
ROPE_THETA = 10000.0
ADAM_LR = 0.001
ADAM_B1 = 0.9
ADAM_B2 = 0.999
ADAM_EPS = 1e-08
ADAM_WD = 0.01
ADAM_STEP = 10

LANES = 128
VMEM_LIMIT = 56 * 1024 * 1024
NT_DIMS = (((1,), (1,)), ((), ()))
TN_DIMS = (((0,), (0,)), ((), ()))


def _pick(n, cap, mult=LANES):
    best = None
    for t in range(mult, min(n, cap) + 1, mult):
        if n % t == 0:
            best = t
    return best if best is not None else n


def _params(*sem):
    return pltpu.CompilerParams(dimension_semantics=sem, vmem_limit_bytes=VMEM_LIMIT)


MM_VMEM_BUDGET = 47 * 1024 * 1024
MM_CONTRACT = 2816


def _mm_rows(m, blocks_for, mult=16):
    for cap in (1024, 512, 256, 128):
        tm = _pick(m, cap, mult)
        if sum(r * c * b * n for r, c, b, n in blocks_for(tm)) <= MM_VMEM_BUDGET:
            return tm
    return _pick(m, 128, mult)


def _accumulate(acc, step, steps, part, finish):
    if steps == 1:
        finish(part)
        return

    @pl.when(step == 0)
    def _():
        acc[...] = part

    @pl.when(step != 0)
    def _():
        acc[...] += part

    @pl.when(step == steps - 1)
    def _():
        finish(acc[...])


def _mm_nn(a, b, *, out_dtype, name, residual=None):
    m, k = a.shape
    nc, _, ncol = b.shape
    tn, tk = _pick(ncol, 1536), _pick(k, MM_CONTRACT)
    per, nk = ncol // tn, k // tk
    osz = jnp.dtype(out_dtype).itemsize
    tm = _mm_rows(m, lambda t: [(t, tk, a.dtype.itemsize, 2), (tk, tn, 2, 2), (t, tn, osz, 2),
                                (t, tn, 4, 2 * (residual is not None)), (t, tn, 4, nk > 1)])

    def body(a_ref, b_ref, *rest):
        o_ref = rest[-2] if nk > 1 else rest[-1]

        def finish(r):
            if residual is not None:
                r = r + rest[0][...]
            o_ref[...] = r.astype(o_ref.dtype)

        part = jnp.dot(a_ref[...].astype(BF16), b_ref[...], preferred_element_type=F32)
        _accumulate(rest[-1], pl.program_id(2), nk, part, finish)

    in_specs = [
        pl.BlockSpec((tm, tk), lambda i, j, kk: (i, kk)),
        pl.BlockSpec((None, tk, tn), lambda i, j, kk: (j // per, kk, j % per)),
    ]
    ops = [a, b]
    if residual is not None:
        in_specs.append(pl.BlockSpec((tm, tn), lambda i, j, kk: (i, j)))
        ops.append(residual)
    return pl.pallas_call(
        body,
        name=name,
        grid=(m // tm, nc * per, nk),
        in_specs=in_specs,
        out_specs=pl.BlockSpec((tm, tn), lambda i, j, kk: (i, j)),
        out_shape=jax.ShapeDtypeStruct((m, nc * ncol), out_dtype),
        scratch_shapes=[pltpu.VMEM((tm, tn), F32)] * (nk > 1),
        compiler_params=_params("parallel", "parallel", "arbitrary"),
    )(*ops)


def _token(x):
    return x[(0,) * (x.ndim - 2) + (slice(0, 16), slice(0, LANES))]


def _mm_nt(a, b, *, out_dtype, name, after=()):
    parts = tuple(a) if isinstance(a, (tuple, list)) else (a,)
    m, width = parts[0].shape
    nc, k, ncol = b.shape
    tko, tn = _pick(k, 1024), _pick(math.gcd(ncol, width), MM_CONTRACT)
    per, each, nn = ncol // tn, width // tn, len(parts) * width // tn
    osz = jnp.dtype(out_dtype).itemsize
    tm = _mm_rows(m, lambda t: [(t, tn, parts[0].dtype.itemsize, 2 * len(parts)), (tko, tn, 2, 2), (t, tko, osz, 2),
                                (t, tko, 4, nn > 1)])

    def body(*refs):
        a_refs, b_ref, rest = refs[: len(parts)], refs[len(parts)], refs[len(parts) + 1 :]
        o_ref = rest[len(after)]
        step = pl.program_id(2)

        def finish(r):
            o_ref[...] = r.astype(o_ref.dtype)

        def use(a_ref):
            part = lax.dot_general(a_ref[...].astype(BF16), b_ref[...], NT_DIMS, preferred_element_type=F32)
            _accumulate(rest[-1], step, nn, part, finish)

        _for_part(step // each, a_refs, use)

    return pl.pallas_call(
        body,
        name=name,
        grid=(m // tm, k // tko, nn),
        in_specs=[_part_spec((tm, tn), p, each, lambda i, j, s: (i, s)) for p in range(len(parts))]
        + [pl.BlockSpec((None, tko, tn), lambda i, j, s: (s // per, j, s % per))]
        + [pl.BlockSpec(t.shape, lambda i, j, s: (0, 0)) for t in after],
        out_specs=pl.BlockSpec((tm, tko), lambda i, j, s: (i, j)),
        out_shape=jax.ShapeDtypeStruct((m, k), out_dtype),
        scratch_shapes=[pltpu.VMEM((tm, tko), F32)] * (nn > 1),
        compiler_params=_params("parallel", "parallel", "arbitrary"),
    )(*parts, b, *after)


def _for_part(which, refs, use):
    if len(refs) == 1:
        use(refs[0])
        return
    for p, ref in enumerate(refs):
        pl.when(which == p)(functools.partial(use, ref))


def _part_spec(block, p, each, tile_of):
    def index(*ids):
        r, c = tile_of(*ids)
        return r, jnp.clip(c - p * each, 0, each - 1)

    return pl.BlockSpec(block, index)


def _mm_tn(a, g, nc, *, name):
    parts = tuple(g) if isinstance(g, (tuple, list)) else (g,)
    s, k = a.shape
    width = parts[0].shape[1]
    ncol = len(parts) * width // nc
    ts, tn = _pick(s, MM_CONTRACT, 16), _pick(math.gcd(ncol, width), 1536)
    per, each, ns = ncol // tn, width // tn, s // ts
    tko = _mm_rows(k, lambda t: [(ts, t, a.dtype.itemsize, 2), (ts, tn, parts[0].dtype.itemsize, 2 * len(parts)),
                                 (t, tn, 2, 2), (t, tn, 4, ns > 1)], mult=LANES)

    def body(a_ref, *refs):
        g_refs, rest = refs[: len(parts)], refs[len(parts) :]
        o_ref = rest[0]

        def finish(r):
            o_ref[...] = r.astype(o_ref.dtype)

        def use(g_ref):
            part = lax.dot_general(a_ref[...].astype(BF16), g_ref[...].astype(BF16), TN_DIMS, preferred_element_type=F32)
            _accumulate(rest[-1], pl.program_id(2), ns, part, finish)

        _for_part(pl.program_id(1) // each, g_refs, use)

    return pl.pallas_call(
        body,
        name=name,
        grid=(k // tko, nc * per, ns),
        in_specs=[pl.BlockSpec((ts, tko), lambda i, j, t: (t, i))]
        + [_part_spec((ts, tn), p, each, lambda i, j, t: (t, j)) for p in range(len(parts))],
        out_specs=pl.BlockSpec((None, tko, tn), lambda i, j, t: (j // per, i, j % per)),
        out_shape=jax.ShapeDtypeStruct((nc, k, ncol), BF16),
        scratch_shapes=[pltpu.VMEM((tko, tn), F32)] * (ns > 1),
        compiler_params=_params("parallel", "parallel", "arbitrary"),
    )(a, *parts)


ROW_TILE = 256


def _cast_slab(w, slab, *, name):
    _, rows, cols = w.shape
    tr = _pick(rows, ROW_TILE, 16)

    def body(w_ref, o_ref):
        o_ref[...] = w_ref[...].astype(o_ref.dtype)

    return pl.pallas_call(
        body,
        name=name,
        grid=(rows // tr,),
        in_specs=[pl.BlockSpec((None, tr, cols), lambda i: (slab, i, 0))],
        out_specs=pl.BlockSpec((tr, cols), lambda i: (i, 0)),
        out_shape=jax.ShapeDtypeStruct((rows, cols), BF16),
        compiler_params=_params("parallel"),
    )(w)


def _rms_fwd(h, g, *, name):
    s, d = h.shape
    tr = _pick(s, ROW_TILE, 16)

    def body(h_ref, g_ref, o_ref):
        x = h_ref[...]
        r = lax.rsqrt(jnp.mean(x * x, axis=-1, keepdims=True) + EPS)
        o_ref[...] = (x * r * g_ref[...]).astype(o_ref.dtype)

    return pl.pallas_call(
        body,
        name=name,
        grid=(s // tr,),
        in_specs=[pl.BlockSpec((tr, d), lambda i: (i, 0)), pl.BlockSpec((1, d), lambda i: (0, 0))],
        out_specs=pl.BlockSpec((tr, d), lambda i: (i, 0)),
        out_shape=jax.ShapeDtypeStruct((s, d), BF16),
        compiler_params=_params("parallel"),
    )(h, g)


def _rms_bwd(h, g, dy, dres, *, name):
    s, d = h.shape
    tr = _pick(s, ROW_TILE, 16)

    def body(h_ref, g_ref, dy_ref, dres_ref, dh_ref, dhb_ref, dg_ref):
        x = h_ref[...]
        r = lax.rsqrt(jnp.mean(x * x, axis=-1, keepdims=True) + EPS)
        xhat = x * r
        dyv = dy_ref[...].astype(F32)
        dyg = dyv * g_ref[...]
        dx = r * (dyg - xhat * jnp.mean(dyg * xhat, axis=-1, keepdims=True))
        dh = dres_ref[...] + dx
        dh_ref[...] = dh
        dhb_ref[...] = dh.astype(dhb_ref.dtype)
        part = jnp.sum(dyv * xhat, axis=0, keepdims=True)

        @pl.when(pl.program_id(0) == 0)
        def _():
            dg_ref[...] = part

        @pl.when(pl.program_id(0) != 0)
        def _():
            dg_ref[...] += part

    row = pl.BlockSpec((tr, d), lambda i: (i, 0))
    vec = pl.BlockSpec((1, d), lambda i: (0, 0))
    return pl.pallas_call(
        body,
        name=name,
        grid=(s // tr,),
        in_specs=[row, vec, row, row],
        out_specs=[row, row, vec],
        out_shape=[jax.ShapeDtypeStruct((s, d), F32), jax.ShapeDtypeStruct((s, d), BF16),
                   jax.ShapeDtypeStruct((1, d), F32)],
        compiler_params=_params("arbitrary"),
    )(h, g, dy, dres)


def _loss_head(h, g, target, *, name):
    s, d = h.shape
    tr = _pick(s, ROW_TILE, 16)

    def body(h_ref, g_ref, t_ref, dh_ref, dhb_ref, dg_ref, loss_ref):
        x = h_ref[...]
        r = lax.rsqrt(jnp.mean(x * x, axis=-1, keepdims=True) + EPS)
        xhat = x * r
        gv = g_ref[...]
        err = xhat * gv - t_ref[...]
        dyv = err * (1.0 / d)
        dyg = dyv * gv
        dh = r * (dyg - xhat * jnp.mean(dyg * xhat, axis=-1, keepdims=True))
        dh_ref[...] = dh
        dhb_ref[...] = dh.astype(dhb_ref.dtype)
        part = jnp.sum(dyv * xhat, axis=0, keepdims=True)
        lpart =jnp.sum(jnp.sum(err * err, axis=-1, keepdims=True), axis=0, keepdims=True) * (0.5 / d)

        @pl.when(pl.program_id(0) == 0)
        def _():
            dg_ref[...] = part
            loss_ref[...] = jnp.broadcast_to(lpart, loss_ref.shape)

        @pl.when(pl.program_id(0) != 0)
        def _():
            dg_ref[...] += part
            loss_ref[...] += jnp.broadcast_to(lpart, loss_ref.shape)

    row = pl.BlockSpec((tr, d), lambda i: (i, 0))
    vec = pl.BlockSpec((1, d), lambda i: (0, 0))
    return pl.pallas_call(
        body,
        name=name,
        grid=(s // tr,),
        in_specs=[row, vec, row],
        out_specs=[row, row, vec, pl.BlockSpec((1, LANES), lambda i: (0, 0))],
        out_shape=[
            jax.ShapeDtypeStruct((s, d), F32),
            jax.ShapeDtypeStruct((s, d), BF16),
            jax.ShapeDtypeStruct((1, d), F32),
            jax.ShapeDtypeStruct((1, LANES), F32),
        ],
        compiler_params=_params("arbitrary"),
    )(h, g, target)


def _shift_prev(x):
    row = lax.broadcasted_iota(jnp.int32, x.shape, 0)
    return jnp.where(row == 0, 0.0, pltpu.roll(x, 1, 0))


def _shift_next(x):
    n = x.shape[0]
    row = lax.broadcasted_iota(jnp.int32, x.shape, 0)
    return jnp.where(row == n - 1, 0.0, pltpu.roll(x, n - 1, 0))


def _conv3(x, w):
    xm, xp = _shift_prev(x), _shift_next(x)
    return xm * w[0:1] + x * w[1:2] + xp * w[2:3], xm, xp


def _conv3_t(d, w):
    return _shift_next(d) * w[0:1] + d * w[1:2] + _shift_prev(d) * w[2:3]


def _colsum(x):
    return jnp.sum(x, axis=0, keepdims=True)


def _ffn_mid_fwd(up, cw, cb, *, name):
    s, f2 = up.shape
    f = f2 // 2
    ncol = cw.shape[2]
    tc = _pick(ncol, 256)
    nt, per = f // tc, ncol // tc

    def body(ug_ref, uu_ref, wg_ref, wu_ref, bg_ref, bu_ref, o_ref, cg_ref, cu_ref):
        cg = _conv3(ug_ref[...].astype(F32), wg_ref[...])[0] + bg_ref[...]
        cu = _conv3(uu_ref[...].astype(F32), wu_ref[...])[0] + bu_ref[...]
        o_ref[...] = (cg * (1.0 / (1.0 + jnp.exp(-cg))) * cu).astype(o_ref.dtype)
        cg_ref[...] = cg.astype(cg_ref.dtype)
        cu_ref[...] = cu.astype(cu_ref.dtype)

    out = pl.BlockSpec((s, tc), lambda j: (0, j))
    return pl.pallas_call(
        body,
        name=name,
        grid=(nt,),
        in_specs=[
            pl.BlockSpec((s, tc), lambda j: (0, j)),
            pl.BlockSpec((s, tc), lambda j: (0, nt + j)),
            pl.BlockSpec((None, 3, tc), lambda j: (j // per, 0, j % per)),
            pl.BlockSpec((None, 3, tc), lambda j: ((nt + j) // per, 0, (nt + j) % per)),
            pl.BlockSpec((1, tc), lambda j: (0, j)),
            pl.BlockSpec((1, tc), lambda j: (0, nt + j)),
        ],
        out_specs=[out, out, out],
        out_shape=[jax.ShapeDtypeStruct((s, f), BF16)] * 3,
        compiler_params=_params("parallel"),
    )(up, up, cw, cw, cb, cb)


def _ffn_mid_bwd(up, cg, cu, cw, dact, *, name):
    s, f2 = up.shape
    f = f2 // 2
    ncol = cw.shape[2]
    tc = _pick(ncol, 256)
    nt, per = f // tc, ncol // tc

    def side(dc, u_ref, w_ref, du_ref, dw_ref, db_ref):
        w, u = w_ref[...], u_ref[...].astype(F32)
        nxt, prv = _shift_next(dc), _shift_prev(dc)
        du_ref[...] = (nxt * w[0:1] + dc * w[1:2] + prv * w[2:3]).astype(du_ref.dtype)
        dw_ref[0:1, :] = _colsum(nxt * u)
        dw_ref[1:2, :] = _colsum(dc * u)
        dw_ref[2:3, :] = _colsum(prv * u)
        db_ref[...] = _colsum(dc)

    def body(ug_ref, uu_ref, cg_ref, cu_ref, wg_ref, wu_ref, da_ref, dug_ref, duu_ref, dwg_ref, dwu_ref, dbg_ref, dbu_ref):
        cgv, cuv, da = cg_ref[...].astype(F32), cu_ref[...].astype(F32), da_ref[...].astype(F32)
        sig = 1.0 / (1.0 + jnp.exp(-cgv))
        side(da * cuv * (sig * (1.0 + cgv * (1.0 - sig))), ug_ref, wg_ref, dug_ref, dwg_ref, dbg_ref)
        side(da * (cgv * sig), uu_ref, wu_ref, duu_ref, dwu_ref, dbu_ref)

    col = pl.BlockSpec((s, tc), lambda j: (0, j))
    w3 = pl.BlockSpec((3, tc), lambda j: (0, j))
    b1 = pl.BlockSpec((1, tc), lambda j: (0, j))
    return pl.pallas_call(
        body,
        name=name,
        grid=(nt,),
        in_specs=[
            col,
            pl.BlockSpec((s, tc), lambda j: (0, nt + j)),
            col,
            col,
            pl.BlockSpec((None, 3, tc), lambda j: (j // per, 0, j % per)),
            pl.BlockSpec((None, 3, tc), lambda j: ((nt + j) // per, 0, (nt + j) % per)),
            col,
        ],
        out_specs=[col, col, w3, w3, b1, b1],
        out_shape=[
            jax.ShapeDtypeStruct((s, f), BF16),
            jax.ShapeDtypeStruct((s, f), BF16),
            jax.ShapeDtypeStruct((3, f), F32),
            jax.ShapeDtypeStruct((3, f), F32),
            jax.ShapeDtypeStruct((1, f), F32),
            jax.ShapeDtypeStruct((1, f), F32),
        ],
        compiler_params=_params("parallel"),
    )(up, up, cg, cu, cw, cw, dact)


def _sc_mid_fwd(z, cw, *, name):
    s, d3 = z.shape
    d = d3 // 3
    ncol = cw.shape[2]
    tc = _pick(ncol, 256)
    nt, per = d // tc, ncol // tc

    def body(gb_ref, gc_ref, hh_ref, w_ref, o_ref):
        p = gc_ref[...].astype(F32) * hh_ref[...].astype(F32)
        o_ref[...] = (gb_ref[...].astype(F32) * _conv3(p, w_ref[...])[0]).astype(o_ref.dtype)

    return pl.pallas_call(
        body,
        name=name,
        grid=(nt,),
        in_specs=[
            pl.BlockSpec((s, tc), lambda j: (0, j)),
            pl.BlockSpec((s, tc), lambda j: (0, nt + j)),
            pl.BlockSpec((s, tc), lambda j: (0, 2 * nt + j)),
            pl.BlockSpec((None, 3, tc), lambda j: (j // per, 0, j % per)),
        ],
        out_specs=pl.BlockSpec((s, tc), lambda j: (0, j)),
        out_shape=jax.ShapeDtypeStruct((s, d), BF16),
        compiler_params=_params("parallel"),
    )(z, z, z, cw)


def _sc_mid_bwd(z, cw, dmid, *, name):
    s, d3 = z.shape
    d = d3 // 3
    ncol = cw.shape[2]
    tc = _pick(ncol, 256)
    nt, per = d // tc, ncol // tc

    def body(gb_ref, gc_ref, hh_ref, w_ref, dm_ref, dgb_ref, dgc_ref, dhh_ref, dw_ref):
        gb, gc, hh = gb_ref[...].astype(F32), gc_ref[...].astype(F32), hh_ref[...].astype(F32)
        w = w_ref[...]
        p = gc * hh
        cv, pm, pp = _conv3(p, w)
        dm = dm_ref[...].astype(F32)
        dgb_ref[...] = (dm * cv).astype(dgb_ref.dtype)
        dcv = dm * gb
        dp = _conv3_t(dcv, w)
        dgc_ref[...] = (dp * hh).astype(dgc_ref.dtype)
        dhh_ref[...] = (dp * gc).astype(dhh_ref.dtype)
        dw_ref[0:1, :] = _colsum(dcv * pm)
        dw_ref[1:2, :] = _colsum(dcv * p)
        dw_ref[2:3, :] = _colsum(dcv * pp)

    col = pl.BlockSpec((s, tc), lambda j: (0, j))
    return pl.pallas_call(
        body,
        name=name,
        grid=(nt,),
        in_specs=[
            col,
            pl.BlockSpec((s, tc), lambda j: (0, nt + j)),
            pl.BlockSpec((s, tc), lambda j: (0, 2 * nt + j)),
            pl.BlockSpec((None, 3, tc), lambda j: (j // per, 0, j % per)),
            col,
        ],
        out_specs=[col, col, col, pl.BlockSpec((3, tc), lambda j: (0, j))],
        out_shape=[jax.ShapeDtypeStruct((s, d), BF16)] * 3 + [jax.ShapeDtypeStruct((3, d), F32)],
        compiler_params=_params("parallel"),
    )(z, z, z, cw, dmid)


NA_KEYS = NA_WIN_R * GRID_W


def _na_row_start(r, rows):
    return jnp.clip(r - NA_WIN_R // 2, 0, rows - NA_WIN_R)


def _na_base(rpb):
    h = rpb.shape[0]
    pos, neg = rpb[:, :, NA_WIN_C - 1:], rpb[:, :, : NA_WIN_C - 1]
    zeros = jnp.zeros((h, NA_WIN_R, GRID_W - 2 * NA_WIN_C + 1), F32)
    out = []
    for first in range(NA_WIN_R):
        p = pos[:, first : first + NA_WIN_R]
        n = jnp.roll(neg[:, first : first + NA_WIN_R], -1, axis=1)
        out.append(jnp.concatenate([p, zeros, n], axis=-1).reshape(h, 1, NA_KEYS))
    return jnp.stack(out, axis=1)


def _skew_right(x):
    return pltpu.roll(x, 0, 1, stride=1, stride_axis=0)


def _skew_left(x):
    n = x.shape[1]
    row = lax.broadcasted_iota(jnp.int32, x.shape, 0)
    for b in range(GRID_W.bit_length() - 1):
        x = jnp.where(((row >> b) & 1) == 1, pltpu.roll(x, n - (1 << b), 1), x)
    return x


def _na_bias(base, *, name):
    h = base.shape[0]

    def body(b_ref, o_ref):
        q = lax.broadcasted_iota(jnp.int32, (GRID_W, NA_KEYS), 0)
        kc = lax.broadcasted_iota(jnp.int32, (GRID_W, NA_KEYS), 1) % GRID_W
        start = jnp.clip(q - NA_WIN_C // 2, 0, GRID_W - NA_WIN_C)
        inside = (kc >= start) & (kc < start + NA_WIN_C)
        for slot in range(NA_WIN_R):
            x = _skew_right(jnp.broadcast_to(b_ref[slot], (GRID_W, NA_KEYS)))
            o_ref[slot] = jnp.where(inside, x, NEG_INF)

    return pl.pallas_call(
        body,
        name=name,
        grid=(h,),
        in_specs=[pl.BlockSpec((None, NA_WIN_R, 1, NA_KEYS), lambda i: (i, 0, 0, 0))],
        out_specs=pl.BlockSpec((None, NA_WIN_R, GRID_W, NA_KEYS), lambda i: (i, 0, 0, 0)),
        out_shape=jax.ShapeDtypeStruct((h, NA_WIN_R, GRID_W, NA_KEYS), F32),
        compiler_params=_params("parallel"),
    )(base)


NA_STEP_ROWS = 4
NA_BLOCK_ROWS = NA_STEP_ROWS + NA_WIN_R
NA_STEP_Q = NA_STEP_ROWS * GRID_W
NA_BLOCK_K = NA_BLOCK_ROWS * GRID_W
NA_PATTERNS = 3


def _na_plan(rows):
    steps = rows // NA_STEP_ROWS

    def start(r):
        return min(max(r - NA_WIN_R // 2, 0), rows - NA_WIN_R)

    plan = []
    for t in range(steps):
        first = [start(NA_STEP_ROWS * t + i) for i in range(NA_STEP_ROWS)]
        block = min(first[0], rows - NA_BLOCK_ROWS)
        offsets = tuple(f - block for f in first)
        slots = tuple(f - (NA_STEP_ROWS * t + i) + NA_WIN_R - 1 for i, f in enumerate(first))
        plan.append((offsets, slots))
    middle = plan[1] if steps > 2 else (tuple(range(NA_STEP_ROWS)), (NA_WIN_R // 2 - 1,) * NA_STEP_ROWS)
    assert steps >= 2 and all(p == middle for p in plan[1:-1])
    return steps, (plan[0], middle, plan[-1])


def _na_block_start(t, rows):
    return jnp.minimum(_na_row_start(NA_STEP_ROWS * t, rows), rows - NA_BLOCK_ROWS)


def _na_pattern(t, steps):
    return jnp.where(t == 0, 0, jnp.where(t == steps - 1, 2, 1))


def _na_block_bias(bias, rows, *, name):
    h = bias.shape[0]
    _, patterns = _na_plan(rows)

    def body(b_ref, o_ref):
        outside = jnp.full((GRID_W, NA_BLOCK_K - NA_KEYS), NEG_INF, F32)
        for p, (offsets, slots) in enumerate(patterns):

            @pl.when(pl.program_id(1) == p)
            def _(offsets=offsets, slots=slots):
                for i in range(NA_STEP_ROWS):
                    wide = jnp.concatenate([b_ref[slots[i]], outside], axis=1)
                    shift = offsets[i] * GRID_W
                    o_ref[i * GRID_W : (i + 1) * GRID_W, :] = pltpu.roll(wide, shift, 1) if shift else wide

    return pl.pallas_call(
        body,
        name=name,
        grid=(h, NA_PATTERNS),
        in_specs=[pl.BlockSpec((None, NA_WIN_R, GRID_W, NA_KEYS), lambda i, p: (i, 0, 0, 0))],
        out_specs=pl.BlockSpec((None, None, NA_STEP_Q, NA_BLOCK_K), lambda i, p: (i, p, 0, 0)),
        out_shape=jax.ShapeDtypeStruct((h, NA_PATTERNS, NA_STEP_Q, NA_BLOCK_K), F32),
        compiler_params=_params("parallel", "arbitrary"),
    )(bias)


def _na_unblock(dblock, rows, *, name):
    h = dblock.shape[0]
    steps, patterns = _na_plan(rows)
    used = [p for p in range(NA_PATTERNS) if p != 1 or steps > 2]

    def body(d_ref, o_ref):
        total = {}
        for p in used:
            offsets, slots = patterns[p]
            for i in range(NA_STEP_ROWS):
                wide = d_ref[p, i * GRID_W : (i + 1) * GRID_W, :]
                back = (NA_BLOCK_K - offsets[i] * GRID_W) % NA_BLOCK_K
                piece = (pltpu.roll(wide, back, 1) if back else wide)[:, :NA_KEYS]
                total[slots[i]] = piece if slots[i] not in total else total[slots[i]] + piece
        for slot in range(NA_WIN_R):
            o_ref[slot] = total.get(slot, jnp.zeros((GRID_W, NA_KEYS), F32))

    return pl.pallas_call(
        body,
        name=name,
        grid=(h,),
        in_specs=[pl.BlockSpec((None, NA_PATTERNS, NA_STEP_Q, NA_BLOCK_K), lambda i: (i, 0, 0, 0))],
        out_specs=pl.BlockSpec((None, NA_WIN_R, GRID_W, NA_KEYS), lambda i: (i, 0, 0, 0)),
        out_shape=jax.ShapeDtypeStruct((h, NA_WIN_R, GRID_W, NA_KEYS), F32),
        compiler_params=_params("parallel"),
    )(dblock)


def _na_specs(s, heads, rows, steps):
    q = pl.BlockSpec((NA_STEP_Q, HEAD_DIM), lambda h, t: (t, h))
    k = pl.BlockSpec((s, HEAD_DIM), lambda h, t: (0, heads + h))
    v = pl.BlockSpec((s, HEAD_DIM), lambda h, t: (0, 2 * heads + h))
    bias = pl.BlockSpec((None, None, NA_STEP_Q, NA_BLOCK_K), lambda h, t: (h, _na_pattern(t, steps), 0, 0))
    return q, k, v, bias


def _na_probs(q, k, bias):
    sc = lax.dot_general(q, k, NT_DIMS, preferred_element_type=F32) * (HEAD_DIM ** -0.5) + bias
    p = jnp.exp(sc - jnp.max(sc, axis=-1, keepdims=True))
    return p, jnp.sum(p, axis=-1, keepdims=True)


def _na_fwd(qkv, block_bias, *, name):
    s = qkv.shape[0]
    heads = qkv.shape[1] // (3 * HEAD_DIM)
    rows = s // GRID_W
    steps, _ = _na_plan(rows)

    def body(q_ref, k_ref, v_ref, b_ref, o_ref):
        start = _na_block_start(pl.program_id(1), rows)
        block = pl.ds(pl.multiple_of(start * GRID_W, GRID_W), NA_BLOCK_K)
        p, l = _na_probs(q_ref[...], k_ref[block, :], b_ref[...])
        o = jnp.dot(p.astype(BF16), v_ref[block, :], preferred_element_type=F32)
        o_ref[...] = (o / l).astype(o_ref.dtype)

    q, k, v, b = _na_specs(s, heads, rows, steps)
    return pl.pallas_call(
        body,
        name=name,
        grid=(heads, steps),
        in_specs=[q, k, v, b],
        out_specs=q,
        out_shape=jax.ShapeDtypeStruct((s, heads * HEAD_DIM), BF16),
        compiler_params=_params("parallel", "arbitrary"),
    )(qkv, qkv, qkv, block_bias)


def _na_bwd(qkv, block_bias, dout, *, name):
    s = qkv.shape[0]
    heads = qkv.shape[1] // (3 * HEAD_DIM)
    rows = s // GRID_W
    steps, _ = _na_plan(rows)
    scale = HEAD_DIM ** -0.5

    def body(q_ref, k_ref, v_ref, b_ref, do_ref, dq_ref, dk_ref, dv_ref, db_ref, dk_acc, dv_acc):
        step = pl.program_id(1)

        @pl.when(step == 0)
        def _():
            dk_acc[...] = jnp.zeros_like(dk_acc)
            dv_acc[...] = jnp.zeros_like(dv_acc)

        block = pl.ds(pl.multiple_of(_na_block_start(step, rows) * GRID_W, GRID_W), NA_BLOCK_K)
        q, k, v, do = q_ref[...], k_ref[block, :], v_ref[block, :], do_ref[...]
        p, l = _na_probs(q, k, b_ref[...])
        pn = p / l
        dp = lax.dot_general(do, v, NT_DIMS, preferred_element_type=F32)
        ds = pn * (dp - jnp.sum(pn * dp, axis=-1, keepdims=True))
        dsb = ds.astype(BF16)
        dq_ref[...] = (jnp.dot(dsb, k, preferred_element_type=F32) * scale).astype(dq_ref.dtype)
        dk_acc[block, :] += lax.dot_general(dsb, q, TN_DIMS, preferred_element_type=F32) * scale
        dv_acc[block, :] += lax.dot_general(pn.astype(BF16), do, TN_DIMS, preferred_element_type=F32)
        opens = (step <= 1) | (step == steps - 1)

        @pl.when(opens)
        def _():
            db_ref[...] = ds

        @pl.when(jnp.logical_not(opens))
        def _():
            db_ref[...] += ds

        @pl.when(step == steps - 1)
        def _():
            dk_ref[...] = dk_acc[...].astype(dk_ref.dtype)
            dv_ref[...] = dv_acc[...].astype(dv_ref.dtype)

    q, k, v, b = _na_specs(s, heads, rows, steps)
    kv_out = pl.BlockSpec((s, HEAD_DIM), lambda h, t: (0, h))
    shape = jax.ShapeDtypeStruct((s, heads * HEAD_DIM), BF16)
    return pl.pallas_call(
        body,
        name=name,
        grid=(heads, steps),
        in_specs=[q, k, v, b, q],
        out_specs=[q, kv_out, kv_out, b],
        out_shape=[shape, shape, shape, jax.ShapeDtypeStruct((heads, NA_PATTERNS, NA_STEP_Q, NA_BLOCK_K), F32)],
        scratch_shapes=[pltpu.VMEM((s, HEAD_DIM), F32), pltpu.VMEM((s, HEAD_DIM), F32)],
        compiler_params=_params("parallel", "arbitrary"),
    )(qkv, qkv, qkv, block_bias, dout)


RPB_ROWS = 2 * NA_WIN_R - 1
RPB_COLS = 2 * NA_WIN_C - 1
RPB_PAD = 512


def _rpb_fold_matrix():
    idx = jnp.arange(NA_WIN_R * NA_KEYS, dtype=jnp.int32)
    first, i, kc = idx // NA_KEYS, (idx // GRID_W) % NA_WIN_R, idx % GRID_W
    pos, neg = kc < NA_WIN_C, kc >= GRID_W - NA_WIN_C + 1
    dr = jnp.where(pos, first + i, first + (i + 1) % NA_WIN_R)
    dc = jnp.where(pos, kc + NA_WIN_C - 1, kc - (GRID_W - NA_WIN_C + 1))
    target = jnp.where(pos | neg, dr * RPB_COLS + dc, -1)
    return (target[:, None] == jnp.arange(RPB_PAD, dtype=jnp.int32)[None, :]).astype(F32)


def _rpb_fold(dbias, *, name):
    h = dbias.shape[0]

    def skew_body(g_ref, o_ref):
        for slot in range(NA_WIN_R):
            o_ref[slot] = _colsum(_skew_left(g_ref[slot]))

    skewed = pl.pallas_call(
        skew_body,
        name=name + "_skew",
        grid=(h,),
        in_specs=[pl.BlockSpec((None, NA_WIN_R, GRID_W, NA_KEYS), lambda i: (i, 0, 0, 0))],
        out_specs=pl.BlockSpec((None, NA_WIN_R, 1, NA_KEYS), lambda i: (i, 0, 0, 0)),
        out_shape=jax.ShapeDtypeStruct((h, NA_WIN_R, 1, NA_KEYS), F32),
        compiler_params=_params("parallel"),
    )(dbias)

    def fold_body(g_ref, m_ref, o_ref):
        o_ref[...] = jnp.dot(g_ref[...], m_ref[...], preferred_element_type=F32, precision=lax.Precision.HIGHEST)

    return pl.pallas_call(
        fold_body,
        name=name,
        out_shape=jax.ShapeDtypeStruct((h, RPB_PAD), F32),
        compiler_params=pltpu.CompilerParams(vmem_limit_bytes=VMEM_LIMIT),
    )(skewed.reshape(h, NA_WIN_R * NA_KEYS), _rpb_fold_matrix())


GQA_Q_TILE = 256
GQA_Q_TILE_BWD = 512


def _rope_tables(s):
    t = jnp.arange(s)
    row = (t // GRID_W).astype(F32)[:, None]
    col = (t % GRID_W).astype(F32)[:, None]
    half = HEAD_DIM // 2
    inv = ROPE_THETA ** (-jnp.arange(0, half, 2, dtype=F32) / half)
    ang = jnp.concatenate([row * inv, row * inv, col * inv, col * inv], axis=-1)
    return jnp.cos(ang), jnp.sin(ang)


def _rot_half(y):
    quarter = HEAD_DIM // 4
    lane = lax.broadcasted_iota(jnp.int32, y.shape, 1)
    low = (lane % (2 * quarter)) < quarter
    return jnp.where(low, -pltpu.roll(y, HEAD_DIM - quarter, 1), pltpu.roll(y, quarter, 1))


def _gqa_prep_fwd(qkv, gq, gk, cos, sin, hq, hkv, *, name):
    s = qkv.shape[0]

    def body(x_ref, gq_ref, gk_ref, cos_ref, sin_ref, o_ref):
        isq = pl.program_id(0) < hq
        x = x_ref[...].astype(F32)
        g = jnp.where(isq, gq_ref[...], gk_ref[...])
        y = x * lax.rsqrt(jnp.mean(x * x, axis=-1, keepdims=True) + EPS) * g
        z = y * cos_ref[...] + _rot_half(y) * sin_ref[...]
        o_ref[...] = (z * jnp.where(isq, HEAD_DIM ** -0.5, 1.0)).astype(o_ref.dtype)

    head = pl.BlockSpec((s, HEAD_DIM), lambda h: (0, h))
    vec = pl.BlockSpec((1, HEAD_DIM), lambda h: (0, 0))
    tab = pl.BlockSpec((s, HEAD_DIM), lambda h: (0, 0))
    return pl.pallas_call(
        body,
        name=name,
        grid=(hq + hkv,),
        in_specs=[head, vec, vec, tab, tab],
        out_specs=head,
        out_shape=jax.ShapeDtypeStruct((s, (hq + hkv) * HEAD_DIM), BF16),
        compiler_params=_params("parallel"),
    )(qkv, gq, gk, cos, sin)


def _gqa_prep_bwd(qkv, gq, gk, cos, sin, dqn, dkn, hq, hkv, *, name):
    s = qkv.shape[0]

    def body(x_ref, gq_ref, gk_ref, cos_ref, sin_ref, dq_ref, dk_ref, dx_ref, dgq_ref, dgk_ref):
        hh = pl.program_id(0)
        isq = hh < hq
        x = x_ref[...].astype(F32)
        g = jnp.where(isq, gq_ref[...], gk_ref[...])
        r = lax.rsqrt(jnp.mean(x * x, axis=-1, keepdims=True) + EPS)
        xhat = x * r
        dz = jnp.where(isq, dq_ref[...].astype(F32) * (HEAD_DIM ** -0.5), dk_ref[...].astype(F32))
        dy = dz * cos_ref[...] - _rot_half(dz * sin_ref[...])
        dyg = dy * g
        dx_ref[...] = (r * (dyg - xhat * jnp.mean(dyg * xhat, axis=-1, keepdims=True))).astype(dx_ref.dtype)
        part = _colsum(dy * xhat)

        @pl.when(hh == 0)
        def _():
            dgq_ref[...] = jnp.zeros_like(dgq_ref)
            dgk_ref[...] = jnp.zeros_like(dgk_ref)

        @pl.when(isq)
        def _():
            dgq_ref[...] += part

        @pl.when(jnp.logical_not(isq))
        def _():
            dgk_ref[...] += part

    head = pl.BlockSpec((s, HEAD_DIM), lambda h: (0, h))
    vec = pl.BlockSpec((1, HEAD_DIM), lambda h: (0, 0))
    tab = pl.BlockSpec((s, HEAD_DIM), lambda h: (0, 0))
    return pl.pallas_call(
        body,
        name=name,
        grid=(hq + hkv,),
        in_specs=[
            head,
            vec,
            vec,
            tab,
            tab,
            pl.BlockSpec((s, HEAD_DIM), lambda h: (0, jnp.minimum(h, hq - 1))),
            pl.BlockSpec((s, HEAD_DIM), lambda h: (0, jnp.maximum(h - hq, 0))),
        ],
        out_specs=[head, vec, vec],
        out_shape=[
            jax.ShapeDtypeStruct((s, (hq + hkv) * HEAD_DIM), BF16),
            jax.ShapeDtypeStruct((1, HEAD_DIM), F32),
            jax.ShapeDtypeStruct((1, HEAD_DIM), F32),
        ],
        compiler_params=_params("arbitrary"),
    )(qkv, gq, gk, cos, sin, dqn, dkn)


def _gqa_fwd(qkn, qkv, hq, hkv, *, name):
    s = qkv.shape[0]
    tq = _pick(s, GQA_Q_TILE, 16)

    def body(q_ref, k_ref, v_ref, o_ref):
        sc = lax.dot_general(q_ref[...], k_ref[...], NT_DIMS, preferred_element_type=F32)
        p = jnp.exp(sc - jnp.max(sc, axis=-1, keepdims=True))
        l = jnp.sum(p, axis=-1, keepdims=True)
        o_ref[...] = (jnp.dot(p.astype(BF16), v_ref[...], preferred_element_type=F32) / l).astype(o_ref.dtype)

    q = pl.BlockSpec((tq, HEAD_DIM), lambda h, i: (i, h))
    return pl.pallas_call(
        body,
        name=name,
        grid=(hq, s // tq),
        in_specs=[
            q,
            pl.BlockSpec((s, HEAD_DIM), lambda h, i: (0, hq + h // GQA_GROUP)),
            pl.BlockSpec((s, HEAD_DIM), lambda h, i: (0, hq + hkv + h // GQA_GROUP)),
        ],
        out_specs=q,
        out_shape=jax.ShapeDtypeStruct((s, hq * HEAD_DIM), BF16),
        compiler_params=_params("parallel", "parallel"),
    )(qkn, qkn, qkv)


def _gqa_bwd(qkn, qkv, dout, hq, hkv, *, name):
    s = qkv.shape[0]
    tq = _pick(s, GQA_Q_TILE_BWD, 16)
    nq = s // tq

    def body(q_ref, k_ref, v_ref, do_ref, dq_ref, dk_ref, dv_ref, dk_acc, dv_acc):
        g, i = pl.program_id(1), pl.program_id(2)

        @pl.when((g == 0) & (i == 0))
        def _():
            dk_acc[...] = jnp.zeros_like(dk_acc)
            dv_acc[...] = jnp.zeros_like(dv_acc)

        q, k, v, do = q_ref[...], k_ref[...], v_ref[...], do_ref[...]
        sc = lax.dot_general(q, k, NT_DIMS, preferred_element_type=F32)
        p = jnp.exp(sc - jnp.max(sc, axis=-1, keepdims=True))
        pn = p / jnp.sum(p, axis=-1, keepdims=True)
        dp = lax.dot_general(do, v, NT_DIMS, preferred_element_type=F32)
        dsb = (pn * (dp - jnp.sum(pn * dp, axis=-1, keepdims=True))).astype(BF16)
        dq_ref[...] = jnp.dot(dsb, k, preferred_element_type=F32).astype(dq_ref.dtype)
        dk_acc[...] += lax.dot_general(dsb, q, TN_DIMS, preferred_element_type=F32)
        dv_acc[...] += lax.dot_general(pn.astype(BF16), do, TN_DIMS, preferred_element_type=F32)

        @pl.when((g == GQA_GROUP - 1) & (i == nq - 1))
        def _():
            dk_ref[...] = dk_acc[...].astype(dk_ref.dtype)
            dv_ref[...] = dv_acc[...].astype(dv_ref.dtype)

    q = pl.BlockSpec((tq, HEAD_DIM), lambda kv, g, i: (i, kv * GQA_GROUP + g))
    kv_out = pl.BlockSpec((s, HEAD_DIM), lambda kv, g, i: (0, kv))
    return pl.pallas_call(
        body,
        name=name,
        grid=(hkv, GQA_GROUP, nq),
        in_specs=[
            q,
            pl.BlockSpec((s, HEAD_DIM), lambda kv, g, i: (0, hq + kv)),
            pl.BlockSpec((s, HEAD_DIM), lambda kv, g, i: (0, hq + hkv + kv)),
            q,
        ],
        out_specs=[q, kv_out, kv_out],
        out_shape=[
            jax.ShapeDtypeStruct((s, hq * HEAD_DIM), BF16),
            jax.ShapeDtypeStruct((s, hkv * HEAD_DIM), BF16),
            jax.ShapeDtypeStruct((s, hkv * HEAD_DIM), BF16),
        ],
        scratch_shapes=[pltpu.VMEM((s, HEAD_DIM), F32), pltpu.VMEM((s, HEAD_DIM), F32)],
        compiler_params=_params("parallel", "arbitrary", "arbitrary"),
    )(qkn, qkn, qkv, dout)


ADAM_ROWS = 128


def _adam_update(w, g, m, v):
    m = ADAM_B1 * m + (1.0 - ADAM_B1) * g
    v = ADAM_B2 * v + (1.0 - ADAM_B2) * (g * g)
    m_hat = m / (1.0 - ADAM_B1 ** ADAM_STEP)
    v_hat = v / (1.0 - ADAM_B2 ** ADAM_STEP)
    return -ADAM_LR * (m_hat / (jnp.sqrt(v_hat) + ADAM_EPS) + ADAM_WD * w), m, v


def _adamw_slab(w, m, v, own, sib, slab, prev, after, *, name):
    _, rows, cols = w.shape
    tr = _pick(rows, ADAM_ROWS, 16)
    tokens = [] if after is None else [after]

    def body(w_ref, m_ref, v_ref, own_ref, sib_ref, *rest):
        g_ref, d_ref, nm_ref, nv_ref = rest[-4:]
        g = own_ref[0].astype(F32) + sib_ref[0].astype(F32)
        for q in range(1, N_CHIPS):
            g = g + (own_ref[q].astype(F32) + sib_ref[q].astype(F32))
        g_ref[...] = g
        d_ref[...], nm_ref[...], nv_ref[...] = _adam_update(w_ref[...], g, m_ref[...], v_ref[...])

    one = pl.BlockSpec((None, tr, cols), lambda i: (slab, i, 0))
    piece = pl.BlockSpec((N_CHIPS, tr, cols), lambda i: (0, i, 0))
    carried = [] if prev is None else list(prev)
    shape = jax.ShapeDtypeStruct(w.shape, F32)
    return pl.pallas_call(
        body,
        name=name,
        grid=(rows // tr,),
        in_specs=[one] * 3
        + [piece] * 2
        + [pl.BlockSpec(memory_space=pl.ANY)] * len(carried)
        + [pl.BlockSpec(t.shape, lambda i: (0, 0)) for t in tokens],
        out_specs=[one] * 4,
        out_shape=[shape] * 4,
        input_output_aliases={5 + i: i for i in range(len(carried))},
        compiler_params=_params("parallel"),
    )(w, m, v, own, sib, *carried, *tokens)


def _adamw_small(w, g, m, v, *, name):
    def body(w_ref, g_ref, m_ref, v_ref, d_ref, nm_ref, nv_ref):
        d_ref[...], nm_ref[...], nv_ref[...] = _adam_update(w_ref[...], g_ref[...], m_ref[...], v_ref[...])

    shape = jax.ShapeDtypeStruct(w.shape, F32)
    return pl.pallas_call(
        body,
        name=name,
        out_shape=[shape] * 3,
        compiler_params=pltpu.CompilerParams(vmem_limit_bytes=VMEM_LIMIT),
    )(w, g, m, v)


def _position():
    x, y, c = lax.axis_index("x"), lax.axis_index("y"), lax.axis_index("c")
    return x, y, c, 2 * x + y


def _chip_device(chip, c):
    return (chip >> 1, chip & 1, c)


def _handshake(peers):
    barrier = pltpu.get_barrier_semaphore()
    for peer in peers:
        pl.semaphore_signal(barrier, inc=1, device_id=peer, device_id_type=MESH)
    pl.semaphore_wait(barrier, len(peers))


GATHER_CHUNKS = 2


def _gather_shards(shards, split, *, name, collective_id):
    n = len(shards)
    big = [a for a in range(n) if split[a]]
    small = [a for a in range(n) if not split[a]]
    y_nbr, x_nbr, far = 1, 2, 3

    def body(*refs):
        ins, outs = refs[:n], refs[n : 2 * n]
        near_send, near_recv, far_send, far_recv, pass_send, pass_recv, own_send, own_recv = refs[2 * n :]
        x, y, c, k = _position()
        _handshake([(x, y, 1 - c)] + [_chip_device(k ^ j, c) for j in range(1, N_CHIPS)])

        def own(a, chunk):
            r = shards[a].shape[0]
            part = pl.ds(chunk * (r // GATHER_CHUNKS), r // GATHER_CHUNKS) if split[a] else pl.ds(0, r)
            return pltpu.make_async_remote_copy(
                src_ref=ins[a].at[part],
                dst_ref=outs[a].at[k, part],
                send_sem=own_send.at[a, chunk],
                recv_sem=own_recv.at[a, chunk],
                device_id=(x, y, 1 - c),
                device_id_type=MESH,
            )

        own_copies = [own(a, ch) for a in range(n) for ch in range(GATHER_CHUNKS if split[a] else 1)]
        for cp in own_copies:
            cp.start()

        def run(core):
            sibling = (x, y, 1 - core)
            relay_from, relay_to = (x_nbr, y_nbr) if core == 0 else (y_nbr, x_nbr)

            def rows(a, which, chunk=None):
                r = shards[a].shape[0]
                if not split[a]:
                    return pl.ds(0, r)
                half = r // 2
                if chunk is None:
                    return pl.ds(which * half, half)
                return pl.ds(which * half + chunk * (half // GATHER_CHUNKS), half // GATHER_CHUNKS)

            def direct(a, mask, chunk, src_chip):
                part = rows(a, core, chunk)
                return pltpu.make_async_remote_copy(
                    src_ref=ins[a].at[part],
                    dst_ref=outs[a].at[src_chip, part],
                    send_sem=near_send.at[a, mask - 1, chunk or 0],
                    recv_sem=near_recv.at[a, mask - 1, chunk or 0],
                    device_id=_chip_device(k ^ mask, core),
                    device_id_type=MESH,
                )

            def relay(a, chunk, src_chip, mask):
                part = outs[a].at[src_chip, rows(a, core, chunk)]
                return pltpu.make_async_remote_copy(
                    src_ref=part,
                    dst_ref=part,
                    send_sem=far_send.at[a, chunk or 0],
                    recv_sem=far_recv.at[a, chunk or 0],
                    device_id=_chip_device(k ^ mask, core),
                    device_id_type=MESH,
                )

            def to_sibling(a, mask, which):
                part = outs[a].at[k ^ mask, rows(a, which)]
                return pltpu.make_async_remote_copy(
                    src_ref=part,
                    dst_ref=part,
                    send_sem=pass_send.at[a, mask - 1],
                    recv_sem=pass_recv.at[a, mask - 1],
                    device_id=sibling,
                    device_id_type=MESH,
                )

            sent = []

            def start(cp):
                cp.start()
                sent.append(cp)

            chunks = range(GATHER_CHUNKS)
            for chunk in chunks:
                for a in big:
                    start(direct(a, relay_from, chunk, k))
                    start(direct(a, relay_to, chunk, k))
            for a in small:
                start(direct(a, x_nbr, None, k))
                start(direct(a, y_nbr, None, k))
                start(pltpu.make_async_remote_copy(
                    src_ref=ins[a], dst_ref=outs[a].at[k], send_sem=far_send.at[a, 0], recv_sem=far_recv.at[a, 0],
                    device_id=_chip_device(k ^ far, core), device_id_type=MESH))
            for chunk in chunks:
                for a in big:
                    direct(a, relay_from, chunk, k ^ relay_from).wait_recv()
                    start(relay(a, chunk, k ^ relay_from, relay_to))
            for a in big:
                start(to_sibling(a, relay_from, core))
            for a in big:
                for chunk in chunks:
                    direct(a, relay_to, chunk, k ^ relay_to).wait_recv()
                start(to_sibling(a, relay_to, core))
            for a in big:
                for chunk in chunks:
                    relay(a, chunk, k ^ far, far).wait_recv()
                start(to_sibling(a, far, core))
            for a in small:
                direct(a, x_nbr, None, k ^ x_nbr).wait_recv()
                direct(a, y_nbr, None, k ^ y_nbr).wait_recv()
                relay(a, None, k ^ far, far).wait_recv()
            for a in big:
                for mask in (y_nbr, x_nbr, far):
                    to_sibling(a, mask, 1 - core).wait_recv()
            for cp in sent:
                cp.wait_send()

        for core in (0, 1):
            pl.when(c == core)(functools.partial(run, core))
        for cp in own_copies:
            cp.wait()

    return pl.kernel(
        body,
        name=name,
        out_type=[jax.ShapeDtypeStruct((N_CHIPS,) + a.shape, a.dtype) for a in shards],
        mesh=plsc.ScalarSubcoreMesh(axis_name="sequencer", num_cores=1),
        scratch_types=[
            pltpu.SemaphoreType.DMA((n, 2, GATHER_CHUNKS)),
            pltpu.SemaphoreType.DMA((n, 2, GATHER_CHUNKS)),
            pltpu.SemaphoreType.DMA((n, GATHER_CHUNKS)),
            pltpu.SemaphoreType.DMA((n, GATHER_CHUNKS)),
            pltpu.SemaphoreType.DMA((n, N_CHIPS - 1)),
            pltpu.SemaphoreType.DMA((n, N_CHIPS - 1)),
            pltpu.SemaphoreType.DMA((n, GATHER_CHUNKS)),
            pltpu.SemaphoreType.DMA((n, GATHER_CHUNKS)),
        ],
        compiler_params=pltpu.CompilerParams(collective_id=collective_id),
    )(*shards)


def _scatter_pieces(pieces, *, name, collective_id):
    n = len(pieces)

    def body(*refs):
        ins, own, sib = refs[:n], refs[n : 2 * n], refs[2 * n : 3 * n]
        local_sem, send_sem, recv_sem, pass_send, pass_recv = refs[3 * n :]
        x, y, c, k = _position()
        sibling = (x, y, 1 - c)
        _handshake([sibling] + [_chip_device(k ^ j, c) for j in range(1, N_CHIPS)])

        def over_ici(a, j, piece, slot, to):
            return pltpu.make_async_remote_copy(
                src_ref=ins[a].at[piece],
                dst_ref=own[a].at[slot],
                send_sem=send_sem.at[a, j],
                recv_sem=recv_sem.at[a, j],
                device_id=to,
                device_id_type=MESH,
            )

        def to_sibling(a, j, slot):
            return pltpu.make_async_remote_copy(
                src_ref=own[a].at[slot],
                dst_ref=sib[a].at[slot],
                send_sem=pass_send.at[a, j],
                recv_sem=pass_recv.at[a, j],
                device_id=sibling,
                device_id_type=MESH,
            )

        mine = [pltpu.make_async_copy(ins[a].at[k], own[a].at[k], local_sem.at[a]) for a in range(n)]
        for cp in mine:
            cp.start()
        sent = []
        for j in range(N_CHIPS - 1):
            other = k ^ (j + 1)
            for a in range(n):
                cp = over_ici(a, j, other, k, _chip_device(other, c))
                cp.start()
                sent.append(cp)
        for a in range(n):
            mine[a].wait()
            cp = to_sibling(a, N_CHIPS - 1, k)
            cp.start()
            sent.append(cp)
        for j in range(N_CHIPS - 1):
            other = k ^ (j + 1)
            for a in range(n):
                over_ici(a, j, other, other, sibling).wait_recv()
                cp = to_sibling(a, j, other)
                cp.start()
                sent.append(cp)
        for j in range(N_CHIPS):
            for a in range(n):
                to_sibling(a, j, k).wait_recv()
        for cp in sent:
            cp.wait_send()

    shapes = [jax.ShapeDtypeStruct(a.shape, a.dtype) for a in pieces]
    outs = pl.kernel(
        body,
        name=name,
        out_type=shapes + shapes,
        mesh=plsc.ScalarSubcoreMesh(axis_name="sequencer", num_cores=1),
        scratch_types=[
            pltpu.SemaphoreType.DMA((n,)),
            pltpu.SemaphoreType.DMA((n, N_CHIPS - 1)),
            pltpu.SemaphoreType.DMA((n, N_CHIPS - 1)),
            pltpu.SemaphoreType.DMA((n, N_CHIPS)),
            pltpu.SemaphoreType.DMA((n, N_CHIPS)),
        ],
        compiler_params=pltpu.CompilerParams(collective_id=collective_id),
    )(*pieces)
    return outs[:n], outs[n:]


def _allreduce_small(buf, *, name):
    def body(x_ref, o_ref, slots, send_sem, recv_sem):
        x, y, c, _ = _position()
        me = 4 * x + 2 * y + c
        slots[me] = x_ref[...]

        def copy(d, slot):
            peer = me ^ d
            return pltpu.make_async_remote_copy(
                src_ref=x_ref,
                dst_ref=slots.at[slot],
                send_sem=send_sem.at[d - 1],
                recv_sem=recv_sem.at[d - 1],
                device_id=(peer >> 2, (peer >> 1) & 1, peer & 1),
                device_id_type=MESH,
            )

        sent = [copy(d, me) for d in range(1, N_DEV)]
        for cp in sent:
            cp.start()
        for d in range(1, N_DEV):
            copy(d, me ^ d).wait_recv()
        for cp in sent:
            cp.wait_send()
        acc = slots[0]
        for s in range(1, N_DEV):
            acc = acc + slots[s]
        o_ref[...] = acc

    return pl.pallas_call(
        body,
        name=name,
        in_specs=[pl.BlockSpec(memory_space=pltpu.VMEM)],
        out_specs=pl.BlockSpec(memory_space=pltpu.VMEM),
        out_shape=jax.ShapeDtypeStruct(buf.shape, F32),
        scratch_shapes=[
            pltpu.VMEM((N_DEV,) + buf.shape, F32),
            pltpu.SemaphoreType.DMA((N_DEV - 1,)),
            pltpu.SemaphoreType.DMA((N_DEV - 1,)),
        ],
        compiler_params=pltpu.CompilerParams(vmem_limit_bytes=VMEM_LIMIT),
    )(buf)


def _mixer_of(i):
    return i % N_MIXERS, i // N_MIXERS


def _forward_backward(x, target, norms, layers, send=lambda i, part, pieces: pieces):
    s, d = x.shape
    depth = len(layers)
    heads = d // HEAD_DIM
    hkv = heads // GQA_GROUP
    cos, sin = _rope_tables(s)
    saved = []
    h = x
    for i, lw in enumerate(layers):
        kind, j = _mixer_of(i)
        tag = f"l{i}"
        sv = {"h_in": h}
        a = _rms_fwd(h, norms["mix_norm"][i : i + 1], name=f"{tag}_mix_norm")
        qkv = _mm_nn(a, lw["w_in"], out_dtype=BF16, name=f"{tag}_w_in")
        if kind == 0:
            bias = _na_bias(_na_base(norms["na_rpb"][j]), name=f"{tag}_na_bias")
            sv["bias"] = _na_block_bias(bias, s // GRID_W, name=f"{tag}_na_block_bias")
            o = _na_fwd(qkv, sv["bias"], name=f"{tag}_na_fwd")
        elif kind == 1:
            o = _sc_mid_fwd(qkv, lw["sc_conv_w"], name=f"{tag}_sc_fwd")
        else:
            gq, gk = norms["gqa_q_norm"][j : j + 1], norms["gqa_k_norm"][j : j + 1]
            qkn = _gqa_prep_fwd(qkv, gq, gk, cos, sin, heads, hkv, name=f"{tag}_gqa_prep")
            o = _gqa_fwd(qkn, qkv, heads, hkv, name=f"{tag}_gqa_fwd")
            sv["qkn"] = qkn
        h_mid = _mm_nn(o, lw["w_out"], out_dtype=F32, residual=h, name=f"{tag}_w_out")
        b = _rms_fwd(h_mid, norms["ffn_norm"][i : i + 1], name=f"{tag}_ffn_norm")
        up = _mm_nn(b, lw["w_up"], out_dtype=BF16, name=f"{tag}_w_up")
        act, cg, cu = _ffn_mid_fwd(up, lw["ffn_conv_w"], lw["ffn_conv_b"], name=f"{tag}_ffn_fwd")
        h = _mm_nn(act, lw["w_down"], out_dtype=F32, residual=h_mid, name=f"{tag}_w_down")
        sv.update(a=a, qkv=qkv, o=o, h_mid=h_mid, b=b, up=up, act=act, cg=cg, cu=cu)
        saved.append(sv)

    dh, dh_b, d_final, loss = _loss_head(h, norms["final_norm"][None], target, name="loss_head")

    big = [None] * depth
    small = {"final_norm": d_final, "mix_norm": [None] * depth, "ffn_norm": [None] * depth,
             "ffn_conv_w": [None] * depth, "ffn_conv_b": [None] * depth, "na_rpb": {}}
    after = ()
    for i in reversed(range(depth)):
        kind, j = _mixer_of(i)
        tag = f"l{i}b"
        lw, sv = layers[i], saved[i]
        dw_down = _mm_tn(sv["act"], dh_b, 1, name=f"{tag}_dw_down")
        sent_down = send(i, "down", [dw_down.reshape(N_CHIPS, dw_down.shape[1] // N_CHIPS, d)])
        dact = _mm_nt(dh_b, lw["w_down"], out_dtype=BF16, name=f"{tag}_d_act", after=after + (_token(dw_down),))
        dug, duu, dwg, dwu, dbg, dbu = _ffn_mid_bwd(
            sv["up"], sv["cg"], sv["cu"], lw["ffn_conv_w"], dact, name=f"{tag}_ffn_bwd"
        )
        small["ffn_conv_w"][i] = jnp.concatenate([dwg, dwu], axis=1)
        small["ffn_conv_b"][i] = jnp.concatenate([dbg, dbu], axis=1)
        dw_up = _mm_tn(sv["b"], (dug, duu), N_CHIPS, name=f"{tag}_dw_up")
        sent_up = send(i, "up", [dw_up])
        db = _mm_nt((dug, duu), lw["w_up"], out_dtype=F32, name=f"{tag}_d_b")
        dh_mid, dh_mid_b, small["ffn_norm"][i] = _rms_bwd(
            sv["h_mid"], norms["ffn_norm"][i : i + 1], db, dh, name=f"{tag}_ffn_norm"
        )
        do = _mm_nt(dh_mid_b, lw["w_out"], out_dtype=BF16, name=f"{tag}_d_o", after=(_token(dw_up),))
        dw_out = _mm_tn(sv["o"], dh_mid_b, 1, name=f"{tag}_dw_out")
        if kind == 0:
            dq, dk, dv, dblock = _na_bwd(sv["qkv"], sv["bias"], do, name=f"{tag}_na_bwd")
            dbias = _na_unblock(dblock, s // GRID_W, name=f"{tag}_na_unblock")
            dqkv = jnp.concatenate([dq, dk, dv], axis=1)
            small["na_rpb"][j] = _rpb_fold(dbias, name=f"{tag}_rpb_fold")
        elif kind == 1:
            dgb, dgc, dhh, small["sc_conv_w"] = _sc_mid_bwd(sv["qkv"], lw["sc_conv_w"], do, name=f"{tag}_sc_bwd")
            dqkv = jnp.concatenate([dgb, dgc, dhh], axis=1)
        else:
            gq, gk = norms["gqa_q_norm"][j : j + 1], norms["gqa_k_norm"][j : j + 1]
            dqn, dkn, dv = _gqa_bwd(sv["qkn"], sv["qkv"], do, heads, hkv, name=f"{tag}_gqa_bwd")
            dqk, small["gqa_q_norm"], small["gqa_k_norm"] = _gqa_prep_bwd(
                sv["qkv"], gq, gk, cos, sin, dqn, dkn, heads, hkv, name=f"{tag}_gqa_prep_bwd"
            )
            dqkv = jnp.concatenate([dqk, dv], axis=1)
        dw_in = _mm_tn(sv["a"], dqkv, N_CHIPS, name=f"{tag}_dw_in")
        da = _mm_nt(dqkv, lw["w_in"], out_dtype=F32, name=f"{tag}_d_a")
        dh, dh_b, small["mix_norm"][i] = _rms_bwd(
            sv["h_in"], norms["mix_norm"][i : i + 1], da, dh_mid, name=f"{tag}_mix_norm"
        )
        sent_mix = send(i, "mix", [dw_in, dw_out.reshape(N_CHIPS, dw_out.shape[1] // N_CHIPS, d)])
        after = (_token(dw_in), _token(dw_out))
        big[i] = {"mix": sent_mix, "up": sent_up, "down": sent_down}
    return loss, dh, big, small


def _pack(parts):
    flat = jnp.concatenate([p.reshape(-1).astype(F32) for p in parts])
    pad = (-flat.shape[0]) % (8 * LANES)
    return jnp.pad(flat, (0, pad)).reshape(-1, LANES)


def _unpack(buf, shapes):
    flat = buf.reshape(-1)
    out, at = [], 0
    for shp in shapes:
        size = 1
        for n in shp:
            size *= n
        out.append(flat[at : at + size].reshape(shp))
        at += size
    return out


def kernel(x, mix_norm, ffn_norm, final_norm, na_w_qkv, na_rpb, na_w_o, sc_w_in, sc_conv_w, sc_w_out, gqa_w_qkv, gqa_q_norm, gqa_k_norm, gqa_w_o, ffn_w_up, ffn_conv_w, ffn_conv_b, ffn_w_down, loss_target, m_mix_norm, m_ffn_norm, m_final_norm, m_na_w_qkv, m_na_rpb, m_na_w_o, m_sc_w_in, m_sc_conv_w, m_sc_w_out, m_gqa_w_qkv, m_gqa_q_norm, m_gqa_k_norm, m_gqa_w_o, m_ffn_w_up, m_ffn_conv_w, m_ffn_conv_b, m_ffn_w_down, v_mix_norm, v_ffn_norm, v_final_norm, v_na_w_qkv, v_na_rpb, v_na_w_o, v_sc_w_in, v_sc_conv_w, v_sc_w_out, v_gqa_w_qkv, v_gqa_q_norm, v_gqa_k_norm, v_gqa_w_o, v_ffn_w_up, v_ffn_conv_w, v_ffn_conv_b, v_ffn_w_down):
    depth, d = mix_norm.shape
    chip = 2 * lax.axis_index("x") + lax.axis_index("y")
    w_in_of = {0: na_w_qkv, 1: sc_w_in, 2: gqa_w_qkv}
    w_out_of = {0: na_w_o, 1: sc_w_out, 2: gqa_w_o}

    layers = []
    for i in range(depth):
        kind, j = _mixer_of(i)
        shards = [_cast_slab(w_in_of[kind], j, name=f"cast_w_in_l{i}"), _cast_slab(w_out_of[kind], j, name=f"cast_w_out_l{i}")]
        split = [True, True]
        if kind == 1:
            shards.append(sc_conv_w[j])
            split.append(False)
        mix = _gather_shards(shards, split, name=f"gather_mix_l{i}", collective_id=1 + 2 * i)
        ffn = _gather_shards(
            [_cast_slab(ffn_w_up, i, name=f"cast_w_up_l{i}"), _cast_slab(ffn_w_down, i, name=f"cast_w_down_l{i}"),
             ffn_conv_w[i]],
            [True, True, False],
            name=f"gather_ffn_l{i}",
            collective_id=2 + 2 * i,
        )
        lw = {
            "w_in": mix[0],
            "w_out": mix[1].reshape(1, -1, d),
            "w_up": ffn[0],
            "w_down": ffn[1].reshape(1, -1, d),
            "ffn_conv_w": ffn[2],
            "ffn_conv_b": ffn_conv_b[i : i + 1],
        }
        if kind == 1:
            lw["sc_conv_w"] = mix[2]
        layers.append(lw)

    norms = dict(mix_norm=mix_norm, ffn_norm=ffn_norm, final_norm=final_norm, na_rpb=na_rpb,
                 gqa_q_norm=gqa_q_norm, gqa_k_norm=gqa_k_norm)

    parts = ("down", "up", "mix")

    def send(i, part, pieces):
        cid = 1 + 2 * depth + len(parts) * i + parts.index(part)
        return _scatter_pieces(pieces, name=f"scatter_{part}_l{i}", collective_id=cid)

    loss, grad_x, big, small = _forward_backward(x[0], loss_target[0], norms, layers, send)

    mixer_names = {0: ("na_w_qkv", "na_w_o"), 1: ("sc_w_in", "sc_w_out"), 2: ("gqa_w_qkv", "gqa_w_o")}
    state = {
        "na_w_qkv": (na_w_qkv, m_na_w_qkv, v_na_w_qkv), "na_w_o": (na_w_o, m_na_w_o, v_na_w_o),
        "sc_w_in": (sc_w_in, m_sc_w_in, v_sc_w_in), "sc_w_out": (sc_w_out, m_sc_w_out, v_sc_w_out),
        "gqa_w_qkv": (gqa_w_qkv, m_gqa_w_qkv, v_gqa_w_qkv), "gqa_w_o": (gqa_w_o, m_gqa_w_o, v_gqa_w_o),
        "ffn_w_up": (ffn_w_up, m_ffn_w_up, v_ffn_w_up), "ffn_w_down": (ffn_w_down, m_ffn_w_down, v_ffn_w_down),
    }
    res = {n: None for n in state}
    token = None
    for i in reversed(range(depth)):
        kind, j = _mixer_of(i)
        for part, names, slab in (("down", ("ffn_w_down",), i), ("up", ("ffn_w_up",), i), ("mix", mixer_names[kind], j)):
            own, sib = big[i][part]
            for slot, n in enumerate(names):
                res[n] = _adamw_slab(*state[n], own[slot], sib[slot], slab, res[n], token, name=f"adamw_{n}_l{i}")
                token = res[n][0][slab, :16, :LANES]

    n_na = na_rpb.shape[0]
    rpb_flat = jnp.stack([small["na_rpb"][j] for j in range(n_na)])
    full_parts = [
        loss[:, :1],
        jnp.concatenate(small["mix_norm"], axis=0),
        jnp.concatenate(small["ffn_norm"], axis=0),
        small["final_norm"],
        rpb_flat,
        small["sc_conv_w"],
        small["gqa_q_norm"],
        small["gqa_k_norm"],
        jnp.stack(small["ffn_conv_w"]),
        jnp.concatenate(small["ffn_conv_b"], axis=0),
    ]
    summed = _unpack(_allreduce_small(_pack(full_parts), name="allreduce_small"), [p.shape for p in full_parts])
    loss_all, g_mix, g_ffn, g_final, g_rpb, g_sc_cw, g_gq, g_gk, g_ffn_cw, g_ffn_cb = summed
    g_rpb = g_rpb[:, :, : RPB_ROWS * RPB_COLS].reshape(na_rpb.shape)
    g_sc_cw = lax.dynamic_slice_in_dim(g_sc_cw, chip * sc_conv_w.shape[2], sc_conv_w.shape[2], axis=1)[None]
    g_ffn_cw = lax.dynamic_slice_in_dim(g_ffn_cw, chip * ffn_conv_w.shape[2], ffn_conv_w.shape[2], axis=2)
    small_names = ["mix_norm", "ffn_norm", "final_norm", "na_rpb", "sc_conv_w", "gqa_q_norm", "gqa_k_norm",
                   "ffn_conv_w", "ffn_conv_b"]
    small_g = [g_mix, g_ffn, g_final.reshape(final_norm.shape), g_rpb, g_sc_cw, g_gq, g_gk, g_ffn_cw, g_ffn_cb]
    small_w = [mix_norm, ffn_norm, final_norm, na_rpb, sc_conv_w, gqa_q_norm, gqa_k_norm, ffn_conv_w, ffn_conv_b]
    small_m = [m_mix_norm, m_ffn_norm, m_final_norm, m_na_rpb, m_sc_conv_w, m_gqa_q_norm, m_gqa_k_norm,
               m_ffn_conv_w, m_ffn_conv_b]
    small_v = [v_mix_norm, v_ffn_norm, v_final_norm, v_na_rpb, v_sc_conv_w, v_gqa_q_norm, v_gqa_k_norm,
               v_ffn_conv_w, v_ffn_conv_b]
    shapes = [w.shape for w in small_w]
    packed = _adamw_small(_pack(small_w), _pack(small_g), _pack(small_m), _pack(small_v), name="adamw_small")
    small_d, small_nm, small_nv = (_unpack(p, shapes) for p in packed)
    for n, g, dl, nm, nv in zip(small_names, small_g, small_d, small_nm, small_nv):
        res[n] = (g.reshape(dl.shape), dl, nm, nv)

    order = ["mix_norm", "ffn_norm", "final_norm", "na_w_qkv", "na_rpb", "na_w_o", "sc_w_in", "sc_conv_w",
             "sc_w_out", "gqa_w_qkv", "gqa_q_norm", "gqa_k_norm", "gqa_w_o", "ffn_w_up", "ffn_conv_w",
             "ffn_conv_b", "ffn_w_down"]
    outs = [loss_all.reshape(()), grad_x[None]]
    for part in range(4):
        outs.extend(res[n][part] for n in order)
    return tuple(outs)
```

```python
import functools
import math

import jax
import jax.numpy as jnp
from jax import lax
from jax.experimental import pallas as pl
from jax.experimental.pallas import tpu as pltpu
from jax.experimental.pallas import tpu_sc as plsc

F32 = jnp.float32
BF16 = jnp.bfloat16
MESH = pl.DeviceIdType.MESH

N_CHIPS = 4
N_DEV = 8
N_MIXERS = 3
GRID_W = 64
HEAD_DIM = 128
EPS = 1e-6
NEG_INF = -1e30
NA_WIN_R = 8
NA_WIN_C = 16
GQA_GROUP = 4
ROPE_THETA = 10000.0
ADAM_LR = 0.001
ADAM_B1 = 0.9
ADAM_B2 = 0.999
ADAM_EPS = 1e-08
ADAM_WD = 0.01
ADAM_STEP = 10

LANES = 128
VMEM_LIMIT = 56 * 1024 * 1024
NT_DIMS = (((1,), (1,)), ((), ()))
TN_DIMS = (((0,), (0,)), ((), ()))


def _pick(n, cap, mult=LANES):
    best = None
    for t in range(mult, min(n, cap) + 1, mult):
        if n % t == 0:
            best = t
    return best if best is not None else n


def _params(*sem):
    return pltpu.CompilerParams(dimension_semantics=sem, vmem_limit_bytes=VMEM_LIMIT)


MM_VMEM_BUDGET = 47 * 1024 * 1024
MM_CONTRACT = 2816


def _mm_rows(m, blocks_for, mult=16):
    for cap in (1024, 512, 256, 128):
        tm = _pick(m, cap, mult)
        if sum(r * c * b * n for r, c, b, n in blocks_for(tm)) <= MM_VMEM_BUDGET:
            return tm
    return _pick(m, 128, mult)


def _accumulate(acc, step, steps, part, finish):
    if steps == 1:
        finish(part)
        return

    @pl.when(step == 0)
    def _():
        acc[...] = part

    @pl.when(step != 0)
    def _():
        acc[...] += part

    @pl.when(step == steps - 1)
    def _():
        finish(acc[...])


def _mm_nn(a, b, *, out_dtype, name, residual=None):
    m, k = a.shape
    nc, _, ncol = b.shape
    tn, tk = _pick(ncol, 1536), _pick(k, MM_CONTRACT)
    per, nk = ncol // tn, k // tk
    osz = jnp.dtype(out_dtype).itemsize
    tm = _mm_rows(m, lambda t: [(t, tk, a.dtype.itemsize, 2), (tk, tn, 2, 2), (t, tn, osz, 2),
                                (t, tn, 4, 2 * (residual is not None)), (t, tn, 4, nk > 1)])

    def body(a_ref, b_ref, *rest):
        o_ref = rest[-2] if nk > 1 else rest[-1]

        def finish(r):
            if residual is not None:
                r = r + rest[0][...]
            o_ref[...] = r.astype(o_ref.dtype)

        part = jnp.dot(a_ref[...].astype(BF16), b_ref[...], preferred_element_type=F32)
        _accumulate(rest[-1], pl.program_id(2), nk, part, finish)

    in_specs = [
        pl.BlockSpec((tm, tk), lambda i, j, kk: (i, kk)),
        pl.BlockSpec((None, tk, tn), lambda i, j, kk: (j // per, kk, j % per)),
    ]
    ops = [a, b]
    if residual is not None:
        in_specs.append(pl.BlockSpec((tm, tn), lambda i, j, kk: (i, j)))
        ops.append(residual)
    return pl.pallas_call(
        body,
        name=name,
        grid=(m // tm, nc * per, nk),
        in_specs=in_specs,
        out_specs=pl.BlockSpec((tm, tn), lambda i, j, kk: (i, j)),
        out_shape=jax.ShapeDtypeStruct((m, nc * ncol), out_dtype),
        scratch_shapes=[pltpu.VMEM((tm, tn), F32)] * (nk > 1),
        compiler_params=_params("parallel", "parallel", "arbitrary"),
    )(*ops)


def _token(x):
    return x[(0,) * (x.ndim - 2) + (slice(0, 16), slice(0, LANES))]


def _mm_nt(a, b, *, out_dtype, name, after=()):
    parts = tuple(a) if isinstance(a, (tuple, list)) else (a,)
    m, width = parts[0].shape
    nc, k, ncol = b.shape
    tko, tn = _pick(k, 1024), _pick(math.gcd(ncol, width), MM_CONTRACT)
    per, each, nn = ncol // tn, width // tn, len(parts) * width // tn
    osz = jnp.dtype(out_dtype).itemsize
    tm = _mm_rows(m, lambda t: [(t, tn, parts[0].dtype.itemsize, 2 * len(parts)), (tko, tn, 2, 2), (t, tko, osz, 2),
                                (t, tko, 4, nn > 1)])

    def body(*refs):
        a_refs, b_ref, rest = refs[: len(parts)], refs[len(parts)], refs[len(parts) + 1 :]
        o_ref = rest[len(after)]
        step = pl.program_id(2)

        def finish(r):
            o_ref[...] = r.astype(o_ref.dtype)

        def use(a_ref):
            part = lax.dot_general(a_ref[...].astype(BF16), b_ref[...], NT_DIMS, preferred_element_type=F32)
            _accumulate(rest[-1], step, nn, part, finish)

        _for_part(step // each, a_refs, use)

    return pl.pallas_call(
        body,
        name=name,
        grid=(m // tm, k // tko, nn),
        in_specs=[_part_spec((tm, tn), p, each, lambda i, j, s: (i, s)) for p in range(len(parts))]
        + [pl.BlockSpec((None, tko, tn), lambda i, j, s: (s // per, j, s % per))]
        + [pl.BlockSpec(t.shape, lambda i, j, s: (0, 0)) for t in after],
        out_specs=pl.BlockSpec((tm, tko), lambda i, j, s: (i, j)),
        out_shape=jax.ShapeDtypeStruct((m, k), out_dtype),
        scratch_shapes=[pltpu.VMEM((tm, tko), F32)] * (nn > 1),
        compiler_params=_params("parallel", "parallel", "arbitrary"),
    )(*parts, b, *after)


def _for_part(which, refs, use):
    if len(refs) == 1:
        use(refs[0])
        return
    for p, ref in enumerate(refs):
        pl.when(which == p)(functools.partial(use, ref))


def _part_spec(block, p, each, tile_of):
    def index(*ids):
        r, c = tile_of(*ids)
        return r, jnp.clip(c - p * each, 0, each - 1)

    return pl.BlockSpec(block, index)


def _mm_tn(a, g, nc, *, name):
    parts = tuple(g) if isinstance(g, (tuple, list)) else (g,)
    s, k = a.shape
    width = parts[0].shape[1]
    ncol = len(parts) * width // nc
    ts, tn = _pick(s, MM_CONTRACT, 16), _pick(math.gcd(ncol, width), 1536)
    per, each, ns = ncol // tn, width // tn, s // ts
    tko = _mm_rows(k, lambda t: [(ts, t, a.dtype.itemsize, 2), (ts, tn, parts[0].dtype.itemsize, 2 * len(parts)),
                                 (t, tn, 2, 2), (t, tn, 4, ns > 1)], mult=LANES)

    def body(a_ref, *refs):
        g_refs, rest = refs[: len(parts)], refs[len(parts) :]
        o_ref = rest[0]

        def finish(r):
            o_ref[...] = r.astype(o_ref.dtype)

        def use(g_ref):
            part = lax.dot_general(a_ref[...].astype(BF16), g_ref[...].astype(BF16), TN_DIMS, preferred_element_type=F32)
            _accumulate(rest[-1], pl.program_id(2), ns, part, finish)

        _for_part(pl.program_id(1) // each, g_refs, use)

    return pl.pallas_call(
        body,
        name=name,
        grid=(k // tko, nc * per, ns),
        in_specs=[pl.BlockSpec((ts, tko), lambda i, j, t: (t, i))]
        + [_part_spec((ts, tn), p, each, lambda i, j, t: (t, j)) for p in range(len(parts))],
        out_specs=pl.BlockSpec((None, tko, tn), lambda i, j, t: (j // per, i, j % per)),
        out_shape=jax.ShapeDtypeStruct((nc, k, ncol), BF16),
        scratch_shapes=[pltpu.VMEM((tko, tn), F32)] * (ns > 1),
        compiler_params=_params("parallel", "parallel", "arbitrary"),
    )(a, *parts)


ROW_TILE = 256


def _cast_slab(w, slab, *, name):
    _, rows, cols = w.shape
    tr = _pick(rows, ROW_TILE, 16)

    def body(w_ref, o_ref):
        o_ref[...] = w_ref[...].astype(o_ref.dtype)

    return pl.pallas_call(
        body,
        name=name,
        grid=(rows // tr,),
        in_specs=[pl.BlockSpec((None, tr, cols), lambda i: (slab, i, 0))],
        out_specs=pl.BlockSpec((tr, cols), lambda i: (i, 0)),
        out_shape=jax.ShapeDtypeStruct((rows, cols), BF16),
        compiler_params=_params("parallel"),
    )(w)


def _rms_fwd(h, g, *, name):
    s, d = h.shape
    tr = _pick(s, ROW_TILE, 16)

    def body(h_ref, g_ref, o_ref):
        x = h_ref[...]
        r = lax.rsqrt(jnp.mean(x * x, axis=-1, keepdims=True) + EPS)
        o_ref[...] = (x * r * g_ref[...]).astype(o_ref.dtype)

    return pl.pallas_call(
        body,
        name=name,
        grid=(s // tr,),
        in_specs=[pl.BlockSpec((tr, d), lambda i: (i, 0)), pl.BlockSpec((1, d), lambda i: (0, 0))],
        out_specs=pl.BlockSpec((tr, d), lambda i: (i, 0)),
        out_shape=jax.ShapeDtypeStruct((s, d), BF16),
        compiler_params=_params("parallel"),
    )(h, g)


def _rms_bwd(h, g, dy, dres, *, name):
    s, d = h.shape
    tr = _pick(s, ROW_TILE, 16)

    def body(h_ref, g_ref, dy_ref, dres_ref, dh_ref, dhb_ref, dg_ref):
        x = h_ref[...]
        r = lax.rsqrt(jnp.mean(x * x, axis=-1, keepdims=True) + EPS)
        xhat = x * r
        dyv = dy_ref[...].astype(F32)
        dyg = dyv * g_ref[...]
        dx = r * (dyg - xhat * jnp.mean(dyg * xhat, axis=-1, keepdims=True))
        dh = dres_ref[...] + dx
        dh_ref[...] = dh
        dhb_ref[...] = dh.astype(dhb_ref.dtype)
        part = jnp.sum(dyv * xhat, axis=0, keepdims=True)

        @pl.when(pl.program_id(0) == 0)
        def _():
            dg_ref[...] = part

        @pl.when(pl.program_id(0) != 0)
        def _():
            dg_ref[...] += part

    row = pl.BlockSpec((tr, d), lambda i: (i, 0))
    vec = pl.BlockSpec((1, d), lambda i: (0, 0))
    return pl.pallas_call(
        body,
        name=name,
        grid=(s // tr,),
        in_specs=[row, vec, row, row],
        out_specs=[row, row, vec],
        out_shape=[jax.ShapeDtypeStruct((s, d), F32), jax.ShapeDtypeStruct((s, d), BF16),
                   jax.ShapeDtypeStruct((1, d), F32)],
        compiler_params=_params("arbitrary"),
    )(h, g, dy, dres)


def _loss_head(h, g, target, *, name):
    s, d = h.shape
    tr = _pick(s, ROW_TILE, 16)

    def body(h_ref, g_ref, t_ref, dh_ref, dhb_ref, dg_ref, loss_ref):
        x = h_ref[...]
        r = lax.rsqrt(jnp.mean(x * x, axis=-1, keepdims=True) + EPS)
        xhat = x * r
        gv = g_ref[...]
        err = xhat * gv - t_ref[...]
        dyv = err * (1.0 / d)
        dyg = dyv * gv
        dh = r * (dyg - xhat * jnp.mean(dyg * xhat, axis=-1, keepdims=True))
        dh_ref[...] = dh
        dhb_ref[...] = dh.astype(dhb_ref.dtype)
        part = jnp.sum(dyv * xhat, axis=0, keepdims=True)
        lpart =jnp.sum(jnp.sum(err * err, axis=-1, keepdims=True), axis=0, keepdims=True) * (0.5 / d)

        @pl.when(pl.program_id(0) == 0)
        def _():
            dg_ref[...] = part
            loss_ref[...] = jnp.broadcast_to(lpart, loss_ref.shape)

        @pl.when(pl.program_id(0) != 0)
        def _():
            dg_ref[...] += part
            loss_ref[...] += jnp.broadcast_to(lpart, loss_ref.shape)

    row = pl.BlockSpec((tr, d), lambda i: (i, 0))
    vec = pl.BlockSpec((1, d), lambda i: (0, 0))
    return pl.pallas_call(
        body,
        name=name,
        grid=(s // tr,),
        in_specs=[row, vec, row],
        out_specs=[row, row, vec, pl.BlockSpec((1, LANES), lambda i: (0, 0))],
        out_shape=[
            jax.ShapeDtypeStruct((s, d), F32),
            jax.ShapeDtypeStruct((s, d), BF16),
            jax.ShapeDtypeStruct((1, d), F32),
            jax.ShapeDtypeStruct((1, LANES), F32),
        ],
        compiler_params=_params("arbitrary"),
    )(h, g, target)


def _shift_prev(x):
    row = lax.broadcasted_iota(jnp.int32, x.shape, 0)
    return jnp.where(row == 0, 0.0, pltpu.roll(x, 1, 0))


def _shift_next(x):
    n = x.shape[0]
    row = lax.broadcasted_iota(jnp.int32, x.shape, 0)
    return jnp.where(row == n - 1, 0.0, pltpu.roll(x, n - 1, 0))


def _conv3(x, w):
    xm, xp = _shift_prev(x), _shift_next(x)
    return xm * w[0:1] + x * w[1:2] + xp * w[2:3], xm, xp


def _conv3_t(d, w):
    return _shift_next(d) * w[0:1] + d * w[1:2] + _shift_prev(d) * w[2:3]


def _colsum(x):
    return jnp.sum(x, axis=0, keepdims=True)


def _ffn_mid_fwd(up, cw, cb, *, name):
    s, f2 = up.shape
    f = f2 // 2
    ncol = cw.shape[2]
    tc = _pick(ncol, 256)
    nt, per = f // tc, ncol // tc

    def body(ug_ref, uu_ref, wg_ref, wu_ref, bg_ref, bu_ref, o_ref, cg_ref, cu_ref):
        cg = _conv3(ug_ref[...].astype(F32), wg_ref[...])[0] + bg_ref[...]
        cu = _conv3(uu_ref[...].astype(F32), wu_ref[...])[0] + bu_ref[...]
        o_ref[...] = (cg * (1.0 / (1.0 + jnp.exp(-cg))) * cu).astype(o_ref.dtype)
        cg_ref[...] = cg.astype(cg_ref.dtype)
        cu_ref[...] = cu.astype(cu_ref.dtype)

    out = pl.BlockSpec((s, tc), lambda j: (0, j))
    return pl.pallas_call(
        body,
        name=name,
        grid=(nt,),
        in_specs=[
            pl.BlockSpec((s, tc), lambda j: (0, j)),
            pl.BlockSpec((s, tc), lambda j: (0, nt + j)),
            pl.BlockSpec((None, 3, tc), lambda j: (j // per, 0, j % per)),
            pl.BlockSpec((None, 3, tc), lambda j: ((nt + j) // per, 0, (nt + j) % per)),
            pl.BlockSpec((1, tc), lambda j: (0, j)),
            pl.BlockSpec((1, tc), lambda j: (0, nt + j)),
        ],
        out_specs=[out, out, out],
        out_shape=[jax.ShapeDtypeStruct((s, f), BF16)] * 3,
        compiler_params=_params("parallel"),
    )(up, up, cw, cw, cb, cb)


def _ffn_mid_bwd(up, cg, cu, cw, dact, *, name):
    s, f2 = up.shape
    f = f2 // 2
    ncol = cw.shape[2]
    tc = _pick(ncol, 256)
    nt, per = f // tc, ncol // tc

    def side(dc, u_ref, w_ref, du_ref, dw_ref, db_ref):
        w, u = w_ref[...], u_ref[...].astype(F32)
        nxt, prv = _shift_next(dc), _shift_prev(dc)
        du_ref[...] = (nxt * w[0:1] + dc * w[1:2] + prv * w[2:3]).astype(du_ref.dtype)
        dw_ref[0:1, :] = _colsum(nxt * u)
        dw_ref[1:2, :] = _colsum(dc * u)
        dw_ref[2:3, :] = _colsum(prv * u)
        db_ref[...] = _colsum(dc)

    def body(ug_ref, uu_ref, cg_ref, cu_ref, wg_ref, wu_ref, da_ref, dug_ref, duu_ref, dwg_ref, dwu_ref, dbg_ref, dbu_ref):
        cgv, cuv, da = cg_ref[...].astype(F32), cu_ref[...].astype(F32), da_ref[...].astype(F32)
        sig = 1.0 / (1.0 + jnp.exp(-cgv))
        side(da * cuv * (sig * (1.0 + cgv * (1.0 - sig))), ug_ref, wg_ref, dug_ref, dwg_ref, dbg_ref)
        side(da * (cgv * sig), uu_ref, wu_ref, duu_ref, dwu_ref, dbu_ref)

    col = pl.BlockSpec((s, tc), lambda j: (0, j))
    w3 = pl.BlockSpec((3, tc), lambda j: (0, j))
    b1 = pl.BlockSpec((1, tc), lambda j: (0, j))
    return pl.pallas_call(
        body,
        name=name,
        grid=(nt,),
        in_specs=[
            col,
            pl.BlockSpec((s, tc), lambda j: (0, nt + j)),
            col,
            col,
            pl.BlockSpec((None, 3, tc), lambda j: (j // per, 0, j % per)),
            pl.BlockSpec((None, 3, tc), lambda j: ((nt + j) // per, 0, (nt + j) % per)),
            col,
        ],
        out_specs=[col, col, w3, w3, b1, b1],
        out_shape=[
            jax.ShapeDtypeStruct((s, f), BF16),
            jax.ShapeDtypeStruct((s, f), BF16),
            jax.ShapeDtypeStruct((3, f), F32),
            jax.ShapeDtypeStruct((3, f), F32),
            jax.ShapeDtypeStruct((1, f), F32),
            jax.ShapeDtypeStruct((1, f), F32),
        ],
        compiler_params=_params("parallel"),
    )(up, up, cg, cu, cw, cw, dact)


def _sc_mid_fwd(z, cw, *, name):
    s, d3 = z.shape
    d = d3 // 3
    ncol = cw.shape[2]
    tc = _pick(ncol, 256)
    nt, per = d // tc, ncol // tc

    def body(gb_ref, gc_ref, hh_ref, w_ref, o_ref):
        p = gc_ref[...].astype(F32) * hh_ref[...].astype(F32)
        o_ref[...] = (gb_ref[...].astype(F32) * _conv3(p, w_ref[...])[0]).astype(o_ref.dtype)

    return pl.pallas_call(
        body,
        name=name,
        grid=(nt,),
        in_specs=[
            pl.BlockSpec((s, tc), lambda j: (0, j)),
            pl.BlockSpec((s, tc), lambda j: (0, nt + j)),
            pl.BlockSpec((s, tc), lambda j: (0, 2 * nt + j)),
            pl.BlockSpec((None, 3, tc), lambda j: (j // per, 0, j % per)),
        ],
        out_specs=pl.BlockSpec((s, tc), lambda j: (0, j)),
        out_shape=jax.ShapeDtypeStruct((s, d), BF16),
        compiler_params=_params("parallel"),
    )(z, z, z, cw)


def _sc_mid_bwd(z, cw, dmid, *, name):
    s, d3 = z.shape
    d = d3 // 3
    ncol = cw.shape[2]
    tc = _pick(ncol, 256)
    nt, per = d // tc, ncol // tc

    def body(gb_ref, gc_ref, hh_ref, w_ref, dm_ref, dgb_ref, dgc_ref, dhh_ref, dw_ref):
        gb, gc, hh = gb_ref[...].astype(F32), gc_ref[...].astype(F32), hh_ref[...].astype(F32)
        w = w_ref[...]
        p = gc * hh
        cv, pm, pp = _conv3(p, w)
        dm = dm_ref[...].astype(F32)
        dgb_ref[...] = (dm * cv).astype(dgb_ref.dtype)
        dcv = dm * gb
        dp = _conv3_t(dcv, w)
        dgc_ref[...] = (dp * hh).astype(dgc_ref.dtype)
        dhh_ref[...] = (dp * gc).astype(dhh_ref.dtype)
        dw_ref[0:1, :] = _colsum(dcv * pm)
        dw_ref[1:2, :] = _colsum(dcv * p)
        dw_ref[2:3, :] = _colsum(dcv * pp)

    col = pl.BlockSpec((s, tc), lambda j: (0, j))
    return pl.pallas_call(
        body,
        name=name,
        grid=(nt,),
        in_specs=[
            col,
            pl.BlockSpec((s, tc), lambda j: (0, nt + j)),
            pl.BlockSpec((s, tc), lambda j: (0, 2 * nt + j)),
            pl.BlockSpec((None, 3, tc), lambda j: (j // per, 0, j % per)),
            col,
        ],
        out_specs=[col, col, col, pl.BlockSpec((3, tc), lambda j: (0, j))],
        out_shape=[jax.ShapeDtypeStruct((s, d), BF16)] * 3 + [jax.ShapeDtypeStruct((3, d), F32)],
        compiler_params=_params("parallel"),
    )(z, z, z, cw, dmid)


NA_KEYS = NA_WIN_R * GRID_W


def _na_row_start(r, rows):
    return jnp.clip(r - NA_WIN_R // 2, 0, rows - NA_WIN_R)


def _na_base(rpb):
    h = rpb.shape[0]
    pos, neg = rpb[:, :, NA_WIN_C - 1:], rpb[:, :, : NA_WIN_C - 1]
    zeros = jnp.zeros((h, NA_WIN_R, GRID_W - 2 * NA_WIN_C + 1), F32)
    out = []
    for first in range(NA_WIN_R):
        p = pos[:, first : first + NA_WIN_R]
        n = jnp.roll(neg[:, first : first + NA_WIN_R], -1, axis=1)
        out.append(jnp.concatenate([p, zeros, n], axis=-1).reshape(h, 1, NA_KEYS))
    return jnp.stack(out, axis=1)


def _skew_right(x):
    return pltpu.roll(x, 0, 1, stride=1, stride_axis=0)


def _skew_left(x):
    n = x.shape[1]
    row = lax.broadcasted_iota(jnp.int32, x.shape, 0)
    for b in range(GRID_W.bit_length() - 1):
        x = jnp.where(((row >> b) & 1) == 1, pltpu.roll(x, n - (1 << b), 1), x)
    return x


def _na_bias(base, *, name):
    h = base.shape[0]

    def body(b_ref, o_ref):
        q = lax.broadcasted_iota(jnp.int32, (GRID_W, NA_KEYS), 0)
        kc = lax.broadcasted_iota(jnp.int32, (GRID_W, NA_KEYS), 1) % GRID_W
        start = jnp.clip(q - NA_WIN_C // 2, 0, GRID_W - NA_WIN_C)
        inside = (kc >= start) & (kc < start + NA_WIN_C)
        for slot in range(NA_WIN_R):
            x = _skew_right(jnp.broadcast_to(b_ref[slot], (GRID_W, NA_KEYS)))
            o_ref[slot] = jnp.where(inside, x, NEG_INF)

    return pl.pallas_call(
        body,
        name=name,
        grid=(h,),
        in_specs=[pl.BlockSpec((None, NA_WIN_R, 1, NA_KEYS), lambda i: (i, 0, 0, 0))],
        out_specs=pl.BlockSpec((None, NA_WIN_R, GRID_W, NA_KEYS), lambda i: (i, 0, 0, 0)),
        out_shape=jax.ShapeDtypeStruct((h, NA_WIN_R, GRID_W, NA_KEYS), F32),
        compiler_params=_params("parallel"),
    )(base)


NA_STEP_ROWS = 4
NA_BLOCK_ROWS = NA_STEP_ROWS + NA_WIN_R
NA_STEP_Q = NA_STEP_ROWS * GRID_W
NA_BLOCK_K = NA_BLOCK_ROWS * GRID_W
NA_PATTERNS = 3


def _na_plan(rows):
    steps = rows // NA_STEP_ROWS

    def start(r):
        return min(max(r - NA_WIN_R // 2, 0), rows - NA_WIN_R)

    plan = []
    for t in range(steps):
        first = [start(NA_STEP_ROWS * t + i) for i in range(NA_STEP_ROWS)]
        block = min(first[0], rows - NA_BLOCK_ROWS)
        offsets = tuple(f - block for f in first)
        slots = tuple(f - (NA_STEP_ROWS * t + i) + NA_WIN_R - 1 for i, f in enumerate(first))
        plan.append((offsets, slots))
    middle = plan[1] if steps > 2 else (tuple(range(NA_STEP_ROWS)), (NA_WIN_R // 2 - 1,) * NA_STEP_ROWS)
    assert steps >= 2 and all(p == middle for p in plan[1:-1])
    return steps, (plan[0], middle, plan[-1])


def _na_block_start(t, rows):
    return jnp.minimum(_na_row_start(NA_STEP_ROWS * t, rows), rows - NA_BLOCK_ROWS)


def _na_pattern(t, steps):
    return jnp.where(t == 0, 0, jnp.where(t == steps - 1, 2, 1))


def _na_block_bias(bias, rows, *, name):
    h = bias.shape[0]
    _, patterns = _na_plan(rows)

    def body(b_ref, o_ref):
        outside = jnp.full((GRID_W, NA_BLOCK_K - NA_KEYS), NEG_INF, F32)
        for p, (offsets, slots) in enumerate(patterns):

            @pl.when(pl.program_id(1) == p)
            def _(offsets=offsets, slots=slots):
                for i in range(NA_STEP_ROWS):
                    wide = jnp.concatenate([b_ref[slots[i]], outside], axis=1)
                    shift = offsets[i] * GRID_W
                    o_ref[i * GRID_W : (i + 1) * GRID_W, :] = pltpu.roll(wide, shift, 1) if shift else wide

    return pl.pallas_call(
        body,
        name=name,
        grid=(h, NA_PATTERNS),
        in_specs=[pl.BlockSpec((None, NA_WIN_R, GRID_W, NA_KEYS), lambda i, p: (i, 0, 0, 0))],
        out_specs=pl.BlockSpec((None, None, NA_STEP_Q, NA_BLOCK_K), lambda i, p: (i, p, 0, 0)),
        out_shape=jax.ShapeDtypeStruct((h, NA_PATTERNS, NA_STEP_Q, NA_BLOCK_K), F32),
        compiler_params=_params("parallel", "arbitrary"),
    )(bias)


def _na_unblock(dblock, rows, *, name):
    h = dblock.shape[0]
    steps, patterns = _na_plan(rows)
    used = [p for p in range(NA_PATTERNS) if p != 1 or steps > 2]

    def body(d_ref, o_ref):
        total = {}
        for p in used:
            offsets, slots = patterns[p]
            for i in range(NA_STEP_ROWS):
                wide = d_ref[p, i * GRID_W : (i + 1) * GRID_W, :]
                back = (NA_BLOCK_K - offsets[i] * GRID_W) % NA_BLOCK_K
                piece = (pltpu.roll(wide, back, 1) if back else wide)[:, :NA_KEYS]
                total[slots[i]] = piece if slots[i] not in total else total[slots[i]] + piece
        for slot in range(NA_WIN_R):
            o_ref[slot] = total.get(slot, jnp.zeros((GRID_W, NA_KEYS), F32))

    return pl.pallas_call(
        body,
        name=name,
        grid=(h,),
        in_specs=[pl.BlockSpec((None, NA_PATTERNS, NA_STEP_Q, NA_BLOCK_K), lambda i: (i, 0, 0, 0))],
        out_specs=pl.BlockSpec((None, NA_WIN_R, GRID_W, NA_KEYS), lambda i: (i, 0, 0, 0)),
        out_shape=jax.ShapeDtypeStruct((h, NA_WIN_R, GRID_W, NA_KEYS), F32),
        compiler_params=_params("parallel"),
    )(dblock)


def _na_specs(s, heads, rows, steps):
    q = pl.BlockSpec((NA_STEP_Q, HEAD_DIM), lambda h, t: (t, h))
    k = pl.BlockSpec((s, HEAD_DIM), lambda h, t: (0, heads + h))
    v = pl.BlockSpec((s, HEAD_DIM), lambda h, t: (0, 2 * heads + h))
    bias = pl.BlockSpec((None, None, NA_STEP_Q, NA_BLOCK_K), lambda h, t: (h, _na_pattern(t, steps), 0, 0))
    return q, k, v, bias


def _na_probs(q, k, bias):
    sc = lax.dot_general(q, k, NT_DIMS, preferred_element_type=F32) * (HEAD_DIM ** -0.5) + bias
    p = jnp.exp(sc - jnp.max(sc, axis=-1, keepdims=True))
    return p, jnp.sum(p, axis=-1, keepdims=True)


def _na_fwd(qkv, block_bias, *, name):
    s = qkv.shape[0]
    heads = qkv.shape[1] // (3 * HEAD_DIM)
    rows = s // GRID_W
    steps, _ = _na_plan(rows)

    per = 2 if heads % 2 == 0 else 1
    width = per * HEAD_DIM

    def body(q_ref, k_ref, v_ref, b_ref, o_ref):
        start = _na_block_start(pl.program_id(1), rows)
        block = pl.ds(pl.multiple_of(start * GRID_W, GRID_W), NA_BLOCK_K)
        for g in range(per):
            cols = slice(g * HEAD_DIM, (g + 1) * HEAD_DIM)
            p, l = _na_probs(q_ref[:, cols], k_ref[block, cols], b_ref[g])
            o = jnp.dot(p.astype(BF16), v_ref[block, cols], preferred_element_type=F32)
            o_ref[:, cols] = (o / l).astype(o_ref.dtype)

    q = pl.BlockSpec((NA_STEP_Q, width), lambda h, t: (t, h))
    k = pl.BlockSpec((s, width), lambda h, t: (0, heads // per + h))
    v = pl.BlockSpec((s, width), lambda h, t: (0, 2 * heads // per + h))
    b = pl.BlockSpec((per, None, NA_STEP_Q, NA_BLOCK_K), lambda h, t: (h, _na_pattern(t, steps), 0, 0))
    return pl.pallas_call(
        body,
        name=name,
        grid=(heads // per, steps),
        in_specs=[q, k, v, b],
        out_specs=q,
        out_shape=jax.ShapeDtypeStruct((s, heads * HEAD_DIM), BF16),
        compiler_params=_params("parallel", "arbitrary"),
    )(qkv, qkv, qkv, block_bias)


def _na_bwd(qkv, block_bias, dout, *, name):
    s = qkv.shape[0]
    heads = qkv.shape[1] // (3 * HEAD_DIM)
    rows = s // GRID_W
    steps, _ = _na_plan(rows)
    scale = HEAD_DIM ** -0.5

    def body(q_ref, k_ref, v_ref, b_ref, do_ref, dq_ref, dk_ref, dv_ref, db_ref, dk_acc, dv_acc):
        step = pl.program_id(1)

        @pl.when(step == 0)
        def _():
            dk_acc[...] = jnp.zeros_like(dk_acc)
            dv_acc[...] = jnp.zeros_like(dv_acc)

        block = pl.ds(pl.multiple_of(_na_block_start(step, rows) * GRID_W, GRID_W), NA_BLOCK_K)
        q, k, v, do = q_ref[...], k_ref[block, :], v_ref[block, :], do_ref[...]
        p, l = _na_probs(q, k, b_ref[...])
        pn = p / l
        dp = lax.dot_general(do, v, NT_DIMS, preferred_element_type=F32)
        ds = pn * (dp - jnp.sum(pn * dp, axis=-1, keepdims=True))
        dsb = ds.astype(BF16)
        dq_ref[...] = (jnp.dot(dsb, k, preferred_element_type=F32) * scale).astype(dq_ref.dtype)
        dk_acc[block, :] += lax.dot_general(dsb, q, TN_DIMS, preferred_element_type=F32) * scale
        dv_acc[block, :] += lax.dot_general(pn.astype(BF16), do, TN_DIMS, preferred_element_type=F32)
        opens = (step <= 1) | (step == steps - 1)

        @pl.when(opens)
        def _():
            db_ref[...] = ds

        @pl.when(jnp.logical_not(opens))
        def _():
            db_ref[...] += ds

        @pl.when(step == steps - 1)
        def _():
            dk_ref[...] = dk_acc[...].astype(dk_ref.dtype)
            dv_ref[...] = dv_acc[...].astype(dv_ref.dtype)

    q, k, v, b = _na_specs(s, heads, rows, steps)
    kv_out = pl.BlockSpec((s, HEAD_DIM), lambda h, t: (0, h))
    shape = jax.ShapeDtypeStruct((s, heads * HEAD_DIM), BF16)
    return pl.pallas_call(
        body,
        name=name,
        grid=(heads, steps),
        in_specs=[q, k, v, b, q],
        out_specs=[q, kv_out, kv_out, b],
        out_shape=[shape, shape, shape, jax.ShapeDtypeStruct((heads, NA_PATTERNS, NA_STEP_Q, NA_BLOCK_K), F32)],
        scratch_shapes=[pltpu.VMEM((s, HEAD_DIM), F32), pltpu.VMEM((s, HEAD_DIM), F32)],
        compiler_params=_params("parallel", "arbitrary"),
    )(qkv, qkv, qkv, block_bias, dout)


RPB_ROWS = 2 * NA_WIN_R - 1
RPB_COLS = 2 * NA_WIN_C - 1
RPB_PAD = 512


def _rpb_fold_matrix():
    idx = jnp.arange(NA_WIN_R * NA_KEYS, dtype=jnp.int32)
    first, i, kc = idx // NA_KEYS, (idx // GRID_W) % NA_WIN_R, idx % GRID_W
    pos, neg = kc < NA_WIN_C, kc >= GRID_W - NA_WIN_C + 1
    dr = jnp.where(pos, first + i, first + (i + 1) % NA_WIN_R)
    dc = jnp.where(pos, kc + NA_WIN_C - 1, kc - (GRID_W - NA_WIN_C + 1))
    target = jnp.where(pos | neg, dr * RPB_COLS + dc, -1)
    return (target[:, None] == jnp.arange(RPB_PAD, dtype=jnp.int32)[None, :]).astype(F32)


def _rpb_fold(dbias, *, name):
    h = dbias.shape[0]

    def skew_body(g_ref, o_ref):
        for slot in range(NA_WIN_R):
            o_ref[slot] = _colsum(_skew_left(g_ref[slot]))

    skewed = pl.pallas_call(
        skew_body,
        name=name + "_skew",
        grid=(h,),
        in_specs=[pl.BlockSpec((None, NA_WIN_R, GRID_W, NA_KEYS), lambda i: (i, 0, 0, 0))],
        out_specs=pl.BlockSpec((None, NA_WIN_R, 1, NA_KEYS), lambda i: (i, 0, 0, 0)),
        out_shape=jax.ShapeDtypeStruct((h, NA_WIN_R, 1, NA_KEYS), F32),
        compiler_params=_params("parallel"),
    )(dbias)

    def fold_body(g_ref, m_ref, o_ref):
        o_ref[...] = jnp.dot(g_ref[...], m_ref[...], preferred_element_type=F32, precision=lax.Precision.HIGHEST)

    return pl.pallas_call(
        fold_body,
        name=name,
        out_shape=jax.ShapeDtypeStruct((h, RPB_PAD), F32),
        compiler_params=pltpu.CompilerParams(vmem_limit_bytes=VMEM_LIMIT),
    )(skewed.reshape(h, NA_WIN_R * NA_KEYS), _rpb_fold_matrix())


GQA_Q_TILE = 256
GQA_Q_TILE_BWD = 512


def _rope_tables(s):
    t = jnp.arange(s)
    row = (t // GRID_W).astype(F32)[:, None]
    col = (t % GRID_W).astype(F32)[:, None]
    half = HEAD_DIM // 2
    inv = ROPE_THETA ** (-jnp.arange(0, half, 2, dtype=F32) / half)
    ang = jnp.concatenate([row * inv, row * inv, col * inv, col * inv], axis=-1)
    return jnp.cos(ang), jnp.sin(ang)


def _rot_half(y):
    quarter = HEAD_DIM // 4
    lane = lax.broadcasted_iota(jnp.int32, y.shape, 1)
    low = (lane % (2 * quarter)) < quarter
    return jnp.where(low, -pltpu.roll(y, HEAD_DIM - quarter, 1), pltpu.roll(y, quarter, 1))


def _gqa_prep_fwd(qkv, gq, gk, cos, sin, hq, hkv, *, name):
    s = qkv.shape[0]

    def body(x_ref, gq_ref, gk_ref, cos_ref, sin_ref, o_ref):
        isq = pl.program_id(0) < hq
        x = x_ref[...].astype(F32)
        g = jnp.where(isq, gq_ref[...], gk_ref[...])
        y = x * lax.rsqrt(jnp.mean(x * x, axis=-1, keepdims=True) + EPS) * g
        z = y * cos_ref[...] + _rot_half(y) * sin_ref[...]
        o_ref[...] = (z * jnp.where(isq, HEAD_DIM ** -0.5, 1.0)).astype(o_ref.dtype)

    head = pl.BlockSpec((s, HEAD_DIM), lambda h: (0, h))
    vec = pl.BlockSpec((1, HEAD_DIM), lambda h: (0, 0))
    tab = pl.BlockSpec((s, HEAD_DIM), lambda h: (0, 0))
    return pl.pallas_call(
        body,
        name=name,
        grid=(hq + hkv,),
        in_specs=[head, vec, vec, tab, tab],
        out_specs=head,
        out_shape=jax.ShapeDtypeStruct((s, (hq + hkv) * HEAD_DIM), BF16),
        compiler_params=_params("parallel"),
    )(qkv, gq, gk, cos, sin)


def _gqa_prep_bwd(qkv, gq, gk, cos, sin, dqn, dkn, hq, hkv, *, name):
    s = qkv.shape[0]

    def body(x_ref, gq_ref, gk_ref, cos_ref, sin_ref, dq_ref, dk_ref, dx_ref, dgq_ref, dgk_ref):
        hh = pl.program_id(0)
        isq = hh < hq
        x = x_ref[...].astype(F32)
        g = jnp.where(isq, gq_ref[...], gk_ref[...])
        r = lax.rsqrt(jnp.mean(x * x, axis=-1, keepdims=True) + EPS)
        xhat = x * r
        dz = jnp.where(isq, dq_ref[...].astype(F32) * (HEAD_DIM ** -0.5), dk_ref[...].astype(F32))
        dy = dz * cos_ref[...] - _rot_half(dz * sin_ref[...])
        dyg = dy * g
        dx_ref[...] = (r * (dyg - xhat * jnp.mean(dyg * xhat, axis=-1, keepdims=True))).astype(dx_ref.dtype)
        part = _colsum(dy * xhat)

        @pl.when(hh == 0)
        def _():
            dgq_ref[...] = jnp.zeros_like(dgq_ref)
            dgk_ref[...] = jnp.zeros_like(dgk_ref)

        @pl.when(isq)
        def _():
            dgq_ref[...] += part

        @pl.when(jnp.logical_not(isq))
        def _():
            dgk_ref[...] += part

    head = pl.BlockSpec((s, HEAD_DIM), lambda h: (0, h))
    vec = pl.BlockSpec((1, HEAD_DIM), lambda h: (0, 0))
    tab = pl.BlockSpec((s, HEAD_DIM), lambda h: (0, 0))
    return pl.pallas_call(
        body,
        name=name,
        grid=(hq + hkv,),
        in_specs=[
            head,
            vec,
            vec,
            tab,
            tab,
            pl.BlockSpec((s, HEAD_DIM), lambda h: (0, jnp.minimum(h, hq - 1))),
            pl.BlockSpec((s, HEAD_DIM), lambda h: (0, jnp.maximum(h - hq, 0))),
        ],
        out_specs=[head, vec, vec],
        out_shape=[
            jax.ShapeDtypeStruct((s, (hq + hkv) * HEAD_DIM), BF16),
            jax.ShapeDtypeStruct((1, HEAD_DIM), F32),
            jax.ShapeDtypeStruct((1, HEAD_DIM), F32),
        ],
        compiler_params=_params("arbitrary"),
    )(qkv, gq, gk, cos, sin, dqn, dkn)


def _gqa_fwd(qkn, qkv, hq, hkv, *, name):
    s = qkv.shape[0]
    tq = _pick(s, GQA_Q_TILE, 16)

    def body(q_ref, k_ref, v_ref, o_ref):
        k, v = k_ref[...], v_ref[...]
        for g in range(GQA_GROUP):
            cols = slice(g * HEAD_DIM, (g + 1) * HEAD_DIM)
            sc = lax.dot_general(q_ref[:, cols], k, NT_DIMS, preferred_element_type=F32)
            p = jnp.exp(sc - jnp.max(sc, axis=-1, keepdims=True))
            l = jnp.sum(p, axis=-1, keepdims=True)
            o_ref[:, cols] = (jnp.dot(p.astype(BF16), v, preferred_element_type=F32) / l).astype(o_ref.dtype)

    q = pl.BlockSpec((tq, GQA_GROUP * HEAD_DIM), lambda h, i: (i, h))
    return pl.pallas_call(
        body,
        name=name,
        grid=(hkv, s // tq),
        in_specs=[
            q,
            pl.BlockSpec((s, HEAD_DIM), lambda h, i: (0, hq + h)),
            pl.BlockSpec((s, HEAD_DIM), lambda h, i: (0, hq + hkv + h)),
        ],
        out_specs=q,
        out_shape=jax.ShapeDtypeStruct((s, hq * HEAD_DIM), BF16),
        compiler_params=_params("parallel", "parallel"),
    )(qkn, qkn, qkv)


def _gqa_bwd(qkn, qkv, dout, hq, hkv, *, name):
    s = qkv.shape[0]
    tq = _pick(s, GQA_Q_TILE_BWD, 16)
    nq = s // tq

    def body(q_ref, k_ref, v_ref, do_ref, dq_ref, dk_ref, dv_ref, dk_acc, dv_acc):
        g, i = pl.program_id(1), pl.program_id(2)

        @pl.when((g == 0) & (i == 0))
        def _():
            dk_acc[...] = jnp.zeros_like(dk_acc)
            dv_acc[...] = jnp.zeros_like(dv_acc)

        q, k, v, do = q_ref[...], k_ref[...], v_ref[...], do_ref[...]
        sc = lax.dot_general(q, k, NT_DIMS, preferred_element_type=F32)
        p = jnp.exp(sc - jnp.max(sc, axis=-1, keepdims=True))
        pn = p / jnp.sum(p, axis=-1, keepdims=True)
        dp = lax.dot_general(do, v, NT_DIMS, preferred_element_type=F32)
        dsb = (pn * (dp - jnp.sum(pn * dp, axis=-1, keepdims=True))).astype(BF16)
        dq_ref[...] = jnp.dot(dsb, k, preferred_element_type=F32).astype(dq_ref.dtype)
        dk_acc[...] += lax.dot_general(dsb, q, TN_DIMS, preferred_element_type=F32)
        dv_acc[...] += lax.dot_general(pn.astype(BF16), do, TN_DIMS, preferred_element_type=F32)

        @pl.when((g == GQA_GROUP - 1) & (i == nq - 1))
        def _():
            dk_ref[...] = dk_acc[...].astype(dk_ref.dtype)
            dv_ref[...] = dv_acc[...].astype(dv_ref.dtype)

    q = pl.BlockSpec((tq, HEAD_DIM), lambda kv, g, i: (i, kv * GQA_GROUP + g))
    kv_out = pl.BlockSpec((s, HEAD_DIM), lambda kv, g, i: (0, kv))
    return pl.pallas_call(
        body,
        name=name,
        grid=(hkv, GQA_GROUP, nq),
        in_specs=[
            q,
            pl.BlockSpec((s, HEAD_DIM), lambda kv, g, i: (0, hq + kv)),
            pl.BlockSpec((s, HEAD_DIM), lambda kv, g, i: (0, hq + hkv + kv)),
            q,
        ],
        out_specs=[q, kv_out, kv_out],
        out_shape=[
            jax.ShapeDtypeStruct((s, hq * HEAD_DIM), BF16),
            jax.ShapeDtypeStruct((s, hkv * HEAD_DIM), BF16),
            jax.ShapeDtypeStruct((s, hkv * HEAD_DIM), BF16),
        ],
        scratch_shapes=[pltpu.VMEM((s, HEAD_DIM), F32), pltpu.VMEM((s, HEAD_DIM), F32)],
        compiler_params=_params("parallel", "arbitrary", "arbitrary"),
    )(qkn, qkn, qkv, dout)


ADAM_ROWS = 128


def _adam_update(w, g, m, v):
    m = ADAM_B1 * m + (1.0 - ADAM_B1) * g
    v = ADAM_B2 * v + (1.0 - ADAM_B2) * (g * g)
    m_hat = m / (1.0 - ADAM_B1 ** ADAM_STEP)
    v_hat = v / (1.0 - ADAM_B2 ** ADAM_STEP)
    return -ADAM_LR * (m_hat / (jnp.sqrt(v_hat) + ADAM_EPS) + ADAM_WD * w), m, v


def _adamw_slab(w, m, v, own, sib, slab, prev, after, *, name):
    _, rows, cols = w.shape
    tr = _pick(rows, ADAM_ROWS, 16)
    tokens = [] if after is None else [after]

    def body(w_ref, m_ref, v_ref, own_ref, sib_ref, *rest):
        g_ref, d_ref, nm_ref, nv_ref = rest[-4:]
        g = own_ref[0].astype(F32) + sib_ref[0].astype(F32)
        for q in range(1, N_CHIPS):
            g = g + (own_ref[q].astype(F32) + sib_ref[q].astype(F32))
        g_ref[...] = g
        d_ref[...], nm_ref[...], nv_ref[...] = _adam_update(w_ref[...], g, m_ref[...], v_ref[...])

    one = pl.BlockSpec((None, tr, cols), lambda i: (slab, i, 0))
    piece = pl.BlockSpec((N_CHIPS, tr, cols), lambda i: (0, i, 0))
    carried = [] if prev is None else list(prev)
    shape = jax.ShapeDtypeStruct(w.shape, F32)
    return pl.pallas_call(
        body,
        name=name,
        grid=(rows // tr,),
        in_specs=[one] * 3
        + [piece] * 2
        + [pl.BlockSpec(memory_space=pl.ANY)] * len(carried)
        + [pl.BlockSpec(t.shape, lambda i: (0, 0)) for t in tokens],
        out_specs=[one] * 4,
        out_shape=[shape] * 4,
        input_output_aliases={5 + i: i for i in range(len(carried))},
        compiler_params=_params("parallel"),
    )(w, m, v, own, sib, *carried, *tokens)


def _adamw_small(w, g, m, v, *, name):
    def body(w_ref, g_ref, m_ref, v_ref, d_ref, nm_ref, nv_ref):
        d_ref[...], nm_ref[...], nv_ref[...] = _adam_update(w_ref[...], g_ref[...], m_ref[...], v_ref[...])

    shape = jax.ShapeDtypeStruct(w.shape, F32)
    return pl.pallas_call(
        body,
        name=name,
        out_shape=[shape] * 3,
        compiler_params=pltpu.CompilerParams(vmem_limit_bytes=VMEM_LIMIT),
    )(w, g, m, v)


def _position():
    x, y, c = lax.axis_index("x"), lax.axis_index("y"), lax.axis_index("c")
    return x, y, c, 2 * x + y


def _chip_device(chip, c):
    return (chip >> 1, chip & 1, c)


def _handshake(peers):
    barrier = pltpu.get_barrier_semaphore()
    for peer in peers:
        pl.semaphore_signal(barrier, inc=1, device_id=peer, device_id_type=MESH)
    pl.semaphore_wait(barrier, len(peers))


GATHER_CHUNKS = 2


def _gather_shards(shards, split, *, name, collective_id):
    n = len(shards)
    big = [a for a in range(n) if split[a]]
    small = [a for a in range(n) if not split[a]]
    y_nbr, x_nbr, far = 1, 2, 3

    def body(*refs):
        ins, outs = refs[:n], refs[n : 2 * n]
        near_send, near_recv, far_send, far_recv, pass_send, pass_recv, own_send, own_recv = refs[2 * n :]
        x, y, c, k = _position()
        _handshake([(x, y, 1 - c)] + [_chip_device(k ^ j, c) for j in range(1, N_CHIPS)])

        def own(a, chunk):
            r = shards[a].shape[0]
            part = pl.ds(chunk * (r // GATHER_CHUNKS), r // GATHER_CHUNKS) if split[a] else pl.ds(0, r)
            return pltpu.make_async_remote_copy(
                src_ref=ins[a].at[part],
                dst_ref=outs[a].at[k, part],
                send_sem=own_send.at[a, chunk],
                recv_sem=own_recv.at[a, chunk],
                device_id=(x, y, 1 - c),
                device_id_type=MESH,
            )

        own_copies = [own(a, ch) for a in range(n) for ch in range(GATHER_CHUNKS if split[a] else 1)]
        for cp in own_copies:
            cp.start()

        def run(core):
            sibling = (x, y, 1 - core)
            relay_from, relay_to = (x_nbr, y_nbr) if core == 0 else (y_nbr, x_nbr)

            def rows(a, which, chunk=None):
                r = shards[a].shape[0]
                if not split[a]:
                    return pl.ds(0, r)
                half = r // 2
                if chunk is None:
                    return pl.ds(which * half, half)
                return pl.ds(which * half + chunk * (half // GATHER_CHUNKS), half // GATHER_CHUNKS)

            def direct(a, mask, chunk, src_chip):
                part = rows(a, core, chunk)
                return pltpu.make_async_remote_copy(
                    src_ref=ins[a].at[part],
                    dst_ref=outs[a].at[src_chip, part],
                    send_sem=near_send.at[a, mask - 1, chunk or 0],
                    recv_sem=near_recv.at[a, mask - 1, chunk or 0],
                    device_id=_chip_device(k ^ mask, core),
                    device_id_type=MESH,
                )

            def relay(a, chunk, src_chip, mask):
                part = outs[a].at[src_chip, rows(a, core, chunk)]
                return pltpu.make_async_remote_copy(
                    src_ref=part,
                    dst_ref=part,
                    send_sem=far_send.at[a, chunk or 0],
                    recv_sem=far_recv.at[a, chunk or 0],
                    device_id=_chip_device(k ^ mask, core),
                    device_id_type=MESH,
                )

            def to_sibling(a, mask, which):
                part = outs[a].at[k ^ mask, rows(a, which)]
                return pltpu.make_async_remote_copy(
                    src_ref=part,
                    dst_ref=part,
                    send_sem=pass_send.at[a, mask - 1],
                    recv_sem=pass_recv.at[a, mask - 1],
                    device_id=sibling,
                    device_id_type=MESH,
                )

            sent = []

            def start(cp):
                cp.start()
                sent.append(cp)

            chunks = range(GATHER_CHUNKS)
            for chunk in chunks:
                for a in big:
                    start(direct(a, relay_from, chunk, k))
                    start(direct(a, relay_to, chunk, k))
            for a in small:
                start(direct(a, x_nbr, None, k))
                start(direct(a, y_nbr, None, k))
                start(pltpu.make_async_remote_copy(
                    src_ref=ins[a], dst_ref=outs[a].at[k], send_sem=far_send.at[a, 0], recv_sem=far_recv.at[a, 0],
                    device_id=_chip_device(k ^ far, core), device_id_type=MESH))
            for chunk in chunks:
                for a in big:
                    direct(a, relay_from, chunk, k ^ relay_from).wait_recv()
                    start(relay(a, chunk, k ^ relay_from, relay_to))
            for a in big:
                start(to_sibling(a, relay_from, core))
            for a in big:
                for chunk in chunks:
                    direct(a, relay_to, chunk, k ^ relay_to).wait_recv()
                start(to_sibling(a, relay_to, core))
            for a in big:
                for chunk in chunks:
                    relay(a, chunk, k ^ far, far).wait_recv()
                start(to_sibling(a, far, core))
            for a in small:
                direct(a, x_nbr, None, k ^ x_nbr).wait_recv()
                direct(a, y_nbr, None, k ^ y_nbr).wait_recv()
                relay(a, None, k ^ far, far).wait_recv()
            for a in big:
                for mask in (y_nbr, x_nbr, far):
                    to_sibling(a, mask, 1 - core).wait_recv()
            for cp in sent:
                cp.wait_send()

        for core in (0, 1):
            pl.when(c == core)(functools.partial(run, core))
        for cp in own_copies:
            cp.wait()

    return pl.kernel(
        body,
        name=name,
        out_type=[jax.ShapeDtypeStruct((N_CHIPS,) + a.shape, a.dtype) for a in shards],
        mesh=plsc.ScalarSubcoreMesh(axis_name="sequencer", num_cores=1),
        scratch_types=[
            pltpu.SemaphoreType.DMA((n, 2, GATHER_CHUNKS)),
            pltpu.SemaphoreType.DMA((n, 2, GATHER_CHUNKS)),
            pltpu.SemaphoreType.DMA((n, GATHER_CHUNKS)),
            pltpu.SemaphoreType.DMA((n, GATHER_CHUNKS)),
            pltpu.SemaphoreType.DMA((n, N_CHIPS - 1)),
            pltpu.SemaphoreType.DMA((n, N_CHIPS - 1)),
            pltpu.SemaphoreType.DMA((n, GATHER_CHUNKS)),
            pltpu.SemaphoreType.DMA((n, GATHER_CHUNKS)),
        ],
        compiler_params=pltpu.CompilerParams(collective_id=collective_id),
    )(*shards)


def _scatter_pieces(pieces, *, name, collective_id):
    n = len(pieces)

    def body(*refs):
        ins, own, sib = refs[:n], refs[n : 2 * n], refs[2 * n : 3 * n]
        local_sem, send_sem, recv_sem, pass_send, pass_recv = refs[3 * n :]
        x, y, c, k = _position()
        sibling = (x, y, 1 - c)
        _handshake([sibling] + [_chip_device(k ^ j, c) for j in range(1, N_CHIPS)])

        def over_ici(a, j, piece, slot, to):
            return pltpu.make_async_remote_copy(
                src_ref=ins[a].at[piece],
                dst_ref=own[a].at[slot],
                send_sem=send_sem.at[a, j],
                recv_sem=recv_sem.at[a, j],
                device_id=to,
                device_id_type=MESH,
            )

        def to_sibling(a, j, slot):
            return pltpu.make_async_remote_copy(
                src_ref=own[a].at[slot],
                dst_ref=sib[a].at[slot],
                send_sem=pass_send.at[a, j],
                recv_sem=pass_recv.at[a, j],
                device_id=sibling,
                device_id_type=MESH,
            )

        mine = [pltpu.make_async_copy(ins[a].at[k], own[a].at[k], local_sem.at[a]) for a in range(n)]
        for cp in mine:
            cp.start()
        sent = []
        for j in range(N_CHIPS - 1):
            other = k ^ (j + 1)
            for a in range(n):
                cp = over_ici(a, j, other, k, _chip_device(other, c))
                cp.start()
                sent.append(cp)
        for a in range(n):
            mine[a].wait()
            cp = to_sibling(a, N_CHIPS - 1, k)
            cp.start()
            sent.append(cp)
        for j in range(N_CHIPS - 1):
            other = k ^ (j + 1)
            for a in range(n):
                over_ici(a, j, other, other, sibling).wait_recv()
                cp = to_sibling(a, j, other)
                cp.start()
                sent.append(cp)
        for j in range(N_CHIPS):
            for a in range(n):
                to_sibling(a, j, k).wait_recv()
        for cp in sent:
            cp.wait_send()

    shapes = [jax.ShapeDtypeStruct(a.shape, a.dtype) for a in pieces]
    outs = pl.kernel(
        body,
        name=name,
        out_type=shapes + shapes,
        mesh=plsc.ScalarSubcoreMesh(axis_name="sequencer", num_cores=1),
        scratch_types=[
            pltpu.SemaphoreType.DMA((n,)),
            pltpu.SemaphoreType.DMA((n, N_CHIPS - 1)),
            pltpu.SemaphoreType.DMA((n, N_CHIPS - 1)),
            pltpu.SemaphoreType.DMA((n, N_CHIPS)),
            pltpu.SemaphoreType.DMA((n, N_CHIPS)),
        ],
        compiler_params=pltpu.CompilerParams(collective_id=collective_id),
    )(*pieces)
    return outs[:n], outs[n:]


def _allreduce_small(buf, *, name):
    def body(x_ref, o_ref, slots, send_sem, recv_sem):
        x, y, c, _ = _position()
        me = 4 * x + 2 * y + c
        slots[me] = x_ref[...]

        def copy(d, slot):
            peer = me ^ d
            return pltpu.make_async_remote_copy(
                src_ref=x_ref,
                dst_ref=slots.at[slot],
                send_sem=send_sem.at[d - 1],
                recv_sem=recv_sem.at[d - 1],
                device_id=(peer >> 2, (peer >> 1) & 1, peer & 1),
                device_id_type=MESH,
            )

        sent = [copy(d, me) for d in range(1, N_DEV)]
        for cp in sent:
            cp.start()
        for d in range(1, N_DEV):
            copy(d, me ^ d).wait_recv()
        for cp in sent:
            cp.wait_send()
        acc = slots[0]
        for s in range(1, N_DEV):
            acc = acc + slots[s]
        o_ref[...] = acc

    return pl.pallas_call(
        body,
        name=name,
        in_specs=[pl.BlockSpec(memory_space=pltpu.VMEM)],
        out_specs=pl.BlockSpec(memory_space=pltpu.VMEM),
        out_shape=jax.ShapeDtypeStruct(buf.shape, F32),
        scratch_shapes=[
            pltpu.VMEM((N_DEV,) + buf.shape, F32),
            pltpu.SemaphoreType.DMA((N_DEV - 1,)),
            pltpu.SemaphoreType.DMA((N_DEV - 1,)),
        ],
        compiler_params=pltpu.CompilerParams(vmem_limit_bytes=VMEM_LIMIT),
    )(buf)


def _mixer_of(i):
    return i % N_MIXERS, i // N_MIXERS


def _forward_backward(x, target, norms, layers, send=lambda i, part, pieces: pieces):
    s, d = x.shape
    depth = len(layers)
    heads = d // HEAD_DIM
    hkv = heads // GQA_GROUP
    cos, sin = _rope_tables(s)
    saved = []
    h = x
    for i, lw in enumerate(layers):
        kind, j = _mixer_of(i)
        tag = f"l{i}"
        sv = {"h_in": h}
        a = _rms_fwd(h, norms["mix_norm"][i : i + 1], name=f"{tag}_mix_norm")
        qkv = _mm_nn(a, lw["w_in"], out_dtype=BF16, name=f"{tag}_w_in")
        if kind == 0:
            bias = _na_bias(_na_base(norms["na_rpb"][j]), name=f"{tag}_na_bias")
            sv["bias"] = _na_block_bias(bias, s // GRID_W, name=f"{tag}_na_block_bias")
            o = _na_fwd(qkv, sv["bias"], name=f"{tag}_na_fwd")
        elif kind == 1:
            o = _sc_mid_fwd(qkv, lw["sc_conv_w"], name=f"{tag}_sc_fwd")
        else:
            gq, gk = norms["gqa_q_norm"][j : j + 1], norms["gqa_k_norm"][j : j + 1]
            qkn = _gqa_prep_fwd(qkv, gq, gk, cos, sin, heads, hkv, name=f"{tag}_gqa_prep")
            o = _gqa_fwd(qkn, qkv, heads, hkv, name=f"{tag}_gqa_fwd")
            sv["qkn"] = qkn
        h_mid = _mm_nn(o, lw["w_out"], out_dtype=F32, residual=h, name=f"{tag}_w_out")
        b = _rms_fwd(h_mid, norms["ffn_norm"][i : i + 1], name=f"{tag}_ffn_norm")
        up = _mm_nn(b, lw["w_up"], out_dtype=BF16, name=f"{tag}_w_up")
        act, cg, cu = _ffn_mid_fwd(up, lw["ffn_conv_w"], lw["ffn_conv_b"], name=f"{tag}_ffn_fwd")
        h = _mm_nn(act, lw["w_down"], out_dtype=F32, residual=h_mid, name=f"{tag}_w_down")
        sv.update(a=a, qkv=qkv, o=o, h_mid=h_mid, b=b, up=up, act=act, cg=cg, cu=cu)
        saved.append(sv)

    dh, dh_b, d_final, loss = _loss_head(h, norms["final_norm"][None], target, name="loss_head")

    big = [None] * depth
    small = {"final_norm": d_final, "mix_norm": [None] * depth, "ffn_norm": [None] * depth,
             "ffn_conv_w": [None] * depth, "ffn_conv_b": [None] * depth, "na_rpb": {}}
    after = ()
    for i in reversed(range(depth)):
        kind, j = _mixer_of(i)
        tag = f"l{i}b"
        lw, sv = layers[i], saved[i]
        dw_down = _mm_tn(sv["act"], dh_b, 1, name=f"{tag}_dw_down")
        sent_down = send(i, "down", [dw_down.reshape(N_CHIPS, dw_down.shape[1] // N_CHIPS, d)])
        dact = _mm_nt(dh_b, lw["w_down"], out_dtype=BF16, name=f"{tag}_d_act", after=after + (_token(dw_down),))
        dug, duu, dwg, dwu, dbg, dbu = _ffn_mid_bwd(
            sv["up"], sv["cg"], sv["cu"], lw["ffn_conv_w"], dact, name=f"{tag}_ffn_bwd"
        )
        small["ffn_conv_w"][i] = jnp.concatenate([dwg, dwu], axis=1)
        small["ffn_conv_b"][i] = jnp.concatenate([dbg, dbu], axis=1)
        dw_up = _mm_tn(sv["b"], (dug, duu), N_CHIPS, name=f"{tag}_dw_up")
        sent_up = send(i, "up", [dw_up])
        db = _mm_nt((dug, duu), lw["w_up"], out_dtype=F32, name=f"{tag}_d_b")
        dh_mid, dh_mid_b, small["ffn_norm"][i] = _rms_bwd(
            sv["h_mid"], norms["ffn_norm"][i : i + 1], db, dh, name=f"{tag}_ffn_norm"
        )
        do = _mm_nt(dh_mid_b, lw["w_out"], out_dtype=BF16, name=f"{tag}_d_o", after=(_token(dw_up),))
        dw_out = _mm_tn(sv["o"], dh_mid_b, 1, name=f"{tag}_dw_out")
        if kind == 0:
            dq, dk, dv, dblock = _na_bwd(sv["qkv"], sv["bias"], do, name=f"{tag}_na_bwd")
            dbias = _na_unblock(dblock, s // GRID_W, name=f"{tag}_na_unblock")
            dqkv = jnp.concatenate([dq, dk, dv], axis=1)
            small["na_rpb"][j] = _rpb_fold(dbias, name=f"{tag}_rpb_fold")
        elif kind == 1:
            dgb, dgc, dhh, small["sc_conv_w"] = _sc_mid_bwd(sv["qkv"], lw["sc_conv_w"], do, name=f"{tag}_sc_bwd")
            dqkv = jnp.concatenate([dgb, dgc, dhh], axis=1)
        else:
            gq, gk = norms["gqa_q_norm"][j : j + 1], norms["gqa_k_norm"][j : j + 1]
            dqn, dkn, dv = _gqa_bwd(sv["qkn"], sv["qkv"], do, heads, hkv, name=f"{tag}_gqa_bwd")
            dqk, small["gqa_q_norm"], small["gqa_k_norm"] = _gqa_prep_bwd(
                sv["qkv"], gq, gk, cos, sin, dqn, dkn, heads, hkv, name=f"{tag}_gqa_prep_bwd"
            )
            dqkv = jnp.concatenate([dqk, dv], axis=1)
        dw_in = _mm_tn(sv["a"], dqkv, N_CHIPS, name=f"{tag}_dw_in")
        da = _mm_nt(dqkv, lw["w_in"], out_dtype=F32, name=f"{tag}_d_a")
        dh, dh_b, small["mix_norm"][i] = _rms_bwd(
            sv["h_in"], norms["mix_norm"][i : i + 1], da, dh_mid, name=f"{tag}_mix_norm"
        )
        sent_mix = send(i, "mix", [dw_in, dw_out.reshape(N_CHIPS, dw_out.shape[1] // N_CHIPS, d)])
        after = (_token(dw_in), _token(dw_out))
        big[i] = {"mix": sent_mix, "up": sent_up, "down": sent_down}
    return loss, dh, big, small


def _pack(parts):
    flat = jnp.concatenate([p.reshape(-1).astype(F32) for p in parts])
    pad = (-flat.shape[0]) % (8 * LANES)
    return jnp.pad(flat, (0, pad)).reshape(-1, LANES)


def _unpack(buf, shapes):
    flat = buf.reshape(-1)
    out, at = [], 0
    for shp in shapes:
        size = 1
        for n in shp:
            size *= n
        out.append(flat[at : at + size].reshape(shp))
        at += size
    return out


def kernel(x, mix_norm, ffn_norm, final_norm, na_w_qkv, na_rpb, na_w_o, sc_w_in, sc_conv_w, sc_w_out, gqa_w_qkv, gqa_q_norm, gqa_k_norm, gqa_w_o, ffn_w_up, ffn_conv_w, ffn_conv_b, ffn_w_down, loss_target, m_mix_norm, m_ffn_norm, m_final_norm, m_na_w_qkv, m_na_rpb, m_na_w_o, m_sc_w_in, m_sc_conv_w, m_sc_w_out, m_gqa_w_qkv, m_gqa_q_norm, m_gqa_k_norm, m_gqa_w_o, m_ffn_w_up, m_ffn_conv_w, m_ffn_conv_b, m_ffn_w_down, v_mix_norm, v_ffn_norm, v_final_norm, v_na_w_qkv, v_na_rpb, v_na_w_o, v_sc_w_in, v_sc_conv_w, v_sc_w_out, v_gqa_w_qkv, v_gqa_q_norm, v_gqa_k_norm, v_gqa_w_o, v_ffn_w_up, v_ffn_conv_w, v_ffn_conv_b, v_ffn_w_down):
    depth, d = mix_norm.shape
    chip = 2 * lax.axis_index("x") + lax.axis_index("y")
    w_in_of = {0: na_w_qkv, 1: sc_w_in, 2: gqa_w_qkv}
    w_out_of = {0: na_w_o, 1: sc_w_out, 2: gqa_w_o}

    layers = []
    for i in range(depth):
        kind, j = _mixer_of(i)
        shards = [_cast_slab(w_in_of[kind], j, name=f"cast_w_in_l{i}"), _cast_slab(w_out_of[kind], j, name=f"cast_w_out_l{i}")]
        split = [True, True]
        if kind == 1:
            shards.append(sc_conv_w[j])
            split.append(False)
        mix = _gather_shards(shards, split, name=f"gather_mix_l{i}", collective_id=1 + 2 * i)
        ffn = _gather_shards(
            [_cast_slab(ffn_w_up, i, name=f"cast_w_up_l{i}"), _cast_slab(ffn_w_down, i, name=f"cast_w_down_l{i}"),
             ffn_conv_w[i]],
            [True, True, False],
            name=f"gather_ffn_l{i}",
            collective_id=2 + 2 * i,
        )
        lw = {
            "w_in": mix[0],
            "w_out": mix[1].reshape(1, -1, d),
            "w_up": ffn[0],
            "w_down": ffn[1].reshape(1, -1, d),
            "ffn_conv_w": ffn[2],
            "ffn_conv_b": ffn_conv_b[i : i + 1],
        }
        if kind == 1:
            lw["sc_conv_w"] = mix[2]
        layers.append(lw)

    norms = dict(mix_norm=mix_norm, ffn_norm=ffn_norm, final_norm=final_norm, na_rpb=na_rpb,
                 gqa_q_norm=gqa_q_norm, gqa_k_norm=gqa_k_norm)

    parts = ("down", "up", "mix")

    def send(i, part, pieces):
        cid = 1 + 2 * depth + len(parts) * i + parts.index(part)
        return _scatter_pieces(pieces, name=f"scatter_{part}_l{i}", collective_id=cid)

    loss, grad_x, big, small = _forward_backward(x[0], loss_target[0], norms, layers, send)

    mixer_names = {0: ("na_w_qkv", "na_w_o"), 1: ("sc_w_in", "sc_w_out"), 2: ("gqa_w_qkv", "gqa_w_o")}
    state = {
        "na_w_qkv": (na_w_qkv, m_na_w_qkv, v_na_w_qkv), "na_w_o": (na_w_o, m_na_w_o, v_na_w_o),
        "sc_w_in": (sc_w_in, m_sc_w_in, v_sc_w_in), "sc_w_out": (sc_w_out, m_sc_w_out, v_sc_w_out),
        "gqa_w_qkv": (gqa_w_qkv, m_gqa_w_qkv, v_gqa_w_qkv), "gqa_w_o": (gqa_w_o, m_gqa_w_o, v_gqa_w_o),
        "ffn_w_up": (ffn_w_up, m_ffn_w_up, v_ffn_w_up), "ffn_w_down": (ffn_w_down, m_ffn_w_down, v_ffn_w_down),
    }
    res = {n: None for n in state}
    token = None
    for i in reversed(range(depth)):
        kind, j = _mixer_of(i)
        for part, names, slab in (("down", ("ffn_w_down",), i), ("up", ("ffn_w_up",), i), ("mix", mixer_names[kind], j)):
            own, sib = big[i][part]
            for slot, n in enumerate(names):
                res[n] = _adamw_slab(*state[n], own[slot], sib[slot], slab, res[n], token, name=f"adamw_{n}_l{i}")
                token = res[n][0][slab, :16, :LANES]

    n_na = na_rpb.shape[0]
    rpb_flat = jnp.stack([small["na_rpb"][j] for j in range(n_na)])
    full_parts = [
        loss[:, :1],
        jnp.concatenate(small["mix_norm"], axis=0),
        jnp.concatenate(small["ffn_norm"], axis=0),
        small["final_norm"],
        rpb_flat,
        small["sc_conv_w"],
        small["gqa_q_norm"],
        small["gqa_k_norm"],
        jnp.stack(small["ffn_conv_w"]),
        jnp.concatenate(small["ffn_conv_b"], axis=0),
    ]
    summed = _unpack(_allreduce_small(_pack(full_parts), name="allreduce_small"), [p.shape for p in full_parts])
    loss_all, g_mix, g_ffn, g_final, g_rpb, g_sc_cw, g_gq, g_gk, g_ffn_cw, g_ffn_cb = summed
    g_rpb = g_rpb[:, :, : RPB_ROWS * RPB_COLS].reshape(na_rpb.shape)
    g_sc_cw = lax.dynamic_slice_in_dim(g_sc_cw, chip * sc_conv_w.shape[2], sc_conv_w.shape[2], axis=1)[None]
    g_ffn_cw = lax.dynamic_slice_in_dim(g_ffn_cw, chip * ffn_conv_w.shape[2], ffn_conv_w.shape[2], axis=2)
    small_names = ["mix_norm", "ffn_norm", "final_norm", "na_rpb", "sc_conv_w", "gqa_q_norm", "gqa_k_norm",
                   "ffn_conv_w", "ffn_conv_b"]
    small_g = [g_mix, g_ffn, g_final.reshape(final_norm.shape), g_rpb, g_sc_cw, g_gq, g_gk, g_ffn_cw, g_ffn_cb]
    small_w = [mix_norm, ffn_norm, final_norm, na_rpb, sc_conv_w, gqa_q_norm, gqa_k_norm, ffn_conv_w, ffn_conv_b]
    small_m = [m_mix_norm, m_ffn_norm, m_final_norm, m_na_rpb, m_sc_conv_w, m_gqa_q_norm, m_gqa_k_norm,
               m_ffn_conv_w, m_ffn_conv_b]
    small_v = [v_mix_norm, v_ffn_norm, v_final_norm, v_na_rpb, v_sc_conv_w, v_gqa_q_norm, v_gqa_k_norm,
               v_ffn_conv_w, v_ffn_conv_b]
    shapes = [w.shape for w in small_w]
    packed = _adamw_small(_pack(small_w), _pack(small_g), _pack(small_m), _pack(small_v), name="adamw_small")
    small_d, small_nm, small_nv = (_unpack(p, shapes) for p in packed)
    for n, g, dl, nm, nv in zip(small_names, small_g, small_d, small_nm, small_nv):
        res[n] = (g.reshape(dl.shape), dl, nm, nv)

    order = ["mix_norm", "ffn_norm", "final_norm", "na_w_qkv", "na_rpb", "na_w_o", "sc_w_in", "sc_conv_w",
             "sc_w_out", "gqa_w_qkv", "gqa_q_norm", "gqa_k_norm", "gqa_w_o", "ffn_w_up", "ffn_conv_w",
             "ffn_conv_b", "ffn_w_down"]
    outs = [loss_all.reshape(()), grad_x[None]]
    for part in range(4):
        outs.extend(res[n][part] for n in order)
    return tuple(outs)
```

```python
import functools
import math

import jax
import jax.numpy as jnp
from jax import lax
from jax.experimental import pallas as pl
from jax.experimental.pallas import tpu as pltpu
from jax.experimental.pallas import tpu_sc as plsc

F32 = jnp.float32
BF16 = jnp.bfloat16
MESH = pl.DeviceIdType.MESH

N_CHIPS = 4
N_DEV = 8
N_MIXERS = 3
GRID_W = 64
HEAD_DIM = 128
EPS = 1e-6
NEG_INF = -1e30
NA_WIN_R = 8
NA_WIN_C = 16
GQA_GROUP = 4
ROPE_THETA = 10000.0
ADAM_LR = 0.001
ADAM_B1 = 0.9
ADAM_B2 = 0.999
ADAM_EPS = 1e-08
ADAM_WD = 0.01
ADAM_STEP = 10

LANES = 128
VMEM_LIMIT = 56 * 1024 * 1024
NT_DIMS = (((1,), (1,)), ((), ()))
TN_DIMS = (((0,), (0,)), ((), ()))


def _pick(n, cap, mult=LANES):
    best = None
    for t in range(mult, min(n, cap) + 1, mult):
        if n % t == 0:
            best = t
    return best if best is not None else n


def _params(*sem):
    return pltpu.CompilerParams(dimension_semantics=sem, vmem_limit_bytes=VMEM_LIMIT)


MM_VMEM_BUDGET = 47 * 1024 * 1024
MM_CONTRACT = 2816


def _mm_rows(m, blocks_for, mult=16):
    for cap in (1024, 512, 256, 128):
        tm = _pick(m, cap, mult)
        if sum(r * c * b * n for r, c, b, n in blocks_for(tm)) <= MM_VMEM_BUDGET:
            return tm
    return _pick(m, 128, mult)


def _accumulate(acc, step, steps, part, finish):
    if steps == 1:
        finish(part)
        return

    @pl.when(step == 0)
    def _():
        acc[...] = part

    @pl.when(step != 0)
    def _():
        acc[...] += part

    @pl.when(step == steps - 1)
    def _():
        finish(acc[...])


def _mm_nn(a, b, *, out_dtype, name, residual=None):
    m, k = a.shape
    nc, _, ncol = b.shape
    tn, tk = _pick(ncol, 1536), _pick(k, MM_CONTRACT)
    per, nk = ncol // tn, k // tk
    osz = jnp.dtype(out_dtype).itemsize
    tm = _mm_rows(m, lambda t: [(t, tk, a.dtype.itemsize, 2), (tk, tn, 2, 2), (t, tn, osz, 2),
                                (t, tn, 4, 2 * (residual is not None)), (t, tn, 4, nk > 1)])

    def body(a_ref, b_ref, *rest):
        o_ref = rest[-2] if nk > 1 else rest[-1]

        def finish(r):
            if residual is not None:
                r = r + rest[0][...]
            o_ref[...] = r.astype(o_ref.dtype)

        part = jnp.dot(a_ref[...].astype(BF16), b_ref[...], preferred_element_type=F32)
        _accumulate(rest[-1], pl.program_id(2), nk, part, finish)

    in_specs = [
        pl.BlockSpec((tm, tk), lambda i, j, kk: (i, kk)),
        pl.BlockSpec((None, tk, tn), lambda i, j, kk: (j // per, kk, j % per)),
    ]
    ops = [a, b]
    if residual is not None:
        in_specs.append(pl.BlockSpec((tm, tn), lambda i, j, kk: (i, j)))
        ops.append(residual)
    return pl.pallas_call(
        body,
        name=name,
        grid=(m // tm, nc * per, nk),
        in_specs=in_specs,
        out_specs=pl.BlockSpec((tm, tn), lambda i, j, kk: (i, j)),
        out_shape=jax.ShapeDtypeStruct((m, nc * ncol), out_dtype),
        scratch_shapes=[pltpu.VMEM((tm, tn), F32)] * (nk > 1),
        compiler_params=_params("parallel", "parallel", "arbitrary"),
    )(*ops)


def _token(x):
    return x[(0,) * (x.ndim - 2) + (slice(0, 16), slice(0, LANES))]


def _mm_nt(a, b, *, out_dtype, name, after=()):
    parts = tuple(a) if isinstance(a, (tuple, list)) else (a,)
    m, width = parts[0].shape
    nc, k, ncol = b.shape
    tko, tn = _pick(k, 1024), _pick(math.gcd(ncol, width), MM_CONTRACT)
    per, each, nn = ncol // tn, width // tn, len(parts) * width // tn
    osz = jnp.dtype(out_dtype).itemsize
    tm = _mm_rows(m, lambda t: [(t, tn, parts[0].dtype.itemsize, 2 * len(parts)), (tko, tn, 2, 2), (t, tko, osz, 2),
                                (t, tko, 4, nn > 1)])

    def body(*refs):
        a_refs, b_ref, rest = refs[: len(parts)], refs[len(parts)], refs[len(parts) + 1 :]
        o_ref = rest[len(after)]
        step = pl.program_id(2)

        def finish(r):
            o_ref[...] = r.astype(o_ref.dtype)

        def use(a_ref):
            part = lax.dot_general(a_ref[...].astype(BF16), b_ref[...], NT_DIMS, preferred_element_type=F32)
            _accumulate(rest[-1], step, nn, part, finish)

        _for_part(step // each, a_refs, use)

    return pl.pallas_call(
        body,
        name=name,
        grid=(m // tm, k // tko, nn),
        in_specs=[_part_spec((tm, tn), p, each, lambda i, j, s: (i, s)) for p in range(len(parts))]
        + [pl.BlockSpec((None, tko, tn), lambda i, j, s: (s // per, j, s % per))]
        + [pl.BlockSpec(t.shape, lambda i, j, s: (0, 0)) for t in after],
        out_specs=pl.BlockSpec((tm, tko), lambda i, j, s: (i, j)),
        out_shape=jax.ShapeDtypeStruct((m, k), out_dtype),
        scratch_shapes=[pltpu.VMEM((tm, tko), F32)] * (nn > 1),
        compiler_params=_params("parallel", "parallel", "arbitrary"),
    )(*parts, b, *after)


def _for_part(which, refs, use):
    if len(refs) == 1:
        use(refs[0])
        return
    for p, ref in enumerate(refs):
        pl.when(which == p)(functools.partial(use, ref))


def _part_spec(block, p, each, tile_of):
    def index(*ids):
        r, c = tile_of(*ids)
        return r, jnp.clip(c - p * each, 0, each - 1)

    return pl.BlockSpec(block, index)


def _mm_tn(a, g, nc, *, name):
    parts = tuple(g) if isinstance(g, (tuple, list)) else (g,)
    s, k = a.shape
    width = parts[0].shape[1]
    ncol = len(parts) * width // nc
    ts, tn = _pick(s, MM_CONTRACT, 16), _pick(math.gcd(ncol, width), 1536)
    per, each, ns = ncol // tn, width // tn, s // ts
    tko = _mm_rows(k, lambda t: [(ts, t, a.dtype.itemsize, 2), (ts, tn, parts[0].dtype.itemsize, 2 * len(parts)),
                                 (t, tn, 2, 2), (t, tn, 4, ns > 1)], mult=LANES)

    def body(a_ref, *refs):
        g_refs, rest = refs[: len(parts)], refs[len(parts) :]
        o_ref = rest[0]

        def finish(r):
            o_ref[...] = r.astype(o_ref.dtype)

        def use(g_ref):
            part = lax.dot_general(a_ref[...].astype(BF16), g_ref[...].astype(BF16), TN_DIMS, preferred_element_type=F32)
            _accumulate(rest[-1], pl.program_id(2), ns, part, finish)

        _for_part(pl.program_id(1) // each, g_refs, use)

    return pl.pallas_call(
        body,
        name=name,
        grid=(k // tko, nc * per, ns),
        in_specs=[pl.BlockSpec((ts, tko), lambda i, j, t: (t, i))]
        + [_part_spec((ts, tn), p, each, lambda i, j, t: (t, j)) for p in range(len(parts))],
        out_specs=pl.BlockSpec((None, tko, tn), lambda i, j, t: (j // per, i, j % per)),
        out_shape=jax.ShapeDtypeStruct((nc, k, ncol), BF16),
        scratch_shapes=[pltpu.VMEM((tko, tn), F32)] * (ns > 1),
        compiler_params=_params("parallel", "parallel", "arbitrary"),
    )(a, *parts)


ROW_TILE = 256


def _cast_slab(w, slab, *, name):
    _, rows, cols = w.shape
    tr = _pick(rows, ROW_TILE, 16)

    def body(w_ref, o_ref):
        o_ref[...] = w_ref[...].astype(o_ref.dtype)

    return pl.pallas_call(
        body,
        name=name,
        grid=(rows // tr,),
        in_specs=[pl.BlockSpec((None, tr, cols), lambda i: (slab, i, 0))],
        out_specs=pl.BlockSpec((tr, cols), lambda i: (i, 0)),
        out_shape=jax.ShapeDtypeStruct((rows, cols), BF16),
        compiler_params=_params("parallel"),
    )(w)


def _rms_fwd(h, g, *, name):
    s, d = h.shape
    tr = _pick(s, 2 * ROW_TILE, 16)

    def body(h_ref, g_ref, o_ref):
        x = h_ref[...]
        r = lax.rsqrt(jnp.mean(x * x, axis=-1, keepdims=True) + EPS)
        o_ref[...] = (x * r * g_ref[...]).astype(o_ref.dtype)

    return pl.pallas_call(
        body,
        name=name,
        grid=(s // tr,),
        in_specs=[pl.BlockSpec((tr, d), lambda i: (i, 0)), pl.BlockSpec((1, d), lambda i: (0, 0))],
        out_specs=pl.BlockSpec((tr, d), lambda i: (i, 0)),
        out_shape=jax.ShapeDtypeStruct((s, d), BF16),
        compiler_params=_params("parallel"),
    )(h, g)


def _rms_bwd(h, g, dy, dres, *, name):
    s, d = h.shape
    tr = _pick(s, ROW_TILE, 16)

    def body(h_ref, g_ref, dy_ref, dres_ref, dh_ref, dhb_ref, dg_ref):
        x = h_ref[...]
        r = lax.rsqrt(jnp.mean(x * x, axis=-1, keepdims=True) + EPS)
        xhat = x * r
        dyv = dy_ref[...].astype(F32)
        dyg = dyv * g_ref[...]
        dx = r * (dyg - xhat * jnp.mean(dyg * xhat, axis=-1, keepdims=True))
        dh = dres_ref[...] + dx
        dh_ref[...] = dh
        dhb_ref[...] = dh.astype(dhb_ref.dtype)
        part = jnp.sum(dyv * xhat, axis=0, keepdims=True)

        @pl.when(pl.program_id(0) == 0)
        def _():
            dg_ref[...] = part

        @pl.when(pl.program_id(0) != 0)
        def _():
            dg_ref[...] += part

    row = pl.BlockSpec((tr, d), lambda i: (i, 0))
    vec = pl.BlockSpec((1, d), lambda i: (0, 0))
    return pl.pallas_call(
        body,
        name=name,
        grid=(s // tr,),
        in_specs=[row, vec, row, row],
        out_specs=[row, row, vec],
        out_shape=[jax.ShapeDtypeStruct((s, d), F32), jax.ShapeDtypeStruct((s, d), BF16),
                   jax.ShapeDtypeStruct((1, d), F32)],
        compiler_params=_params("arbitrary"),
    )(h, g, dy, dres)


def _loss_head(h, g, target, *, name):
    s, d = h.shape
    tr = _pick(s, ROW_TILE, 16)

    def body(h_ref, g_ref, t_ref, dh_ref, dhb_ref, dg_ref, loss_ref):
        x = h_ref[...]
        r = lax.rsqrt(jnp.mean(x * x, axis=-1, keepdims=True) + EPS)
        xhat = x * r
        gv = g_ref[...]
        err = xhat * gv - t_ref[...]
        dyv = err * (1.0 / d)
        dyg = dyv * gv
        dh = r * (dyg - xhat * jnp.mean(dyg * xhat, axis=-1, keepdims=True))
        dh_ref[...] = dh
        dhb_ref[...] = dh.astype(dhb_ref.dtype)
        part = jnp.sum(dyv * xhat, axis=0, keepdims=True)
        lpart =jnp.sum(jnp.sum(err * err, axis=-1, keepdims=True), axis=0, keepdims=True) * (0.5 / d)

        @pl.when(pl.program_id(0) == 0)
        def _():
            dg_ref[...] = part
            loss_ref[...] = jnp.broadcast_to(lpart, loss_ref.shape)

        @pl.when(pl.program_id(0) != 0)
        def _():
            dg_ref[...] += part
            loss_ref[...] += jnp.broadcast_to(lpart, loss_ref.shape)

    row = pl.BlockSpec((tr, d), lambda i: (i, 0))
    vec = pl.BlockSpec((1, d), lambda i: (0, 0))
    return pl.pallas_call(
        body,
        name=name,
        grid=(s // tr,),
        in_specs=[row, vec, row],
        out_specs=[row, row, vec, pl.BlockSpec((1, LANES), lambda i: (0, 0))],
        out_shape=[
            jax.ShapeDtypeStruct((s, d), F32),
            jax.ShapeDtypeStruct((s, d), BF16),
            jax.ShapeDtypeStruct((1, d), F32),
            jax.ShapeDtypeStruct((1, LANES), F32),
        ],
        compiler_params=_params("arbitrary"),
    )(h, g, target)


def _shift_prev(x):
    row = lax.broadcasted_iota(jnp.int32, x.shape, 0)
    return jnp.where(row == 0, 0.0, pltpu.roll(x, 1, 0))


def _shift_next(x):
    n = x.shape[0]
    row = lax.broadcasted_iota(jnp.int32, x.shape, 0)
    return jnp.where(row == n - 1, 0.0, pltpu.roll(x, n - 1, 0))


def _conv3(x, w):
    xm, xp = _shift_prev(x), _shift_next(x)
    return xm * w[0:1] + x * w[1:2] + xp * w[2:3], xm, xp


def _conv3_t(d, w):
    return _shift_next(d) * w[0:1] + d * w[1:2] + _shift_prev(d) * w[2:3]


def _colsum(x):
    return jnp.sum(x, axis=0, keepdims=True)


def _ffn_mid_fwd(up, cw, cb, *, name):
    s, f2 = up.shape
    f = f2 // 2
    ncol = cw.shape[2]
    tc = _pick(ncol, 256)
    nt, per = f // tc, ncol // tc

    def body(ug_ref, uu_ref, wg_ref, wu_ref, bg_ref, bu_ref, o_ref, cg_ref, cu_ref):
        cg = _conv3(ug_ref[...].astype(F32), wg_ref[...])[0] + bg_ref[...]
        cu = _conv3(uu_ref[...].astype(F32), wu_ref[...])[0] + bu_ref[...]
        o_ref[...] = (cg * (1.0 / (1.0 + jnp.exp(-cg))) * cu).astype(o_ref.dtype)
        cg_ref[...] = cg.astype(cg_ref.dtype)
        cu_ref[...] = cu.astype(cu_ref.dtype)

    out = pl.BlockSpec((s, tc), lambda j: (0, j))
    return pl.pallas_call(
        body,
        name=name,
        grid=(nt,),
        in_specs=[
            pl.BlockSpec((s, tc), lambda j: (0, j)),
            pl.BlockSpec((s, tc), lambda j: (0, nt + j)),
            pl.BlockSpec((None, 3, tc), lambda j: (j // per, 0, j % per)),
            pl.BlockSpec((None, 3, tc), lambda j: ((nt + j) // per, 0, (nt + j) % per)),
            pl.BlockSpec((1, tc), lambda j: (0, j)),
            pl.BlockSpec((1, tc), lambda j: (0, nt + j)),
        ],
        out_specs=[out, out, out],
        out_shape=[jax.ShapeDtypeStruct((s, f), BF16)] * 3,
        compiler_params=_params("parallel"),
    )(up, up, cw, cw, cb, cb)


def _ffn_mid_bwd(up, cg, cu, cw, dact, *, name):
    s, f2 = up.shape
    f = f2 // 2
    ncol = cw.shape[2]
    tc = _pick(ncol, 256)
    nt, per = f // tc, ncol // tc

    def side(dc, u_ref, w_ref, du_ref, dw_ref, db_ref):
        w, u = w_ref[...], u_ref[...].astype(F32)
        nxt, prv = _shift_next(dc), _shift_prev(dc)
        du_ref[...] = (nxt * w[0:1] + dc * w[1:2] + prv * w[2:3]).astype(du_ref.dtype)
        dw_ref[0:1, :] = _colsum(nxt * u)
        dw_ref[1:2, :] = _colsum(dc * u)
        dw_ref[2:3, :] = _colsum(prv * u)
        db_ref[...] = _colsum(dc)

    def body(ug_ref, uu_ref, cg_ref, cu_ref, wg_ref, wu_ref, da_ref, dug_ref, duu_ref, dwg_ref, dwu_ref, dbg_ref, dbu_ref):
        cgv, cuv, da = cg_ref[...].astype(F32), cu_ref[...].astype(F32), da_ref[...].astype(F32)
        sig = 1.0 / (1.0 + jnp.exp(-cgv))
        side(da * cuv * (sig * (1.0 + cgv * (1.0 - sig))), ug_ref, wg_ref, dug_ref, dwg_ref, dbg_ref)
        side(da * (cgv * sig), uu_ref, wu_ref, duu_ref, dwu_ref, dbu_ref)

    col = pl.BlockSpec((s, tc), lambda j: (0, j))
    w3 = pl.BlockSpec((3, tc), lambda j: (0, j))
    b1 = pl.BlockSpec((1, tc), lambda j: (0, j))
    return pl.pallas_call(
        body,
        name=name,
        grid=(nt,),
        in_specs=[
            col,
            pl.BlockSpec((s, tc), lambda j: (0, nt + j)),
            col,
            col,
            pl.BlockSpec((None, 3, tc), lambda j: (j // per, 0, j % per)),
            pl.BlockSpec((None, 3, tc), lambda j: ((nt + j) // per, 0, (nt + j) % per)),
            col,
        ],
        out_specs=[col, col, w3, w3, b1, b1],
        out_shape=[
            jax.ShapeDtypeStruct((s, f), BF16),
            jax.ShapeDtypeStruct((s, f), BF16),
            jax.ShapeDtypeStruct((3, f), F32),
            jax.ShapeDtypeStruct((3, f), F32),
            jax.ShapeDtypeStruct((1, f), F32),
            jax.ShapeDtypeStruct((1, f), F32),
        ],
        compiler_params=_params("parallel"),
    )(up, up, cg, cu, cw, cw, dact)


def _sc_mid_fwd(z, cw, *, name):
    s, d3 = z.shape
    d = d3 // 3
    ncol = cw.shape[2]
    tc = _pick(ncol, 256)
    nt, per = d // tc, ncol // tc

    def body(gb_ref, gc_ref, hh_ref, w_ref, o_ref):
        p = gc_ref[...].astype(F32) * hh_ref[...].astype(F32)
        o_ref[...] = (gb_ref[...].astype(F32) * _conv3(p, w_ref[...])[0]).astype(o_ref.dtype)

    return pl.pallas_call(
        body,
        name=name,
        grid=(nt,),
        in_specs=[
            pl.BlockSpec((s, tc), lambda j: (0, j)),
            pl.BlockSpec((s, tc), lambda j: (0, nt + j)),
            pl.BlockSpec((s, tc), lambda j: (0, 2 * nt + j)),
            pl.BlockSpec((None, 3, tc), lambda j: (j // per, 0, j % per)),
        ],
        out_specs=pl.BlockSpec((s, tc), lambda j: (0, j)),
        out_shape=jax.ShapeDtypeStruct((s, d), BF16),
        compiler_params=_params("parallel"),
    )(z, z, z, cw)


def _sc_mid_bwd(z, cw, dmid, *, name):
    s, d3 = z.shape
    d = d3 // 3
    ncol = cw.shape[2]
    tc = _pick(ncol, 256)
    nt, per = d // tc, ncol // tc

    def body(gb_ref, gc_ref, hh_ref, w_ref, dm_ref, dgb_ref, dgc_ref, dhh_ref, dw_ref):
        gb, gc, hh = gb_ref[...].astype(F32), gc_ref[...].astype(F32), hh_ref[...].astype(F32)
        w = w_ref[...]
        p = gc * hh
        cv, pm, pp = _conv3(p, w)
        dm = dm_ref[...].astype(F32)
        dgb_ref[...] = (dm * cv).astype(dgb_ref.dtype)
        dcv = dm * gb
        dp = _conv3_t(dcv, w)
        dgc_ref[...] = (dp * hh).astype(dgc_ref.dtype)
        dhh_ref[...] = (dp * gc).astype(dhh_ref.dtype)
        dw_ref[0:1, :] = _colsum(dcv * pm)
        dw_ref[1:2, :] = _colsum(dcv * p)
        dw_ref[2:3, :] = _colsum(dcv * pp)

    col = pl.BlockSpec((s, tc), lambda j: (0, j))
    return pl.pallas_call(
        body,
        name=name,
        grid=(nt,),
        in_specs=[
            col,
            pl.BlockSpec((s, tc), lambda j: (0, nt + j)),
            pl.BlockSpec((s, tc), lambda j: (0, 2 * nt + j)),
            pl.BlockSpec((None, 3, tc), lambda j: (j // per, 0, j % per)),
            col,
        ],
        out_specs=[col, col, col, pl.BlockSpec((3, tc), lambda j: (0, j))],
        out_shape=[jax.ShapeDtypeStruct((s, d), BF16)] * 3 + [jax.ShapeDtypeStruct((3, d), F32)],
        compiler_params=_params("parallel"),
    )(z, z, z, cw, dmid)


NA_KEYS = NA_WIN_R * GRID_W


def _na_row_start(r, rows):
    return jnp.clip(r - NA_WIN_R // 2, 0, rows - NA_WIN_R)


def _na_base(rpb):
    h = rpb.shape[0]
    pos, neg = rpb[:, :, NA_WIN_C - 1:], rpb[:, :, : NA_WIN_C - 1]
    zeros = jnp.zeros((h, NA_WIN_R, GRID_W - 2 * NA_WIN_C + 1), F32)
    out = []
    for first in range(NA_WIN_R):
        p = pos[:, first : first + NA_WIN_R]
        n = jnp.roll(neg[:, first : first + NA_WIN_R], -1, axis=1)
        out.append(jnp.concatenate([p, zeros, n], axis=-1).reshape(h, 1, NA_KEYS))
    return jnp.stack(out, axis=1)


def _skew_right(x):
    return pltpu.roll(x, 0, 1, stride=1, stride_axis=0)


def _skew_left(x):
    n = x.shape[1]
    row = lax.broadcasted_iota(jnp.int32, x.shape, 0)
    for b in range(GRID_W.bit_length() - 1):
        x = jnp.where(((row >> b) & 1) == 1, pltpu.roll(x, n - (1 << b), 1), x)
    return x


def _na_bias(base, *, name):
    h = base.shape[0]

    def body(b_ref, o_ref):
        q = lax.broadcasted_iota(jnp.int32, (GRID_W, NA_KEYS), 0)
        kc = lax.broadcasted_iota(jnp.int32, (GRID_W, NA_KEYS), 1) % GRID_W
        start = jnp.clip(q - NA_WIN_C // 2, 0, GRID_W - NA_WIN_C)
        inside = (kc >= start) & (kc < start + NA_WIN_C)
        for slot in range(NA_WIN_R):
            x = _skew_right(jnp.broadcast_to(b_ref[slot], (GRID_W, NA_KEYS)))
            o_ref[slot] = jnp.where(inside, x, NEG_INF)

    return pl.pallas_call(
        body,
        name=name,
        grid=(h,),
        in_specs=[pl.BlockSpec((None, NA_WIN_R, 1, NA_KEYS), lambda i: (i, 0, 0, 0))],
        out_specs=pl.BlockSpec((None, NA_WIN_R, GRID_W, NA_KEYS), lambda i: (i, 0, 0, 0)),
        out_shape=jax.ShapeDtypeStruct((h, NA_WIN_R, GRID_W, NA_KEYS), F32),
        compiler_params=_params("parallel"),
    )(base)


NA_STEP_ROWS = 4
NA_BLOCK_ROWS = NA_STEP_ROWS + NA_WIN_R
NA_STEP_Q = NA_STEP_ROWS * GRID_W
NA_BLOCK_K = NA_BLOCK_ROWS * GRID_W
NA_PATTERNS = 3


def _na_plan(rows):
    steps = rows // NA_STEP_ROWS

    def start(r):
        return min(max(r - NA_WIN_R // 2, 0), rows - NA_WIN_R)

    plan = []
    for t in range(steps):
        first = [start(NA_STEP_ROWS * t + i) for i in range(NA_STEP_ROWS)]
        block = min(first[0], rows - NA_BLOCK_ROWS)
        offsets = tuple(f - block for f in first)
        slots = tuple(f - (NA_STEP_ROWS * t + i) + NA_WIN_R - 1 for i, f in enumerate(first))
        plan.append((offsets, slots))
    middle = plan[1] if steps > 2 else (tuple(range(NA_STEP_ROWS)), (NA_WIN_R // 2 - 1,) * NA_STEP_ROWS)
    assert steps >= 2 and all(p == middle for p in plan[1:-1])
    return steps, (plan[0], middle, plan[-1])


def _na_block_start(t, rows):
    return jnp.minimum(_na_row_start(NA_STEP_ROWS * t, rows), rows - NA_BLOCK_ROWS)


def _na_pattern(t, steps):
    return jnp.where(t == 0, 0, jnp.where(t == steps - 1, 2, 1))


def _na_block_bias(bias, rows, *, name):
    h = bias.shape[0]
    _, patterns = _na_plan(rows)

    def body(b_ref, o_ref):
        outside = jnp.full((GRID_W, NA_BLOCK_K - NA_KEYS), NEG_INF, F32)
        for p, (offsets, slots) in enumerate(patterns):

            @pl.when(pl.program_id(1) == p)
            def _(offsets=offsets, slots=slots):
                for i in range(NA_STEP_ROWS):
                    wide = jnp.concatenate([b_ref[slots[i]], outside], axis=1)
                    shift = offsets[i] * GRID_W
                    o_ref[i * GRID_W : (i + 1) * GRID_W, :] = pltpu.roll(wide, shift, 1) if shift else wide

    return pl.pallas_call(
        body,
        name=name,
        grid=(h, NA_PATTERNS),
        in_specs=[pl.BlockSpec((None, NA_WIN_R, GRID_W, NA_KEYS), lambda i, p: (i, 0, 0, 0))],
        out_specs=pl.BlockSpec((None, None, NA_STEP_Q, NA_BLOCK_K), lambda i, p: (i, p, 0, 0)),
        out_shape=jax.ShapeDtypeStruct((h, NA_PATTERNS, NA_STEP_Q, NA_BLOCK_K), F32),
        compiler_params=_params("parallel", "arbitrary"),
    )(bias)


def _na_unblock(dblock, rows, *, name):
    h = dblock.shape[0]
    steps, patterns = _na_plan(rows)
    used = [p for p in range(NA_PATTERNS) if p != 1 or steps > 2]

    def body(d_ref, o_ref):
        total = {}
        for p in used:
            offsets, slots = patterns[p]
            for i in range(NA_STEP_ROWS):
                wide = d_ref[p, i * GRID_W : (i + 1) * GRID_W, :]
                back = (NA_BLOCK_K - offsets[i] * GRID_W) % NA_BLOCK_K
                piece = (pltpu.roll(wide, back, 1) if back else wide)[:, :NA_KEYS]
                total[slots[i]] = piece if slots[i] not in total else total[slots[i]] + piece
        for slot in range(NA_WIN_R):
            o_ref[slot] = total.get(slot, jnp.zeros((GRID_W, NA_KEYS), F32))

    return pl.pallas_call(
        body,
        name=name,
        grid=(h,),
        in_specs=[pl.BlockSpec((None, NA_PATTERNS, NA_STEP_Q, NA_BLOCK_K), lambda i: (i, 0, 0, 0))],
        out_specs=pl.BlockSpec((None, NA_WIN_R, GRID_W, NA_KEYS), lambda i: (i, 0, 0, 0)),
        out_shape=jax.ShapeDtypeStruct((h, NA_WIN_R, GRID_W, NA_KEYS), F32),
        compiler_params=_params("parallel"),
    )(dblock)


def _na_probs(q, k, bias):
    sc = lax.dot_general(q, k, NT_DIMS, preferred_element_type=F32) * (HEAD_DIM ** -0.5) + bias
    p = jnp.exp(sc - jnp.max(sc, axis=-1, keepdims=True))
    return p, jnp.sum(p, axis=-1, keepdims=True)


def _na_fwd(qkv, block_bias, *, name):
    s = qkv.shape[0]
    heads = qkv.shape[1] // (3 * HEAD_DIM)
    rows = s // GRID_W
    steps, _ = _na_plan(rows)

    per = next(n for n in (4, 2, 1) if heads % n == 0)
    width = per * HEAD_DIM

    def body(q_ref, k_ref, v_ref, b_ref, o_ref):
        start = _na_block_start(pl.program_id(1), rows)
        block = pl.ds(pl.multiple_of(start * GRID_W, GRID_W), NA_BLOCK_K)
        for g in range(per):
            cols = slice(g * HEAD_DIM, (g + 1) * HEAD_DIM)
            p, l = _na_probs(q_ref[:, cols], k_ref[block, cols], b_ref[g])
            o = jnp.dot(p.astype(BF16), v_ref[block, cols], preferred_element_type=F32)
            o_ref[:, cols] = (o / l).astype(o_ref.dtype)

    q = pl.BlockSpec((NA_STEP_Q, width), lambda h, t: (t, h))
    k = pl.BlockSpec((s, width), lambda h, t: (0, heads // per + h))
    v = pl.BlockSpec((s, width), lambda h, t: (0, 2 * heads // per + h))
    b = pl.BlockSpec((per, None, NA_STEP_Q, NA_BLOCK_K), lambda h, t: (h, _na_pattern(t, steps), 0, 0))
    return pl.pallas_call(
        body,
        name=name,
        grid=(heads // per, steps),
        in_specs=[q, k, v, b],
        out_specs=q,
        out_shape=jax.ShapeDtypeStruct((s, heads * HEAD_DIM), BF16),
        compiler_params=_params("parallel", "arbitrary"),
    )(qkv, qkv, qkv, block_bias)


def _na_bwd(qkv, block_bias, dout, *, name):
    s = qkv.shape[0]
    heads = qkv.shape[1] // (3 * HEAD_DIM)
    rows = s // GRID_W
    steps, _ = _na_plan(rows)
    scale = HEAD_DIM ** -0.5
    per = 2 if heads % 2 == 0 else 1
    width = per * HEAD_DIM

    def body(q_ref, k_ref, v_ref, b_ref, do_ref, dq_ref, dk_ref, dv_ref, db_ref, dk_acc, dv_acc):
        step = pl.program_id(1)

        @pl.when(step == 0)
        def _():
            dk_acc[...] = jnp.zeros_like(dk_acc)
            dv_acc[...] = jnp.zeros_like(dv_acc)

        block = pl.ds(pl.multiple_of(_na_block_start(step, rows) * GRID_W, GRID_W), NA_BLOCK_K)
        opens = (step <= 1) | (step == steps - 1)
        for g in range(per):
            cols = slice(g * HEAD_DIM, (g + 1) * HEAD_DIM)
            q, k, v, do = q_ref[:, cols], k_ref[block, cols], v_ref[block, cols], do_ref[:, cols]
            p, l = _na_probs(q, k, b_ref[g])
            pn = p / l
            dp = lax.dot_general(do, v, NT_DIMS, preferred_element_type=F32)
            ds = pn * (dp - jnp.sum(pn * dp, axis=-1, keepdims=True))
            dsb = ds.astype(BF16)
            dq_ref[:, cols] = (jnp.dot(dsb, k, preferred_element_type=F32) * scale).astype(dq_ref.dtype)
            dk_acc[block, cols] += lax.dot_general(dsb, q, TN_DIMS, preferred_element_type=F32) * scale
            dv_acc[block, cols] += lax.dot_general(pn.astype(BF16), do, TN_DIMS, preferred_element_type=F32)

            @pl.when(opens)
            def _(g=g, ds=ds):
                db_ref[g] = ds

            @pl.when(jnp.logical_not(opens))
            def _(g=g, ds=ds):
                db_ref[g] += ds

        @pl.when(step == steps - 1)
        def _():
            dk_ref[...] = dk_acc[...].astype(dk_ref.dtype)
            dv_ref[...] = dv_acc[...].astype(dv_ref.dtype)

    q = pl.BlockSpec((NA_STEP_Q, width), lambda h, t: (t, h))
    k = pl.BlockSpec((s, width), lambda h, t: (0, heads // per + h))
    v = pl.BlockSpec((s, width), lambda h, t: (0, 2 * heads // per + h))
    b = pl.BlockSpec((per, None, NA_STEP_Q, NA_BLOCK_K), lambda h, t: (h, _na_pattern(t, steps), 0, 0))
    kv_out = pl.BlockSpec((s, width), lambda h, t: (0, h))
    shape = jax.ShapeDtypeStruct((s, heads * HEAD_DIM), BF16)
    return pl.pallas_call(
        body,
        name=name,
        grid=(heads // per, steps),
        in_specs=[q, k, v, b, q],
        out_specs=[q, kv_out, kv_out, b],
        out_shape=[shape, shape, shape, jax.ShapeDtypeStruct((heads, NA_PATTERNS, NA_STEP_Q, NA_BLOCK_K), F32)],
        scratch_shapes=[pltpu.VMEM((s, width), F32), pltpu.VMEM((s, width), F32)],
        compiler_params=_params("parallel", "arbitrary"),
    )(qkv, qkv, qkv, block_bias, dout)


RPB_ROWS = 2 * NA_WIN_R - 1
RPB_COLS = 2 * NA_WIN_C - 1
RPB_PAD = 512


def _rpb_fold_matrix():
    idx = jnp.arange(NA_WIN_R * NA_KEYS, dtype=jnp.int32)
    first, i, kc = idx // NA_KEYS, (idx // GRID_W) % NA_WIN_R, idx % GRID_W
    pos, neg = kc < NA_WIN_C, kc >= GRID_W - NA_WIN_C + 1
    dr = jnp.where(pos, first + i, first + (i + 1) % NA_WIN_R)
    dc = jnp.where(pos, kc + NA_WIN_C - 1, kc - (GRID_W - NA_WIN_C + 1))
    target = jnp.where(pos | neg, dr * RPB_COLS + dc, -1)
    return (target[:, None] == jnp.arange(RPB_PAD, dtype=jnp.int32)[None, :]).astype(F32)


def _rpb_fold(dbias, *, name):
    h = dbias.shape[0]

    def skew_body(g_ref, o_ref):
        for slot in range(NA_WIN_R):
            o_ref[slot] = _colsum(_skew_left(g_ref[slot]))

    skewed = pl.pallas_call(
        skew_body,
        name=name + "_skew",
        grid=(h,),
        in_specs=[pl.BlockSpec((None, NA_WIN_R, GRID_W, NA_KEYS), lambda i: (i, 0, 0, 0))],
        out_specs=pl.BlockSpec((None, NA_WIN_R, 1, NA_KEYS), lambda i: (i, 0, 0, 0)),
        out_shape=jax.ShapeDtypeStruct((h, NA_WIN_R, 1, NA_KEYS), F32),
        compiler_params=_params("parallel"),
    )(dbias)

    def fold_body(g_ref, m_ref, o_ref):
        o_ref[...] = jnp.dot(g_ref[...], m_ref[...], preferred_element_type=F32, precision=lax.Precision.HIGHEST)

    return pl.pallas_call(
        fold_body,
        name=name,
        out_shape=jax.ShapeDtypeStruct((h, RPB_PAD), F32),
        compiler_params=pltpu.CompilerParams(vmem_limit_bytes=VMEM_LIMIT),
    )(skewed.reshape(h, NA_WIN_R * NA_KEYS), _rpb_fold_matrix())


GQA_Q_TILE = 256
GQA_Q_TILE_BWD = 512


def _rope_tables(s):
    t = jnp.arange(s)
    row = (t // GRID_W).astype(F32)[:, None]
    col = (t % GRID_W).astype(F32)[:, None]
    half = HEAD_DIM // 2
    inv = ROPE_THETA ** (-jnp.arange(0, half, 2, dtype=F32) / half)
    ang = jnp.concatenate([row * inv, row * inv, col * inv, col * inv], axis=-1)
    return jnp.cos(ang), jnp.sin(ang)


def _rot_half(y):
    quarter = HEAD_DIM // 4
    lane = lax.broadcasted_iota(jnp.int32, y.shape, 1)
    low = (lane % (2 * quarter)) < quarter
    return jnp.where(low, -pltpu.roll(y, HEAD_DIM - quarter, 1), pltpu.roll(y, quarter, 1))


def _gqa_prep_fwd(qkv, gq, gk, cos, sin, hq, hkv, *, name):
    s = qkv.shape[0]

    def body(x_ref, gq_ref, gk_ref, cos_ref, sin_ref, o_ref):
        isq = pl.program_id(0) < hq
        x = x_ref[...].astype(F32)
        g = jnp.where(isq, gq_ref[...], gk_ref[...])
        y = x * lax.rsqrt(jnp.mean(x * x, axis=-1, keepdims=True) + EPS) * g
        z = y * cos_ref[...] + _rot_half(y) * sin_ref[...]
        o_ref[...] = (z * jnp.where(isq, HEAD_DIM ** -0.5, 1.0)).astype(o_ref.dtype)

    head = pl.BlockSpec((s, HEAD_DIM), lambda h: (0, h))
    vec = pl.BlockSpec((1, HEAD_DIM), lambda h: (0, 0))
    tab = pl.BlockSpec((s, HEAD_DIM), lambda h: (0, 0))
    return pl.pallas_call(
        body,
        name=name,
        grid=(hq + hkv,),
        in_specs=[head, vec, vec, tab, tab],
        out_specs=head,
        out_shape=jax.ShapeDtypeStruct((s, (hq + hkv) * HEAD_DIM), BF16),
        compiler_params=_params("parallel"),
    )(qkv, gq, gk, cos, sin)


def _gqa_prep_bwd(qkv, gq, gk, cos, sin, dqn, dkn, hq, hkv, *, name):
    s = qkv.shape[0]

    def body(x_ref, gq_ref, gk_ref, cos_ref, sin_ref, dq_ref, dk_ref, dx_ref, dgq_ref, dgk_ref):
        hh = pl.program_id(0)
        isq = hh < hq
        x = x_ref[...].astype(F32)
        g = jnp.where(isq, gq_ref[...], gk_ref[...])
        r = lax.rsqrt(jnp.mean(x * x, axis=-1, keepdims=True) + EPS)
        xhat = x * r
        dz = jnp.where(isq, dq_ref[...].astype(F32) * (HEAD_DIM ** -0.5), dk_ref[...].astype(F32))
        dy = dz * cos_ref[...] - _rot_half(dz * sin_ref[...])
        dyg = dy * g
        dx_ref[...] = (r * (dyg - xhat * jnp.mean(dyg * xhat, axis=-1, keepdims=True))).astype(dx_ref.dtype)
        part = _colsum(dy * xhat)

        @pl.when(hh == 0)
        def _():
            dgq_ref[...] = jnp.zeros_like(dgq_ref)
            dgk_ref[...] = jnp.zeros_like(dgk_ref)

        @pl.when(isq)
        def _():
            dgq_ref[...] += part

        @pl.when(jnp.logical_not(isq))
        def _():
            dgk_ref[...] += part

    head = pl.BlockSpec((s, HEAD_DIM), lambda h: (0, h))
    vec = pl.BlockSpec((1, HEAD_DIM), lambda h: (0, 0))
    tab = pl.BlockSpec((s, HEAD_DIM), lambda h: (0, 0))
    return pl.pallas_call(
        body,
        name=name,
        grid=(hq + hkv,),
        in_specs=[
            head,
            vec,
            vec,
            tab,
            tab,
            pl.BlockSpec((s, HEAD_DIM), lambda h: (0, jnp.minimum(h, hq - 1))),
            pl.BlockSpec((s, HEAD_DIM), lambda h: (0, jnp.maximum(h - hq, 0))),
        ],
        out_specs=[head, vec, vec],
        out_shape=[
            jax.ShapeDtypeStruct((s, (hq + hkv) * HEAD_DIM), BF16),
            jax.ShapeDtypeStruct((1, HEAD_DIM), F32),
            jax.ShapeDtypeStruct((1, HEAD_DIM), F32),
        ],
        compiler_params=_params("arbitrary"),
    )(qkv, gq, gk, cos, sin, dqn, dkn)


def _gqa_fwd(qkn, qkv, hq, hkv, *, name):
    s = qkv.shape[0]
    tq = _pick(s, GQA_Q_TILE, 16)

    def body(q_ref, k_ref, v_ref, o_ref):
        k, v = k_ref[...], v_ref[...]
        for g in range(GQA_GROUP):
            cols = slice(g * HEAD_DIM, (g + 1) * HEAD_DIM)
            sc = lax.dot_general(q_ref[:, cols], k, NT_DIMS, preferred_element_type=F32)
            p = jnp.exp(sc - jnp.max(sc, axis=-1, keepdims=True))
            l = jnp.sum(p, axis=-1, keepdims=True)
            o_ref[:, cols] = (jnp.dot(p.astype(BF16), v, preferred_element_type=F32) / l).astype(o_ref.dtype)

    q = pl.BlockSpec((tq, GQA_GROUP * HEAD_DIM), lambda h, i: (i, h))
    return pl.pallas_call(
        body,
        name=name,
        grid=(hkv, s // tq),
        in_specs=[
            q,
            pl.BlockSpec((s, HEAD_DIM), lambda h, i: (0, hq + h)),
            pl.BlockSpec((s, HEAD_DIM), lambda h, i: (0, hq + hkv + h)),
        ],
        out_specs=q,
        out_shape=jax.ShapeDtypeStruct((s, hq * HEAD_DIM), BF16),
        compiler_params=_params("parallel", "parallel"),
    )(qkn, qkn, qkv)


def _gqa_bwd(qkn, qkv, dout, hq, hkv, *, name):
    s = qkv.shape[0]
    tq = _pick(s, GQA_Q_TILE_BWD, 16)
    nq = s // tq

    def body(q_ref, k_ref, v_ref, do_ref, dq_ref, dk_ref, dv_ref, dk_acc, dv_acc):
        g, i = pl.program_id(1), pl.program_id(2)

        @pl.when((g == 0) & (i == 0))
        def _():
            dk_acc[...] = jnp.zeros_like(dk_acc)
            dv_acc[...] = jnp.zeros_like(dv_acc)

        q, k, v, do = q_ref[...], k_ref[...], v_ref[...], do_ref[...]
        sc = lax.dot_general(q, k, NT_DIMS, preferred_element_type=F32)
        p = jnp.exp(sc - jnp.max(sc, axis=-1, keepdims=True))
        pn = p / jnp.sum(p, axis=-1, keepdims=True)
        dp = lax.dot_general(do, v, NT_DIMS, preferred_element_type=F32)
        dsb = (pn * (dp - jnp.sum(pn * dp, axis=-1, keepdims=True))).astype(BF16)
        dq_ref[...] = jnp.dot(dsb, k, preferred_element_type=F32).astype(dq_ref.dtype)
        dk_acc[...] += lax.dot_general(dsb, q, TN_DIMS, preferred_element_type=F32)
        dv_acc[...] += lax.dot_general(pn.astype(BF16), do, TN_DIMS, preferred_element_type=F32)

        @pl.when((g == GQA_GROUP - 1) & (i == nq - 1))
        def _():
            dk_ref[...] = dk_acc[...].astype(dk_ref.dtype)
            dv_ref[...] = dv_acc[...].astype(dv_ref.dtype)

    q = pl.BlockSpec((tq, HEAD_DIM), lambda kv, g, i: (i, kv * GQA_GROUP + g))
    kv_out = pl.BlockSpec((s, HEAD_DIM), lambda kv, g, i: (0, kv))
    return pl.pallas_call(
        body,
        name=name,
        grid=(hkv, GQA_GROUP, nq),
        in_specs=[
            q,
            pl.BlockSpec((s, HEAD_DIM), lambda kv, g, i: (0, hq + kv)),
            pl.BlockSpec((s, HEAD_DIM), lambda kv, g, i: (0, hq + hkv + kv)),
            q,
        ],
        out_specs=[q, kv_out, kv_out],
        out_shape=[
            jax.ShapeDtypeStruct((s, hq * HEAD_DIM), BF16),
            jax.ShapeDtypeStruct((s, hkv * HEAD_DIM), BF16),
            jax.ShapeDtypeStruct((s, hkv * HEAD_DIM), BF16),
        ],
        scratch_shapes=[pltpu.VMEM((s, HEAD_DIM), F32), pltpu.VMEM((s, HEAD_DIM), F32)],
        compiler_params=_params("parallel", "arbitrary", "arbitrary"),
    )(qkn, qkn, qkv, dout)


ADAM_ROWS = 128


def _adam_update(w, g, m, v):
    m = ADAM_B1 * m + (1.0 - ADAM_B1) * g
    v = ADAM_B2 * v + (1.0 - ADAM_B2) * (g * g)
    m_hat = m / (1.0 - ADAM_B1 ** ADAM_STEP)
    v_hat = v / (1.0 - ADAM_B2 ** ADAM_STEP)
    return -ADAM_LR * (m_hat / (jnp.sqrt(v_hat) + ADAM_EPS) + ADAM_WD * w), m, v


def _adamw_slab(w, m, v, own, sib, slab, prev, after, *, name):
    _, rows, cols = w.shape
    tr = _pick(rows, ADAM_ROWS, 16)
    tokens = [] if after is None else [after]

    def body(w_ref, m_ref, v_ref, own_ref, sib_ref, *rest):
        g_ref, d_ref, nm_ref, nv_ref = rest[-4:]
        g = own_ref[0].astype(F32) + sib_ref[0].astype(F32)
        for q in range(1, N_CHIPS):
            g = g + (own_ref[q].astype(F32) + sib_ref[q].astype(F32))
        g_ref[...] = g
        d_ref[...], nm_ref[...], nv_ref[...] = _adam_update(w_ref[...], g, m_ref[...], v_ref[...])

    one = pl.BlockSpec((None, tr, cols), lambda i: (slab, i, 0))
    piece = pl.BlockSpec((N_CHIPS, tr, cols), lambda i: (0, i, 0))
    carried = [] if prev is None else list(prev)
    shape = jax.ShapeDtypeStruct(w.shape, F32)
    return pl.pallas_call(
        body,
        name=name,
        grid=(rows // tr,),
        in_specs=[one] * 3
        + [piece] * 2
        + [pl.BlockSpec(memory_space=pl.ANY)] * len(carried)
        + [pl.BlockSpec(t.shape, lambda i: (0, 0)) for t in tokens],
        out_specs=[one] * 4,
        out_shape=[shape] * 4,
        input_output_aliases={5 + i: i for i in range(len(carried))},
        compiler_params=_params("parallel"),
    )(w, m, v, own, sib, *carried, *tokens)


def _adamw_small(w, g, m, v, *, name):
    def body(w_ref, g_ref, m_ref, v_ref, d_ref, nm_ref, nv_ref):
        d_ref[...], nm_ref[...], nv_ref[...] = _adam_update(w_ref[...], g_ref[...], m_ref[...], v_ref[...])

    shape = jax.ShapeDtypeStruct(w.shape, F32)
    return pl.pallas_call(
        body,
        name=name,
        out_shape=[shape] * 3,
        compiler_params=pltpu.CompilerParams(vmem_limit_bytes=VMEM_LIMIT),
    )(w, g, m, v)


def _position():
    x, y, c = lax.axis_index("x"), lax.axis_index("y"), lax.axis_index("c")
    return x, y, c, 2 * x + y


def _chip_device(chip, c):
    return (chip >> 1, chip & 1, c)


def _handshake(peers):
    barrier = pltpu.get_barrier_semaphore()
    for peer in peers:
        pl.semaphore_signal(barrier, inc=1, device_id=peer, device_id_type=MESH)
    pl.semaphore_wait(barrier, len(peers))


GATHER_CHUNKS = 2


def _gather_shards(shards, split, *, name, collective_id):
    n = len(shards)
    big = [a for a in range(n) if split[a]]
    small = [a for a in range(n) if not split[a]]
    y_nbr, x_nbr, far = 1, 2, 3

    def body(*refs):
        ins, outs = refs[:n], refs[n : 2 * n]
        near_send, near_recv, far_send, far_recv, pass_send, pass_recv, own_send, own_recv = refs[2 * n :]
        x, y, c, k = _position()
        _handshake([(x, y, 1 - c)] + [_chip_device(k ^ j, c) for j in range(1, N_CHIPS)])

        def own(a, chunk):
            r = shards[a].shape[0]
            part = pl.ds(chunk * (r // GATHER_CHUNKS), r // GATHER_CHUNKS) if split[a] else pl.ds(0, r)
            return pltpu.make_async_remote_copy(
                src_ref=ins[a].at[part],
                dst_ref=outs[a].at[k, part],
                send_sem=own_send.at[a, chunk],
                recv_sem=own_recv.at[a, chunk],
                device_id=(x, y, 1 - c),
                device_id_type=MESH,
            )

        own_copies = [own(a, ch) for a in range(n) for ch in range(GATHER_CHUNKS if split[a] else 1)]
        for cp in own_copies:
            cp.start()

        def run(core):
            sibling = (x, y, 1 - core)
            relay_from, relay_to = (x_nbr, y_nbr) if core == 0 else (y_nbr, x_nbr)

            def rows(a, which, chunk=None):
                r = shards[a].shape[0]
                if not split[a]:
                    return pl.ds(0, r)
                half = r // 2
                if chunk is None:
                    return pl.ds(which * half, half)
                return pl.ds(which * half + chunk * (half // GATHER_CHUNKS), half // GATHER_CHUNKS)

            def direct(a, mask, chunk, src_chip):
                part = rows(a, core, chunk)
                return pltpu.make_async_remote_copy(
                    src_ref=ins[a].at[part],
                    dst_ref=outs[a].at[src_chip, part],
                    send_sem=near_send.at[a, mask - 1, chunk or 0],
                    recv_sem=near_recv.at[a, mask - 1, chunk or 0],
                    device_id=_chip_device(k ^ mask, core),
                    device_id_type=MESH,
                )

            def relay(a, chunk, src_chip, mask):
                part = outs[a].at[src_chip, rows(a, core, chunk)]
                return pltpu.make_async_remote_copy(
                    src_ref=part,
                    dst_ref=part,
                    send_sem=far_send.at[a, chunk or 0],
                    recv_sem=far_recv.at[a, chunk or 0],
                    device_id=_chip_device(k ^ mask, core),
                    device_id_type=MESH,
                )

            def to_sibling(a, mask, which):
                part = outs[a].at[k ^ mask, rows(a, which)]
                return pltpu.make_async_remote_copy(
                    src_ref=part,
                    dst_ref=part,
                    send_sem=pass_send.at[a, mask - 1],
                    recv_sem=pass_recv.at[a, mask - 1],
                    device_id=sibling,
                    device_id_type=MESH,
                )

            sent = []

            def start(cp):
                cp.start()
                sent.append(cp)

            chunks = range(GATHER_CHUNKS)
            for chunk in chunks:
                for a in big:
                    start(direct(a, relay_from, chunk, k))
                    start(direct(a, relay_to, chunk, k))
            for a in small:
                start(direct(a, x_nbr, None, k))
                start(direct(a, y_nbr, None, k))
                start(pltpu.make_async_remote_copy(
                    src_ref=ins[a], dst_ref=outs[a].at[k], send_sem=far_send.at[a, 0], recv_sem=far_recv.at[a, 0],
                    device_id=_chip_device(k ^ far, core), device_id_type=MESH))
            for chunk in chunks:
                for a in big:
                    direct(a, relay_from, chunk, k ^ relay_from).wait_recv()
                    start(relay(a, chunk, k ^ relay_from, relay_to))
            for a in big:
                start(to_sibling(a, relay_from, core))
            for a in big:
                for chunk in chunks:
                    direct(a, relay_to, chunk, k ^ relay_to).wait_recv()
                start(to_sibling(a, relay_to, core))
            for a in big:
                for chunk in chunks:
                    relay(a, chunk, k ^ far, far).wait_recv()
                start(to_sibling(a, far, core))
            for a in small:
                direct(a, x_nbr, None, k ^ x_nbr).wait_recv()
                direct(a, y_nbr, None, k ^ y_nbr).wait_recv()
                relay(a, None, k ^ far, far).wait_recv()
            for a in big:
                for mask in (y_nbr, x_nbr, far):
                    to_sibling(a, mask, 1 - core).wait_recv()
            for cp in sent:
                cp.wait_send()

        for core in (0, 1):
            pl.when(c == core)(functools.partial(run, core))
        for cp in own_copies:
            cp.wait()

    return pl.kernel(
        body,
        name=name,
        out_type=[jax.ShapeDtypeStruct((N_CHIPS,) + a.shape, a.dtype) for a in shards],
        mesh=plsc.ScalarSubcoreMesh(axis_name="sequencer", num_cores=1),
        scratch_types=[
            pltpu.SemaphoreType.DMA((n, 2, GATHER_CHUNKS)),
            pltpu.SemaphoreType.DMA((n, 2, GATHER_CHUNKS)),
            pltpu.SemaphoreType.DMA((n, GATHER_CHUNKS)),
            pltpu.SemaphoreType.DMA((n, GATHER_CHUNKS)),
            pltpu.SemaphoreType.DMA((n, N_CHIPS - 1)),
            pltpu.SemaphoreType.DMA((n, N_CHIPS - 1)),
            pltpu.SemaphoreType.DMA((n, GATHER_CHUNKS)),
            pltpu.SemaphoreType.DMA((n, GATHER_CHUNKS)),
        ],
        compiler_params=pltpu.CompilerParams(collective_id=collective_id),
    )(*shards)


def _scatter_pieces(pieces, *, name, collective_id):
    n = len(pieces)

    def body(*refs):
        ins, own, sib = refs[:n], refs[n : 2 * n], refs[2 * n : 3 * n]
        local_sem, send_sem, recv_sem, pass_send, pass_recv = refs[3 * n :]
        x, y, c, k = _position()
        sibling = (x, y, 1 - c)
        _handshake([sibling] + [_chip_device(k ^ j, c) for j in range(1, N_CHIPS)])

        def over_ici(a, j, piece, slot, to):
            return pltpu.make_async_remote_copy(
                src_ref=ins[a].at[piece],
                dst_ref=own[a].at[slot],
                send_sem=send_sem.at[a, j],
                recv_sem=recv_sem.at[a, j],
                device_id=to,
                device_id_type=MESH,
            )

        def to_sibling(a, j, slot):
            return pltpu.make_async_remote_copy(
                src_ref=own[a].at[slot],
                dst_ref=sib[a].at[slot],
                send_sem=pass_send.at[a, j],
                recv_sem=pass_recv.at[a, j],
                device_id=sibling,
                device_id_type=MESH,
            )

        mine = [pltpu.make_async_copy(ins[a].at[k], own[a].at[k], local_sem.at[a]) for a in range(n)]
        for cp in mine:
            cp.start()
        sent = []
        for j in range(N_CHIPS - 1):
            other = k ^ (j + 1)
            for a in range(n):
                cp = over_ici(a, j, other, k, _chip_device(other, c))
                cp.start()
                sent.append(cp)
        for a in range(n):
            mine[a].wait()
            cp = to_sibling(a, N_CHIPS - 1, k)
            cp.start()
            sent.append(cp)
        for j in range(N_CHIPS - 1):
            other = k ^ (j + 1)
            for a in range(n):
                over_ici(a, j, other, other, sibling).wait_recv()
                cp = to_sibling(a, j, other)
                cp.start()
                sent.append(cp)
        for j in range(N_CHIPS):
            for a in range(n):
                to_sibling(a, j, k).wait_recv()
        for cp in sent:
            cp.wait_send()

    shapes = [jax.ShapeDtypeStruct(a.shape, a.dtype) for a in pieces]
    outs = pl.kernel(
        body,
        name=name,
        out_type=shapes + shapes,
        mesh=plsc.ScalarSubcoreMesh(axis_name="sequencer", num_cores=1),
        scratch_types=[
            pltpu.SemaphoreType.DMA((n,)),
            pltpu.SemaphoreType.DMA((n, N_CHIPS - 1)),
            pltpu.SemaphoreType.DMA((n, N_CHIPS - 1)),
            pltpu.SemaphoreType.DMA((n, N_CHIPS)),
            pltpu.SemaphoreType.DMA((n, N_CHIPS)),
        ],
        compiler_params=pltpu.CompilerParams(collective_id=collective_id),
    )(*pieces)
    return outs[:n], outs[n:]


def _allreduce_small(buf, *, name):
    def body(x_ref, o_ref, slots, send_sem, recv_sem):
        x, y, c, _ = _position()
        me = 4 * x + 2 * y + c
        slots[me] = x_ref[...]

        def copy(d, slot):
            peer = me ^ d
            return pltpu.make_async_remote_copy(
                src_ref=x_ref,
                dst_ref=slots.at[slot],
                send_sem=send_sem.at[d - 1],
                recv_sem=recv_sem.at[d - 1],
                device_id=(peer >> 2, (peer >> 1) & 1, peer & 1),
                device_id_type=MESH,
            )

        sent = [copy(d, me) for d in range(1, N_DEV)]
        for cp in sent:
            cp.start()
        for d in range(1, N_DEV):
            copy(d, me ^ d).wait_recv()
        for cp in sent:
            cp.wait_send()
        acc = slots[0]
        for s in range(1, N_DEV):
            acc = acc + slots[s]
        o_ref[...] = acc

    return pl.pallas_call(
        body,
        name=name,
        in_specs=[pl.BlockSpec(memory_space=pltpu.VMEM)],
        out_specs=pl.BlockSpec(memory_space=pltpu.VMEM),
        out_shape=jax.ShapeDtypeStruct(buf.shape, F32),
        scratch_shapes=[
            pltpu.VMEM((N_DEV,) + buf.shape, F32),
            pltpu.SemaphoreType.DMA((N_DEV - 1,)),
            pltpu.SemaphoreType.DMA((N_DEV - 1,)),
        ],
        compiler_params=pltpu.CompilerParams(vmem_limit_bytes=VMEM_LIMIT),
    )(buf)


def _mixer_of(i):
    return i % N_MIXERS, i // N_MIXERS


def _forward_backward(x, target, norms, layers, send=lambda i, part, pieces: pieces):
    s, d = x.shape
    depth = len(layers)
    heads = d // HEAD_DIM
    hkv = heads // GQA_GROUP
    cos, sin = _rope_tables(s)
    saved = []
    h = x
    for i, lw in enumerate(layers):
        kind, j = _mixer_of(i)
        tag = f"l{i}"
        sv = {"h_in": h}
        a = _rms_fwd(h, norms["mix_norm"][i : i + 1], name=f"{tag}_mix_norm")
        qkv = _mm_nn(a, lw["w_in"], out_dtype=BF16, name=f"{tag}_w_in")
        if kind == 0:
            bias = _na_bias(_na_base(norms["na_rpb"][j]), name=f"{tag}_na_bias")
            sv["bias"] = _na_block_bias(bias, s // GRID_W, name=f"{tag}_na_block_bias")
            o = _na_fwd(qkv, sv["bias"], name=f"{tag}_na_fwd")
        elif kind == 1:
            o = _sc_mid_fwd(qkv, lw["sc_conv_w"], name=f"{tag}_sc_fwd")
        else:
            gq, gk = norms["gqa_q_norm"][j : j + 1], norms["gqa_k_norm"][j : j + 1]
            qkn = _gqa_prep_fwd(qkv, gq, gk, cos, sin, heads, hkv, name=f"{tag}_gqa_prep")
            o = _gqa_fwd(qkn, qkv, heads, hkv, name=f"{tag}_gqa_fwd")
            sv["qkn"] = qkn
        h_mid = _mm_nn(o, lw["w_out"], out_dtype=F32, residual=h, name=f"{tag}_w_out")
        b = _rms_fwd(h_mid, norms["ffn_norm"][i : i + 1], name=f"{tag}_ffn_norm")
        up = _mm_nn(b, lw["w_up"], out_dtype=BF16, name=f"{tag}_w_up")
        act, cg, cu = _ffn_mid_fwd(up, lw["ffn_conv_w"], lw["ffn_conv_b"], name=f"{tag}_ffn_fwd")
        h = _mm_nn(act, lw["w_down"], out_dtype=F32, residual=h_mid, name=f"{tag}_w_down")
        sv.update(a=a, qkv=qkv, o=o, h_mid=h_mid, b=b, up=up, act=act, cg=cg, cu=cu)
        saved.append(sv)

    dh, dh_b, d_final, loss = _loss_head(h, norms["final_norm"][None], target, name="loss_head")

    big = [None] * depth
    small = {"final_norm": d_final, "mix_norm": [None] * depth, "ffn_norm": [None] * depth,
             "ffn_conv_w": [None] * depth, "ffn_conv_b": [None] * depth, "na_rpb": {}}
    after = ()
    for i in reversed(range(depth)):
        kind, j = _mixer_of(i)
        tag = f"l{i}b"
        lw, sv = layers[i], saved[i]
        dw_down = _mm_tn(sv["act"], dh_b, 1, name=f"{tag}_dw_down")
        sent_down = send(i, "down", [dw_down.reshape(N_CHIPS, dw_down.shape[1] // N_CHIPS, d)])
        dact = _mm_nt(dh_b, lw["w_down"], out_dtype=BF16, name=f"{tag}_d_act", after=after + (_token(dw_down),))
        dug, duu, dwg, dwu, dbg, dbu = _ffn_mid_bwd(
            sv["up"], sv["cg"], sv["cu"], lw["ffn_conv_w"], dact, name=f"{tag}_ffn_bwd"
        )
        small["ffn_conv_w"][i] = jnp.concatenate([dwg, dwu], axis=1)
        small["ffn_conv_b"][i] = jnp.concatenate([dbg, dbu], axis=1)
        dw_up = _mm_tn(sv["b"], (dug, duu), N_CHIPS, name=f"{tag}_dw_up")
        sent_up = send(i, "up", [dw_up])
        db = _mm_nt((dug, duu), lw["w_up"], out_dtype=F32, name=f"{tag}_d_b")
        dh_mid, dh_mid_b, small["ffn_norm"][i] = _rms_bwd(
            sv["h_mid"], norms["ffn_norm"][i : i + 1], db, dh, name=f"{tag}_ffn_norm"
        )
        do = _mm_nt(dh_mid_b, lw["w_out"], out_dtype=BF16, name=f"{tag}_d_o", after=(_token(dw_up),))
        dw_out = _mm_tn(sv["o"], dh_mid_b, 1, name=f"{tag}_dw_out")
        if kind == 0:
            dq, dk, dv, dblock = _na_bwd(sv["qkv"], sv["bias"], do, name=f"{tag}_na_bwd")
            dbias = _na_unblock(dblock, s // GRID_W, name=f"{tag}_na_unblock")
            dqkv = jnp.concatenate([dq, dk, dv], axis=1)
            small["na_rpb"][j] = _rpb_fold(dbias, name=f"{tag}_rpb_fold")
        elif kind == 1:
            dgb, dgc, dhh, small["sc_conv_w"] = _sc_mid_bwd(sv["qkv"], lw["sc_conv_w"], do, name=f"{tag}_sc_bwd")
            dqkv = jnp.concatenate([dgb, dgc, dhh], axis=1)
        else:
            gq, gk = norms["gqa_q_norm"][j : j + 1], norms["gqa_k_norm"][j : j + 1]
            dqn, dkn, dv = _gqa_bwd(sv["qkn"], sv["qkv"], do, heads, hkv, name=f"{tag}_gqa_bwd")
            dqk, small["gqa_q_norm"], small["gqa_k_norm"] = _gqa_prep_bwd(
                sv["qkv"], gq, gk, cos, sin, dqn, dkn, heads, hkv, name=f"{tag}_gqa_prep_bwd"
            )
            dqkv = jnp.concatenate([dqk, dv], axis=1)
        dw_in = _mm_tn(sv["a"], dqkv, N_CHIPS, name=f"{tag}_dw_in")
        da = _mm_nt(dqkv, lw["w_in"], out_dtype=F32, name=f"{tag}_d_a")
        dh, dh_b, small["mix_norm"][i] = _rms_bwd(
            sv["h_in"], norms["mix_norm"][i : i + 1], da, dh_mid, name=f"{tag}_mix_norm"
        )
        sent_mix = send(i, "mix", [dw_in, dw_out.reshape(N_CHIPS, dw_out.shape[1] // N_CHIPS, d)])
        after = (_token(dw_in), _token(dw_out))
        big[i] = {"mix": sent_mix, "up": sent_up, "down": sent_down}
    return loss, dh, big, small


def _pack(parts):
    flat = jnp.concatenate([p.reshape(-1).astype(F32) for p in parts])
    pad = (-flat.shape[0]) % (8 * LANES)
    return jnp.pad(flat, (0, pad)).reshape(-1, LANES)


def _unpack(buf, shapes):
    flat = buf.reshape(-1)
    out, at = [], 0
    for shp in shapes:
        size = 1
        for n in shp:
            size *= n
        out.append(flat[at : at + size].reshape(shp))
        at += size
    return out


def kernel(x, mix_norm, ffn_norm, final_norm, na_w_qkv, na_rpb, na_w_o, sc_w_in, sc_conv_w, sc_w_out, gqa_w_qkv, gqa_q_norm, gqa_k_norm, gqa_w_o, ffn_w_up, ffn_conv_w, ffn_conv_b, ffn_w_down, loss_target, m_mix_norm, m_ffn_norm, m_final_norm, m_na_w_qkv, m_na_rpb, m_na_w_o, m_sc_w_in, m_sc_conv_w, m_sc_w_out, m_gqa_w_qkv, m_gqa_q_norm, m_gqa_k_norm, m_gqa_w_o, m_ffn_w_up, m_ffn_conv_w, m_ffn_conv_b, m_ffn_w_down, v_mix_norm, v_ffn_norm, v_final_norm, v_na_w_qkv, v_na_rpb, v_na_w_o, v_sc_w_in, v_sc_conv_w, v_sc_w_out, v_gqa_w_qkv, v_gqa_q_norm, v_gqa_k_norm, v_gqa_w_o, v_ffn_w_up, v_ffn_conv_w, v_ffn_conv_b, v_ffn_w_down):
    depth, d = mix_norm.shape
    chip = 2 * lax.axis_index("x") + lax.axis_index("y")
    w_in_of = {0: na_w_qkv, 1: sc_w_in, 2: gqa_w_qkv}
    w_out_of = {0: na_w_o, 1: sc_w_out, 2: gqa_w_o}

    layers = []
    for i in range(depth):
        kind, j = _mixer_of(i)
        shards = [_cast_slab(w_in_of[kind], j, name=f"cast_w_in_l{i}"), _cast_slab(w_out_of[kind], j, name=f"cast_w_out_l{i}")]
        split = [True, True]
        if kind == 1:
            shards.append(sc_conv_w[j])
            split.append(False)
        mix = _gather_shards(shards, split, name=f"gather_mix_l{i}", collective_id=1 + 2 * i)
        ffn = _gather_shards(
            [_cast_slab(ffn_w_up, i, name=f"cast_w_up_l{i}"), _cast_slab(ffn_w_down, i, name=f"cast_w_down_l{i}"),
             ffn_conv_w[i]],
            [True, True, False],
            name=f"gather_ffn_l{i}",
            collective_id=2 + 2 * i,
        )
        lw = {
            "w_in": mix[0],
            "w_out": mix[1].reshape(1, -1, d),
            "w_up": ffn[0],
            "w_down": ffn[1].reshape(1, -1, d),
            "ffn_conv_w": ffn[2],
            "ffn_conv_b": ffn_conv_b[i : i + 1],
        }
        if kind == 1:
            lw["sc_conv_w"] = mix[2]
        layers.append(lw)

    norms = dict(mix_norm=mix_norm, ffn_norm=ffn_norm, final_norm=final_norm, na_rpb=na_rpb,
                 gqa_q_norm=gqa_q_norm, gqa_k_norm=gqa_k_norm)

    parts = ("down", "up", "mix")

    def send(i, part, pieces):
        cid = 1 + 2 * depth + len(parts) * i + parts.index(part)
        return _scatter_pieces(pieces, name=f"scatter_{part}_l{i}", collective_id=cid)

    loss, grad_x, big, small = _forward_backward(x[0], loss_target[0], norms, layers, send)

    mixer_names = {0: ("na_w_qkv", "na_w_o"), 1: ("sc_w_in", "sc_w_out"), 2: ("gqa_w_qkv", "gqa_w_o")}
    state = {
        "na_w_qkv": (na_w_qkv, m_na_w_qkv, v_na_w_qkv), "na_w_o": (na_w_o, m_na_w_o, v_na_w_o),
        "sc_w_in": (sc_w_in, m_sc_w_in, v_sc_w_in), "sc_w_out": (sc_w_out, m_sc_w_out, v_sc_w_out),
        "gqa_w_qkv": (gqa_w_qkv, m_gqa_w_qkv, v_gqa_w_qkv), "gqa_w_o": (gqa_w_o, m_gqa_w_o, v_gqa_w_o),
        "ffn_w_up": (ffn_w_up, m_ffn_w_up, v_ffn_w_up), "ffn_w_down": (ffn_w_down, m_ffn_w_down, v_ffn_w_down),
    }
    res = {n: None for n in state}
    token = None
    for i in reversed(range(depth)):
        kind, j = _mixer_of(i)
        for part, names, slab in (("down", ("ffn_w_down",), i), ("up", ("ffn_w_up",), i), ("mix", mixer_names[kind], j)):
            own, sib = big[i][part]
            for slot, n in enumerate(names):
                res[n] = _adamw_slab(*state[n], own[slot], sib[slot], slab, res[n], token, name=f"adamw_{n}_l{i}")
                token = res[n][0][slab, :16, :LANES]

    n_na = na_rpb.shape[0]
    rpb_flat = jnp.stack([small["na_rpb"][j] for j in range(n_na)])
    full_parts = [
        loss[:, :1],
        jnp.concatenate(small["mix_norm"], axis=0),
        jnp.concatenate(small["ffn_norm"], axis=0),
        small["final_norm"],
        rpb_flat,
        small["sc_conv_w"],
        small["gqa_q_norm"],
        small["gqa_k_norm"],
        jnp.stack(small["ffn_conv_w"]),
        jnp.concatenate(small["ffn_conv_b"], axis=0),
    ]
    summed = _unpack(_allreduce_small(_pack(full_parts), name="allreduce_small"), [p.shape for p in full_parts])
    loss_all, g_mix, g_ffn, g_final, g_rpb, g_sc_cw, g_gq, g_gk, g_ffn_cw, g_ffn_cb = summed
    g_rpb = g_rpb[:, :, : RPB_ROWS * RPB_COLS].reshape(na_rpb.shape)
    g_sc_cw = lax.dynamic_slice_in_dim(g_sc_cw, chip * sc_conv_w.shape[2], sc_conv_w.shape[2], axis=1)[None]
    g_ffn_cw = lax.dynamic_slice_in_dim(g_ffn_cw, chip * ffn_conv_w.shape[2], ffn_conv_w.shape[2], axis=2)
    small_names = ["mix_norm", "ffn_norm", "final_norm", "na_rpb", "sc_conv_w", "gqa_q_norm", "gqa_k_norm",
                   "ffn_conv_w", "ffn_conv_b"]
    small_g = [g_mix, g_ffn, g_final.reshape(final_norm.shape), g_rpb, g_sc_cw, g_gq, g_gk, g_ffn_cw, g_ffn_cb]
    small_w = [mix_norm, ffn_norm, final_norm, na_rpb, sc_conv_w, gqa_q_norm, gqa_k_norm, ffn_conv_w, ffn_conv_b]
    small_m = [m_mix_norm, m_ffn_norm, m_final_norm, m_na_rpb, m_sc_conv_w, m_gqa_q_norm, m_gqa_k_norm,
               m_ffn_conv_w, m_ffn_conv_b]
    small_v = [v_mix_norm, v_ffn_norm, v_final_norm, v_na_rpb, v_sc_conv_w, v_gqa_q_norm, v_gqa_k_norm,
               v_ffn_conv_w, v_ffn_conv_b]
    shapes = [w.shape for w in small_w]
    packed = _adamw_small(_pack(small_w), _pack(small_g), _pack(small_m), _pack(small_v), name="adamw_small")
    small_d, small_nm, small_nv = (_unpack(p, shapes) for p in packed)
    for n, g, dl, nm, nv in zip(small_names, small_g, small_d, small_nm, small_nv):
        res[n] = (g.reshape(dl.shape), dl, nm, nv)

    order = ["mix_norm", "ffn_norm", "final_norm", "na_w_qkv", "na_rpb", "na_w_o", "sc_w_in", "sc_conv_w",
             "sc_w_out", "gqa_w_qkv", "gqa_q_norm", "gqa_k_norm", "gqa_w_o", "ffn_w_up", "ffn_conv_w",
             "ffn_conv_b", "ffn_w_down"]
    outs = [loss_all.reshape(()), grad_x[None]]
    for part in range(4):
        outs.extend(res[n][part] for n in order)
    return tuple(outs)
```

```python
import functools
import math

import jax
import jax.numpy as jnp
from jax import lax
from jax.experimental import pallas as pl
from jax.experimental.pallas import tpu as pltpu
from jax.experimental.pallas import tpu_sc as plsc

F32 = jnp.float32
BF16 = jnp.bfloat16
MESH = pl.DeviceIdType.MESH

N_CHIPS = 4
N_DEV = 8
N_MIXERS = 3
GRID_W = 64
HEAD_DIM = 128
EPS = 1e-6
NEG_INF = -1e30
NA_WIN_R = 8
NA_WIN_C = 16
GQA_GROUP = 4
ROPE_THETA = 10000.0
ADAM_LR = 0.001
ADAM_B1 = 0.9
ADAM_B2 = 0.999
ADAM_EPS = 1e-08
ADAM_WD = 0.01
ADAM_STEP = 10

LANES = 128
VMEM_LIMIT = 56 * 1024 * 1024
NT_DIMS = (((1,), (1,)), ((), ()))
TN_DIMS = (((0,), (0,)), ((), ()))


def _pick(n, cap, mult=LANES):
    best = None
    for t in range(mult, min(n, cap) + 1, mult):
        if n % t == 0:
            best = t
    return best if best is not None else n


def _params(*sem):
    return pltpu.CompilerParams(dimension_semantics=sem, vmem_limit_bytes=VMEM_LIMIT)


MM_VMEM_BUDGET = 47 * 1024 * 1024
MM_CONTRACT = 2816


def _mm_rows(m, blocks_for, mult=16):
    for cap in (1024, 512, 256, 128):
        tm = _pick(m, cap, mult)
        if sum(r * c * b * n for r, c, b, n in blocks_for(tm)) <= MM_VMEM_BUDGET:
            return tm
    return _pick(m, 128, mult)


def _accumulate(acc, step, steps, part, finish):
    if steps == 1:
        finish(part)
        return

    @pl.when(step == 0)
    def _():
        acc[...] = part

    @pl.when(step != 0)
    def _():
        acc[...] += part

    @pl.when(step == steps - 1)
    def _():
        finish(acc[...])


def _mm_nn(a, b, *, out_dtype, name, residual=None):
    m, k = a.shape
    nc, _, ncol = b.shape
    tn, tk = _pick(ncol, 1536), _pick(k, MM_CONTRACT)
    per, nk = ncol // tn, k // tk
    osz = jnp.dtype(out_dtype).itemsize
    tm = _mm_rows(m, lambda t: [(t, tk, a.dtype.itemsize, 2), (tk, tn, 2, 2), (t, tn, osz, 2),
                                (t, tn, 4, 2 * (residual is not None)), (t, tn, 4, nk > 1)])

    def body(a_ref, b_ref, *rest):
        o_ref = rest[-2] if nk > 1 else rest[-1]

        def finish(r):
            if residual is not None:
                r = r + rest[0][...]
            o_ref[...] = r.astype(o_ref.dtype)

        part = jnp.dot(a_ref[...].astype(BF16), b_ref[...], preferred_element_type=F32)
        _accumulate(rest[-1], pl.program_id(2), nk, part, finish)

    in_specs = [
        pl.BlockSpec((tm, tk), lambda i, j, kk: (i, kk)),
        pl.BlockSpec((None, tk, tn), lambda i, j, kk: (j // per, kk, j % per)),
    ]
    ops = [a, b]
    if residual is not None:
        in_specs.append(pl.BlockSpec((tm, tn), lambda i, j, kk: (i, j)))
        ops.append(residual)
    return pl.pallas_call(
        body,
        name=name,
        grid=(m // tm, nc * per, nk),
        in_specs=in_specs,
        out_specs=pl.BlockSpec((tm, tn), lambda i, j, kk: (i, j)),
        out_shape=jax.ShapeDtypeStruct((m, nc * ncol), out_dtype),
        scratch_shapes=[pltpu.VMEM((tm, tn), F32)] * (nk > 1),
        compiler_params=_params("parallel", "parallel", "arbitrary"),
    )(*ops)


def _token(x):
    return x[(0,) * (x.ndim - 2) + (slice(0, 16), slice(0, LANES))]


def _mm_nt(a, b, *, out_dtype, name, after=()):
    parts = tuple(a) if isinstance(a, (tuple, list)) else (a,)
    m, width = parts[0].shape
    nc, k, ncol = b.shape
    tko, tn = _pick(k, 1024), _pick(math.gcd(ncol, width), MM_CONTRACT)
    per, each, nn = ncol // tn, width // tn, len(parts) * width // tn
    osz = jnp.dtype(out_dtype).itemsize
    tm = _mm_rows(m, lambda t: [(t, tn, parts[0].dtype.itemsize, 2 * len(parts)), (tko, tn, 2, 2), (t, tko, osz, 2),
                                (t, tko, 4, nn > 1)])

    def body(*refs):
        a_refs, b_ref, rest = refs[: len(parts)], refs[len(parts)], refs[len(parts) + 1 :]
        o_ref = rest[len(after)]
        step = pl.program_id(2)

        def finish(r):
            o_ref[...] = r.astype(o_ref.dtype)

        def use(a_ref):
            part = lax.dot_general(a_ref[...].astype(BF16), b_ref[...], NT_DIMS, preferred_element_type=F32)
            _accumulate(rest[-1], step, nn, part, finish)

        _for_part(step // each, a_refs, use)

    return pl.pallas_call(
        body,
        name=name,
        grid=(m // tm, k // tko, nn),
        in_specs=[_part_spec((tm, tn), p, each, lambda i, j, s: (i, s)) for p in range(len(parts))]
        + [pl.BlockSpec((None, tko, tn), lambda i, j, s: (s // per, j, s % per))]
        + [pl.BlockSpec(t.shape, lambda i, j, s: (0, 0)) for t in after],
        out_specs=pl.BlockSpec((tm, tko), lambda i, j, s: (i, j)),
        out_shape=jax.ShapeDtypeStruct((m, k), out_dtype),
        scratch_shapes=[pltpu.VMEM((tm, tko), F32)] * (nn > 1),
        compiler_params=_params("parallel", "parallel", "arbitrary"),
    )(*parts, b, *after)


def _for_part(which, refs, use):
    if len(refs) == 1:
        use(refs[0])
        return
    for p, ref in enumerate(refs):
        pl.when(which == p)(functools.partial(use, ref))


def _part_spec(block, p, each, tile_of):
    def index(*ids):
        r, c = tile_of(*ids)
        return r, jnp.clip(c - p * each, 0, each - 1)

    return pl.BlockSpec(block, index)


def _mm_tn(a, g, nc, *, name):
    parts = tuple(g) if isinstance(g, (tuple, list)) else (g,)
    s, k = a.shape
    width = parts[0].shape[1]
    ncol = len(parts) * width // nc
    ts, tn = _pick(s, MM_CONTRACT, 16), _pick(math.gcd(ncol, width), 1536)
    per, each, ns = ncol // tn, width // tn, s // ts
    tko = _mm_rows(k, lambda t: [(ts, t, a.dtype.itemsize, 2), (ts, tn, parts[0].dtype.itemsize, 2 * len(parts)),
                                 (t, tn, 2, 2), (t, tn, 4, ns > 1)], mult=LANES)

    def body(a_ref, *refs):
        g_refs, rest = refs[: len(parts)], refs[len(parts) :]
        o_ref = rest[0]

        def finish(r):
            o_ref[...] = r.astype(o_ref.dtype)

        def use(g_ref):
            part = lax.dot_general(a_ref[...].astype(BF16), g_ref[...].astype(BF16), TN_DIMS, preferred_element_type=F32)
            _accumulate(rest[-1], pl.program_id(2), ns, part, finish)

        _for_part(pl.program_id(1) // each, g_refs, use)

    return pl.pallas_call(
        body,
        name=name,
        grid=(k // tko, nc * per, ns),
        in_specs=[pl.BlockSpec((ts, tko), lambda i, j, t: (t, i))]
        + [_part_spec((ts, tn), p, each, lambda i, j, t: (t, j)) for p in range(len(parts))],
        out_specs=pl.BlockSpec((None, tko, tn), lambda i, j, t: (j // per, i, j % per)),
        out_shape=jax.ShapeDtypeStruct((nc, k, ncol), BF16),
        scratch_shapes=[pltpu.VMEM((tko, tn), F32)] * (ns > 1),
        compiler_params=_params("parallel", "parallel", "arbitrary"),
    )(a, *parts)


ROW_TILE = 256


def _cast_slab(w, slab, *, name):
    _, rows, cols = w.shape
    tr = _pick(rows, ROW_TILE, 16)

    def body(w_ref, o_ref):
        o_ref[...] = w_ref[...].astype(o_ref.dtype)

    return pl.pallas_call(
        body,
        name=name,
        grid=(rows // tr,),
        in_specs=[pl.BlockSpec((None, tr, cols), lambda i: (slab, i, 0))],
        out_specs=pl.BlockSpec((tr, cols), lambda i: (i, 0)),
        out_shape=jax.ShapeDtypeStruct((rows, cols), BF16),
        compiler_params=_params("parallel"),
    )(w)


def _rms_fwd(h, g, *, name):
    s, d = h.shape
    tr = _pick(s, 2 * ROW_TILE, 16)

    def body(h_ref, g_ref, o_ref):
        x = h_ref[...]
        r = lax.rsqrt(jnp.mean(x * x, axis=-1, keepdims=True) + EPS)
        o_ref[...] = (x * r * g_ref[...]).astype(o_ref.dtype)

    return pl.pallas_call(
        body,
        name=name,
        grid=(s // tr,),
        in_specs=[pl.BlockSpec((tr, d), lambda i: (i, 0)), pl.BlockSpec((1, d), lambda i: (0, 0))],
        out_specs=pl.BlockSpec((tr, d), lambda i: (i, 0)),
        out_shape=jax.ShapeDtypeStruct((s, d), BF16),
        compiler_params=_params("parallel"),
    )(h, g)


def _rms_bwd(h, g, dy, dres, *, name):
    s, d = h.shape
    tr = _pick(s, ROW_TILE, 16)

    def body(h_ref, g_ref, dy_ref, dres_ref, dh_ref, dhb_ref, dg_ref):
        x = h_ref[...]
        r = lax.rsqrt(jnp.mean(x * x, axis=-1, keepdims=True) + EPS)
        xhat = x * r
        dyv = dy_ref[...].astype(F32)
        dyg = dyv * g_ref[...]
        dx = r * (dyg - xhat * jnp.mean(dyg * xhat, axis=-1, keepdims=True))
        dh = dres_ref[...] + dx
        dh_ref[...] = dh
        dhb_ref[...] = dh.astype(dhb_ref.dtype)
        part = jnp.sum(dyv * xhat, axis=0, keepdims=True)

        @pl.when(pl.program_id(0) == 0)
        def _():
            dg_ref[...] = part

        @pl.when(pl.program_id(0) != 0)
        def _():
            dg_ref[...] += part

    row = pl.BlockSpec((tr, d), lambda i: (i, 0))
    vec = pl.BlockSpec((1, d), lambda i: (0, 0))
    return pl.pallas_call(
        body,
        name=name,
        grid=(s // tr,),
        in_specs=[row, vec, row, row],
        out_specs=[row, row, vec],
        out_shape=[jax.ShapeDtypeStruct((s, d), F32), jax.ShapeDtypeStruct((s, d), BF16),
                   jax.ShapeDtypeStruct((1, d), F32)],
        compiler_params=_params("arbitrary"),
    )(h, g, dy, dres)


def _loss_head(h, g, target, *, name):
    s, d = h.shape
    tr = _pick(s, ROW_TILE, 16)

    def body(h_ref, g_ref, t_ref, dh_ref, dhb_ref, dg_ref, loss_ref):
        x = h_ref[...]
        r = lax.rsqrt(jnp.mean(x * x, axis=-1, keepdims=True) + EPS)
        xhat = x * r
        gv = g_ref[...]
        err = xhat * gv - t_ref[...]
        dyv = err * (1.0 / d)
        dyg = dyv * gv
        dh = r * (dyg - xhat * jnp.mean(dyg * xhat, axis=-1, keepdims=True))
        dh_ref[...] = dh
        dhb_ref[...] = dh.astype(dhb_ref.dtype)
        part = jnp.sum(dyv * xhat, axis=0, keepdims=True)
        lpart =jnp.sum(jnp.sum(err * err, axis=-1, keepdims=True), axis=0, keepdims=True) * (0.5 / d)

        @pl.when(pl.program_id(0) == 0)
        def _():
            dg_ref[...] = part
            loss_ref[...] = jnp.broadcast_to(lpart, loss_ref.shape)

        @pl.when(pl.program_id(0) != 0)
        def _():
            dg_ref[...] += part
            loss_ref[...] += jnp.broadcast_to(lpart, loss_ref.shape)

    row = pl.BlockSpec((tr, d), lambda i: (i, 0))
    vec = pl.BlockSpec((1, d), lambda i: (0, 0))
    return pl.pallas_call(
        body,
        name=name,
        grid=(s // tr,),
        in_specs=[row, vec, row],
        out_specs=[row, row, vec, pl.BlockSpec((1, LANES), lambda i: (0, 0))],
        out_shape=[
            jax.ShapeDtypeStruct((s, d), F32),
            jax.ShapeDtypeStruct((s, d), BF16),
            jax.ShapeDtypeStruct((1, d), F32),
            jax.ShapeDtypeStruct((1, LANES), F32),
        ],
        compiler_params=_params("arbitrary"),
    )(h, g, target)


def _shift_prev(x):
    row = lax.broadcasted_iota(jnp.int32, x.shape, 0)
    return jnp.where(row == 0, 0.0, pltpu.roll(x, 1, 0))


def _shift_next(x):
    n = x.shape[0]
    row = lax.broadcasted_iota(jnp.int32, x.shape, 0)
    return jnp.where(row == n - 1, 0.0, pltpu.roll(x, n - 1, 0))


def _conv3(x, w):
    xm, xp = _shift_prev(x), _shift_next(x)
    return xm * w[0:1] + x * w[1:2] + xp * w[2:3], xm, xp


def _conv3_t(d, w):
    return _shift_next(d) * w[0:1] + d * w[1:2] + _shift_prev(d) * w[2:3]


def _colsum(x):
    return jnp.sum(x, axis=0, keepdims=True)


def _ffn_mid_fwd(up, cw, cb, *, name):
    s, f2 = up.shape
    f = f2 // 2
    ncol = cw.shape[2]
    tc = _pick(ncol, 256)
    nt, per = f // tc, ncol // tc

    def body(ug_ref, uu_ref, wg_ref, wu_ref, bg_ref, bu_ref, o_ref, cg_ref, cu_ref):
        cg = _conv3(ug_ref[...].astype(F32), wg_ref[...])[0] + bg_ref[...]
        cu = _conv3(uu_ref[...].astype(F32), wu_ref[...])[0] + bu_ref[...]
        o_ref[...] = (cg * (1.0 / (1.0 + jnp.exp(-cg))) * cu).astype(o_ref.dtype)
        cg_ref[...] = cg.astype(cg_ref.dtype)
        cu_ref[...] = cu.astype(cu_ref.dtype)

    out = pl.BlockSpec((s, tc), lambda j: (0, j))
    return pl.pallas_call(
        body,
        name=name,
        grid=(nt,),
        in_specs=[
            pl.BlockSpec((s, tc), lambda j: (0, j)),
            pl.BlockSpec((s, tc), lambda j: (0, nt + j)),
            pl.BlockSpec((None, 3, tc), lambda j: (j // per, 0, j % per)),
            pl.BlockSpec((None, 3, tc), lambda j: ((nt + j) // per, 0, (nt + j) % per)),
            pl.BlockSpec((1, tc), lambda j: (0, j)),
            pl.BlockSpec((1, tc), lambda j: (0, nt + j)),
        ],
        out_specs=[out, out, out],
        out_shape=[jax.ShapeDtypeStruct((s, f), BF16)] * 3,
        compiler_params=_params("parallel"),
    )(up, up, cw, cw, cb, cb)


def _ffn_mid_bwd(up, cg, cu, cw, dact, *, name):
    s, f2 = up.shape
    f = f2 // 2
    ncol = cw.shape[2]
    tc = _pick(ncol, 256)
    nt, per = f // tc, ncol // tc

    def side(dc, u_ref, w_ref, du_ref, dw_ref, db_ref):
        w, u = w_ref[...], u_ref[...].astype(F32)
        nxt, prv = _shift_next(dc), _shift_prev(dc)
        du_ref[...] = (nxt * w[0:1] + dc * w[1:2] + prv * w[2:3]).astype(du_ref.dtype)
        dw_ref[0:1, :] = _colsum(nxt * u)
        dw_ref[1:2, :] = _colsum(dc * u)
        dw_ref[2:3, :] = _colsum(prv * u)
        db_ref[...] = _colsum(dc)

    def body(ug_ref, uu_ref, cg_ref, cu_ref, wg_ref, wu_ref, da_ref, dug_ref, duu_ref, dwg_ref, dwu_ref, dbg_ref, dbu_ref):
        cgv, cuv, da = cg_ref[...].astype(F32), cu_ref[...].astype(F32), da_ref[...].astype(F32)
        sig = 1.0 / (1.0 + jnp.exp(-cgv))
        side(da * cuv * (sig * (1.0 + cgv * (1.0 - sig))), ug_ref, wg_ref, dug_ref, dwg_ref, dbg_ref)
        side(da * (cgv * sig), uu_ref, wu_ref, duu_ref, dwu_ref, dbu_ref)

    col = pl.BlockSpec((s, tc), lambda j: (0, j))
    w3 = pl.BlockSpec((3, tc), lambda j: (0, j))
    b1 = pl.BlockSpec((1, tc), lambda j: (0, j))
    return pl.pallas_call(
        body,
        name=name,
        grid=(nt,),
        in_specs=[
            col,
            pl.BlockSpec((s, tc), lambda j: (0, nt + j)),
            col,
            col,
            pl.BlockSpec((None, 3, tc), lambda j: (j // per, 0, j % per)),
            pl.BlockSpec((None, 3, tc), lambda j: ((nt + j) // per, 0, (nt + j) % per)),
            col,
        ],
        out_specs=[col, col, w3, w3, b1, b1],
        out_shape=[
            jax.ShapeDtypeStruct((s, f), BF16),
            jax.ShapeDtypeStruct((s, f), BF16),
            jax.ShapeDtypeStruct((3, f), F32),
            jax.ShapeDtypeStruct((3, f), F32),
            jax.ShapeDtypeStruct((1, f), F32),
            jax.ShapeDtypeStruct((1, f), F32),
        ],
        compiler_params=_params("parallel"),
    )(up, up, cg, cu, cw, cw, dact)


def _sc_mid_fwd(z, cw, *, name):
    s, d3 = z.shape
    d = d3 // 3
    ncol = cw.shape[2]
    tc = _pick(ncol, 256)
    nt, per = d // tc, ncol // tc

    def body(gb_ref, gc_ref, hh_ref, w_ref, o_ref):
        p = gc_ref[...].astype(F32) * hh_ref[...].astype(F32)
        o_ref[...] = (gb_ref[...].astype(F32) * _conv3(p, w_ref[...])[0]).astype(o_ref.dtype)

    return pl.pallas_call(
        body,
        name=name,
        grid=(nt,),
        in_specs=[
            pl.BlockSpec((s, tc), lambda j: (0, j)),
            pl.BlockSpec((s, tc), lambda j: (0, nt + j)),
            pl.BlockSpec((s, tc), lambda j: (0, 2 * nt + j)),
            pl.BlockSpec((None, 3, tc), lambda j: (j // per, 0, j % per)),
        ],
        out_specs=pl.BlockSpec((s, tc), lambda j: (0, j)),
        out_shape=jax.ShapeDtypeStruct((s, d), BF16),
        compiler_params=_params("parallel"),
    )(z, z, z, cw)


def _sc_mid_bwd(z, cw, dmid, *, name):
    s, d3 = z.shape
    d = d3 // 3
    ncol = cw.shape[2]
    tc = _pick(ncol, 256)
    nt, per = d // tc, ncol // tc

    def body(gb_ref, gc_ref, hh_ref, w_ref, dm_ref, dgb_ref, dgc_ref, dhh_ref, dw_ref):
        gb, gc, hh = gb_ref[...].astype(F32), gc_ref[...].astype(F32), hh_ref[...].astype(F32)
        w = w_ref[...]
        p = gc * hh
        cv, pm, pp = _conv3(p, w)
        dm = dm_ref[...].astype(F32)
        dgb_ref[...] = (dm * cv).astype(dgb_ref.dtype)
        dcv = dm * gb
        dp = _conv3_t(dcv, w)
        dgc_ref[...] = (dp * hh).astype(dgc_ref.dtype)
        dhh_ref[...] = (dp * gc).astype(dhh_ref.dtype)
        dw_ref[0:1, :] = _colsum(dcv * pm)
        dw_ref[1:2, :] = _colsum(dcv * p)
        dw_ref[2:3, :] = _colsum(dcv * pp)

    col = pl.BlockSpec((s, tc), lambda j: (0, j))
    return pl.pallas_call(
        body,
        name=name,
        grid=(nt,),
        in_specs=[
            col,
            pl.BlockSpec((s, tc), lambda j: (0, nt + j)),
            pl.BlockSpec((s, tc), lambda j: (0, 2 * nt + j)),
            pl.BlockSpec((None, 3, tc), lambda j: (j // per, 0, j % per)),
            col,
        ],
        out_specs=[col, col, col, pl.BlockSpec((3, tc), lambda j: (0, j))],
        out_shape=[jax.ShapeDtypeStruct((s, d), BF16)] * 3 + [jax.ShapeDtypeStruct((3, d), F32)],
        compiler_params=_params("parallel"),
    )(z, z, z, cw, dmid)


NA_KEYS = NA_WIN_R * GRID_W


def _na_row_start(r, rows):
    return jnp.clip(r - NA_WIN_R // 2, 0, rows - NA_WIN_R)


def _na_base(rpb):
    h = rpb.shape[0]
    pos, neg = rpb[:, :, NA_WIN_C - 1:], rpb[:, :, : NA_WIN_C - 1]
    zeros = jnp.zeros((h, NA_WIN_R, GRID_W - 2 * NA_WIN_C + 1), F32)
    out = []
    for first in range(NA_WIN_R):
        p = pos[:, first : first + NA_WIN_R]
        n = jnp.roll(neg[:, first : first + NA_WIN_R], -1, axis=1)
        out.append(jnp.concatenate([p, zeros, n], axis=-1).reshape(h, 1, NA_KEYS))
    return jnp.stack(out, axis=1)


def _skew_right(x):
    return pltpu.roll(x, 0, 1, stride=1, stride_axis=0)


def _skew_left(x):
    n = x.shape[1]
    row = lax.broadcasted_iota(jnp.int32, x.shape, 0)
    for b in range(GRID_W.bit_length() - 1):
        x = jnp.where(((row >> b) & 1) == 1, pltpu.roll(x, n - (1 << b), 1), x)
    return x


def _na_bias(base, *, name):
    h = base.shape[0]

    def body(b_ref, o_ref):
        q = lax.broadcasted_iota(jnp.int32, (GRID_W, NA_KEYS), 0)
        kc = lax.broadcasted_iota(jnp.int32, (GRID_W, NA_KEYS), 1) % GRID_W
        start = jnp.clip(q - NA_WIN_C // 2, 0, GRID_W - NA_WIN_C)
        inside = (kc >= start) & (kc < start + NA_WIN_C)
        for slot in range(NA_WIN_R):
            x = _skew_right(jnp.broadcast_to(b_ref[slot], (GRID_W, NA_KEYS)))
            o_ref[slot] = jnp.where(inside, x, NEG_INF)

    return pl.pallas_call(
        body,
        name=name,
        grid=(h,),
        in_specs=[pl.BlockSpec((None, NA_WIN_R, 1, NA_KEYS), lambda i: (i, 0, 0, 0))],
        out_specs=pl.BlockSpec((None, NA_WIN_R, GRID_W, NA_KEYS), lambda i: (i, 0, 0, 0)),
        out_shape=jax.ShapeDtypeStruct((h, NA_WIN_R, GRID_W, NA_KEYS), F32),
        compiler_params=_params("parallel"),
    )(base)


NA_STEP_ROWS = 4
NA_BLOCK_ROWS = NA_STEP_ROWS + NA_WIN_R
NA_STEP_Q = NA_STEP_ROWS * GRID_W
NA_BLOCK_K = NA_BLOCK_ROWS * GRID_W
NA_PATTERNS = 3


def _na_plan(rows):
    steps = rows // NA_STEP_ROWS

    def start(r):
        return min(max(r - NA_WIN_R // 2, 0), rows - NA_WIN_R)

    plan = []
    for t in range(steps):
        first = [start(NA_STEP_ROWS * t + i) for i in range(NA_STEP_ROWS)]
        block = min(first[0], rows - NA_BLOCK_ROWS)
        offsets = tuple(f - block for f in first)
        slots = tuple(f - (NA_STEP_ROWS * t + i) + NA_WIN_R - 1 for i, f in enumerate(first))
        plan.append((offsets, slots))
    middle = plan[1] if steps > 2 else (tuple(range(NA_STEP_ROWS)), (NA_WIN_R // 2 - 1,) * NA_STEP_ROWS)
    assert steps >= 2 and all(p == middle for p in plan[1:-1])
    return steps, (plan[0], middle, plan[-1])


def _na_block_start(t, rows):
    return jnp.minimum(_na_row_start(NA_STEP_ROWS * t, rows), rows - NA_BLOCK_ROWS)


def _na_pattern(t, steps):
    return jnp.where(t == 0, 0, jnp.where(t == steps - 1, 2, 1))


def _na_block_bias(bias, rows, *, name):
    h = bias.shape[0]
    _, patterns = _na_plan(rows)

    def body(b_ref, o_ref):
        outside = jnp.full((GRID_W, NA_BLOCK_K - NA_KEYS), NEG_INF, F32)
        for p, (offsets, slots) in enumerate(patterns):

            @pl.when(pl.program_id(1) == p)
            def _(offsets=offsets, slots=slots):
                for i in range(NA_STEP_ROWS):
                    wide = jnp.concatenate([b_ref[slots[i]], outside], axis=1)
                    shift = offsets[i] * GRID_W
                    o_ref[i * GRID_W : (i + 1) * GRID_W, :] = pltpu.roll(wide, shift, 1) if shift else wide

    return pl.pallas_call(
        body,
        name=name,
        grid=(h, NA_PATTERNS),
        in_specs=[pl.BlockSpec((None, NA_WIN_R, GRID_W, NA_KEYS), lambda i, p: (i, 0, 0, 0))],
        out_specs=pl.BlockSpec((None, None, NA_STEP_Q, NA_BLOCK_K), lambda i, p: (i, p, 0, 0)),
        out_shape=jax.ShapeDtypeStruct((h, NA_PATTERNS, NA_STEP_Q, NA_BLOCK_K), F32),
        compiler_params=_params("parallel", "arbitrary"),
    )(bias)


def _na_unblock(dblock, rows, *, name):
    h = dblock.shape[0]
    steps, patterns = _na_plan(rows)
    used = [p for p in range(NA_PATTERNS) if p != 1 or steps > 2]

    def body(d_ref, o_ref):
        total = {}
        for p in used:
            offsets, slots = patterns[p]
            for i in range(NA_STEP_ROWS):
                wide = d_ref[p, i * GRID_W : (i + 1) * GRID_W, :]
                back = (NA_BLOCK_K - offsets[i] * GRID_W) % NA_BLOCK_K
                piece = (pltpu.roll(wide, back, 1) if back else wide)[:, :NA_KEYS]
                total[slots[i]] = piece if slots[i] not in total else total[slots[i]] + piece
        for slot in range(NA_WIN_R):
            o_ref[slot] = total.get(slot, jnp.zeros((GRID_W, NA_KEYS), F32))

    return pl.pallas_call(
        body,
        name=name,
        grid=(h,),
        in_specs=[pl.BlockSpec((None, NA_PATTERNS, NA_STEP_Q, NA_BLOCK_K), lambda i: (i, 0, 0, 0))],
        out_specs=pl.BlockSpec((None, NA_WIN_R, GRID_W, NA_KEYS), lambda i: (i, 0, 0, 0)),
        out_shape=jax.ShapeDtypeStruct((h, NA_WIN_R, GRID_W, NA_KEYS), F32),
        compiler_params=_params("parallel"),
    )(dblock)


def _na_probs(q, k, bias):
    sc = lax.dot_general(q, k, NT_DIMS, preferred_element_type=F32) * (HEAD_DIM ** -0.5) + bias
    p = jnp.exp(sc - jnp.max(sc, axis=-1, keepdims=True))
    return p, jnp.sum(p, axis=-1, keepdims=True)


def _na_fwd(qkv, block_bias, *, name):
    s = qkv.shape[0]
    heads = qkv.shape[1] // (3 * HEAD_DIM)
    rows = s // GRID_W
    steps, _ = _na_plan(rows)

    per = next(n for n in (4, 2, 1) if heads % n == 0)
    width = per * HEAD_DIM

    def body(q_ref, k_ref, v_ref, b_ref, o_ref):
        start = _na_block_start(pl.program_id(1), rows)
        block = pl.ds(pl.multiple_of(start * GRID_W, GRID_W), NA_BLOCK_K)
        for g in range(per):
            cols = slice(g * HEAD_DIM, (g + 1) * HEAD_DIM)
            p, l = _na_probs(q_ref[:, cols], k_ref[block, cols], b_ref[g])
            o = jnp.dot(p.astype(BF16), v_ref[block, cols], preferred_element_type=F32)
            o_ref[:, cols] = (o / l).astype(o_ref.dtype)

    q = pl.BlockSpec((NA_STEP_Q, width), lambda h, t: (t, h))
    k = pl.BlockSpec((s, width), lambda h, t: (0, heads // per + h))
    v = pl.BlockSpec((s, width), lambda h, t: (0, 2 * heads // per + h))
    b = pl.BlockSpec((per, None, NA_STEP_Q, NA_BLOCK_K), lambda h, t: (h, _na_pattern(t, steps), 0, 0))
    return pl.pallas_call(
        body,
        name=name,
        grid=(heads // per, steps),
        in_specs=[q, k, v, b],
        out_specs=q,
        out_shape=jax.ShapeDtypeStruct((s, heads * HEAD_DIM), BF16),
        compiler_params=_params("parallel", "arbitrary"),
    )(qkv, qkv, qkv, block_bias)


def _na_bwd(qkv, block_bias, dout, *, name):
    s = qkv.shape[0]
    heads = qkv.shape[1] // (3 * HEAD_DIM)
    rows = s // GRID_W
    steps, _ = _na_plan(rows)
    scale = HEAD_DIM ** -0.5
    per = 2 if heads % 2 == 0 else 1
    width = per * HEAD_DIM

    def body(q_ref, k_ref, v_ref, b_ref, do_ref, dq_ref, dk_ref, dv_ref, db_ref, dk_acc, dv_acc):
        step = pl.program_id(1)

        @pl.when(step == 0)
        def _():
            dk_acc[...] = jnp.zeros_like(dk_acc)
            dv_acc[...] = jnp.zeros_like(dv_acc)

        block = pl.ds(pl.multiple_of(_na_block_start(step, rows) * GRID_W, GRID_W), NA_BLOCK_K)
        opens = (step <= 1) | (step == steps - 1)
        for g in range(per):
            cols = slice(g * HEAD_DIM, (g + 1) * HEAD_DIM)
            q, k, v, do = q_ref[:, cols], k_ref[block, cols], v_ref[block, cols], do_ref[:, cols]
            p, l = _na_probs(q, k, b_ref[g])
            pn = p / l
            dp = lax.dot_general(do, v, NT_DIMS, preferred_element_type=F32)
            ds = pn * (dp - jnp.sum(pn * dp, axis=-1, keepdims=True))
            dsb = ds.astype(BF16)
            dq_ref[:, cols] = (jnp.dot(dsb, k, preferred_element_type=F32) * scale).astype(dq_ref.dtype)
            dk_acc[block, cols] += lax.dot_general(dsb, q, TN_DIMS, preferred_element_type=F32) * scale
            dv_acc[block, cols] += lax.dot_general(pn.astype(BF16), do, TN_DIMS, preferred_element_type=F32)

            @pl.when(opens)
            def _(g=g, ds=ds):
                db_ref[g] = ds

            @pl.when(jnp.logical_not(opens))
            def _(g=g, ds=ds):
                db_ref[g] += ds

        @pl.when(step == steps - 1)
        def _():
            dk_ref[...] = dk_acc[...].astype(dk_ref.dtype)
            dv_ref[...] = dv_acc[...].astype(dv_ref.dtype)

    q = pl.BlockSpec((NA_STEP_Q, width), lambda h, t: (t, h))
    k = pl.BlockSpec((s, width), lambda h, t: (0, heads // per + h))
    v = pl.BlockSpec((s, width), lambda h, t: (0, 2 * heads // per + h))
    b = pl.BlockSpec((per, None, NA_STEP_Q, NA_BLOCK_K), lambda h, t: (h, _na_pattern(t, steps), 0, 0))
    kv_out = pl.BlockSpec((s, width), lambda h, t: (0, h))
    shape = jax.ShapeDtypeStruct((s, heads * HEAD_DIM), BF16)
    return pl.pallas_call(
        body,
        name=name,
        grid=(heads // per, steps),
        in_specs=[q, k, v, b, q],
        out_specs=[q, kv_out, kv_out, b],
        out_shape=[shape, shape, shape, jax.ShapeDtypeStruct((heads, NA_PATTERNS, NA_STEP_Q, NA_BLOCK_K), F32)],
        scratch_shapes=[pltpu.VMEM((s, width), F32), pltpu.VMEM((s, width), F32)],
        compiler_params=_params("parallel", "arbitrary"),
    )(qkv, qkv, qkv, block_bias, dout)


RPB_ROWS = 2 * NA_WIN_R - 1
RPB_COLS = 2 * NA_WIN_C - 1
RPB_PAD = 512


def _rpb_fold_matrix():
    idx = jnp.arange(NA_WIN_R * NA_KEYS, dtype=jnp.int32)
    first, i, kc = idx // NA_KEYS, (idx // GRID_W) % NA_WIN_R, idx % GRID_W
    pos, neg = kc < NA_WIN_C, kc >= GRID_W - NA_WIN_C + 1
    dr = jnp.where(pos, first + i, first + (i + 1) % NA_WIN_R)
    dc = jnp.where(pos, kc + NA_WIN_C - 1, kc - (GRID_W - NA_WIN_C + 1))
    target = jnp.where(pos | neg, dr * RPB_COLS + dc, -1)
    return (target[:, None] == jnp.arange(RPB_PAD, dtype=jnp.int32)[None, :]).astype(F32)


def _rpb_fold(dbias, *, name):
    h = dbias.shape[0]

    def skew_body(g_ref, o_ref):
        for slot in range(NA_WIN_R):
            o_ref[slot] = _colsum(_skew_left(g_ref[slot]))

    skewed = pl.pallas_call(
        skew_body,
        name=name + "_skew",
        grid=(h,),
        in_specs=[pl.BlockSpec((None, NA_WIN_R, GRID_W, NA_KEYS), lambda i: (i, 0, 0, 0))],
        out_specs=pl.BlockSpec((None, NA_WIN_R, 1, NA_KEYS), lambda i: (i, 0, 0, 0)),
        out_shape=jax.ShapeDtypeStruct((h, NA_WIN_R, 1, NA_KEYS), F32),
        compiler_params=_params("parallel"),
    )(dbias)

    def fold_body(g_ref, m_ref, o_ref):
        o_ref[...] = jnp.dot(g_ref[...], m_ref[...], preferred_element_type=F32, precision=lax.Precision.HIGHEST)

    return pl.pallas_call(
        fold_body,
        name=name,
        out_shape=jax.ShapeDtypeStruct((h, RPB_PAD), F32),
        compiler_params=pltpu.CompilerParams(vmem_limit_bytes=VMEM_LIMIT),
    )(skewed.reshape(h, NA_WIN_R * NA_KEYS), _rpb_fold_matrix())


GQA_Q_TILE = 512
GQA_Q_TILE_BWD = 512


def _rope_tables(s):
    t = jnp.arange(s)
    row = (t // GRID_W).astype(F32)[:, None]
    col = (t % GRID_W).astype(F32)[:, None]
    half = HEAD_DIM // 2
    inv = ROPE_THETA ** (-jnp.arange(0, half, 2, dtype=F32) / half)
    ang = jnp.concatenate([row * inv, row * inv, col * inv, col * inv], axis=-1)
    return jnp.cos(ang), jnp.sin(ang)


def _rot_half(y):
    quarter = HEAD_DIM // 4
    lane = lax.broadcasted_iota(jnp.int32, y.shape, 1)
    low = (lane % (2 * quarter)) < quarter
    return jnp.where(low, -pltpu.roll(y, HEAD_DIM - quarter, 1), pltpu.roll(y, quarter, 1))


def _gqa_prep_fwd(qkv, gq, gk, cos, sin, hq, hkv, *, name):
    s = qkv.shape[0]

    def body(x_ref, gq_ref, gk_ref, cos_ref, sin_ref, o_ref):
        isq = pl.program_id(0) < hq
        x = x_ref[...].astype(F32)
        g = jnp.where(isq, gq_ref[...], gk_ref[...])
        y = x * lax.rsqrt(jnp.mean(x * x, axis=-1, keepdims=True) + EPS) * g
        z = y * cos_ref[...] + _rot_half(y) * sin_ref[...]
        o_ref[...] = (z * jnp.where(isq, HEAD_DIM ** -0.5, 1.0)).astype(o_ref.dtype)

    head = pl.BlockSpec((s, HEAD_DIM), lambda h: (0, h))
    vec = pl.BlockSpec((1, HEAD_DIM), lambda h: (0, 0))
    tab = pl.BlockSpec((s, HEAD_DIM), lambda h: (0, 0))
    return pl.pallas_call(
        body,
        name=name,
        grid=(hq + hkv,),
        in_specs=[head, vec, vec, tab, tab],
        out_specs=head,
        out_shape=jax.ShapeDtypeStruct((s, (hq + hkv) * HEAD_DIM), BF16),
        compiler_params=_params("parallel"),
    )(qkv, gq, gk, cos, sin)


def _gqa_prep_bwd(qkv, gq, gk, cos, sin, dqn, dkn, hq, hkv, *, name):
    s = qkv.shape[0]

    def body(x_ref, gq_ref, gk_ref, cos_ref, sin_ref, dq_ref, dk_ref, dx_ref, dgq_ref, dgk_ref):
        hh = pl.program_id(0)
        isq = hh < hq
        x = x_ref[...].astype(F32)
        g = jnp.where(isq, gq_ref[...], gk_ref[...])
        r = lax.rsqrt(jnp.mean(x * x, axis=-1, keepdims=True) + EPS)
        xhat = x * r
        dz = jnp.where(isq, dq_ref[...].astype(F32) * (HEAD_DIM ** -0.5), dk_ref[...].astype(F32))
        dy = dz * cos_ref[...] - _rot_half(dz * sin_ref[...])
        dyg = dy * g
        dx_ref[...] = (r * (dyg - xhat * jnp.mean(dyg * xhat, axis=-1, keepdims=True))).astype(dx_ref.dtype)
        part = _colsum(dy * xhat)

        @pl.when(hh == 0)
        def _():
            dgq_ref[...] = jnp.zeros_like(dgq_ref)
            dgk_ref[...] = jnp.zeros_like(dgk_ref)

        @pl.when(isq)
        def _():
            dgq_ref[...] += part

        @pl.when(jnp.logical_not(isq))
        def _():
            dgk_ref[...] += part

    head = pl.BlockSpec((s, HEAD_DIM), lambda h: (0, h))
    vec = pl.BlockSpec((1, HEAD_DIM), lambda h: (0, 0))
    tab = pl.BlockSpec((s, HEAD_DIM), lambda h: (0, 0))
    return pl.pallas_call(
        body,
        name=name,
        grid=(hq + hkv,),
        in_specs=[
            head,
            vec,
            vec,
            tab,
            tab,
            pl.BlockSpec((s, HEAD_DIM), lambda h: (0, jnp.minimum(h, hq - 1))),
            pl.BlockSpec((s, HEAD_DIM), lambda h: (0, jnp.maximum(h - hq, 0))),
        ],
        out_specs=[head, vec, vec],
        out_shape=[
            jax.ShapeDtypeStruct((s, (hq + hkv) * HEAD_DIM), BF16),
            jax.ShapeDtypeStruct((1, HEAD_DIM), F32),
            jax.ShapeDtypeStruct((1, HEAD_DIM), F32),
        ],
        compiler_params=_params("arbitrary"),
    )(qkv, gq, gk, cos, sin, dqn, dkn)


def _gqa_fwd(qkn, qkv, hq, hkv, *, name):
    s = qkv.shape[0]
    tq = _pick(s, GQA_Q_TILE, 16)

    def body(q_ref, k_ref, v_ref, o_ref):
        k, v = k_ref[...], v_ref[...]
        for g in range(GQA_GROUP):
            cols = slice(g * HEAD_DIM, (g + 1) * HEAD_DIM)
            sc = lax.dot_general(q_ref[:, cols], k, NT_DIMS, preferred_element_type=F32)
            p = jnp.exp(sc - jnp.max(sc, axis=-1, keepdims=True))
            l = jnp.sum(p, axis=-1, keepdims=True)
            o_ref[:, cols] = (jnp.dot(p.astype(BF16), v, preferred_element_type=F32) / l).astype(o_ref.dtype)

    q = pl.BlockSpec((tq, GQA_GROUP * HEAD_DIM), lambda h, i: (i, h))
    return pl.pallas_call(
        body,
        name=name,
        grid=(hkv, s // tq),
        in_specs=[
            q,
            pl.BlockSpec((s, HEAD_DIM), lambda h, i: (0, hq + h)),
            pl.BlockSpec((s, HEAD_DIM), lambda h, i: (0, hq + hkv + h)),
        ],
        out_specs=q,
        out_shape=jax.ShapeDtypeStruct((s, hq * HEAD_DIM), BF16),
        compiler_params=_params("parallel", "parallel"),
    )(qkn, qkn, qkv)


def _gqa_bwd(qkn, qkv, dout, hq, hkv, *, name):
    s = qkv.shape[0]
    tq = _pick(s, GQA_Q_TILE_BWD, 16)
    nq = s // tq

    def body(q_ref, k_ref, v_ref, do_ref, dq_ref, dk_ref, dv_ref, dk_acc, dv_acc):
        g, i = pl.program_id(1), pl.program_id(2)

        @pl.when((g == 0) & (i == 0))
        def _():
            dk_acc[...] = jnp.zeros_like(dk_acc)
            dv_acc[...] = jnp.zeros_like(dv_acc)

        q, k, v, do = q_ref[...], k_ref[...], v_ref[...], do_ref[...]
        sc = lax.dot_general(q, k, NT_DIMS, preferred_element_type=F32)
        p = jnp.exp(sc - jnp.max(sc, axis=-1, keepdims=True))
        pn = p / jnp.sum(p, axis=-1, keepdims=True)
        dp = lax.dot_general(do, v, NT_DIMS, preferred_element_type=F32)
        dsb = (pn * (dp - jnp.sum(pn * dp, axis=-1, keepdims=True))).astype(BF16)
        dq_ref[...] = jnp.dot(dsb, k, preferred_element_type=F32).astype(dq_ref.dtype)
        dk_acc[...] += lax.dot_general(dsb, q, TN_DIMS, preferred_element_type=F32)
        dv_acc[...] += lax.dot_general(pn.astype(BF16), do, TN_DIMS, preferred_element_type=F32)

        @pl.when((g == GQA_GROUP - 1) & (i == nq - 1))
        def _():
            dk_ref[...] = dk_acc[...].astype(dk_ref.dtype)
            dv_ref[...] = dv_acc[...].astype(dv_ref.dtype)

    q = pl.BlockSpec((tq, HEAD_DIM), lambda kv, g, i: (i, kv * GQA_GROUP + g))
    kv_out = pl.BlockSpec((s, HEAD_DIM), lambda kv, g, i: (0, kv))
    return pl.pallas_call(
        body,
        name=name,
        grid=(hkv, GQA_GROUP, nq),
        in_specs=[
            q,
            pl.BlockSpec((s, HEAD_DIM), lambda kv, g, i: (0, hq + kv)),
            pl.BlockSpec((s, HEAD_DIM), lambda kv, g, i: (0, hq + hkv + kv)),
            q,
        ],
        out_specs=[q, kv_out, kv_out],
        out_shape=[
            jax.ShapeDtypeStruct((s, hq * HEAD_DIM), BF16),
            jax.ShapeDtypeStruct((s, hkv * HEAD_DIM), BF16),
            jax.ShapeDtypeStruct((s, hkv * HEAD_DIM), BF16),
        ],
        scratch_shapes=[pltpu.VMEM((s, HEAD_DIM), F32), pltpu.VMEM((s, HEAD_DIM), F32)],
        compiler_params=_params("parallel", "arbitrary", "arbitrary"),
    )(qkn, qkn, qkv, dout)


ADAM_ROWS = 128


def _adam_update(w, g, m, v):
    m = ADAM_B1 * m + (1.0 - ADAM_B1) * g
    v = ADAM_B2 * v + (1.0 - ADAM_B2) * (g * g)
    m_hat = m / (1.0 - ADAM_B1 ** ADAM_STEP)
    v_hat = v / (1.0 - ADAM_B2 ** ADAM_STEP)
    return -ADAM_LR * (m_hat / (jnp.sqrt(v_hat) + ADAM_EPS) + ADAM_WD * w), m, v


def _adamw_slab(w, m, v, own, sib, slab, prev, after, *, name):
    _, rows, cols = w.shape
    tr = _pick(rows, ADAM_ROWS, 16)
    tokens = [] if after is None else [after]

    def body(w_ref, m_ref, v_ref, own_ref, sib_ref, *rest):
        g_ref, d_ref, nm_ref, nv_ref = rest[-4:]
        g = own_ref[0].astype(F32) + sib_ref[0].astype(F32)
        for q in range(1, N_CHIPS):
            g = g + (own_ref[q].astype(F32) + sib_ref[q].astype(F32))
        g_ref[...] = g
        d_ref[...], nm_ref[...], nv_ref[...] = _adam_update(w_ref[...], g, m_ref[...], v_ref[...])

    one = pl.BlockSpec((None, tr, cols), lambda i: (slab, i, 0))
    piece = pl.BlockSpec((N_CHIPS, tr, cols), lambda i: (0, i, 0))
    carried = [] if prev is None else list(prev)
    shape = jax.ShapeDtypeStruct(w.shape, F32)
    return pl.pallas_call(
        body,
        name=name,
        grid=(rows // tr,),
        in_specs=[one] * 3
        + [piece] * 2
        + [pl.BlockSpec(memory_space=pl.ANY)] * len(carried)
        + [pl.BlockSpec(t.shape, lambda i: (0, 0)) for t in tokens],
        out_specs=[one] * 4,
        out_shape=[shape] * 4,
        input_output_aliases={5 + i: i for i in range(len(carried))},
        compiler_params=_params("parallel"),
    )(w, m, v, own, sib, *carried, *tokens)


def _adamw_small(w, g, m, v, *, name):
    def body(w_ref, g_ref, m_ref, v_ref, d_ref, nm_ref, nv_ref):
        d_ref[...], nm_ref[...], nv_ref[...] = _adam_update(w_ref[...], g_ref[...], m_ref[...], v_ref[...])

    shape = jax.ShapeDtypeStruct(w.shape, F32)
    return pl.pallas_call(
        body,
        name=name,
        out_shape=[shape] * 3,
        compiler_params=pltpu.CompilerParams(vmem_limit_bytes=VMEM_LIMIT),
    )(w, g, m, v)


def _position():
    x, y, c = lax.axis_index("x"), lax.axis_index("y"), lax.axis_index("c")
    return x, y, c, 2 * x + y


def _chip_device(chip, c):
    return (chip >> 1, chip & 1, c)


def _handshake(peers):
    barrier = pltpu.get_barrier_semaphore()
    for peer in peers:
        pl.semaphore_signal(barrier, inc=1, device_id=peer, device_id_type=MESH)
    pl.semaphore_wait(barrier, len(peers))


GATHER_CHUNKS = 2


def _gather_shards(shards, split, *, name, collective_id):
    n = len(shards)
    big = [a for a in range(n) if split[a]]
    small = [a for a in range(n) if not split[a]]
    y_nbr, x_nbr, far = 1, 2, 3

    def body(*refs):
        ins, outs = refs[:n], refs[n : 2 * n]
        near_send, near_recv, far_send, far_recv, pass_send, pass_recv, own_send, own_recv = refs[2 * n :]
        x, y, c, k = _position()
        _handshake([(x, y, 1 - c)] + [_chip_device(k ^ j, c) for j in range(1, N_CHIPS)])

        def own(a, chunk):
            r = shards[a].shape[0]
            part = pl.ds(chunk * (r // GATHER_CHUNKS), r // GATHER_CHUNKS) if split[a] else pl.ds(0, r)
            return pltpu.make_async_remote_copy(
                src_ref=ins[a].at[part],
                dst_ref=outs[a].at[k, part],
                send_sem=own_send.at[a, chunk],
                recv_sem=own_recv.at[a, chunk],
                device_id=(x, y, 1 - c),
                device_id_type=MESH,
            )

        own_copies = [own(a, ch) for a in range(n) for ch in range(GATHER_CHUNKS if split[a] else 1)]
        for cp in own_copies:
            cp.start()

        def run(core):
            sibling = (x, y, 1 - core)
            relay_from, relay_to = (x_nbr, y_nbr) if core == 0 else (y_nbr, x_nbr)

            def rows(a, which, chunk=None):
                r = shards[a].shape[0]
                if not split[a]:
                    return pl.ds(0, r)
                half = r // 2
                if chunk is None:
                    return pl.ds(which * half, half)
                return pl.ds(which * half + chunk * (half // GATHER_CHUNKS), half // GATHER_CHUNKS)

            def direct(a, mask, chunk, src_chip):
                part = rows(a, core, chunk)
                return pltpu.make_async_remote_copy(
                    src_ref=ins[a].at[part],
                    dst_ref=outs[a].at[src_chip, part],
                    send_sem=near_send.at[a, mask - 1, chunk or 0],
                    recv_sem=near_recv.at[a, mask - 1, chunk or 0],
                    device_id=_chip_device(k ^ mask, core),
                    device_id_type=MESH,
                )

            def relay(a, chunk, src_chip, mask):
                part = outs[a].at[src_chip, rows(a, core, chunk)]
                return pltpu.make_async_remote_copy(
                    src_ref=part,
                    dst_ref=part,
                    send_sem=far_send.at[a, chunk or 0],
                    recv_sem=far_recv.at[a, chunk or 0],
                    device_id=_chip_device(k ^ mask, core),
                    device_id_type=MESH,
                )

            def to_sibling(a, mask, which):
                part = outs[a].at[k ^ mask, rows(a, which)]
                return pltpu.make_async_remote_copy(
                    src_ref=part,
                    dst_ref=part,
                    send_sem=pass_send.at[a, mask - 1],
                    recv_sem=pass_recv.at[a, mask - 1],
                    device_id=sibling,
                    device_id_type=MESH,
                )

            sent = []

            def start(cp):
                cp.start()
                sent.append(cp)

            chunks = range(GATHER_CHUNKS)
            for chunk in chunks:
                for a in big:
                    start(direct(a, relay_from, chunk, k))
                    start(direct(a, relay_to, chunk, k))
            for a in small:
                start(direct(a, x_nbr, None, k))
                start(direct(a, y_nbr, None, k))
                start(pltpu.make_async_remote_copy(
                    src_ref=ins[a], dst_ref=outs[a].at[k], send_sem=far_send.at[a, 0], recv_sem=far_recv.at[a, 0],
                    device_id=_chip_device(k ^ far, core), device_id_type=MESH))
            for chunk in chunks:
                for a in big:
                    direct(a, relay_from, chunk, k ^ relay_from).wait_recv()
                    start(relay(a, chunk, k ^ relay_from, relay_to))
            for a in big:
                start(to_sibling(a, relay_from, core))
            for a in big:
                for chunk in chunks:
                    direct(a, relay_to, chunk, k ^ relay_to).wait_recv()
                start(to_sibling(a, relay_to, core))
            for a in big:
                for chunk in chunks:
                    relay(a, chunk, k ^ far, far).wait_recv()
                start(to_sibling(a, far, core))
            for a in small:
                direct(a, x_nbr, None, k ^ x_nbr).wait_recv()
                direct(a, y_nbr, None, k ^ y_nbr).wait_recv()
                relay(a, None, k ^ far, far).wait_recv()
            for a in big:
                for mask in (y_nbr, x_nbr, far):
                    to_sibling(a, mask, 1 - core).wait_recv()
            for cp in sent:
                cp.wait_send()

        for core in (0, 1):
            pl.when(c == core)(functools.partial(run, core))
        for cp in own_copies:
            cp.wait()

    return pl.kernel(
        body,
        name=name,
        out_type=[jax.ShapeDtypeStruct((N_CHIPS,) + a.shape, a.dtype) for a in shards],
        mesh=plsc.ScalarSubcoreMesh(axis_name="sequencer", num_cores=1),
        scratch_types=[
            pltpu.SemaphoreType.DMA((n, 2, GATHER_CHUNKS)),
            pltpu.SemaphoreType.DMA((n, 2, GATHER_CHUNKS)),
            pltpu.SemaphoreType.DMA((n, GATHER_CHUNKS)),
            pltpu.SemaphoreType.DMA((n, GATHER_CHUNKS)),
            pltpu.SemaphoreType.DMA((n, N_CHIPS - 1)),
            pltpu.SemaphoreType.DMA((n, N_CHIPS - 1)),
            pltpu.SemaphoreType.DMA((n, GATHER_CHUNKS)),
            pltpu.SemaphoreType.DMA((n, GATHER_CHUNKS)),
        ],
        compiler_params=pltpu.CompilerParams(collective_id=collective_id),
    )(*shards)


def _scatter_pieces(pieces, *, name, collective_id):
    n = len(pieces)

    def body(*refs):
        ins, own, sib = refs[:n], refs[n : 2 * n], refs[2 * n : 3 * n]
        local_sem, send_sem, recv_sem, pass_send, pass_recv = refs[3 * n :]
        x, y, c, k = _position()
        sibling = (x, y, 1 - c)
        _handshake([sibling] + [_chip_device(k ^ j, c) for j in range(1, N_CHIPS)])

        def over_ici(a, j, piece, slot, to):
            return pltpu.make_async_remote_copy(
                src_ref=ins[a].at[piece],
                dst_ref=own[a].at[slot],
                send_sem=send_sem.at[a, j],
                recv_sem=recv_sem.at[a, j],
                device_id=to,
                device_id_type=MESH,
            )

        def to_sibling(a, j, slot):
            return pltpu.make_async_remote_copy(
                src_ref=own[a].at[slot],
                dst_ref=sib[a].at[slot],
                send_sem=pass_send.at[a, j],
                recv_sem=pass_recv.at[a, j],
                device_id=sibling,
                device_id_type=MESH,
            )

        mine = [pltpu.make_async_copy(ins[a].at[k], own[a].at[k], local_sem.at[a]) for a in range(n)]
        for cp in mine:
            cp.start()
        sent = []
        for j in range(N_CHIPS - 1):
            other = k ^ (j + 1)
            for a in range(n):
                cp = over_ici(a, j, other, k, _chip_device(other, c))
                cp.start()
                sent.append(cp)
        for a in range(n):
            mine[a].wait()
            cp = to_sibling(a, N_CHIPS - 1, k)
            cp.start()
            sent.append(cp)
        for j in range(N_CHIPS - 1):
            other = k ^ (j + 1)
            for a in range(n):
                over_ici(a, j, other, other, sibling).wait_recv()
                cp = to_sibling(a, j, other)
                cp.start()
                sent.append(cp)
        for j in range(N_CHIPS):
            for a in range(n):
                to_sibling(a, j, k).wait_recv()
        for cp in sent:
            cp.wait_send()

    shapes = [jax.ShapeDtypeStruct(a.shape, a.dtype) for a in pieces]
    outs = pl.kernel(
        body,
        name=name,
        out_type=shapes + shapes,
        mesh=plsc.ScalarSubcoreMesh(axis_name="sequencer", num_cores=1),
        scratch_types=[
            pltpu.SemaphoreType.DMA((n,)),
            pltpu.SemaphoreType.DMA((n, N_CHIPS - 1)),
            pltpu.SemaphoreType.DMA((n, N_CHIPS - 1)),
            pltpu.SemaphoreType.DMA((n, N_CHIPS)),
            pltpu.SemaphoreType.DMA((n, N_CHIPS)),
        ],
        compiler_params=pltpu.CompilerParams(collective_id=collective_id),
    )(*pieces)
    return outs[:n], outs[n:]


def _allreduce_small(buf, *, name):
    def body(x_ref, o_ref, slots, send_sem, recv_sem):
        x, y, c, _ = _position()
        me = 4 * x + 2 * y + c
        slots[me] = x_ref[...]

        def copy(d, slot):
            peer = me ^ d
            return pltpu.make_async_remote_copy(
                src_ref=x_ref,
                dst_ref=slots.at[slot],
                send_sem=send_sem.at[d - 1],
                recv_sem=recv_sem.at[d - 1],
                device_id=(peer >> 2, (peer >> 1) & 1, peer & 1),
                device_id_type=MESH,
            )

        sent = [copy(d, me) for d in range(1, N_DEV)]
        for cp in sent:
            cp.start()
        for d in range(1, N_DEV):
            copy(d, me ^ d).wait_recv()
        for cp in sent:
            cp.wait_send()
        acc = slots[0]
        for s in range(1, N_DEV):
            acc = acc + slots[s]
        o_ref[...] = acc

    return pl.pallas_call(
        body,
        name=name,
        in_specs=[pl.BlockSpec(memory_space=pltpu.VMEM)],
        out_specs=pl.BlockSpec(memory_space=pltpu.VMEM),
        out_shape=jax.ShapeDtypeStruct(buf.shape, F32),
        scratch_shapes=[
            pltpu.VMEM((N_DEV,) + buf.shape, F32),
            pltpu.SemaphoreType.DMA((N_DEV - 1,)),
            pltpu.SemaphoreType.DMA((N_DEV - 1,)),
        ],
        compiler_params=pltpu.CompilerParams(vmem_limit_bytes=VMEM_LIMIT),
    )(buf)


def _mixer_of(i):
    return i % N_MIXERS, i // N_MIXERS


def _forward_backward(x, target, norms, layers, send=lambda i, part, pieces: pieces):
    s, d = x.shape
    depth = len(layers)
    heads = d // HEAD_DIM
    hkv = heads // GQA_GROUP
    cos, sin = _rope_tables(s)
    saved = []
    h = x
    for i, lw in enumerate(layers):
        kind, j = _mixer_of(i)
        tag = f"l{i}"
        sv = {"h_in": h}
        a = _rms_fwd(h, norms["mix_norm"][i : i + 1], name=f"{tag}_mix_norm")
        qkv = _mm_nn(a, lw["w_in"], out_dtype=BF16, name=f"{tag}_w_in")
        if kind == 0:
            bias = _na_bias(_na_base(norms["na_rpb"][j]), name=f"{tag}_na_bias")
            sv["bias"] = _na_block_bias(bias, s // GRID_W, name=f"{tag}_na_block_bias")
            o = _na_fwd(qkv, sv["bias"], name=f"{tag}_na_fwd")
        elif kind == 1:
            o = _sc_mid_fwd(qkv, lw["sc_conv_w"], name=f"{tag}_sc_fwd")
        else:
            gq, gk = norms["gqa_q_norm"][j : j + 1], norms["gqa_k_norm"][j : j + 1]
            qkn = _gqa_prep_fwd(qkv, gq, gk, cos, sin, heads, hkv, name=f"{tag}_gqa_prep")
            o = _gqa_fwd(qkn, qkv, heads, hkv, name=f"{tag}_gqa_fwd")
            sv["qkn"] = qkn
        h_mid = _mm_nn(o, lw["w_out"], out_dtype=F32, residual=h, name=f"{tag}_w_out")
        b = _rms_fwd(h_mid, norms["ffn_norm"][i : i + 1], name=f"{tag}_ffn_norm")
        up = _mm_nn(b, lw["w_up"], out_dtype=BF16, name=f"{tag}_w_up")
        act, cg, cu = _ffn_mid_fwd(up, lw["ffn_conv_w"], lw["ffn_conv_b"], name=f"{tag}_ffn_fwd")
        h = _mm_nn(act, lw["w_down"], out_dtype=F32, residual=h_mid, name=f"{tag}_w_down")
        sv.update(a=a, qkv=qkv, o=o, h_mid=h_mid, b=b, up=up, act=act, cg=cg, cu=cu)
        saved.append(sv)

    dh, dh_b, d_final, loss = _loss_head(h, norms["final_norm"][None], target, name="loss_head")

    big = [None] * depth
    small = {"final_norm": d_final, "mix_norm": [None] * depth, "ffn_norm": [None] * depth,
             "ffn_conv_w": [None] * depth, "ffn_conv_b": [None] * depth, "na_rpb": {}}
    after = ()
    for i in reversed(range(depth)):
        kind, j = _mixer_of(i)
        tag = f"l{i}b"
        lw, sv = layers[i], saved[i]
        dw_down = _mm_tn(sv["act"], dh_b, 1, name=f"{tag}_dw_down")
        sent_down = send(i, "down", [dw_down.reshape(N_CHIPS, dw_down.shape[1] // N_CHIPS, d)])
        dact = _mm_nt(dh_b, lw["w_down"], out_dtype=BF16, name=f"{tag}_d_act", after=after + (_token(dw_down),))
        dug, duu, dwg, dwu, dbg, dbu = _ffn_mid_bwd(
            sv["up"], sv["cg"], sv["cu"], lw["ffn_conv_w"], dact, name=f"{tag}_ffn_bwd"
        )
        small["ffn_conv_w"][i] = jnp.concatenate([dwg, dwu], axis=1)
        small["ffn_conv_b"][i] = jnp.concatenate([dbg, dbu], axis=1)
        dw_up = _mm_tn(sv["b"], (dug, duu), N_CHIPS, name=f"{tag}_dw_up")
        sent_up = send(i, "up", [dw_up])
        db = _mm_nt((dug, duu), lw["w_up"], out_dtype=F32, name=f"{tag}_d_b")
        dh_mid, dh_mid_b, small["ffn_norm"][i] = _rms_bwd(
            sv["h_mid"], norms["ffn_norm"][i : i + 1], db, dh, name=f"{tag}_ffn_norm"
        )
        do = _mm_nt(dh_mid_b, lw["w_out"], out_dtype=BF16, name=f"{tag}_d_o", after=(_token(dw_up),))
        dw_out = _mm_tn(sv["o"], dh_mid_b, 1, name=f"{tag}_dw_out")
        if kind == 0:
            dq, dk, dv, dblock = _na_bwd(sv["qkv"], sv["bias"], do, name=f"{tag}_na_bwd")
            dbias = _na_unblock(dblock, s // GRID_W, name=f"{tag}_na_unblock")
            dqkv = jnp.concatenate([dq, dk, dv], axis=1)
            small["na_rpb"][j] = _rpb_fold(dbias, name=f"{tag}_rpb_fold")
        elif kind == 1:
            dgb, dgc, dhh, small["sc_conv_w"] = _sc_mid_bwd(sv["qkv"], lw["sc_conv_w"], do, name=f"{tag}_sc_bwd")
            dqkv = jnp.concatenate([dgb, dgc, dhh], axis=1)
        else:
            gq, gk = norms["gqa_q_norm"][j : j + 1], norms["gqa_k_norm"][j : j + 1]
            dqn, dkn, dv = _gqa_bwd(sv["qkn"], sv["qkv"], do, heads, hkv, name=f"{tag}_gqa_bwd")
            dqk, small["gqa_q_norm"], small["gqa_k_norm"] = _gqa_prep_bwd(
                sv["qkv"], gq, gk, cos, sin, dqn, dkn, heads, hkv, name=f"{tag}_gqa_prep_bwd"
            )
            dqkv = jnp.concatenate([dqk, dv], axis=1)
        dw_in = _mm_tn(sv["a"], dqkv, N_CHIPS, name=f"{tag}_dw_in")
        da = _mm_nt(dqkv, lw["w_in"], out_dtype=F32, name=f"{tag}_d_a")
        dh, dh_b, small["mix_norm"][i] = _rms_bwd(
            sv["h_in"], norms["mix_norm"][i : i + 1], da, dh_mid, name=f"{tag}_mix_norm"
        )
        sent_mix = send(i, "mix", [dw_in, dw_out.reshape(N_CHIPS, dw_out.shape[1] // N_CHIPS, d)])
        after = (_token(dw_in), _token(dw_out))
        big[i] = {"mix": sent_mix, "up": sent_up, "down": sent_down}
    return loss, dh, big, small


def _pack(parts):
    flat = jnp.concatenate([p.reshape(-1).astype(F32) for p in parts])
    pad = (-flat.shape[0]) % (8 * LANES)
    return jnp.pad(flat, (0, pad)).reshape(-1, LANES)


def _unpack(buf, shapes):
    flat = buf.reshape(-1)
    out, at = [], 0
    for shp in shapes:
        size = 1
        for n in shp:
            size *= n
        out.append(flat[at : at + size].reshape(shp))
        at += size
    return out


def kernel(x, mix_norm, ffn_norm, final_norm, na_w_qkv, na_rpb, na_w_o, sc_w_in, sc_conv_w, sc_w_out, gqa_w_qkv, gqa_q_norm, gqa_k_norm, gqa_w_o, ffn_w_up, ffn_conv_w, ffn_conv_b, ffn_w_down, loss_target, m_mix_norm, m_ffn_norm, m_final_norm, m_na_w_qkv, m_na_rpb, m_na_w_o, m_sc_w_in, m_sc_conv_w, m_sc_w_out, m_gqa_w_qkv, m_gqa_q_norm, m_gqa_k_norm, m_gqa_w_o, m_ffn_w_up, m_ffn_conv_w, m_ffn_conv_b, m_ffn_w_down, v_mix_norm, v_ffn_norm, v_final_norm, v_na_w_qkv, v_na_rpb, v_na_w_o, v_sc_w_in, v_sc_conv_w, v_sc_w_out, v_gqa_w_qkv, v_gqa_q_norm, v_gqa_k_norm, v_gqa_w_o, v_ffn_w_up, v_ffn_conv_w, v_ffn_conv_b, v_ffn_w_down):
    depth, d = mix_norm.shape
    chip = 2 * lax.axis_index("x") + lax.axis_index("y")
    w_in_of = {0: na_w_qkv, 1: sc_w_in, 2: gqa_w_qkv}
    w_out_of = {0: na_w_o, 1: sc_w_out, 2: gqa_w_o}

    layers = []
    for i in range(depth):
        kind, j = _mixer_of(i)
        shards = [_cast_slab(w_in_of[kind], j, name=f"cast_w_in_l{i}"), _cast_slab(w_out_of[kind], j, name=f"cast_w_out_l{i}")]
        split = [True, True]
        if kind == 1:
            shards.append(sc_conv_w[j])
            split.append(False)
        mix = _gather_shards(shards, split, name=f"gather_mix_l{i}", collective_id=1 + 2 * i)
        ffn = _gather_shards(
            [_cast_slab(ffn_w_up, i, name=f"cast_w_up_l{i}"), _cast_slab(ffn_w_down, i, name=f"cast_w_down_l{i}"),
             ffn_conv_w[i]],
            [True, True, False],
            name=f"gather_ffn_l{i}",
            collective_id=2 + 2 * i,
        )
        lw = {
            "w_in": mix[0],
            "w_out": mix[1].reshape(1, -1, d),
            "w_up": ffn[0],
            "w_down": ffn[1].reshape(1, -1, d),
            "ffn_conv_w": ffn[2],
            "ffn_conv_b": ffn_conv_b[i : i + 1],
        }
        if kind == 1:
            lw["sc_conv_w"] = mix[2]
        layers.append(lw)

    norms = dict(mix_norm=mix_norm, ffn_norm=ffn_norm, final_norm=final_norm, na_rpb=na_rpb,
                 gqa_q_norm=gqa_q_norm, gqa_k_norm=gqa_k_norm)

    parts = ("down", "up", "mix")

    def send(i, part, pieces):
        cid = 1 + 2 * depth + len(parts) * i + parts.index(part)
        return _scatter_pieces(pieces, name=f"scatter_{part}_l{i}", collective_id=cid)

    loss, grad_x, big, small = _forward_backward(x[0], loss_target[0], norms, layers, send)

    mixer_names = {0: ("na_w_qkv", "na_w_o"), 1: ("sc_w_in", "sc_w_out"), 2: ("gqa_w_qkv", "gqa_w_o")}
    state = {
        "na_w_qkv": (na_w_qkv, m_na_w_qkv, v_na_w_qkv), "na_w_o": (na_w_o, m_na_w_o, v_na_w_o),
        "sc_w_in": (sc_w_in, m_sc_w_in, v_sc_w_in), "sc_w_out": (sc_w_out, m_sc_w_out, v_sc_w_out),
        "gqa_w_qkv": (gqa_w_qkv, m_gqa_w_qkv, v_gqa_w_qkv), "gqa_w_o": (gqa_w_o, m_gqa_w_o, v_gqa_w_o),
        "ffn_w_up": (ffn_w_up, m_ffn_w_up, v_ffn_w_up), "ffn_w_down": (ffn_w_down, m_ffn_w_down, v_ffn_w_down),
    }
    res = {n: None for n in state}
    token = None
    for i in reversed(range(depth)):
        kind, j = _mixer_of(i)
        for part, names, slab in (("down", ("ffn_w_down",), i), ("up", ("ffn_w_up",), i), ("mix", mixer_names[kind], j)):
            own, sib = big[i][part]
            for slot, n in enumerate(names):
                res[n] = _adamw_slab(*state[n], own[slot], sib[slot], slab, res[n], token, name=f"adamw_{n}_l{i}")
                token = res[n][0][slab, :16, :LANES]

    n_na = na_rpb.shape[0]
    rpb_flat = jnp.stack([small["na_rpb"][j] for j in range(n_na)])
    full_parts = [
        loss[:, :1],
        jnp.concatenate(small["mix_norm"], axis=0),
        jnp.concatenate(small["ffn_norm"], axis=0),
        small["final_norm"],
        rpb_flat,
        small["sc_conv_w"],
        small["gqa_q_norm"],
        small["gqa_k_norm"],
        jnp.stack(small["ffn_conv_w"]),
        jnp.concatenate(small["ffn_conv_b"], axis=0),
    ]
    summed = _unpack(_allreduce_small(_pack(full_parts), name="allreduce_small"), [p.shape for p in full_parts])
    loss_all, g_mix, g_ffn, g_final, g_rpb, g_sc_cw, g_gq, g_gk, g_ffn_cw, g_ffn_cb = summed
    g_rpb = g_rpb[:, :, : RPB_ROWS * RPB_COLS].reshape(na_rpb.shape)
    g_sc_cw = lax.dynamic_slice_in_dim(g_sc_cw, chip * sc_conv_w.shape[2], sc_conv_w.shape[2], axis=1)[None]
    g_ffn_cw = lax.dynamic_slice_in_dim(g_ffn_cw, chip * ffn_conv_w.shape[2], ffn_conv_w.shape[2], axis=2)
    small_names = ["mix_norm", "ffn_norm", "final_norm", "na_rpb", "sc_conv_w", "gqa_q_norm", "gqa_k_norm",
                   "ffn_conv_w", "ffn_conv_b"]
    small_g = [g_mix, g_ffn, g_final.reshape(final_norm.shape), g_rpb, g_sc_cw, g_gq, g_gk, g_ffn_cw, g_ffn_cb]
    small_w = [mix_norm, ffn_norm, final_norm, na_rpb, sc_conv_w, gqa_q_norm, gqa_k_norm, ffn_conv_w, ffn_conv_b]
    small_m = [m_mix_norm, m_ffn_norm, m_final_norm, m_na_rpb, m_sc_conv_w, m_gqa_q_norm, m_gqa_k_norm,
               m_ffn_conv_w, m_ffn_conv_b]
    small_v = [v_mix_norm, v_ffn_norm, v_final_norm, v_na_rpb, v_sc_conv_w, v_gqa_q_norm, v_gqa_k_norm,
               v_ffn_conv_w, v_ffn_conv_b]
    shapes = [w.shape for w in small_w]
    packed = _adamw_small(_pack(small_w), _pack(small_g), _pack(small_m), _pack(small_v), name="adamw_small")
    small_d, small_nm, small_nv = (_unpack(p, shapes) for p in packed)
    for n, g, dl, nm, nv in zip(small_names, small_g, small_d, small_nm, small_nv):
        res[n] = (g.reshape(dl.shape), dl, nm, nv)

    order = ["mix_norm", "ffn_norm", "final_norm", "na_w_qkv", "na_rpb", "na_w_o", "sc_w_in", "sc_conv_w",
             "sc_w_out", "gqa_w_qkv", "gqa_q_norm", "gqa_k_norm", "gqa_w_o", "ffn_w_up", "ffn_conv_w",
             "ffn_conv_b", "ffn_w_down"]
    outs = [loss_all.reshape(()), grad_x[None]]
    for part in range(4):
        outs.extend(res[n][part] for n in order)
    return tuple(outs)
```

```python
import functools
import math

import jax
import jax.numpy as jnp
from jax import lax
from jax.experimental import pallas as pl
from jax.experimental.pallas import tpu as pltpu
from jax.experimental.pallas import tpu_sc as plsc

F32 = jnp.float32
BF16 = jnp.bfloat16
MESH = pl.DeviceIdType.MESH

N_CHIPS = 4
N_DEV = 8
N_MIXERS = 3
GRID_W = 64
HEAD_DIM = 128
EPS = 1e-6
NEG_INF = -1e30
NA_WIN_R = 8
NA_WIN_C = 16
GQA_GROUP = 4
ROPE_THETA = 10000.0
ADAM_LR = 0.001
ADAM_B1 = 0.9
ADAM_B2 = 0.999
ADAM_EPS = 1e-08
ADAM_WD = 0.01
ADAM_STEP = 10

LANES = 128
VMEM_LIMIT = 56 * 1024 * 1024
NT_DIMS = (((1,), (1,)), ((), ()))
TN_DIMS = (((0,), (0,)), ((), ()))


def _pick(n, cap, mult=LANES):
    best = None
    for t in range(mult, min(n, cap) + 1, mult):
        if n % t == 0:
            best = t
    return best if best is not None else n


def _params(*sem):
    return pltpu.CompilerParams(dimension_semantics=sem, vmem_limit_bytes=VMEM_LIMIT)


MM_VMEM_BUDGET = 47 * 1024 * 1024
MM_CONTRACT = 2816


def _mm_rows(m, blocks_for, mult=16):
    for cap in (1024, 512, 256, 128):
        tm = _pick(m, cap, mult)
        if sum(r * c * b * n for r, c, b, n in blocks_for(tm)) <= MM_VMEM_BUDGET:
            return tm
    return _pick(m, 128, mult)


def _accumulate(acc, step, steps, part, finish):
    if steps == 1:
        finish(part)
        return

    @pl.when(step == 0)
    def _():
        acc[...] = part

    @pl.when(step != 0)
    def _():
        acc[...] += part

    @pl.when(step == steps - 1)
    def _():
        finish(acc[...])


def _mm_nn(a, b, *, out_dtype, name, residual=None):
    m, k = a.shape
    nc, _, ncol = b.shape
    tn, tk = _pick(ncol, 1536), _pick(k, MM_CONTRACT)
    per, nk = ncol // tn, k // tk
    osz = jnp.dtype(out_dtype).itemsize
    tm = _mm_rows(m, lambda t: [(t, tk, a.dtype.itemsize, 2), (tk, tn, 2, 2), (t, tn, osz, 2),
                                (t, tn, 4, 2 * (residual is not None)), (t, tn, 4, nk > 1)])

    def body(a_ref, b_ref, *rest):
        o_ref = rest[-2] if nk > 1 else rest[-1]

        def finish(r):
            if residual is not None:
                r = r + rest[0][...]
            o_ref[...] = r.astype(o_ref.dtype)

        part = jnp.dot(a_ref[...].astype(BF16), b_ref[...], preferred_element_type=F32)
        _accumulate(rest[-1], pl.program_id(2), nk, part, finish)

    in_specs = [
        pl.BlockSpec((tm, tk), lambda i, j, kk: (i, kk)),
        pl.BlockSpec((None, tk, tn), lambda i, j, kk: (j // per, kk, j % per)),
    ]
    ops = [a, b]
    if residual is not None:
        in_specs.append(pl.BlockSpec((tm, tn), lambda i, j, kk: (i, j)))
        ops.append(residual)
    return pl.pallas_call(
        body,
        name=name,
        grid=(m // tm, nc * per, nk),
        in_specs=in_specs,
        out_specs=pl.BlockSpec((tm, tn), lambda i, j, kk: (i, j)),
        out_shape=jax.ShapeDtypeStruct((m, nc * ncol), out_dtype),
        scratch_shapes=[pltpu.VMEM((tm, tn), F32)] * (nk > 1),
        compiler_params=_params("parallel", "parallel", "arbitrary"),
    )(*ops)


def _token(x):
    return x[(0,) * (x.ndim - 2) + (slice(0, 16), slice(0, LANES))]


def _mm_nt(a, b, *, out_dtype, name, after=()):
    parts = tuple(a) if isinstance(a, (tuple, list)) else (a,)
    m, width = parts[0].shape
    nc, k, ncol = b.shape
    tko, tn = _pick(k, 1024), _pick(math.gcd(ncol, width), MM_CONTRACT)
    per, each, nn = ncol // tn, width // tn, len(parts) * width // tn
    osz = jnp.dtype(out_dtype).itemsize
    tm = _mm_rows(m, lambda t: [(t, tn, parts[0].dtype.itemsize, 2 * len(parts)), (tko, tn, 2, 2), (t, tko, osz, 2),
                                (t, tko, 4, nn > 1)])

    def body(*refs):
        a_refs, b_ref, rest = refs[: len(parts)], refs[len(parts)], refs[len(parts) + 1 :]
        o_ref = rest[len(after)]
        step = pl.program_id(2)

        def finish(r):
            o_ref[...] = r.astype(o_ref.dtype)

        def use(a_ref):
            part = lax.dot_general(a_ref[...].astype(BF16), b_ref[...], NT_DIMS, preferred_element_type=F32)
            _accumulate(rest[-1], step, nn, part, finish)

        _for_part(step // each, a_refs, use)

    return pl.pallas_call(
        body,
        name=name,
        grid=(m // tm, k // tko, nn),
        in_specs=[_part_spec((tm, tn), p, each, lambda i, j, s: (i, s)) for p in range(len(parts))]
        + [pl.BlockSpec((None, tko, tn), lambda i, j, s: (s // per, j, s % per))]
        + [pl.BlockSpec(t.shape, lambda i, j, s: (0, 0)) for t in after],
        out_specs=pl.BlockSpec((tm, tko), lambda i, j, s: (i, j)),
        out_shape=jax.ShapeDtypeStruct((m, k), out_dtype),
        scratch_shapes=[pltpu.VMEM((tm, tko), F32)] * (nn > 1),
        compiler_params=_params("parallel", "parallel", "arbitrary"),
    )(*parts, b, *after)


def _for_part(which, refs, use):
    if len(refs) == 1:
        use(refs[0])
        return
    for p, ref in enumerate(refs):
        pl.when(which == p)(functools.partial(use, ref))


def _part_spec(block, p, each, tile_of):
    def index(*ids):
        r, c = tile_of(*ids)
        return r, jnp.clip(c - p * each, 0, each - 1)

    return pl.BlockSpec(block, index)


def _mm_tn(a, g, nc, *, name):
    parts = tuple(g) if isinstance(g, (tuple, list)) else (g,)
    s, k = a.shape
    width = parts[0].shape[1]
    ncol = len(parts) * width // nc
    ts, tn = _pick(s, MM_CONTRACT, 16), _pick(math.gcd(ncol, width), 2048 if nc == 1 else 1536)
    per, each, ns = ncol // tn, width // tn, s // ts
    tko = _mm_rows(k, lambda t: [(ts, t, a.dtype.itemsize, 2), (ts, tn, parts[0].dtype.itemsize, 2 * len(parts)),
                                 (t, tn, 2, 2), (t, tn, 4, ns > 1)], mult=LANES)

    def body(a_ref, *refs):
        g_refs, rest = refs[: len(parts)], refs[len(parts) :]
        o_ref = rest[0]

        def finish(r):
            o_ref[...] = r.astype(o_ref.dtype)

        def use(g_ref):
            part = lax.dot_general(a_ref[...].astype(BF16), g_ref[...].astype(BF16), TN_DIMS, preferred_element_type=F32)
            _accumulate(rest[-1], pl.program_id(2), ns, part, finish)

        _for_part(pl.program_id(1) // each, g_refs, use)

    return pl.pallas_call(
        body,
        name=name,
        grid=(k // tko, nc * per, ns),
        in_specs=[pl.BlockSpec((ts, tko), lambda i, j, t: (t, i))]
        + [_part_spec((ts, tn), p, each, lambda i, j, t: (t, j)) for p in range(len(parts))],
        out_specs=pl.BlockSpec((None, tko, tn), lambda i, j, t: (j // per, i, j % per)),
        out_shape=jax.ShapeDtypeStruct((nc, k, ncol), BF16),
        scratch_shapes=[pltpu.VMEM((tko, tn), F32)] * (ns > 1),
        compiler_params=_params("parallel", "parallel", "arbitrary"),
    )(a, *parts)


ROW_TILE = 256


def _cast_slab(w, slab, *, name):
    _, rows, cols = w.shape
    tr = _pick(rows, ROW_TILE, 16)

    def body(w_ref, o_ref):
        o_ref[...] = w_ref[...].astype(o_ref.dtype)

    return pl.pallas_call(
        body,
        name=name,
        grid=(rows // tr,),
        in_specs=[pl.BlockSpec((None, tr, cols), lambda i: (slab, i, 0))],
        out_specs=pl.BlockSpec((tr, cols), lambda i: (i, 0)),
        out_shape=jax.ShapeDtypeStruct((rows, cols), BF16),
        compiler_params=_params("parallel"),
    )(w)


def _rms_fwd(h, g, *, name):
    s, d = h.shape
    tr = _pick(s, 2 * ROW_TILE, 16)

    def body(h_ref, g_ref, o_ref):
        x = h_ref[...]
        r = lax.rsqrt(jnp.mean(x * x, axis=-1, keepdims=True) + EPS)
        o_ref[...] = (x * r * g_ref[...]).astype(o_ref.dtype)

    return pl.pallas_call(
        body,
        name=name,
        grid=(s // tr,),
        in_specs=[pl.BlockSpec((tr, d), lambda i: (i, 0)), pl.BlockSpec((1, d), lambda i: (0, 0))],
        out_specs=pl.BlockSpec((tr, d), lambda i: (i, 0)),
        out_shape=jax.ShapeDtypeStruct((s, d), BF16),
        compiler_params=_params("parallel"),
    )(h, g)


def _rms_bwd(h, g, dy, dres, *, name):
    s, d = h.shape
    tr = _pick(s, ROW_TILE, 16)

    def body(h_ref, g_ref, dy_ref, dres_ref, dh_ref, dhb_ref, dg_ref):
        x = h_ref[...]
        r = lax.rsqrt(jnp.mean(x * x, axis=-1, keepdims=True) + EPS)
        xhat = x * r
        dyv = dy_ref[...].astype(F32)
        dyg = dyv * g_ref[...]
        dx = r * (dyg - xhat * jnp.mean(dyg * xhat, axis=-1, keepdims=True))
        dh = dres_ref[...] + dx
        dh_ref[...] = dh
        dhb_ref[...] = dh.astype(dhb_ref.dtype)
        part = jnp.sum(dyv * xhat, axis=0, keepdims=True)

        @pl.when(pl.program_id(0) == 0)
        def _():
            dg_ref[...] = part

        @pl.when(pl.program_id(0) != 0)
        def _():
            dg_ref[...] += part

    row = pl.BlockSpec((tr, d), lambda i: (i, 0))
    vec = pl.BlockSpec((1, d), lambda i: (0, 0))
    return pl.pallas_call(
        body,
        name=name,
        grid=(s // tr,),
        in_specs=[row, vec, row, row],
        out_specs=[row, row, vec],
        out_shape=[jax.ShapeDtypeStruct((s, d), F32), jax.ShapeDtypeStruct((s, d), BF16),
                   jax.ShapeDtypeStruct((1, d), F32)],
        compiler_params=_params("arbitrary"),
    )(h, g, dy, dres)


def _loss_head(h, g, target, *, name):
    s, d = h.shape
    tr = _pick(s, ROW_TILE, 16)

    def body(h_ref, g_ref, t_ref, dh_ref, dhb_ref, dg_ref, loss_ref):
        x = h_ref[...]
        r = lax.rsqrt(jnp.mean(x * x, axis=-1, keepdims=True) + EPS)
        xhat = x * r
        gv = g_ref[...]
        err = xhat * gv - t_ref[...]
        dyv = err * (1.0 / d)
        dyg = dyv * gv
        dh = r * (dyg - xhat * jnp.mean(dyg * xhat, axis=-1, keepdims=True))
        dh_ref[...] = dh
        dhb_ref[...] = dh.astype(dhb_ref.dtype)
        part = jnp.sum(dyv * xhat, axis=0, keepdims=True)
        lpart =jnp.sum(jnp.sum(err * err, axis=-1, keepdims=True), axis=0, keepdims=True) * (0.5 / d)

        @pl.when(pl.program_id(0) == 0)
        def _():
            dg_ref[...] = part
            loss_ref[...] = jnp.broadcast_to(lpart, loss_ref.shape)

        @pl.when(pl.program_id(0) != 0)
        def _():
            dg_ref[...] += part
            loss_ref[...] += jnp.broadcast_to(lpart, loss_ref.shape)

    row = pl.BlockSpec((tr, d), lambda i: (i, 0))
    vec = pl.BlockSpec((1, d), lambda i: (0, 0))
    return pl.pallas_call(
        body,
        name=name,
        grid=(s // tr,),
        in_specs=[row, vec, row],
        out_specs=[row, row, vec, pl.BlockSpec((1, LANES), lambda i: (0, 0))],
        out_shape=[
            jax.ShapeDtypeStruct((s, d), F32),
            jax.ShapeDtypeStruct((s, d), BF16),
            jax.ShapeDtypeStruct((1, d), F32),
            jax.ShapeDtypeStruct((1, LANES), F32),
        ],
        compiler_params=_params("arbitrary"),
    )(h, g, target)


def _shift_prev(x):
    row = lax.broadcasted_iota(jnp.int32, x.shape, 0)
    return jnp.where(row == 0, 0.0, pltpu.roll(x, 1, 0))


def _shift_next(x):
    n = x.shape[0]
    row = lax.broadcasted_iota(jnp.int32, x.shape, 0)
    return jnp.where(row == n - 1, 0.0, pltpu.roll(x, n - 1, 0))


def _conv3(x, w):
    xm, xp = _shift_prev(x), _shift_next(x)
    return xm * w[0:1] + x * w[1:2] + xp * w[2:3], xm, xp


def _conv3_t(d, w):
    return _shift_next(d) * w[0:1] + d * w[1:2] + _shift_prev(d) * w[2:3]


def _colsum(x):
    return jnp.sum(x, axis=0, keepdims=True)


def _ffn_mid_fwd(up, cw, cb, *, name):
    s, f2 = up.shape
    f = f2 // 2
    ncol = cw.shape[2]
    tc = _pick(ncol, 256)
    nt, per = f // tc, ncol // tc

    def body(ug_ref, uu_ref, wg_ref, wu_ref, bg_ref, bu_ref, o_ref, cg_ref, cu_ref):
        cg = _conv3(ug_ref[...].astype(F32), wg_ref[...])[0] + bg_ref[...]
        cu = _conv3(uu_ref[...].astype(F32), wu_ref[...])[0] + bu_ref[...]
        o_ref[...] = (cg * (1.0 / (1.0 + jnp.exp(-cg))) * cu).astype(o_ref.dtype)
        cg_ref[...] = cg.astype(cg_ref.dtype)
        cu_ref[...] = cu.astype(cu_ref.dtype)

    out = pl.BlockSpec((s, tc), lambda j: (0, j))
    return pl.pallas_call(
        body,
        name=name,
        grid=(nt,),
        in_specs=[
            pl.BlockSpec((s, tc), lambda j: (0, j)),
            pl.BlockSpec((s, tc), lambda j: (0, nt + j)),
            pl.BlockSpec((None, 3, tc), lambda j: (j // per, 0, j % per)),
            pl.BlockSpec((None, 3, tc), lambda j: ((nt + j) // per, 0, (nt + j) % per)),
            pl.BlockSpec((1, tc), lambda j: (0, j)),
            pl.BlockSpec((1, tc), lambda j: (0, nt + j)),
        ],
        out_specs=[out, out, out],
        out_shape=[jax.ShapeDtypeStruct((s, f), BF16)] * 3,
        compiler_params=_params("parallel"),
    )(up, up, cw, cw, cb, cb)


def _ffn_mid_bwd(up, cg, cu, cw, dact, *, name):
    s, f2 = up.shape
    f = f2 // 2
    ncol = cw.shape[2]
    tc = _pick(ncol, 256)
    nt, per = f // tc, ncol // tc

    def side(dc, u_ref, w_ref, du_ref, dw_ref, db_ref):
        w, u = w_ref[...], u_ref[...].astype(F32)
        nxt, prv = _shift_next(dc), _shift_prev(dc)
        du_ref[...] = (nxt * w[0:1] + dc * w[1:2] + prv * w[2:3]).astype(du_ref.dtype)
        dw_ref[0:1, :] = _colsum(nxt * u)
        dw_ref[1:2, :] = _colsum(dc * u)
        dw_ref[2:3, :] = _colsum(prv * u)
        db_ref[...] = _colsum(dc)

    def body(ug_ref, uu_ref, cg_ref, cu_ref, wg_ref, wu_ref, da_ref, dug_ref, duu_ref, dwg_ref, dwu_ref, dbg_ref, dbu_ref):
        cgv, cuv, da = cg_ref[...].astype(F32), cu_ref[...].astype(F32), da_ref[...].astype(F32)
        sig = 1.0 / (1.0 + jnp.exp(-cgv))
        side(da * cuv * (sig * (1.0 + cgv * (1.0 - sig))), ug_ref, wg_ref, dug_ref, dwg_ref, dbg_ref)
        side(da * (cgv * sig), uu_ref, wu_ref, duu_ref, dwu_ref, dbu_ref)

    col = pl.BlockSpec((s, tc), lambda j: (0, j))
    w3 = pl.BlockSpec((3, tc), lambda j: (0, j))
    b1 = pl.BlockSpec((1, tc), lambda j: (0, j))
    return pl.pallas_call(
        body,
        name=name,
        grid=(nt,),
        in_specs=[
            col,
            pl.BlockSpec((s, tc), lambda j: (0, nt + j)),
            col,
            col,
            pl.BlockSpec((None, 3, tc), lambda j: (j // per, 0, j % per)),
            pl.BlockSpec((None, 3, tc), lambda j: ((nt + j) // per, 0, (nt + j) % per)),
            col,
        ],
        out_specs=[col, col, w3, w3, b1, b1],
        out_shape=[
            jax.ShapeDtypeStruct((s, f), BF16),
            jax.ShapeDtypeStruct((s, f), BF16),
            jax.ShapeDtypeStruct((3, f), F32),
            jax.ShapeDtypeStruct((3, f), F32),
            jax.ShapeDtypeStruct((1, f), F32),
            jax.ShapeDtypeStruct((1, f), F32),
        ],
        compiler_params=_params("parallel"),
    )(up, up, cg, cu, cw, cw, dact)


def _sc_mid_fwd(z, cw, *, name):
    s, d3 = z.shape
    d = d3 // 3
    ncol = cw.shape[2]
    tc = _pick(ncol, 256)
    nt, per = d // tc, ncol // tc

    def body(gb_ref, gc_ref, hh_ref, w_ref, o_ref):
        p = gc_ref[...].astype(F32) * hh_ref[...].astype(F32)
        o_ref[...] = (gb_ref[...].astype(F32) * _conv3(p, w_ref[...])[0]).astype(o_ref.dtype)

    return pl.pallas_call(
        body,
        name=name,
        grid=(nt,),
        in_specs=[
            pl.BlockSpec((s, tc), lambda j: (0, j)),
            pl.BlockSpec((s, tc), lambda j: (0, nt + j)),
            pl.BlockSpec((s, tc), lambda j: (0, 2 * nt + j)),
            pl.BlockSpec((None, 3, tc), lambda j: (j // per, 0, j % per)),
        ],
        out_specs=pl.BlockSpec((s, tc), lambda j: (0, j)),
        out_shape=jax.ShapeDtypeStruct((s, d), BF16),
        compiler_params=_params("parallel"),
    )(z, z, z, cw)


def _sc_mid_bwd(z, cw, dmid, *, name):
    s, d3 = z.shape
    d = d3 // 3
    ncol = cw.shape[2]
    tc = _pick(ncol, 256)
    nt, per = d // tc, ncol // tc

    def body(gb_ref, gc_ref, hh_ref, w_ref, dm_ref, dgb_ref, dgc_ref, dhh_ref, dw_ref):
        gb, gc, hh = gb_ref[...].astype(F32), gc_ref[...].astype(F32), hh_ref[...].astype(F32)
        w = w_ref[...]
        p = gc * hh
        cv, pm, pp = _conv3(p, w)
        dm = dm_ref[...].astype(F32)
        dgb_ref[...] = (dm * cv).astype(dgb_ref.dtype)
        dcv = dm * gb
        dp = _conv3_t(dcv, w)
        dgc_ref[...] = (dp * hh).astype(dgc_ref.dtype)
        dhh_ref[...] = (dp * gc).astype(dhh_ref.dtype)
        dw_ref[0:1, :] = _colsum(dcv * pm)
        dw_ref[1:2, :] = _colsum(dcv * p)
        dw_ref[2:3, :] = _colsum(dcv * pp)

    col = pl.BlockSpec((s, tc), lambda j: (0, j))
    return pl.pallas_call(
        body,
        name=name,
        grid=(nt,),
        in_specs=[
            col,
            pl.BlockSpec((s, tc), lambda j: (0, nt + j)),
            pl.BlockSpec((s, tc), lambda j: (0, 2 * nt + j)),
            pl.BlockSpec((None, 3, tc), lambda j: (j // per, 0, j % per)),
            col,
        ],
        out_specs=[col, col, col, pl.BlockSpec((3, tc), lambda j: (0, j))],
        out_shape=[jax.ShapeDtypeStruct((s, d), BF16)] * 3 + [jax.ShapeDtypeStruct((3, d), F32)],
        compiler_params=_params("parallel"),
    )(z, z, z, cw, dmid)


NA_KEYS = NA_WIN_R * GRID_W


def _na_row_start(r, rows):
    return jnp.clip(r - NA_WIN_R // 2, 0, rows - NA_WIN_R)


def _na_base(rpb):
    h = rpb.shape[0]
    pos, neg = rpb[:, :, NA_WIN_C - 1:], rpb[:, :, : NA_WIN_C - 1]
    zeros = jnp.zeros((h, NA_WIN_R, GRID_W - 2 * NA_WIN_C + 1), F32)
    out = []
    for first in range(NA_WIN_R):
        p = pos[:, first : first + NA_WIN_R]
        n = jnp.roll(neg[:, first : first + NA_WIN_R], -1, axis=1)
        out.append(jnp.concatenate([p, zeros, n], axis=-1).reshape(h, 1, NA_KEYS))
    return jnp.stack(out, axis=1)


def _skew_right(x):
    return pltpu.roll(x, 0, 1, stride=1, stride_axis=0)


def _skew_left(x):
    n = x.shape[1]
    row = lax.broadcasted_iota(jnp.int32, x.shape, 0)
    for b in range(GRID_W.bit_length() - 1):
        x = jnp.where(((row >> b) & 1) == 1, pltpu.roll(x, n - (1 << b), 1), x)
    return x


def _na_bias(base, *, name):
    h = base.shape[0]

    def body(b_ref, o_ref):
        q = lax.broadcasted_iota(jnp.int32, (GRID_W, NA_KEYS), 0)
        kc = lax.broadcasted_iota(jnp.int32, (GRID_W, NA_KEYS), 1) % GRID_W
        start = jnp.clip(q - NA_WIN_C // 2, 0, GRID_W - NA_WIN_C)
        inside = (kc >= start) & (kc < start + NA_WIN_C)
        for slot in range(NA_WIN_R):
            x = _skew_right(jnp.broadcast_to(b_ref[slot], (GRID_W, NA_KEYS)))
            o_ref[slot] = jnp.where(inside, x, NEG_INF)

    return pl.pallas_call(
        body,
        name=name,
        grid=(h,),
        in_specs=[pl.BlockSpec((None, NA_WIN_R, 1, NA_KEYS), lambda i: (i, 0, 0, 0))],
        out_specs=pl.BlockSpec((None, NA_WIN_R, GRID_W, NA_KEYS), lambda i: (i, 0, 0, 0)),
        out_shape=jax.ShapeDtypeStruct((h, NA_WIN_R, GRID_W, NA_KEYS), F32),
        compiler_params=_params("parallel"),
    )(base)


NA_STEP_ROWS = 4
NA_BLOCK_ROWS = NA_STEP_ROWS + NA_WIN_R
NA_STEP_Q = NA_STEP_ROWS * GRID_W
NA_BLOCK_K = NA_BLOCK_ROWS * GRID_W
NA_PATTERNS = 3


def _na_plan(rows):
    steps = rows // NA_STEP_ROWS

    def start(r):
        return min(max(r - NA_WIN_R // 2, 0), rows - NA_WIN_R)

    plan = []
    for t in range(steps):
        first = [start(NA_STEP_ROWS * t + i) for i in range(NA_STEP_ROWS)]
        block = min(first[0], rows - NA_BLOCK_ROWS)
        offsets = tuple(f - block for f in first)
        slots = tuple(f - (NA_STEP_ROWS * t + i) + NA_WIN_R - 1 for i, f in enumerate(first))
        plan.append((offsets, slots))
    middle = plan[1] if steps > 2 else (tuple(range(NA_STEP_ROWS)), (NA_WIN_R // 2 - 1,) * NA_STEP_ROWS)
    assert steps >= 2 and all(p == middle for p in plan[1:-1])
    return steps, (plan[0], middle, plan[-1])


def _na_block_start(t, rows):
    return jnp.minimum(_na_row_start(NA_STEP_ROWS * t, rows), rows - NA_BLOCK_ROWS)


def _na_pattern(t, steps):
    return jnp.where(t == 0, 0, jnp.where(t == steps - 1, 2, 1))


def _na_block_bias(bias, rows, *, name):
    h = bias.shape[0]
    _, patterns = _na_plan(rows)

    def body(b_ref, o_ref):
        outside = jnp.full((GRID_W, NA_BLOCK_K - NA_KEYS), NEG_INF, F32)
        for p, (offsets, slots) in enumerate(patterns):

            @pl.when(pl.program_id(1) == p)
            def _(offsets=offsets, slots=slots):
                for i in range(NA_STEP_ROWS):
                    wide = jnp.concatenate([b_ref[slots[i]], outside], axis=1)
                    shift = offsets[i] * GRID_W
                    o_ref[i * GRID_W : (i + 1) * GRID_W, :] = pltpu.roll(wide, shift, 1) if shift else wide

    return pl.pallas_call(
        body,
        name=name,
        grid=(h, NA_PATTERNS),
        in_specs=[pl.BlockSpec((None, NA_WIN_R, GRID_W, NA_KEYS), lambda i, p: (i, 0, 0, 0))],
        out_specs=pl.BlockSpec((None, None, NA_STEP_Q, NA_BLOCK_K), lambda i, p: (i, p, 0, 0)),
        out_shape=jax.ShapeDtypeStruct((h, NA_PATTERNS, NA_STEP_Q, NA_BLOCK_K), F32),
        compiler_params=_params("parallel", "arbitrary"),
    )(bias)


def _na_unblock(dblock, rows, *, name):
    h = dblock.shape[0]
    steps, patterns = _na_plan(rows)
    used = [p for p in range(NA_PATTERNS) if p != 1 or steps > 2]

    def body(d_ref, o_ref):
        total = {}
        for p in used:
            offsets, slots = patterns[p]
            for i in range(NA_STEP_ROWS):
                wide = d_ref[p, i * GRID_W : (i + 1) * GRID_W, :]
                back = (NA_BLOCK_K - offsets[i] * GRID_W) % NA_BLOCK_K
                piece = (pltpu.roll(wide, back, 1) if back else wide)[:, :NA_KEYS]
                total[slots[i]] = piece if slots[i] not in total else total[slots[i]] + piece
        for slot in range(NA_WIN_R):
            o_ref[slot] = total.get(slot, jnp.zeros((GRID_W, NA_KEYS), F32))

    return pl.pallas_call(
        body,
        name=name,
        grid=(h,),
        in_specs=[pl.BlockSpec((None, NA_PATTERNS, NA_STEP_Q, NA_BLOCK_K), lambda i: (i, 0, 0, 0))],
        out_specs=pl.BlockSpec((None, NA_WIN_R, GRID_W, NA_KEYS), lambda i: (i, 0, 0, 0)),
        out_shape=jax.ShapeDtypeStruct((h, NA_WIN_R, GRID_W, NA_KEYS), F32),
        compiler_params=_params("parallel"),
    )(dblock)


def _na_probs(q, k, bias):
    sc = lax.dot_general(q, k, NT_DIMS, preferred_element_type=F32) * (HEAD_DIM ** -0.5) + bias
    p = jnp.exp(sc - jnp.max(sc, axis=-1, keepdims=True))
    return p, jnp.sum(p, axis=-1, keepdims=True)


def _na_fwd(qkv, block_bias, *, name):
    s = qkv.shape[0]
    heads = qkv.shape[1] // (3 * HEAD_DIM)
    rows = s // GRID_W
    steps, _ = _na_plan(rows)

    per = next(n for n in (4, 2, 1) if heads % n == 0)
    width = per * HEAD_DIM

    def body(q_ref, k_ref, v_ref, b_ref, o_ref):
        start = _na_block_start(pl.program_id(1), rows)
        block = pl.ds(pl.multiple_of(start * GRID_W, GRID_W), NA_BLOCK_K)
        for g in range(per):
            cols = slice(g * HEAD_DIM, (g + 1) * HEAD_DIM)
            p, l = _na_probs(q_ref[:, cols], k_ref[block, cols], b_ref[g])
            o = jnp.dot(p.astype(BF16), v_ref[block, cols], preferred_element_type=F32)
            o_ref[:, cols] = (o / l).astype(o_ref.dtype)

    q = pl.BlockSpec((NA_STEP_Q, width), lambda h, t: (t, h))
    k = pl.BlockSpec((s, width), lambda h, t: (0, heads // per + h))
    v = pl.BlockSpec((s, width), lambda h, t: (0, 2 * heads // per + h))
    b = pl.BlockSpec((per, None, NA_STEP_Q, NA_BLOCK_K), lambda h, t: (h, _na_pattern(t, steps), 0, 0))
    return pl.pallas_call(
        body,
        name=name,
        grid=(heads // per, steps),
        in_specs=[q, k, v, b],
        out_specs=q,
        out_shape=jax.ShapeDtypeStruct((s, heads * HEAD_DIM), BF16),
        compiler_params=_params("parallel", "arbitrary"),
    )(qkv, qkv, qkv, block_bias)


def _na_bwd(qkv, block_bias, dout, *, name):
    s = qkv.shape[0]
    heads = qkv.shape[1] // (3 * HEAD_DIM)
    rows = s // GRID_W
    steps, _ = _na_plan(rows)
    scale = HEAD_DIM ** -0.5
    per = 2 if heads % 2 == 0 else 1
    width = per * HEAD_DIM

    def body(q_ref, k_ref, v_ref, b_ref, do_ref, dq_ref, dk_ref, dv_ref, db_ref, dk_acc, dv_acc):
        step = pl.program_id(1)

        @pl.when(step == 0)
        def _():
            dk_acc[...] = jnp.zeros_like(dk_acc)
            dv_acc[...] = jnp.zeros_like(dv_acc)

        block = pl.ds(pl.multiple_of(_na_block_start(step, rows) * GRID_W, GRID_W), NA_BLOCK_K)
        opens = (step <= 1) | (step == steps - 1)
        for g in range(per):
            cols = slice(g * HEAD_DIM, (g + 1) * HEAD_DIM)
            q, k, v, do = q_ref[:, cols], k_ref[block, cols], v_ref[block, cols], do_ref[:, cols]
            p, l = _na_probs(q, k, b_ref[g])
            pn = p / l
            dp = lax.dot_general(do, v, NT_DIMS, preferred_element_type=F32)
            ds = pn * (dp - jnp.sum(pn * dp, axis=-1, keepdims=True))
            dsb = ds.astype(BF16)
            dq_ref[:, cols] = (jnp.dot(dsb, k, preferred_element_type=F32) * scale).astype(dq_ref.dtype)
            dk_acc[block, cols] += lax.dot_general(dsb, q, TN_DIMS, preferred_element_type=F32) * scale
            dv_acc[block, cols] += lax.dot_general(pn.astype(BF16), do, TN_DIMS, preferred_element_type=F32)

            @pl.when(opens)
            def _(g=g, ds=ds):
                db_ref[g] = ds

            @pl.when(jnp.logical_not(opens))
            def _(g=g, ds=ds):
                db_ref[g] += ds

        @pl.when(step == steps - 1)
        def _():
            dk_ref[...] = dk_acc[...].astype(dk_ref.dtype)
            dv_ref[...] = dv_acc[...].astype(dv_ref.dtype)

    q = pl.BlockSpec((NA_STEP_Q, width), lambda h, t: (t, h))
    k = pl.BlockSpec((s, width), lambda h, t: (0, heads // per + h))
    v = pl.BlockSpec((s, width), lambda h, t: (0, 2 * heads // per + h))
    b = pl.BlockSpec((per, None, NA_STEP_Q, NA_BLOCK_K), lambda h, t: (h, _na_pattern(t, steps), 0, 0))
    kv_out = pl.BlockSpec((s, width), lambda h, t: (0, h))
    shape = jax.ShapeDtypeStruct((s, heads * HEAD_DIM), BF16)
    return pl.pallas_call(
        body,
        name=name,
        grid=(heads // per, steps),
        in_specs=[q, k, v, b, q],
        out_specs=[q, kv_out, kv_out, b],
        out_shape=[shape, shape, shape, jax.ShapeDtypeStruct((heads, NA_PATTERNS, NA_STEP_Q, NA_BLOCK_K), F32)],
        scratch_shapes=[pltpu.VMEM((s, width), F32), pltpu.VMEM((s, width), F32)],
        compiler_params=_params("parallel", "arbitrary"),
    )(qkv, qkv, qkv, block_bias, dout)


RPB_ROWS = 2 * NA_WIN_R - 1
RPB_COLS = 2 * NA_WIN_C - 1
RPB_PAD = 512


def _rpb_fold_matrix():
    idx = jnp.arange(NA_WIN_R * NA_KEYS, dtype=jnp.int32)
    first, i, kc = idx // NA_KEYS, (idx // GRID_W) % NA_WIN_R, idx % GRID_W
    pos, neg = kc < NA_WIN_C, kc >= GRID_W - NA_WIN_C + 1
    dr = jnp.where(pos, first + i, first + (i + 1) % NA_WIN_R)
    dc = jnp.where(pos, kc + NA_WIN_C - 1, kc - (GRID_W - NA_WIN_C + 1))
    target = jnp.where(pos | neg, dr * RPB_COLS + dc, -1)
    return (target[:, None] == jnp.arange(RPB_PAD, dtype=jnp.int32)[None, :]).astype(F32)


def _rpb_fold(dbias, *, name):
    h = dbias.shape[0]

    def skew_body(g_ref, o_ref):
        for slot in range(NA_WIN_R):
            o_ref[slot] = _colsum(_skew_left(g_ref[slot]))

    skewed = pl.pallas_call(
        skew_body,
        name=name + "_skew",
        grid=(h,),
        in_specs=[pl.BlockSpec((None, NA_WIN_R, GRID_W, NA_KEYS), lambda i: (i, 0, 0, 0))],
        out_specs=pl.BlockSpec((None, NA_WIN_R, 1, NA_KEYS), lambda i: (i, 0, 0, 0)),
        out_shape=jax.ShapeDtypeStruct((h, NA_WIN_R, 1, NA_KEYS), F32),
        compiler_params=_params("parallel"),
    )(dbias)

    def fold_body(g_ref, m_ref, o_ref):
        o_ref[...] = jnp.dot(g_ref[...], m_ref[...], preferred_element_type=F32, precision=lax.Precision.HIGHEST)

    return pl.pallas_call(
        fold_body,
        name=name,
        out_shape=jax.ShapeDtypeStruct((h, RPB_PAD), F32),
        compiler_params=pltpu.CompilerParams(vmem_limit_bytes=VMEM_LIMIT),
    )(skewed.reshape(h, NA_WIN_R * NA_KEYS), _rpb_fold_matrix())


GQA_Q_TILE = 512
GQA_Q_TILE_BWD = 512


def _rope_tables(s):
    t = jnp.arange(s)
    row = (t // GRID_W).astype(F32)[:, None]
    col = (t % GRID_W).astype(F32)[:, None]
    half = HEAD_DIM // 2
    inv = ROPE_THETA ** (-jnp.arange(0, half, 2, dtype=F32) / half)
    ang = jnp.concatenate([row * inv, row * inv, col * inv, col * inv], axis=-1)
    return jnp.cos(ang), jnp.sin(ang)


def _rot_half(y):
    quarter = HEAD_DIM // 4
    lane = lax.broadcasted_iota(jnp.int32, y.shape, 1)
    low = (lane % (2 * quarter)) < quarter
    return jnp.where(low, -pltpu.roll(y, HEAD_DIM - quarter, 1), pltpu.roll(y, quarter, 1))


def _gqa_prep_fwd(qkv, gq, gk, cos, sin, hq, hkv, *, name):
    s = qkv.shape[0]

    def body(x_ref, gq_ref, gk_ref, cos_ref, sin_ref, o_ref):
        isq = pl.program_id(0) < hq
        x = x_ref[...].astype(F32)
        g = jnp.where(isq, gq_ref[...], gk_ref[...])
        y = x * lax.rsqrt(jnp.mean(x * x, axis=-1, keepdims=True) + EPS) * g
        z = y * cos_ref[...] + _rot_half(y) * sin_ref[...]
        o_ref[...] = (z * jnp.where(isq, HEAD_DIM ** -0.5, 1.0)).astype(o_ref.dtype)

    head = pl.BlockSpec((s, HEAD_DIM), lambda h: (0, h))
    vec = pl.BlockSpec((1, HEAD_DIM), lambda h: (0, 0))
    tab = pl.BlockSpec((s, HEAD_DIM), lambda h: (0, 0))
    return pl.pallas_call(
        body,
        name=name,
        grid=(hq + hkv,),
        in_specs=[head, vec, vec, tab, tab],
        out_specs=head,
        out_shape=jax.ShapeDtypeStruct((s, (hq + hkv) * HEAD_DIM), BF16),
        compiler_params=_params("parallel"),
    )(qkv, gq, gk, cos, sin)


def _gqa_prep_bwd(qkv, gq, gk, cos, sin, dqn, dkn, hq, hkv, *, name):
    s = qkv.shape[0]

    def body(x_ref, gq_ref, gk_ref, cos_ref, sin_ref, dq_ref, dk_ref, dx_ref, dgq_ref, dgk_ref):
        hh = pl.program_id(0)
        isq = hh < hq
        x = x_ref[...].astype(F32)
        g = jnp.where(isq, gq_ref[...], gk_ref[...])
        r = lax.rsqrt(jnp.mean(x * x, axis=-1, keepdims=True) + EPS)
        xhat = x * r
        dz = jnp.where(isq, dq_ref[...].astype(F32) * (HEAD_DIM ** -0.5), dk_ref[...].astype(F32))
        dy = dz * cos_ref[...] - _rot_half(dz * sin_ref[...])
        dyg = dy * g
        dx_ref[...] = (r * (dyg - xhat * jnp.mean(dyg * xhat, axis=-1, keepdims=True))).astype(dx_ref.dtype)
        part = _colsum(dy * xhat)

        @pl.when(hh == 0)
        def _():
            dgq_ref[...] = jnp.zeros_like(dgq_ref)
            dgk_ref[...] = jnp.zeros_like(dgk_ref)

        @pl.when(isq)
        def _():
            dgq_ref[...] += part

        @pl.when(jnp.logical_not(isq))
        def _():
            dgk_ref[...] += part

    head = pl.BlockSpec((s, HEAD_DIM), lambda h: (0, h))
    vec = pl.BlockSpec((1, HEAD_DIM), lambda h: (0, 0))
    tab = pl.BlockSpec((s, HEAD_DIM), lambda h: (0, 0))
    return pl.pallas_call(
        body,
        name=name,
        grid=(hq + hkv,),
        in_specs=[
            head,
            vec,
            vec,
            tab,
            tab,
            pl.BlockSpec((s, HEAD_DIM), lambda h: (0, jnp.minimum(h, hq - 1))),
            pl.BlockSpec((s, HEAD_DIM), lambda h: (0, jnp.maximum(h - hq, 0))),
        ],
        out_specs=[head, vec, vec],
        out_shape=[
            jax.ShapeDtypeStruct((s, (hq + hkv) * HEAD_DIM), BF16),
            jax.ShapeDtypeStruct((1, HEAD_DIM), F32),
            jax.ShapeDtypeStruct((1, HEAD_DIM), F32),
        ],
        compiler_params=_params("arbitrary"),
    )(qkv, gq, gk, cos, sin, dqn, dkn)


def _gqa_fwd(qkn, qkv, hq, hkv, *, name):
    s = qkv.shape[0]
    tq = _pick(s, GQA_Q_TILE, 16)

    def body(q_ref, k_ref, v_ref, o_ref):
        k, v = k_ref[...], v_ref[...]
        for g in range(GQA_GROUP):
            cols = slice(g * HEAD_DIM, (g + 1) * HEAD_DIM)
            sc = lax.dot_general(q_ref[:, cols], k, NT_DIMS, preferred_element_type=F32)
            p = jnp.exp(sc - jnp.max(sc, axis=-1, keepdims=True))
            l = jnp.sum(p, axis=-1, keepdims=True)
            o_ref[:, cols] = (jnp.dot(p.astype(BF16), v, preferred_element_type=F32) / l).astype(o_ref.dtype)

    q = pl.BlockSpec((tq, GQA_GROUP * HEAD_DIM), lambda h, i: (i, h))
    return pl.pallas_call(
        body,
        name=name,
        grid=(hkv, s // tq),
        in_specs=[
            q,
            pl.BlockSpec((s, HEAD_DIM), lambda h, i: (0, hq + h)),
            pl.BlockSpec((s, HEAD_DIM), lambda h, i: (0, hq + hkv + h)),
        ],
        out_specs=q,
        out_shape=jax.ShapeDtypeStruct((s, hq * HEAD_DIM), BF16),
        compiler_params=_params("parallel", "parallel"),
    )(qkn, qkn, qkv)


def _gqa_bwd(qkn, qkv, dout, hq, hkv, *, name):
    s = qkv.shape[0]
    tq = _pick(s, GQA_Q_TILE_BWD, 16)
    nq = s // tq

    def body(q_ref, k_ref, v_ref, do_ref, dq_ref, dk_ref, dv_ref, dk_acc, dv_acc):
        g, i = pl.program_id(1), pl.program_id(2)

        @pl.when((g == 0) & (i == 0))
        def _():
            dk_acc[...] = jnp.zeros_like(dk_acc)
            dv_acc[...] = jnp.zeros_like(dv_acc)

        q, k, v, do = q_ref[...], k_ref[...], v_ref[...], do_ref[...]
        sc = lax.dot_general(q, k, NT_DIMS, preferred_element_type=F32)
        p = jnp.exp(sc - jnp.max(sc, axis=-1, keepdims=True))
        pn = p / jnp.sum(p, axis=-1, keepdims=True)
        dp = lax.dot_general(do, v, NT_DIMS, preferred_element_type=F32)
        dsb = (pn * (dp - jnp.sum(pn * dp, axis=-1, keepdims=True))).astype(BF16)
        dq_ref[...] = jnp.dot(dsb, k, preferred_element_type=F32).astype(dq_ref.dtype)
        dk_acc[...] += lax.dot_general(dsb, q, TN_DIMS, preferred_element_type=F32)
        dv_acc[...] += lax.dot_general(pn.astype(BF16), do, TN_DIMS, preferred_element_type=F32)

        @pl.when((g == GQA_GROUP - 1) & (i == nq - 1))
        def _():
            dk_ref[...] = dk_acc[...].astype(dk_ref.dtype)
            dv_ref[...] = dv_acc[...].astype(dv_ref.dtype)

    q = pl.BlockSpec((tq, HEAD_DIM), lambda kv, g, i: (i, kv * GQA_GROUP + g))
    kv_out = pl.BlockSpec((s, HEAD_DIM), lambda kv, g, i: (0, kv))
    return pl.pallas_call(
        body,
        name=name,
        grid=(hkv, GQA_GROUP, nq),
        in_specs=[
            q,
            pl.BlockSpec((s, HEAD_DIM), lambda kv, g, i: (0, hq + kv)),
            pl.BlockSpec((s, HEAD_DIM), lambda kv, g, i: (0, hq + hkv + kv)),
            q,
        ],
        out_specs=[q, kv_out, kv_out],
        out_shape=[
            jax.ShapeDtypeStruct((s, hq * HEAD_DIM), BF16),
            jax.ShapeDtypeStruct((s, hkv * HEAD_DIM), BF16),
            jax.ShapeDtypeStruct((s, hkv * HEAD_DIM), BF16),
        ],
        scratch_shapes=[pltpu.VMEM((s, HEAD_DIM), F32), pltpu.VMEM((s, HEAD_DIM), F32)],
        compiler_params=_params("parallel", "arbitrary", "arbitrary"),
    )(qkn, qkn, qkv, dout)


ADAM_ROWS = 128


def _adam_update(w, g, m, v):
    m = ADAM_B1 * m + (1.0 - ADAM_B1) * g
    v = ADAM_B2 * v + (1.0 - ADAM_B2) * (g * g)
    m_hat = m / (1.0 - ADAM_B1 ** ADAM_STEP)
    v_hat = v / (1.0 - ADAM_B2 ** ADAM_STEP)
    return -ADAM_LR * (m_hat / (jnp.sqrt(v_hat) + ADAM_EPS) + ADAM_WD * w), m, v


def _adamw_slab(w, m, v, own, sib, slab, prev, after, *, name):
    _, rows, cols = w.shape
    tr = _pick(rows, ADAM_ROWS, 16)
    tokens = [] if after is None else [after]

    def body(w_ref, m_ref, v_ref, own_ref, sib_ref, *rest):
        g_ref, d_ref, nm_ref, nv_ref = rest[-4:]
        g = own_ref[0].astype(F32) + sib_ref[0].astype(F32)
        for q in range(1, N_CHIPS):
            g = g + (own_ref[q].astype(F32) + sib_ref[q].astype(F32))
        g_ref[...] = g
        d_ref[...], nm_ref[...], nv_ref[...] = _adam_update(w_ref[...], g, m_ref[...], v_ref[...])

    one = pl.BlockSpec((None, tr, cols), lambda i: (slab, i, 0))
    piece = pl.BlockSpec((N_CHIPS, tr, cols), lambda i: (0, i, 0))
    carried = [] if prev is None else list(prev)
    shape = jax.ShapeDtypeStruct(w.shape, F32)
    return pl.pallas_call(
        body,
        name=name,
        grid=(rows // tr,),
        in_specs=[one] * 3
        + [piece] * 2
        + [pl.BlockSpec(memory_space=pl.ANY)] * len(carried)
        + [pl.BlockSpec(t.shape, lambda i: (0, 0)) for t in tokens],
        out_specs=[one] * 4,
        out_shape=[shape] * 4,
        input_output_aliases={5 + i: i for i in range(len(carried))},
        compiler_params=_params("parallel"),
    )(w, m, v, own, sib, *carried, *tokens)


def _adamw_small(w, g, m, v, *, name):
    def body(w_ref, g_ref, m_ref, v_ref, d_ref, nm_ref, nv_ref):
        d_ref[...], nm_ref[...], nv_ref[...] = _adam_update(w_ref[...], g_ref[...], m_ref[...], v_ref[...])

    shape = jax.ShapeDtypeStruct(w.shape, F32)
    return pl.pallas_call(
        body,
        name=name,
        out_shape=[shape] * 3,
        compiler_params=pltpu.CompilerParams(vmem_limit_bytes=VMEM_LIMIT),
    )(w, g, m, v)


def _position():
    x, y, c = lax.axis_index("x"), lax.axis_index("y"), lax.axis_index("c")
    return x, y, c, 2 * x + y


def _chip_device(chip, c):
    return (chip >> 1, chip & 1, c)


def _handshake(peers):
    barrier = pltpu.get_barrier_semaphore()
    for peer in peers:
        pl.semaphore_signal(barrier, inc=1, device_id=peer, device_id_type=MESH)
    pl.semaphore_wait(barrier, len(peers))


GATHER_CHUNKS = 2


def _gather_shards(shards, split, *, name, collective_id):
    n = len(shards)
    big = [a for a in range(n) if split[a]]
    small = [a for a in range(n) if not split[a]]
    y_nbr, x_nbr, far = 1, 2, 3

    def body(*refs):
        ins, outs = refs[:n], refs[n : 2 * n]
        near_send, near_recv, far_send, far_recv, pass_send, pass_recv, own_send, own_recv = refs[2 * n :]
        x, y, c, k = _position()
        _handshake([(x, y, 1 - c)] + [_chip_device(k ^ j, c) for j in range(1, N_CHIPS)])

        def own(a, chunk):
            r = shards[a].shape[0]
            part = pl.ds(chunk * (r // GATHER_CHUNKS), r // GATHER_CHUNKS) if split[a] else pl.ds(0, r)
            return pltpu.make_async_remote_copy(
                src_ref=ins[a].at[part],
                dst_ref=outs[a].at[k, part],
                send_sem=own_send.at[a, chunk],
                recv_sem=own_recv.at[a, chunk],
                device_id=(x, y, 1 - c),
                device_id_type=MESH,
            )

        own_copies = [own(a, ch) for a in range(n) for ch in range(GATHER_CHUNKS if split[a] else 1)]
        for cp in own_copies:
            cp.start()

        def run(core):
            sibling = (x, y, 1 - core)
            relay_from, relay_to = (x_nbr, y_nbr) if core == 0 else (y_nbr, x_nbr)

            def rows(a, which, chunk=None):
                r = shards[a].shape[0]
                if not split[a]:
                    return pl.ds(0, r)
                half = r // 2
                if chunk is None:
                    return pl.ds(which * half, half)
                return pl.ds(which * half + chunk * (half // GATHER_CHUNKS), half // GATHER_CHUNKS)

            def direct(a, mask, chunk, src_chip):
                part = rows(a, core, chunk)
                return pltpu.make_async_remote_copy(
                    src_ref=ins[a].at[part],
                    dst_ref=outs[a].at[src_chip, part],
                    send_sem=near_send.at[a, mask - 1, chunk or 0],
                    recv_sem=near_recv.at[a, mask - 1, chunk or 0],
                    device_id=_chip_device(k ^ mask, core),
                    device_id_type=MESH,
                )

            def relay(a, chunk, src_chip, mask):
                part = outs[a].at[src_chip, rows(a, core, chunk)]
                return pltpu.make_async_remote_copy(
                    src_ref=part,
                    dst_ref=part,
                    send_sem=far_send.at[a, chunk or 0],
                    recv_sem=far_recv.at[a, chunk or 0],
                    device_id=_chip_device(k ^ mask, core),
                    device_id_type=MESH,
                )

            def to_sibling(a, mask, which):
                part = outs[a].at[k ^ mask, rows(a, which)]
                return pltpu.make_async_remote_copy(
                    src_ref=part,
                    dst_ref=part,
                    send_sem=pass_send.at[a, mask - 1],
                    recv_sem=pass_recv.at[a, mask - 1],
                    device_id=sibling,
                    device_id_type=MESH,
                )

            sent = []

            def start(cp):
                cp.start()
                sent.append(cp)

            chunks = range(GATHER_CHUNKS)
            for chunk in chunks:
                for a in big:
                    start(direct(a, relay_from, chunk, k))
                    start(direct(a, relay_to, chunk, k))
            for a in small:
                start(direct(a, x_nbr, None, k))
                start(direct(a, y_nbr, None, k))
                start(pltpu.make_async_remote_copy(
                    src_ref=ins[a], dst_ref=outs[a].at[k], send_sem=far_send.at[a, 0], recv_sem=far_recv.at[a, 0],
                    device_id=_chip_device(k ^ far, core), device_id_type=MESH))
            for chunk in chunks:
                for a in big:
                    direct(a, relay_from, chunk, k ^ relay_from).wait_recv()
                    start(relay(a, chunk, k ^ relay_from, relay_to))
            for a in big:
                start(to_sibling(a, relay_from, core))
            for a in big:
                for chunk in chunks:
                    direct(a, relay_to, chunk, k ^ relay_to).wait_recv()
                start(to_sibling(a, relay_to, core))
            for a in big:
                for chunk in chunks:
                    relay(a, chunk, k ^ far, far).wait_recv()
                start(to_sibling(a, far, core))
            for a in small:
                direct(a, x_nbr, None, k ^ x_nbr).wait_recv()
                direct(a, y_nbr, None, k ^ y_nbr).wait_recv()
                relay(a, None, k ^ far, far).wait_recv()
            for a in big:
                for mask in (y_nbr, x_nbr, far):
                    to_sibling(a, mask, 1 - core).wait_recv()
            for cp in sent:
                cp.wait_send()

        for core in (0, 1):
            pl.when(c == core)(functools.partial(run, core))
        for cp in own_copies:
            cp.wait()

    return pl.kernel(
        body,
        name=name,
        out_type=[jax.ShapeDtypeStruct((N_CHIPS,) + a.shape, a.dtype) for a in shards],
        mesh=plsc.ScalarSubcoreMesh(axis_name="sequencer", num_cores=1),
        scratch_types=[
            pltpu.SemaphoreType.DMA((n, 2, GATHER_CHUNKS)),
            pltpu.SemaphoreType.DMA((n, 2, GATHER_CHUNKS)),
            pltpu.SemaphoreType.DMA((n, GATHER_CHUNKS)),
            pltpu.SemaphoreType.DMA((n, GATHER_CHUNKS)),
            pltpu.SemaphoreType.DMA((n, N_CHIPS - 1)),
            pltpu.SemaphoreType.DMA((n, N_CHIPS - 1)),
            pltpu.SemaphoreType.DMA((n, GATHER_CHUNKS)),
            pltpu.SemaphoreType.DMA((n, GATHER_CHUNKS)),
        ],
        compiler_params=pltpu.CompilerParams(collective_id=collective_id),
    )(*shards)


def _scatter_pieces(pieces, *, name, collective_id):
    n = len(pieces)

    def body(*refs):
        ins, own, sib = refs[:n], refs[n : 2 * n], refs[2 * n : 3 * n]
        local_sem, send_sem, recv_sem, pass_send, pass_recv = refs[3 * n :]
        x, y, c, k = _position()
        sibling = (x, y, 1 - c)
        _handshake([sibling] + [_chip_device(k ^ j, c) for j in range(1, N_CHIPS)])

        def over_ici(a, j, piece, slot, to):
            return pltpu.make_async_remote_copy(
                src_ref=ins[a].at[piece],
                dst_ref=own[a].at[slot],
                send_sem=send_sem.at[a, j],
                recv_sem=recv_sem.at[a, j],
                device_id=to,
                device_id_type=MESH,
            )

        def to_sibling(a, j, slot):
            return pltpu.make_async_remote_copy(
                src_ref=own[a].at[slot],
                dst_ref=sib[a].at[slot],
                send_sem=pass_send.at[a, j],
                recv_sem=pass_recv.at[a, j],
                device_id=sibling,
                device_id_type=MESH,
            )

        mine = [pltpu.make_async_copy(ins[a].at[k], own[a].at[k], local_sem.at[a]) for a in range(n)]
        for cp in mine:
            cp.start()
        sent = []
        for j in range(N_CHIPS - 1):
            other = k ^ (j + 1)
            for a in range(n):
                cp = over_ici(a, j, other, k, _chip_device(other, c))
                cp.start()
                sent.append(cp)
        for a in range(n):
            mine[a].wait()
            cp = to_sibling(a, N_CHIPS - 1, k)
            cp.start()
            sent.append(cp)
        for j in range(N_CHIPS - 1):
            other = k ^ (j + 1)
            for a in range(n):
                over_ici(a, j, other, other, sibling).wait_recv()
                cp = to_sibling(a, j, other)
                cp.start()
                sent.append(cp)
        for j in range(N_CHIPS):
            for a in range(n):
                to_sibling(a, j, k).wait_recv()
        for cp in sent:
            cp.wait_send()

    shapes = [jax.ShapeDtypeStruct(a.shape, a.dtype) for a in pieces]
    outs = pl.kernel(
        body,
        name=name,
        out_type=shapes + shapes,
        mesh=plsc.ScalarSubcoreMesh(axis_name="sequencer", num_cores=1),
        scratch_types=[
            pltpu.SemaphoreType.DMA((n,)),
            pltpu.SemaphoreType.DMA((n, N_CHIPS - 1)),
            pltpu.SemaphoreType.DMA((n, N_CHIPS - 1)),
            pltpu.SemaphoreType.DMA((n, N_CHIPS)),
            pltpu.SemaphoreType.DMA((n, N_CHIPS)),
        ],
        compiler_params=pltpu.CompilerParams(collective_id=collective_id),
    )(*pieces)
    return outs[:n], outs[n:]


def _allreduce_small(buf, *, name):
    def body(x_ref, o_ref, slots, send_sem, recv_sem):
        x, y, c, _ = _position()
        me = 4 * x + 2 * y + c
        slots[me] = x_ref[...]

        def copy(d, slot):
            peer = me ^ d
            return pltpu.make_async_remote_copy(
                src_ref=x_ref,
                dst_ref=slots.at[slot],
                send_sem=send_sem.at[d - 1],
                recv_sem=recv_sem.at[d - 1],
                device_id=(peer >> 2, (peer >> 1) & 1, peer & 1),
                device_id_type=MESH,
            )

        sent = [copy(d, me) for d in range(1, N_DEV)]
        for cp in sent:
            cp.start()
        for d in range(1, N_DEV):
            copy(d, me ^ d).wait_recv()
        for cp in sent:
            cp.wait_send()
        acc = slots[0]
        for s in range(1, N_DEV):
            acc = acc + slots[s]
        o_ref[...] = acc

    return pl.pallas_call(
        body,
        name=name,
        in_specs=[pl.BlockSpec(memory_space=pltpu.VMEM)],
        out_specs=pl.BlockSpec(memory_space=pltpu.VMEM),
        out_shape=jax.ShapeDtypeStruct(buf.shape, F32),
        scratch_shapes=[
            pltpu.VMEM((N_DEV,) + buf.shape, F32),
            pltpu.SemaphoreType.DMA((N_DEV - 1,)),
            pltpu.SemaphoreType.DMA((N_DEV - 1,)),
        ],
        compiler_params=pltpu.CompilerParams(vmem_limit_bytes=VMEM_LIMIT),
    )(buf)


def _mixer_of(i):
    return i % N_MIXERS, i // N_MIXERS


def _forward_backward(x, target, norms, layers, send=lambda i, part, pieces: pieces):
    s, d = x.shape
    depth = len(layers)
    heads = d // HEAD_DIM
    hkv = heads // GQA_GROUP
    cos, sin = _rope_tables(s)
    saved = []
    h = x
    for i, lw in enumerate(layers):
        kind, j = _mixer_of(i)
        tag = f"l{i}"
        sv = {"h_in": h}
        a = _rms_fwd(h, norms["mix_norm"][i : i + 1], name=f"{tag}_mix_norm")
        qkv = _mm_nn(a, lw["w_in"], out_dtype=BF16, name=f"{tag}_w_in")
        if kind == 0:
            bias = _na_bias(_na_base(norms["na_rpb"][j]), name=f"{tag}_na_bias")
            sv["bias"] = _na_block_bias(bias, s // GRID_W, name=f"{tag}_na_block_bias")
            o = _na_fwd(qkv, sv["bias"], name=f"{tag}_na_fwd")
        elif kind == 1:
            o = _sc_mid_fwd(qkv, lw["sc_conv_w"], name=f"{tag}_sc_fwd")
        else:
            gq, gk = norms["gqa_q_norm"][j : j + 1], norms["gqa_k_norm"][j : j + 1]
            qkn = _gqa_prep_fwd(qkv, gq, gk, cos, sin, heads, hkv, name=f"{tag}_gqa_prep")
            o = _gqa_fwd(qkn, qkv, heads, hkv, name=f"{tag}_gqa_fwd")
            sv["qkn"] = qkn
        h_mid = _mm_nn(o, lw["w_out"], out_dtype=F32, residual=h, name=f"{tag}_w_out")
        b = _rms_fwd(h_mid, norms["ffn_norm"][i : i + 1], name=f"{tag}_ffn_norm")
        up = _mm_nn(b, lw["w_up"], out_dtype=BF16, name=f"{tag}_w_up")
        act, cg, cu = _ffn_mid_fwd(up, lw["ffn_conv_w"], lw["ffn_conv_b"], name=f"{tag}_ffn_fwd")
        h = _mm_nn(act, lw["w_down"], out_dtype=F32, residual=h_mid, name=f"{tag}_w_down")
        sv.update(a=a, qkv=qkv, o=o, h_mid=h_mid, b=b, up=up, act=act, cg=cg, cu=cu)
        saved.append(sv)

    dh, dh_b, d_final, loss = _loss_head(h, norms["final_norm"][None], target, name="loss_head")

    big = [None] * depth
    small = {"final_norm": d_final, "mix_norm": [None] * depth, "ffn_norm": [None] * depth,
             "ffn_conv_w": [None] * depth, "ffn_conv_b": [None] * depth, "na_rpb": {}}
    after = ()
    for i in reversed(range(depth)):
        kind, j = _mixer_of(i)
        tag = f"l{i}b"
        lw, sv = layers[i], saved[i]
        dw_down = _mm_tn(sv["act"], dh_b, 1, name=f"{tag}_dw_down")
        sent_down = send(i, "down", [dw_down.reshape(N_CHIPS, dw_down.shape[1] // N_CHIPS, d)])
        dact = _mm_nt(dh_b, lw["w_down"], out_dtype=BF16, name=f"{tag}_d_act", after=after + (_token(dw_down),))
        dug, duu, dwg, dwu, dbg, dbu = _ffn_mid_bwd(
            sv["up"], sv["cg"], sv["cu"], lw["ffn_conv_w"], dact, name=f"{tag}_ffn_bwd"
        )
        small["ffn_conv_w"][i] = jnp.concatenate([dwg, dwu], axis=1)
        small["ffn_conv_b"][i] = jnp.concatenate([dbg, dbu], axis=1)
        dw_up = _mm_tn(sv["b"], (dug, duu), N_CHIPS, name=f"{tag}_dw_up")
        sent_up = send(i, "up", [dw_up])
        db = _mm_nt((dug, duu), lw["w_up"], out_dtype=F32, name=f"{tag}_d_b")
        dh_mid, dh_mid_b, small["ffn_norm"][i] = _rms_bwd(
            sv["h_mid"], norms["ffn_norm"][i : i + 1], db, dh, name=f"{tag}_ffn_norm"
        )
        do = _mm_nt(dh_mid_b, lw["w_out"], out_dtype=BF16, name=f"{tag}_d_o", after=(_token(dw_up),))
        dw_out = _mm_tn(sv["o"], dh_mid_b, 1, name=f"{tag}_dw_out")
        if kind == 0:
            dq, dk, dv, dblock = _na_bwd(sv["qkv"], sv["bias"], do, name=f"{tag}_na_bwd")
            dbias = _na_unblock(dblock, s // GRID_W, name=f"{tag}_na_unblock")
            dqkv = jnp.concatenate([dq, dk, dv], axis=1)
            small["na_rpb"][j] = _rpb_fold(dbias, name=f"{tag}_rpb_fold")
        elif kind == 1:
            dgb, dgc, dhh, small["sc_conv_w"] = _sc_mid_bwd(sv["qkv"], lw["sc_conv_w"], do, name=f"{tag}_sc_bwd")
            dqkv = jnp.concatenate([dgb, dgc, dhh], axis=1)
        else:
            gq, gk = norms["gqa_q_norm"][j : j + 1], norms["gqa_k_norm"][j : j + 1]
            dqn, dkn, dv = _gqa_bwd(sv["qkn"], sv["qkv"], do, heads, hkv, name=f"{tag}_gqa_bwd")
            dqk, small["gqa_q_norm"], small["gqa_k_norm"] = _gqa_prep_bwd(
                sv["qkv"], gq, gk, cos, sin, dqn, dkn, heads, hkv, name=f"{tag}_gqa_prep_bwd"
            )
            dqkv = jnp.concatenate([dqk, dv], axis=1)
        dw_in = _mm_tn(sv["a"], dqkv, N_CHIPS, name=f"{tag}_dw_in")
        da = _mm_nt(dqkv, lw["w_in"], out_dtype=F32, name=f"{tag}_d_a")
        dh, dh_b, small["mix_norm"][i] = _rms_bwd(
            sv["h_in"], norms["mix_norm"][i : i + 1], da, dh_mid, name=f"{tag}_mix_norm"
        )
        sent_mix = send(i, "mix", [dw_in, dw_out.reshape(N_CHIPS, dw_out.shape[1] // N_CHIPS, d)])
        after = (_token(dw_in), _token(dw_out))
        big[i] = {"mix": sent_mix, "up": sent_up, "down": sent_down}
    return loss, dh, big, small


def _pack(parts):
    flat = jnp.concatenate([p.reshape(-1).astype(F32) for p in parts])
    pad = (-flat.shape[0]) % (8 * LANES)
    return jnp.pad(flat, (0, pad)).reshape(-1, LANES)


def _unpack(buf, shapes):
    flat = buf.reshape(-1)
    out, at = [], 0
    for shp in shapes:
        size = 1
        for n in shp:
            size *= n
        out.append(flat[at : at + size].reshape(shp))
        at += size
    return out


def kernel(x, mix_norm, ffn_norm, final_norm, na_w_qkv, na_rpb, na_w_o, sc_w_in, sc_conv_w, sc_w_out, gqa_w_qkv, gqa_q_norm, gqa_k_norm, gqa_w_o, ffn_w_up, ffn_conv_w, ffn_conv_b, ffn_w_down, loss_target, m_mix_norm, m_ffn_norm, m_final_norm, m_na_w_qkv, m_na_rpb, m_na_w_o, m_sc_w_in, m_sc_conv_w, m_sc_w_out, m_gqa_w_qkv, m_gqa_q_norm, m_gqa_k_norm, m_gqa_w_o, m_ffn_w_up, m_ffn_conv_w, m_ffn_conv_b, m_ffn_w_down, v_mix_norm, v_ffn_norm, v_final_norm, v_na_w_qkv, v_na_rpb, v_na_w_o, v_sc_w_in, v_sc_conv_w, v_sc_w_out, v_gqa_w_qkv, v_gqa_q_norm, v_gqa_k_norm, v_gqa_w_o, v_ffn_w_up, v_ffn_conv_w, v_ffn_conv_b, v_ffn_w_down):
    depth, d = mix_norm.shape
    chip = 2 * lax.axis_index("x") + lax.axis_index("y")
    w_in_of = {0: na_w_qkv, 1: sc_w_in, 2: gqa_w_qkv}
    w_out_of = {0: na_w_o, 1: sc_w_out, 2: gqa_w_o}

    layers = []
    for i in range(depth):
        kind, j = _mixer_of(i)
        shards = [_cast_slab(w_in_of[kind], j, name=f"cast_w_in_l{i}"), _cast_slab(w_out_of[kind], j, name=f"cast_w_out_l{i}")]
        split = [True, True]
        if kind == 1:
            shards.append(sc_conv_w[j])
            split.append(False)
        mix = _gather_shards(shards, split, name=f"gather_mix_l{i}", collective_id=1 + 2 * i)
        ffn = _gather_shards(
            [_cast_slab(ffn_w_up, i, name=f"cast_w_up_l{i}"), _cast_slab(ffn_w_down, i, name=f"cast_w_down_l{i}"),
             ffn_conv_w[i]],
            [True, True, False],
            name=f"gather_ffn_l{i}",
            collective_id=2 + 2 * i,
        )
        lw = {
            "w_in": mix[0],
            "w_out": mix[1].reshape(1, -1, d),
            "w_up": ffn[0],
            "w_down": ffn[1].reshape(1, -1, d),
            "ffn_conv_w": ffn[2],
            "ffn_conv_b": ffn_conv_b[i : i + 1],
        }
        if kind == 1:
            lw["sc_conv_w"] = mix[2]
        layers.append(lw)

    norms = dict(mix_norm=mix_norm, ffn_norm=ffn_norm, final_norm=final_norm, na_rpb=na_rpb,
                 gqa_q_norm=gqa_q_norm, gqa_k_norm=gqa_k_norm)

    parts = ("down", "up", "mix")

    def send(i, part, pieces):
        cid = 1 + 2 * depth + len(parts) * i + parts.index(part)
        return _scatter_pieces(pieces, name=f"scatter_{part}_l{i}", collective_id=cid)

    loss, grad_x, big, small = _forward_backward(x[0], loss_target[0], norms, layers, send)

    mixer_names = {0: ("na_w_qkv", "na_w_o"), 1: ("sc_w_in", "sc_w_out"), 2: ("gqa_w_qkv", "gqa_w_o")}
    state = {
        "na_w_qkv": (na_w_qkv, m_na_w_qkv, v_na_w_qkv), "na_w_o": (na_w_o, m_na_w_o, v_na_w_o),
        "sc_w_in": (sc_w_in, m_sc_w_in, v_sc_w_in), "sc_w_out": (sc_w_out, m_sc_w_out, v_sc_w_out),
        "gqa_w_qkv": (gqa_w_qkv, m_gqa_w_qkv, v_gqa_w_qkv), "gqa_w_o": (gqa_w_o, m_gqa_w_o, v_gqa_w_o),
        "ffn_w_up": (ffn_w_up, m_ffn_w_up, v_ffn_w_up), "ffn_w_down": (ffn_w_down, m_ffn_w_down, v_ffn_w_down),
    }
    res = {n: None for n in state}
    token = None
    for i in reversed(range(depth)):
        kind, j = _mixer_of(i)
        for part, names, slab in (("down", ("ffn_w_down",), i), ("up", ("ffn_w_up",), i), ("mix", mixer_names[kind], j)):
            own, sib = big[i][part]
            for slot, n in enumerate(names):
                res[n] = _adamw_slab(*state[n], own[slot], sib[slot], slab, res[n], token, name=f"adamw_{n}_l{i}")
                token = res[n][0][slab, :16, :LANES]

    n_na = na_rpb.shape[0]
    rpb_flat = jnp.stack([small["na_rpb"][j] for j in range(n_na)])
    full_parts = [
        loss[:, :1],
        jnp.concatenate(small["mix_norm"], axis=0),
        jnp.concatenate(small["ffn_norm"], axis=0),
        small["final_norm"],
        rpb_flat,
        small["sc_conv_w"],
        small["gqa_q_norm"],
        small["gqa_k_norm"],
        jnp.stack(small["ffn_conv_w"]),
        jnp.concatenate(small["ffn_conv_b"], axis=0),
    ]
    summed = _unpack(_allreduce_small(_pack(full_parts), name="allreduce_small"), [p.shape for p in full_parts])
    loss_all, g_mix, g_ffn, g_final, g_rpb, g_sc_cw, g_gq, g_gk, g_ffn_cw, g_ffn_cb = summed
    g_rpb = g_rpb[:, :, : RPB_ROWS * RPB_COLS].reshape(na_rpb.shape)
    g_sc_cw = lax.dynamic_slice_in_dim(g_sc_cw, chip * sc_conv_w.shape[2], sc_conv_w.shape[2], axis=1)[None]
    g_ffn_cw = lax.dynamic_slice_in_dim(g_ffn_cw, chip * ffn_conv_w.shape[2], ffn_conv_w.shape[2], axis=2)
    small_names = ["mix_norm", "ffn_norm", "final_norm", "na_rpb", "sc_conv_w", "gqa_q_norm", "gqa_k_norm",
                   "ffn_conv_w", "ffn_conv_b"]
    small_g = [g_mix, g_ffn, g_final.reshape(final_norm.shape), g_rpb, g_sc_cw, g_gq, g_gk, g_ffn_cw, g_ffn_cb]
    small_w = [mix_norm, ffn_norm, final_norm, na_rpb, sc_conv_w, gqa_q_norm, gqa_k_norm, ffn_conv_w, ffn_conv_b]
    small_m = [m_mix_norm, m_ffn_norm, m_final_norm, m_na_rpb, m_sc_conv_w, m_gqa_q_norm, m_gqa_k_norm,
               m_ffn_conv_w, m_ffn_conv_b]
    small_v = [v_mix_norm, v_ffn_norm, v_final_norm, v_na_rpb, v_sc_conv_w, v_gqa_q_norm, v_gqa_k_norm,
               v_ffn_conv_w, v_ffn_conv_b]
    shapes = [w.shape for w in small_w]
    packed = _adamw_small(_pack(small_w), _pack(small_g), _pack(small_m), _pack(small_v), name="adamw_small")
    small_d, small_nm, small_nv = (_unpack(p, shapes) for p in packed)
    for n, g, dl, nm, nv in zip(small_names, small_g, small_d, small_nm, small_nv):
        res[n] = (g.reshape(dl.shape), dl, nm, nv)

    order = ["mix_norm", "ffn_norm", "final_norm", "na_w_qkv", "na_rpb", "na_w_o", "sc_w_in", "sc_conv_w",
             "sc_w_out", "gqa_w_qkv", "gqa_q_norm", "gqa_k_norm", "gqa_w_o", "ffn_w_up", "ffn_conv_w",
             "ffn_conv_b", "ffn_w_down"]
    outs = [loss_all.reshape(()), grad_x[None]]
    for part in range(4):
        outs.extend(res[n][part] for n in order)
    return tuple(outs)
```
